```python
import math
import jax, jax.numpy as jnp
from jax import lax
import numpy as np

D_MODEL = 1024
BATCH = 8
SEQ = 8192
DEPTH = 4

GRID_W = 64
CTX_LEN = 256
N_BRANCH = 3
BRANCH_WIDTH = 512
HY_WIDTH = BRANCH_WIDTH
HY_EMB_DIM = 33
HY_BANDS = (HY_EMB_DIM - 1) // 2
HY_FILTER_WIDTH = 64
HY_DECAY_TARGET = 1e-2
HY_FAST_DECAY = 0.3
HY_SLOW_DECAY = 1.5
GQA_HEADS = 8
GQA_KV_HEADS = 2
GQA_GROUP = GQA_HEADS // GQA_KV_HEADS
GQA_HEAD_DIM = BRANCH_WIDTH // GQA_HEADS
GQA_SCALE = GQA_HEAD_DIM ** -0.5
WINDOW = 128
BLOCK = 128
MLA_HEADS = 8
MLA_Q_RANK = 384
MLA_KV_RANK = 256
MLA_NOPE_DIM = 64
MLA_ROPE_DIM = 32
MLA_V_DIM = BRANCH_WIDTH // MLA_HEADS
MLA_QK_DIM = MLA_NOPE_DIM + MLA_ROPE_DIM
MLA_SCALE = MLA_QK_DIM ** -0.5
D_FF = 4 * D_MODEL
ROPE_BASE = 10000.0
EPS = 1e-6
IN_SIZES = (3 * HY_WIDTH, GQA_HEADS * GQA_HEAD_DIM, GQA_KV_HEADS * GQA_HEAD_DIM, GQA_KV_HEADS * GQA_HEAD_DIM, MLA_Q_RANK, MLA_KV_RANK, MLA_ROPE_DIM, N_BRANCH * D_MODEL)
N_IN = 3 * HY_WIDTH + GQA_HEADS * GQA_HEAD_DIM + 2 * GQA_KV_HEADS * GQA_HEAD_DIM + MLA_Q_RANK + MLA_KV_RANK + MLA_ROPE_DIM + N_BRANCH * D_MODEL

kernel_name = 'hybrid_hyena_swa_mla_prefix_dit'


def rms_norm(x, g):
    xf = x.astype(jnp.float32)
    y = xf * lax.rsqrt(jnp.mean(xf * xf, axis=-1, keepdims=True) + EPS)
    return (y * g.astype(jnp.float32)).astype(x.dtype)


def modulate(x, shift, scale):
    return x * (1 + scale) + shift


def split_cols(p, sizes):
    offs, acc = [], 0
    for s in sizes[:-1]:
        acc += s
        offs.append(acc)
    return jnp.split(p, offs, axis=-1)


def axial_rope(rows, dim, dtype):
    n_freq = dim // 4
    inv = ROPE_BASE ** (-jnp.arange(n_freq, dtype=jnp.float32) / n_freq)
    r = jnp.repeat(jnp.arange(rows, dtype=jnp.float32), GRID_W)
    col = jnp.tile(jnp.arange(GRID_W, dtype=jnp.float32), rows)
    ang = jnp.concatenate([r[:, None] * inv, col[:, None] * inv], axis=-1)
    return jnp.cos(ang).astype(dtype), jnp.sin(ang).astype(dtype)


def apply_rope(x, cos, sin):
    x1, x2 = jnp.split(x, 2, axis=-1)
    cs, sn = cos[:, None, :], sin[:, None, :]
    return jnp.concatenate([x1 * cs - x2 * sn, x1 * sn + x2 * cs], axis=-1)


def hyena_filter(length, f1_w, f1_b, f2_w, f2_b, sin_freq, f3_w):
    f32 = jnp.float32
    t = jnp.linspace(0.0, 1.0, length, dtype=f32)[:, None]
    w = 2.0 * math.pi * jnp.arange(length, dtype=f32)[:, None] / length
    f = jnp.linspace(1e-4, HY_BANDS - 1, HY_BANDS, dtype=f32)[None, :]
    z = jnp.concatenate([t, jnp.cos(f * w), -jnp.sin(f * w)], axis=-1)
    h = jnp.sin(sin_freq[0].astype(f32) * (z @ f1_w.astype(f32) + f1_b.astype(f32)))
    h = jnp.sin(sin_freq[1].astype(f32) * (h @ f2_w.astype(f32) + f2_b.astype(f32)))
    h = (h @ f3_w.astype(f32)).reshape(length, 2, HY_WIDTH)
    max_decay = math.log(HY_DECAY_TARGET) / HY_FAST_DECAY
    min_decay = math.log(HY_DECAY_TARGET) / HY_SLOW_DECAY
    deltas = jnp.abs(jnp.linspace(min_decay, max_decay, HY_WIDTH, dtype=f32))
    h = h * jnp.exp(-t * deltas)[:, None, :]
    k = jnp.concatenate([h[:, 0], jnp.zeros((1, HY_WIDTH), f32), h[:0:-1, 1]], axis=0)
    return k * lax.rsqrt(jnp.sum(k * k, axis=0, keepdims=True) + EPS)


def hyena_mix(u, short_w, kernel, skip):
    L = u.shape[1]
    up = jnp.pad(u, ((0, 0), (1, 1), (0, 0)))
    uc = up[:, :-2] * short_w[0] + up[:, 1:-1] * short_w[1] + up[:, 2:] * short_w[2]
    x0, x1, v = jnp.split(uc, 3, axis=-1)
    z = (x1 * v).astype(jnp.float32)
    zf = jnp.fft.rfft(z, n=2 * L, axis=1)
    kf = jnp.fft.rfft(kernel, axis=0)
    y = jnp.fft.irfft(zf * kf[None], n=2 * L, axis=1)[:, :L] + z * skip.astype(jnp.float32)
    return (x0.astype(jnp.float32) * y).astype(u.dtype)


def joint_attention(q, keys, vals, masks, sink, scale):
    logits = []
    for kk, mm in zip(keys, masks):
        s = jnp.einsum('bqhgd,bkhd->bhgqk', q, kk, preferred_element_type=jnp.float32) * scale
        if mm is not None:
            s = jnp.where(mm, s, -1e30)
        logits.append(s)
    if sink is not None:
        logits.append(jnp.broadcast_to(sink.astype(jnp.float32)[None, :, :, None, None], logits[0].shape[:-1] + (1,)))
    p = jax.nn.softmax(jnp.concatenate(logits, axis=-1), axis=-1)
    out, start = 0, 0
    for vv in vals:
        n = vv.shape[1]
        out = out + jnp.einsum('bhgqk,bkhd->bqhgd', p[..., start:start + n].astype(vv.dtype), vv)
        start += n
    return out


def window_gqa_latent(q, k, v, k_ctx, v_ctx, sink):
    B, L = q.shape[0], q.shape[1]
    nb = L // BLOCK
    span = BLOCK + 2 * WINDOW
    kp = jnp.pad(k, ((0, 0), (WINDOW, WINDOW), (0, 0), (0, 0)))
    vp = jnp.pad(v, ((0, 0), (WINDOW, WINDOW), (0, 0), (0, 0)))
    qi = jnp.arange(BLOCK)[:, None]
    kj = jnp.arange(span)[None, :]
    band = (kj >= qi) & (kj <= qi + 2 * WINDOW)

    def one_block(b):
        start = b * BLOCK
        qb = lax.dynamic_slice_in_dim(q, start, BLOCK, axis=1)
        kb = lax.dynamic_slice_in_dim(kp, start, span, axis=1)
        vb = lax.dynamic_slice_in_dim(vp, start, span, axis=1)
        jpos = start - WINDOW + kj
        mask = band & (jpos >= 0) & (jpos < L)
        return joint_attention(qb, [kb, k_ctx], [vb, v_ctx], [mask, None], sink, GQA_SCALE)

    out = lax.map(one_block, jnp.arange(nb))
    return jnp.moveaxis(out, 0, 1).reshape(B, L, GQA_HEADS * GQA_HEAD_DIM)


def mla_latent(q, k, v, k_ctx, v_ctx):
    B, L = q.shape[0], q.shape[1]
    nb = L // BLOCK

    def one_block(b):
        qb = lax.dynamic_slice_in_dim(q, b * BLOCK, BLOCK, axis=1)
        return joint_attention(qb, [k, k_ctx], [v, v_ctx], [None, None], None, MLA_SCALE)

    out = lax.map(one_block, jnp.arange(nb))
    return jnp.moveaxis(out, 0, 1).reshape(B, L, MLA_HEADS * MLA_V_DIM)


def project_stream(h, w_in, gqa_qn, gqa_kn, mla_qa_n, mla_kva_n, w_q_b, w_kv_b, mla_qn, mla_kn, rope_gqa, rope_mla):
    B, L, _ = h.shape
    u, q, k, v, cq, ckv, kr, g = split_cols(h @ w_in, IN_SIZES)
    q = rms_norm(q.reshape(B, L, GQA_HEADS, GQA_HEAD_DIM), gqa_qn)
    k = rms_norm(k.reshape(B, L, GQA_KV_HEADS, GQA_HEAD_DIM), gqa_kn)
    v = v.reshape(B, L, GQA_KV_HEADS, GQA_HEAD_DIM)
    qm = (rms_norm(cq, mla_qa_n) @ w_q_b).reshape(B, L, MLA_HEADS, MLA_QK_DIM)
    kvm = (rms_norm(ckv, mla_kva_n) @ w_kv_b).reshape(B, L, MLA_HEADS, MLA_NOPE_DIM + MLA_V_DIM)
    k_nope, vm = jnp.split(kvm, [MLA_NOPE_DIM], axis=-1)
    km = jnp.concatenate([k_nope, jnp.broadcast_to(kr[:, :, None, :], (B, L, MLA_HEADS, MLA_ROPE_DIM))], axis=-1)
    qm = rms_norm(qm, mla_qn)
    km = rms_norm(km, mla_kn)
    if rope_gqa is not None:
        q = apply_rope(q, *rope_gqa)
        k = apply_rope(k, *rope_gqa)
        qm = jnp.concatenate([qm[..., :MLA_NOPE_DIM], apply_rope(qm[..., MLA_NOPE_DIM:], *rope_mla)], axis=-1)
        km = jnp.concatenate([km[..., :MLA_NOPE_DIM], apply_rope(km[..., MLA_NOPE_DIM:], *rope_mla)], axis=-1)
    gates = jax.nn.sigmoid(g.astype(jnp.float32)).astype(h.dtype).reshape(B, L, N_BRANCH, D_MODEL)
    q = q.reshape(B, L, GQA_KV_HEADS, GQA_GROUP, GQA_HEAD_DIM)
    qm = qm[:, :, :, None, :]
    return u, q, k, v, qm, km, vm, gates


def merge_branches(y_hy, y_gqa, y_mla, gates, w_branch, w_out):
    ys = (y_hy, y_gqa, y_mla)
    merged = sum(gates[:, :, n] * (ys[n] @ w_branch[n]) for n in range(N_BRANCH))
    return merged @ w_out


def squared_relu_mlp(h, w1, w2):
    return jnp.square(jax.nn.relu(h @ w1)) @ w2


def setup_inputs(seed: int = 0) -> dict:
    key = jax.random.key(seed)
    ks = iter(jax.random.split(key, 32))
    D = D_MODEL

    def nrm(shape, scale):
        return jax.random.normal(next(ks), shape, jnp.float32) * scale

    return {
        'x': nrm((BATCH, SEQ, D), 1.0),
        'c': nrm((BATCH, D), 1.0),
        'ctx': nrm((BATCH, CTX_LEN, D), 1.0),
        'c_ctx': nrm((D,), 1.0),
        'w_mod': nrm((DEPTH, D, 6 * D), 0.5 * D ** -0.5),
        'b_mod': nrm((DEPTH, 6 * D), 0.02),
        'norm_mix_g': 1.0 + nrm((DEPTH, D), 0.02),
        'norm_mlp_g': 1.0 + nrm((DEPTH, D), 0.02),
        'w_in': nrm((DEPTH, D, N_IN), D ** -0.5),
        'hy_short_w': nrm((DEPTH, 3, 3 * HY_WIDTH), 3 ** -0.5),
        'hy_f1_w': nrm((DEPTH, HY_EMB_DIM, HY_FILTER_WIDTH), HY_EMB_DIM ** -0.5),
        'hy_f1_b': nrm((DEPTH, HY_FILTER_WIDTH), 0.1),
        'hy_f2_w': nrm((DEPTH, HY_FILTER_WIDTH, HY_FILTER_WIDTH), HY_FILTER_WIDTH ** -0.5),
        'hy_f2_b': nrm((DEPTH, HY_FILTER_WIDTH), 0.1),
        'hy_sin_freq': 1.0 + nrm((DEPTH, 2, HY_FILTER_WIDTH), 0.1),
        'hy_f3_w': nrm((DEPTH, HY_FILTER_WIDTH, 2 * HY_WIDTH), HY_FILTER_WIDTH ** -0.5),
        'hy_skip': nrm((DEPTH, HY_WIDTH), 0.5),
        'gqa_q_norm': 1.0 + nrm((DEPTH, GQA_HEAD_DIM), 0.02),
        'gqa_k_norm': 1.0 + nrm((DEPTH, GQA_HEAD_DIM), 0.02),
        'gqa_sink': nrm((DEPTH, GQA_HEADS), 0.5),
        'mla_q_a_norm': 1.0 + nrm((DEPTH, MLA_Q_RANK), 0.02),
        'mla_kv_a_norm': 1.0 + nrm((DEPTH, MLA_KV_RANK), 0.02),
        'w_q_b': nrm((DEPTH, MLA_Q_RANK, MLA_HEADS * MLA_QK_DIM), MLA_Q_RANK ** -0.5),
        'w_kv_b': nrm((DEPTH, MLA_KV_RANK, MLA_HEADS * (MLA_NOPE_DIM + MLA_V_DIM)), MLA_KV_RANK ** -0.5),
        'mla_q_norm': 1.0 + nrm((DEPTH, MLA_QK_DIM), 0.02),
        'mla_k_norm': 1.0 + nrm((DEPTH, MLA_QK_DIM), 0.02),
        'w_branch': nrm((DEPTH, N_BRANCH, BRANCH_WIDTH, D), BRANCH_WIDTH ** -0.5),
        'w_out': nrm((DEPTH, D, D), D ** -0.5),
        'w_mlp1': nrm((DEPTH, D, D_FF), D ** -0.5),
        'w_mlp2': nrm((DEPTH, D_FF, D), D_FF ** -0.5),
    }


def reference(x, c, ctx, c_ctx, w_mod, b_mod, norm_mix_g, norm_mlp_g, w_in, hy_short_w, hy_f1_w, hy_f1_b, hy_f2_w, hy_f2_b, hy_sin_freq, hy_f3_w, hy_skip, gqa_q_norm, gqa_k_norm, gqa_sink, mla_q_a_norm, mla_kv_a_norm, w_q_b, w_kv_b, mla_q_norm, mla_k_norm, w_branch, w_out, w_mlp1, w_mlp2):
    B, L, _ = x.shape
    Lc = ctx.shape[1]
    rows = L // GRID_W
    rope_gqa = axial_rope(rows, GQA_HEAD_DIM, x.dtype)
    rope_mla = axial_rope(rows, MLA_ROPE_DIM, x.dtype)
    s_lat = jax.nn.silu(c)[:, None, :]
    s_ctx = jax.nn.silu(c_ctx)[None, None, :]
    for l in range(DEPTH):
        mod = jnp.split(s_lat @ w_mod[l] + b_mod[l], 6, axis=-1)
        mod_c = jnp.split(s_ctx @ w_mod[l] + b_mod[l], 6, axis=-1)
        proj_w = (w_in[l], gqa_q_norm[l], gqa_k_norm[l], mla_q_a_norm[l], mla_kv_a_norm[l], w_q_b[l], w_kv_b[l], mla_q_norm[l], mla_k_norm[l])
        filt_w = (hy_f1_w[l], hy_f1_b[l], hy_f2_w[l], hy_f2_b[l], hy_sin_freq[l], hy_f3_w[l])
        sink = gqa_sink[l].reshape(GQA_KV_HEADS, GQA_GROUP)

        hc = modulate(rms_norm(ctx, norm_mix_g[l]), mod_c[0], mod_c[1])
        u_c, q_c, k_c, v_c, qm_c, km_c, vm_c, gt_c = project_stream(hc, *proj_w, None, None)
        h = modulate(rms_norm(x, norm_mix_g[l]), mod[0], mod[1])
        u, q, k, v, qm, km, vm, gt = project_stream(h, *proj_w, rope_gqa, rope_mla)

        y_hy = hyena_mix(u, hy_short_w[l], hyena_filter(L, *filt_w), hy_skip[l])
        y_gqa = window_gqa_latent(q, k, v, k_c, v_c, sink)
        y_mla = mla_latent(qm, km, vm, km_c, vm_c)
        x = x + mod[2] * merge_branches(y_hy, y_gqa, y_mla, gt, w_branch[l], w_out[l])
        x = x + mod[5] * squared_relu_mlp(modulate(rms_norm(x, norm_mlp_g[l]), mod[3], mod[4]), w_mlp1[l], w_mlp2[l])

        if l < DEPTH - 1:
            yc_hy = hyena_mix(u_c, hy_short_w[l], hyena_filter(Lc, *filt_w), hy_skip[l])
            yc_gqa = joint_attention(q_c, [k_c], [v_c], [None], sink, GQA_SCALE).reshape(B, Lc, GQA_HEADS * GQA_HEAD_DIM)
            yc_mla = joint_attention(qm_c, [km_c], [vm_c], [None], None, MLA_SCALE).reshape(B, Lc, MLA_HEADS * MLA_V_DIM)
            ctx = ctx + mod_c[2] * merge_branches(yc_hy, yc_gqa, yc_mla, gt_c, w_branch[l], w_out[l])
            ctx = ctx + mod_c[5] * squared_relu_mlp(modulate(rms_norm(ctx, norm_mlp_g[l]), mod_c[3], mod_c[4]), w_mlp1[l], w_mlp2[l])
    return x
```

```python
import functools
import math

import numpy as np
import jax
import jax.numpy as jnp
from jax import lax
from jax.experimental import pallas as pl
from jax.experimental.pallas import tpu as pltpu

D_MODEL = 1024
GRID_W = 64
HY_WIDTH = 512
HY_EMB_DIM = 33
HY_BANDS = (HY_EMB_DIM - 1) // 2
HY_FILTER_WIDTH = 64
HY_DECAY_TARGET = 1e-2
HY_FAST_DECAY = 0.3
HY_SLOW_DECAY = 1.5
GQA_HEADS = 8
GQA_KV_HEADS = 2
GQA_GROUP = GQA_HEADS // GQA_KV_HEADS
GQA_HEAD_DIM = 64
GQA_SCALE = GQA_HEAD_DIM ** -0.5
WINDOW = 128
MLA_HEADS = 8
MLA_Q_RANK = 384
MLA_KV_RANK = 256
MLA_NOPE_DIM = 64
MLA_ROPE_DIM = 32
MLA_V_DIM = 64
MLA_QK_DIM = MLA_NOPE_DIM + MLA_ROPE_DIM
MLA_SCALE = MLA_QK_DIM ** -0.5
D_FF = 4 * D_MODEL
ROPE_BASE = 10000.0
EPS = 1e-6
LOG2E = 1.4426950408889634
NEG_BIG = -1e30

V7X_LANES = 128
V7X_VMEM_BYTES = 64 * 1024 * 1024

ROW_TILE = 256
MLA_HEAD_PAD = 128
KV_CHUNK = 3 * ROW_TILE
KV_SUB = 256
MLA_TQ = 512
GQA_TQ = 256
FF_CHUNK = 1024
FFT_N2 = 128
FFT_NB = 16

F32 = jnp.float32
BF16 = jnp.bfloat16


def _cparams(sem, vmem_mb):
    return pltpu.CompilerParams(dimension_semantics=sem, vmem_limit_bytes=vmem_mb * 1024 * 1024)


def _dot(a, b):
    return jnp.dot(a, b, preferred_element_type=F32)


def _dot_hi(a, b):
    return jnp.dot(a, b, preferred_element_type=F32, precision=lax.Precision.HIGHEST)


def _const_spec(shape):
    nd = len(shape)
    return pl.BlockSpec(shape, lambda *_: (0,) * nd)


def _mod_kernel(c_ref, w_ref, b_ref, o_ref):
    c = c_ref[...]
    s = c * jax.nn.sigmoid(c)
    o_ref[...] = _dot_hi(s, w_ref[...]) + b_ref[...]


def _mod_all(cond, w_mod, b_mod):
    depth, d, n = w_mod.shape
    rows = cond.shape[0]
    tn = 1536
    return pl.pallas_call(
        _mod_kernel,
        grid=(depth, n // tn),
        in_specs=[
            pl.BlockSpec((rows, d), lambda l, j: (0, 0)),
            pl.BlockSpec((None, d, tn), lambda l, j: (l, 0, j)),
            pl.BlockSpec((None, 1, tn), lambda l, j: (l, 0, j)),
        ],
        out_specs=pl.BlockSpec((None, rows, tn), lambda l, j: (l, 0, j)),
        out_shape=jax.ShapeDtypeStruct((depth, rows, n), F32),
        compiler_params=_cparams(("arbitrary", "arbitrary"), 40),
        name="adaln_mod",
    )(cond, w_mod, b_mod.reshape(depth, 1, n))


def _rms_rows(x, n):
    return lax.rsqrt(jnp.sum(x * x, axis=0, keepdims=True) * (1.0 / n) + EPS)


def _rope_rows(x1, x2, cs, sn):
    return x1 * cs - x2 * sn, x1 * sn + x2 * cs


def _proj_kernel(x_ref, mod_ref, g_ref, wu_ref, wg_ref, wt_ref, wqb_ref, wkvb_ref,
                 gq_ref, gk_ref, gqa_ref, gkva_ref, gmq_ref, gmk_ref,
                 cg_ref, sg_ref, cm_ref, sm_ref,
                 u_ref, gate_ref, qg_ref, kg_ref, vg_ref, qm_ref, km_ref, vm_ref):
    d = D_MODEL
    x = x_ref[...]
    tm = x.shape[0]
    shift = mod_ref[:, 0:d]
    scale = mod_ref[:, d:2 * d]
    xn = x * lax.rsqrt(jnp.mean(x * x, axis=-1, keepdims=True) + EPS) * g_ref[...]
    h = xn * (1.0 + scale) + shift
    hb = h.astype(BF16)

    u_ref[...] = _dot(hb, wu_ref[...]).astype(BF16)
    gate_ref[...] = jax.nn.sigmoid(_dot(hb, wg_ref[...])).astype(BF16)

    ht = h.T.astype(BF16)
    t = _dot(wt_ref[...], ht)
    o_q, o_k, o_v = 0, 512, 640
    o_cq, o_ckv, o_kr = 768, 768 + MLA_Q_RANK, 768 + MLA_Q_RANK + MLA_KV_RANK

    cg, sg = cg_ref[...], sg_ref[...]
    cm, sm = cm_ref[...], sm_ref[...]
    hd, hh = GQA_HEAD_DIM, GQA_HEAD_DIM // 2

    gq = gq_ref[...]
    for n in range(GQA_HEADS):
        xh = t[o_q + n * hd:o_q + (n + 1) * hd]
        xh = xh * _rms_rows(xh, hd) * gq
        a, b = _rope_rows(xh[:hh], xh[hh:], cg, sg)
        qg_ref[n * hd:n * hd + hh, :] = (a * (GQA_SCALE * LOG2E)).astype(BF16)
        qg_ref[n * hd + hh:(n + 1) * hd, :] = (b * (GQA_SCALE * LOG2E)).astype(BF16)

    gk = gk_ref[...]
    zpad = jnp.zeros((V7X_LANES - hd, tm), F32)
    parts = []
    for n in range(GQA_KV_HEADS):
        xh = t[o_k + n * hd:o_k + (n + 1) * hd]
        xh = xh * _rms_rows(xh, hd) * gk
        a, b = _rope_rows(xh[:hh], xh[hh:], cg, sg)
        parts += [a, b, zpad]
    kg_ref[...] = jnp.concatenate(parts, axis=0).T.astype(BF16)
    vg_ref[...] = t[o_v:o_v + GQA_KV_HEADS * hd].astype(BF16)

    cq = t[o_cq:o_cq + MLA_Q_RANK]
    cqn = (cq * _rms_rows(cq, MLA_Q_RANK) * gqa_ref[...]).astype(BF16)
    qm = _dot(wqb_ref[...], cqn)
    gmq = gmq_ref[...]
    nd, rh = MLA_NOPE_DIM, MLA_ROPE_DIM // 2
    qpad = jnp.zeros((MLA_HEAD_PAD - MLA_QK_DIM, tm), BF16)
    for n in range(MLA_HEADS):
        xh = qm[n * MLA_QK_DIM:(n + 1) * MLA_QK_DIM]
        xh = xh * _rms_rows(xh, MLA_QK_DIM) * gmq * (MLA_SCALE * LOG2E)
        a, b = _rope_rows(xh[nd:nd + rh], xh[nd + rh:], cm, sm)
        base = n * MLA_HEAD_PAD
        qm_ref[base:base + nd, :] = xh[:nd].astype(BF16)
        qm_ref[base + nd:base + nd + rh, :] = a.astype(BF16)
        qm_ref[base + nd + rh:base + MLA_QK_DIM, :] = b.astype(BF16)
        qm_ref[base + MLA_QK_DIM:base + MLA_HEAD_PAD, :] = qpad

    ckv = t[o_ckv:o_ckv + MLA_KV_RANK]
    ckvn = (ckv * _rms_rows(ckv, MLA_KV_RANK) * gkva_ref[...]).astype(BF16)
    kv = _dot(wkvb_ref[...], ckvn)
    kr = t[o_kr:o_kr + MLA_ROPE_DIM]
    kr_ss = jnp.sum(kr * kr, axis=0, keepdims=True)
    gmk = gmk_ref[...]
    kzero = jnp.zeros((MLA_HEAD_PAD - MLA_QK_DIM, tm), F32)
    parts = []
    for n in range(MLA_HEADS):
        kn = kv[n * 128:n * 128 + nd]
        vm_ref[n * MLA_V_DIM:(n + 1) * MLA_V_DIM, :] = kv[n * 128 + nd:(n + 1) * 128].astype(BF16)
        rs = lax.rsqrt((jnp.sum(kn * kn, axis=0, keepdims=True) + kr_ss) * (1.0 / MLA_QK_DIM) + EPS)
        krn = kr * rs * gmk[nd:]
        a, b = _rope_rows(krn[:rh], krn[rh:], cm, sm)
        parts += [kn * rs * gmk[:nd], a, b, kzero]
    km_ref[...] = jnp.concatenate(parts, axis=0).T.astype(BF16)


def _proj(xs, modsel, g, wts, tabs, lat_tiles):
    bsz, s, d = xs.shape
    tm = ROW_TILE
    nt = s // tm
    ncs = KV_CHUNK // tm
    (wu, wg, wt, wqb, wkvb, gq, gk, gqa, gkva, gmq, gmk) = wts
    cg, sg, cm, sm = tabs
    row = lambda b, i: (b, i, 0)
    col = lambda b, i: (b, 0, i)
    tab = lambda b, i: (0, i)
    in_specs = [
        pl.BlockSpec((None, tm, d), row),
        pl.BlockSpec((None, None, 1, 6 * d), lambda b, i: (b, jnp.where(i >= lat_tiles, 1, 0), 0, 0)),
        _const_spec(g.shape), _const_spec(wu.shape), _const_spec(wg.shape), _const_spec(wt.shape),
        _const_spec(wqb.shape), _const_spec(wkvb.shape),
        _const_spec(gq.shape), _const_spec(gk.shape), _const_spec(gqa.shape), _const_spec(gkva.shape),
        _const_spec(gmq.shape), _const_spec(gmk.shape),
        pl.BlockSpec((cg.shape[0], tm), tab), pl.BlockSpec((sg.shape[0], tm), tab),
        pl.BlockSpec((cm.shape[0], tm), tab), pl.BlockSpec((sm.shape[0], tm), tab),
    ]
    out_shape = [
        jax.ShapeDtypeStruct((bsz, s, 3 * HY_WIDTH), BF16),
        jax.ShapeDtypeStruct((bsz, s, 3 * d), BF16),
        jax.ShapeDtypeStruct((bsz, GQA_HEADS * GQA_HEAD_DIM, s), BF16),
        jax.ShapeDtypeStruct((bsz, s, GQA_KV_HEADS * V7X_LANES), BF16),
        jax.ShapeDtypeStruct((bsz, GQA_KV_HEADS * GQA_HEAD_DIM, s), BF16),
        jax.ShapeDtypeStruct((bsz, MLA_HEADS * MLA_HEAD_PAD, s), BF16),
        jax.ShapeDtypeStruct((bsz, s, MLA_HEADS * MLA_HEAD_PAD), BF16),
        jax.ShapeDtypeStruct((bsz, s // KV_CHUNK, MLA_HEADS * MLA_V_DIM, KV_CHUNK), BF16),
    ]
    out_specs = [
        pl.BlockSpec((None, tm, 3 * HY_WIDTH), row),
        pl.BlockSpec((None, tm, 3 * d), row),
        pl.BlockSpec((None, GQA_HEADS * GQA_HEAD_DIM, tm), col),
        pl.BlockSpec((None, tm, GQA_KV_HEADS * V7X_LANES), row),
        pl.BlockSpec((None, GQA_KV_HEADS * GQA_HEAD_DIM, tm), col),
        pl.BlockSpec((None, MLA_HEADS * MLA_HEAD_PAD, tm), col),
        pl.BlockSpec((None, tm, MLA_HEADS * MLA_HEAD_PAD), row),
        pl.BlockSpec((None, None, MLA_HEADS * MLA_V_DIM, tm), lambda b, i: (b, i // ncs, 0, i % ncs)),
    ]
    return pl.pallas_call(
        _proj_kernel,
        grid=(bsz, nt),
        in_specs=in_specs,
        out_specs=out_specs,
        out_shape=out_shape,
        compiler_params=_cparams(("parallel", "arbitrary"), 56),
        name="in_proj",
    )(xs, modsel, g, wu, wg, wt, wqb, wkvb, gq, gk, gqa, gkva, gmq, gmk, cg, sg, cm, sm)


def _short_conv_kernel(u_ref, up_ref, un_ref, w_ref, x0_ref, z_ref, *, n_tiles):
    i = pl.program_id(1)
    u = u_ref[...].astype(F32)
    tm = u.shape[0]
    prev = jnp.where(i > 0, up_ref[7:8, :].astype(F32), 0.0)
    nxt = jnp.where(i < n_tiles - 1, un_ref[0:1, :].astype(F32), 0.0)
    ridx = lax.broadcasted_iota(jnp.int32, u.shape, 0)
    up = jnp.where(ridx == 0, prev, pltpu.roll(u, 1, axis=0))
    dn = jnp.where(ridx == tm - 1, nxt, pltpu.roll(u, tm - 1, axis=0))
    uc = up * w_ref[0:1, :] + u * w_ref[1:2, :] + dn * w_ref[2:3, :]
    c = HY_WIDTH
    x0_ref[...] = uc[:, :c].astype(BF16)
    z_ref[...] = (uc[:, c:2 * c] * uc[:, 2 * c:]).astype(BF16)


def _short_conv(u, short_w, row0, rows):
    bsz, s, c3 = u.shape
    tm = ROW_TILE
    nt = rows // tm
    t0 = row0 // tm
    r8 = tm // 8
    last8 = s // 8 - 1
    return pl.pallas_call(
        functools.partial(_short_conv_kernel, n_tiles=nt),
        grid=(bsz, nt),
        in_specs=[
            pl.BlockSpec((None, tm, c3), lambda b, i: (b, t0 + i, 0)),
            pl.BlockSpec((None, 8, c3), lambda b, i: (b, jnp.maximum((t0 + i) * r8 - 1, 0), 0)),
            pl.BlockSpec((None, 8, c3), lambda b, i: (b, jnp.minimum((t0 + i + 1) * r8, last8), 0)),
            _const_spec(short_w.shape),
        ],
        out_specs=[pl.BlockSpec((None, tm, HY_WIDTH), lambda b, i: (b, i, 0))] * 2,
        out_shape=[jax.ShapeDtypeStruct((bsz, rows, HY_WIDTH), BF16)] * 2,
        compiler_params=_cparams(("parallel", "arbitrary"), 32),
        name="hyena_short_conv",
    )(u, u, u, short_w)


def _filter_kernel(zf_ref, w1_ref, b1_ref, w2_ref, b2_ref, fr_ref, w3_ref, dl_ref, h_ref, ss_ref):
    i = pl.program_id(0)
    zf = zf_ref[...]
    tl = zf.shape[0]
    h = jnp.sin(fr_ref[0:1, :] * (_dot_hi(zf, w1_ref[...]) + b1_ref[...]))
    h = jnp.sin(fr_ref[1:2, :] * (_dot_hi(h, w2_ref[...]) + b2_ref[...]))
    h = _dot_hi(h, w3_ref[...])
    decay = jnp.exp(-zf[:, 0:1] * dl_ref[...])
    c = HY_WIDTH
    hf = h[:, :c] * decay
    ridx = lax.broadcasted_iota(jnp.int32, (tl, c), 0) + i * tl
    hb = jnp.where(ridx == 0, 0.0, h[:, c:] * decay)
    h_ref[0] = hf.astype(BF16)
    h_ref[1] = hb.astype(BF16)
    ss = jnp.sum(hf * hf + hb * hb, axis=0, keepdims=True)

    @pl.when(i == 0)
    def _():
        ss_ref[...] = ss

    @pl.when(i > 0)
    def _():
        ss_ref[...] += ss


def _filter(zfeat, fw, deltas):
    length = zfeat.shape[0]
    tl = min(length, 1024)
    w1, b1, w2, b2, fr, w3 = fw
    return pl.pallas_call(
        _filter_kernel,
        grid=(length // tl,),
        in_specs=[pl.BlockSpec((tl, zfeat.shape[1]), lambda i: (i, 0))]
        + [_const_spec(a.shape) for a in (w1, b1, w2, b2, fr, w3, deltas)],
        out_specs=[pl.BlockSpec((2, tl, HY_WIDTH), lambda i: (0, i, 0)),
                   pl.BlockSpec((1, HY_WIDTH), lambda i: (0, 0))],
        out_shape=[jax.ShapeDtypeStruct((2, length, HY_WIDTH), BF16),
                   jax.ShapeDtypeStruct((1, HY_WIDTH), F32)],
        compiler_params=_cparams(("arbitrary",), 40),
        name="hyena_filter",
    )(zfeat, w1, b1, w2, b2, fr, w3, deltas)


def _fft_a_kernel(g_ref, x_ref, o_ref, *, nb, cw):
    for j in range(nb):
        sl = slice(j * cw, (j + 1) * cw)
        o_ref[:, sl] = _dot(g_ref[j], x_ref[:, sl]).astype(o_ref.dtype)


def _fft_a(x2d, gmat):
    bx, k1, w = x2d.shape
    n2, two_n1, _ = gmat.shape
    cw = w // n2
    nb = FFT_NB
    return pl.pallas_call(
        functools.partial(_fft_a_kernel, nb=nb, cw=cw),
        grid=(bx, n2 // nb),
        in_specs=[pl.BlockSpec((nb, two_n1, k1), lambda b, j: (j, 0, 0)),
                  pl.BlockSpec((None, k1, nb * cw), lambda b, j: (b, 0, j))],
        out_specs=pl.BlockSpec((None, two_n1, nb * cw), lambda b, j: (b, 0, j)),
        out_shape=jax.ShapeDtypeStruct((bx, two_n1, w), BF16),
        compiler_params=_cparams(("parallel", "arbitrary"), 40),
        name="fft_stage_a",
    )(gmat, x2d)


def _fft_filter_b_kernel(fb_ref, a_ref, ss_ref, kf_ref, *, inv_n):
    n1 = a_ref.shape[2]
    xf = _dot(fb_ref[...], jnp.concatenate([a_ref[0, 0], a_ref[0, 1]], axis=0))
    xb = _dot(fb_ref[...], jnp.concatenate([a_ref[1, 0], a_ref[1, 1]], axis=0))
    rs = lax.rsqrt(ss_ref[...] + EPS) * inv_n
    kf_ref[0] = (xf[:n1] + xb[:n1]) * rs
    kf_ref[1] = (xf[n1:] - xb[n1:]) * rs


def _fft_filter_b(a5, fb, ssq, inv_n):
    _, _, n1, n2, c = a5.shape
    return pl.pallas_call(
        functools.partial(_fft_filter_b_kernel, inv_n=inv_n),
        grid=(n1,),
        in_specs=[_const_spec(fb.shape),
                  pl.BlockSpec((2, 2, None, n2, c), lambda k: (0, 0, k, 0, 0)),
                  _const_spec(ssq.shape)],
        out_specs=pl.BlockSpec((2, None, n2, c), lambda k: (0, k, 0, 0)),
        out_shape=jax.ShapeDtypeStruct((2, n1, n2, c), F32),
        compiler_params=_cparams(("arbitrary",), 32),
        name="fft_filter_stage_b",
    )(fb, a5, ssq)


def _fft_b_kernel(fb_ref, fbi_ref, a_ref, kf_ref, o_ref):
    n = a_ref.shape[1]
    x = _dot(fb_ref[...], jnp.concatenate([a_ref[0], a_ref[1]], axis=0))
    xr, xi = x[:n], x[n:]
    kr, ki = kf_ref[0], kf_ref[1]
    y = jnp.concatenate([xr * kr - xi * ki, xr * ki + xi * kr], axis=0).astype(BF16)
    c = _dot(fbi_ref[...], y)
    o_ref[0] = c[:n].astype(o_ref.dtype)
    o_ref[1] = c[n:].astype(o_ref.dtype)


def _fft_b(a5, kf, fb, fbi):
    bsz, _, n1, n2, c = a5.shape
    return pl.pallas_call(
        _fft_b_kernel,
        grid=(n1, bsz),
        in_specs=[_const_spec(fb.shape), _const_spec(fbi.shape),
                  pl.BlockSpec((None, 2, None, n2, c), lambda k, b: (b, 0, k, 0, 0)),
                  pl.BlockSpec((2, None, n2, c), lambda k, b: (0, k, 0, 0))],
        out_specs=pl.BlockSpec((None, 2, None, n2, c), lambda k, b: (b, 0, k, 0, 0)),
        out_shape=jax.ShapeDtypeStruct(a5.shape, BF16),
        compiler_params=_cparams(("arbitrary", "arbitrary"), 32),
        name="fft_stage_b",
    )(fb, fbi, a5, kf)


def _fft_c_kernel(h_ref, c_ref, z_ref, x0_ref, sk_ref, o_ref, *, nb, cw):
    sk = sk_ref[...]
    for j in range(nb):
        sl = slice(j * cw, (j + 1) * cw)
        y = _dot(h_ref[j], c_ref[:, sl])
        z = z_ref[:, sl].astype(F32)
        o_ref[:, sl] = (x0_ref[:, sl].astype(F32) * (y + z * sk)).astype(o_ref.dtype)


def _fft_c(c2d, hmat, z2d, x02d, skip):
    bsz, two_n1, w = c2d.shape
    n2, k1, _ = hmat.shape
    cw = w // n2
    nb = FFT_NB
    blk = lambda b, j: (b, 0, j)
    return pl.pallas_call(
        functools.partial(_fft_c_kernel, nb=nb, cw=cw),
        grid=(bsz, n2 // nb),
        in_specs=[pl.BlockSpec((nb, k1, two_n1), lambda b, j: (j, 0, 0)),
                  pl.BlockSpec((None, two_n1, nb * cw), blk),
                  pl.BlockSpec((None, k1, nb * cw), blk),
                  pl.BlockSpec((None, k1, nb * cw), blk),
                  _const_spec(skip.shape)],
        out_specs=pl.BlockSpec((None, k1, nb * cw), blk),
        out_shape=jax.ShapeDtypeStruct((bsz, k1, w), BF16),
        compiler_params=_cparams(("parallel", "arbitrary"), 40),
        name="fft_stage_c",
    )(hmat, c2d, z2d, x02d, skip)


def _ctx_conv_kernel(fc_ref, fci_ref, h_ref, ss_ref, z_ref, x0_ref, sk_ref, o_ref):
    n = fc_ref.shape[0] // 2
    fc = fc_ref[...]
    kf = _dot(fc, h_ref[0])
    kb = _dot(fc, h_ref[1])
    rs = lax.rsqrt(ss_ref[...] + EPS)
    kr = (kf[:n] + kb[:n]) * rs
    ki = (kf[n:] - kb[n:]) * rs
    zb = z_ref[...]
    x = _dot(fc, zb)
    xr, xi = x[:n], x[n:]
    y = jnp.concatenate([xr * kr - xi * ki, xr * ki + xi * kr], axis=0).astype(BF16)
    y = _dot(fci_ref[...], y)
    z = zb.astype(F32)
    o_ref[...] = (x0_ref[...].astype(F32) * (y + z * sk_ref[...])).astype(o_ref.dtype)


def _ctx_conv(fc, fci, hfb, ssq, z, x0, skip):
    bsz, lc, c = z.shape
    blk = pl.BlockSpec((None, lc, c), lambda b: (b, 0, 0))
    return pl.pallas_call(
        _ctx_conv_kernel,
        grid=(bsz,),
        in_specs=[_const_spec(fc.shape), _const_spec(fci.shape), _const_spec(hfb.shape),
                  _const_spec(ssq.shape), blk, blk, _const_spec(skip.shape)],
        out_specs=blk,
        out_shape=jax.ShapeDtypeStruct((bsz, lc, c), BF16),
        compiler_params=_cparams(("arbitrary",), 32),
        name="hyena_ctx_conv",
    )(fc, fci, hfb, ssq, z, x0, skip)


@functools.lru_cache(maxsize=None)
def _fft_tables(length):
    n = 2 * length
    n2 = FFT_N2
    n1 = n // n2
    k1 = length // n2
    kk = np.arange(n1)[:, None]
    g = np.empty((n2, 2 * n1, k1), np.float64)
    h = np.empty((n2, k1, 2 * n1), np.float64)
    nn = np.arange(k1)[None, :]
    for j in range(n2):
        ang = 2.0 * np.pi * (((n2 * nn * kk) % n) + (j * kk) % n) / n
        g[j, :n1] = np.cos(ang)
        g[j, n1:] = -np.sin(ang)
        h[j, :, :n1] = np.cos(ang).T
        h[j, :, n1:] = -np.sin(ang).T
    a = np.arange(n2)
    ph = 2.0 * np.pi * ((a[:, None] * a[None, :]) % n2) / n2
    c, s = np.cos(ph), np.sin(ph)
    fb = np.block([[c, s], [-s, c]])
    fbi = np.block([[c, -s], [s, c]])
    return (jnp.asarray(g, BF16), jnp.asarray(h, BF16), jnp.asarray(fb, BF16), jnp.asarray(fbi, BF16), n1, k1)


@functools.lru_cache(maxsize=None)
def _dft_tables(length):
    n = 2 * length
    k = np.arange(n)[:, None]
    t = np.arange(length)[None, :]
    ang = 2.0 * np.pi * ((k * t) % n) / n
    fc = np.concatenate([np.cos(ang), -np.sin(ang)], axis=0)
    fci = np.concatenate([np.cos(ang).T, -np.sin(ang).T], axis=1) / n
    return jnp.asarray(fc, BF16), jnp.asarray(fci, BF16)


@functools.lru_cache(maxsize=None)
def _filter_features(length):
    t = np.linspace(0.0, 1.0, length, dtype=np.float32)[:, None]
    w = (2.0 * math.pi * np.arange(length, dtype=np.float32)[:, None] / length).astype(np.float32)
    f = np.linspace(1e-4, HY_BANDS - 1, HY_BANDS, dtype=np.float32)[None, :]
    z = np.concatenate([t, np.cos(f * w), -np.sin(f * w)], axis=-1).astype(np.float32)
    zp = np.zeros((length, V7X_LANES), np.float32)
    zp[:, :HY_EMB_DIM] = z
    return jnp.asarray(zp)


def _hyena_deltas():
    max_decay = math.log(HY_DECAY_TARGET) / HY_FAST_DECAY
    min_decay = math.log(HY_DECAY_TARGET) / HY_SLOW_DECAY
    return jnp.abs(jnp.linspace(min_decay, max_decay, HY_WIDTH, dtype=F32))[None, :]


def _hyena(u, short_w, fw, skip, lat, lc):
    bsz = u.shape[0]
    c = HY_WIDTH
    deltas = _hyena_deltas()
    skip2 = skip.reshape(1, c)
    x0, z = _short_conv(u, short_w, 0, lat)
    gmat, hmat, fb, fbi, n1, k1 = _fft_tables(lat)
    n2 = FFT_N2
    hfb, ssq = _filter(_filter_features(lat), fw, deltas)
    fa = _fft_a(hfb.reshape(2, k1, n2 * c), gmat)
    kf = _fft_filter_b(fa.reshape(2, 2, n1, n2, c), fb, ssq, 1.0 / (2 * lat))
    za = _fft_a(z.reshape(bsz, k1, n2 * c), gmat)
    zc = _fft_b(za.reshape(bsz, 2, n1, n2, c), kf, fb, fbi)
    y = _fft_c(zc.reshape(bsz, 2 * n1, n2 * c), hmat, z.reshape(bsz, k1, n2 * c),
               x0.reshape(bsz, k1, n2 * c), skip2)
    y_lat = y.reshape(bsz, lat, c)
    x0c, zc_ = _short_conv(u, short_w, lat, lc)
    hfb_c, ssq_c = _filter(_filter_features(lc), fw, deltas)
    fc, fci = _dft_tables(lc)
    y_ctx = _ctx_conv(fc, fci, hfb_c, ssq_c, zc_, x0c, skip2)
    return y_lat, y_ctx


def _gqa_kernel(q_ref, kc_ref, kp_ref, kn_ref, kx_ref, vc_ref, vp_ref, vn_ref, vx_ref, sink_ref, o_ref,
                *, lat):
    i = pl.program_id(2)
    tq = q_ref.shape[1]
    hd = GQA_HEAD_DIM
    q_pos = i * tq + lax.broadcasted_iota(jnp.int32, (1, tq), 1)
    q_lat = q_pos < lat

    def window_mask(k0, rows):
        k_pos = k0 + lax.broadcasted_iota(jnp.int32, (rows, 1), 0)
        return (k_pos >= 0) & (k_pos < lat) & q_lat & (jnp.abs(k_pos - q_pos) <= WINDOW)

    m_prev = window_mask(i * tq - WINDOW, WINDOW)
    m_cur = window_mask(i * tq, tq)
    m_next = window_mask((i + 1) * tq, WINDOW)
    kp, kc, kn, kx = kp_ref[...], kc_ref[...], kn_ref[...], kx_ref[...]
    vp, vc, vn, vx = vp_ref[...], vc_ref[...], vn_ref[...], vx_ref[...]
    zq = jnp.zeros((V7X_LANES - hd, tq), BF16)
    for n in range(GQA_GROUP):
        qh = jnp.concatenate([q_ref[n * hd:(n + 1) * hd, :], zq], axis=0)
        s_p = jnp.where(m_prev, _dot(kp, qh), NEG_BIG)
        s_c = jnp.where(m_cur, _dot(kc, qh), NEG_BIG)
        s_n = jnp.where(m_next, _dot(kn, qh), NEG_BIG)
        s_x = _dot(kx, qh)
        sink = sink_ref[n:n + 1, :] * LOG2E
        m = jnp.maximum(jnp.maximum(jnp.max(s_p, axis=0, keepdims=True), jnp.max(s_c, axis=0, keepdims=True)),
                        jnp.maximum(jnp.max(s_n, axis=0, keepdims=True), jnp.max(s_x, axis=0, keepdims=True)))
        m = jnp.maximum(m, sink)
        p_p, p_c = jnp.exp2(s_p - m), jnp.exp2(s_c - m)
        p_n, p_x = jnp.exp2(s_n - m), jnp.exp2(s_x - m)
        den = (jnp.sum(p_p, axis=0, keepdims=True) + jnp.sum(p_c, axis=0, keepdims=True)
               + jnp.sum(p_n, axis=0, keepdims=True) + jnp.sum(p_x, axis=0, keepdims=True)
               + jnp.exp2(sink - m))
        o = (_dot(vp, p_p.astype(BF16)) + _dot(vc, p_c.astype(BF16))
             + _dot(vn, p_n.astype(BF16)) + _dot(vx, p_x.astype(BF16)))
        o_ref[n * hd:(n + 1) * hd, :] = (o / den).astype(o_ref.dtype)


def _gqa(qg, kg, vg, sink_rows, lat, lc):
    bsz, _, s = qg.shape
    tq = GQA_TQ
    w = WINDOW
    gq = GQA_GROUP * GQA_HEAD_DIM
    wpt = tq // w
    last_w = s // w - 1
    return pl.pallas_call(
        functools.partial(_gqa_kernel, lat=lat),
        grid=(bsz, GQA_KV_HEADS, s // tq),
        in_specs=[
            pl.BlockSpec((None, gq, tq), lambda b, g, i: (b, g, i)),
            pl.BlockSpec((None, tq, V7X_LANES), lambda b, g, i: (b, i, g)),
            pl.BlockSpec((None, w, V7X_LANES), lambda b, g, i: (b, jnp.maximum(i * wpt - 1, 0), g)),
            pl.BlockSpec((None, w, V7X_LANES), lambda b, g, i: (b, jnp.minimum((i + 1) * wpt, last_w), g)),
            pl.BlockSpec((None, lc, V7X_LANES), lambda b, g, i: (b, lat // lc, g)),
            pl.BlockSpec((None, GQA_HEAD_DIM, tq), lambda b, g, i: (b, g, i)),
            pl.BlockSpec((None, GQA_HEAD_DIM, w), lambda b, g, i: (b, g, jnp.maximum(i * wpt - 1, 0))),
            pl.BlockSpec((None, GQA_HEAD_DIM, w), lambda b, g, i: (b, g, jnp.minimum((i + 1) * wpt, last_w))),
            pl.BlockSpec((None, GQA_HEAD_DIM, lc), lambda b, g, i: (b, g, lat // lc)),
            pl.BlockSpec((None, GQA_GROUP, tq), lambda b, g, i: (g, 0, 0)),
        ],
        out_specs=pl.BlockSpec((None, gq, tq), lambda b, g, i: (b, g, i)),
        out_shape=jax.ShapeDtypeStruct((bsz, GQA_HEADS * GQA_HEAD_DIM, s), BF16),
        compiler_params=_cparams(("parallel", "arbitrary", "arbitrary"), 32),
        name="gqa_window_attn",
    )(qg, kg, kg, kg, kg, vg, vg, vg, vg, sink_rows)


def _mla_kernel(*refs, plan, aliased):
    if aliased:
        q_ref, k_ref, v_ref, _, o_ref, m_sc, l_sc, acc_sc = refs
    else:
        q_ref, k_ref, v_ref, o_ref, m_sc, l_sc, acc_sc = refs
    q = q_ref[...]
    m_sc[...] = jnp.full(m_sc.shape, NEG_BIG, F32)
    l_sc[...] = jnp.zeros(l_sc.shape, F32)
    acc_sc[...] = jnp.zeros(acc_sc.shape, F32)
    c0, nchunks, subs = plan

    def chunk(j, carry):
        vt = v_ref[j]
        row0 = pl.multiple_of(j * KV_CHUNK, KV_CHUNK)
        for t in subs:
            k = k_ref[pl.ds(row0 + t * KV_SUB, KV_SUB), :]
            s = _dot(k, q)
            m_old = m_sc[...]
            m_new = jnp.maximum(m_old, jnp.max(s, axis=0, keepdims=True))
            alpha = jnp.exp2(m_old - m_new)
            p = jnp.exp2(s - m_new)
            l_sc[...] = alpha * l_sc[...] + jnp.sum(p, axis=0, keepdims=True)
            acc_sc[...] = alpha * acc_sc[...] + _dot(vt[:, t * KV_SUB:(t + 1) * KV_SUB], p.astype(BF16))
            m_sc[...] = m_new
        return carry

    lax.fori_loop(c0, c0 + nchunks, chunk, 0)
    o_ref[...] = (acc_sc[...] / l_sc[...]).astype(o_ref.dtype)


def _mla_call(qm, km, vm, prev_out, tq, q_tile0, n_q, plan):
    bsz, _, s = qm.shape
    nc = vm.shape[1]
    aliased = prev_out is not None
    in_specs = [
        pl.BlockSpec((None, MLA_HEAD_PAD, tq), lambda b, h, i: (b, h, q_tile0 + i)),
        pl.BlockSpec((None, s, MLA_HEAD_PAD), lambda b, h, i: (b, 0, h)),
        pl.BlockSpec((None, nc, MLA_V_DIM, KV_CHUNK), lambda b, h, i: (b, 0, h, 0)),
    ]
    args = [qm, km, vm]
    if aliased:
        in_specs.append(pl.BlockSpec(memory_space=pl.ANY))
        args.append(prev_out)
    return pl.pallas_call(
        functools.partial(_mla_kernel, plan=plan, aliased=aliased),
        grid=(bsz, MLA_HEADS, n_q),
        in_specs=in_specs,
        out_specs=pl.BlockSpec((None, MLA_V_DIM, tq), lambda b, h, i: (b, h, q_tile0 + i)),
        out_shape=jax.ShapeDtypeStruct((bsz, MLA_HEADS * MLA_V_DIM, s), BF16),
        scratch_shapes=[pltpu.VMEM((1, tq), F32), pltpu.VMEM((1, tq), F32), pltpu.VMEM((MLA_V_DIM, tq), F32)],
        input_output_aliases={3: 0} if aliased else {},
        compiler_params=_cparams(("parallel", "arbitrary", "arbitrary"), 40),
        name="mla_attn_ctx" if aliased else "mla_attn",
    )(*args)


def _mla(qm, km, vm, lat, lc):
    s = lat + lc
    nc = s // KV_CHUNK
    nsub = KV_CHUNK // KV_SUB
    tq = min(MLA_TQ, lat)
    out = _mla_call(qm, km, vm, None, tq, 0, lat // tq, (0, nc, tuple(range(nsub))))
    assert lc == KV_SUB and lat % lc == 0
    out = _mla_call(qm, km, vm, out, lc, lat // lc, 1, (nc - 1, 1, (nsub - 1,)))
    return out


def _merge_kernel(x_ref, mod_ref, yl_ref, yc_ref, yg_ref, ym_ref, gt_ref, wb_ref, wo_ref, o_ref, *, lat_tiles):
    d = D_MODEL
    i = pl.program_id(1)
    yh = jnp.where(i < lat_tiles, yl_ref[...], yc_ref[...])
    yg = yg_ref[...].astype(F32).T.astype(BF16)
    ym = ym_ref[...].astype(F32).T.astype(BF16)
    merged = (gt_ref[:, 0:d].astype(F32) * _dot(yh, wb_ref[0])
              + gt_ref[:, d:2 * d].astype(F32) * _dot(yg, wb_ref[1])
              + gt_ref[:, 2 * d:].astype(F32) * _dot(ym, wb_ref[2]))
    res = _dot(merged.astype(BF16), wo_ref[...])
    o_ref[...] = x_ref[...] + mod_ref[:, 2 * d:3 * d] * res


def _merge(xs, modsel, y_lat, y_ctx, yg, ym, gates, wb, wo, lat_tiles):
    bsz, s, d = xs.shape
    tm = ROW_TILE
    c = HY_WIDTH
    row = lambda b, i: (b, i, 0)
    col = lambda b, i: (b, 0, i)
    return pl.pallas_call(
        functools.partial(_merge_kernel, lat_tiles=lat_tiles),
        grid=(bsz, s // tm),
        in_specs=[
            pl.BlockSpec((None, tm, d), row),
            pl.BlockSpec((None, None, 1, 6 * d), lambda b, i: (b, jnp.where(i >= lat_tiles, 1, 0), 0, 0)),
            pl.BlockSpec((None, tm, c), lambda b, i: (b, jnp.minimum(i, lat_tiles - 1), 0)),
            pl.BlockSpec((None, tm, c), lambda b, i: (b, jnp.maximum(i - lat_tiles, 0), 0)),
            pl.BlockSpec((None, c, tm), col),
            pl.BlockSpec((None, c, tm), col),
            pl.BlockSpec((None, tm, 3 * d), row),
            _const_spec(wb.shape), _const_spec(wo.shape),
        ],
        out_specs=pl.BlockSpec((None, tm, d), row),
        out_shape=jax.ShapeDtypeStruct(xs.shape, F32),
        compiler_params=_cparams(("parallel", "arbitrary"), 40),
        name="branch_merge",
    )(xs, modsel, y_lat, y_ctx, yg, ym, gates, wb, wo)


def _mlp_kernel(x_ref, mod_ref, g_ref, w1_ref, w2_ref, o_ref):
    d = D_MODEL
    x = x_ref[...]
    xn = x * lax.rsqrt(jnp.mean(x * x, axis=-1, keepdims=True) + EPS) * g_ref[...]
    h = (xn * (1.0 + mod_ref[:, 4 * d:5 * d]) + mod_ref[:, 3 * d:4 * d]).astype(BF16)
    acc = jnp.zeros(x.shape, F32)
    for j in range(D_FF // FF_CHUNK):
        sl = slice(j * FF_CHUNK, (j + 1) * FF_CHUNK)
        a = jnp.maximum(_dot(h, w1_ref[:, sl]), 0.0)
        acc = acc + _dot((a * a).astype(BF16), w2_ref[sl, :])
    o_ref[...] = x + mod_ref[:, 5 * d:] * acc


def _mlp(xs, modsel, g, w1, w2, lat_tiles):
    bsz, s, d = xs.shape
    tm = ROW_TILE
    row = lambda b, i: (b, i, 0)
    return pl.pallas_call(
        _mlp_kernel,
        grid=(bsz, s // tm),
        in_specs=[
            pl.BlockSpec((None, tm, d), row),
            pl.BlockSpec((None, None, 1, 6 * d), lambda b, i: (b, jnp.where(i >= lat_tiles, 1, 0), 0, 0)),
            _const_spec(g.shape), _const_spec(w1.shape), _const_spec(w2.shape),
        ],
        out_specs=pl.BlockSpec((None, tm, d), row),
        out_shape=jax.ShapeDtypeStruct(xs.shape, F32),
        compiler_params=_cparams(("parallel", "arbitrary"), 56),
        name="relu2_mlp",
    )(xs, modsel, g, w1, w2)


def _rope_tables_t(rows, dim, lc):
    n_freq = dim // 4
    inv = ROPE_BASE ** (-jnp.arange(n_freq, dtype=F32) / n_freq)
    r = jnp.repeat(jnp.arange(rows, dtype=F32), GRID_W)
    col = jnp.tile(jnp.arange(GRID_W, dtype=F32), rows)
    ang = jnp.concatenate([r[:, None] * inv, col[:, None] * inv], axis=-1)
    cos_t = jnp.concatenate([jnp.cos(ang).T, jnp.ones((dim // 2, lc), F32)], axis=1)
    sin_t = jnp.concatenate([jnp.sin(ang).T, jnp.zeros((dim // 2, lc), F32)], axis=1)
    return cos_t, sin_t


def _lane_bcast(v):
    return jnp.broadcast_to(v.astype(F32)[:, None], (v.shape[0], ROW_TILE))


def _layer_weights(l, w_in, gqa_q_norm, gqa_k_norm, mla_q_a_norm, mla_kv_a_norm, w_q_b, w_kv_b, mla_q_norm, mla_k_norm):
    w = w_in[l]
    o = np.cumsum([0, 3 * HY_WIDTH, GQA_HEADS * GQA_HEAD_DIM, GQA_KV_HEADS * GQA_HEAD_DIM,
                   GQA_KV_HEADS * GQA_HEAD_DIM, MLA_Q_RANK, MLA_KV_RANK, MLA_ROPE_DIM, 3 * D_MODEL])
    wu = w[:, o[0]:o[1]].astype(BF16)
    wt = w[:, o[1]:o[7]].T.astype(BF16)
    wg = w[:, o[7]:o[8]].astype(BF16)
    wqb = w_q_b[l].T.astype(BF16)
    wkvb = w_kv_b[l].T.astype(BF16)
    return (wu, wg, wt, wqb, wkvb,
            _lane_bcast(gqa_q_norm[l]), _lane_bcast(gqa_k_norm[l]),
            _lane_bcast(mla_q_a_norm[l]), _lane_bcast(mla_kv_a_norm[l]),
            _lane_bcast(mla_q_norm[l]), _lane_bcast(mla_k_norm[l]))


def kernel(x, c, ctx, c_ctx, w_mod, b_mod, norm_mix_g, norm_mlp_g, w_in, hy_short_w, hy_f1_w, hy_f1_b, hy_f2_w, hy_f2_b, hy_sin_freq, hy_f3_w, hy_skip, gqa_q_norm, gqa_k_norm, gqa_sink, mla_q_a_norm, mla_kv_a_norm, w_q_b, w_kv_b, mla_q_norm, mla_k_norm, w_branch, w_out, w_mlp1, w_mlp2):
    bsz, lat, d = x.shape
    lc = ctx.shape[1]
    depth = w_mod.shape[0]
    s = lat + lc
    assert d == D_MODEL and lat % GRID_W == 0 and lat % ROW_TILE == 0 and lc == ROW_TILE and s % KV_CHUNK == 0
    lat_tiles = lat // ROW_TILE

    pad = (-(bsz + 1)) % 8
    cond = jnp.concatenate([c, c_ctx[None, :], jnp.zeros((pad, d), F32)], axis=0)
    mods = _mod_all(cond, w_mod, b_mod)

    tabs = _rope_tables_t(lat // GRID_W, GQA_HEAD_DIM, lc) + _rope_tables_t(lat // GRID_W, MLA_ROPE_DIM, lc)
    xs = jnp.concatenate([x, ctx], axis=1)

    for l in range(depth):
        ml = mods[l]
        modsel = jnp.stack([ml[:bsz], jnp.broadcast_to(ml[bsz][None], (bsz, 6 * d))], axis=1)[:, :, None, :]
        wts = _layer_weights(l, w_in, gqa_q_norm, gqa_k_norm, mla_q_a_norm, mla_kv_a_norm, w_q_b, w_kv_b,
                             mla_q_norm, mla_k_norm)
        u, gates, qg, kg, vg, qm, km, vm = _proj(xs, modsel, norm_mix_g[l][None, :], wts, tabs, lat_tiles)

        f1w = jnp.zeros((V7X_LANES, HY_FILTER_WIDTH), F32).at[:HY_EMB_DIM].set(hy_f1_w[l])
        fw = (f1w, hy_f1_b[l][None, :], hy_f2_w[l], hy_f2_b[l][None, :], hy_sin_freq[l], hy_f3_w[l])
        y_lat, y_ctx = _hyena(u, hy_short_w[l], fw, hy_skip[l], lat, lc)

        sink_rows = jnp.broadcast_to(gqa_sink[l].reshape(GQA_KV_HEADS, GQA_GROUP, 1),
                                     (GQA_KV_HEADS, GQA_GROUP, GQA_TQ)).astype(F32)
        yg = _gqa(qg, kg, vg, sink_rows, lat, lc)
        ym = _mla(qm, km, vm, lat, lc)

        xs = _merge(xs, modsel, y_lat, y_ctx, yg, ym, gates, w_branch[l].astype(BF16), w_out[l].astype(BF16),
                    lat_tiles)
        xs = _mlp(xs, modsel, norm_mlp_g[l][None, :], w_mlp1[l].astype(BF16), w_mlp2[l].astype(BF16), lat_tiles)
    return xs[:, :lat]
```

```python
import functools
import math

import numpy as np
import jax
import jax.numpy as jnp
from jax import lax
from jax.experimental import pallas as pl
from jax.experimental.pallas import tpu as pltpu

D_MODEL = 1024
GRID_W = 64
HY_WIDTH = 512
HY_EMB_DIM = 33
HY_BANDS = (HY_EMB_DIM - 1) // 2
HY_FILTER_WIDTH = 64
HY_DECAY_TARGET = 1e-2
HY_FAST_DECAY = 0.3
HY_SLOW_DECAY = 1.5
GQA_HEADS = 8
GQA_KV_HEADS = 2
GQA_GROUP = GQA_HEADS // GQA_KV_HEADS
GQA_HEAD_DIM = 64
GQA_SCALE = GQA_HEAD_DIM ** -0.5
WINDOW = 128
MLA_HEADS = 8
MLA_Q_RANK = 384
MLA_KV_RANK = 256
MLA_NOPE_DIM = 64
MLA_ROPE_DIM = 32
MLA_V_DIM = 64
MLA_QK_DIM = MLA_NOPE_DIM + MLA_ROPE_DIM
MLA_SCALE = MLA_QK_DIM ** -0.5
D_FF = 4 * D_MODEL
ROPE_BASE = 10000.0
EPS = 1e-6
LOG2E = 1.4426950408889634
NEG_BIG = -1e30

V7X_LANES = 128
V7X_VMEM_BYTES = 64 * 1024 * 1024

ROW_TILE = 256
MLA_HEAD_PAD = 128
KV_CHUNK = 3 * ROW_TILE
KV_SUB = 256
MLA_TQ = 512
GQA_TQ = 256
FF_CHUNK = 1024
FFT_N2 = 128
FFT_NB = 16

F32 = jnp.float32
BF16 = jnp.bfloat16


def _cparams(sem, vmem_mb):
    return pltpu.CompilerParams(dimension_semantics=sem, vmem_limit_bytes=vmem_mb * 1024 * 1024)


def _dot(a, b):
    return jnp.dot(a, b, preferred_element_type=F32)


def _dot_hi(a, b):
    return jnp.dot(a, b, preferred_element_type=F32, precision=lax.Precision.HIGHEST)


def _const_spec(shape):
    nd = len(shape)
    return pl.BlockSpec(shape, lambda *_: (0,) * nd)


def _mod_kernel(c_ref, w_ref, b_ref, o_ref):
    c = c_ref[...]
    s = c * jax.nn.sigmoid(c)
    o_ref[...] = _dot_hi(s, w_ref[...]) + b_ref[...]


def _mod_all(cond, w_mod, b_mod):
    depth, d, n = w_mod.shape
    rows = cond.shape[0]
    tn = 1536
    return pl.pallas_call(
        _mod_kernel,
        grid=(depth, n // tn),
        in_specs=[
            pl.BlockSpec((rows, d), lambda l, j: (0, 0)),
            pl.BlockSpec((None, d, tn), lambda l, j: (l, 0, j)),
            pl.BlockSpec((None, 1, tn), lambda l, j: (l, 0, j)),
        ],
        out_specs=pl.BlockSpec((None, rows, tn), lambda l, j: (l, 0, j)),
        out_shape=jax.ShapeDtypeStruct((depth, rows, n), F32),
        compiler_params=_cparams(("arbitrary", "arbitrary"), 40),
        name="adaln_mod",
    )(cond, w_mod, b_mod.reshape(depth, 1, n))


def _rms_rows(x, n):
    return lax.rsqrt(jnp.sum(x * x, axis=0, keepdims=True) * (1.0 / n) + EPS)


def _rope_rows(x1, x2, cs, sn):
    return x1 * cs - x2 * sn, x1 * sn + x2 * cs


def _proj_kernel(x_ref, mod_ref, g_ref, wu_ref, wg_ref, wt_ref, wqb_ref, wkvb_ref,
                 gq_ref, gk_ref, gqa_ref, gkva_ref, gmq_ref, gmk_ref,
                 cg_ref, sg_ref, cm_ref, sm_ref,
                 u_ref, gate_ref, qg_ref, kg_ref, vg_ref, qm_ref, km_ref, vm_ref):
    d = D_MODEL
    x = x_ref[...]
    tm = x.shape[0]
    shift = mod_ref[:, 0:d]
    scale = mod_ref[:, d:2 * d]
    xn = x * lax.rsqrt(jnp.mean(x * x, axis=-1, keepdims=True) + EPS) * g_ref[...]
    h = xn * (1.0 + scale) + shift
    hb = h.astype(BF16)

    u_ref[...] = _dot(hb, wu_ref[...]).astype(BF16)
    gate_ref[...] = jax.nn.sigmoid(_dot(hb, wg_ref[...])).astype(BF16)

    ht = h.T.astype(BF16)
    t = _dot(wt_ref[...], ht)
    o_q, o_k, o_v = 0, 512, 640
    o_cq, o_ckv, o_kr = 768, 768 + MLA_Q_RANK, 768 + MLA_Q_RANK + MLA_KV_RANK

    cg, sg = cg_ref[...], sg_ref[...]
    cm, sm = cm_ref[...], sm_ref[...]
    hd, hh = GQA_HEAD_DIM, GQA_HEAD_DIM // 2

    gq = gq_ref[...]
    for n in range(GQA_HEADS):
        xh = t[o_q + n * hd:o_q + (n + 1) * hd]
        xh = xh * _rms_rows(xh, hd) * gq
        a, b = _rope_rows(xh[:hh], xh[hh:], cg, sg)
        qg_ref[n * hd:n * hd + hh, :] = (a * (GQA_SCALE * LOG2E)).astype(BF16)
        qg_ref[n * hd + hh:(n + 1) * hd, :] = (b * (GQA_SCALE * LOG2E)).astype(BF16)

    gk = gk_ref[...]
    zpad = jnp.zeros((V7X_LANES - hd, tm), F32)
    parts = []
    for n in range(GQA_KV_HEADS):
        xh = t[o_k + n * hd:o_k + (n + 1) * hd]
        xh = xh * _rms_rows(xh, hd) * gk
        a, b = _rope_rows(xh[:hh], xh[hh:], cg, sg)
        parts += [a, b, zpad]
    kg_ref[...] = jnp.concatenate(parts, axis=0).T.astype(BF16)
    vg_ref[...] = t[o_v:o_v + GQA_KV_HEADS * hd].astype(BF16)

    cq = t[o_cq:o_cq + MLA_Q_RANK]
    cqn = (cq * _rms_rows(cq, MLA_Q_RANK) * gqa_ref[...]).astype(BF16)
    qm = _dot(wqb_ref[...], cqn)
    gmq = gmq_ref[...]
    nd, rh = MLA_NOPE_DIM, MLA_ROPE_DIM // 2
    qpad = jnp.zeros((MLA_HEAD_PAD - MLA_QK_DIM, tm), BF16)
    for n in range(MLA_HEADS):
        xh = qm[n * MLA_QK_DIM:(n + 1) * MLA_QK_DIM]
        xh = xh * _rms_rows(xh, MLA_QK_DIM) * gmq * (MLA_SCALE * LOG2E)
        a, b = _rope_rows(xh[nd:nd + rh], xh[nd + rh:], cm, sm)
        base = n * MLA_HEAD_PAD
        qm_ref[base:base + nd, :] = xh[:nd].astype(BF16)
        qm_ref[base + nd:base + nd + rh, :] = a.astype(BF16)
        qm_ref[base + nd + rh:base + MLA_QK_DIM, :] = b.astype(BF16)
        qm_ref[base + MLA_QK_DIM:base + MLA_HEAD_PAD, :] = qpad

    ckv = t[o_ckv:o_ckv + MLA_KV_RANK]
    ckvn = (ckv * _rms_rows(ckv, MLA_KV_RANK) * gkva_ref[...]).astype(BF16)
    kv = _dot(wkvb_ref[...], ckvn)
    kr = t[o_kr:o_kr + MLA_ROPE_DIM]
    kr_ss = jnp.sum(kr * kr, axis=0, keepdims=True)
    gmk = gmk_ref[...]
    kzero = jnp.zeros((MLA_HEAD_PAD - MLA_QK_DIM, tm), F32)
    parts = []
    for n in range(MLA_HEADS):
        kn = kv[n * 128:n * 128 + nd]
        vm_ref[n * MLA_V_DIM:(n + 1) * MLA_V_DIM, :] = kv[n * 128 + nd:(n + 1) * 128].astype(BF16)
        rs = lax.rsqrt((jnp.sum(kn * kn, axis=0, keepdims=True) + kr_ss) * (1.0 / MLA_QK_DIM) + EPS)
        krn = kr * rs * gmk[nd:]
        a, b = _rope_rows(krn[:rh], krn[rh:], cm, sm)
        parts += [kn * rs * gmk[:nd], a, b, kzero]
    km_ref[...] = jnp.concatenate(parts, axis=0).T.astype(BF16)


def _proj(xs, modsel, g, wts, tabs, lat_tiles):
    bsz, s, d = xs.shape
    tm = ROW_TILE
    nt = s // tm
    ncs = KV_CHUNK // tm
    (wu, wg, wt, wqb, wkvb, gq, gk, gqa, gkva, gmq, gmk) = wts
    cg, sg, cm, sm = tabs
    row = lambda b, i: (b, i, 0)
    col = lambda b, i: (b, 0, i)
    tab = lambda b, i: (0, i)
    in_specs = [
        pl.BlockSpec((None, tm, d), row),
        pl.BlockSpec((None, None, 1, 6 * d), lambda b, i: (b, jnp.where(i >= lat_tiles, 1, 0), 0, 0)),
        _const_spec(g.shape), _const_spec(wu.shape), _const_spec(wg.shape), _const_spec(wt.shape),
        _const_spec(wqb.shape), _const_spec(wkvb.shape),
        _const_spec(gq.shape), _const_spec(gk.shape), _const_spec(gqa.shape), _const_spec(gkva.shape),
        _const_spec(gmq.shape), _const_spec(gmk.shape),
        pl.BlockSpec((cg.shape[0], tm), tab), pl.BlockSpec((sg.shape[0], tm), tab),
        pl.BlockSpec((cm.shape[0], tm), tab), pl.BlockSpec((sm.shape[0], tm), tab),
    ]
    out_shape = [
        jax.ShapeDtypeStruct((bsz, s, 3 * HY_WIDTH), BF16),
        jax.ShapeDtypeStruct((bsz, s, 3 * d), BF16),
        jax.ShapeDtypeStruct((bsz, GQA_HEADS * GQA_HEAD_DIM, s), BF16),
        jax.ShapeDtypeStruct((bsz, s, GQA_KV_HEADS * V7X_LANES), BF16),
        jax.ShapeDtypeStruct((bsz, GQA_KV_HEADS * GQA_HEAD_DIM, s), BF16),
        jax.ShapeDtypeStruct((bsz, MLA_HEADS * MLA_HEAD_PAD, s), BF16),
        jax.ShapeDtypeStruct((bsz, s, MLA_HEADS * MLA_HEAD_PAD), BF16),
        jax.ShapeDtypeStruct((bsz, s // KV_CHUNK, MLA_HEADS * MLA_V_DIM, KV_CHUNK), BF16),
    ]
    out_specs = [
        pl.BlockSpec((None, tm, 3 * HY_WIDTH), row),
        pl.BlockSpec((None, tm, 3 * d), row),
        pl.BlockSpec((None, GQA_HEADS * GQA_HEAD_DIM, tm), col),
        pl.BlockSpec((None, tm, GQA_KV_HEADS * V7X_LANES), row),
        pl.BlockSpec((None, GQA_KV_HEADS * GQA_HEAD_DIM, tm), col),
        pl.BlockSpec((None, MLA_HEADS * MLA_HEAD_PAD, tm), col),
        pl.BlockSpec((None, tm, MLA_HEADS * MLA_HEAD_PAD), row),
        pl.BlockSpec((None, None, MLA_HEADS * MLA_V_DIM, tm), lambda b, i: (b, i // ncs, 0, i % ncs)),
    ]
    return pl.pallas_call(
        _proj_kernel,
        grid=(bsz, nt),
        in_specs=in_specs,
        out_specs=out_specs,
        out_shape=out_shape,
        compiler_params=_cparams(("parallel", "arbitrary"), 56),
        name="in_proj",
    )(xs, modsel, g, wu, wg, wt, wqb, wkvb, gq, gk, gqa, gkva, gmq, gmk, cg, sg, cm, sm)


def _short_conv_kernel(u_ref, up_ref, un_ref, w_ref, x0_ref, z_ref, *, n_tiles):
    i = pl.program_id(1)
    u = u_ref[...].astype(F32)
    tm = u.shape[0]
    prev = jnp.where(i > 0, up_ref[7:8, :].astype(F32), 0.0)
    nxt = jnp.where(i < n_tiles - 1, un_ref[0:1, :].astype(F32), 0.0)
    ridx = lax.broadcasted_iota(jnp.int32, u.shape, 0)
    up = jnp.where(ridx == 0, prev, pltpu.roll(u, 1, axis=0))
    dn = jnp.where(ridx == tm - 1, nxt, pltpu.roll(u, tm - 1, axis=0))
    uc = up * w_ref[0:1, :] + u * w_ref[1:2, :] + dn * w_ref[2:3, :]
    c = HY_WIDTH
    x0_ref[...] = uc[:, :c].astype(BF16)
    z_ref[...] = (uc[:, c:2 * c] * uc[:, 2 * c:]).astype(BF16)


def _short_conv(u, short_w, row0, rows):
    bsz, s, c3 = u.shape
    tm = ROW_TILE
    nt = rows // tm
    t0 = row0 // tm
    r8 = tm // 8
    last8 = s // 8 - 1
    return pl.pallas_call(
        functools.partial(_short_conv_kernel, n_tiles=nt),
        grid=(bsz, nt),
        in_specs=[
            pl.BlockSpec((None, tm, c3), lambda b, i: (b, t0 + i, 0)),
            pl.BlockSpec((None, 8, c3), lambda b, i: (b, jnp.maximum((t0 + i) * r8 - 1, 0), 0)),
            pl.BlockSpec((None, 8, c3), lambda b, i: (b, jnp.minimum((t0 + i + 1) * r8, last8), 0)),
            _const_spec(short_w.shape),
        ],
        out_specs=[pl.BlockSpec((None, tm, HY_WIDTH), lambda b, i: (b, i, 0))] * 2,
        out_shape=[jax.ShapeDtypeStruct((bsz, rows, HY_WIDTH), BF16)] * 2,
        compiler_params=_cparams(("parallel", "arbitrary"), 32),
        name="hyena_short_conv",
    )(u, u, u, short_w)


def _filter_kernel(zf_ref, w1_ref, b1_ref, w2_ref, b2_ref, fr_ref, w3_ref, dl_ref, h_ref, ss_ref):
    i = pl.program_id(0)
    zf = zf_ref[...]
    tl = zf.shape[0]
    h = jnp.sin(fr_ref[0:1, :] * (_dot_hi(zf, w1_ref[...]) + b1_ref[...]))
    h = jnp.sin(fr_ref[1:2, :] * (_dot_hi(h, w2_ref[...]) + b2_ref[...]))
    h = _dot_hi(h, w3_ref[...])
    decay = jnp.exp(-zf[:, 0:1] * dl_ref[...])
    c = HY_WIDTH
    hf = h[:, :c] * decay
    ridx = lax.broadcasted_iota(jnp.int32, (tl, c), 0) + i * tl
    hb = jnp.where(ridx == 0, 0.0, h[:, c:] * decay)
    h_ref[0] = hf.astype(BF16)
    h_ref[1] = hb.astype(BF16)
    ss = jnp.sum(hf * hf + hb * hb, axis=0, keepdims=True)

    @pl.when(i == 0)
    def _():
        ss_ref[...] = ss

    @pl.when(i > 0)
    def _():
        ss_ref[...] += ss


def _filter(zfeat, fw, deltas):
    length = zfeat.shape[0]
    tl = min(length, 1024)
    w1, b1, w2, b2, fr, w3 = fw
    return pl.pallas_call(
        _filter_kernel,
        grid=(length // tl,),
        in_specs=[pl.BlockSpec((tl, zfeat.shape[1]), lambda i: (i, 0))]
        + [_const_spec(a.shape) for a in (w1, b1, w2, b2, fr, w3, deltas)],
        out_specs=[pl.BlockSpec((2, tl, HY_WIDTH), lambda i: (0, i, 0)),
                   pl.BlockSpec((1, HY_WIDTH), lambda i: (0, 0))],
        out_shape=[jax.ShapeDtypeStruct((2, length, HY_WIDTH), BF16),
                   jax.ShapeDtypeStruct((1, HY_WIDTH), F32)],
        compiler_params=_cparams(("arbitrary",), 40),
        name="hyena_filter",
    )(zfeat, w1, b1, w2, b2, fr, w3, deltas)


def _fft_a_kernel(g_ref, x_ref, o_ref, *, nb, cw):
    for j in range(nb):
        sl = slice(j * cw, (j + 1) * cw)
        o_ref[:, sl] = _dot(g_ref[j], x_ref[:, sl]).astype(o_ref.dtype)


def _fft_a(x2d, gmat):
    bx, k1, w = x2d.shape
    n2, two_n1, _ = gmat.shape
    cw = w // n2
    nb = FFT_NB
    return pl.pallas_call(
        functools.partial(_fft_a_kernel, nb=nb, cw=cw),
        grid=(bx, n2 // nb),
        in_specs=[pl.BlockSpec((nb, two_n1, k1), lambda b, j: (j, 0, 0)),
                  pl.BlockSpec((None, k1, nb * cw), lambda b, j: (b, 0, j))],
        out_specs=pl.BlockSpec((None, two_n1, nb * cw), lambda b, j: (b, 0, j)),
        out_shape=jax.ShapeDtypeStruct((bx, two_n1, w), BF16),
        compiler_params=_cparams(("parallel", "arbitrary"), 40),
        name="fft_stage_a",
    )(gmat, x2d)


def _fft_filter_b_kernel(fb_ref, a_ref, ss_ref, kf_ref, *, inv_n):
    n1 = a_ref.shape[2]
    xf = _dot(fb_ref[...], jnp.concatenate([a_ref[0, 0], a_ref[0, 1]], axis=0))
    xb = _dot(fb_ref[...], jnp.concatenate([a_ref[1, 0], a_ref[1, 1]], axis=0))
    rs = lax.rsqrt(ss_ref[...] + EPS) * inv_n
    kf_ref[0] = (xf[:n1] + xb[:n1]) * rs
    kf_ref[1] = (xf[n1:] - xb[n1:]) * rs


def _fft_filter_b(a5, fb, ssq, inv_n):
    _, _, n1, n2, c = a5.shape
    return pl.pallas_call(
        functools.partial(_fft_filter_b_kernel, inv_n=inv_n),
        grid=(n1,),
        in_specs=[_const_spec(fb.shape),
                  pl.BlockSpec((2, 2, None, n2, c), lambda k: (0, 0, k, 0, 0)),
                  _const_spec(ssq.shape)],
        out_specs=pl.BlockSpec((2, None, n2, c), lambda k: (0, k, 0, 0)),
        out_shape=jax.ShapeDtypeStruct((2, n1, n2, c), F32),
        compiler_params=_cparams(("arbitrary",), 32),
        name="fft_filter_stage_b",
    )(fb, a5, ssq)


def _fft_b_kernel(fb_ref, fbi_ref, a_ref, kf_ref, o_ref):
    n = a_ref.shape[1]
    x = _dot(fb_ref[...], jnp.concatenate([a_ref[0], a_ref[1]], axis=0))
    xr, xi = x[:n], x[n:]
    kr, ki = kf_ref[0], kf_ref[1]
    y = jnp.concatenate([xr * kr - xi * ki, xr * ki + xi * kr], axis=0).astype(BF16)
    c = _dot(fbi_ref[...], y)
    o_ref[0] = c[:n].astype(o_ref.dtype)
    o_ref[1] = c[n:].astype(o_ref.dtype)


def _fft_b(a5, kf, fb, fbi):
    bsz, _, n1, n2, c = a5.shape
    return pl.pallas_call(
        _fft_b_kernel,
        grid=(n1, bsz),
        in_specs=[_const_spec(fb.shape), _const_spec(fbi.shape),
                  pl.BlockSpec((None, 2, None, n2, c), lambda k, b: (b, 0, k, 0, 0)),
                  pl.BlockSpec((2, None, n2, c), lambda k, b: (0, k, 0, 0))],
        out_specs=pl.BlockSpec((None, 2, None, n2, c), lambda k, b: (b, 0, k, 0, 0)),
        out_shape=jax.ShapeDtypeStruct(a5.shape, BF16),
        compiler_params=_cparams(("arbitrary", "arbitrary"), 32),
        name="fft_stage_b",
    )(fb, fbi, a5, kf)


def _fft_c_kernel(h_ref, c_ref, z_ref, x0_ref, sk_ref, o_ref, *, nb, cw):
    sk = sk_ref[...]
    for j in range(nb):
        sl = slice(j * cw, (j + 1) * cw)
        y = _dot(h_ref[j], c_ref[:, sl])
        z = z_ref[:, sl].astype(F32)
        o_ref[:, sl] = (x0_ref[:, sl].astype(F32) * (y + z * sk)).astype(o_ref.dtype)


def _fft_c(c2d, hmat, z2d, x02d, skip):
    bsz, two_n1, w = c2d.shape
    n2, k1, _ = hmat.shape
    cw = w // n2
    nb = FFT_NB
    blk = lambda b, j: (b, 0, j)
    return pl.pallas_call(
        functools.partial(_fft_c_kernel, nb=nb, cw=cw),
        grid=(bsz, n2 // nb),
        in_specs=[pl.BlockSpec((nb, k1, two_n1), lambda b, j: (j, 0, 0)),
                  pl.BlockSpec((None, two_n1, nb * cw), blk),
                  pl.BlockSpec((None, k1, nb * cw), blk),
                  pl.BlockSpec((None, k1, nb * cw), blk),
                  _const_spec(skip.shape)],
        out_specs=pl.BlockSpec((None, k1, nb * cw), blk),
        out_shape=jax.ShapeDtypeStruct((bsz, k1, w), BF16),
        compiler_params=_cparams(("parallel", "arbitrary"), 40),
        name="fft_stage_c",
    )(hmat, c2d, z2d, x02d, skip)


def _ctx_conv_kernel(fc_ref, fci_ref, h_ref, ss_ref, z_ref, x0_ref, sk_ref, o_ref):
    n = fc_ref.shape[0] // 2
    fc = fc_ref[...]
    kf = _dot(fc, h_ref[0])
    kb = _dot(fc, h_ref[1])
    rs = lax.rsqrt(ss_ref[...] + EPS)
    kr = (kf[:n] + kb[:n]) * rs
    ki = (kf[n:] - kb[n:]) * rs
    zb = z_ref[...]
    x = _dot(fc, zb)
    xr, xi = x[:n], x[n:]
    y = jnp.concatenate([xr * kr - xi * ki, xr * ki + xi * kr], axis=0).astype(BF16)
    y = _dot(fci_ref[...], y)
    z = zb.astype(F32)
    o_ref[...] = (x0_ref[...].astype(F32) * (y + z * sk_ref[...])).astype(o_ref.dtype)


def _ctx_conv(fc, fci, hfb, ssq, z, x0, skip):
    bsz, lc, c = z.shape
    blk = pl.BlockSpec((None, lc, c), lambda b: (b, 0, 0))
    return pl.pallas_call(
        _ctx_conv_kernel,
        grid=(bsz,),
        in_specs=[_const_spec(fc.shape), _const_spec(fci.shape), _const_spec(hfb.shape),
                  _const_spec(ssq.shape), blk, blk, _const_spec(skip.shape)],
        out_specs=blk,
        out_shape=jax.ShapeDtypeStruct((bsz, lc, c), BF16),
        compiler_params=_cparams(("arbitrary",), 32),
        name="hyena_ctx_conv",
    )(fc, fci, hfb, ssq, z, x0, skip)


@functools.lru_cache(maxsize=None)
def _fft_tables(length):
    n = 2 * length
    n2 = FFT_N2
    n1 = n // n2
    k1 = length // n2
    kk = np.arange(n1)[:, None]
    g = np.empty((n2, 2 * n1, k1), np.float64)
    h = np.empty((n2, k1, 2 * n1), np.float64)
    nn = np.arange(k1)[None, :]
    for j in range(n2):
        ang = 2.0 * np.pi * (((n2 * nn * kk) % n) + (j * kk) % n) / n
        g[j, :n1] = np.cos(ang)
        g[j, n1:] = -np.sin(ang)
        h[j, :, :n1] = np.cos(ang).T
        h[j, :, n1:] = -np.sin(ang).T
    a = np.arange(n2)
    ph = 2.0 * np.pi * ((a[:, None] * a[None, :]) % n2) / n2
    c, s = np.cos(ph), np.sin(ph)
    fb = np.block([[c, s], [-s, c]])
    fbi = np.block([[c, -s], [s, c]])
    return (jnp.asarray(g, BF16), jnp.asarray(h, BF16), jnp.asarray(fb, BF16), jnp.asarray(fbi, BF16), n1, k1)


@functools.lru_cache(maxsize=None)
def _dft_tables(length):
    n = 2 * length
    k = np.arange(n)[:, None]
    t = np.arange(length)[None, :]
    ang = 2.0 * np.pi * ((k * t) % n) / n
    fc = np.concatenate([np.cos(ang), -np.sin(ang)], axis=0)
    fci = np.concatenate([np.cos(ang).T, -np.sin(ang).T], axis=1) / n
    return jnp.asarray(fc, BF16), jnp.asarray(fci, BF16)


@functools.lru_cache(maxsize=None)
def _filter_features(length):
    t = np.linspace(0.0, 1.0, length, dtype=np.float32)[:, None]
    w = (2.0 * math.pi * np.arange(length, dtype=np.float32)[:, None] / length).astype(np.float32)
    f = np.linspace(1e-4, HY_BANDS - 1, HY_BANDS, dtype=np.float32)[None, :]
    z = np.concatenate([t, np.cos(f * w), -np.sin(f * w)], axis=-1).astype(np.float32)
    zp = np.zeros((length, V7X_LANES), np.float32)
    zp[:, :HY_EMB_DIM] = z
    return jnp.asarray(zp)


def _hyena_deltas():
    max_decay = math.log(HY_DECAY_TARGET) / HY_FAST_DECAY
    min_decay = math.log(HY_DECAY_TARGET) / HY_SLOW_DECAY
    return jnp.abs(jnp.linspace(min_decay, max_decay, HY_WIDTH, dtype=F32))[None, :]


def _hyena(u, short_w, fw, skip, lat, lc):
    bsz = u.shape[0]
    c = HY_WIDTH
    deltas = _hyena_deltas()
    skip2 = skip.reshape(1, c)
    x0, z = _short_conv(u, short_w, 0, lat)
    gmat, hmat, fb, fbi, n1, k1 = _fft_tables(lat)
    n2 = FFT_N2
    hfb, ssq = _filter(_filter_features(lat), fw, deltas)
    fa = _fft_a(hfb.reshape(2, k1, n2 * c), gmat)
    kf = _fft_filter_b(fa.reshape(2, 2, n1, n2, c), fb, ssq, 1.0 / (2 * lat))
    za = _fft_a(z.reshape(bsz, k1, n2 * c), gmat)
    zc = _fft_b(za.reshape(bsz, 2, n1, n2, c), kf, fb, fbi)
    y = _fft_c(zc.reshape(bsz, 2 * n1, n2 * c), hmat, z.reshape(bsz, k1, n2 * c),
               x0.reshape(bsz, k1, n2 * c), skip2)
    y_lat = y.reshape(bsz, lat, c)
    x0c, zc_ = _short_conv(u, short_w, lat, lc)
    hfb_c, ssq_c = _filter(_filter_features(lc), fw, deltas)
    fc, fci = _dft_tables(lc)
    y_ctx = _ctx_conv(fc, fci, hfb_c, ssq_c, zc_, x0c, skip2)
    return y_lat, y_ctx


def _gqa_kernel(q_ref, kc_ref, kp_ref, kn_ref, kx_ref, vc_ref, vp_ref, vn_ref, vx_ref, sink_ref, o_ref,
                *, lat):
    i = pl.program_id(2)
    tq = q_ref.shape[1]
    hd = GQA_HEAD_DIM
    q_pos = i * tq + lax.broadcasted_iota(jnp.int32, (1, tq), 1)
    q_lat = q_pos < lat

    def window_mask(k0, rows):
        k_pos = k0 + lax.broadcasted_iota(jnp.int32, (rows, 1), 0)
        return (k_pos >= 0) & (k_pos < lat) & q_lat & (jnp.abs(k_pos - q_pos) <= WINDOW)

    m_prev = window_mask(i * tq - WINDOW, WINDOW)
    m_cur = window_mask(i * tq, tq)
    m_next = window_mask((i + 1) * tq, WINDOW)
    kp, kc, kn, kx = kp_ref[...], kc_ref[...], kn_ref[...], kx_ref[...]
    vp, vc, vn, vx = vp_ref[...], vc_ref[...], vn_ref[...], vx_ref[...]
    zq = jnp.zeros((V7X_LANES - hd, tq), BF16)
    for n in range(GQA_GROUP):
        qh = jnp.concatenate([q_ref[n * hd:(n + 1) * hd, :], zq], axis=0)
        s_p = jnp.where(m_prev, _dot(kp, qh), NEG_BIG)
        s_c = jnp.where(m_cur, _dot(kc, qh), NEG_BIG)
        s_n = jnp.where(m_next, _dot(kn, qh), NEG_BIG)
        s_x = _dot(kx, qh)
        sink = sink_ref[n:n + 1, :] * LOG2E
        m = jnp.maximum(jnp.maximum(jnp.max(s_p, axis=0, keepdims=True), jnp.max(s_c, axis=0, keepdims=True)),
                        jnp.maximum(jnp.max(s_n, axis=0, keepdims=True), jnp.max(s_x, axis=0, keepdims=True)))
        m = jnp.maximum(m, sink)
        p_p, p_c = jnp.exp2(s_p - m), jnp.exp2(s_c - m)
        p_n, p_x = jnp.exp2(s_n - m), jnp.exp2(s_x - m)
        den = (jnp.sum(p_p, axis=0, keepdims=True) + jnp.sum(p_c, axis=0, keepdims=True)
               + jnp.sum(p_n, axis=0, keepdims=True) + jnp.sum(p_x, axis=0, keepdims=True)
               + jnp.exp2(sink - m))
        o = (_dot(vp, p_p.astype(BF16)) + _dot(vc, p_c.astype(BF16))
             + _dot(vn, p_n.astype(BF16)) + _dot(vx, p_x.astype(BF16)))
        o_ref[n * hd:(n + 1) * hd, :] = (o / den).astype(o_ref.dtype)


def _gqa(qg, kg, vg, sink_rows, lat, lc):
    bsz, _, s = qg.shape
    tq = GQA_TQ
    w = WINDOW
    gq = GQA_GROUP * GQA_HEAD_DIM
    wpt = tq // w
    last_w = s // w - 1
    return pl.pallas_call(
        functools.partial(_gqa_kernel, lat=lat),
        grid=(bsz, GQA_KV_HEADS, s // tq),
        in_specs=[
            pl.BlockSpec((None, gq, tq), lambda b, g, i: (b, g, i)),
            pl.BlockSpec((None, tq, V7X_LANES), lambda b, g, i: (b, i, g)),
            pl.BlockSpec((None, w, V7X_LANES), lambda b, g, i: (b, jnp.maximum(i * wpt - 1, 0), g)),
            pl.BlockSpec((None, w, V7X_LANES), lambda b, g, i: (b, jnp.minimum((i + 1) * wpt, last_w), g)),
            pl.BlockSpec((None, lc, V7X_LANES), lambda b, g, i: (b, lat // lc, g)),
            pl.BlockSpec((None, GQA_HEAD_DIM, tq), lambda b, g, i: (b, g, i)),
            pl.BlockSpec((None, GQA_HEAD_DIM, w), lambda b, g, i: (b, g, jnp.maximum(i * wpt - 1, 0))),
            pl.BlockSpec((None, GQA_HEAD_DIM, w), lambda b, g, i: (b, g, jnp.minimum((i + 1) * wpt, last_w))),
            pl.BlockSpec((None, GQA_HEAD_DIM, lc), lambda b, g, i: (b, g, lat // lc)),
            pl.BlockSpec((None, GQA_GROUP, tq), lambda b, g, i: (g, 0, 0)),
        ],
        out_specs=pl.BlockSpec((None, gq, tq), lambda b, g, i: (b, g, i)),
        out_shape=jax.ShapeDtypeStruct((bsz, GQA_HEADS * GQA_HEAD_DIM, s), BF16),
        compiler_params=_cparams(("parallel", "arbitrary", "arbitrary"), 32),
        name="gqa_window_attn",
    )(qg, kg, kg, kg, kg, vg, vg, vg, vg, sink_rows)


def _mla_update(s_ref, vt, m, l, acc):
    s = s_ref[...]
    m_new = jnp.maximum(m, jnp.max(s, axis=0, keepdims=True))
    alpha = jnp.exp2(m - m_new)
    p = jnp.exp2(s - m_new)
    l = alpha * l + jnp.sum(p, axis=0, keepdims=True)
    acc = alpha * acc + _dot(vt, p.astype(BF16))
    return m_new, l, acc


def _mla_kernel(*refs, nc, ctx_only, aliased):
    if aliased:
        q_ref, k_ref, v_ref, _, o_ref, s_a, s_b = refs
    else:
        q_ref, k_ref, v_ref, o_ref, s_a, s_b = refs
    q = q_ref[...]
    tq = q.shape[1]
    ck = KV_CHUNK
    carry = (jnp.full((1, tq), NEG_BIG, F32), jnp.zeros((1, tq), F32), jnp.zeros((MLA_V_DIM, tq), F32))

    def score(j, dst):
        dst[...] = _dot(k_ref[pl.ds(pl.multiple_of(j * ck, ck), ck), :], q)

    if ctx_only:
        s_a[0:KV_SUB, :] = _dot(k_ref[nc * ck - KV_SUB:nc * ck, :], q)
        carry = _mla_update(s_a.at[0:KV_SUB, :], v_ref[nc - 1][:, ck - KV_SUB:], *carry)
    else:
        score(0, s_a)

        def pair(i, c):
            score(2 * i + 1, s_b)
            c = _mla_update(s_a, v_ref[2 * i], *c)
            score(2 * i + 2, s_a)
            return _mla_update(s_b, v_ref[2 * i + 1], *c)

        npairs = (nc - 1) // 2
        if npairs > 0:
            carry = lax.fori_loop(0, npairs, pair, carry)
        if nc % 2 == 1:
            carry = _mla_update(s_a, v_ref[nc - 1], *carry)
        else:
            score(nc - 1, s_b)
            carry = _mla_update(s_a, v_ref[nc - 2], *carry)
            carry = _mla_update(s_b, v_ref[nc - 1], *carry)
    _, l, acc = carry
    o_ref[...] = (acc / l).astype(o_ref.dtype)


def _mla_call(qm, km, vm, prev_out, tq, q_tile0, n_q, ctx_only):
    bsz, _, s = qm.shape
    nc = vm.shape[1]
    aliased = prev_out is not None
    in_specs = [
        pl.BlockSpec((None, MLA_HEAD_PAD, tq), lambda b, h, i: (b, h, q_tile0 + i)),
        pl.BlockSpec((None, s, MLA_HEAD_PAD), lambda b, h, i: (b, 0, h)),
        pl.BlockSpec((None, nc, MLA_V_DIM, KV_CHUNK), lambda b, h, i: (b, 0, h, 0)),
    ]
    args = [qm, km, vm]
    if aliased:
        in_specs.append(pl.BlockSpec(memory_space=pl.ANY))
        args.append(prev_out)
    return pl.pallas_call(
        functools.partial(_mla_kernel, nc=nc, ctx_only=ctx_only, aliased=aliased),
        grid=(bsz, MLA_HEADS, n_q),
        in_specs=in_specs,
        out_specs=pl.BlockSpec((None, MLA_V_DIM, tq), lambda b, h, i: (b, h, q_tile0 + i)),
        out_shape=jax.ShapeDtypeStruct((bsz, MLA_HEADS * MLA_V_DIM, s), BF16),
        scratch_shapes=[pltpu.VMEM((KV_CHUNK, tq), F32), pltpu.VMEM((KV_CHUNK, tq), F32)],
        input_output_aliases={3: 0} if aliased else {},
        compiler_params=_cparams(("parallel", "arbitrary", "arbitrary"), 40),
        name="mla_attn_ctx" if aliased else "mla_attn",
    )(*args)


def _mla(qm, km, vm, lat, lc):
    tq = min(MLA_TQ, lat)
    out = _mla_call(qm, km, vm, None, tq, 0, lat // tq, False)
    assert lc == KV_SUB and lat % lc == 0
    return _mla_call(qm, km, vm, out, lc, lat // lc, 1, True)


def _merge_kernel(x_ref, mod_ref, yl_ref, yc_ref, yg_ref, ym_ref, gt_ref, wb_ref, wo_ref, o_ref, *, lat_tiles):
    d = D_MODEL
    i = pl.program_id(1)
    yh = jnp.where(i < lat_tiles, yl_ref[...], yc_ref[...])
    yg = yg_ref[...].astype(F32).T.astype(BF16)
    ym = ym_ref[...].astype(F32).T.astype(BF16)
    merged = (gt_ref[:, 0:d].astype(F32) * _dot(yh, wb_ref[0])
              + gt_ref[:, d:2 * d].astype(F32) * _dot(yg, wb_ref[1])
              + gt_ref[:, 2 * d:].astype(F32) * _dot(ym, wb_ref[2]))
    res = _dot(merged.astype(BF16), wo_ref[...])
    o_ref[...] = x_ref[...] + mod_ref[:, 2 * d:3 * d] * res


def _merge(xs, modsel, y_lat, y_ctx, yg, ym, gates, wb, wo, lat_tiles):
    bsz, s, d = xs.shape
    tm = ROW_TILE
    c = HY_WIDTH
    row = lambda b, i: (b, i, 0)
    col = lambda b, i: (b, 0, i)
    return pl.pallas_call(
        functools.partial(_merge_kernel, lat_tiles=lat_tiles),
        grid=(bsz, s // tm),
        in_specs=[
            pl.BlockSpec((None, tm, d), row),
            pl.BlockSpec((None, None, 1, 6 * d), lambda b, i: (b, jnp.where(i >= lat_tiles, 1, 0), 0, 0)),
            pl.BlockSpec((None, tm, c), lambda b, i: (b, jnp.minimum(i, lat_tiles - 1), 0)),
            pl.BlockSpec((None, tm, c), lambda b, i: (b, jnp.maximum(i - lat_tiles, 0), 0)),
            pl.BlockSpec((None, c, tm), col),
            pl.BlockSpec((None, c, tm), col),
            pl.BlockSpec((None, tm, 3 * d), row),
            _const_spec(wb.shape), _const_spec(wo.shape),
        ],
        out_specs=pl.BlockSpec((None, tm, d), row),
        out_shape=jax.ShapeDtypeStruct(xs.shape, F32),
        compiler_params=_cparams(("parallel", "arbitrary"), 40),
        name="branch_merge",
    )(xs, modsel, y_lat, y_ctx, yg, ym, gates, wb, wo)


def _mlp_kernel(x_ref, mod_ref, g_ref, w1_ref, w2_ref, o_ref):
    d = D_MODEL
    x = x_ref[...]
    xn = x * lax.rsqrt(jnp.mean(x * x, axis=-1, keepdims=True) + EPS) * g_ref[...]
    h = (xn * (1.0 + mod_ref[:, 4 * d:5 * d]) + mod_ref[:, 3 * d:4 * d]).astype(BF16)
    acc = jnp.zeros(x.shape, F32)
    for j in range(D_FF // FF_CHUNK):
        sl = slice(j * FF_CHUNK, (j + 1) * FF_CHUNK)
        a = jnp.maximum(_dot(h, w1_ref[:, sl]), 0.0)
        acc = acc + _dot((a * a).astype(BF16), w2_ref[sl, :])
    o_ref[...] = x + mod_ref[:, 5 * d:] * acc


def _mlp(xs, modsel, g, w1, w2, lat_tiles):
    bsz, s, d = xs.shape
    tm = ROW_TILE
    row = lambda b, i: (b, i, 0)
    return pl.pallas_call(
        _mlp_kernel,
        grid=(bsz, s // tm),
        in_specs=[
            pl.BlockSpec((None, tm, d), row),
            pl.BlockSpec((None, None, 1, 6 * d), lambda b, i: (b, jnp.where(i >= lat_tiles, 1, 0), 0, 0)),
            _const_spec(g.shape), _const_spec(w1.shape), _const_spec(w2.shape),
        ],
        out_specs=pl.BlockSpec((None, tm, d), row),
        out_shape=jax.ShapeDtypeStruct(xs.shape, F32),
        compiler_params=_cparams(("parallel", "arbitrary"), 56),
        name="relu2_mlp",
    )(xs, modsel, g, w1, w2)


def _rope_tables_t(rows, dim, lc):
    n_freq = dim // 4
    inv = ROPE_BASE ** (-jnp.arange(n_freq, dtype=F32) / n_freq)
    r = jnp.repeat(jnp.arange(rows, dtype=F32), GRID_W)
    col = jnp.tile(jnp.arange(GRID_W, dtype=F32), rows)
    ang = jnp.concatenate([r[:, None] * inv, col[:, None] * inv], axis=-1)
    cos_t = jnp.concatenate([jnp.cos(ang).T, jnp.ones((dim // 2, lc), F32)], axis=1)
    sin_t = jnp.concatenate([jnp.sin(ang).T, jnp.zeros((dim // 2, lc), F32)], axis=1)
    return cos_t, sin_t


def _lane_bcast(v):
    return jnp.broadcast_to(v.astype(F32)[:, None], (v.shape[0], ROW_TILE))


def _layer_weights(l, w_in, gqa_q_norm, gqa_k_norm, mla_q_a_norm, mla_kv_a_norm, w_q_b, w_kv_b, mla_q_norm, mla_k_norm):
    w = w_in[l]
    o = np.cumsum([0, 3 * HY_WIDTH, GQA_HEADS * GQA_HEAD_DIM, GQA_KV_HEADS * GQA_HEAD_DIM,
                   GQA_KV_HEADS * GQA_HEAD_DIM, MLA_Q_RANK, MLA_KV_RANK, MLA_ROPE_DIM, 3 * D_MODEL])
    wu = w[:, o[0]:o[1]].astype(BF16)
    wt = w[:, o[1]:o[7]].T.astype(BF16)
    wg = w[:, o[7]:o[8]].astype(BF16)
    wqb = w_q_b[l].T.astype(BF16)
    wkvb = w_kv_b[l].T.astype(BF16)
    return (wu, wg, wt, wqb, wkvb,
            _lane_bcast(gqa_q_norm[l]), _lane_bcast(gqa_k_norm[l]),
            _lane_bcast(mla_q_a_norm[l]), _lane_bcast(mla_kv_a_norm[l]),
            _lane_bcast(mla_q_norm[l]), _lane_bcast(mla_k_norm[l]))


def kernel(x, c, ctx, c_ctx, w_mod, b_mod, norm_mix_g, norm_mlp_g, w_in, hy_short_w, hy_f1_w, hy_f1_b, hy_f2_w, hy_f2_b, hy_sin_freq, hy_f3_w, hy_skip, gqa_q_norm, gqa_k_norm, gqa_sink, mla_q_a_norm, mla_kv_a_norm, w_q_b, w_kv_b, mla_q_norm, mla_k_norm, w_branch, w_out, w_mlp1, w_mlp2):
    bsz, lat, d = x.shape
    lc = ctx.shape[1]
    depth = w_mod.shape[0]
    s = lat + lc
    assert d == D_MODEL and lat % GRID_W == 0 and lat % ROW_TILE == 0 and lc == ROW_TILE and s % KV_CHUNK == 0
    lat_tiles = lat // ROW_TILE

    pad = (-(bsz + 1)) % 8
    cond = jnp.concatenate([c, c_ctx[None, :], jnp.zeros((pad, d), F32)], axis=0)
    mods = _mod_all(cond, w_mod, b_mod)

    tabs = _rope_tables_t(lat // GRID_W, GQA_HEAD_DIM, lc) + _rope_tables_t(lat // GRID_W, MLA_ROPE_DIM, lc)
    xs = jnp.concatenate([x, ctx], axis=1)

    for l in range(depth):
        ml = mods[l]
        modsel = jnp.stack([ml[:bsz], jnp.broadcast_to(ml[bsz][None], (bsz, 6 * d))], axis=1)[:, :, None, :]
        wts = _layer_weights(l, w_in, gqa_q_norm, gqa_k_norm, mla_q_a_norm, mla_kv_a_norm, w_q_b, w_kv_b,
                             mla_q_norm, mla_k_norm)
        u, gates, qg, kg, vg, qm, km, vm = _proj(xs, modsel, norm_mix_g[l][None, :], wts, tabs, lat_tiles)

        f1w = jnp.zeros((V7X_LANES, HY_FILTER_WIDTH), F32).at[:HY_EMB_DIM].set(hy_f1_w[l])
        fw = (f1w, hy_f1_b[l][None, :], hy_f2_w[l], hy_f2_b[l][None, :], hy_sin_freq[l], hy_f3_w[l])
        y_lat, y_ctx = _hyena(u, hy_short_w[l], fw, hy_skip[l], lat, lc)

        sink_rows = jnp.broadcast_to(gqa_sink[l].reshape(GQA_KV_HEADS, GQA_GROUP, 1),
                                     (GQA_KV_HEADS, GQA_GROUP, GQA_TQ)).astype(F32)
        yg = _gqa(qg, kg, vg, sink_rows, lat, lc)
        ym = _mla(qm, km, vm, lat, lc)

        xs = _merge(xs, modsel, y_lat, y_ctx, yg, ym, gates, w_branch[l].astype(BF16), w_out[l].astype(BF16),
                    lat_tiles)
        xs = _mlp(xs, modsel, norm_mlp_g[l][None, :], w_mlp1[l].astype(BF16), w_mlp2[l].astype(BF16), lat_tiles)
    return xs[:, :lat]
```

```python
import functools
import math

import numpy as np
import jax
import jax.numpy as jnp
from jax import lax
from jax.experimental import pallas as pl
from jax.experimental.pallas import tpu as pltpu

D_MODEL = 1024
GRID_W = 64
HY_WIDTH = 512
HY_EMB_DIM = 33
HY_BANDS = (HY_EMB_DIM - 1) // 2
HY_FILTER_WIDTH = 64
HY_DECAY_TARGET = 1e-2
HY_FAST_DECAY = 0.3
HY_SLOW_DECAY = 1.5
GQA_HEADS = 8
GQA_KV_HEADS = 2
GQA_GROUP = GQA_HEADS // GQA_KV_HEADS
GQA_HEAD_DIM = 64
GQA_SCALE = GQA_HEAD_DIM ** -0.5
WINDOW = 128
MLA_HEADS = 8
MLA_Q_RANK = 384
MLA_KV_RANK = 256
MLA_NOPE_DIM = 64
MLA_ROPE_DIM = 32
MLA_V_DIM = 64
MLA_QK_DIM = MLA_NOPE_DIM + MLA_ROPE_DIM
MLA_SCALE = MLA_QK_DIM ** -0.5
D_FF = 4 * D_MODEL
ROPE_BASE = 10000.0
EPS = 1e-6
LOG2E = 1.4426950408889634
NEG_BIG = -1e30

V7X_LANES = 128
V7X_VMEM_BYTES = 64 * 1024 * 1024

ROW_TILE = 256
MLA_HEAD_PAD = 128
V_ROWS = 80
KV_CHUNK = 3 * ROW_TILE
KV_SUB = 256
MLA_TQ = 512
GQA_TQ = 256
FF_CHUNK = 1024
FFT_N2 = 128
FFT_NB = 16
FFT_KB = 4

F32 = jnp.float32
BF16 = jnp.bfloat16


def _cparams(sem, vmem_mb, flags=None):
    return pltpu.CompilerParams(dimension_semantics=sem, vmem_limit_bytes=vmem_mb * 1024 * 1024, flags=flags)


def _dot(a, b):
    return jnp.dot(a, b, preferred_element_type=F32)


def _dot_hi(a, b):
    return jnp.dot(a, b, preferred_element_type=F32, precision=lax.Precision.HIGHEST)


def _ones_row_block(width):
    r = lax.broadcasted_iota(jnp.int32, (V_ROWS - MLA_V_DIM, width), 0)
    return jnp.where(r == 0, 1.0, 0.0).astype(BF16)


def _const_spec(shape):
    nd = len(shape)
    return pl.BlockSpec(shape, lambda *_: (0,) * nd)


def _mod_kernel(c_ref, w_ref, b_ref, o_ref):
    c = c_ref[...]
    s = c * jax.nn.sigmoid(c)
    o_ref[...] = _dot_hi(s, w_ref[...]) + b_ref[...]


def _mod_all(cond, w_mod, b_mod):
    depth, d, n = w_mod.shape
    rows = cond.shape[0]
    tn = 1536
    return pl.pallas_call(
        _mod_kernel,
        grid=(depth, n // tn),
        in_specs=[
            pl.BlockSpec((rows, d), lambda l, j: (0, 0)),
            pl.BlockSpec((None, d, tn), lambda l, j: (l, 0, j)),
            pl.BlockSpec((None, 1, tn), lambda l, j: (l, 0, j)),
        ],
        out_specs=pl.BlockSpec((None, rows, tn), lambda l, j: (l, 0, j)),
        out_shape=jax.ShapeDtypeStruct((depth, rows, n), F32),
        compiler_params=_cparams(("arbitrary", "arbitrary"), 40),
        name="adaln_mod",
    )(cond, w_mod, b_mod.reshape(depth, 1, n))


def _rms_rows(x, n):
    return lax.rsqrt(jnp.sum(x * x, axis=0, keepdims=True) * (1.0 / n) + EPS)


def _rope_rows(x1, x2, cs, sn):
    return x1 * cs - x2 * sn, x1 * sn + x2 * cs


def _proj_kernel(x_ref, mod_ref, g_ref, wu_ref, wg_ref, wt_ref, wqb_ref, wkvb_ref,
                 gq_ref, gk_ref, gqa_ref, gkva_ref, gmq_ref, gmk_ref,
                 cg_ref, sg_ref, cm_ref, sm_ref,
                 u_ref, gate_ref, qg_ref, kg_ref, vg_ref, qm_ref, km_ref, vm_ref):
    d = D_MODEL
    x = x_ref[...]
    tm = x.shape[0]
    shift = mod_ref[:, 0:d]
    scale = mod_ref[:, d:2 * d]
    xn = x * lax.rsqrt(jnp.mean(x * x, axis=-1, keepdims=True) + EPS) * g_ref[...]
    h = xn * (1.0 + scale) + shift
    hb = h.astype(BF16)

    u_ref[...] = _dot(hb, wu_ref[...]).astype(BF16)
    gate_ref[...] = jax.nn.sigmoid(_dot(hb, wg_ref[...])).astype(BF16)

    ht = h.T.astype(BF16)
    t = _dot(wt_ref[...], ht)
    o_q, o_k, o_v = 0, 512, 640
    o_cq, o_ckv, o_kr = 768, 768 + MLA_Q_RANK, 768 + MLA_Q_RANK + MLA_KV_RANK

    cg, sg = cg_ref[...], sg_ref[...]
    cm, sm = cm_ref[...], sm_ref[...]
    hd, hh = GQA_HEAD_DIM, GQA_HEAD_DIM // 2

    gq = gq_ref[...]
    for n in range(GQA_HEADS):
        xh = t[o_q + n * hd:o_q + (n + 1) * hd]
        xh = xh * _rms_rows(xh, hd) * gq
        a, b = _rope_rows(xh[:hh], xh[hh:], cg, sg)
        qg_ref[n * hd:n * hd + hh, :] = (a * (GQA_SCALE * LOG2E)).astype(BF16)
        qg_ref[n * hd + hh:(n + 1) * hd, :] = (b * (GQA_SCALE * LOG2E)).astype(BF16)

    gk = gk_ref[...]
    zpad = jnp.zeros((V7X_LANES - hd, tm), F32)
    parts = []
    for n in range(GQA_KV_HEADS):
        xh = t[o_k + n * hd:o_k + (n + 1) * hd]
        xh = xh * _rms_rows(xh, hd) * gk
        a, b = _rope_rows(xh[:hh], xh[hh:], cg, sg)
        parts += [a, b, zpad]
    kg_ref[...] = jnp.concatenate(parts, axis=0).T.astype(BF16)
    ones_rows = _ones_row_block(tm)
    for n in range(GQA_KV_HEADS):
        vg_ref[n * V_ROWS:n * V_ROWS + hd, :] = t[o_v + n * hd:o_v + (n + 1) * hd].astype(BF16)
        vg_ref[n * V_ROWS + hd:(n + 1) * V_ROWS, :] = ones_rows

    cq = t[o_cq:o_cq + MLA_Q_RANK]
    cqn = (cq * _rms_rows(cq, MLA_Q_RANK) * gqa_ref[...]).astype(BF16)
    qm = _dot(wqb_ref[...], cqn)
    gmq = gmq_ref[...]
    nd, rh = MLA_NOPE_DIM, MLA_ROPE_DIM // 2
    qpad = jnp.zeros((MLA_HEAD_PAD - MLA_QK_DIM, tm), BF16)
    for n in range(MLA_HEADS):
        xh = qm[n * MLA_QK_DIM:(n + 1) * MLA_QK_DIM]
        xh = xh * _rms_rows(xh, MLA_QK_DIM) * gmq * (MLA_SCALE * LOG2E)
        a, b = _rope_rows(xh[nd:nd + rh], xh[nd + rh:], cm, sm)
        base = n * MLA_HEAD_PAD
        qm_ref[base:base + nd, :] = xh[:nd].astype(BF16)
        qm_ref[base + nd:base + nd + rh, :] = a.astype(BF16)
        qm_ref[base + nd + rh:base + MLA_QK_DIM, :] = b.astype(BF16)
        qm_ref[base + MLA_QK_DIM:base + MLA_HEAD_PAD, :] = qpad

    ckv = t[o_ckv:o_ckv + MLA_KV_RANK]
    ckvn = (ckv * _rms_rows(ckv, MLA_KV_RANK) * gkva_ref[...]).astype(BF16)
    kv = _dot(wkvb_ref[...], ckvn)
    kr = t[o_kr:o_kr + MLA_ROPE_DIM]
    kr_ss = jnp.sum(kr * kr, axis=0, keepdims=True)
    gmk = gmk_ref[...]
    kzero = jnp.zeros((MLA_HEAD_PAD - MLA_QK_DIM, tm), F32)
    ones_rows = _ones_row_block(tm)
    parts = []
    for n in range(MLA_HEADS):
        kn = kv[n * 128:n * 128 + nd]
        vm_ref[n * V_ROWS:n * V_ROWS + MLA_V_DIM, :] = kv[n * 128 + nd:(n + 1) * 128].astype(BF16)
        vm_ref[n * V_ROWS + MLA_V_DIM:(n + 1) * V_ROWS, :] = ones_rows
        rs = lax.rsqrt((jnp.sum(kn * kn, axis=0, keepdims=True) + kr_ss) * (1.0 / MLA_QK_DIM) + EPS)
        krn = kr * rs * gmk[nd:]
        a, b = _rope_rows(krn[:rh], krn[rh:], cm, sm)
        parts += [kn * rs * gmk[:nd], a, b, kzero]
    km_ref[...] = jnp.concatenate(parts, axis=0).T.astype(BF16)


def _proj(xs, modsel, g, wts, tabs, lat_tiles):
    bsz, s, d = xs.shape
    tm = ROW_TILE
    nt = s // tm
    ncs = KV_CHUNK // tm
    (wu, wg, wt, wqb, wkvb, gq, gk, gqa, gkva, gmq, gmk) = wts
    cg, sg, cm, sm = tabs
    row = lambda b, i: (b, i, 0)
    col = lambda b, i: (b, 0, i)
    tab = lambda b, i: (0, i)
    in_specs = [
        pl.BlockSpec((None, tm, d), row),
        pl.BlockSpec((None, None, 1, 6 * d), lambda b, i: (b, jnp.where(i >= lat_tiles, 1, 0), 0, 0)),
        _const_spec(g.shape), _const_spec(wu.shape), _const_spec(wg.shape), _const_spec(wt.shape),
        _const_spec(wqb.shape), _const_spec(wkvb.shape),
        _const_spec(gq.shape), _const_spec(gk.shape), _const_spec(gqa.shape), _const_spec(gkva.shape),
        _const_spec(gmq.shape), _const_spec(gmk.shape),
        pl.BlockSpec((cg.shape[0], tm), tab), pl.BlockSpec((sg.shape[0], tm), tab),
        pl.BlockSpec((cm.shape[0], tm), tab), pl.BlockSpec((sm.shape[0], tm), tab),
    ]
    out_shape = [
        jax.ShapeDtypeStruct((bsz, s, 3 * HY_WIDTH), BF16),
        jax.ShapeDtypeStruct((bsz, s, 3 * d), BF16),
        jax.ShapeDtypeStruct((bsz, GQA_HEADS * GQA_HEAD_DIM, s), BF16),
        jax.ShapeDtypeStruct((bsz, s, GQA_KV_HEADS * V7X_LANES), BF16),
        jax.ShapeDtypeStruct((bsz, GQA_KV_HEADS * V_ROWS, s), BF16),
        jax.ShapeDtypeStruct((bsz, MLA_HEADS * MLA_HEAD_PAD, s), BF16),
        jax.ShapeDtypeStruct((bsz, s, MLA_HEADS * MLA_HEAD_PAD), BF16),
        jax.ShapeDtypeStruct((bsz, s // KV_CHUNK, MLA_HEADS * V_ROWS, KV_CHUNK), BF16),
    ]
    out_specs = [
        pl.BlockSpec((None, tm, 3 * HY_WIDTH), row),
        pl.BlockSpec((None, tm, 3 * d), row),
        pl.BlockSpec((None, GQA_HEADS * GQA_HEAD_DIM, tm), col),
        pl.BlockSpec((None, tm, GQA_KV_HEADS * V7X_LANES), row),
        pl.BlockSpec((None, GQA_KV_HEADS * V_ROWS, tm), col),
        pl.BlockSpec((None, MLA_HEADS * MLA_HEAD_PAD, tm), col),
        pl.BlockSpec((None, tm, MLA_HEADS * MLA_HEAD_PAD), row),
        pl.BlockSpec((None, None, MLA_HEADS * V_ROWS, tm), lambda b, i: (b, i // ncs, 0, i % ncs)),
    ]
    return pl.pallas_call(
        _proj_kernel,
        grid=(bsz, nt),
        in_specs=in_specs,
        out_specs=out_specs,
        out_shape=out_shape,
        compiler_params=_cparams(("parallel", "arbitrary"), 56),
        name="in_proj",
    )(xs, modsel, g, wu, wg, wt, wqb, wkvb, gq, gk, gqa, gkva, gmq, gmk, cg, sg, cm, sm)


def _short_conv_kernel(u_ref, up_ref, un_ref, w_ref, x0_ref, z_ref, *, n_tiles):
    i = pl.program_id(1)
    u = u_ref[...].astype(F32)
    tm = u.shape[0]
    prev = jnp.where(i > 0, up_ref[7:8, :].astype(F32), 0.0)
    nxt = jnp.where(i < n_tiles - 1, un_ref[0:1, :].astype(F32), 0.0)
    ridx = lax.broadcasted_iota(jnp.int32, u.shape, 0)
    up = jnp.where(ridx == 0, prev, pltpu.roll(u, 1, axis=0))
    dn = jnp.where(ridx == tm - 1, nxt, pltpu.roll(u, tm - 1, axis=0))
    uc = up * w_ref[0:1, :] + u * w_ref[1:2, :] + dn * w_ref[2:3, :]
    c = HY_WIDTH
    x0_ref[...] = uc[:, :c].astype(BF16)
    z_ref[...] = (uc[:, c:2 * c] * uc[:, 2 * c:]).astype(BF16)


def _short_conv(u, short_w, row0, rows):
    bsz, s, c3 = u.shape
    tm = ROW_TILE
    nt = rows // tm
    t0 = row0 // tm
    r8 = tm // 8
    last8 = s // 8 - 1
    return pl.pallas_call(
        functools.partial(_short_conv_kernel, n_tiles=nt),
        grid=(bsz, nt),
        in_specs=[
            pl.BlockSpec((None, tm, c3), lambda b, i: (b, t0 + i, 0)),
            pl.BlockSpec((None, 8, c3), lambda b, i: (b, jnp.maximum((t0 + i) * r8 - 1, 0), 0)),
            pl.BlockSpec((None, 8, c3), lambda b, i: (b, jnp.minimum((t0 + i + 1) * r8, last8), 0)),
            _const_spec(short_w.shape),
        ],
        out_specs=[pl.BlockSpec((None, tm, HY_WIDTH), lambda b, i: (b, i, 0))] * 2,
        out_shape=[jax.ShapeDtypeStruct((bsz, rows, HY_WIDTH), BF16)] * 2,
        compiler_params=_cparams(("parallel", "arbitrary"), 32),
        name="hyena_short_conv",
    )(u, u, u, short_w)


def _filter_kernel(zf_ref, w1_ref, b1_ref, w2_ref, b2_ref, fr_ref, w3_ref, dl_ref, h_ref, ss_ref):
    i = pl.program_id(0)
    zf = zf_ref[...]
    tl = zf.shape[0]
    h = jnp.sin(fr_ref[0:1, :] * (_dot_hi(zf, w1_ref[...]) + b1_ref[...]))
    h = jnp.sin(fr_ref[1:2, :] * (_dot_hi(h, w2_ref[...]) + b2_ref[...]))
    h = _dot_hi(h, w3_ref[...])
    decay = jnp.exp(-zf[:, 0:1] * dl_ref[...])
    c = HY_WIDTH
    hf = h[:, :c] * decay
    ridx = lax.broadcasted_iota(jnp.int32, (tl, c), 0) + i * tl
    hb = jnp.where(ridx == 0, 0.0, h[:, c:] * decay)
    h_ref[0] = hf.astype(BF16)
    h_ref[1] = hb.astype(BF16)
    ss = jnp.sum(hf * hf + hb * hb, axis=0, keepdims=True)

    @pl.when(i == 0)
    def _():
        ss_ref[...] = ss

    @pl.when(i > 0)
    def _():
        ss_ref[...] += ss


def _filter(zfeat, fw, deltas):
    length = zfeat.shape[0]
    tl = min(length, 1024)
    w1, b1, w2, b2, fr, w3 = fw
    return pl.pallas_call(
        _filter_kernel,
        grid=(length // tl,),
        in_specs=[pl.BlockSpec((tl, zfeat.shape[1]), lambda i: (i, 0))]
        + [_const_spec(a.shape) for a in (w1, b1, w2, b2, fr, w3, deltas)],
        out_specs=[pl.BlockSpec((2, tl, HY_WIDTH), lambda i: (0, i, 0)),
                   pl.BlockSpec((1, HY_WIDTH), lambda i: (0, 0))],
        out_shape=[jax.ShapeDtypeStruct((2, length, HY_WIDTH), BF16),
                   jax.ShapeDtypeStruct((1, HY_WIDTH), F32)],
        compiler_params=_cparams(("arbitrary",), 40),
        name="hyena_filter",
    )(zfeat, w1, b1, w2, b2, fr, w3, deltas)


def _fft_a_kernel(g_ref, x_ref, o_ref, *, nb, cw):
    for j in range(nb):
        sl = slice(j * cw, (j + 1) * cw)
        o_ref[:, sl] = _dot(g_ref[j], x_ref[:, sl]).astype(o_ref.dtype)


def _fft_a(x2d, gmat):
    bx, k1, w = x2d.shape
    n2, two_n1, _ = gmat.shape
    cw = w // n2
    nb = FFT_NB
    return pl.pallas_call(
        functools.partial(_fft_a_kernel, nb=nb, cw=cw),
        grid=(bx, n2 // nb),
        in_specs=[pl.BlockSpec((nb, two_n1, k1), lambda b, j: (j, 0, 0)),
                  pl.BlockSpec((None, k1, nb * cw), lambda b, j: (b, 0, j))],
        out_specs=pl.BlockSpec((None, two_n1, nb * cw), lambda b, j: (b, 0, j)),
        out_shape=jax.ShapeDtypeStruct((bx, two_n1, w), BF16),
        compiler_params=_cparams(("parallel", "arbitrary"), 40),
        name="fft_stage_a",
    )(gmat, x2d)


def _fft_filter_b_kernel(fb_ref, a_ref, ss_ref, kf_ref, *, inv_n):
    n1 = a_ref.shape[2]
    xf = _dot(fb_ref[...], jnp.concatenate([a_ref[0, 0], a_ref[0, 1]], axis=0))
    xb = _dot(fb_ref[...], jnp.concatenate([a_ref[1, 0], a_ref[1, 1]], axis=0))
    rs = lax.rsqrt(ss_ref[...] + EPS) * inv_n
    kf_ref[0] = (xf[:n1] + xb[:n1]) * rs
    kf_ref[1] = (xf[n1:] - xb[n1:]) * rs


def _fft_filter_b(a5, fb, ssq, inv_n):
    _, _, n1, n2, c = a5.shape
    return pl.pallas_call(
        functools.partial(_fft_filter_b_kernel, inv_n=inv_n),
        grid=(n1,),
        in_specs=[_const_spec(fb.shape),
                  pl.BlockSpec((2, 2, None, n2, c), lambda k: (0, 0, k, 0, 0)),
                  _const_spec(ssq.shape)],
        out_specs=pl.BlockSpec((2, None, n2, c), lambda k: (0, k, 0, 0)),
        out_shape=jax.ShapeDtypeStruct((2, n1, n2, c), F32),
        compiler_params=_cparams(("arbitrary",), 32),
        name="fft_filter_stage_b",
    )(fb, a5, ssq)


def _fft_b_kernel(fb_ref, fbi_ref, a_ref, kf_ref, o_ref):
    n = a_ref.shape[2]
    for r in range(a_ref.shape[1]):
        x = _dot(fb_ref[...], jnp.concatenate([a_ref[0, r], a_ref[1, r]], axis=0))
        xr, xi = x[:n], x[n:]
        kr, ki = kf_ref[0, r], kf_ref[1, r]
        y = jnp.concatenate([xr * kr - xi * ki, xr * ki + xi * kr], axis=0).astype(BF16)
        c = _dot(fbi_ref[...], y)
        o_ref[0, r] = c[:n].astype(o_ref.dtype)
        o_ref[1, r] = c[n:].astype(o_ref.dtype)


def _fft_b(a5, kf, fb, fbi):
    bsz, _, n1, n2, c = a5.shape
    kb = min(FFT_KB, n1)
    return pl.pallas_call(
        _fft_b_kernel,
        grid=(n1 // kb, bsz),
        in_specs=[_const_spec(fb.shape), _const_spec(fbi.shape),
                  pl.BlockSpec((None, 2, kb, n2, c), lambda k, b: (b, 0, k, 0, 0)),
                  pl.BlockSpec((2, kb, n2, c), lambda k, b: (0, k, 0, 0))],
        out_specs=pl.BlockSpec((None, 2, kb, n2, c), lambda k, b: (b, 0, k, 0, 0)),
        out_shape=jax.ShapeDtypeStruct(a5.shape, BF16),
        compiler_params=_cparams(("arbitrary", "arbitrary"), 32),
        name="fft_stage_b",
    )(fb, fbi, a5, kf)


def _fft_c_kernel(h_ref, c_ref, z_ref, x0_ref, sk_ref, o_ref, *, nb, cw):
    sk = sk_ref[...]
    for j in range(nb):
        sl = slice(j * cw, (j + 1) * cw)
        y = _dot(h_ref[j], c_ref[:, sl])
        z = z_ref[:, sl].astype(F32)
        o_ref[:, sl] = (x0_ref[:, sl].astype(F32) * (y + z * sk)).astype(o_ref.dtype)


def _fft_c(c2d, hmat, z2d, x02d, skip):
    bsz, two_n1, w = c2d.shape
    n2, k1, _ = hmat.shape
    cw = w // n2
    nb = FFT_NB
    blk = lambda b, j: (b, 0, j)
    return pl.pallas_call(
        functools.partial(_fft_c_kernel, nb=nb, cw=cw),
        grid=(bsz, n2 // nb),
        in_specs=[pl.BlockSpec((nb, k1, two_n1), lambda b, j: (j, 0, 0)),
                  pl.BlockSpec((None, two_n1, nb * cw), blk),
                  pl.BlockSpec((None, k1, nb * cw), blk),
                  pl.BlockSpec((None, k1, nb * cw), blk),
                  _const_spec(skip.shape)],
        out_specs=pl.BlockSpec((None, k1, nb * cw), blk),
        out_shape=jax.ShapeDtypeStruct((bsz, k1, w), BF16),
        compiler_params=_cparams(("parallel", "arbitrary"), 40),
        name="fft_stage_c",
    )(hmat, c2d, z2d, x02d, skip)


def _ctx_conv_kernel(fc_ref, fci_ref, h_ref, ss_ref, z_ref, x0_ref, sk_ref, o_ref):
    n = fc_ref.shape[0] // 2
    fc = fc_ref[...]
    kf = _dot(fc, h_ref[0])
    kb = _dot(fc, h_ref[1])
    rs = lax.rsqrt(ss_ref[...] + EPS)
    kr = (kf[:n] + kb[:n]) * rs
    ki = (kf[n:] - kb[n:]) * rs
    zb = z_ref[...]
    x = _dot(fc, zb)
    xr, xi = x[:n], x[n:]
    y = jnp.concatenate([xr * kr - xi * ki, xr * ki + xi * kr], axis=0).astype(BF16)
    y = _dot(fci_ref[...], y)
    z = zb.astype(F32)
    o_ref[...] = (x0_ref[...].astype(F32) * (y + z * sk_ref[...])).astype(o_ref.dtype)


def _ctx_conv(fc, fci, hfb, ssq, z, x0, skip):
    bsz, lc, c = z.shape
    blk = pl.BlockSpec((None, lc, c), lambda b: (b, 0, 0))
    return pl.pallas_call(
        _ctx_conv_kernel,
        grid=(bsz,),
        in_specs=[_const_spec(fc.shape), _const_spec(fci.shape), _const_spec(hfb.shape),
                  _const_spec(ssq.shape), blk, blk, _const_spec(skip.shape)],
        out_specs=blk,
        out_shape=jax.ShapeDtypeStruct((bsz, lc, c), BF16),
        compiler_params=_cparams(("arbitrary",), 32),
        name="hyena_ctx_conv",
    )(fc, fci, hfb, ssq, z, x0, skip)


@functools.lru_cache(maxsize=None)
def _fft_tables(length):
    n = 2 * length
    n2 = FFT_N2
    n1 = n // n2
    k1 = length // n2
    kk = np.arange(n1)[:, None]
    g = np.empty((n2, 2 * n1, k1), np.float64)
    h = np.empty((n2, k1, 2 * n1), np.float64)
    nn = np.arange(k1)[None, :]
    for j in range(n2):
        ang = 2.0 * np.pi * (((n2 * nn * kk) % n) + (j * kk) % n) / n
        g[j, :n1] = np.cos(ang)
        g[j, n1:] = -np.sin(ang)
        h[j, :, :n1] = np.cos(ang).T
        h[j, :, n1:] = -np.sin(ang).T
    a = np.arange(n2)
    ph = 2.0 * np.pi * ((a[:, None] * a[None, :]) % n2) / n2
    c, s = np.cos(ph), np.sin(ph)
    fb = np.block([[c, s], [-s, c]])
    fbi = np.block([[c, -s], [s, c]])
    return (jnp.asarray(g, BF16), jnp.asarray(h, BF16), jnp.asarray(fb, BF16), jnp.asarray(fbi, BF16), n1, k1)


@functools.lru_cache(maxsize=None)
def _dft_tables(length):
    n = 2 * length
    k = np.arange(n)[:, None]
    t = np.arange(length)[None, :]
    ang = 2.0 * np.pi * ((k * t) % n) / n
    fc = np.concatenate([np.cos(ang), -np.sin(ang)], axis=0)
    fci = np.concatenate([np.cos(ang).T, -np.sin(ang).T], axis=1) / n
    return jnp.asarray(fc, BF16), jnp.asarray(fci, BF16)


@functools.lru_cache(maxsize=None)
def _filter_features(length):
    t = np.linspace(0.0, 1.0, length, dtype=np.float32)[:, None]
    w = (2.0 * math.pi * np.arange(length, dtype=np.float32)[:, None] / length).astype(np.float32)
    f = np.linspace(1e-4, HY_BANDS - 1, HY_BANDS, dtype=np.float32)[None, :]
    z = np.concatenate([t, np.cos(f * w), -np.sin(f * w)], axis=-1).astype(np.float32)
    zp = np.zeros((length, V7X_LANES), np.float32)
    zp[:, :HY_EMB_DIM] = z
    return jnp.asarray(zp)


def _hyena_deltas():
    max_decay = math.log(HY_DECAY_TARGET) / HY_FAST_DECAY
    min_decay = math.log(HY_DECAY_TARGET) / HY_SLOW_DECAY
    return jnp.abs(jnp.linspace(min_decay, max_decay, HY_WIDTH, dtype=F32))[None, :]


def _hyena(u, short_w, fw, skip, lat, lc):
    bsz = u.shape[0]
    c = HY_WIDTH
    deltas = _hyena_deltas()
    skip2 = skip.reshape(1, c)
    x0, z = _short_conv(u, short_w, 0, lat)
    gmat, hmat, fb, fbi, n1, k1 = _fft_tables(lat)
    n2 = FFT_N2
    hfb, ssq = _filter(_filter_features(lat), fw, deltas)
    fa = _fft_a(hfb.reshape(2, k1, n2 * c), gmat)
    kf = _fft_filter_b(fa.reshape(2, 2, n1, n2, c), fb, ssq, 1.0 / (2 * lat))
    za = _fft_a(z.reshape(bsz, k1, n2 * c), gmat)
    zc = _fft_b(za.reshape(bsz, 2, n1, n2, c), kf, fb, fbi)
    y = _fft_c(zc.reshape(bsz, 2 * n1, n2 * c), hmat, z.reshape(bsz, k1, n2 * c),
               x0.reshape(bsz, k1, n2 * c), skip2)
    y_lat = y.reshape(bsz, lat, c)
    x0c, zc_ = _short_conv(u, short_w, lat, lc)
    hfb_c, ssq_c = _filter(_filter_features(lc), fw, deltas)
    fc, fci = _dft_tables(lc)
    y_ctx = _ctx_conv(fc, fci, hfb_c, ssq_c, zc_, x0c, skip2)
    return y_lat, y_ctx


def _gqa_kernel(q_ref, kc_ref, kp_ref, kn_ref, kx_ref, vc_ref, vp_ref, vn_ref, vx_ref, sink_ref, o_ref,
                s_a, s_b, *, lat):
    i = pl.program_id(2)
    tq = q_ref.shape[1]
    hd = GQA_HEAD_DIM
    w = WINDOW
    lc = kx_ref.shape[0]
    nwin = tq + 2 * w
    k_all = jnp.concatenate([kp_ref[...], kc_ref[...], kn_ref[...], kx_ref[...]], axis=0)
    v_all = jnp.concatenate([vp_ref[...], vc_ref[...], vn_ref[...], vx_ref[...]], axis=1)
    q_pos = i * tq + lax.broadcasted_iota(jnp.int32, (1, tq), 1)
    k_pos = i * tq - w + lax.broadcasted_iota(jnp.int32, (nwin, 1), 0)
    bias = (jnp.where(jnp.abs(k_pos - q_pos) <= w, 0.0, NEG_BIG)
            + jnp.where((k_pos >= 0) & (k_pos < lat), 0.0, NEG_BIG)
            + jnp.where(q_pos < lat, 0.0, NEG_BIG))
    bias = jnp.concatenate([bias, jnp.zeros((lc, tq), F32)], axis=0)
    bias2 = jnp.concatenate([bias, bias], axis=1)
    zq = jnp.zeros((V7X_LANES - hd, 2 * tq), BF16)
    staged = []
    for h0, s_sc in ((0, s_a), (2, s_b)):
        q2 = jnp.concatenate([q_ref[h0 * hd:(h0 + 1) * hd, :], q_ref[(h0 + 1) * hd:(h0 + 2) * hd, :]], axis=1)
        s = _dot(k_all, jnp.concatenate([q2, zq], axis=0)) + bias2
        s_sc[...] = s
        staged.append(jnp.max(s, axis=0, keepdims=True))
    for (h0, s_sc), s_max in zip(((0, s_a), (2, s_b)), staged):
        sink = jnp.concatenate([sink_ref[h0:h0 + 1, :], sink_ref[h0 + 1:h0 + 2, :]], axis=1) * LOG2E
        m = jnp.maximum(s_max, sink)
        p = jnp.exp2(s_sc[...] - m).astype(BF16)
        pv = _dot(v_all, p)
        o = pv[:hd] / (pv[hd:hd + 1] + jnp.exp2(sink - m))
        o_ref[h0 * hd:(h0 + 1) * hd, :] = o[:, :tq].astype(o_ref.dtype)
        o_ref[(h0 + 1) * hd:(h0 + 2) * hd, :] = o[:, tq:].astype(o_ref.dtype)


def _gqa(qg, kg, vg, sink_rows, lat, lc):
    bsz, _, s = qg.shape
    tq = GQA_TQ
    w = WINDOW
    gq = GQA_GROUP * GQA_HEAD_DIM
    wpt = tq // w
    last_w = s // w - 1
    return pl.pallas_call(
        functools.partial(_gqa_kernel, lat=lat),
        grid=(bsz, GQA_KV_HEADS, s // tq),
        in_specs=[
            pl.BlockSpec((None, gq, tq), lambda b, g, i: (b, g, i)),
            pl.BlockSpec((None, tq, V7X_LANES), lambda b, g, i: (b, i, g)),
            pl.BlockSpec((None, w, V7X_LANES), lambda b, g, i: (b, jnp.maximum(i * wpt - 1, 0), g)),
            pl.BlockSpec((None, w, V7X_LANES), lambda b, g, i: (b, jnp.minimum((i + 1) * wpt, last_w), g)),
            pl.BlockSpec((None, lc, V7X_LANES), lambda b, g, i: (b, lat // lc, g)),
            pl.BlockSpec((None, V_ROWS, tq), lambda b, g, i: (b, g, i)),
            pl.BlockSpec((None, V_ROWS, w), lambda b, g, i: (b, g, jnp.maximum(i * wpt - 1, 0))),
            pl.BlockSpec((None, V_ROWS, w), lambda b, g, i: (b, g, jnp.minimum((i + 1) * wpt, last_w))),
            pl.BlockSpec((None, V_ROWS, lc), lambda b, g, i: (b, g, lat // lc)),
            pl.BlockSpec((None, GQA_GROUP, tq), lambda b, g, i: (g, 0, 0)),
        ],
        out_specs=pl.BlockSpec((None, gq, tq), lambda b, g, i: (b, g, i)),
        out_shape=jax.ShapeDtypeStruct((bsz, GQA_HEADS * GQA_HEAD_DIM, s), BF16),
        scratch_shapes=[pltpu.VMEM((tq + 2 * w + lc, 2 * tq), F32)] * 2,
        compiler_params=_cparams(("parallel", "arbitrary", "arbitrary"), 32),
        name="gqa_window_attn",
    )(qg, kg, kg, kg, kg, vg, vg, vg, vg, sink_rows)


def _mla_update(s_ref, s_max, vt, m, acc):
    m_new = jnp.maximum(m, s_max)
    alpha = jnp.exp2(m - m_new)
    p = jnp.exp2(s_ref[...] - m_new).astype(BF16)
    acc = alpha * acc + _dot(vt, p)
    return m_new, acc


def _mla_kernel(*refs, nc, ctx_only, aliased):
    if aliased:
        q_ref, k_ref, v_ref, _, o_ref, s_a, s_b, s_c = refs
    else:
        q_ref, k_ref, v_ref, o_ref, s_a, s_b, s_c = refs
    q = q_ref[...]
    tq = q.shape[1]
    ck = KV_CHUNK
    carry = (jnp.full((1, tq), NEG_BIG, F32), jnp.zeros((V_ROWS, tq), F32))

    def score(k, dst):
        s = _dot(k, q)
        dst[...] = s
        return jnp.max(s, axis=0, keepdims=True)

    def score_chunk(j, dst):
        return score(k_ref[pl.ds(pl.multiple_of(j * ck, ck), ck), :], dst)

    if ctx_only:
        s_x = s_a.at[0:KV_SUB, :]
        mx = score(k_ref[nc * ck - KV_SUB:nc * ck, :], s_x)
        m, acc = _mla_update(s_x, mx, v_ref[nc - 1][:, ck - KV_SUB:], *carry)
    elif nc == 1:
        m, acc = _mla_update(s_a, score_chunk(0, s_a), v_ref[0], *carry)
    else:
        m, acc = carry
        mx_a = score_chunk(0, s_a)
        mx_b = score_chunk(1, s_b)

        def triple(i, c):
            m, acc, mx_a, mx_b = c
            j = 3 * i
            mx_c = score_chunk(j + 2, s_c)
            m, acc = _mla_update(s_a, mx_a, v_ref[j], m, acc)
            mx_a = score_chunk(j + 3, s_a)
            m, acc = _mla_update(s_b, mx_b, v_ref[j + 1], m, acc)
            mx_b = score_chunk(j + 4, s_b)
            m, acc = _mla_update(s_c, mx_c, v_ref[j + 2], m, acc)
            return m, acc, mx_a, mx_b

        nt = (nc - 2) // 3
        if nt > 0:
            m, acc, mx_a, mx_b = lax.fori_loop(0, nt, triple, (m, acc, mx_a, mx_b))
        j = 3 * nt
        left = nc - j
        if left >= 3:
            mx_c = score_chunk(j + 2, s_c)
        m, acc = _mla_update(s_a, mx_a, v_ref[j], m, acc)
        if left == 4:
            mx_a = score_chunk(j + 3, s_a)
        m, acc = _mla_update(s_b, mx_b, v_ref[j + 1], m, acc)
        if left >= 3:
            m, acc = _mla_update(s_c, mx_c, v_ref[j + 2], m, acc)
        if left == 4:
            m, acc = _mla_update(s_a, mx_a, v_ref[j + 3], m, acc)
    o_ref[...] = (acc[:MLA_V_DIM] / acc[MLA_V_DIM:MLA_V_DIM + 1]).astype(o_ref.dtype)


def _mla_call(qm, km, vm, prev_out, tq, q_tile0, n_q, ctx_only):
    bsz, _, s = qm.shape
    nc = vm.shape[1]
    aliased = prev_out is not None
    in_specs = [
        pl.BlockSpec((None, MLA_HEAD_PAD, tq), lambda b, h, i: (b, h, q_tile0 + i)),
        pl.BlockSpec((None, s, MLA_HEAD_PAD), lambda b, h, i: (b, 0, h)),
        pl.BlockSpec((None, nc, V_ROWS, KV_CHUNK), lambda b, h, i: (b, 0, h, 0)),
    ]
    args = [qm, km, vm]
    if aliased:
        in_specs.append(pl.BlockSpec(memory_space=pl.ANY))
        args.append(prev_out)
    return pl.pallas_call(
        functools.partial(_mla_kernel, nc=nc, ctx_only=ctx_only, aliased=aliased),
        grid=(bsz, MLA_HEADS, n_q),
        in_specs=in_specs,
        out_specs=pl.BlockSpec((None, MLA_V_DIM, tq), lambda b, h, i: (b, h, q_tile0 + i)),
        out_shape=jax.ShapeDtypeStruct((bsz, MLA_HEADS * MLA_V_DIM, s), BF16),
        scratch_shapes=[pltpu.VMEM((KV_CHUNK, tq), F32)] * 3,
        input_output_aliases={3: 0} if aliased else {},
        compiler_params=_cparams(("parallel", "arbitrary", "arbitrary"), 40),
        name="mla_attn_ctx" if aliased else "mla_attn",
    )(*args)


def _mla(qm, km, vm, lat, lc):
    tq = min(MLA_TQ, lat)
    out = _mla_call(qm, km, vm, None, tq, 0, lat // tq, False)
    assert lc == KV_SUB and lat % lc == 0
    return _mla_call(qm, km, vm, out, lc, lat // lc, 1, True)


def _merge_kernel(x_ref, mod_ref, yl_ref, yc_ref, yg_ref, ym_ref, gt_ref, wb_ref, wo_ref, o_ref, *, lat_tiles):
    d = D_MODEL
    i = pl.program_id(1)
    yh = jnp.where(i < lat_tiles, yl_ref[...], yc_ref[...])
    yg = yg_ref[...].astype(F32).T.astype(BF16)
    ym = ym_ref[...].astype(F32).T.astype(BF16)
    merged = (gt_ref[:, 0:d].astype(F32) * _dot(yh, wb_ref[0])
              + gt_ref[:, d:2 * d].astype(F32) * _dot(yg, wb_ref[1])
              + gt_ref[:, 2 * d:].astype(F32) * _dot(ym, wb_ref[2]))
    res = _dot(merged.astype(BF16), wo_ref[...])
    o_ref[...] = x_ref[...] + mod_ref[:, 2 * d:3 * d] * res


def _merge(xs, modsel, y_lat, y_ctx, yg, ym, gates, wb, wo, lat_tiles):
    bsz, s, d = xs.shape
    tm = ROW_TILE
    c = HY_WIDTH
    row = lambda b, i: (b, i, 0)
    col = lambda b, i: (b, 0, i)
    return pl.pallas_call(
        functools.partial(_merge_kernel, lat_tiles=lat_tiles),
        grid=(bsz, s // tm),
        in_specs=[
            pl.BlockSpec((None, tm, d), row),
            pl.BlockSpec((None, None, 1, 6 * d), lambda b, i: (b, jnp.where(i >= lat_tiles, 1, 0), 0, 0)),
            pl.BlockSpec((None, tm, c), lambda b, i: (b, jnp.minimum(i, lat_tiles - 1), 0)),
            pl.BlockSpec((None, tm, c), lambda b, i: (b, jnp.maximum(i - lat_tiles, 0), 0)),
            pl.BlockSpec((None, c, tm), col),
            pl.BlockSpec((None, c, tm), col),
            pl.BlockSpec((None, tm, 3 * d), row),
            _const_spec(wb.shape), _const_spec(wo.shape),
        ],
        out_specs=pl.BlockSpec((None, tm, d), row),
        out_shape=jax.ShapeDtypeStruct(xs.shape, F32),
        compiler_params=_cparams(("parallel", "arbitrary"), 40),
        name="branch_merge",
    )(xs, modsel, y_lat, y_ctx, yg, ym, gates, wb, wo)


def _mlp_kernel(x_ref, mod_ref, g_ref, w1_ref, w2_ref, o_ref):
    d = D_MODEL
    x = x_ref[...]
    xn = x * lax.rsqrt(jnp.mean(x * x, axis=-1, keepdims=True) + EPS) * g_ref[...]
    h = (xn * (1.0 + mod_ref[:, 4 * d:5 * d]) + mod_ref[:, 3 * d:4 * d]).astype(BF16)
    acc = jnp.zeros(x.shape, F32)
    for j in range(D_FF // FF_CHUNK):
        sl = slice(j * FF_CHUNK, (j + 1) * FF_CHUNK)
        a = jnp.maximum(_dot(h, w1_ref[:, sl]), 0.0)
        acc = acc + _dot((a * a).astype(BF16), w2_ref[sl, :])
    o_ref[...] = x + mod_ref[:, 5 * d:] * acc


def _mlp(xs, modsel, g, w1, w2, lat_tiles):
    bsz, s, d = xs.shape
    tm = ROW_TILE
    row = lambda b, i: (b, i, 0)
    return pl.pallas_call(
        _mlp_kernel,
        grid=(bsz, s // tm),
        in_specs=[
            pl.BlockSpec((None, tm, d), row),
            pl.BlockSpec((None, None, 1, 6 * d), lambda b, i: (b, jnp.where(i >= lat_tiles, 1, 0), 0, 0)),
            _const_spec(g.shape), _const_spec(w1.shape), _const_spec(w2.shape),
        ],
        out_specs=pl.BlockSpec((None, tm, d), row),
        out_shape=jax.ShapeDtypeStruct(xs.shape, F32),
        compiler_params=_cparams(("parallel", "arbitrary"), 56),
        name="relu2_mlp",
    )(xs, modsel, g, w1, w2)


def _rope_tables_t(rows, dim, lc):
    n_freq = dim // 4
    inv = ROPE_BASE ** (-jnp.arange(n_freq, dtype=F32) / n_freq)
    r = jnp.repeat(jnp.arange(rows, dtype=F32), GRID_W)
    col = jnp.tile(jnp.arange(GRID_W, dtype=F32), rows)
    ang = jnp.concatenate([r[:, None] * inv, col[:, None] * inv], axis=-1)
    cos_t = jnp.concatenate([jnp.cos(ang).T, jnp.ones((dim // 2, lc), F32)], axis=1)
    sin_t = jnp.concatenate([jnp.sin(ang).T, jnp.zeros((dim // 2, lc), F32)], axis=1)
    return cos_t, sin_t


def _lane_bcast(v):
    return jnp.broadcast_to(v.astype(F32)[:, None], (v.shape[0], ROW_TILE))


def _layer_weights(l, w_in, gqa_q_norm, gqa_k_norm, mla_q_a_norm, mla_kv_a_norm, w_q_b, w_kv_b, mla_q_norm, mla_k_norm):
    w = w_in[l]
    o = np.cumsum([0, 3 * HY_WIDTH, GQA_HEADS * GQA_HEAD_DIM, GQA_KV_HEADS * GQA_HEAD_DIM,
                   GQA_KV_HEADS * GQA_HEAD_DIM, MLA_Q_RANK, MLA_KV_RANK, MLA_ROPE_DIM, 3 * D_MODEL])
    wu = w[:, o[0]:o[1]].astype(BF16)
    wt = w[:, o[1]:o[7]].T.astype(BF16)
    wg = w[:, o[7]:o[8]].astype(BF16)
    wqb = w_q_b[l].T.astype(BF16)
    wkvb = w_kv_b[l].T.astype(BF16)
    return (wu, wg, wt, wqb, wkvb,
            _lane_bcast(gqa_q_norm[l]), _lane_bcast(gqa_k_norm[l]),
            _lane_bcast(mla_q_a_norm[l]), _lane_bcast(mla_kv_a_norm[l]),
            _lane_bcast(mla_q_norm[l]), _lane_bcast(mla_k_norm[l]))


def kernel(x, c, ctx, c_ctx, w_mod, b_mod, norm_mix_g, norm_mlp_g, w_in, hy_short_w, hy_f1_w, hy_f1_b, hy_f2_w, hy_f2_b, hy_sin_freq, hy_f3_w, hy_skip, gqa_q_norm, gqa_k_norm, gqa_sink, mla_q_a_norm, mla_kv_a_norm, w_q_b, w_kv_b, mla_q_norm, mla_k_norm, w_branch, w_out, w_mlp1, w_mlp2):
    bsz, lat, d = x.shape
    lc = ctx.shape[1]
    depth = w_mod.shape[0]
    s = lat + lc
    assert d == D_MODEL and lat % GRID_W == 0 and lat % ROW_TILE == 0 and lc == ROW_TILE and s % KV_CHUNK == 0
    lat_tiles = lat // ROW_TILE

    pad = (-(bsz + 1)) % 8
    cond = jnp.concatenate([c, c_ctx[None, :], jnp.zeros((pad, d), F32)], axis=0)
    mods = _mod_all(cond, w_mod, b_mod)

    tabs = _rope_tables_t(lat // GRID_W, GQA_HEAD_DIM, lc) + _rope_tables_t(lat // GRID_W, MLA_ROPE_DIM, lc)
    xs = jnp.concatenate([x, ctx], axis=1)

    for l in range(depth):
        ml = mods[l]
        modsel = jnp.stack([ml[:bsz], jnp.broadcast_to(ml[bsz][None], (bsz, 6 * d))], axis=1)[:, :, None, :]
        wts = _layer_weights(l, w_in, gqa_q_norm, gqa_k_norm, mla_q_a_norm, mla_kv_a_norm, w_q_b, w_kv_b,
                             mla_q_norm, mla_k_norm)
        u, gates, qg, kg, vg, qm, km, vm = _proj(xs, modsel, norm_mix_g[l][None, :], wts, tabs, lat_tiles)

        f1w = jnp.zeros((V7X_LANES, HY_FILTER_WIDTH), F32).at[:HY_EMB_DIM].set(hy_f1_w[l])
        fw = (f1w, hy_f1_b[l][None, :], hy_f2_w[l], hy_f2_b[l][None, :], hy_sin_freq[l], hy_f3_w[l])
        y_lat, y_ctx = _hyena(u, hy_short_w[l], fw, hy_skip[l], lat, lc)

        sink_rows = jnp.broadcast_to(gqa_sink[l].reshape(GQA_KV_HEADS, GQA_GROUP, 1),
                                     (GQA_KV_HEADS, GQA_GROUP, GQA_TQ)).astype(F32)
        yg = _gqa(qg, kg, vg, sink_rows, lat, lc)
        ym = _mla(qm, km, vm, lat, lc)

        xs = _merge(xs, modsel, y_lat, y_ctx, yg, ym, gates, w_branch[l].astype(BF16), w_out[l].astype(BF16),
                    lat_tiles)
        xs = _mlp(xs, modsel, norm_mlp_g[l][None, :], w_mlp1[l].astype(BF16), w_mlp2[l].astype(BF16), lat_tiles)
    return xs[:, :lat]
```

```python
import functools
import math

import numpy as np
import jax
import jax.numpy as jnp
from jax import lax
from jax.experimental import pallas as pl
from jax.experimental.pallas import tpu as pltpu

D_MODEL = 1024
GRID_W = 64
HY_WIDTH = 512
HY_EMB_DIM = 33
HY_BANDS = (HY_EMB_DIM - 1) // 2
HY_FILTER_WIDTH = 64
HY_DECAY_TARGET = 1e-2
HY_FAST_DECAY = 0.3
HY_SLOW_DECAY = 1.5
GQA_HEADS = 8
GQA_KV_HEADS = 2
GQA_GROUP = GQA_HEADS // GQA_KV_HEADS
GQA_HEAD_DIM = 64
GQA_SCALE = GQA_HEAD_DIM ** -0.5
WINDOW = 128
MLA_HEADS = 8
MLA_Q_RANK = 384
MLA_KV_RANK = 256
MLA_NOPE_DIM = 64
MLA_ROPE_DIM = 32
MLA_V_DIM = 64
MLA_QK_DIM = MLA_NOPE_DIM + MLA_ROPE_DIM
MLA_SCALE = MLA_QK_DIM ** -0.5
D_FF = 4 * D_MODEL
ROPE_BASE = 10000.0
EPS = 1e-6
LOG2E = 1.4426950408889634
NEG_BIG = -1e30

V7X_LANES = 128
V7X_VMEM_BYTES = 64 * 1024 * 1024

ROW_TILE = 256
MLA_HEAD_PAD = 128
V_ROWS = 80
KV_CHUNK = 1 * ROW_TILE
KV_SUB = 256
MLA_UNROLL = 5
MLA_TQ = 512
GQA_TQ = 256
FF_CHUNK = 1024
FFT_N2 = 128
FFT_NB = 16
FFT_KB = 8

F32 = jnp.float32
BF16 = jnp.bfloat16


def _cparams(sem, vmem_mb, flags=None):
    return pltpu.CompilerParams(dimension_semantics=sem, vmem_limit_bytes=vmem_mb * 1024 * 1024, flags=flags)


def _dot(a, b):
    return jnp.dot(a, b, preferred_element_type=F32)


def _dot_hi(a, b):
    return jnp.dot(a, b, preferred_element_type=F32, precision=lax.Precision.HIGHEST)


def _ones_row_block(width):
    r = lax.broadcasted_iota(jnp.int32, (V_ROWS - MLA_V_DIM, width), 0)
    return jnp.where(r == 0, 1.0, 0.0).astype(BF16)


def _const_spec(shape):
    nd = len(shape)
    return pl.BlockSpec(shape, lambda *_: (0,) * nd)


def _mod_kernel(c_ref, w_ref, b_ref, o_ref):
    c = c_ref[...]
    s = c * jax.nn.sigmoid(c)
    o_ref[...] = _dot_hi(s, w_ref[...]) + b_ref[...]


def _mod_all(cond, w_mod, b_mod):
    depth, d, n = w_mod.shape
    rows = cond.shape[0]
    tn = 1536
    return pl.pallas_call(
        _mod_kernel,
        grid=(depth, n // tn),
        in_specs=[
            pl.BlockSpec((rows, d), lambda l, j: (0, 0)),
            pl.BlockSpec((None, d, tn), lambda l, j: (l, 0, j)),
            pl.BlockSpec((None, 1, tn), lambda l, j: (l, 0, j)),
        ],
        out_specs=pl.BlockSpec((None, rows, tn), lambda l, j: (l, 0, j)),
        out_shape=jax.ShapeDtypeStruct((depth, rows, n), F32),
        compiler_params=_cparams(("arbitrary", "arbitrary"), 40),
        name="adaln_mod",
    )(cond, w_mod, b_mod.reshape(depth, 1, n))


def _rms_rows(x, n):
    return lax.rsqrt(jnp.sum(x * x, axis=0, keepdims=True) * (1.0 / n) + EPS)


def _rope_rows(x1, x2, cs, sn):
    return x1 * cs - x2 * sn, x1 * sn + x2 * cs


def _proj_kernel(x_ref, mod_ref, g_ref, wu_ref, wg_ref, wt_ref, wqb_ref, wkvb_ref,
                 gq_ref, gk_ref, gqa_ref, gkva_ref, gmq_ref, gmk_ref,
                 cg_ref, sg_ref, cm_ref, sm_ref,
                 u_ref, gate_ref, qg_ref, kg_ref, vg_ref, qm_ref, km_ref, vm_ref):
    d = D_MODEL
    x = x_ref[...]
    tm = x.shape[0]
    shift = mod_ref[:, 0:d]
    scale = mod_ref[:, d:2 * d]
    xn = x * lax.rsqrt(jnp.mean(x * x, axis=-1, keepdims=True) + EPS) * g_ref[...]
    h = xn * (1.0 + scale) + shift
    hb = h.astype(BF16)

    u_ref[...] = _dot(hb, wu_ref[...]).astype(BF16)
    gate_ref[...] = jax.nn.sigmoid(_dot(hb, wg_ref[...])).astype(BF16)

    ht = h.T.astype(BF16)
    t = _dot(wt_ref[...], ht)
    o_q, o_k, o_v = 0, 512, 640
    o_cq, o_ckv, o_kr = 768, 768 + MLA_Q_RANK, 768 + MLA_Q_RANK + MLA_KV_RANK

    cg, sg = cg_ref[...], sg_ref[...]
    cm, sm = cm_ref[...], sm_ref[...]
    hd, hh = GQA_HEAD_DIM, GQA_HEAD_DIM // 2

    gq = gq_ref[...]
    for n in range(GQA_HEADS):
        xh = t[o_q + n * hd:o_q + (n + 1) * hd]
        xh = xh * _rms_rows(xh, hd) * gq
        a, b = _rope_rows(xh[:hh], xh[hh:], cg, sg)
        qg_ref[n * hd:n * hd + hh, :] = (a * (GQA_SCALE * LOG2E)).astype(BF16)
        qg_ref[n * hd + hh:(n + 1) * hd, :] = (b * (GQA_SCALE * LOG2E)).astype(BF16)

    gk = gk_ref[...]
    zpad = jnp.zeros((V7X_LANES - hd, tm), F32)
    parts = []
    for n in range(GQA_KV_HEADS):
        xh = t[o_k + n * hd:o_k + (n + 1) * hd]
        xh = xh * _rms_rows(xh, hd) * gk
        a, b = _rope_rows(xh[:hh], xh[hh:], cg, sg)
        parts += [a, b, zpad]
    kg_ref[...] = jnp.concatenate(parts, axis=0).T.astype(BF16)
    ones_rows = _ones_row_block(tm)
    for n in range(GQA_KV_HEADS):
        vg_ref[n * V_ROWS:n * V_ROWS + hd, :] = t[o_v + n * hd:o_v + (n + 1) * hd].astype(BF16)
        vg_ref[n * V_ROWS + hd:(n + 1) * V_ROWS, :] = ones_rows

    cq = t[o_cq:o_cq + MLA_Q_RANK]
    cqn = (cq * _rms_rows(cq, MLA_Q_RANK) * gqa_ref[...]).astype(BF16)
    qm = _dot(wqb_ref[...], cqn)
    gmq = gmq_ref[...]
    nd, rh = MLA_NOPE_DIM, MLA_ROPE_DIM // 2
    qpad = jnp.zeros((MLA_HEAD_PAD - MLA_QK_DIM, tm), BF16)
    for n in range(MLA_HEADS):
        xh = qm[n * MLA_QK_DIM:(n + 1) * MLA_QK_DIM]
        xh = xh * _rms_rows(xh, MLA_QK_DIM) * gmq * (MLA_SCALE * LOG2E)
        a, b = _rope_rows(xh[nd:nd + rh], xh[nd + rh:], cm, sm)
        base = n * MLA_HEAD_PAD
        qm_ref[base:base + nd, :] = xh[:nd].astype(BF16)
        qm_ref[base + nd:base + nd + rh, :] = a.astype(BF16)
        qm_ref[base + nd + rh:base + MLA_QK_DIM, :] = b.astype(BF16)
        qm_ref[base + MLA_QK_DIM:base + MLA_HEAD_PAD, :] = qpad

    ckv = t[o_ckv:o_ckv + MLA_KV_RANK]
    ckvn = (ckv * _rms_rows(ckv, MLA_KV_RANK) * gkva_ref[...]).astype(BF16)
    kv = _dot(wkvb_ref[...], ckvn)
    kr = t[o_kr:o_kr + MLA_ROPE_DIM]
    kr_ss = jnp.sum(kr * kr, axis=0, keepdims=True)
    gmk = gmk_ref[...]
    kzero = jnp.zeros((MLA_HEAD_PAD - MLA_QK_DIM, tm), F32)
    ones_rows = _ones_row_block(tm)
    parts = []
    for n in range(MLA_HEADS):
        kn = kv[n * 128:n * 128 + nd]
        vm_ref[n * V_ROWS:n * V_ROWS + MLA_V_DIM, :] = kv[n * 128 + nd:(n + 1) * 128].astype(BF16)
        vm_ref[n * V_ROWS + MLA_V_DIM:(n + 1) * V_ROWS, :] = ones_rows
        rs = lax.rsqrt((jnp.sum(kn * kn, axis=0, keepdims=True) + kr_ss) * (1.0 / MLA_QK_DIM) + EPS)
        krn = kr * rs * gmk[nd:]
        a, b = _rope_rows(krn[:rh], krn[rh:], cm, sm)
        parts += [kn * rs * gmk[:nd], a, b, kzero]
    km_ref[...] = jnp.concatenate(parts, axis=0).T.astype(BF16)


def _proj(xs, modsel, g, wts, tabs, lat_tiles):
    bsz, s, d = xs.shape
    tm = ROW_TILE
    nt = s // tm
    ncs = KV_CHUNK // tm
    (wu, wg, wt, wqb, wkvb, gq, gk, gqa, gkva, gmq, gmk) = wts
    cg, sg, cm, sm = tabs
    row = lambda b, i: (b, i, 0)
    col = lambda b, i: (b, 0, i)
    tab = lambda b, i: (0, i)
    in_specs = [
        pl.BlockSpec((None, tm, d), row),
        pl.BlockSpec((None, None, 1, 6 * d), lambda b, i: (b, jnp.where(i >= lat_tiles, 1, 0), 0, 0)),
        _const_spec(g.shape), _const_spec(wu.shape), _const_spec(wg.shape), _const_spec(wt.shape),
        _const_spec(wqb.shape), _const_spec(wkvb.shape),
        _const_spec(gq.shape), _const_spec(gk.shape), _const_spec(gqa.shape), _const_spec(gkva.shape),
        _const_spec(gmq.shape), _const_spec(gmk.shape),
        pl.BlockSpec((cg.shape[0], tm), tab), pl.BlockSpec((sg.shape[0], tm), tab),
        pl.BlockSpec((cm.shape[0], tm), tab), pl.BlockSpec((sm.shape[0], tm), tab),
    ]
    out_shape = [
        jax.ShapeDtypeStruct((bsz, s, 3 * HY_WIDTH), BF16),
        jax.ShapeDtypeStruct((bsz, s, 3 * d), BF16),
        jax.ShapeDtypeStruct((bsz, GQA_HEADS * GQA_HEAD_DIM, s), BF16),
        jax.ShapeDtypeStruct((bsz, s, GQA_KV_HEADS * V7X_LANES), BF16),
        jax.ShapeDtypeStruct((bsz, GQA_KV_HEADS * V_ROWS, s), BF16),
        jax.ShapeDtypeStruct((bsz, MLA_HEADS * MLA_HEAD_PAD, s), BF16),
        jax.ShapeDtypeStruct((bsz, s, MLA_HEADS * MLA_HEAD_PAD), BF16),
        jax.ShapeDtypeStruct((bsz, s // KV_CHUNK, MLA_HEADS * V_ROWS, KV_CHUNK), BF16),
    ]
    out_specs = [
        pl.BlockSpec((None, tm, 3 * HY_WIDTH), row),
        pl.BlockSpec((None, tm, 3 * d), row),
        pl.BlockSpec((None, GQA_HEADS * GQA_HEAD_DIM, tm), col),
        pl.BlockSpec((None, tm, GQA_KV_HEADS * V7X_LANES), row),
        pl.BlockSpec((None, GQA_KV_HEADS * V_ROWS, tm), col),
        pl.BlockSpec((None, MLA_HEADS * MLA_HEAD_PAD, tm), col),
        pl.BlockSpec((None, tm, MLA_HEADS * MLA_HEAD_PAD), row),
        pl.BlockSpec((None, None, MLA_HEADS * V_ROWS, tm), lambda b, i: (b, i // ncs, 0, i % ncs)),
    ]
    return pl.pallas_call(
        _proj_kernel,
        grid=(bsz, nt),
        in_specs=in_specs,
        out_specs=out_specs,
        out_shape=out_shape,
        compiler_params=_cparams(("parallel", "arbitrary"), 56),
        name="in_proj",
    )(xs, modsel, g, wu, wg, wt, wqb, wkvb, gq, gk, gqa, gkva, gmq, gmk, cg, sg, cm, sm)


def _short_conv_kernel(u_ref, up_ref, un_ref, w_ref, x0_ref, z_ref, *, n_tiles):
    i = pl.program_id(1)
    u = u_ref[...].astype(F32)
    tm = u.shape[0]
    prev = jnp.where(i > 0, up_ref[7:8, :].astype(F32), 0.0)
    nxt = jnp.where(i < n_tiles - 1, un_ref[0:1, :].astype(F32), 0.0)
    ridx = lax.broadcasted_iota(jnp.int32, u.shape, 0)
    up = jnp.where(ridx == 0, prev, pltpu.roll(u, 1, axis=0))
    dn = jnp.where(ridx == tm - 1, nxt, pltpu.roll(u, tm - 1, axis=0))
    uc = up * w_ref[0:1, :] + u * w_ref[1:2, :] + dn * w_ref[2:3, :]
    c = HY_WIDTH
    x0_ref[...] = uc[:, :c].astype(BF16)
    z_ref[...] = (uc[:, c:2 * c] * uc[:, 2 * c:]).astype(BF16)


def _short_conv(u, short_w, row0, rows):
    bsz, s, c3 = u.shape
    tm = ROW_TILE
    nt = rows // tm
    t0 = row0 // tm
    r8 = tm // 8
    last8 = s // 8 - 1
    return pl.pallas_call(
        functools.partial(_short_conv_kernel, n_tiles=nt),
        grid=(bsz, nt),
        in_specs=[
            pl.BlockSpec((None, tm, c3), lambda b, i: (b, t0 + i, 0)),
            pl.BlockSpec((None, 8, c3), lambda b, i: (b, jnp.maximum((t0 + i) * r8 - 1, 0), 0)),
            pl.BlockSpec((None, 8, c3), lambda b, i: (b, jnp.minimum((t0 + i + 1) * r8, last8), 0)),
            _const_spec(short_w.shape),
        ],
        out_specs=[pl.BlockSpec((None, tm, HY_WIDTH), lambda b, i: (b, i, 0))] * 2,
        out_shape=[jax.ShapeDtypeStruct((bsz, rows, HY_WIDTH), BF16)] * 2,
        compiler_params=_cparams(("parallel", "arbitrary"), 32),
        name="hyena_short_conv",
    )(u, u, u, short_w)


def _filter_kernel(zf_ref, w1_ref, b1_ref, w2_ref, b2_ref, fr_ref, w3_ref, dl_ref, h_ref, ss_ref):
    i = pl.program_id(0)
    zf = zf_ref[...]
    tl = zf.shape[0]
    h = jnp.sin(fr_ref[0:1, :] * (_dot_hi(zf, w1_ref[...]) + b1_ref[...]))
    h = jnp.sin(fr_ref[1:2, :] * (_dot_hi(h, w2_ref[...]) + b2_ref[...]))
    h = _dot_hi(h, w3_ref[...])
    decay = jnp.exp(-zf[:, 0:1] * dl_ref[...])
    c = HY_WIDTH
    hf = h[:, :c] * decay
    ridx = lax.broadcasted_iota(jnp.int32, (tl, c), 0) + i * tl
    hb = jnp.where(ridx == 0, 0.0, h[:, c:] * decay)
    h_ref[0] = hf.astype(BF16)
    h_ref[1] = hb.astype(BF16)
    ss = jnp.sum(hf * hf + hb * hb, axis=0, keepdims=True)

    @pl.when(i == 0)
    def _():
        ss_ref[...] = ss

    @pl.when(i > 0)
    def _():
        ss_ref[...] += ss


def _filter(zfeat, fw, deltas):
    length = zfeat.shape[0]
    tl = min(length, 1024)
    w1, b1, w2, b2, fr, w3 = fw
    return pl.pallas_call(
        _filter_kernel,
        grid=(length // tl,),
        in_specs=[pl.BlockSpec((tl, zfeat.shape[1]), lambda i: (i, 0))]
        + [_const_spec(a.shape) for a in (w1, b1, w2, b2, fr, w3, deltas)],
        out_specs=[pl.BlockSpec((2, tl, HY_WIDTH), lambda i: (0, i, 0)),
                   pl.BlockSpec((1, HY_WIDTH), lambda i: (0, 0))],
        out_shape=[jax.ShapeDtypeStruct((2, length, HY_WIDTH), BF16),
                   jax.ShapeDtypeStruct((1, HY_WIDTH), F32)],
        compiler_params=_cparams(("arbitrary",), 40),
        name="hyena_filter",
    )(zfeat, w1, b1, w2, b2, fr, w3, deltas)


def _fft_a_kernel(g_ref, x_ref, o_ref, *, nb, cw):
    for j in range(nb):
        sl = slice(j * cw, (j + 1) * cw)
        o_ref[:, sl] = _dot(g_ref[j], x_ref[:, sl]).astype(o_ref.dtype)


def _fft_a(x2d, gmat):
    bx, k1, w = x2d.shape
    n2, two_n1, _ = gmat.shape
    cw = w // n2
    nb = FFT_NB
    return pl.pallas_call(
        functools.partial(_fft_a_kernel, nb=nb, cw=cw),
        grid=(bx, n2 // nb),
        in_specs=[pl.BlockSpec((nb, two_n1, k1), lambda b, j: (j, 0, 0)),
                  pl.BlockSpec((None, k1, nb * cw), lambda b, j: (b, 0, j))],
        out_specs=pl.BlockSpec((None, two_n1, nb * cw), lambda b, j: (b, 0, j)),
        out_shape=jax.ShapeDtypeStruct((bx, two_n1, w), BF16),
        compiler_params=_cparams(("parallel", "arbitrary"), 40),
        name="fft_stage_a",
    )(gmat, x2d)


def _fft_filter_b_kernel(fb_ref, a_ref, ss_ref, kf_ref, *, inv_n):
    n1 = a_ref.shape[2]
    xf = _dot(fb_ref[...], jnp.concatenate([a_ref[0, 0], a_ref[0, 1]], axis=0))
    xb = _dot(fb_ref[...], jnp.concatenate([a_ref[1, 0], a_ref[1, 1]], axis=0))
    rs = lax.rsqrt(ss_ref[...] + EPS) * inv_n
    kf_ref[0] = (xf[:n1] + xb[:n1]) * rs
    kf_ref[1] = (xf[n1:] - xb[n1:]) * rs


def _fft_filter_b(a5, fb, ssq, inv_n):
    _, _, n1, n2, c = a5.shape
    return pl.pallas_call(
        functools.partial(_fft_filter_b_kernel, inv_n=inv_n),
        grid=(n1,),
        in_specs=[_const_spec(fb.shape),
                  pl.BlockSpec((2, 2, None, n2, c), lambda k: (0, 0, k, 0, 0)),
                  _const_spec(ssq.shape)],
        out_specs=pl.BlockSpec((2, None, n2, c), lambda k: (0, k, 0, 0)),
        out_shape=jax.ShapeDtypeStruct((2, n1, n2, c), F32),
        compiler_params=_cparams(("arbitrary",), 32),
        name="fft_filter_stage_b",
    )(fb, a5, ssq)


def _fft_b_kernel(fb_ref, fbi_ref, a_ref, kf_ref, o_ref):
    n = a_ref.shape[2]
    for r in range(a_ref.shape[1]):
        x = _dot(fb_ref[...], jnp.concatenate([a_ref[0, r], a_ref[1, r]], axis=0))
        xr, xi = x[:n], x[n:]
        kr, ki = kf_ref[0, r], kf_ref[1, r]
        y = jnp.concatenate([xr * kr - xi * ki, xr * ki + xi * kr], axis=0).astype(BF16)
        c = _dot(fbi_ref[...], y)
        o_ref[0, r] = c[:n].astype(o_ref.dtype)
        o_ref[1, r] = c[n:].astype(o_ref.dtype)


def _fft_b(a5, kf, fb, fbi):
    bsz, _, n1, n2, c = a5.shape
    kb = min(FFT_KB, n1)
    return pl.pallas_call(
        _fft_b_kernel,
        grid=(n1 // kb, bsz),
        in_specs=[_const_spec(fb.shape), _const_spec(fbi.shape),
                  pl.BlockSpec((None, 2, kb, n2, c), lambda k, b: (b, 0, k, 0, 0)),
                  pl.BlockSpec((2, kb, n2, c), lambda k, b: (0, k, 0, 0))],
        out_specs=pl.BlockSpec((None, 2, kb, n2, c), lambda k, b: (b, 0, k, 0, 0)),
        out_shape=jax.ShapeDtypeStruct(a5.shape, BF16),
        compiler_params=_cparams(("arbitrary", "arbitrary"), 32),
        name="fft_stage_b",
    )(fb, fbi, a5, kf)


def _fft_c_kernel(h_ref, c_ref, z_ref, x0_ref, sk_ref, o_ref, *, nb, cw):
    sk = sk_ref[...]
    for j in range(nb):
        sl = slice(j * cw, (j + 1) * cw)
        y = _dot(h_ref[j], c_ref[:, sl])
        z = z_ref[:, sl].astype(F32)
        o_ref[:, sl] = (x0_ref[:, sl].astype(F32) * (y + z * sk)).astype(o_ref.dtype)


def _fft_c(c2d, hmat, z2d, x02d, skip):
    bsz, two_n1, w = c2d.shape
    n2, k1, _ = hmat.shape
    cw = w // n2
    nb = FFT_NB
    blk = lambda b, j: (b, 0, j)
    return pl.pallas_call(
        functools.partial(_fft_c_kernel, nb=nb, cw=cw),
        grid=(bsz, n2 // nb),
        in_specs=[pl.BlockSpec((nb, k1, two_n1), lambda b, j: (j, 0, 0)),
                  pl.BlockSpec((None, two_n1, nb * cw), blk),
                  pl.BlockSpec((None, k1, nb * cw), blk),
                  pl.BlockSpec((None, k1, nb * cw), blk),
                  _const_spec(skip.shape)],
        out_specs=pl.BlockSpec((None, k1, nb * cw), blk),
        out_shape=jax.ShapeDtypeStruct((bsz, k1, w), BF16),
        compiler_params=_cparams(("parallel", "arbitrary"), 40),
        name="fft_stage_c",
    )(hmat, c2d, z2d, x02d, skip)


def _ctx_conv_kernel(fc_ref, fci_ref, h_ref, ss_ref, z_ref, x0_ref, sk_ref, o_ref):
    n = fc_ref.shape[0] // 2
    fc = fc_ref[...]
    kf = _dot(fc, h_ref[0])
    kb = _dot(fc, h_ref[1])
    rs = lax.rsqrt(ss_ref[...] + EPS)
    kr = (kf[:n] + kb[:n]) * rs
    ki = (kf[n:] - kb[n:]) * rs
    zb = z_ref[...]
    x = _dot(fc, zb)
    xr, xi = x[:n], x[n:]
    y = jnp.concatenate([xr * kr - xi * ki, xr * ki + xi * kr], axis=0).astype(BF16)
    y = _dot(fci_ref[...], y)
    z = zb.astype(F32)
    o_ref[...] = (x0_ref[...].astype(F32) * (y + z * sk_ref[...])).astype(o_ref.dtype)


def _ctx_conv(fc, fci, hfb, ssq, z, x0, skip):
    bsz, lc, c = z.shape
    blk = pl.BlockSpec((None, lc, c), lambda b: (b, 0, 0))
    return pl.pallas_call(
        _ctx_conv_kernel,
        grid=(bsz,),
        in_specs=[_const_spec(fc.shape), _const_spec(fci.shape), _const_spec(hfb.shape),
                  _const_spec(ssq.shape), blk, blk, _const_spec(skip.shape)],
        out_specs=blk,
        out_shape=jax.ShapeDtypeStruct((bsz, lc, c), BF16),
        compiler_params=_cparams(("arbitrary",), 32),
        name="hyena_ctx_conv",
    )(fc, fci, hfb, ssq, z, x0, skip)


@functools.lru_cache(maxsize=None)
def _fft_tables(length):
    n = 2 * length
    n2 = FFT_N2
    n1 = n // n2
    k1 = length // n2
    kk = np.arange(n1)[:, None]
    g = np.empty((n2, 2 * n1, k1), np.float64)
    h = np.empty((n2, k1, 2 * n1), np.float64)
    nn = np.arange(k1)[None, :]
    for j in range(n2):
        ang = 2.0 * np.pi * (((n2 * nn * kk) % n) + (j * kk) % n) / n
        g[j, :n1] = np.cos(ang)
        g[j, n1:] = -np.sin(ang)
        h[j, :, :n1] = np.cos(ang).T
        h[j, :, n1:] = -np.sin(ang).T
    a = np.arange(n2)
    ph = 2.0 * np.pi * ((a[:, None] * a[None, :]) % n2) / n2
    c, s = np.cos(ph), np.sin(ph)
    fb = np.block([[c, s], [-s, c]])
    fbi = np.block([[c, -s], [s, c]])
    return (jnp.asarray(g, BF16), jnp.asarray(h, BF16), jnp.asarray(fb, BF16), jnp.asarray(fbi, BF16), n1, k1)


@functools.lru_cache(maxsize=None)
def _dft_tables(length):
    n = 2 * length
    k = np.arange(n)[:, None]
    t = np.arange(length)[None, :]
    ang = 2.0 * np.pi * ((k * t) % n) / n
    fc = np.concatenate([np.cos(ang), -np.sin(ang)], axis=0)
    fci = np.concatenate([np.cos(ang).T, -np.sin(ang).T], axis=1) / n
    return jnp.asarray(fc, BF16), jnp.asarray(fci, BF16)


@functools.lru_cache(maxsize=None)
def _filter_features(length):
    t = np.linspace(0.0, 1.0, length, dtype=np.float32)[:, None]
    w = (2.0 * math.pi * np.arange(length, dtype=np.float32)[:, None] / length).astype(np.float32)
    f = np.linspace(1e-4, HY_BANDS - 1, HY_BANDS, dtype=np.float32)[None, :]
    z = np.concatenate([t, np.cos(f * w), -np.sin(f * w)], axis=-1).astype(np.float32)
    zp = np.zeros((length, V7X_LANES), np.float32)
    zp[:, :HY_EMB_DIM] = z
    return jnp.asarray(zp)


def _hyena_deltas():
    max_decay = math.log(HY_DECAY_TARGET) / HY_FAST_DECAY
    min_decay = math.log(HY_DECAY_TARGET) / HY_SLOW_DECAY
    return jnp.abs(jnp.linspace(min_decay, max_decay, HY_WIDTH, dtype=F32))[None, :]


def _hyena(u, short_w, fw, skip, lat, lc):
    bsz = u.shape[0]
    c = HY_WIDTH
    deltas = _hyena_deltas()
    skip2 = skip.reshape(1, c)
    x0, z = _short_conv(u, short_w, 0, lat)
    gmat, hmat, fb, fbi, n1, k1 = _fft_tables(lat)
    n2 = FFT_N2
    hfb, ssq = _filter(_filter_features(lat), fw, deltas)
    fa = _fft_a(hfb.reshape(2, k1, n2 * c), gmat)
    kf = _fft_filter_b(fa.reshape(2, 2, n1, n2, c), fb, ssq, 1.0 / (2 * lat))
    za = _fft_a(z.reshape(bsz, k1, n2 * c), gmat)
    zc = _fft_b(za.reshape(bsz, 2, n1, n2, c), kf, fb, fbi)
    y = _fft_c(zc.reshape(bsz, 2 * n1, n2 * c), hmat, z.reshape(bsz, k1, n2 * c),
               x0.reshape(bsz, k1, n2 * c), skip2)
    y_lat = y.reshape(bsz, lat, c)
    x0c, zc_ = _short_conv(u, short_w, lat, lc)
    hfb_c, ssq_c = _filter(_filter_features(lc), fw, deltas)
    fc, fci = _dft_tables(lc)
    y_ctx = _ctx_conv(fc, fci, hfb_c, ssq_c, zc_, x0c, skip2)
    return y_lat, y_ctx


def _gqa_kernel(q_ref, kc_ref, kp_ref, kn_ref, kx_ref, vc_ref, vp_ref, vn_ref, vx_ref, sink_ref, o_ref,
                s_a, s_b, *, lat):
    i = pl.program_id(2)
    tq = q_ref.shape[1]
    hd = GQA_HEAD_DIM
    w = WINDOW
    lc = kx_ref.shape[0]
    nwin = tq + 2 * w
    k_all = jnp.concatenate([kp_ref[...], kc_ref[...], kn_ref[...], kx_ref[...]], axis=0)
    v_all = jnp.concatenate([vp_ref[...], vc_ref[...], vn_ref[...], vx_ref[...]], axis=1)
    q_pos = i * tq + lax.broadcasted_iota(jnp.int32, (1, tq), 1)
    k_pos = i * tq - w + lax.broadcasted_iota(jnp.int32, (nwin, 1), 0)
    bias = (jnp.where(jnp.abs(k_pos - q_pos) <= w, 0.0, NEG_BIG)
            + jnp.where((k_pos >= 0) & (k_pos < lat), 0.0, NEG_BIG)
            + jnp.where(q_pos < lat, 0.0, NEG_BIG))
    bias = jnp.concatenate([bias, jnp.zeros((lc, tq), F32)], axis=0)
    bias2 = jnp.concatenate([bias, bias], axis=1)
    zq = jnp.zeros((V7X_LANES - hd, 2 * tq), BF16)
    staged = []
    for h0, s_sc in ((0, s_a), (2, s_b)):
        q2 = jnp.concatenate([q_ref[h0 * hd:(h0 + 1) * hd, :], q_ref[(h0 + 1) * hd:(h0 + 2) * hd, :]], axis=1)
        s = _dot(k_all, jnp.concatenate([q2, zq], axis=0)) + bias2
        s_sc[...] = s
        staged.append(jnp.max(s, axis=0, keepdims=True))
    for (h0, s_sc), s_max in zip(((0, s_a), (2, s_b)), staged):
        sink = jnp.concatenate([sink_ref[h0:h0 + 1, :], sink_ref[h0 + 1:h0 + 2, :]], axis=1) * LOG2E
        m = jnp.maximum(s_max, sink)
        p = jnp.exp2(s_sc[...] - m).astype(BF16)
        pv = _dot(v_all, p)
        o = pv[:hd] / (pv[hd:hd + 1] + jnp.exp2(sink - m))
        o_ref[h0 * hd:(h0 + 1) * hd, :] = o[:, :tq].astype(o_ref.dtype)
        o_ref[(h0 + 1) * hd:(h0 + 2) * hd, :] = o[:, tq:].astype(o_ref.dtype)


def _gqa(qg, kg, vg, sink_rows, lat, lc):
    bsz, _, s = qg.shape
    tq = GQA_TQ
    w = WINDOW
    gq = GQA_GROUP * GQA_HEAD_DIM
    wpt = tq // w
    last_w = s // w - 1
    return pl.pallas_call(
        functools.partial(_gqa_kernel, lat=lat),
        grid=(bsz, GQA_KV_HEADS, s // tq),
        in_specs=[
            pl.BlockSpec((None, gq, tq), lambda b, g, i: (b, g, i)),
            pl.BlockSpec((None, tq, V7X_LANES), lambda b, g, i: (b, i, g)),
            pl.BlockSpec((None, w, V7X_LANES), lambda b, g, i: (b, jnp.maximum(i * wpt - 1, 0), g)),
            pl.BlockSpec((None, w, V7X_LANES), lambda b, g, i: (b, jnp.minimum((i + 1) * wpt, last_w), g)),
            pl.BlockSpec((None, lc, V7X_LANES), lambda b, g, i: (b, lat // lc, g)),
            pl.BlockSpec((None, V_ROWS, tq), lambda b, g, i: (b, g, i)),
            pl.BlockSpec((None, V_ROWS, w), lambda b, g, i: (b, g, jnp.maximum(i * wpt - 1, 0))),
            pl.BlockSpec((None, V_ROWS, w), lambda b, g, i: (b, g, jnp.minimum((i + 1) * wpt, last_w))),
            pl.BlockSpec((None, V_ROWS, lc), lambda b, g, i: (b, g, lat // lc)),
            pl.BlockSpec((None, GQA_GROUP, tq), lambda b, g, i: (g, 0, 0)),
        ],
        out_specs=pl.BlockSpec((None, gq, tq), lambda b, g, i: (b, g, i)),
        out_shape=jax.ShapeDtypeStruct((bsz, GQA_HEADS * GQA_HEAD_DIM, s), BF16),
        scratch_shapes=[pltpu.VMEM((tq + 2 * w + lc, 2 * tq), F32)] * 2,
        compiler_params=_cparams(("parallel", "arbitrary", "arbitrary"), 32),
        name="gqa_window_attn",
    )(qg, kg, kg, kg, kg, vg, vg, vg, vg, sink_rows)


def _mla_update(s_ref, s_max, vt, m, acc):
    m_new = jnp.maximum(m, s_max)
    alpha = jnp.exp2(m - m_new)
    p = jnp.exp2(s_ref[...] - m_new).astype(BF16)
    acc = alpha * acc + _dot(vt, p)
    return m_new, acc


def _mla_kernel(q_ref, k_ref, v_ref, o_ref, s_a, s_b, s_c, *, nc, ctx_only):
    q = q_ref[...]
    tq = q.shape[1]
    ck = KV_CHUNK
    m = jnp.full((1, tq), NEG_BIG, F32)
    acc = jnp.zeros((V_ROWS, tq), F32)

    def score(k, dst):
        s = _dot(k, q)
        dst[...] = s
        return jnp.max(s, axis=0, keepdims=True)

    def score_chunk(j, dst):
        return score(k_ref[pl.ds(pl.multiple_of(j * ck, ck), ck), :], dst)

    if ctx_only:
        s_x = s_a.at[0:KV_SUB, :]
        mx = score(k_ref[nc * ck - KV_SUB:nc * ck, :], s_x)
        m, acc = _mla_update(s_x, mx, v_ref[nc - 1][:, ck - KV_SUB:], m, acc)
    elif nc == 1:
        m, acc = _mla_update(s_a, score_chunk(0, s_a), v_ref[0], m, acc)
    else:
        mx_a = score_chunk(0, s_a)
        mx_b = score_chunk(1, s_b)

        def triple(j, c):
            m, acc, mx_a, mx_b = c
            mx_c = score_chunk(j + 2, s_c)
            m, acc = _mla_update(s_a, mx_a, v_ref[j], m, acc)
            mx_a = score_chunk(j + 3, s_a)
            m, acc = _mla_update(s_b, mx_b, v_ref[j + 1], m, acc)
            mx_b = score_chunk(j + 4, s_b)
            m, acc = _mla_update(s_c, mx_c, v_ref[j + 2], m, acc)
            return m, acc, mx_a, mx_b

        def triples(i, c):
            for u in range(MLA_UNROLL):
                c = triple(3 * (MLA_UNROLL * i + u), c)
            return c

        nt = (nc - 2) // 3
        c = (m, acc, mx_a, mx_b)
        if nt >= MLA_UNROLL:
            c = lax.fori_loop(0, nt // MLA_UNROLL, triples, c)
        for t in range(nt - nt % MLA_UNROLL, nt):
            c = triple(3 * t, c)
        m, acc, mx_a, mx_b = c
        j = 3 * nt
        left = nc - j
        if left >= 3:
            mx_c = score_chunk(j + 2, s_c)
        m, acc = _mla_update(s_a, mx_a, v_ref[j], m, acc)
        if left == 4:
            mx_a = score_chunk(j + 3, s_a)
        m, acc = _mla_update(s_b, mx_b, v_ref[j + 1], m, acc)
        if left >= 3:
            m, acc = _mla_update(s_c, mx_c, v_ref[j + 2], m, acc)
        if left == 4:
            m, acc = _mla_update(s_a, mx_a, v_ref[j + 3], m, acc)
    o_ref[...] = (acc[:MLA_V_DIM] / acc[MLA_V_DIM:MLA_V_DIM + 1]).astype(o_ref.dtype)


def _mla_call(qm, km, vm, tq, q_tile0, n_q, ctx_only):
    bsz, _, s = qm.shape
    nc = vm.shape[1]
    return pl.pallas_call(
        functools.partial(_mla_kernel, nc=nc, ctx_only=ctx_only),
        grid=(bsz, MLA_HEADS, n_q),
        in_specs=[
            pl.BlockSpec((None, MLA_HEAD_PAD, tq), lambda b, h, i: (b, h, q_tile0 + i)),
            pl.BlockSpec((None, s, MLA_HEAD_PAD), lambda b, h, i: (b, 0, h)),
            pl.BlockSpec((None, nc, V_ROWS, KV_CHUNK), lambda b, h, i: (b, 0, h, 0)),
        ],
        out_specs=pl.BlockSpec((None, MLA_V_DIM, tq), lambda b, h, i: (b, h, i)),
        out_shape=jax.ShapeDtypeStruct((bsz, MLA_HEADS * MLA_V_DIM, n_q * tq), BF16),
        scratch_shapes=[pltpu.VMEM((KV_CHUNK, tq), F32)] * 3,
        compiler_params=_cparams(("parallel", "arbitrary", "arbitrary"), 40),
        name="mla_attn_ctx" if ctx_only else "mla_attn",
    )(qm, km, vm)


def _mla(qm, km, vm, lat, lc):
    tq = min(MLA_TQ, lat)
    assert lc == KV_SUB and lat % lc == 0
    return (_mla_call(qm, km, vm, tq, 0, lat // tq, False),
            _mla_call(qm, km, vm, lc, lat // lc, 1, True))


def _merge_kernel(x_ref, mod_ref, yl_ref, yc_ref, yg_ref, yml_ref, ymc_ref, gt_ref, wb_ref, wo_ref, o_ref,
                  *, lat_tiles):
    d = D_MODEL
    i = pl.program_id(1)
    is_lat = i < lat_tiles
    yh = jnp.where(is_lat, yl_ref[...], yc_ref[...])
    yg = yg_ref[...].astype(F32).T.astype(BF16)
    ym = jnp.where(is_lat, yml_ref[...], ymc_ref[...]).astype(F32).T.astype(BF16)
    merged = (gt_ref[:, 0:d].astype(F32) * _dot(yh, wb_ref[0])
              + gt_ref[:, d:2 * d].astype(F32) * _dot(yg, wb_ref[1])
              + gt_ref[:, 2 * d:].astype(F32) * _dot(ym, wb_ref[2]))
    res = _dot(merged.astype(BF16), wo_ref[...])
    o_ref[...] = x_ref[...] + mod_ref[:, 2 * d:3 * d] * res


def _merge(xs, modsel, y_lat, y_ctx, yg, ym_lat, ym_ctx, gates, wb, wo, lat_tiles):
    bsz, s, d = xs.shape
    tm = ROW_TILE
    c = HY_WIDTH
    row = lambda b, i: (b, i, 0)
    col = lambda b, i: (b, 0, i)
    return pl.pallas_call(
        functools.partial(_merge_kernel, lat_tiles=lat_tiles),
        grid=(bsz, s // tm),
        in_specs=[
            pl.BlockSpec((None, tm, d), row),
            pl.BlockSpec((None, None, 1, 6 * d), lambda b, i: (b, jnp.where(i >= lat_tiles, 1, 0), 0, 0)),
            pl.BlockSpec((None, tm, c), lambda b, i: (b, jnp.minimum(i, lat_tiles - 1), 0)),
            pl.BlockSpec((None, tm, c), lambda b, i: (b, jnp.maximum(i - lat_tiles, 0), 0)),
            pl.BlockSpec((None, c, tm), col),
            pl.BlockSpec((None, c, tm), lambda b, i: (b, 0, jnp.minimum(i, lat_tiles - 1))),
            pl.BlockSpec((None, c, tm), lambda b, i: (b, 0, jnp.maximum(i - lat_tiles, 0))),
            pl.BlockSpec((None, tm, 3 * d), row),
            _const_spec(wb.shape), _const_spec(wo.shape),
        ],
        out_specs=pl.BlockSpec((None, tm, d), row),
        out_shape=jax.ShapeDtypeStruct(xs.shape, F32),
        compiler_params=_cparams(("parallel", "arbitrary"), 40),
        name="branch_merge",
    )(xs, modsel, y_lat, y_ctx, yg, ym_lat, ym_ctx, gates, wb, wo)


def _mlp_kernel(x_ref, mod_ref, g_ref, w1_ref, w2_ref, o_ref):
    d = D_MODEL
    x = x_ref[...]
    xn = x * lax.rsqrt(jnp.mean(x * x, axis=-1, keepdims=True) + EPS) * g_ref[...]
    h = (xn * (1.0 + mod_ref[:, 4 * d:5 * d]) + mod_ref[:, 3 * d:4 * d]).astype(BF16)
    acc = jnp.zeros(x.shape, F32)
    for j in range(D_FF // FF_CHUNK):
        sl = slice(j * FF_CHUNK, (j + 1) * FF_CHUNK)
        a = jnp.maximum(_dot(h, w1_ref[:, sl]), 0.0)
        acc = acc + _dot((a * a).astype(BF16), w2_ref[sl, :])
    o_ref[...] = x + mod_ref[:, 5 * d:] * acc


def _mlp(xs, modsel, g, w1, w2, lat_tiles):
    bsz, s, d = xs.shape
    tm = ROW_TILE
    row = lambda b, i: (b, i, 0)
    return pl.pallas_call(
        _mlp_kernel,
        grid=(bsz, s // tm),
        in_specs=[
            pl.BlockSpec((None, tm, d), row),
            pl.BlockSpec((None, None, 1, 6 * d), lambda b, i: (b, jnp.where(i >= lat_tiles, 1, 0), 0, 0)),
            _const_spec(g.shape), _const_spec(w1.shape), _const_spec(w2.shape),
        ],
        out_specs=pl.BlockSpec((None, tm, d), row),
        out_shape=jax.ShapeDtypeStruct(xs.shape, F32),
        compiler_params=_cparams(("parallel", "arbitrary"), 56),
        name="relu2_mlp",
    )(xs, modsel, g, w1, w2)


def _rope_tables_t(rows, dim, lc):
    n_freq = dim // 4
    inv = ROPE_BASE ** (-jnp.arange(n_freq, dtype=F32) / n_freq)
    r = jnp.repeat(jnp.arange(rows, dtype=F32), GRID_W)
    col = jnp.tile(jnp.arange(GRID_W, dtype=F32), rows)
    ang = jnp.concatenate([r[:, None] * inv, col[:, None] * inv], axis=-1)
    cos_t = jnp.concatenate([jnp.cos(ang).T, jnp.ones((dim // 2, lc), F32)], axis=1)
    sin_t = jnp.concatenate([jnp.sin(ang).T, jnp.zeros((dim // 2, lc), F32)], axis=1)
    return cos_t, sin_t


def _lane_bcast(v):
    return jnp.broadcast_to(v.astype(F32)[:, None], (v.shape[0], ROW_TILE))


def _layer_weights(l, w_in, gqa_q_norm, gqa_k_norm, mla_q_a_norm, mla_kv_a_norm, w_q_b, w_kv_b, mla_q_norm, mla_k_norm):
    w = w_in[l]
    o = np.cumsum([0, 3 * HY_WIDTH, GQA_HEADS * GQA_HEAD_DIM, GQA_KV_HEADS * GQA_HEAD_DIM,
                   GQA_KV_HEADS * GQA_HEAD_DIM, MLA_Q_RANK, MLA_KV_RANK, MLA_ROPE_DIM, 3 * D_MODEL])
    wu = w[:, o[0]:o[1]].astype(BF16)
    wt = w[:, o[1]:o[7]].T.astype(BF16)
    wg = w[:, o[7]:o[8]].astype(BF16)
    wqb = w_q_b[l].T.astype(BF16)
    wkvb = w_kv_b[l].T.astype(BF16)
    return (wu, wg, wt, wqb, wkvb,
            _lane_bcast(gqa_q_norm[l]), _lane_bcast(gqa_k_norm[l]),
            _lane_bcast(mla_q_a_norm[l]), _lane_bcast(mla_kv_a_norm[l]),
            _lane_bcast(mla_q_norm[l]), _lane_bcast(mla_k_norm[l]))


def kernel(x, c, ctx, c_ctx, w_mod, b_mod, norm_mix_g, norm_mlp_g, w_in, hy_short_w, hy_f1_w, hy_f1_b, hy_f2_w, hy_f2_b, hy_sin_freq, hy_f3_w, hy_skip, gqa_q_norm, gqa_k_norm, gqa_sink, mla_q_a_norm, mla_kv_a_norm, w_q_b, w_kv_b, mla_q_norm, mla_k_norm, w_branch, w_out, w_mlp1, w_mlp2):
    bsz, lat, d = x.shape
    lc = ctx.shape[1]
    depth = w_mod.shape[0]
    s = lat + lc
    assert d == D_MODEL and lat % GRID_W == 0 and lat % ROW_TILE == 0 and lc == ROW_TILE and s % KV_CHUNK == 0
    lat_tiles = lat // ROW_TILE

    pad = (-(bsz + 1)) % 8
    cond = jnp.concatenate([c, c_ctx[None, :], jnp.zeros((pad, d), F32)], axis=0)
    mods = _mod_all(cond, w_mod, b_mod)

    tabs = _rope_tables_t(lat // GRID_W, GQA_HEAD_DIM, lc) + _rope_tables_t(lat // GRID_W, MLA_ROPE_DIM, lc)
    xs = jnp.concatenate([x, ctx], axis=1)

    for l in range(depth):
        ml = mods[l]
        modsel = jnp.stack([ml[:bsz], jnp.broadcast_to(ml[bsz][None], (bsz, 6 * d))], axis=1)[:, :, None, :]
        wts = _layer_weights(l, w_in, gqa_q_norm, gqa_k_norm, mla_q_a_norm, mla_kv_a_norm, w_q_b, w_kv_b,
                             mla_q_norm, mla_k_norm)
        u, gates, qg, kg, vg, qm, km, vm = _proj(xs, modsel, norm_mix_g[l][None, :], wts, tabs, lat_tiles)

        f1w = jnp.zeros((V7X_LANES, HY_FILTER_WIDTH), F32).at[:HY_EMB_DIM].set(hy_f1_w[l])
        fw = (f1w, hy_f1_b[l][None, :], hy_f2_w[l], hy_f2_b[l][None, :], hy_sin_freq[l], hy_f3_w[l])
        y_lat, y_ctx = _hyena(u, hy_short_w[l], fw, hy_skip[l], lat, lc)

        sink_rows = jnp.broadcast_to(gqa_sink[l].reshape(GQA_KV_HEADS, GQA_GROUP, 1),
                                     (GQA_KV_HEADS, GQA_GROUP, GQA_TQ)).astype(F32)
        yg = _gqa(qg, kg, vg, sink_rows, lat, lc)
        ym_lat, ym_ctx = _mla(qm, km, vm, lat, lc)

        xs = _merge(xs, modsel, y_lat, y_ctx, yg, ym_lat, ym_ctx, gates, w_branch[l].astype(BF16),
                    w_out[l].astype(BF16), lat_tiles)
        xs = _mlp(xs, modsel, norm_mlp_g[l][None, :], w_mlp1[l].astype(BF16), w_mlp2[l].astype(BF16), lat_tiles)
    return xs[:, :lat]
```

```python
import functools
import math

import numpy as np
import jax
import jax.numpy as jnp
from jax import lax
from jax.experimental import pallas as pl
from jax.experimental.pallas import tpu as pltpu

D_MODEL = 1024
GRID_W = 64
HY_WIDTH = 512
HY_EMB_DIM = 33
HY_BANDS = (HY_EMB_DIM - 1) // 2
HY_FILTER_WIDTH = 64
HY_DECAY_TARGET = 1e-2
HY_FAST_DECAY = 0.3
HY_SLOW_DECAY = 1.5
GQA_HEADS = 8
GQA_KV_HEADS = 2
GQA_GROUP = GQA_HEADS // GQA_KV_HEADS
GQA_HEAD_DIM = 64
GQA_SCALE = GQA_HEAD_DIM ** -0.5
WINDOW = 128
MLA_HEADS = 8
MLA_Q_RANK = 384
MLA_KV_RANK = 256
MLA_NOPE_DIM = 64
MLA_ROPE_DIM = 32
MLA_V_DIM = 64
MLA_QK_DIM = MLA_NOPE_DIM + MLA_ROPE_DIM
MLA_SCALE = MLA_QK_DIM ** -0.5
D_FF = 4 * D_MODEL
ROPE_BASE = 10000.0
EPS = 1e-6
LOG2E = 1.4426950408889634
NEG_BIG = -1e30

V7X_LANES = 128
V7X_VMEM_BYTES = 64 * 1024 * 1024

ROW_TILE = 256
PROJ_BATCH = 2
MLA_HEAD_PAD = 128
V_ROWS = 80
KV_CHUNK = 1 * ROW_TILE
KV_SUB = 256
MLA_UNROLL = 5
MLA_TQ = 512
GQA_TQ = 256
FF_CHUNK = 1024
FFT_N2 = 128
FFT_NB = 16
FFT_KB = 8

F32 = jnp.float32
BF16 = jnp.bfloat16


def _cparams(sem, vmem_mb, flags=None):
    return pltpu.CompilerParams(dimension_semantics=sem, vmem_limit_bytes=vmem_mb * 1024 * 1024, flags=flags)


def _dot(a, b):
    return jnp.dot(a, b, preferred_element_type=F32)


def _dot_hi(a, b):
    return jnp.dot(a, b, preferred_element_type=F32, precision=lax.Precision.HIGHEST)


def _ones_row_block(width):
    r = lax.broadcasted_iota(jnp.int32, (V_ROWS - MLA_V_DIM, width), 0)
    return jnp.where(r == 0, 1.0, 0.0).astype(BF16)


def _const_spec(shape):
    nd = len(shape)
    return pl.BlockSpec(shape, lambda *_: (0,) * nd, pipeline_mode=pl.Buffered(1))


def _mod_kernel(c_ref, w_ref, b_ref, o_ref):
    c = c_ref[...]
    s = c * jax.nn.sigmoid(c)
    o_ref[...] = _dot_hi(s, w_ref[...]) + b_ref[...]


def _mod_all(cond, w_mod, b_mod):
    depth, d, n = w_mod.shape
    rows = cond.shape[0]
    tn = 1536
    return pl.pallas_call(
        _mod_kernel,
        grid=(depth, n // tn),
        in_specs=[
            pl.BlockSpec((rows, d), lambda l, j: (0, 0)),
            pl.BlockSpec((None, d, tn), lambda l, j: (l, 0, j)),
            pl.BlockSpec((None, 1, tn), lambda l, j: (l, 0, j)),
        ],
        out_specs=pl.BlockSpec((None, rows, tn), lambda l, j: (l, 0, j)),
        out_shape=jax.ShapeDtypeStruct((depth, rows, n), F32),
        compiler_params=_cparams(("arbitrary", "arbitrary"), 40),
        name="adaln_mod",
    )(cond, w_mod, b_mod.reshape(depth, 1, n))


def _rms_rows(x, n):
    return lax.rsqrt(jnp.sum(x * x, axis=0, keepdims=True) * (1.0 / n) + EPS)


def _rope_rows(x1, x2, cs, sn):
    return x1 * cs - x2 * sn, x1 * sn + x2 * cs


def _proj_kernel(x_ref, mod_ref, *refs):
    consts, outs = refs[:16], refs[16:]
    for n in range(x_ref.shape[0]):
        _proj_one(x_ref.at[n], mod_ref.at[n], *consts, *[o.at[n] for o in outs])


def _proj_one(x_ref, mod_ref, g_ref, wu_ref, wg_ref, wt_ref, wqb_ref, wkvb_ref,
              gq_ref, gk_ref, gqa_ref, gkva_ref, gmq_ref, gmk_ref,
              cg_ref, sg_ref, cm_ref, sm_ref,
              u_ref, gate_ref, qg_ref, kg_ref, vg_ref, qm_ref, km_ref, vm_ref):
    d = D_MODEL
    x = x_ref[...]
    tm = x.shape[0]
    shift = mod_ref[:, 0:d]
    scale = mod_ref[:, d:2 * d]
    xn = x * lax.rsqrt(jnp.mean(x * x, axis=-1, keepdims=True) + EPS) * g_ref[...]
    h = xn * (1.0 + scale) + shift
    hb = h.astype(BF16)

    u_ref[...] = _dot(hb, wu_ref[...]).astype(BF16)
    gate_ref[...] = jax.nn.sigmoid(_dot(hb, wg_ref[...])).astype(BF16)

    ht = h.T.astype(BF16)
    t = _dot(wt_ref[...], ht)
    o_q, o_k, o_v = 0, 512, 640
    o_cq, o_ckv, o_kr = 768, 768 + MLA_Q_RANK, 768 + MLA_Q_RANK + MLA_KV_RANK

    cg, sg = cg_ref[...], sg_ref[...]
    cm, sm = cm_ref[...], sm_ref[...]
    hd, hh = GQA_HEAD_DIM, GQA_HEAD_DIM // 2

    gq = gq_ref[...]
    for n in range(GQA_HEADS):
        xh = t[o_q + n * hd:o_q + (n + 1) * hd]
        xh = xh * _rms_rows(xh, hd) * gq
        a, b = _rope_rows(xh[:hh], xh[hh:], cg, sg)
        qg_ref[n * hd:n * hd + hh, :] = (a * (GQA_SCALE * LOG2E)).astype(BF16)
        qg_ref[n * hd + hh:(n + 1) * hd, :] = (b * (GQA_SCALE * LOG2E)).astype(BF16)

    gk = gk_ref[...]
    zpad = jnp.zeros((V7X_LANES - hd, tm), F32)
    parts = []
    for n in range(GQA_KV_HEADS):
        xh = t[o_k + n * hd:o_k + (n + 1) * hd]
        xh = xh * _rms_rows(xh, hd) * gk
        a, b = _rope_rows(xh[:hh], xh[hh:], cg, sg)
        parts += [a, b, zpad]
    kg_ref[...] = jnp.concatenate(parts, axis=0).T.astype(BF16)
    ones_rows = _ones_row_block(tm)
    for n in range(GQA_KV_HEADS):
        vg_ref[n * V_ROWS:n * V_ROWS + hd, :] = t[o_v + n * hd:o_v + (n + 1) * hd].astype(BF16)
        vg_ref[n * V_ROWS + hd:(n + 1) * V_ROWS, :] = ones_rows

    cq = t[o_cq:o_cq + MLA_Q_RANK]
    cqn = (cq * _rms_rows(cq, MLA_Q_RANK) * gqa_ref[...]).astype(BF16)
    qm = _dot(wqb_ref[...], cqn)
    gmq = gmq_ref[...]
    nd, rh = MLA_NOPE_DIM, MLA_ROPE_DIM // 2
    qpad = jnp.zeros((MLA_HEAD_PAD - MLA_QK_DIM, tm), BF16)
    for n in range(MLA_HEADS):
        xh = qm[n * MLA_QK_DIM:(n + 1) * MLA_QK_DIM]
        xh = xh * _rms_rows(xh, MLA_QK_DIM) * gmq * (MLA_SCALE * LOG2E)
        a, b = _rope_rows(xh[nd:nd + rh], xh[nd + rh:], cm, sm)
        base = n * MLA_HEAD_PAD
        qm_ref[base:base + nd, :] = xh[:nd].astype(BF16)
        qm_ref[base + nd:base + nd + rh, :] = a.astype(BF16)
        qm_ref[base + nd + rh:base + MLA_QK_DIM, :] = b.astype(BF16)
        qm_ref[base + MLA_QK_DIM:base + MLA_HEAD_PAD, :] = qpad

    ckv = t[o_ckv:o_ckv + MLA_KV_RANK]
    ckvn = (ckv * _rms_rows(ckv, MLA_KV_RANK) * gkva_ref[...]).astype(BF16)
    kv = _dot(wkvb_ref[...], ckvn)
    kr = t[o_kr:o_kr + MLA_ROPE_DIM]
    kr_ss = jnp.sum(kr * kr, axis=0, keepdims=True)
    gmk = gmk_ref[...]
    kzero = jnp.zeros((MLA_HEAD_PAD - MLA_QK_DIM, tm), F32)
    ones_rows = _ones_row_block(tm)
    parts = []
    for n in range(MLA_HEADS):
        kn = kv[n * 128:n * 128 + nd]
        vm_ref[n * V_ROWS:n * V_ROWS + MLA_V_DIM, :] = kv[n * 128 + nd:(n + 1) * 128].astype(BF16)
        vm_ref[n * V_ROWS + MLA_V_DIM:(n + 1) * V_ROWS, :] = ones_rows
        rs = lax.rsqrt((jnp.sum(kn * kn, axis=0, keepdims=True) + kr_ss) * (1.0 / MLA_QK_DIM) + EPS)
        krn = kr * rs * gmk[nd:]
        a, b = _rope_rows(krn[:rh], krn[rh:], cm, sm)
        parts += [kn * rs * gmk[:nd], a, b, kzero]
    km_ref[...] = jnp.concatenate(parts, axis=0).T.astype(BF16)


def _proj(xs, modsel, g, wts, tabs, lat_tiles):
    bsz, s, d = xs.shape
    tm = ROW_TILE
    nt = s // tm
    ncs = KV_CHUNK // tm
    nb = PROJ_BATCH if bsz % PROJ_BATCH == 0 else 1
    (wu, wg, wt, wqb, wkvb, gq, gk, gqa, gkva, gmq, gmk) = wts
    cg, sg, cm, sm = tabs
    row = lambda b, i: (b, i, 0)
    col = lambda b, i: (b, 0, i)
    tab = lambda b, i: (0, i)
    in_specs = [
        pl.BlockSpec((nb, tm, d), row),
        pl.BlockSpec((nb, None, 1, 6 * d), lambda b, i: (b, jnp.where(i >= lat_tiles, 1, 0), 0, 0)),
        _const_spec(g.shape), _const_spec(wu.shape), _const_spec(wg.shape), _const_spec(wt.shape),
        _const_spec(wqb.shape), _const_spec(wkvb.shape),
        _const_spec(gq.shape), _const_spec(gk.shape), _const_spec(gqa.shape), _const_spec(gkva.shape),
        _const_spec(gmq.shape), _const_spec(gmk.shape),
        pl.BlockSpec((cg.shape[0], tm), tab), pl.BlockSpec((sg.shape[0], tm), tab),
        pl.BlockSpec((cm.shape[0], tm), tab), pl.BlockSpec((sm.shape[0], tm), tab),
    ]
    out_shape = [
        jax.ShapeDtypeStruct((bsz, s, 3 * HY_WIDTH), BF16),
        jax.ShapeDtypeStruct((bsz, s, 3 * d), BF16),
        jax.ShapeDtypeStruct((bsz, GQA_HEADS * GQA_HEAD_DIM, s), BF16),
        jax.ShapeDtypeStruct((bsz, s, GQA_KV_HEADS * V7X_LANES), BF16),
        jax.ShapeDtypeStruct((bsz, GQA_KV_HEADS * V_ROWS, s), BF16),
        jax.ShapeDtypeStruct((bsz, MLA_HEADS * MLA_HEAD_PAD, s), BF16),
        jax.ShapeDtypeStruct((bsz, s, MLA_HEADS * MLA_HEAD_PAD), BF16),
        jax.ShapeDtypeStruct((bsz, s // KV_CHUNK, MLA_HEADS * V_ROWS, KV_CHUNK), BF16),
    ]
    out_specs = [
        pl.BlockSpec((nb, tm, 3 * HY_WIDTH), row),
        pl.BlockSpec((nb, tm, 3 * d), row),
        pl.BlockSpec((nb, GQA_HEADS * GQA_HEAD_DIM, tm), col),
        pl.BlockSpec((nb, tm, GQA_KV_HEADS * V7X_LANES), row),
        pl.BlockSpec((nb, GQA_KV_HEADS * V_ROWS, tm), col),
        pl.BlockSpec((nb, MLA_HEADS * MLA_HEAD_PAD, tm), col),
        pl.BlockSpec((nb, tm, MLA_HEADS * MLA_HEAD_PAD), row),
        pl.BlockSpec((nb, None, MLA_HEADS * V_ROWS, tm), lambda b, i: (b, i // ncs, 0, i % ncs)),
    ]
    return pl.pallas_call(
        _proj_kernel,
        grid=(bsz // nb, nt),
        in_specs=in_specs,
        out_specs=out_specs,
        out_shape=out_shape,
        compiler_params=_cparams(("parallel", "arbitrary"), 56),
        name="in_proj",
    )(xs, modsel, g, wu, wg, wt, wqb, wkvb, gq, gk, gqa, gkva, gmq, gmk, cg, sg, cm, sm)


def _short_conv_kernel(u_ref, up_ref, un_ref, w_ref, x0_ref, z_ref, *, n_tiles):
    i = pl.program_id(1)
    u = u_ref[...].astype(F32)
    tm = u.shape[0]
    prev = jnp.where(i > 0, up_ref[7:8, :].astype(F32), 0.0)
    nxt = jnp.where(i < n_tiles - 1, un_ref[0:1, :].astype(F32), 0.0)
    ridx = lax.broadcasted_iota(jnp.int32, u.shape, 0)
    up = jnp.where(ridx == 0, prev, pltpu.roll(u, 1, axis=0))
    dn = jnp.where(ridx == tm - 1, nxt, pltpu.roll(u, tm - 1, axis=0))
    uc = up * w_ref[0:1, :] + u * w_ref[1:2, :] + dn * w_ref[2:3, :]
    c = HY_WIDTH
    x0_ref[...] = uc[:, :c].astype(BF16)
    z_ref[...] = (uc[:, c:2 * c] * uc[:, 2 * c:]).astype(BF16)


def _short_conv(u, short_w, row0, rows):
    bsz, s, c3 = u.shape
    tm = ROW_TILE
    nt = rows // tm
    t0 = row0 // tm
    r8 = tm // 8
    last8 = s // 8 - 1
    return pl.pallas_call(
        functools.partial(_short_conv_kernel, n_tiles=nt),
        grid=(bsz, nt),
        in_specs=[
            pl.BlockSpec((None, tm, c3), lambda b, i: (b, t0 + i, 0)),
            pl.BlockSpec((None, 8, c3), lambda b, i: (b, jnp.maximum((t0 + i) * r8 - 1, 0), 0)),
            pl.BlockSpec((None, 8, c3), lambda b, i: (b, jnp.minimum((t0 + i + 1) * r8, last8), 0)),
            _const_spec(short_w.shape),
        ],
        out_specs=[pl.BlockSpec((None, tm, HY_WIDTH), lambda b, i: (b, i, 0))] * 2,
        out_shape=[jax.ShapeDtypeStruct((bsz, rows, HY_WIDTH), BF16)] * 2,
        compiler_params=_cparams(("parallel", "arbitrary"), 32),
        name="hyena_short_conv",
    )(u, u, u, short_w)


def _filter_kernel(zf_ref, w1_ref, b1_ref, w2_ref, b2_ref, fr_ref, w3_ref, dl_ref, h_ref, ss_ref):
    i = pl.program_id(0)
    zf = zf_ref[...]
    tl = zf.shape[0]
    h = jnp.sin(fr_ref[0:1, :] * (_dot_hi(zf, w1_ref[...]) + b1_ref[...]))
    h = jnp.sin(fr_ref[1:2, :] * (_dot_hi(h, w2_ref[...]) + b2_ref[...]))
    h = _dot_hi(h, w3_ref[...])
    decay = jnp.exp(-zf[:, 0:1] * dl_ref[...])
    c = HY_WIDTH
    hf = h[:, :c] * decay
    ridx = lax.broadcasted_iota(jnp.int32, (tl, c), 0) + i * tl
    hb = jnp.where(ridx == 0, 0.0, h[:, c:] * decay)
    h_ref[0] = hf.astype(BF16)
    h_ref[1] = hb.astype(BF16)
    ss = jnp.sum(hf * hf + hb * hb, axis=0, keepdims=True)

    @pl.when(i == 0)
    def _():
        ss_ref[...] = ss

    @pl.when(i > 0)
    def _():
        ss_ref[...] += ss


def _filter(zfeat, fw, deltas):
    length = zfeat.shape[0]
    tl = min(length, 1024)
    w1, b1, w2, b2, fr, w3 = fw
    return pl.pallas_call(
        _filter_kernel,
        grid=(length // tl,),
        in_specs=[pl.BlockSpec((tl, zfeat.shape[1]), lambda i: (i, 0))]
        + [_const_spec(a.shape) for a in (w1, b1, w2, b2, fr, w3, deltas)],
        out_specs=[pl.BlockSpec((2, tl, HY_WIDTH), lambda i: (0, i, 0)),
                   pl.BlockSpec((1, HY_WIDTH), lambda i: (0, 0))],
        out_shape=[jax.ShapeDtypeStruct((2, length, HY_WIDTH), BF16),
                   jax.ShapeDtypeStruct((1, HY_WIDTH), F32)],
        compiler_params=_cparams(("arbitrary",), 40),
        name="hyena_filter",
    )(zfeat, w1, b1, w2, b2, fr, w3, deltas)


def _fft_a_kernel(g_ref, x_ref, o_ref, *, nb, cw):
    for j in range(nb):
        sl = slice(j * cw, (j + 1) * cw)
        o_ref[:, sl] = _dot(g_ref[j], x_ref[:, sl]).astype(o_ref.dtype)


def _fft_a(x2d, gmat):
    bx, k1, w = x2d.shape
    n2, two_n1, _ = gmat.shape
    cw = w // n2
    nb = FFT_NB
    return pl.pallas_call(
        functools.partial(_fft_a_kernel, nb=nb, cw=cw),
        grid=(bx, n2 // nb),
        in_specs=[pl.BlockSpec((nb, two_n1, k1), lambda b, j: (j, 0, 0)),
                  pl.BlockSpec((None, k1, nb * cw), lambda b, j: (b, 0, j))],
        out_specs=pl.BlockSpec((None, two_n1, nb * cw), lambda b, j: (b, 0, j)),
        out_shape=jax.ShapeDtypeStruct((bx, two_n1, w), BF16),
        compiler_params=_cparams(("parallel", "arbitrary"), 40),
        name="fft_stage_a",
    )(gmat, x2d)


def _fft_filter_b_kernel(fb_ref, a_ref, ss_ref, kf_ref, *, inv_n):
    n1 = a_ref.shape[2]
    xf = _dot(fb_ref[...], jnp.concatenate([a_ref[0, 0], a_ref[0, 1]], axis=0))
    xb = _dot(fb_ref[...], jnp.concatenate([a_ref[1, 0], a_ref[1, 1]], axis=0))
    rs = lax.rsqrt(ss_ref[...] + EPS) * inv_n
    kf_ref[0] = (xf[:n1] + xb[:n1]) * rs
    kf_ref[1] = (xf[n1:] - xb[n1:]) * rs


def _fft_filter_b(a5, fb, ssq, inv_n):
    _, _, n1, n2, c = a5.shape
    return pl.pallas_call(
        functools.partial(_fft_filter_b_kernel, inv_n=inv_n),
        grid=(n1,),
        in_specs=[_const_spec(fb.shape),
                  pl.BlockSpec((2, 2, None, n2, c), lambda k: (0, 0, k, 0, 0)),
                  _const_spec(ssq.shape)],
        out_specs=pl.BlockSpec((2, None, n2, c), lambda k: (0, k, 0, 0)),
        out_shape=jax.ShapeDtypeStruct((2, n1, n2, c), F32),
        compiler_params=_cparams(("arbitrary",), 32),
        name="fft_filter_stage_b",
    )(fb, a5, ssq)


def _fft_b_kernel(fb_ref, fbi_ref, a_ref, kf_ref, o_ref):
    n = a_ref.shape[2]
    for r in range(a_ref.shape[1]):
        x = _dot(fb_ref[...], jnp.concatenate([a_ref[0, r], a_ref[1, r]], axis=0))
        xr, xi = x[:n], x[n:]
        kr, ki = kf_ref[0, r], kf_ref[1, r]
        y = jnp.concatenate([xr * kr - xi * ki, xr * ki + xi * kr], axis=0).astype(BF16)
        c = _dot(fbi_ref[...], y)
        o_ref[0, r] = c[:n].astype(o_ref.dtype)
        o_ref[1, r] = c[n:].astype(o_ref.dtype)


def _fft_b(a5, kf, fb, fbi):
    bsz, _, n1, n2, c = a5.shape
    kb = min(FFT_KB, n1)
    return pl.pallas_call(
        _fft_b_kernel,
        grid=(n1 // kb, bsz),
        in_specs=[_const_spec(fb.shape), _const_spec(fbi.shape),
                  pl.BlockSpec((None, 2, kb, n2, c), lambda k, b: (b, 0, k, 0, 0)),
                  pl.BlockSpec((2, kb, n2, c), lambda k, b: (0, k, 0, 0))],
        out_specs=pl.BlockSpec((None, 2, kb, n2, c), lambda k, b: (b, 0, k, 0, 0)),
        out_shape=jax.ShapeDtypeStruct(a5.shape, BF16),
        compiler_params=_cparams(("arbitrary", "arbitrary"), 32),
        name="fft_stage_b",
    )(fb, fbi, a5, kf)


def _fft_c_kernel(h_ref, c_ref, o_ref, *, nb, cw):
    for j in range(nb):
        sl = slice(j * cw, (j + 1) * cw)
        o_ref[:, sl] = _dot(h_ref[j], c_ref[:, sl]).astype(o_ref.dtype)


def _fft_c(c2d, hmat):
    bsz, two_n1, w = c2d.shape
    n2, k1, _ = hmat.shape
    cw = w // n2
    nb = FFT_NB
    blk = lambda b, j: (b, 0, j)
    return pl.pallas_call(
        functools.partial(_fft_c_kernel, nb=nb, cw=cw),
        grid=(bsz, n2 // nb),
        in_specs=[pl.BlockSpec((nb, k1, two_n1), lambda b, j: (j, 0, 0)),
                  pl.BlockSpec((None, two_n1, nb * cw), blk)],
        out_specs=pl.BlockSpec((None, k1, nb * cw), blk),
        out_shape=jax.ShapeDtypeStruct((bsz, k1, w), BF16),
        compiler_params=_cparams(("parallel", "arbitrary"), 40),
        name="fft_stage_c",
    )(hmat, c2d)


def _ctx_conv_kernel(fc_ref, fci_ref, h_ref, ss_ref, z_ref, o_ref):
    n = fc_ref.shape[0] // 2
    fc = fc_ref[...]
    kf = _dot(fc, h_ref[0])
    kb = _dot(fc, h_ref[1])
    rs = lax.rsqrt(ss_ref[...] + EPS)
    kr = (kf[:n] + kb[:n]) * rs
    ki = (kf[n:] - kb[n:]) * rs
    x = _dot(fc, z_ref[...])
    xr, xi = x[:n], x[n:]
    y = jnp.concatenate([xr * kr - xi * ki, xr * ki + xi * kr], axis=0).astype(BF16)
    o_ref[...] = _dot(fci_ref[...], y).astype(o_ref.dtype)


def _ctx_conv(fc, fci, hfb, ssq, z):
    bsz, lc, c = z.shape
    blk = pl.BlockSpec((None, lc, c), lambda b: (b, 0, 0))
    return pl.pallas_call(
        _ctx_conv_kernel,
        grid=(bsz,),
        in_specs=[_const_spec(fc.shape), _const_spec(fci.shape), _const_spec(hfb.shape),
                  _const_spec(ssq.shape), blk],
        out_specs=blk,
        out_shape=jax.ShapeDtypeStruct((bsz, lc, c), BF16),
        compiler_params=_cparams(("arbitrary",), 32),
        name="hyena_ctx_conv",
    )(fc, fci, hfb, ssq, z)


@functools.lru_cache(maxsize=None)
def _fft_tables(length):
    n = 2 * length
    n2 = FFT_N2
    n1 = n // n2
    k1 = length // n2
    kk = np.arange(n1)[:, None]
    g = np.empty((n2, 2 * n1, k1), np.float64)
    h = np.empty((n2, k1, 2 * n1), np.float64)
    nn = np.arange(k1)[None, :]
    for j in range(n2):
        ang = 2.0 * np.pi * (((n2 * nn * kk) % n) + (j * kk) % n) / n
        g[j, :n1] = np.cos(ang)
        g[j, n1:] = -np.sin(ang)
        h[j, :, :n1] = np.cos(ang).T
        h[j, :, n1:] = -np.sin(ang).T
    a = np.arange(n2)
    ph = 2.0 * np.pi * ((a[:, None] * a[None, :]) % n2) / n2
    c, s = np.cos(ph), np.sin(ph)
    fb = np.block([[c, s], [-s, c]])
    fbi = np.block([[c, -s], [s, c]])
    return (jnp.asarray(g, BF16), jnp.asarray(h, BF16), jnp.asarray(fb, BF16), jnp.asarray(fbi, BF16), n1, k1)


@functools.lru_cache(maxsize=None)
def _dft_tables(length):
    n = 2 * length
    k = np.arange(n)[:, None]
    t = np.arange(length)[None, :]
    ang = 2.0 * np.pi * ((k * t) % n) / n
    fc = np.concatenate([np.cos(ang), -np.sin(ang)], axis=0)
    fci = np.concatenate([np.cos(ang).T, -np.sin(ang).T], axis=1) / n
    return jnp.asarray(fc, BF16), jnp.asarray(fci, BF16)


@functools.lru_cache(maxsize=None)
def _filter_features(length):
    t = np.linspace(0.0, 1.0, length, dtype=np.float32)[:, None]
    w = (2.0 * math.pi * np.arange(length, dtype=np.float32)[:, None] / length).astype(np.float32)
    f = np.linspace(1e-4, HY_BANDS - 1, HY_BANDS, dtype=np.float32)[None, :]
    z = np.concatenate([t, np.cos(f * w), -np.sin(f * w)], axis=-1).astype(np.float32)
    zp = np.zeros((length, V7X_LANES), np.float32)
    zp[:, :HY_EMB_DIM] = z
    return jnp.asarray(zp)


def _hyena_deltas():
    max_decay = math.log(HY_DECAY_TARGET) / HY_FAST_DECAY
    min_decay = math.log(HY_DECAY_TARGET) / HY_SLOW_DECAY
    return jnp.abs(jnp.linspace(min_decay, max_decay, HY_WIDTH, dtype=F32))[None, :]


def _hyena(u, short_w, fw, lat, lc):
    bsz = u.shape[0]
    c = HY_WIDTH
    deltas = _hyena_deltas()
    x0, z = _short_conv(u, short_w, 0, lat)
    gmat, hmat, fb, fbi, n1, k1 = _fft_tables(lat)
    n2 = FFT_N2
    hfb, ssq = _filter(_filter_features(lat), fw, deltas)
    fa = _fft_a(hfb.reshape(2, k1, n2 * c), gmat)
    kf = _fft_filter_b(fa.reshape(2, 2, n1, n2, c), fb, ssq, 1.0 / (2 * lat))
    za = _fft_a(z.reshape(bsz, k1, n2 * c), gmat)
    zc = _fft_b(za.reshape(bsz, 2, n1, n2, c), kf, fb, fbi)
    conv = _fft_c(zc.reshape(bsz, 2 * n1, n2 * c), hmat).reshape(bsz, lat, c)
    x0_c, z_c = _short_conv(u, short_w, lat, lc)
    hfb_c, ssq_c = _filter(_filter_features(lc), fw, deltas)
    fc, fci = _dft_tables(lc)
    conv_c = _ctx_conv(fc, fci, hfb_c, ssq_c, z_c)
    return (conv, x0, z), (conv_c, x0_c, z_c)


def _gqa_kernel(q_ref, k_ref, v_ref, sink_ref, o_ref, s_a, s_b, s_c, *, lat, lc):
    i = pl.program_id(1)
    tq = q_ref.shape[1]
    s_len = k_ref.shape[0]
    hd = GQA_HEAD_DIM
    w = WINDOW
    nwin = tq + 2 * w
    start = pl.multiple_of(jnp.clip(i * tq - w, 0, s_len - nwin), w)
    k_all = jnp.concatenate([k_ref[pl.ds(start, nwin), :], k_ref[lat:lat + lc, :]], axis=0)
    v_all = jnp.concatenate([v_ref[:, pl.ds(start, nwin)], v_ref[:, lat:lat + lc]], axis=1)
    q_pos = i * tq + lax.broadcasted_iota(jnp.int32, (1, tq), 1)
    k_pos = start + lax.broadcasted_iota(jnp.int32, (nwin, 1), 0)
    bias = (jnp.where(jnp.abs(k_pos - q_pos) <= w, 0.0, NEG_BIG)
            + jnp.where(k_pos < lat, 0.0, NEG_BIG)
            + jnp.where(q_pos < lat, 0.0, NEG_BIG))
    bias = jnp.concatenate([bias, jnp.zeros((lc, tq), F32)], axis=0)
    bias2 = jnp.concatenate([bias, bias], axis=1)
    zq = jnp.zeros((V7X_LANES - hd, 2 * tq), BF16)

    def score(pair, dst):
        g = pair // (GQA_GROUP // 2)
        r = 2 * pair * hd
        q2 = jnp.concatenate([q_ref[r:r + hd, :], q_ref[r + hd:r + 2 * hd, :]], axis=1)
        s = _dot(k_all[:, g * V7X_LANES:(g + 1) * V7X_LANES], jnp.concatenate([q2, zq], axis=0)) + bias2
        dst[...] = s
        return jnp.max(s, axis=0, keepdims=True)

    def update(pair, s_sc, s_max):
        g = pair // (GQA_GROUP // 2)
        r = 2 * pair * hd
        sink = jnp.concatenate([sink_ref[2 * pair:2 * pair + 1, :], sink_ref[2 * pair + 1:2 * pair + 2, :]],
                               axis=1) * LOG2E
        m = jnp.maximum(s_max, sink)
        p = jnp.exp2(s_sc[...] - m).astype(BF16)
        pv = _dot(v_all[g * V_ROWS:(g + 1) * V_ROWS, :], p)
        o = pv[:hd] / (pv[hd:hd + 1] + jnp.exp2(sink - m))
        o_ref[r:r + hd, :] = o[:, :tq].astype(o_ref.dtype)
        o_ref[r + hd:r + 2 * hd, :] = o[:, tq:].astype(o_ref.dtype)

    bufs = (s_a, s_b, s_c)
    npair = GQA_HEADS // 2
    mx = {0: score(0, bufs[0]), 1: score(1, bufs[1])}
    for j in range(npair):
        if j + 2 < npair:
            mx[j + 2] = score(j + 2, bufs[(j + 2) % 3])
        update(j, bufs[j % 3], mx[j])


def _gqa(qg, kg, vg, sink_rows, lat, lc):
    bsz, nq, s = qg.shape
    tq = GQA_TQ
    return pl.pallas_call(
        functools.partial(_gqa_kernel, lat=lat, lc=lc),
        grid=(bsz, s // tq),
        in_specs=[
            pl.BlockSpec((None, nq, tq), lambda b, i: (b, 0, i)),
            pl.BlockSpec((None, s, kg.shape[2]), lambda b, i: (b, 0, 0)),
            pl.BlockSpec((None, vg.shape[1], s), lambda b, i: (b, 0, 0)),
            _const_spec(sink_rows.shape),
        ],
        out_specs=pl.BlockSpec((None, nq, tq), lambda b, i: (b, 0, i)),
        out_shape=jax.ShapeDtypeStruct((bsz, nq, s), BF16),
        scratch_shapes=[pltpu.VMEM((tq + 2 * WINDOW + lc, 2 * tq), F32)] * 3,
        compiler_params=_cparams(("parallel", "arbitrary"), 40),
        name="gqa_window_attn",
    )(qg, kg, vg, sink_rows)


def _mla_update(s_ref, s_max, vt, m, acc):
    m_new = jnp.maximum(m, s_max)
    alpha = jnp.exp2(m - m_new)
    p = jnp.exp2(s_ref[...] - m_new).astype(BF16)
    acc = alpha * acc + _dot(vt, p)
    return m_new, acc


def _mla_kernel(q_ref, k_ref, v_ref, o_ref, s_a, s_b, s_c, *, nc, ctx_only):
    q = q_ref[...]
    tq = q.shape[1]
    ck = KV_CHUNK
    m = jnp.full((1, tq), NEG_BIG, F32)
    acc = jnp.zeros((V_ROWS, tq), F32)

    def score(k, dst):
        s = _dot(k, q)
        dst[...] = s
        return jnp.max(s, axis=0, keepdims=True)

    def score_chunk(j, dst):
        return score(k_ref[pl.ds(pl.multiple_of(j * ck, ck), ck), :], dst)

    if ctx_only:
        s_x = s_a.at[0:KV_SUB, :]
        mx = score(k_ref[nc * ck - KV_SUB:nc * ck, :], s_x)
        m, acc = _mla_update(s_x, mx, v_ref[nc - 1][:, ck - KV_SUB:], m, acc)
    elif nc == 1:
        m, acc = _mla_update(s_a, score_chunk(0, s_a), v_ref[0], m, acc)
    else:
        mx_a = score_chunk(0, s_a)
        mx_b = score_chunk(1, s_b)

        def triple(j, c):
            m, acc, mx_a, mx_b = c
            mx_c = score_chunk(j + 2, s_c)
            m, acc = _mla_update(s_a, mx_a, v_ref[j], m, acc)
            mx_a = score_chunk(j + 3, s_a)
            m, acc = _mla_update(s_b, mx_b, v_ref[j + 1], m, acc)
            mx_b = score_chunk(j + 4, s_b)
            m, acc = _mla_update(s_c, mx_c, v_ref[j + 2], m, acc)
            return m, acc, mx_a, mx_b

        def triples(i, c):
            for u in range(MLA_UNROLL):
                c = triple(3 * (MLA_UNROLL * i + u), c)
            return c

        nt = (nc - 2) // 3
        c = (m, acc, mx_a, mx_b)
        if nt >= MLA_UNROLL:
            c = lax.fori_loop(0, nt // MLA_UNROLL, triples, c)
        for t in range(nt - nt % MLA_UNROLL, nt):
            c = triple(3 * t, c)
        m, acc, mx_a, mx_b = c
        j = 3 * nt
        left = nc - j
        if left >= 3:
            mx_c = score_chunk(j + 2, s_c)
        m, acc = _mla_update(s_a, mx_a, v_ref[j], m, acc)
        if left == 4:
            mx_a = score_chunk(j + 3, s_a)
        m, acc = _mla_update(s_b, mx_b, v_ref[j + 1], m, acc)
        if left >= 3:
            m, acc = _mla_update(s_c, mx_c, v_ref[j + 2], m, acc)
        if left == 4:
            m, acc = _mla_update(s_a, mx_a, v_ref[j + 3], m, acc)
    o_ref[...] = (acc[:MLA_V_DIM] / acc[MLA_V_DIM:MLA_V_DIM + 1]).astype(o_ref.dtype)


def _mla_call(qm, km, vm, tq, q_tile0, n_q, ctx_only):
    bsz, _, s = qm.shape
    nc = vm.shape[1]
    return pl.pallas_call(
        functools.partial(_mla_kernel, nc=nc, ctx_only=ctx_only),
        grid=(bsz, MLA_HEADS, n_q),
        in_specs=[
            pl.BlockSpec((None, MLA_HEAD_PAD, tq), lambda b, h, i: (b, h, q_tile0 + i)),
            pl.BlockSpec((None, s, MLA_HEAD_PAD), lambda b, h, i: (b, 0, h)),
            pl.BlockSpec((None, nc, V_ROWS, KV_CHUNK), lambda b, h, i: (b, 0, h, 0)),
        ],
        out_specs=pl.BlockSpec((None, MLA_V_DIM, tq), lambda b, h, i: (b, h, i)),
        out_shape=jax.ShapeDtypeStruct((bsz, MLA_HEADS * MLA_V_DIM, n_q * tq), BF16),
        scratch_shapes=[pltpu.VMEM((KV_CHUNK, tq), F32)] * 3,
        compiler_params=_cparams(("parallel", "arbitrary", "arbitrary"), 40),
        name="mla_attn_ctx" if ctx_only else "mla_attn",
    )(qm, km, vm)


def _mla(qm, km, vm, lat, lc):
    tq = min(MLA_TQ, lat)
    assert lc == KV_SUB and lat % lc == 0
    return (_mla_call(qm, km, vm, tq, 0, lat // tq, False),
            _mla_call(qm, km, vm, lc, lat // lc, 1, True))


def _merge_kernel(x_ref, mod_ref, hl_refs, hc_refs, sk_ref, yg_ref, yml_ref, ymc_ref, gt_ref, wb_ref, wo_ref, o_ref,
                  *, lat_tiles):
    d = D_MODEL
    is_lat = pl.program_id(1) < lat_tiles
    conv, x0, z = [jnp.where(is_lat, a[...], b[...]).astype(F32) for a, b in zip(hl_refs, hc_refs)]
    yh = (x0 * (conv + z * sk_ref[...])).astype(BF16)
    yg = yg_ref[...].astype(F32).T.astype(BF16)
    ym = jnp.where(is_lat, yml_ref[...], ymc_ref[...]).astype(F32).T.astype(BF16)
    merged = (gt_ref[:, 0:d].astype(F32) * _dot(yh, wb_ref[0])
              + gt_ref[:, d:2 * d].astype(F32) * _dot(yg, wb_ref[1])
              + gt_ref[:, 2 * d:].astype(F32) * _dot(ym, wb_ref[2]))
    res = _dot(merged.astype(BF16), wo_ref[...])
    o_ref[...] = x_ref[...] + mod_ref[:, 2 * d:3 * d] * res


def _merge(xs, modsel, hy_lat, hy_ctx, skip, yg, ym_lat, ym_ctx, gates, wb, wo, lat_tiles):
    bsz, s, d = xs.shape
    tm = ROW_TILE
    c = HY_WIDTH
    row = lambda b, i: (b, i, 0)
    col = lambda b, i: (b, 0, i)
    lat_row = pl.BlockSpec((None, tm, c), lambda b, i: (b, jnp.minimum(i, lat_tiles - 1), 0))
    ctx_row = pl.BlockSpec((None, tm, c), lambda b, i: (b, jnp.maximum(i - lat_tiles, 0), 0))
    return pl.pallas_call(
        functools.partial(_merge_kernel, lat_tiles=lat_tiles),
        grid=(bsz, s // tm),
        in_specs=[
            pl.BlockSpec((None, tm, d), row),
            pl.BlockSpec((None, None, 1, 6 * d), lambda b, i: (b, jnp.where(i >= lat_tiles, 1, 0), 0, 0)),
            [lat_row] * 3, [ctx_row] * 3, _const_spec(skip.shape),
            pl.BlockSpec((None, c, tm), col),
            pl.BlockSpec((None, c, tm), lambda b, i: (b, 0, jnp.minimum(i, lat_tiles - 1))),
            pl.BlockSpec((None, c, tm), lambda b, i: (b, 0, jnp.maximum(i - lat_tiles, 0))),
            pl.BlockSpec((None, tm, 3 * d), row),
            _const_spec(wb.shape), _const_spec(wo.shape),
        ],
        out_specs=pl.BlockSpec((None, tm, d), row),
        out_shape=jax.ShapeDtypeStruct(xs.shape, F32),
        compiler_params=_cparams(("parallel", "arbitrary"), 40),
        name="branch_merge",
    )(xs, modsel, list(hy_lat), list(hy_ctx), skip, yg, ym_lat, ym_ctx, gates, wb, wo)


def _mlp_kernel(x_ref, mod_ref, g_ref, w1_ref, w2_ref, o_ref):
    d = D_MODEL
    x = x_ref[...]
    xn = x * lax.rsqrt(jnp.mean(x * x, axis=-1, keepdims=True) + EPS) * g_ref[...]
    h = (xn * (1.0 + mod_ref[:, 4 * d:5 * d]) + mod_ref[:, 3 * d:4 * d]).astype(BF16)
    acc = jnp.zeros(x.shape, F32)
    for j in range(D_FF // FF_CHUNK):
        sl = slice(j * FF_CHUNK, (j + 1) * FF_CHUNK)
        a = jnp.maximum(_dot(h, w1_ref[:, sl]), 0.0)
        acc = acc + _dot((a * a).astype(BF16), w2_ref[sl, :])
    o_ref[...] = x + mod_ref[:, 5 * d:] * acc


def _mlp(xs, modsel, g, w1, w2, lat_tiles):
    bsz, s, d = xs.shape
    tm = ROW_TILE
    row = lambda b, i: (b, i, 0)
    return pl.pallas_call(
        _mlp_kernel,
        grid=(bsz, s // tm),
        in_specs=[
            pl.BlockSpec((None, tm, d), row),
            pl.BlockSpec((None, None, 1, 6 * d), lambda b, i: (b, jnp.where(i >= lat_tiles, 1, 0), 0, 0)),
            _const_spec(g.shape), _const_spec(w1.shape), _const_spec(w2.shape),
        ],
        out_specs=pl.BlockSpec((None, tm, d), row),
        out_shape=jax.ShapeDtypeStruct(xs.shape, F32),
        compiler_params=_cparams(("parallel", "arbitrary"), 56),
        name="relu2_mlp",
    )(xs, modsel, g, w1, w2)


def _rope_tables_t(rows, dim, lc):
    n_freq = dim // 4
    inv = ROPE_BASE ** (-jnp.arange(n_freq, dtype=F32) / n_freq)
    r = jnp.repeat(jnp.arange(rows, dtype=F32), GRID_W)
    col = jnp.tile(jnp.arange(GRID_W, dtype=F32), rows)
    ang = jnp.concatenate([r[:, None] * inv, col[:, None] * inv], axis=-1)
    cos_t = jnp.concatenate([jnp.cos(ang).T, jnp.ones((dim // 2, lc), F32)], axis=1)
    sin_t = jnp.concatenate([jnp.sin(ang).T, jnp.zeros((dim // 2, lc), F32)], axis=1)
    return cos_t, sin_t


def _lane_bcast(v):
    return jnp.broadcast_to(v.astype(F32)[:, None], (v.shape[0], ROW_TILE))


def _layer_weights(l, w_in, gqa_q_norm, gqa_k_norm, mla_q_a_norm, mla_kv_a_norm, w_q_b, w_kv_b, mla_q_norm, mla_k_norm):
    w = w_in[l]
    o = np.cumsum([0, 3 * HY_WIDTH, GQA_HEADS * GQA_HEAD_DIM, GQA_KV_HEADS * GQA_HEAD_DIM,
                   GQA_KV_HEADS * GQA_HEAD_DIM, MLA_Q_RANK, MLA_KV_RANK, MLA_ROPE_DIM, 3 * D_MODEL])
    wu = w[:, o[0]:o[1]].astype(BF16)
    wt = w[:, o[1]:o[7]].T.astype(BF16)
    wg = w[:, o[7]:o[8]].astype(BF16)
    wqb = w_q_b[l].T.astype(BF16)
    wkvb = w_kv_b[l].T.astype(BF16)
    return (wu, wg, wt, wqb, wkvb,
            _lane_bcast(gqa_q_norm[l]), _lane_bcast(gqa_k_norm[l]),
            _lane_bcast(mla_q_a_norm[l]), _lane_bcast(mla_kv_a_norm[l]),
            _lane_bcast(mla_q_norm[l]), _lane_bcast(mla_k_norm[l]))


def kernel(x, c, ctx, c_ctx, w_mod, b_mod, norm_mix_g, norm_mlp_g, w_in, hy_short_w, hy_f1_w, hy_f1_b, hy_f2_w, hy_f2_b, hy_sin_freq, hy_f3_w, hy_skip, gqa_q_norm, gqa_k_norm, gqa_sink, mla_q_a_norm, mla_kv_a_norm, w_q_b, w_kv_b, mla_q_norm, mla_k_norm, w_branch, w_out, w_mlp1, w_mlp2):
    bsz, lat, d = x.shape
    lc = ctx.shape[1]
    depth = w_mod.shape[0]
    s = lat + lc
    assert d == D_MODEL and lat % GRID_W == 0 and lat % ROW_TILE == 0 and lc == ROW_TILE and s % KV_CHUNK == 0
    lat_tiles = lat // ROW_TILE

    pad = (-(bsz + 1)) % 8
    cond = jnp.concatenate([c, c_ctx[None, :], jnp.zeros((pad, d), F32)], axis=0)
    mods = _mod_all(cond, w_mod, b_mod)

    tabs = _rope_tables_t(lat // GRID_W, GQA_HEAD_DIM, lc) + _rope_tables_t(lat // GRID_W, MLA_ROPE_DIM, lc)
    xs = jnp.concatenate([x, ctx], axis=1)

    for l in range(depth):
        ml = mods[l]
        modsel = jnp.stack([ml[:bsz], jnp.broadcast_to(ml[bsz][None], (bsz, 6 * d))], axis=1)[:, :, None, :]
        wts = _layer_weights(l, w_in, gqa_q_norm, gqa_k_norm, mla_q_a_norm, mla_kv_a_norm, w_q_b, w_kv_b,
                             mla_q_norm, mla_k_norm)
        u, gates, qg, kg, vg, qm, km, vm = _proj(xs, modsel, norm_mix_g[l][None, :], wts, tabs, lat_tiles)

        f1w = jnp.zeros((V7X_LANES, HY_FILTER_WIDTH), F32).at[:HY_EMB_DIM].set(hy_f1_w[l])
        fw = (f1w, hy_f1_b[l][None, :], hy_f2_w[l], hy_f2_b[l][None, :], hy_sin_freq[l], hy_f3_w[l])
        hy_lat, hy_ctx = _hyena(u, hy_short_w[l], fw, lat, lc)

        sink_rows = jnp.broadcast_to(gqa_sink[l].astype(F32)[:, None], (GQA_HEADS, GQA_TQ))
        yg = _gqa(qg, kg, vg, sink_rows, lat, lc)
        ym_lat, ym_ctx = _mla(qm, km, vm, lat, lc)

        xs = _merge(xs, modsel, hy_lat, hy_ctx, hy_skip[l][None, :], yg, ym_lat, ym_ctx, gates,
                    w_branch[l].astype(BF16), w_out[l].astype(BF16), lat_tiles)
        xs = _mlp(xs, modsel, norm_mlp_g[l][None, :], w_mlp1[l].astype(BF16), w_mlp2[l].astype(BF16), lat_tiles)
    return xs[:, :lat]
```

```python
import functools
import math

import numpy as np
import jax
import jax.numpy as jnp
from jax import lax
from jax.experimental import pallas as pl
from jax.experimental.pallas import tpu as pltpu

D_MODEL = 1024
GRID_W = 64
HY_WIDTH = 512
HY_EMB_DIM = 33
HY_BANDS = (HY_EMB_DIM - 1) // 2
HY_FILTER_WIDTH = 64
HY_DECAY_TARGET = 1e-2
HY_FAST_DECAY = 0.3
HY_SLOW_DECAY = 1.5
GQA_HEADS = 8
GQA_KV_HEADS = 2
GQA_GROUP = GQA_HEADS // GQA_KV_HEADS
GQA_HEAD_DIM = 64
GQA_SCALE = GQA_HEAD_DIM ** -0.5
WINDOW = 128
MLA_HEADS = 8
MLA_Q_RANK = 384
MLA_KV_RANK = 256
MLA_NOPE_DIM = 64
MLA_ROPE_DIM = 32
MLA_V_DIM = 64
MLA_QK_DIM = MLA_NOPE_DIM + MLA_ROPE_DIM
MLA_SCALE = MLA_QK_DIM ** -0.5
D_FF = 4 * D_MODEL
ROPE_BASE = 10000.0
EPS = 1e-6
LOG2E = 1.4426950408889634
NEG_BIG = -1e30

V7X_LANES = 128
V7X_VMEM_BYTES = 64 * 1024 * 1024

ROW_TILE = 256
PROJ_BATCH = 2
CONV_TILE = 512
MLA_HEAD_PAD = 128
V_ROWS = 80
KV_CHUNK = 1 * ROW_TILE
KV_SUB = 256
MLA_UNROLL = 5
MLA_TQ = 1024
GQA_TQ = 256
FF_CHUNK = 1024
FFT_N2 = 128
FFT_NB = 16
FFT_KB = 8

F32 = jnp.float32
BF16 = jnp.bfloat16


def _cparams(sem, vmem_mb, flags=None):
    return pltpu.CompilerParams(dimension_semantics=sem, vmem_limit_bytes=vmem_mb * 1024 * 1024, flags=flags)


def _dot(a, b):
    return jnp.dot(a, b, preferred_element_type=F32)


def _dot_hi(a, b):
    return jnp.dot(a, b, preferred_element_type=F32, precision=lax.Precision.HIGHEST)


def _ones_row_block(width):
    r = lax.broadcasted_iota(jnp.int32, (V_ROWS - MLA_V_DIM, width), 0)
    return jnp.where(r == 0, 1.0, 0.0).astype(BF16)


def _const_spec(shape):
    nd = len(shape)
    return pl.BlockSpec(shape, lambda *_: (0,) * nd, pipeline_mode=pl.Buffered(1))


def _mod_kernel(c_ref, w_ref, b_ref, o_ref):
    c = c_ref[...]
    s = c * jax.nn.sigmoid(c)
    o_ref[...] = _dot_hi(s, w_ref[...]) + b_ref[...]


def _mod_all(cond, w_mod, b_mod):
    depth, d, n = w_mod.shape
    rows = cond.shape[0]
    tn = 1536
    return pl.pallas_call(
        _mod_kernel,
        grid=(depth, n // tn),
        in_specs=[
            pl.BlockSpec((rows, d), lambda l, j: (0, 0)),
            pl.BlockSpec((None, d, tn), lambda l, j: (l, 0, j)),
            pl.BlockSpec((None, 1, tn), lambda l, j: (l, 0, j)),
        ],
        out_specs=pl.BlockSpec((None, rows, tn), lambda l, j: (l, 0, j)),
        out_shape=jax.ShapeDtypeStruct((depth, rows, n), F32),
        compiler_params=_cparams(("arbitrary", "arbitrary"), 40),
        name="adaln_mod",
    )(cond, w_mod, b_mod.reshape(depth, 1, n))


def _rms_rows(x, n):
    return lax.rsqrt(jnp.sum(x * x, axis=0, keepdims=True) * (1.0 / n) + EPS)


def _rope_rows(x1, x2, cs, sn):
    return x1 * cs - x2 * sn, x1 * sn + x2 * cs


def _proj_kernel(x_ref, mod_ref, *refs):
    consts, outs = refs[:16], refs[16:]
    for n in range(x_ref.shape[0]):
        _proj_one(x_ref.at[n], mod_ref.at[n], *consts, *[o.at[n] for o in outs])


def _proj_one(x_ref, mod_ref, g_ref, wu_ref, wg_ref, wt_ref, wqb_ref, wkvb_ref,
              gq_ref, gk_ref, gqa_ref, gkva_ref, gmq_ref, gmk_ref,
              cg_ref, sg_ref, cm_ref, sm_ref,
              u_ref, gate_ref, qg_ref, kg_ref, vg_ref, qm_ref, km_ref, vm_ref):
    d = D_MODEL
    x = x_ref[...]
    tm = x.shape[0]
    shift = mod_ref[:, 0:d]
    scale = mod_ref[:, d:2 * d]
    xn = x * lax.rsqrt(jnp.mean(x * x, axis=-1, keepdims=True) + EPS) * g_ref[...]
    h = xn * (1.0 + scale) + shift
    hb = h.astype(BF16)

    u_ref[...] = _dot(hb, wu_ref[...]).astype(BF16)
    gate_ref[...] = jax.nn.sigmoid(_dot(hb, wg_ref[...])).astype(BF16)

    ht = h.T.astype(BF16)
    t = _dot(wt_ref[...], ht)
    o_q, o_k, o_v = 0, 512, 640
    o_cq, o_ckv, o_kr = 768, 768 + MLA_Q_RANK, 768 + MLA_Q_RANK + MLA_KV_RANK

    cg, sg = cg_ref[...], sg_ref[...]
    cm, sm = cm_ref[...], sm_ref[...]
    hd, hh = GQA_HEAD_DIM, GQA_HEAD_DIM // 2

    gq = gq_ref[...]
    for n in range(GQA_HEADS):
        xh = t[o_q + n * hd:o_q + (n + 1) * hd]
        xh = xh * _rms_rows(xh, hd) * gq
        a, b = _rope_rows(xh[:hh], xh[hh:], cg, sg)
        qg_ref[n * hd:n * hd + hh, :] = (a * (GQA_SCALE * LOG2E)).astype(BF16)
        qg_ref[n * hd + hh:(n + 1) * hd, :] = (b * (GQA_SCALE * LOG2E)).astype(BF16)

    gk = gk_ref[...]
    zpad = jnp.zeros((V7X_LANES - hd, tm), F32)
    parts = []
    for n in range(GQA_KV_HEADS):
        xh = t[o_k + n * hd:o_k + (n + 1) * hd]
        xh = xh * _rms_rows(xh, hd) * gk
        a, b = _rope_rows(xh[:hh], xh[hh:], cg, sg)
        parts += [a, b, zpad]
    kg_ref[...] = jnp.concatenate(parts, axis=0).T.astype(BF16)
    ones_rows = _ones_row_block(tm)
    for n in range(GQA_KV_HEADS):
        vg_ref[n * V_ROWS:n * V_ROWS + hd, :] = t[o_v + n * hd:o_v + (n + 1) * hd].astype(BF16)
        vg_ref[n * V_ROWS + hd:(n + 1) * V_ROWS, :] = ones_rows

    cq = t[o_cq:o_cq + MLA_Q_RANK]
    cqn = (cq * _rms_rows(cq, MLA_Q_RANK) * gqa_ref[...]).astype(BF16)
    qm = _dot(wqb_ref[...], cqn)
    gmq = gmq_ref[...]
    nd, rh = MLA_NOPE_DIM, MLA_ROPE_DIM // 2
    qpad = jnp.zeros((MLA_HEAD_PAD - MLA_QK_DIM, tm), BF16)
    for n in range(MLA_HEADS):
        xh = qm[n * MLA_QK_DIM:(n + 1) * MLA_QK_DIM]
        xh = xh * _rms_rows(xh, MLA_QK_DIM) * gmq * (MLA_SCALE * LOG2E)
        a, b = _rope_rows(xh[nd:nd + rh], xh[nd + rh:], cm, sm)
        base = n * MLA_HEAD_PAD
        qm_ref[base:base + nd, :] = xh[:nd].astype(BF16)
        qm_ref[base + nd:base + nd + rh, :] = a.astype(BF16)
        qm_ref[base + nd + rh:base + MLA_QK_DIM, :] = b.astype(BF16)
        qm_ref[base + MLA_QK_DIM:base + MLA_HEAD_PAD, :] = qpad

    ckv = t[o_ckv:o_ckv + MLA_KV_RANK]
    ckvn = (ckv * _rms_rows(ckv, MLA_KV_RANK) * gkva_ref[...]).astype(BF16)
    kv = _dot(wkvb_ref[...], ckvn)
    kr = t[o_kr:o_kr + MLA_ROPE_DIM]
    kr_ss = jnp.sum(kr * kr, axis=0, keepdims=True)
    gmk = gmk_ref[...]
    kzero = jnp.zeros((MLA_HEAD_PAD - MLA_QK_DIM, tm), F32)
    ones_rows = _ones_row_block(tm)
    parts = []
    for n in range(MLA_HEADS):
        kn = kv[n * 128:n * 128 + nd]
        vm_ref[n * V_ROWS:n * V_ROWS + MLA_V_DIM, :] = kv[n * 128 + nd:(n + 1) * 128].astype(BF16)
        vm_ref[n * V_ROWS + MLA_V_DIM:(n + 1) * V_ROWS, :] = ones_rows
        rs = lax.rsqrt((jnp.sum(kn * kn, axis=0, keepdims=True) + kr_ss) * (1.0 / MLA_QK_DIM) + EPS)
        krn = kr * rs * gmk[nd:]
        a, b = _rope_rows(krn[:rh], krn[rh:], cm, sm)
        parts += [kn * rs * gmk[:nd], a, b, kzero]
    km_ref[...] = jnp.concatenate(parts, axis=0).T.astype(BF16)


def _proj(xs, modsel, g, wts, tabs, lat_tiles):
    bsz, s, d = xs.shape
    tm = ROW_TILE
    nt = s // tm
    ncs = KV_CHUNK // tm
    nb = PROJ_BATCH if bsz % PROJ_BATCH == 0 else 1
    (wu, wg, wt, wqb, wkvb, gq, gk, gqa, gkva, gmq, gmk) = wts
    cg, sg, cm, sm = tabs
    row = lambda b, i: (b, i, 0)
    col = lambda b, i: (b, 0, i)
    tab = lambda b, i: (0, i)
    in_specs = [
        pl.BlockSpec((nb, tm, d), row),
        pl.BlockSpec((nb, None, 1, 6 * d), lambda b, i: (b, jnp.where(i >= lat_tiles, 1, 0), 0, 0)),
        _const_spec(g.shape), _const_spec(wu.shape), _const_spec(wg.shape), _const_spec(wt.shape),
        _const_spec(wqb.shape), _const_spec(wkvb.shape),
        _const_spec(gq.shape), _const_spec(gk.shape), _const_spec(gqa.shape), _const_spec(gkva.shape),
        _const_spec(gmq.shape), _const_spec(gmk.shape),
        pl.BlockSpec((cg.shape[0], tm), tab), pl.BlockSpec((sg.shape[0], tm), tab),
        pl.BlockSpec((cm.shape[0], tm), tab), pl.BlockSpec((sm.shape[0], tm), tab),
    ]
    out_shape = [
        jax.ShapeDtypeStruct((bsz, s, 3 * HY_WIDTH), BF16),
        jax.ShapeDtypeStruct((bsz, s, 3 * d), BF16),
        jax.ShapeDtypeStruct((bsz, GQA_HEADS * GQA_HEAD_DIM, s), BF16),
        jax.ShapeDtypeStruct((bsz, s, GQA_KV_HEADS * V7X_LANES), BF16),
        jax.ShapeDtypeStruct((bsz, GQA_KV_HEADS * V_ROWS, s), BF16),
        jax.ShapeDtypeStruct((bsz, MLA_HEADS * MLA_HEAD_PAD, s), BF16),
        jax.ShapeDtypeStruct((bsz, s, MLA_HEADS * MLA_HEAD_PAD), BF16),
        jax.ShapeDtypeStruct((bsz, s // KV_CHUNK, MLA_HEADS * V_ROWS, KV_CHUNK), BF16),
    ]
    out_specs = [
        pl.BlockSpec((nb, tm, 3 * HY_WIDTH), row),
        pl.BlockSpec((nb, tm, 3 * d), row),
        pl.BlockSpec((nb, GQA_HEADS * GQA_HEAD_DIM, tm), col),
        pl.BlockSpec((nb, tm, GQA_KV_HEADS * V7X_LANES), row),
        pl.BlockSpec((nb, GQA_KV_HEADS * V_ROWS, tm), col),
        pl.BlockSpec((nb, MLA_HEADS * MLA_HEAD_PAD, tm), col),
        pl.BlockSpec((nb, tm, MLA_HEADS * MLA_HEAD_PAD), row),
        pl.BlockSpec((nb, None, MLA_HEADS * V_ROWS, tm), lambda b, i: (b, i // ncs, 0, i % ncs)),
    ]
    return pl.pallas_call(
        _proj_kernel,
        grid=(bsz // nb, nt),
        in_specs=in_specs,
        out_specs=out_specs,
        out_shape=out_shape,
        compiler_params=_cparams(("parallel", "arbitrary"), 56),
        name="in_proj",
    )(xs, modsel, g, wu, wg, wt, wqb, wkvb, gq, gk, gqa, gkva, gmq, gmk, cg, sg, cm, sm)


def _short_conv_kernel(u_ref, up_ref, un_ref, w_ref, x0_ref, z_ref, *, n_tiles):
    i = pl.program_id(1)
    u = u_ref[...].astype(F32)
    tm = u.shape[0]
    prev = jnp.where(i > 0, up_ref[7:8, :].astype(F32), 0.0)
    nxt = jnp.where(i < n_tiles - 1, un_ref[0:1, :].astype(F32), 0.0)
    ridx = lax.broadcasted_iota(jnp.int32, u.shape, 0)
    up = jnp.where(ridx == 0, prev, pltpu.roll(u, 1, axis=0))
    dn = jnp.where(ridx == tm - 1, nxt, pltpu.roll(u, tm - 1, axis=0))
    uc = up * w_ref[0:1, :] + u * w_ref[1:2, :] + dn * w_ref[2:3, :]
    c = HY_WIDTH
    x0_ref[...] = uc[:, :c].astype(BF16)
    z_ref[...] = (uc[:, c:2 * c] * uc[:, 2 * c:]).astype(BF16)


def _short_conv(u, short_w, row0, rows):
    bsz, s, c3 = u.shape
    tm = math.gcd(math.gcd(rows, row0), CONV_TILE) if row0 else math.gcd(rows, CONV_TILE)
    nt = rows // tm
    t0 = row0 // tm
    r8 = tm // 8
    last8 = s // 8 - 1
    return pl.pallas_call(
        functools.partial(_short_conv_kernel, n_tiles=nt),
        grid=(bsz, nt),
        in_specs=[
            pl.BlockSpec((None, tm, c3), lambda b, i: (b, t0 + i, 0)),
            pl.BlockSpec((None, 8, c3), lambda b, i: (b, jnp.maximum((t0 + i) * r8 - 1, 0), 0)),
            pl.BlockSpec((None, 8, c3), lambda b, i: (b, jnp.minimum((t0 + i + 1) * r8, last8), 0)),
            _const_spec(short_w.shape),
        ],
        out_specs=[pl.BlockSpec((None, tm, HY_WIDTH), lambda b, i: (b, i, 0))] * 2,
        out_shape=[jax.ShapeDtypeStruct((bsz, rows, HY_WIDTH), BF16)] * 2,
        compiler_params=_cparams(("parallel", "arbitrary"), 32),
        name="hyena_short_conv",
    )(u, u, u, short_w)


def _filter_kernel(zf_ref, w1_ref, b1_ref, w2_ref, b2_ref, fr_ref, w3_ref, dl_ref, h_ref, ss_ref):
    i = pl.program_id(0)
    zf = zf_ref[...]
    tl = zf.shape[0]
    h = jnp.sin(fr_ref[0:1, :] * (_dot_hi(zf, w1_ref[...]) + b1_ref[...]))
    h = jnp.sin(fr_ref[1:2, :] * (_dot_hi(h, w2_ref[...]) + b2_ref[...]))
    h = _dot_hi(h, w3_ref[...])
    decay = jnp.exp(-zf[:, 0:1] * dl_ref[...])
    c = HY_WIDTH
    hf = h[:, :c] * decay
    ridx = lax.broadcasted_iota(jnp.int32, (tl, c), 0) + i * tl
    hb = jnp.where(ridx == 0, 0.0, h[:, c:] * decay)
    h_ref[0] = hf.astype(BF16)
    h_ref[1] = hb.astype(BF16)
    ss = jnp.sum(hf * hf + hb * hb, axis=0, keepdims=True)

    @pl.when(i == 0)
    def _():
        ss_ref[...] = ss

    @pl.when(i > 0)
    def _():
        ss_ref[...] += ss


def _filter(zfeat, fw, deltas):
    length = zfeat.shape[0]
    tl = min(length, 1024)
    w1, b1, w2, b2, fr, w3 = fw
    return pl.pallas_call(
        _filter_kernel,
        grid=(length // tl,),
        in_specs=[pl.BlockSpec((tl, zfeat.shape[1]), lambda i: (i, 0))]
        + [_const_spec(a.shape) for a in (w1, b1, w2, b2, fr, w3, deltas)],
        out_specs=[pl.BlockSpec((2, tl, HY_WIDTH), lambda i: (0, i, 0)),
                   pl.BlockSpec((1, HY_WIDTH), lambda i: (0, 0))],
        out_shape=[jax.ShapeDtypeStruct((2, length, HY_WIDTH), BF16),
                   jax.ShapeDtypeStruct((1, HY_WIDTH), F32)],
        compiler_params=_cparams(("arbitrary",), 40),
        name="hyena_filter",
    )(zfeat, w1, b1, w2, b2, fr, w3, deltas)


def _fft_a_kernel(g_ref, x_ref, o_ref, *, nb, cw):
    for j in range(nb):
        sl = slice(j * cw, (j + 1) * cw)
        o_ref[:, sl] = _dot(g_ref[j], x_ref[:, sl]).astype(o_ref.dtype)


def _fft_a(x2d, gmat):
    bx, k1, w = x2d.shape
    n2, two_n1, _ = gmat.shape
    cw = w // n2
    nb = FFT_NB
    return pl.pallas_call(
        functools.partial(_fft_a_kernel, nb=nb, cw=cw),
        grid=(bx, n2 // nb),
        in_specs=[pl.BlockSpec((nb, two_n1, k1), lambda b, j: (j, 0, 0)),
                  pl.BlockSpec((None, k1, nb * cw), lambda b, j: (b, 0, j))],
        out_specs=pl.BlockSpec((None, two_n1, nb * cw), lambda b, j: (b, 0, j)),
        out_shape=jax.ShapeDtypeStruct((bx, two_n1, w), BF16),
        compiler_params=_cparams(("parallel", "arbitrary"), 40),
        name="fft_stage_a",
    )(gmat, x2d)


def _fft_filter_b_kernel(fb_ref, a_ref, ss_ref, kf_ref, *, inv_n):
    n1 = a_ref.shape[2]
    xf = _dot(fb_ref[...], jnp.concatenate([a_ref[0, 0], a_ref[0, 1]], axis=0))
    xb = _dot(fb_ref[...], jnp.concatenate([a_ref[1, 0], a_ref[1, 1]], axis=0))
    rs = lax.rsqrt(ss_ref[...] + EPS) * inv_n
    kf_ref[0] = (xf[:n1] + xb[:n1]) * rs
    kf_ref[1] = (xf[n1:] - xb[n1:]) * rs


def _fft_filter_b(a5, fb, ssq, inv_n):
    _, _, n1, n2, c = a5.shape
    return pl.pallas_call(
        functools.partial(_fft_filter_b_kernel, inv_n=inv_n),
        grid=(n1,),
        in_specs=[_const_spec(fb.shape),
                  pl.BlockSpec((2, 2, None, n2, c), lambda k: (0, 0, k, 0, 0)),
                  _const_spec(ssq.shape)],
        out_specs=pl.BlockSpec((2, None, n2, c), lambda k: (0, k, 0, 0)),
        out_shape=jax.ShapeDtypeStruct((2, n1, n2, c), F32),
        compiler_params=_cparams(("arbitrary",), 32),
        name="fft_filter_stage_b",
    )(fb, a5, ssq)


def _fft_b_kernel(fb_ref, fbi_ref, a_ref, kf_ref, o_ref):
    kb, n, cw = a_ref.shape[1:]
    a_all = jnp.concatenate([jnp.concatenate([a_ref[0, r], a_ref[1, r]], axis=0) for r in range(kb)], axis=1)
    x = _dot(fb_ref[...], a_all)
    ys = []
    for r in range(kb):
        xr, xi = x[:n, r * cw:(r + 1) * cw], x[n:, r * cw:(r + 1) * cw]
        kr, ki = kf_ref[0, r], kf_ref[1, r]
        ys.append(jnp.concatenate([xr * kr - xi * ki, xr * ki + xi * kr], axis=0).astype(BF16))
    c = _dot(fbi_ref[...], jnp.concatenate(ys, axis=1))
    for r in range(kb):
        o_ref[0, r] = c[:n, r * cw:(r + 1) * cw].astype(o_ref.dtype)
        o_ref[1, r] = c[n:, r * cw:(r + 1) * cw].astype(o_ref.dtype)


def _fft_b(a5, kf, fb, fbi):
    bsz, _, n1, n2, c = a5.shape
    kb = min(FFT_KB, n1)
    return pl.pallas_call(
        _fft_b_kernel,
        grid=(n1 // kb, bsz),
        in_specs=[_const_spec(fb.shape), _const_spec(fbi.shape),
                  pl.BlockSpec((None, 2, kb, n2, c), lambda k, b: (b, 0, k, 0, 0)),
                  pl.BlockSpec((2, kb, n2, c), lambda k, b: (0, k, 0, 0))],
        out_specs=pl.BlockSpec((None, 2, kb, n2, c), lambda k, b: (b, 0, k, 0, 0)),
        out_shape=jax.ShapeDtypeStruct(a5.shape, BF16),
        compiler_params=_cparams(("arbitrary", "arbitrary"), 32),
        name="fft_stage_b",
    )(fb, fbi, a5, kf)


def _fft_c_kernel(h_ref, c_ref, o_ref, *, nb, cw):
    for j in range(nb):
        sl = slice(j * cw, (j + 1) * cw)
        o_ref[:, sl] = _dot(h_ref[j], c_ref[:, sl]).astype(o_ref.dtype)


def _fft_c(c2d, hmat):
    bsz, two_n1, w = c2d.shape
    n2, k1, _ = hmat.shape
    cw = w // n2
    nb = FFT_NB
    blk = lambda b, j: (b, 0, j)
    return pl.pallas_call(
        functools.partial(_fft_c_kernel, nb=nb, cw=cw),
        grid=(bsz, n2 // nb),
        in_specs=[pl.BlockSpec((nb, k1, two_n1), lambda b, j: (j, 0, 0)),
                  pl.BlockSpec((None, two_n1, nb * cw), blk)],
        out_specs=pl.BlockSpec((None, k1, nb * cw), blk),
        out_shape=jax.ShapeDtypeStruct((bsz, k1, w), BF16),
        compiler_params=_cparams(("parallel", "arbitrary"), 40),
        name="fft_stage_c",
    )(hmat, c2d)


def _ctx_conv_kernel(fc_ref, fci_ref, h_ref, ss_ref, z_ref, o_ref):
    n = fc_ref.shape[0] // 2
    fc = fc_ref[...]
    kf = _dot(fc, h_ref[0])
    kb = _dot(fc, h_ref[1])
    rs = lax.rsqrt(ss_ref[...] + EPS)
    kr = (kf[:n] + kb[:n]) * rs
    ki = (kf[n:] - kb[n:]) * rs
    x = _dot(fc, z_ref[...])
    xr, xi = x[:n], x[n:]
    y = jnp.concatenate([xr * kr - xi * ki, xr * ki + xi * kr], axis=0).astype(BF16)
    o_ref[...] = _dot(fci_ref[...], y).astype(o_ref.dtype)


def _ctx_conv(fc, fci, hfb, ssq, z):
    bsz, lc, c = z.shape
    blk = pl.BlockSpec((None, lc, c), lambda b: (b, 0, 0))
    return pl.pallas_call(
        _ctx_conv_kernel,
        grid=(bsz,),
        in_specs=[_const_spec(fc.shape), _const_spec(fci.shape), _const_spec(hfb.shape),
                  _const_spec(ssq.shape), blk],
        out_specs=blk,
        out_shape=jax.ShapeDtypeStruct((bsz, lc, c), BF16),
        compiler_params=_cparams(("arbitrary",), 32),
        name="hyena_ctx_conv",
    )(fc, fci, hfb, ssq, z)


@functools.lru_cache(maxsize=None)
def _fft_tables(length):
    n = 2 * length
    n2 = FFT_N2
    n1 = n // n2
    k1 = length // n2
    kk = np.arange(n1)[:, None]
    g = np.empty((n2, 2 * n1, k1), np.float64)
    h = np.empty((n2, k1, 2 * n1), np.float64)
    nn = np.arange(k1)[None, :]
    for j in range(n2):
        ang = 2.0 * np.pi * (((n2 * nn * kk) % n) + (j * kk) % n) / n
        g[j, :n1] = np.cos(ang)
        g[j, n1:] = -np.sin(ang)
        h[j, :, :n1] = np.cos(ang).T
        h[j, :, n1:] = -np.sin(ang).T
    a = np.arange(n2)
    ph = 2.0 * np.pi * ((a[:, None] * a[None, :]) % n2) / n2
    c, s = np.cos(ph), np.sin(ph)
    fb = np.block([[c, s], [-s, c]])
    fbi = np.block([[c, -s], [s, c]])
    return (jnp.asarray(g, BF16), jnp.asarray(h, BF16), jnp.asarray(fb, BF16), jnp.asarray(fbi, BF16), n1, k1)


@functools.lru_cache(maxsize=None)
def _dft_tables(length):
    n = 2 * length
    k = np.arange(n)[:, None]
    t = np.arange(length)[None, :]
    ang = 2.0 * np.pi * ((k * t) % n) / n
    fc = np.concatenate([np.cos(ang), -np.sin(ang)], axis=0)
    fci = np.concatenate([np.cos(ang).T, -np.sin(ang).T], axis=1) / n
    return jnp.asarray(fc, BF16), jnp.asarray(fci, BF16)


@functools.lru_cache(maxsize=None)
def _filter_features(length):
    t = np.linspace(0.0, 1.0, length, dtype=np.float32)[:, None]
    w = (2.0 * math.pi * np.arange(length, dtype=np.float32)[:, None] / length).astype(np.float32)
    f = np.linspace(1e-4, HY_BANDS - 1, HY_BANDS, dtype=np.float32)[None, :]
    z = np.concatenate([t, np.cos(f * w), -np.sin(f * w)], axis=-1).astype(np.float32)
    zp = np.zeros((length, V7X_LANES), np.float32)
    zp[:, :HY_EMB_DIM] = z
    return jnp.asarray(zp)


def _hyena_deltas():
    max_decay = math.log(HY_DECAY_TARGET) / HY_FAST_DECAY
    min_decay = math.log(HY_DECAY_TARGET) / HY_SLOW_DECAY
    return jnp.abs(jnp.linspace(min_decay, max_decay, HY_WIDTH, dtype=F32))[None, :]


def _hyena(u, short_w, fw, lat, lc):
    bsz = u.shape[0]
    c = HY_WIDTH
    deltas = _hyena_deltas()
    x0, z = _short_conv(u, short_w, 0, lat)
    gmat, hmat, fb, fbi, n1, k1 = _fft_tables(lat)
    n2 = FFT_N2
    hfb, ssq = _filter(_filter_features(lat), fw, deltas)
    fa = _fft_a(hfb.reshape(2, k1, n2 * c), gmat)
    kf = _fft_filter_b(fa.reshape(2, 2, n1, n2, c), fb, ssq, 1.0 / (2 * lat))
    za = _fft_a(z.reshape(bsz, k1, n2 * c), gmat)
    zc = _fft_b(za.reshape(bsz, 2, n1, n2, c), kf, fb, fbi)
    conv = _fft_c(zc.reshape(bsz, 2 * n1, n2 * c), hmat).reshape(bsz, lat, c)
    x0_c, z_c = _short_conv(u, short_w, lat, lc)
    hfb_c, ssq_c = _filter(_filter_features(lc), fw, deltas)
    fc, fci = _dft_tables(lc)
    conv_c = _ctx_conv(fc, fci, hfb_c, ssq_c, z_c)
    return (conv, x0, z), (conv_c, x0_c, z_c)


def _gqa_kernel(q_ref, k_ref, v_ref, sink_ref, o_ref, s_a, s_b, s_c, *, lat, lc):
    i = pl.program_id(1)
    tq = q_ref.shape[1]
    s_len = k_ref.shape[0]
    hd = GQA_HEAD_DIM
    w = WINDOW
    nwin = tq + 2 * w
    start = pl.multiple_of(jnp.clip(i * tq - w, 0, s_len - nwin), w)
    k_all = jnp.concatenate([k_ref[pl.ds(start, nwin), :], k_ref[lat:lat + lc, :]], axis=0)
    v_all = jnp.concatenate([v_ref[:, pl.ds(start, nwin)], v_ref[:, lat:lat + lc]], axis=1)
    q_pos = i * tq + lax.broadcasted_iota(jnp.int32, (1, tq), 1)
    k_pos = start + lax.broadcasted_iota(jnp.int32, (nwin, 1), 0)
    bias = (jnp.where(jnp.abs(k_pos - q_pos) <= w, 0.0, NEG_BIG)
            + jnp.where(k_pos < lat, 0.0, NEG_BIG)
            + jnp.where(q_pos < lat, 0.0, NEG_BIG))
    bias = jnp.concatenate([bias, jnp.zeros((lc, tq), F32)], axis=0)
    bias2 = jnp.concatenate([bias, bias], axis=1)
    zq = jnp.zeros((V7X_LANES - hd, 2 * tq), BF16)

    def score(pair, dst):
        g = pair // (GQA_GROUP // 2)
        r = 2 * pair * hd
        q2 = jnp.concatenate([q_ref[r:r + hd, :], q_ref[r + hd:r + 2 * hd, :]], axis=1)
        s = _dot(k_all[:, g * V7X_LANES:(g + 1) * V7X_LANES], jnp.concatenate([q2, zq], axis=0)) + bias2
        dst[...] = s
        return jnp.max(s, axis=0, keepdims=True)

    def update(pair, s_sc, s_max):
        g = pair // (GQA_GROUP // 2)
        r = 2 * pair * hd
        sink = jnp.concatenate([sink_ref[2 * pair:2 * pair + 1, :], sink_ref[2 * pair + 1:2 * pair + 2, :]],
                               axis=1) * LOG2E
        m = jnp.maximum(s_max, sink)
        p = jnp.exp2(s_sc[...] - m).astype(BF16)
        pv = _dot(v_all[g * V_ROWS:(g + 1) * V_ROWS, :], p)
        o = pv[:hd] / (pv[hd:hd + 1] + jnp.exp2(sink - m))
        o_ref[r:r + hd, :] = o[:, :tq].astype(o_ref.dtype)
        o_ref[r + hd:r + 2 * hd, :] = o[:, tq:].astype(o_ref.dtype)

    bufs = (s_a, s_b, s_c)
    npair = GQA_HEADS // 2
    mx = {0: score(0, bufs[0]), 1: score(1, bufs[1])}
    for j in range(npair):
        if j + 2 < npair:
            mx[j + 2] = score(j + 2, bufs[(j + 2) % 3])
        update(j, bufs[j % 3], mx[j])


def _gqa(qg, kg, vg, sink_rows, lat, lc):
    bsz, nq, s = qg.shape
    tq = GQA_TQ
    return pl.pallas_call(
        functools.partial(_gqa_kernel, lat=lat, lc=lc),
        grid=(bsz, s // tq),
        in_specs=[
            pl.BlockSpec((None, nq, tq), lambda b, i: (b, 0, i)),
            pl.BlockSpec((None, s, kg.shape[2]), lambda b, i: (b, 0, 0)),
            pl.BlockSpec((None, vg.shape[1], s), lambda b, i: (b, 0, 0)),
            _const_spec(sink_rows.shape),
        ],
        out_specs=pl.BlockSpec((None, nq, tq), lambda b, i: (b, 0, i)),
        out_shape=jax.ShapeDtypeStruct((bsz, nq, s), BF16),
        scratch_shapes=[pltpu.VMEM((tq + 2 * WINDOW + lc, 2 * tq), F32)] * 3,
        compiler_params=_cparams(("parallel", "arbitrary"), 40),
        name="gqa_window_attn",
    )(qg, kg, vg, sink_rows)


def _mla_update(s_ref, s_max, vt, m, acc):
    m_new = jnp.maximum(m, s_max)
    alpha = jnp.exp2(m - m_new)
    p = jnp.exp2(s_ref[...] - m_new).astype(BF16)
    acc = alpha * acc + _dot(vt, p)
    return m_new, acc


def _mla_kernel(q_ref, k_ref, v_ref, o_ref, s_a, s_b, s_c, *, nc, ctx_only):
    q = q_ref[...]
    tq = q.shape[1]
    ck = KV_CHUNK
    m = jnp.full((1, tq), NEG_BIG, F32)
    acc = jnp.zeros((V_ROWS, tq), F32)

    def score(k, dst):
        s = _dot(k, q)
        dst[...] = s
        return jnp.max(s, axis=0, keepdims=True)

    def score_chunk(j, dst):
        return score(k_ref[pl.ds(pl.multiple_of(j * ck, ck), ck), :], dst)

    if ctx_only:
        s_x = s_a.at[0:KV_SUB, :]
        mx = score(k_ref[nc * ck - KV_SUB:nc * ck, :], s_x)
        m, acc = _mla_update(s_x, mx, v_ref[nc - 1][:, ck - KV_SUB:], m, acc)
    elif nc == 1:
        m, acc = _mla_update(s_a, score_chunk(0, s_a), v_ref[0], m, acc)
    else:
        mx_a = score_chunk(0, s_a)
        mx_b = score_chunk(1, s_b)

        def triple(j, c):
            m, acc, mx_a, mx_b = c
            mx_c = score_chunk(j + 2, s_c)
            m, acc = _mla_update(s_a, mx_a, v_ref[j], m, acc)
            mx_a = score_chunk(j + 3, s_a)
            m, acc = _mla_update(s_b, mx_b, v_ref[j + 1], m, acc)
            mx_b = score_chunk(j + 4, s_b)
            m, acc = _mla_update(s_c, mx_c, v_ref[j + 2], m, acc)
            return m, acc, mx_a, mx_b

        def triples(i, c):
            for u in range(MLA_UNROLL):
                c = triple(3 * (MLA_UNROLL * i + u), c)
            return c

        nt = (nc - 2) // 3
        c = (m, acc, mx_a, mx_b)
        if nt >= MLA_UNROLL:
            c = lax.fori_loop(0, nt // MLA_UNROLL, triples, c)
        for t in range(nt - nt % MLA_UNROLL, nt):
            c = triple(3 * t, c)
        m, acc, mx_a, mx_b = c
        j = 3 * nt
        left = nc - j
        if left >= 3:
            mx_c = score_chunk(j + 2, s_c)
        m, acc = _mla_update(s_a, mx_a, v_ref[j], m, acc)
        if left == 4:
            mx_a = score_chunk(j + 3, s_a)
        m, acc = _mla_update(s_b, mx_b, v_ref[j + 1], m, acc)
        if left >= 3:
            m, acc = _mla_update(s_c, mx_c, v_ref[j + 2], m, acc)
        if left == 4:
            m, acc = _mla_update(s_a, mx_a, v_ref[j + 3], m, acc)
    o_ref[...] = (acc[:MLA_V_DIM] / acc[MLA_V_DIM:MLA_V_DIM + 1]).astype(o_ref.dtype)


def _mla_call(qm, km, vm, tq, q_tile0, n_q, ctx_only):
    bsz, _, s = qm.shape
    nc = vm.shape[1]
    return pl.pallas_call(
        functools.partial(_mla_kernel, nc=nc, ctx_only=ctx_only),
        grid=(bsz, MLA_HEADS, n_q),
        in_specs=[
            pl.BlockSpec((None, MLA_HEAD_PAD, tq), lambda b, h, i: (b, h, q_tile0 + i)),
            pl.BlockSpec((None, s, MLA_HEAD_PAD), lambda b, h, i: (b, 0, h)),
            pl.BlockSpec((None, nc, V_ROWS, KV_CHUNK), lambda b, h, i: (b, 0, h, 0)),
        ],
        out_specs=pl.BlockSpec((None, MLA_V_DIM, tq), lambda b, h, i: (b, h, i)),
        out_shape=jax.ShapeDtypeStruct((bsz, MLA_HEADS * MLA_V_DIM, n_q * tq), BF16),
        scratch_shapes=[pltpu.VMEM((KV_CHUNK, tq), F32)] * 3,
        compiler_params=_cparams(("parallel", "arbitrary", "arbitrary"), 40),
        name="mla_attn_ctx" if ctx_only else "mla_attn",
    )(qm, km, vm)


def _mla(qm, km, vm, lat, lc):
    tq = min(MLA_TQ, lat)
    assert lc == KV_SUB and lat % lc == 0
    return (_mla_call(qm, km, vm, tq, 0, lat // tq, False),
            _mla_call(qm, km, vm, lc, lat // lc, 1, True))


def _merge_kernel(x_ref, mod_ref, hl_refs, hc_refs, sk_ref, yg_ref, yml_ref, ymc_ref, gt_ref, wb_ref, wo_ref, o_ref,
                  *, lat_tiles):
    d = D_MODEL
    is_lat = pl.program_id(1) < lat_tiles
    conv, x0, z = [jnp.where(is_lat, a[...], b[...]).astype(F32) for a, b in zip(hl_refs, hc_refs)]
    yh = (x0 * (conv + z * sk_ref[...])).astype(BF16)
    yg = yg_ref[...].astype(F32).T.astype(BF16)
    ym = jnp.where(is_lat, yml_ref[...], ymc_ref[...]).astype(F32).T.astype(BF16)
    merged = (gt_ref[:, 0:d].astype(F32) * _dot(yh, wb_ref[0])
              + gt_ref[:, d:2 * d].astype(F32) * _dot(yg, wb_ref[1])
              + gt_ref[:, 2 * d:].astype(F32) * _dot(ym, wb_ref[2]))
    res = _dot(merged.astype(BF16), wo_ref[...])
    o_ref[...] = x_ref[...] + mod_ref[:, 2 * d:3 * d] * res


def _merge(xs, modsel, hy_lat, hy_ctx, skip, yg, ym_lat, ym_ctx, gates, wb, wo, lat_tiles, n_tiles):
    bsz, _, d = xs.shape
    tm = ROW_TILE
    c = HY_WIDTH
    row = lambda b, i: (b, i, 0)
    col = lambda b, i: (b, 0, i)
    lat_row = pl.BlockSpec((None, tm, c), lambda b, i: (b, jnp.minimum(i, lat_tiles - 1), 0))
    ctx_row = pl.BlockSpec((None, tm, c), lambda b, i: (b, jnp.maximum(i - lat_tiles, 0), 0))
    return pl.pallas_call(
        functools.partial(_merge_kernel, lat_tiles=lat_tiles),
        grid=(bsz, n_tiles),
        in_specs=[
            pl.BlockSpec((None, tm, d), row),
            pl.BlockSpec((None, None, 1, 6 * d), lambda b, i: (b, jnp.where(i >= lat_tiles, 1, 0), 0, 0)),
            [lat_row] * 3, [ctx_row] * 3, _const_spec(skip.shape),
            pl.BlockSpec((None, c, tm), col),
            pl.BlockSpec((None, c, tm), lambda b, i: (b, 0, jnp.minimum(i, lat_tiles - 1))),
            pl.BlockSpec((None, c, tm), lambda b, i: (b, 0, jnp.maximum(i - lat_tiles, 0))),
            pl.BlockSpec((None, tm, 3 * d), row),
            _const_spec(wb.shape), _const_spec(wo.shape),
        ],
        out_specs=pl.BlockSpec((None, tm, d), row),
        out_shape=jax.ShapeDtypeStruct((bsz, n_tiles * tm, d), F32),
        compiler_params=_cparams(("parallel", "arbitrary"), 40),
        name="branch_merge",
    )(xs, modsel, list(hy_lat), list(hy_ctx), skip, yg, ym_lat, ym_ctx, gates, wb, wo)


def _mlp_kernel(x_ref, mod_ref, g_ref, w1_ref, w2_ref, o_ref):
    d = D_MODEL
    x = x_ref[...]
    xn = x * lax.rsqrt(jnp.mean(x * x, axis=-1, keepdims=True) + EPS) * g_ref[...]
    h = (xn * (1.0 + mod_ref[:, 4 * d:5 * d]) + mod_ref[:, 3 * d:4 * d]).astype(BF16)
    acc = jnp.zeros(x.shape, F32)
    for j in range(D_FF // FF_CHUNK):
        sl = slice(j * FF_CHUNK, (j + 1) * FF_CHUNK)
        a = jnp.maximum(_dot(h, w1_ref[:, sl]), 0.0)
        acc = acc + _dot((a * a).astype(BF16), w2_ref[sl, :])
    o_ref[...] = x + mod_ref[:, 5 * d:] * acc


def _mlp(xs, modsel, g, w1, w2, lat_tiles):
    bsz, s, d = xs.shape
    tm = ROW_TILE
    row = lambda b, i: (b, i, 0)
    return pl.pallas_call(
        _mlp_kernel,
        grid=(bsz, s // tm),
        in_specs=[
            pl.BlockSpec((None, tm, d), row),
            pl.BlockSpec((None, None, 1, 6 * d), lambda b, i: (b, jnp.where(i >= lat_tiles, 1, 0), 0, 0)),
            _const_spec(g.shape), _const_spec(w1.shape), _const_spec(w2.shape),
        ],
        out_specs=pl.BlockSpec((None, tm, d), row),
        out_shape=jax.ShapeDtypeStruct(xs.shape, F32),
        compiler_params=_cparams(("parallel", "arbitrary"), 56),
        name="relu2_mlp",
    )(xs, modsel, g, w1, w2)


def _rope_tables_t(rows, dim, lc):
    n_freq = dim // 4
    inv = ROPE_BASE ** (-jnp.arange(n_freq, dtype=F32) / n_freq)
    r = jnp.repeat(jnp.arange(rows, dtype=F32), GRID_W)
    col = jnp.tile(jnp.arange(GRID_W, dtype=F32), rows)
    ang = jnp.concatenate([r[:, None] * inv, col[:, None] * inv], axis=-1)
    cos_t = jnp.concatenate([jnp.cos(ang).T, jnp.ones((dim // 2, lc), F32)], axis=1)
    sin_t = jnp.concatenate([jnp.sin(ang).T, jnp.zeros((dim // 2, lc), F32)], axis=1)
    return cos_t, sin_t


def _lane_bcast(v):
    return jnp.broadcast_to(v.astype(F32)[:, None], (v.shape[0], ROW_TILE))


def _layer_weights(l, w_in, gqa_q_norm, gqa_k_norm, mla_q_a_norm, mla_kv_a_norm, w_q_b, w_kv_b, mla_q_norm, mla_k_norm):
    w = w_in[l]
    o = np.cumsum([0, 3 * HY_WIDTH, GQA_HEADS * GQA_HEAD_DIM, GQA_KV_HEADS * GQA_HEAD_DIM,
                   GQA_KV_HEADS * GQA_HEAD_DIM, MLA_Q_RANK, MLA_KV_RANK, MLA_ROPE_DIM, 3 * D_MODEL])
    wu = w[:, o[0]:o[1]].astype(BF16)
    wt = w[:, o[1]:o[7]].T.astype(BF16)
    wg = w[:, o[7]:o[8]].astype(BF16)
    wqb = w_q_b[l].T.astype(BF16)
    wkvb = w_kv_b[l].T.astype(BF16)
    return (wu, wg, wt, wqb, wkvb,
            _lane_bcast(gqa_q_norm[l]), _lane_bcast(gqa_k_norm[l]),
            _lane_bcast(mla_q_a_norm[l]), _lane_bcast(mla_kv_a_norm[l]),
            _lane_bcast(mla_q_norm[l]), _lane_bcast(mla_k_norm[l]))


def kernel(x, c, ctx, c_ctx, w_mod, b_mod, norm_mix_g, norm_mlp_g, w_in, hy_short_w, hy_f1_w, hy_f1_b, hy_f2_w, hy_f2_b, hy_sin_freq, hy_f3_w, hy_skip, gqa_q_norm, gqa_k_norm, gqa_sink, mla_q_a_norm, mla_kv_a_norm, w_q_b, w_kv_b, mla_q_norm, mla_k_norm, w_branch, w_out, w_mlp1, w_mlp2):
    bsz, lat, d = x.shape
    lc = ctx.shape[1]
    depth = w_mod.shape[0]
    s = lat + lc
    assert d == D_MODEL and lat % GRID_W == 0 and lat % ROW_TILE == 0 and lc == ROW_TILE and s % KV_CHUNK == 0
    lat_tiles = lat // ROW_TILE

    pad = (-(bsz + 1)) % 8
    cond = jnp.concatenate([c, c_ctx[None, :], jnp.zeros((pad, d), F32)], axis=0)
    mods = _mod_all(cond, w_mod, b_mod)

    tabs = _rope_tables_t(lat // GRID_W, GQA_HEAD_DIM, lc) + _rope_tables_t(lat // GRID_W, MLA_ROPE_DIM, lc)
    xs = jnp.concatenate([x, ctx], axis=1)

    for l in range(depth):
        ml = mods[l]
        modsel = jnp.stack([ml[:bsz], jnp.broadcast_to(ml[bsz][None], (bsz, 6 * d))], axis=1)[:, :, None, :]
        wts = _layer_weights(l, w_in, gqa_q_norm, gqa_k_norm, mla_q_a_norm, mla_kv_a_norm, w_q_b, w_kv_b,
                             mla_q_norm, mla_k_norm)
        u, gates, qg, kg, vg, qm, km, vm = _proj(xs, modsel, norm_mix_g[l][None, :], wts, tabs, lat_tiles)

        f1w = jnp.zeros((V7X_LANES, HY_FILTER_WIDTH), F32).at[:HY_EMB_DIM].set(hy_f1_w[l])
        fw = (f1w, hy_f1_b[l][None, :], hy_f2_w[l], hy_f2_b[l][None, :], hy_sin_freq[l], hy_f3_w[l])
        hy_lat, hy_ctx = _hyena(u, hy_short_w[l], fw, lat, lc)

        sink_rows = jnp.broadcast_to(gqa_sink[l].astype(F32)[:, None], (GQA_HEADS, GQA_TQ))
        yg = _gqa(qg, kg, vg, sink_rows, lat, lc)
        ym_lat, ym_ctx = _mla(qm, km, vm, lat, lc)

        n_tiles = lat_tiles if l == depth - 1 else s // ROW_TILE
        xs = _merge(xs, modsel, hy_lat, hy_ctx, hy_skip[l][None, :], yg, ym_lat, ym_ctx, gates,
                    w_branch[l].astype(BF16), w_out[l].astype(BF16), lat_tiles, n_tiles)
        xs = _mlp(xs, modsel, norm_mlp_g[l][None, :], w_mlp1[l].astype(BF16), w_mlp2[l].astype(BF16), lat_tiles)
    return xs
```

```python
import functools
import math

import numpy as np
import jax
import jax.numpy as jnp
from jax import lax
from jax.experimental import pallas as pl
from jax.experimental.pallas import tpu as pltpu

D_MODEL = 1024
GRID_W = 64
HY_WIDTH = 512
HY_EMB_DIM = 33
HY_BANDS = (HY_EMB_DIM - 1) // 2
HY_FILTER_WIDTH = 64
HY_DECAY_TARGET = 1e-2
HY_FAST_DECAY = 0.3
HY_SLOW_DECAY = 1.5
GQA_HEADS = 8
GQA_KV_HEADS = 2
GQA_GROUP = GQA_HEADS // GQA_KV_HEADS
GQA_HEAD_DIM = 64
GQA_SCALE = GQA_HEAD_DIM ** -0.5
WINDOW = 128
MLA_HEADS = 8
MLA_Q_RANK = 384
MLA_KV_RANK = 256
MLA_NOPE_DIM = 64
MLA_ROPE_DIM = 32
MLA_V_DIM = 64
MLA_QK_DIM = MLA_NOPE_DIM + MLA_ROPE_DIM
MLA_SCALE = MLA_QK_DIM ** -0.5
D_FF = 4 * D_MODEL
ROPE_BASE = 10000.0
EPS = 1e-6
LOG2E = 1.4426950408889634
NEG_BIG = -1e30

V7X_LANES = 128
V7X_VMEM_BYTES = 64 * 1024 * 1024

ROW_TILE = 256
PROJ_BATCH = 2
CONV_TILE = 512
MLA_HEAD_PAD = 128
V_ROWS = 80
KV_CHUNK = 1 * ROW_TILE
KV_SUB = 256
MLA_UNROLL = 5
MLA_TQ = 1024
MLA_SUBTILES = 2
GQA_TQ = 256
FF_CHUNK = 1024
FFT_N2 = 128
FFT_NB = 16
FFT_KB = 8

F32 = jnp.float32
BF16 = jnp.bfloat16


def _cparams(sem, vmem_mb, flags=None):
    return pltpu.CompilerParams(dimension_semantics=sem, vmem_limit_bytes=vmem_mb * 1024 * 1024, flags=flags)


def _dot(a, b):
    return jnp.dot(a, b, preferred_element_type=F32)


def _dot_hi(a, b):
    return jnp.dot(a, b, preferred_element_type=F32, precision=lax.Precision.HIGHEST)


def _ones_row_block(width):
    r = lax.broadcasted_iota(jnp.int32, (V_ROWS - MLA_V_DIM, width), 0)
    return jnp.where(r == 0, 1.0, 0.0).astype(BF16)


def _const_spec(shape):
    nd = len(shape)
    return pl.BlockSpec(shape, lambda *_: (0,) * nd, pipeline_mode=pl.Buffered(1))


def _mod_kernel(c_ref, w_ref, b_ref, o_ref):
    c = c_ref[...]
    s = c * jax.nn.sigmoid(c)
    o_ref[...] = _dot_hi(s, w_ref[...]) + b_ref[...]


def _mod_all(cond, w_mod, b_mod):
    depth, d, n = w_mod.shape
    rows = cond.shape[0]
    tn = 1536
    return pl.pallas_call(
        _mod_kernel,
        grid=(depth, n // tn),
        in_specs=[
            pl.BlockSpec((rows, d), lambda l, j: (0, 0)),
            pl.BlockSpec((None, d, tn), lambda l, j: (l, 0, j)),
            pl.BlockSpec((None, 1, tn), lambda l, j: (l, 0, j)),
        ],
        out_specs=pl.BlockSpec((None, rows, tn), lambda l, j: (l, 0, j)),
        out_shape=jax.ShapeDtypeStruct((depth, rows, n), F32),
        compiler_params=_cparams(("arbitrary", "arbitrary"), 40),
        name="adaln_mod",
    )(cond, w_mod, b_mod.reshape(depth, 1, n))


def _rms_rows(x, n):
    return lax.rsqrt(jnp.sum(x * x, axis=0, keepdims=True) * (1.0 / n) + EPS)


def _rope_rows(x1, x2, cs, sn):
    return x1 * cs - x2 * sn, x1 * sn + x2 * cs


def _proj_kernel(x_ref, mod_ref, *refs):
    consts, outs = refs[:16], refs[16:]
    for n in range(x_ref.shape[0]):
        _proj_one(x_ref.at[n], mod_ref.at[n], *consts, *[o.at[n] for o in outs])


def _proj_one(x_ref, mod_ref, g_ref, wu_ref, wg_ref, wt_ref, wqb_ref, wkvb_ref,
              gq_ref, gk_ref, gqa_ref, gkva_ref, gmq_ref, gmk_ref,
              cg_ref, sg_ref, cm_ref, sm_ref,
              u_ref, gate_ref, qg_ref, kg_ref, vg_ref, qm_ref, km_ref, vm_ref):
    d = D_MODEL
    x = x_ref[...]
    tm = x.shape[0]
    shift = mod_ref[:, 0:d]
    scale = mod_ref[:, d:2 * d]
    xn = x * lax.rsqrt(jnp.mean(x * x, axis=-1, keepdims=True) + EPS) * g_ref[...]
    h = xn * (1.0 + scale) + shift
    hb = h.astype(BF16)

    u_ref[...] = _dot(hb, wu_ref[...]).astype(BF16)
    gate_ref[...] = jax.nn.sigmoid(_dot(hb, wg_ref[...])).astype(BF16)

    ht = h.T.astype(BF16)
    t = _dot(wt_ref[...], ht)
    o_q, o_k, o_v = 0, 512, 640
    o_cq, o_ckv, o_kr = 768, 768 + MLA_Q_RANK, 768 + MLA_Q_RANK + MLA_KV_RANK

    cg, sg = cg_ref[...], sg_ref[...]
    cm, sm = cm_ref[...], sm_ref[...]
    hd, hh = GQA_HEAD_DIM, GQA_HEAD_DIM // 2

    gq = gq_ref[...]
    for n in range(GQA_HEADS):
        xh = t[o_q + n * hd:o_q + (n + 1) * hd]
        xh = xh * _rms_rows(xh, hd) * gq
        a, b = _rope_rows(xh[:hh], xh[hh:], cg, sg)
        qg_ref[n * hd:n * hd + hh, :] = (a * (GQA_SCALE * LOG2E)).astype(BF16)
        qg_ref[n * hd + hh:(n + 1) * hd, :] = (b * (GQA_SCALE * LOG2E)).astype(BF16)

    gk = gk_ref[...]
    zpad = jnp.zeros((V7X_LANES - hd, tm), F32)
    parts = []
    for n in range(GQA_KV_HEADS):
        xh = t[o_k + n * hd:o_k + (n + 1) * hd]
        xh = xh * _rms_rows(xh, hd) * gk
        a, b = _rope_rows(xh[:hh], xh[hh:], cg, sg)
        parts += [a, b, zpad]
    kg_ref[...] = jnp.concatenate(parts, axis=0).T.astype(BF16)
    ones_rows = _ones_row_block(tm)
    for n in range(GQA_KV_HEADS):
        vg_ref[n * V_ROWS:n * V_ROWS + hd, :] = t[o_v + n * hd:o_v + (n + 1) * hd].astype(BF16)
        vg_ref[n * V_ROWS + hd:(n + 1) * V_ROWS, :] = ones_rows

    cq = t[o_cq:o_cq + MLA_Q_RANK]
    cqn = (cq * _rms_rows(cq, MLA_Q_RANK) * gqa_ref[...]).astype(BF16)
    qm = _dot(wqb_ref[...], cqn)
    gmq = gmq_ref[...]
    nd, rh = MLA_NOPE_DIM, MLA_ROPE_DIM // 2
    qpad = jnp.zeros((MLA_HEAD_PAD - MLA_QK_DIM, tm), BF16)
    for n in range(MLA_HEADS):
        xh = qm[n * MLA_QK_DIM:(n + 1) * MLA_QK_DIM]
        xh = xh * _rms_rows(xh, MLA_QK_DIM) * gmq * (MLA_SCALE * LOG2E)
        a, b = _rope_rows(xh[nd:nd + rh], xh[nd + rh:], cm, sm)
        base = n * MLA_HEAD_PAD
        qm_ref[base:base + nd, :] = xh[:nd].astype(BF16)
        qm_ref[base + nd:base + nd + rh, :] = a.astype(BF16)
        qm_ref[base + nd + rh:base + MLA_QK_DIM, :] = b.astype(BF16)
        qm_ref[base + MLA_QK_DIM:base + MLA_HEAD_PAD, :] = qpad

    ckv = t[o_ckv:o_ckv + MLA_KV_RANK]
    ckvn = (ckv * _rms_rows(ckv, MLA_KV_RANK) * gkva_ref[...]).astype(BF16)
    kv = _dot(wkvb_ref[...], ckvn)
    kr = t[o_kr:o_kr + MLA_ROPE_DIM]
    kr_ss = jnp.sum(kr * kr, axis=0, keepdims=True)
    gmk = gmk_ref[...]
    kzero = jnp.zeros((MLA_HEAD_PAD - MLA_QK_DIM, tm), F32)
    ones_rows = _ones_row_block(tm)
    parts = []
    for n in range(MLA_HEADS):
        kn = kv[n * 128:n * 128 + nd]
        vm_ref[n * V_ROWS:n * V_ROWS + MLA_V_DIM, :] = kv[n * 128 + nd:(n + 1) * 128].astype(BF16)
        vm_ref[n * V_ROWS + MLA_V_DIM:(n + 1) * V_ROWS, :] = ones_rows
        rs = lax.rsqrt((jnp.sum(kn * kn, axis=0, keepdims=True) + kr_ss) * (1.0 / MLA_QK_DIM) + EPS)
        krn = kr * rs * gmk[nd:]
        a, b = _rope_rows(krn[:rh], krn[rh:], cm, sm)
        parts += [kn * rs * gmk[:nd], a, b, kzero]
    km_ref[...] = jnp.concatenate(parts, axis=0).T.astype(BF16)


def _proj(xs, modsel, g, wts, tabs, lat_tiles):
    bsz, s, d = xs.shape
    tm = ROW_TILE
    nt = s // tm
    ncs = KV_CHUNK // tm
    nb = PROJ_BATCH if bsz % PROJ_BATCH == 0 else 1
    (wu, wg, wt, wqb, wkvb, gq, gk, gqa, gkva, gmq, gmk) = wts
    cg, sg, cm, sm = tabs
    row = lambda b, i: (b, i, 0)
    col = lambda b, i: (b, 0, i)
    tab = lambda b, i: (0, i)
    in_specs = [
        pl.BlockSpec((nb, tm, d), row),
        pl.BlockSpec((nb, None, 1, 6 * d), lambda b, i: (b, jnp.where(i >= lat_tiles, 1, 0), 0, 0)),
        _const_spec(g.shape), _const_spec(wu.shape), _const_spec(wg.shape), _const_spec(wt.shape),
        _const_spec(wqb.shape), _const_spec(wkvb.shape),
        _const_spec(gq.shape), _const_spec(gk.shape), _const_spec(gqa.shape), _const_spec(gkva.shape),
        _const_spec(gmq.shape), _const_spec(gmk.shape),
        pl.BlockSpec((cg.shape[0], tm), tab), pl.BlockSpec((sg.shape[0], tm), tab),
        pl.BlockSpec((cm.shape[0], tm), tab), pl.BlockSpec((sm.shape[0], tm), tab),
    ]
    out_shape = [
        jax.ShapeDtypeStruct((bsz, s, 3 * HY_WIDTH), BF16),
        jax.ShapeDtypeStruct((bsz, s, 3 * d), BF16),
        jax.ShapeDtypeStruct((bsz, GQA_HEADS * GQA_HEAD_DIM, s), BF16),
        jax.ShapeDtypeStruct((bsz, s, GQA_KV_HEADS * V7X_LANES), BF16),
        jax.ShapeDtypeStruct((bsz, GQA_KV_HEADS * V_ROWS, s), BF16),
        jax.ShapeDtypeStruct((bsz, MLA_HEADS * MLA_HEAD_PAD, s), BF16),
        jax.ShapeDtypeStruct((bsz, s, MLA_HEADS * MLA_HEAD_PAD), BF16),
        jax.ShapeDtypeStruct((bsz, s // KV_CHUNK, MLA_HEADS * V_ROWS, KV_CHUNK), BF16),
    ]
    out_specs = [
        pl.BlockSpec((nb, tm, 3 * HY_WIDTH), row),
        pl.BlockSpec((nb, tm, 3 * d), row),
        pl.BlockSpec((nb, GQA_HEADS * GQA_HEAD_DIM, tm), col),
        pl.BlockSpec((nb, tm, GQA_KV_HEADS * V7X_LANES), row),
        pl.BlockSpec((nb, GQA_KV_HEADS * V_ROWS, tm), col),
        pl.BlockSpec((nb, MLA_HEADS * MLA_HEAD_PAD, tm), col),
        pl.BlockSpec((nb, tm, MLA_HEADS * MLA_HEAD_PAD), row),
        pl.BlockSpec((nb, None, MLA_HEADS * V_ROWS, tm), lambda b, i: (b, i // ncs, 0, i % ncs)),
    ]
    return pl.pallas_call(
        _proj_kernel,
        grid=(bsz // nb, nt),
        in_specs=in_specs,
        out_specs=out_specs,
        out_shape=out_shape,
        compiler_params=_cparams(("parallel", "arbitrary"), 56),
        name="in_proj",
    )(xs, modsel, g, wu, wg, wt, wqb, wkvb, gq, gk, gqa, gkva, gmq, gmk, cg, sg, cm, sm)


def _short_conv_kernel(u_ref, up_ref, un_ref, w_ref, x0_ref, z_ref, *, n_tiles):
    i = pl.program_id(1)
    u = u_ref[...].astype(F32)
    tm = u.shape[0]
    prev = jnp.where(i > 0, up_ref[7:8, :].astype(F32), 0.0)
    nxt = jnp.where(i < n_tiles - 1, un_ref[0:1, :].astype(F32), 0.0)
    ridx = lax.broadcasted_iota(jnp.int32, u.shape, 0)
    up = jnp.where(ridx == 0, prev, pltpu.roll(u, 1, axis=0))
    dn = jnp.where(ridx == tm - 1, nxt, pltpu.roll(u, tm - 1, axis=0))
    uc = up * w_ref[0:1, :] + u * w_ref[1:2, :] + dn * w_ref[2:3, :]
    c = HY_WIDTH
    x0_ref[...] = uc[:, :c].astype(BF16)
    z_ref[...] = (uc[:, c:2 * c] * uc[:, 2 * c:]).astype(BF16)


def _short_conv(u, short_w, row0, rows):
    bsz, s, c3 = u.shape
    tm = math.gcd(math.gcd(rows, row0), CONV_TILE) if row0 else math.gcd(rows, CONV_TILE)
    nt = rows // tm
    t0 = row0 // tm
    r8 = tm // 8
    last8 = s // 8 - 1
    return pl.pallas_call(
        functools.partial(_short_conv_kernel, n_tiles=nt),
        grid=(bsz, nt),
        in_specs=[
            pl.BlockSpec((None, tm, c3), lambda b, i: (b, t0 + i, 0)),
            pl.BlockSpec((None, 8, c3), lambda b, i: (b, jnp.maximum((t0 + i) * r8 - 1, 0), 0)),
            pl.BlockSpec((None, 8, c3), lambda b, i: (b, jnp.minimum((t0 + i + 1) * r8, last8), 0)),
            _const_spec(short_w.shape),
        ],
        out_specs=[pl.BlockSpec((None, tm, HY_WIDTH), lambda b, i: (b, i, 0))] * 2,
        out_shape=[jax.ShapeDtypeStruct((bsz, rows, HY_WIDTH), BF16)] * 2,
        compiler_params=_cparams(("parallel", "arbitrary"), 32),
        name="hyena_short_conv",
    )(u, u, u, short_w)


def _filter_kernel(zf_ref, w1_ref, b1_ref, w2_ref, b2_ref, fr_ref, w3_ref, dl_ref, h_ref, ss_ref):
    i = pl.program_id(0)
    zf = zf_ref[...]
    tl = zf.shape[0]
    h = jnp.sin(fr_ref[0:1, :] * (_dot_hi(zf, w1_ref[...]) + b1_ref[...]))
    h = jnp.sin(fr_ref[1:2, :] * (_dot_hi(h, w2_ref[...]) + b2_ref[...]))
    h = _dot_hi(h, w3_ref[...])
    decay = jnp.exp(-zf[:, 0:1] * dl_ref[...])
    c = HY_WIDTH
    hf = h[:, :c] * decay
    ridx = lax.broadcasted_iota(jnp.int32, (tl, c), 0) + i * tl
    hb = jnp.where(ridx == 0, 0.0, h[:, c:] * decay)
    h_ref[0] = hf.astype(BF16)
    h_ref[1] = hb.astype(BF16)
    ss = jnp.sum(hf * hf + hb * hb, axis=0, keepdims=True)

    @pl.when(i == 0)
    def _():
        ss_ref[...] = ss

    @pl.when(i > 0)
    def _():
        ss_ref[...] += ss


def _filter(zfeat, fw, deltas):
    length = zfeat.shape[0]
    tl = min(length, 1024)
    w1, b1, w2, b2, fr, w3 = fw
    return pl.pallas_call(
        _filter_kernel,
        grid=(length // tl,),
        in_specs=[pl.BlockSpec((tl, zfeat.shape[1]), lambda i: (i, 0))]
        + [_const_spec(a.shape) for a in (w1, b1, w2, b2, fr, w3, deltas)],
        out_specs=[pl.BlockSpec((2, tl, HY_WIDTH), lambda i: (0, i, 0)),
                   pl.BlockSpec((1, HY_WIDTH), lambda i: (0, 0))],
        out_shape=[jax.ShapeDtypeStruct((2, length, HY_WIDTH), BF16),
                   jax.ShapeDtypeStruct((1, HY_WIDTH), F32)],
        compiler_params=_cparams(("arbitrary",), 40),
        name="hyena_filter",
    )(zfeat, w1, b1, w2, b2, fr, w3, deltas)


def _fft_a_kernel(g_ref, x_ref, o_ref, *, nb, cw):
    for j in range(nb):
        sl = slice(j * cw, (j + 1) * cw)
        o_ref[:, sl] = _dot(g_ref[j], x_ref[:, sl]).astype(o_ref.dtype)


def _fft_a(x2d, gmat):
    bx, k1, w = x2d.shape
    n2, two_n1, _ = gmat.shape
    cw = w // n2
    nb = FFT_NB
    return pl.pallas_call(
        functools.partial(_fft_a_kernel, nb=nb, cw=cw),
        grid=(bx, n2 // nb),
        in_specs=[pl.BlockSpec((nb, two_n1, k1), lambda b, j: (j, 0, 0)),
                  pl.BlockSpec((None, k1, nb * cw), lambda b, j: (b, 0, j))],
        out_specs=pl.BlockSpec((None, two_n1, nb * cw), lambda b, j: (b, 0, j)),
        out_shape=jax.ShapeDtypeStruct((bx, two_n1, w), BF16),
        compiler_params=_cparams(("parallel", "arbitrary"), 40),
        name="fft_stage_a",
    )(gmat, x2d)


def _fft_filter_b_kernel(fb_ref, a_ref, ss_ref, kf_ref, *, inv_n):
    n = a_ref.shape[3]
    rs = lax.rsqrt(ss_ref[...] + EPS) * inv_n
    for r in range(a_ref.shape[2]):
        xf = _dot(fb_ref[...], jnp.concatenate([a_ref[0, 0, r], a_ref[0, 1, r]], axis=0))
        xb = _dot(fb_ref[...], jnp.concatenate([a_ref[1, 0, r], a_ref[1, 1, r]], axis=0))
        kf_ref[0, r] = (xf[:n] + xb[:n]) * rs
        kf_ref[1, r] = (xf[n:] - xb[n:]) * rs


def _fft_filter_b(a5, fb, ssq, inv_n):
    _, _, n1, n2, c = a5.shape
    kb = min(FFT_KB, n1)
    return pl.pallas_call(
        functools.partial(_fft_filter_b_kernel, inv_n=inv_n),
        grid=(n1 // kb,),
        in_specs=[_const_spec(fb.shape),
                  pl.BlockSpec((2, 2, kb, n2, c), lambda k: (0, 0, k, 0, 0)),
                  _const_spec(ssq.shape)],
        out_specs=pl.BlockSpec((2, kb, n2, c), lambda k: (0, k, 0, 0)),
        out_shape=jax.ShapeDtypeStruct((2, n1, n2, c), F32),
        compiler_params=_cparams(("arbitrary",), 32),
        name="fft_filter_stage_b",
    )(fb, a5, ssq)


def _fft_b_kernel(fb_ref, fbi_ref, a_ref, kf_ref, o_ref):
    kb, n, cw = a_ref.shape[1:]
    a_all = jnp.concatenate([jnp.concatenate([a_ref[0, r], a_ref[1, r]], axis=0) for r in range(kb)], axis=1)
    x = _dot(fb_ref[...], a_all)
    ys = []
    for r in range(kb):
        xr, xi = x[:n, r * cw:(r + 1) * cw], x[n:, r * cw:(r + 1) * cw]
        kr, ki = kf_ref[0, r], kf_ref[1, r]
        ys.append(jnp.concatenate([xr * kr - xi * ki, xr * ki + xi * kr], axis=0).astype(BF16))
    c = _dot(fbi_ref[...], jnp.concatenate(ys, axis=1))
    for r in range(kb):
        o_ref[0, r] = c[:n, r * cw:(r + 1) * cw].astype(o_ref.dtype)
        o_ref[1, r] = c[n:, r * cw:(r + 1) * cw].astype(o_ref.dtype)


def _fft_b(a5, kf, fb, fbi):
    bsz, _, n1, n2, c = a5.shape
    kb = min(FFT_KB, n1)
    return pl.pallas_call(
        _fft_b_kernel,
        grid=(n1 // kb, bsz),
        in_specs=[_const_spec(fb.shape), _const_spec(fbi.shape),
                  pl.BlockSpec((None, 2, kb, n2, c), lambda k, b: (b, 0, k, 0, 0)),
                  pl.BlockSpec((2, kb, n2, c), lambda k, b: (0, k, 0, 0))],
        out_specs=pl.BlockSpec((None, 2, kb, n2, c), lambda k, b: (b, 0, k, 0, 0)),
        out_shape=jax.ShapeDtypeStruct(a5.shape, BF16),
        compiler_params=_cparams(("arbitrary", "arbitrary"), 32),
        name="fft_stage_b",
    )(fb, fbi, a5, kf)


def _fft_c_kernel(h_ref, c_ref, o_ref, *, nb, cw):
    for j in range(nb):
        sl = slice(j * cw, (j + 1) * cw)
        o_ref[:, sl] = _dot(h_ref[j], c_ref[:, sl]).astype(o_ref.dtype)


def _fft_c(c2d, hmat):
    bsz, two_n1, w = c2d.shape
    n2, k1, _ = hmat.shape
    cw = w // n2
    nb = FFT_NB
    blk = lambda b, j: (b, 0, j)
    return pl.pallas_call(
        functools.partial(_fft_c_kernel, nb=nb, cw=cw),
        grid=(bsz, n2 // nb),
        in_specs=[pl.BlockSpec((nb, k1, two_n1), lambda b, j: (j, 0, 0)),
                  pl.BlockSpec((None, two_n1, nb * cw), blk)],
        out_specs=pl.BlockSpec((None, k1, nb * cw), blk),
        out_shape=jax.ShapeDtypeStruct((bsz, k1, w), BF16),
        compiler_params=_cparams(("parallel", "arbitrary"), 40),
        name="fft_stage_c",
    )(hmat, c2d)


def _ctx_conv_kernel(fc_ref, fci_ref, h_ref, ss_ref, z_ref, o_ref):
    n = fc_ref.shape[0] // 2
    fc = fc_ref[...]
    kf = _dot(fc, h_ref[0])
    kb = _dot(fc, h_ref[1])
    rs = lax.rsqrt(ss_ref[...] + EPS)
    kr = (kf[:n] + kb[:n]) * rs
    ki = (kf[n:] - kb[n:]) * rs
    x = _dot(fc, z_ref[...])
    xr, xi = x[:n], x[n:]
    y = jnp.concatenate([xr * kr - xi * ki, xr * ki + xi * kr], axis=0).astype(BF16)
    o_ref[...] = _dot(fci_ref[...], y).astype(o_ref.dtype)


def _ctx_conv(fc, fci, hfb, ssq, z):
    bsz, lc, c = z.shape
    blk = pl.BlockSpec((None, lc, c), lambda b: (b, 0, 0))
    return pl.pallas_call(
        _ctx_conv_kernel,
        grid=(bsz,),
        in_specs=[_const_spec(fc.shape), _const_spec(fci.shape), _const_spec(hfb.shape),
                  _const_spec(ssq.shape), blk],
        out_specs=blk,
        out_shape=jax.ShapeDtypeStruct((bsz, lc, c), BF16),
        compiler_params=_cparams(("arbitrary",), 32),
        name="hyena_ctx_conv",
    )(fc, fci, hfb, ssq, z)


@functools.lru_cache(maxsize=None)
def _fft_tables(length):
    n = 2 * length
    n2 = FFT_N2
    n1 = n // n2
    k1 = length // n2
    kk = np.arange(n1)[:, None]
    g = np.empty((n2, 2 * n1, k1), np.float64)
    h = np.empty((n2, k1, 2 * n1), np.float64)
    nn = np.arange(k1)[None, :]
    for j in range(n2):
        ang = 2.0 * np.pi * (((n2 * nn * kk) % n) + (j * kk) % n) / n
        g[j, :n1] = np.cos(ang)
        g[j, n1:] = -np.sin(ang)
        h[j, :, :n1] = np.cos(ang).T
        h[j, :, n1:] = -np.sin(ang).T
    a = np.arange(n2)
    ph = 2.0 * np.pi * ((a[:, None] * a[None, :]) % n2) / n2
    c, s = np.cos(ph), np.sin(ph)
    fb = np.block([[c, s], [-s, c]])
    fbi = np.block([[c, -s], [s, c]])
    return (jnp.asarray(g, BF16), jnp.asarray(h, BF16), jnp.asarray(fb, BF16), jnp.asarray(fbi, BF16), n1, k1)


@functools.lru_cache(maxsize=None)
def _dft_tables(length):
    n = 2 * length
    k = np.arange(n)[:, None]
    t = np.arange(length)[None, :]
    ang = 2.0 * np.pi * ((k * t) % n) / n
    fc = np.concatenate([np.cos(ang), -np.sin(ang)], axis=0)
    fci = np.concatenate([np.cos(ang).T, -np.sin(ang).T], axis=1) / n
    return jnp.asarray(fc, BF16), jnp.asarray(fci, BF16)


@functools.lru_cache(maxsize=None)
def _filter_features(length):
    t = np.linspace(0.0, 1.0, length, dtype=np.float32)[:, None]
    w = (2.0 * math.pi * np.arange(length, dtype=np.float32)[:, None] / length).astype(np.float32)
    f = np.linspace(1e-4, HY_BANDS - 1, HY_BANDS, dtype=np.float32)[None, :]
    z = np.concatenate([t, np.cos(f * w), -np.sin(f * w)], axis=-1).astype(np.float32)
    zp = np.zeros((length, V7X_LANES), np.float32)
    zp[:, :HY_EMB_DIM] = z
    return jnp.asarray(zp)


def _hyena_deltas():
    max_decay = math.log(HY_DECAY_TARGET) / HY_FAST_DECAY
    min_decay = math.log(HY_DECAY_TARGET) / HY_SLOW_DECAY
    return jnp.abs(jnp.linspace(min_decay, max_decay, HY_WIDTH, dtype=F32))[None, :]


def _hyena(u, short_w, fw, lat, lc):
    bsz = u.shape[0]
    c = HY_WIDTH
    deltas = _hyena_deltas()
    x0, z = _short_conv(u, short_w, 0, lat)
    gmat, hmat, fb, fbi, n1, k1 = _fft_tables(lat)
    n2 = FFT_N2
    hfb, ssq = _filter(_filter_features(lat), fw, deltas)
    fa = _fft_a(hfb.reshape(2, k1, n2 * c), gmat)
    kf = _fft_filter_b(fa.reshape(2, 2, n1, n2, c), fb, ssq, 1.0 / (2 * lat))
    za = _fft_a(z.reshape(bsz, k1, n2 * c), gmat)
    zc = _fft_b(za.reshape(bsz, 2, n1, n2, c), kf, fb, fbi)
    conv = _fft_c(zc.reshape(bsz, 2 * n1, n2 * c), hmat).reshape(bsz, lat, c)
    x0_c, z_c = _short_conv(u, short_w, lat, lc)
    hfb_c, ssq_c = _filter(_filter_features(lc), fw, deltas)
    fc, fci = _dft_tables(lc)
    conv_c = _ctx_conv(fc, fci, hfb_c, ssq_c, z_c)
    return (conv, x0, z), (conv_c, x0_c, z_c)


def _gqa_kernel(q_ref, k_ref, v_ref, sink_ref, o_ref, s_a, s_b, s_c, *, lat, lc):
    i = pl.program_id(1)
    tq = q_ref.shape[1]
    s_len = k_ref.shape[0]
    hd = GQA_HEAD_DIM
    w = WINDOW
    nwin = tq + 2 * w
    start = pl.multiple_of(jnp.clip(i * tq - w, 0, s_len - nwin), w)
    k_all = jnp.concatenate([k_ref[pl.ds(start, nwin), :], k_ref[lat:lat + lc, :]], axis=0)
    v_all = jnp.concatenate([v_ref[:, pl.ds(start, nwin)], v_ref[:, lat:lat + lc]], axis=1)
    q_pos = i * tq + lax.broadcasted_iota(jnp.int32, (1, tq), 1)
    k_pos = start + lax.broadcasted_iota(jnp.int32, (nwin, 1), 0)
    bias = (jnp.where(jnp.abs(k_pos - q_pos) <= w, 0.0, NEG_BIG)
            + jnp.where(k_pos < lat, 0.0, NEG_BIG)
            + jnp.where(q_pos < lat, 0.0, NEG_BIG))
    bias = jnp.concatenate([bias, jnp.zeros((lc, tq), F32)], axis=0)
    bias2 = jnp.concatenate([bias, bias], axis=1)
    zq = jnp.zeros((V7X_LANES - hd, 2 * tq), BF16)

    def score(pair, dst):
        g = pair // (GQA_GROUP // 2)
        r = 2 * pair * hd
        q2 = jnp.concatenate([q_ref[r:r + hd, :], q_ref[r + hd:r + 2 * hd, :]], axis=1)
        s = _dot(k_all[:, g * V7X_LANES:(g + 1) * V7X_LANES], jnp.concatenate([q2, zq], axis=0)) + bias2
        dst[...] = s
        return jnp.max(s, axis=0, keepdims=True)

    def update(pair, s_sc, s_max):
        g = pair // (GQA_GROUP // 2)
        r = 2 * pair * hd
        sink = jnp.concatenate([sink_ref[2 * pair:2 * pair + 1, :], sink_ref[2 * pair + 1:2 * pair + 2, :]],
                               axis=1) * LOG2E
        m = jnp.maximum(s_max, sink)
        p = jnp.exp2(s_sc[...] - m).astype(BF16)
        pv = _dot(v_all[g * V_ROWS:(g + 1) * V_ROWS, :], p)
        o = pv[:hd] / (pv[hd:hd + 1] + jnp.exp2(sink - m))
        o_ref[r:r + hd, :] = o[:, :tq].astype(o_ref.dtype)
        o_ref[r + hd:r + 2 * hd, :] = o[:, tq:].astype(o_ref.dtype)

    bufs = (s_a, s_b, s_c)
    npair = GQA_HEADS // 2
    mx = {0: score(0, bufs[0]), 1: score(1, bufs[1])}
    for j in range(npair):
        if j + 2 < npair:
            mx[j + 2] = score(j + 2, bufs[(j + 2) % 3])
        update(j, bufs[j % 3], mx[j])


def _gqa(qg, kg, vg, sink_rows, lat, lc):
    bsz, nq, s = qg.shape
    tq = GQA_TQ
    return pl.pallas_call(
        functools.partial(_gqa_kernel, lat=lat, lc=lc),
        grid=(bsz, s // tq),
        in_specs=[
            pl.BlockSpec((None, nq, tq), lambda b, i: (b, 0, i)),
            pl.BlockSpec((None, s, kg.shape[2]), lambda b, i: (b, 0, 0)),
            pl.BlockSpec((None, vg.shape[1], s), lambda b, i: (b, 0, 0)),
            _const_spec(sink_rows.shape),
        ],
        out_specs=pl.BlockSpec((None, nq, tq), lambda b, i: (b, 0, i)),
        out_shape=jax.ShapeDtypeStruct((bsz, nq, s), BF16),
        scratch_shapes=[pltpu.VMEM((tq + 2 * WINDOW + lc, 2 * tq), F32)] * 3,
        compiler_params=_cparams(("parallel", "arbitrary"), 40),
        name="gqa_window_attn",
    )(qg, kg, vg, sink_rows)


def _mla_update(s_ref, s_max, vt, m, acc):
    m_new = jnp.maximum(m, s_max)
    alpha = jnp.exp2(m - m_new)
    p = jnp.exp2(s_ref[...] - m_new).astype(BF16)
    acc = alpha * acc + _dot(vt, p)
    return m_new, acc


def _mla_kernel(q_ref, k_ref, v_ref, o_ref, s_a, s_b, s_c, *, nc, ctx_only, tq):
    for t in range(q_ref.shape[1] // tq):
        lanes = slice(t * tq, (t + 1) * tq)
        _mla_tile(q_ref[:, lanes], k_ref, v_ref, o_ref.at[:, lanes], s_a, s_b, s_c, nc, ctx_only)


def _mla_tile(q, k_ref, v_ref, o_ref, s_a, s_b, s_c, nc, ctx_only):
    tq = q.shape[1]
    ck = KV_CHUNK
    m = jnp.full((1, tq), NEG_BIG, F32)
    acc = jnp.zeros((V_ROWS, tq), F32)

    def score(k, dst):
        s = _dot(k, q)
        dst[...] = s
        return jnp.max(s, axis=0, keepdims=True)

    def score_chunk(j, dst):
        return score(k_ref[pl.ds(pl.multiple_of(j * ck, ck), ck), :], dst)

    if ctx_only:
        s_x = s_a.at[0:KV_SUB, :]
        mx = score(k_ref[k_ref.shape[0] - KV_SUB:, :], s_x)
        m, acc = _mla_update(s_x, mx, v_ref[v_ref.shape[0] - 1][:, ck - KV_SUB:], m, acc)
    elif nc == 1:
        m, acc = _mla_update(s_a, score_chunk(0, s_a), v_ref[0], m, acc)
    else:
        mx_a = score_chunk(0, s_a)
        mx_b = score_chunk(1, s_b)

        def triple(j, c):
            m, acc, mx_a, mx_b = c
            mx_c = score_chunk(j + 2, s_c)
            m, acc = _mla_update(s_a, mx_a, v_ref[j], m, acc)
            mx_a = score_chunk(j + 3, s_a)
            m, acc = _mla_update(s_b, mx_b, v_ref[j + 1], m, acc)
            mx_b = score_chunk(j + 4, s_b)
            m, acc = _mla_update(s_c, mx_c, v_ref[j + 2], m, acc)
            return m, acc, mx_a, mx_b

        def triples(i, c):
            for u in range(MLA_UNROLL):
                c = triple(3 * (MLA_UNROLL * i + u), c)
            return c

        nt = (nc - 2) // 3
        c = (m, acc, mx_a, mx_b)
        if nt >= MLA_UNROLL:
            c = lax.fori_loop(0, nt // MLA_UNROLL, triples, c)
        for t in range(nt - nt % MLA_UNROLL, nt):
            c = triple(3 * t, c)
        m, acc, mx_a, mx_b = c
        j = 3 * nt
        left = nc - j
        if left >= 3:
            mx_c = score_chunk(j + 2, s_c)
        m, acc = _mla_update(s_a, mx_a, v_ref[j], m, acc)
        if left == 4:
            mx_a = score_chunk(j + 3, s_a)
        m, acc = _mla_update(s_b, mx_b, v_ref[j + 1], m, acc)
        if left >= 3:
            m, acc = _mla_update(s_c, mx_c, v_ref[j + 2], m, acc)
        if left == 4:
            m, acc = _mla_update(s_a, mx_a, v_ref[j + 3], m, acc)
    o_ref[...] = (acc[:MLA_V_DIM] / acc[MLA_V_DIM:MLA_V_DIM + 1]).astype(o_ref.dtype)


def _mla_call(qm, km, vm, tq, sub, q_tile0, n_q, ctx_only):
    bsz, _, s = qm.shape
    nc = vm.shape[1]
    wq = tq * sub
    if ctx_only:
        k_spec = pl.BlockSpec((None, KV_SUB, MLA_HEAD_PAD), lambda b, h, i: (b, s // KV_SUB - 1, h))
        v_spec = pl.BlockSpec((None, 1, V_ROWS, KV_CHUNK), lambda b, h, i: (b, nc - 1, h, 0))
    else:
        k_spec = pl.BlockSpec((None, s, MLA_HEAD_PAD), lambda b, h, i: (b, 0, h))
        v_spec = pl.BlockSpec((None, nc, V_ROWS, KV_CHUNK), lambda b, h, i: (b, 0, h, 0))
    return pl.pallas_call(
        functools.partial(_mla_kernel, nc=nc, ctx_only=ctx_only, tq=tq),
        grid=(bsz, MLA_HEADS, n_q),
        in_specs=[pl.BlockSpec((None, MLA_HEAD_PAD, wq), lambda b, h, i: (b, h, q_tile0 + i)), k_spec, v_spec],
        out_specs=pl.BlockSpec((None, MLA_V_DIM, wq), lambda b, h, i: (b, h, i)),
        out_shape=jax.ShapeDtypeStruct((bsz, MLA_HEADS * MLA_V_DIM, n_q * wq), BF16),
        scratch_shapes=[pltpu.VMEM((KV_CHUNK, tq), F32)] * 3,
        compiler_params=_cparams(("parallel", "arbitrary", "arbitrary"), 40),
        name="mla_attn_ctx" if ctx_only else "mla_attn",
    )(qm, km, vm)


def _mla(qm, km, vm, lat, lc):
    tq = min(MLA_TQ, lat)
    sub = MLA_SUBTILES if lat % (tq * MLA_SUBTILES) == 0 else 1
    assert lc == KV_SUB and lat % lc == 0
    return (_mla_call(qm, km, vm, tq, sub, 0, lat // (tq * sub), False),
            _mla_call(qm, km, vm, lc, 1, lat // lc, 1, True))


def _merge_kernel(x_ref, mod_ref, hl_refs, hc_refs, sk_ref, yg_ref, yml_ref, ymc_ref, gt_ref, wb_ref, wo_ref, o_ref,
                  *, lat_tiles):
    d = D_MODEL
    is_lat = pl.program_id(1) < lat_tiles
    conv, x0, z = [jnp.where(is_lat, a[...], b[...]).astype(F32) for a, b in zip(hl_refs, hc_refs)]
    yh = (x0 * (conv + z * sk_ref[...])).astype(BF16)
    yg = yg_ref[...].astype(F32).T.astype(BF16)
    ym = jnp.where(is_lat, yml_ref[...], ymc_ref[...]).astype(F32).T.astype(BF16)
    merged = (gt_ref[:, 0:d].astype(F32) * _dot(yh, wb_ref[0])
              + gt_ref[:, d:2 * d].astype(F32) * _dot(yg, wb_ref[1])
              + gt_ref[:, 2 * d:].astype(F32) * _dot(ym, wb_ref[2]))
    res = _dot(merged.astype(BF16), wo_ref[...])
    o_ref[...] = x_ref[...] + mod_ref[:, 2 * d:3 * d] * res


def _merge(xs, modsel, hy_lat, hy_ctx, skip, yg, ym_lat, ym_ctx, gates, wb, wo, lat_tiles, n_tiles):
    bsz, _, d = xs.shape
    tm = ROW_TILE
    c = HY_WIDTH
    row = lambda b, i: (b, i, 0)
    col = lambda b, i: (b, 0, i)
    lat_row = pl.BlockSpec((None, tm, c), lambda b, i: (b, jnp.minimum(i, lat_tiles - 1), 0))
    ctx_row = pl.BlockSpec((None, tm, c), lambda b, i: (b, jnp.maximum(i - lat_tiles, 0), 0))
    return pl.pallas_call(
        functools.partial(_merge_kernel, lat_tiles=lat_tiles),
        grid=(bsz, n_tiles),
        in_specs=[
            pl.BlockSpec((None, tm, d), row),
            pl.BlockSpec((None, None, 1, 6 * d), lambda b, i: (b, jnp.where(i >= lat_tiles, 1, 0), 0, 0)),
            [lat_row] * 3, [ctx_row] * 3, _const_spec(skip.shape),
            pl.BlockSpec((None, c, tm), col),
            pl.BlockSpec((None, c, tm), lambda b, i: (b, 0, jnp.minimum(i, lat_tiles - 1))),
            pl.BlockSpec((None, c, tm), lambda b, i: (b, 0, jnp.maximum(i - lat_tiles, 0))),
            pl.BlockSpec((None, tm, 3 * d), row),
            _const_spec(wb.shape), _const_spec(wo.shape),
        ],
        out_specs=pl.BlockSpec((None, tm, d), row),
        out_shape=jax.ShapeDtypeStruct((bsz, n_tiles * tm, d), F32),
        compiler_params=_cparams(("parallel", "arbitrary"), 40),
        name="branch_merge",
    )(xs, modsel, list(hy_lat), list(hy_ctx), skip, yg, ym_lat, ym_ctx, gates, wb, wo)


def _mlp_kernel(x_ref, mod_ref, g_ref, w1_ref, w2_ref, o_ref):
    d = D_MODEL
    x = x_ref[...]
    xn = x * lax.rsqrt(jnp.mean(x * x, axis=-1, keepdims=True) + EPS) * g_ref[...]
    h = (xn * (1.0 + mod_ref[:, 4 * d:5 * d]) + mod_ref[:, 3 * d:4 * d]).astype(BF16)
    acc = jnp.zeros(x.shape, F32)
    for j in range(D_FF // FF_CHUNK):
        sl = slice(j * FF_CHUNK, (j + 1) * FF_CHUNK)
        a = jnp.maximum(_dot(h, w1_ref[:, sl]), 0.0)
        acc = acc + _dot((a * a).astype(BF16), w2_ref[sl, :])
    o_ref[...] = x + mod_ref[:, 5 * d:] * acc


def _mlp(xs, modsel, g, w1, w2, lat_tiles):
    bsz, s, d = xs.shape
    tm = ROW_TILE
    row = lambda b, i: (b, i, 0)
    return pl.pallas_call(
        _mlp_kernel,
        grid=(bsz, s // tm),
        in_specs=[
            pl.BlockSpec((None, tm, d), row),
            pl.BlockSpec((None, None, 1, 6 * d), lambda b, i: (b, jnp.where(i >= lat_tiles, 1, 0), 0, 0)),
            _const_spec(g.shape), _const_spec(w1.shape), _const_spec(w2.shape),
        ],
        out_specs=pl.BlockSpec((None, tm, d), row),
        out_shape=jax.ShapeDtypeStruct(xs.shape, F32),
        compiler_params=_cparams(("parallel", "arbitrary"), 56),
        name="relu2_mlp",
    )(xs, modsel, g, w1, w2)


def _rope_tables_t(rows, dim, lc):
    n_freq = dim // 4
    inv = ROPE_BASE ** (-jnp.arange(n_freq, dtype=F32) / n_freq)
    r = jnp.repeat(jnp.arange(rows, dtype=F32), GRID_W)
    col = jnp.tile(jnp.arange(GRID_W, dtype=F32), rows)
    ang = jnp.concatenate([r[:, None] * inv, col[:, None] * inv], axis=-1)
    cos_t = jnp.concatenate([jnp.cos(ang).T, jnp.ones((dim // 2, lc), F32)], axis=1)
    sin_t = jnp.concatenate([jnp.sin(ang).T, jnp.zeros((dim // 2, lc), F32)], axis=1)
    return cos_t, sin_t


def _lane_bcast(v):
    return jnp.broadcast_to(v.astype(F32)[:, None], (v.shape[0], ROW_TILE))


def _layer_weights(l, w_in, gqa_q_norm, gqa_k_norm, mla_q_a_norm, mla_kv_a_norm, w_q_b, w_kv_b, mla_q_norm, mla_k_norm):
    w = w_in[l]
    o = np.cumsum([0, 3 * HY_WIDTH, GQA_HEADS * GQA_HEAD_DIM, GQA_KV_HEADS * GQA_HEAD_DIM,
                   GQA_KV_HEADS * GQA_HEAD_DIM, MLA_Q_RANK, MLA_KV_RANK, MLA_ROPE_DIM, 3 * D_MODEL])
    wu = w[:, o[0]:o[1]].astype(BF16)
    wt = w[:, o[1]:o[7]].T.astype(BF16)
    wg = w[:, o[7]:o[8]].astype(BF16)
    wqb = w_q_b[l].T.astype(BF16)
    wkvb = w_kv_b[l].T.astype(BF16)
    return (wu, wg, wt, wqb, wkvb,
            _lane_bcast(gqa_q_norm[l]), _lane_bcast(gqa_k_norm[l]),
            _lane_bcast(mla_q_a_norm[l]), _lane_bcast(mla_kv_a_norm[l]),
            _lane_bcast(mla_q_norm[l]), _lane_bcast(mla_k_norm[l]))


def kernel(x, c, ctx, c_ctx, w_mod, b_mod, norm_mix_g, norm_mlp_g, w_in, hy_short_w, hy_f1_w, hy_f1_b, hy_f2_w, hy_f2_b, hy_sin_freq, hy_f3_w, hy_skip, gqa_q_norm, gqa_k_norm, gqa_sink, mla_q_a_norm, mla_kv_a_norm, w_q_b, w_kv_b, mla_q_norm, mla_k_norm, w_branch, w_out, w_mlp1, w_mlp2):
    bsz, lat, d = x.shape
    lc = ctx.shape[1]
    depth = w_mod.shape[0]
    s = lat + lc
    assert d == D_MODEL and lat % GRID_W == 0 and lat % ROW_TILE == 0 and lc == ROW_TILE and s % KV_CHUNK == 0
    lat_tiles = lat // ROW_TILE

    pad = (-(bsz + 1)) % 8
    cond = jnp.concatenate([c, c_ctx[None, :], jnp.zeros((pad, d), F32)], axis=0)
    mods = _mod_all(cond, w_mod, b_mod)

    tabs = _rope_tables_t(lat // GRID_W, GQA_HEAD_DIM, lc) + _rope_tables_t(lat // GRID_W, MLA_ROPE_DIM, lc)
    xs = jnp.concatenate([x, ctx], axis=1)

    for l in range(depth):
        ml = mods[l]
        modsel = jnp.stack([ml[:bsz], jnp.broadcast_to(ml[bsz][None], (bsz, 6 * d))], axis=1)[:, :, None, :]
        wts = _layer_weights(l, w_in, gqa_q_norm, gqa_k_norm, mla_q_a_norm, mla_kv_a_norm, w_q_b, w_kv_b,
                             mla_q_norm, mla_k_norm)
        u, gates, qg, kg, vg, qm, km, vm = _proj(xs, modsel, norm_mix_g[l][None, :], wts, tabs, lat_tiles)

        f1w = jnp.zeros((V7X_LANES, HY_FILTER_WIDTH), F32).at[:HY_EMB_DIM].set(hy_f1_w[l])
        fw = (f1w, hy_f1_b[l][None, :], hy_f2_w[l], hy_f2_b[l][None, :], hy_sin_freq[l], hy_f3_w[l])
        hy_lat, hy_ctx = _hyena(u, hy_short_w[l], fw, lat, lc)

        sink_rows = jnp.broadcast_to(gqa_sink[l].astype(F32)[:, None], (GQA_HEADS, GQA_TQ))
        yg = _gqa(qg, kg, vg, sink_rows, lat, lc)
        ym_lat, ym_ctx = _mla(qm, km, vm, lat, lc)

        n_tiles = lat_tiles if l == depth - 1 else s // ROW_TILE
        xs = _merge(xs, modsel, hy_lat, hy_ctx, hy_skip[l][None, :], yg, ym_lat, ym_ctx, gates,
                    w_branch[l].astype(BF16), w_out[l].astype(BF16), lat_tiles, n_tiles)
        xs = _mlp(xs, modsel, norm_mlp_g[l][None, :], w_mlp1[l].astype(BF16), w_mlp2[l].astype(BF16), lat_tiles)
    return xs
```

```python
import functools
import math

import numpy as np
import jax
import jax.numpy as jnp
from jax import lax
from jax.experimental import pallas as pl
from jax.experimental.pallas import tpu as pltpu

D_MODEL = 1024
GRID_W = 64
HY_WIDTH = 512
HY_EMB_DIM = 33
HY_BANDS = (HY_EMB_DIM - 1) // 2
HY_FILTER_WIDTH = 64
HY_DECAY_TARGET = 1e-2
HY_FAST_DECAY = 0.3
HY_SLOW_DECAY = 1.5
GQA_HEADS = 8
GQA_KV_HEADS = 2
GQA_GROUP = GQA_HEADS // GQA_KV_HEADS
GQA_HEAD_DIM = 64
GQA_SCALE = GQA_HEAD_DIM ** -0.5
WINDOW = 128
MLA_HEADS = 8
MLA_Q_RANK = 384
MLA_KV_RANK = 256
MLA_NOPE_DIM = 64
MLA_ROPE_DIM = 32
MLA_V_DIM = 64
MLA_QK_DIM = MLA_NOPE_DIM + MLA_ROPE_DIM
MLA_SCALE = MLA_QK_DIM ** -0.5
D_FF = 4 * D_MODEL
ROPE_BASE = 10000.0
EPS = 1e-6
LOG2E = 1.4426950408889634
NEG_BIG = -1e30

V7X_LANES = 128
V7X_VMEM_BYTES = 64 * 1024 * 1024

ROW_TILE = 256
PROJ_BATCH = 2
CONV_TILE = 512
MLA_HEAD_PAD = 128
V_ROWS = 80
KV_CHUNK = 1 * ROW_TILE
KV_SUB = 256
MLA_UNROLL = 5
MLA_TQ = 1024
MLA_SUBTILES = 2
GQA_TQ = 256
FF_CHUNK = 1024
FFT_N2 = 128
FFT_NB = 16
FFT_KB = 8

F32 = jnp.float32
BF16 = jnp.bfloat16


def _cparams(sem, vmem_mb, flags=None):
    return pltpu.CompilerParams(dimension_semantics=sem, vmem_limit_bytes=vmem_mb * 1024 * 1024, flags=flags)


def _dot(a, b):
    return jnp.dot(a, b, preferred_element_type=F32)


def _dot_hi(a, b):
    return jnp.dot(a, b, preferred_element_type=F32, precision=lax.Precision.HIGHEST)


def _ones_row_block(width):
    r = lax.broadcasted_iota(jnp.int32, (V_ROWS - MLA_V_DIM, width), 0)
    return jnp.where(r == 0, 1.0, 0.0).astype(BF16)


def _const_spec(shape):
    nd = len(shape)
    return pl.BlockSpec(shape, lambda *_: (0,) * nd, pipeline_mode=pl.Buffered(1))


def _mod_kernel(c_ref, w_ref, b_ref, o_ref):
    c = c_ref[...]
    s = c * jax.nn.sigmoid(c)
    o_ref[...] = _dot_hi(s, w_ref[...]) + b_ref[...]


def _mod_all(cond, w_mod, b_mod):
    depth, d, n = w_mod.shape
    rows = cond.shape[0]
    tn = 1536
    return pl.pallas_call(
        _mod_kernel,
        grid=(depth, n // tn),
        in_specs=[
            pl.BlockSpec((rows, d), lambda l, j: (0, 0)),
            pl.BlockSpec((None, d, tn), lambda l, j: (l, 0, j)),
            pl.BlockSpec((None, 1, tn), lambda l, j: (l, 0, j)),
        ],
        out_specs=pl.BlockSpec((None, rows, tn), lambda l, j: (l, 0, j)),
        out_shape=jax.ShapeDtypeStruct((depth, rows, n), F32),
        compiler_params=_cparams(("arbitrary", "arbitrary"), 40),
        name="adaln_mod",
    )(cond, w_mod, b_mod.reshape(depth, 1, n))


def _rms_rows(x, n):
    return lax.rsqrt(jnp.sum(x * x, axis=0, keepdims=True) * (1.0 / n) + EPS)


def _rope_rows(x1, x2, cs, sn):
    return x1 * cs - x2 * sn, x1 * sn + x2 * cs


def _proj_kernel(x_ref, mod_ref, *refs):
    consts, outs = refs[:16], refs[16:]
    for n in range(x_ref.shape[0]):
        _proj_one(x_ref.at[n], mod_ref.at[n], *consts, *[o.at[n] for o in outs])


def _proj_one(x_ref, mod_ref, g_ref, wu_ref, wg_ref, wt_ref, wqb_ref, wkvb_ref,
              gq_ref, gk_ref, gqa_ref, gkva_ref, gmq_ref, gmk_ref,
              cg_ref, sg_ref, cm_ref, sm_ref,
              u_ref, gate_ref, qg_ref, kg_ref, vg_ref, qm_ref, km_ref, vm_ref):
    d = D_MODEL
    x = x_ref[...]
    tm = x.shape[0]
    shift = mod_ref[:, 0:d]
    scale = mod_ref[:, d:2 * d]
    xn = x * lax.rsqrt(jnp.mean(x * x, axis=-1, keepdims=True) + EPS) * g_ref[...]
    h = xn * (1.0 + scale) + shift
    hb = h.astype(BF16)

    u_ref[...] = _dot(hb, wu_ref[...]).astype(BF16)
    gate_ref[...] = jax.nn.sigmoid(_dot(hb, wg_ref[...])).astype(BF16)

    ht = h.T.astype(BF16)
    t = _dot(wt_ref[...], ht)
    o_q, o_k, o_v = 0, 512, 640
    o_cq, o_ckv, o_kr = 768, 768 + MLA_Q_RANK, 768 + MLA_Q_RANK + MLA_KV_RANK

    cg, sg = cg_ref[...], sg_ref[...]
    cm, sm = cm_ref[...], sm_ref[...]
    hd, hh = GQA_HEAD_DIM, GQA_HEAD_DIM // 2

    gq = gq_ref[...]
    for n in range(GQA_HEADS):
        xh = t[o_q + n * hd:o_q + (n + 1) * hd]
        xh = xh * _rms_rows(xh, hd) * gq
        a, b = _rope_rows(xh[:hh], xh[hh:], cg, sg)
        qg_ref[n * hd:n * hd + hh, :] = (a * (GQA_SCALE * LOG2E)).astype(BF16)
        qg_ref[n * hd + hh:(n + 1) * hd, :] = (b * (GQA_SCALE * LOG2E)).astype(BF16)

    gk = gk_ref[...]
    zpad = jnp.zeros((V7X_LANES - hd, tm), F32)
    parts = []
    for n in range(GQA_KV_HEADS):
        xh = t[o_k + n * hd:o_k + (n + 1) * hd]
        xh = xh * _rms_rows(xh, hd) * gk
        a, b = _rope_rows(xh[:hh], xh[hh:], cg, sg)
        parts += [a, b, zpad]
    kg_ref[...] = jnp.concatenate(parts, axis=0).T.astype(BF16)
    ones_rows = _ones_row_block(tm)
    for n in range(GQA_KV_HEADS):
        vg_ref[n * V_ROWS:n * V_ROWS + hd, :] = t[o_v + n * hd:o_v + (n + 1) * hd].astype(BF16)
        vg_ref[n * V_ROWS + hd:(n + 1) * V_ROWS, :] = ones_rows

    cq = t[o_cq:o_cq + MLA_Q_RANK]
    cqn = (cq * _rms_rows(cq, MLA_Q_RANK) * gqa_ref[...]).astype(BF16)
    qm = _dot(wqb_ref[...], cqn)
    gmq = gmq_ref[...]
    nd, rh = MLA_NOPE_DIM, MLA_ROPE_DIM // 2
    qpad = jnp.zeros((MLA_HEAD_PAD - MLA_QK_DIM, tm), BF16)
    for n in range(MLA_HEADS):
        xh = qm[n * MLA_QK_DIM:(n + 1) * MLA_QK_DIM]
        xh = xh * _rms_rows(xh, MLA_QK_DIM) * gmq * (MLA_SCALE * LOG2E)
        a, b = _rope_rows(xh[nd:nd + rh], xh[nd + rh:], cm, sm)
        base = n * MLA_HEAD_PAD
        qm_ref[base:base + nd, :] = xh[:nd].astype(BF16)
        qm_ref[base + nd:base + nd + rh, :] = a.astype(BF16)
        qm_ref[base + nd + rh:base + MLA_QK_DIM, :] = b.astype(BF16)
        qm_ref[base + MLA_QK_DIM:base + MLA_HEAD_PAD, :] = qpad

    ckv = t[o_ckv:o_ckv + MLA_KV_RANK]
    ckvn = (ckv * _rms_rows(ckv, MLA_KV_RANK) * gkva_ref[...]).astype(BF16)
    kv = _dot(wkvb_ref[...], ckvn)
    kr = t[o_kr:o_kr + MLA_ROPE_DIM]
    kr_ss = jnp.sum(kr * kr, axis=0, keepdims=True)
    gmk = gmk_ref[...]
    kzero = jnp.zeros((MLA_HEAD_PAD - MLA_QK_DIM, tm), F32)
    ones_rows = _ones_row_block(tm)
    parts = []
    for n in range(MLA_HEADS):
        kn = kv[n * 128:n * 128 + nd]
        vm_ref[n * V_ROWS:n * V_ROWS + MLA_V_DIM, :] = kv[n * 128 + nd:(n + 1) * 128].astype(BF16)
        vm_ref[n * V_ROWS + MLA_V_DIM:(n + 1) * V_ROWS, :] = ones_rows
        rs = lax.rsqrt((jnp.sum(kn * kn, axis=0, keepdims=True) + kr_ss) * (1.0 / MLA_QK_DIM) + EPS)
        krn = kr * rs * gmk[nd:]
        a, b = _rope_rows(krn[:rh], krn[rh:], cm, sm)
        parts += [kn * rs * gmk[:nd], a, b, kzero]
    km_ref[...] = jnp.concatenate(parts, axis=0).T.astype(BF16)


def _proj(xs, modsel, g, wts, tabs, lat_tiles):
    bsz, s, d = xs.shape
    tm = ROW_TILE
    nt = s // tm
    ncs = KV_CHUNK // tm
    nb = PROJ_BATCH if bsz % PROJ_BATCH == 0 else 1
    (wu, wg, wt, wqb, wkvb, gq, gk, gqa, gkva, gmq, gmk) = wts
    cg, sg, cm, sm = tabs
    row = lambda b, i: (b, i, 0)
    col = lambda b, i: (b, 0, i)
    tab = lambda b, i: (0, i)
    in_specs = [
        pl.BlockSpec((nb, tm, d), row),
        pl.BlockSpec((nb, None, 1, 6 * d), lambda b, i: (b, jnp.where(i >= lat_tiles, 1, 0), 0, 0)),
        _const_spec(g.shape), _const_spec(wu.shape), _const_spec(wg.shape), _const_spec(wt.shape),
        _const_spec(wqb.shape), _const_spec(wkvb.shape),
        _const_spec(gq.shape), _const_spec(gk.shape), _const_spec(gqa.shape), _const_spec(gkva.shape),
        _const_spec(gmq.shape), _const_spec(gmk.shape),
        pl.BlockSpec((cg.shape[0], tm), tab), pl.BlockSpec((sg.shape[0], tm), tab),
        pl.BlockSpec((cm.shape[0], tm), tab), pl.BlockSpec((sm.shape[0], tm), tab),
    ]
    out_shape = [
        jax.ShapeDtypeStruct((bsz, s, 3 * HY_WIDTH), BF16),
        jax.ShapeDtypeStruct((bsz, s, 3 * d), BF16),
        jax.ShapeDtypeStruct((bsz, GQA_HEADS * GQA_HEAD_DIM, s), BF16),
        jax.ShapeDtypeStruct((bsz, s, GQA_KV_HEADS * V7X_LANES), BF16),
        jax.ShapeDtypeStruct((bsz, GQA_KV_HEADS * V_ROWS, s), BF16),
        jax.ShapeDtypeStruct((bsz, MLA_HEADS * MLA_HEAD_PAD, s), BF16),
        jax.ShapeDtypeStruct((bsz, s, MLA_HEADS * MLA_HEAD_PAD), BF16),
        jax.ShapeDtypeStruct((bsz, s // KV_CHUNK, MLA_HEADS * V_ROWS, KV_CHUNK), BF16),
    ]
    out_specs = [
        pl.BlockSpec((nb, tm, 3 * HY_WIDTH), row),
        pl.BlockSpec((nb, tm, 3 * d), row),
        pl.BlockSpec((nb, GQA_HEADS * GQA_HEAD_DIM, tm), col),
        pl.BlockSpec((nb, tm, GQA_KV_HEADS * V7X_LANES), row),
        pl.BlockSpec((nb, GQA_KV_HEADS * V_ROWS, tm), col),
        pl.BlockSpec((nb, MLA_HEADS * MLA_HEAD_PAD, tm), col),
        pl.BlockSpec((nb, tm, MLA_HEADS * MLA_HEAD_PAD), row),
        pl.BlockSpec((nb, None, MLA_HEADS * V_ROWS, tm), lambda b, i: (b, i // ncs, 0, i % ncs)),
    ]
    return pl.pallas_call(
        _proj_kernel,
        grid=(bsz // nb, nt),
        in_specs=in_specs,
        out_specs=out_specs,
        out_shape=out_shape,
        compiler_params=_cparams(("parallel", "arbitrary"), 56),
        name="in_proj",
    )(xs, modsel, g, wu, wg, wt, wqb, wkvb, gq, gk, gqa, gkva, gmq, gmk, cg, sg, cm, sm)


def _short_conv_kernel(u_ref, up_ref, un_ref, w_ref, x0_ref, z_ref, *, n_tiles):
    i = pl.program_id(1)
    u = u_ref[...].astype(F32)
    tm = u.shape[0]
    prev = jnp.where(i > 0, up_ref[7:8, :].astype(F32), 0.0)
    nxt = jnp.where(i < n_tiles - 1, un_ref[0:1, :].astype(F32), 0.0)
    ridx = lax.broadcasted_iota(jnp.int32, u.shape, 0)
    up = jnp.where(ridx == 0, prev, pltpu.roll(u, 1, axis=0))
    dn = jnp.where(ridx == tm - 1, nxt, pltpu.roll(u, tm - 1, axis=0))
    uc = up * w_ref[0:1, :] + u * w_ref[1:2, :] + dn * w_ref[2:3, :]
    c = HY_WIDTH
    x0_ref[...] = uc[:, :c].astype(BF16)
    z_ref[...] = (uc[:, c:2 * c] * uc[:, 2 * c:]).astype(BF16)


def _short_conv(u, short_w, row0, rows):
    bsz, s, c3 = u.shape
    tm = math.gcd(math.gcd(rows, row0), CONV_TILE) if row0 else math.gcd(rows, CONV_TILE)
    nt = rows // tm
    t0 = row0 // tm
    r8 = tm // 8
    last8 = s // 8 - 1
    return pl.pallas_call(
        functools.partial(_short_conv_kernel, n_tiles=nt),
        grid=(bsz, nt),
        in_specs=[
            pl.BlockSpec((None, tm, c3), lambda b, i: (b, t0 + i, 0)),
            pl.BlockSpec((None, 8, c3), lambda b, i: (b, jnp.maximum((t0 + i) * r8 - 1, 0), 0)),
            pl.BlockSpec((None, 8, c3), lambda b, i: (b, jnp.minimum((t0 + i + 1) * r8, last8), 0)),
            _const_spec(short_w.shape),
        ],
        out_specs=[pl.BlockSpec((None, tm, HY_WIDTH), lambda b, i: (b, i, 0))] * 2,
        out_shape=[jax.ShapeDtypeStruct((bsz, rows, HY_WIDTH), BF16)] * 2,
        compiler_params=_cparams(("parallel", "arbitrary"), 32),
        name="hyena_short_conv",
    )(u, u, u, short_w)


def _filter_kernel(zf_ref, w1_ref, b1_ref, w2_ref, b2_ref, fr_ref, w3_ref, dl_ref, h_ref, ss_ref):
    i = pl.program_id(0)
    zf = zf_ref[...]
    tl = zf.shape[0]
    h = jnp.sin(fr_ref[0:1, :] * (_dot_hi(zf, w1_ref[...]) + b1_ref[...]))
    h = jnp.sin(fr_ref[1:2, :] * (_dot_hi(h, w2_ref[...]) + b2_ref[...]))
    h = _dot_hi(h, w3_ref[...])
    decay = jnp.exp(-zf[:, 0:1] * dl_ref[...])
    c = HY_WIDTH
    hf = h[:, :c] * decay
    ridx = lax.broadcasted_iota(jnp.int32, (tl, c), 0) + i * tl
    hb = jnp.where(ridx == 0, 0.0, h[:, c:] * decay)
    h_ref[0] = hf.astype(BF16)
    h_ref[1] = hb.astype(BF16)
    ss = jnp.sum(hf * hf + hb * hb, axis=0, keepdims=True)

    @pl.when(i == 0)
    def _():
        ss_ref[...] = ss

    @pl.when(i > 0)
    def _():
        ss_ref[...] += ss


def _filter(zfeat, fw, deltas):
    length = zfeat.shape[0]
    tl = min(length, 1024)
    w1, b1, w2, b2, fr, w3 = fw
    return pl.pallas_call(
        _filter_kernel,
        grid=(length // tl,),
        in_specs=[pl.BlockSpec((tl, zfeat.shape[1]), lambda i: (i, 0))]
        + [_const_spec(a.shape) for a in (w1, b1, w2, b2, fr, w3, deltas)],
        out_specs=[pl.BlockSpec((2, tl, HY_WIDTH), lambda i: (0, i, 0)),
                   pl.BlockSpec((1, HY_WIDTH), lambda i: (0, 0))],
        out_shape=[jax.ShapeDtypeStruct((2, length, HY_WIDTH), BF16),
                   jax.ShapeDtypeStruct((1, HY_WIDTH), F32)],
        compiler_params=_cparams(("arbitrary",), 40),
        name="hyena_filter",
    )(zfeat, w1, b1, w2, b2, fr, w3, deltas)


def _fft_a_kernel(g_ref, x_ref, o_ref, *, nb, cw):
    for j in range(nb):
        sl = slice(j * cw, (j + 1) * cw)
        o_ref[:, sl] = _dot(g_ref[j], x_ref[:, sl]).astype(o_ref.dtype)


def _fft_a(x2d, gmat):
    bx, k1, w = x2d.shape
    n2, two_n1, _ = gmat.shape
    cw = w // n2
    nb = FFT_NB
    return pl.pallas_call(
        functools.partial(_fft_a_kernel, nb=nb, cw=cw),
        grid=(bx, n2 // nb),
        in_specs=[pl.BlockSpec((nb, two_n1, k1), lambda b, j: (j, 0, 0)),
                  pl.BlockSpec((None, k1, nb * cw), lambda b, j: (b, 0, j))],
        out_specs=pl.BlockSpec((None, two_n1, nb * cw), lambda b, j: (b, 0, j)),
        out_shape=jax.ShapeDtypeStruct((bx, two_n1, w), BF16),
        compiler_params=_cparams(("parallel", "arbitrary"), 40),
        name="fft_stage_a",
    )(gmat, x2d)


def _fft_filter_b_kernel(fb_ref, a_ref, ss_ref, kf_ref, *, inv_n):
    n = a_ref.shape[3]
    rs = lax.rsqrt(ss_ref[...] + EPS) * inv_n
    for r in range(a_ref.shape[2]):
        xf = _dot(fb_ref[...], jnp.concatenate([a_ref[0, 0, r], a_ref[0, 1, r]], axis=0))
        xb = _dot(fb_ref[...], jnp.concatenate([a_ref[1, 0, r], a_ref[1, 1, r]], axis=0))
        kf_ref[0, r] = (xf[:n] + xb[:n]) * rs
        kf_ref[1, r] = (xf[n:] - xb[n:]) * rs


def _fft_filter_b(a5, fb, ssq, inv_n):
    _, _, n1, n2, c = a5.shape
    kb = min(FFT_KB, n1)
    return pl.pallas_call(
        functools.partial(_fft_filter_b_kernel, inv_n=inv_n),
        grid=(n1 // kb,),
        in_specs=[_const_spec(fb.shape),
                  pl.BlockSpec((2, 2, kb, n2, c), lambda k: (0, 0, k, 0, 0)),
                  _const_spec(ssq.shape)],
        out_specs=pl.BlockSpec((2, kb, n2, c), lambda k: (0, k, 0, 0)),
        out_shape=jax.ShapeDtypeStruct((2, n1, n2, c), F32),
        compiler_params=_cparams(("arbitrary",), 32),
        name="fft_filter_stage_b",
    )(fb, a5, ssq)


def _fft_b_kernel(fb_ref, fbi_ref, a_ref, kf_ref, o_ref):
    kb, n, cw = a_ref.shape[1:]
    a_all = jnp.concatenate([jnp.concatenate([a_ref[0, r], a_ref[1, r]], axis=0) for r in range(kb)], axis=1)
    x = _dot(fb_ref[...], a_all)
    ys = []
    for r in range(kb):
        xr, xi = x[:n, r * cw:(r + 1) * cw], x[n:, r * cw:(r + 1) * cw]
        kr, ki = kf_ref[0, r], kf_ref[1, r]
        ys.append(jnp.concatenate([xr * kr - xi * ki, xr * ki + xi * kr], axis=0).astype(BF16))
    c = _dot(fbi_ref[...], jnp.concatenate(ys, axis=1))
    for r in range(kb):
        o_ref[0, r] = c[:n, r * cw:(r + 1) * cw].astype(o_ref.dtype)
        o_ref[1, r] = c[n:, r * cw:(r + 1) * cw].astype(o_ref.dtype)


def _fft_b(a5, kf, fb, fbi):
    bsz, _, n1, n2, c = a5.shape
    kb = min(FFT_KB, n1)
    return pl.pallas_call(
        _fft_b_kernel,
        grid=(n1 // kb, bsz),
        in_specs=[_const_spec(fb.shape), _const_spec(fbi.shape),
                  pl.BlockSpec((None, 2, kb, n2, c), lambda k, b: (b, 0, k, 0, 0)),
                  pl.BlockSpec((2, kb, n2, c), lambda k, b: (0, k, 0, 0))],
        out_specs=pl.BlockSpec((None, 2, kb, n2, c), lambda k, b: (b, 0, k, 0, 0)),
        out_shape=jax.ShapeDtypeStruct(a5.shape, BF16),
        compiler_params=_cparams(("arbitrary", "arbitrary"), 32),
        name="fft_stage_b",
    )(fb, fbi, a5, kf)


def _fft_c_kernel(h_ref, c_ref, o_ref, *, nb, cw):
    for j in range(nb):
        sl = slice(j * cw, (j + 1) * cw)
        o_ref[:, sl] = _dot(h_ref[j], c_ref[:, sl]).astype(o_ref.dtype)


def _fft_c(c2d, hmat):
    bsz, two_n1, w = c2d.shape
    n2, k1, _ = hmat.shape
    cw = w // n2
    nb = FFT_NB
    blk = lambda b, j: (b, 0, j)
    return pl.pallas_call(
        functools.partial(_fft_c_kernel, nb=nb, cw=cw),
        grid=(bsz, n2 // nb),
        in_specs=[pl.BlockSpec((nb, k1, two_n1), lambda b, j: (j, 0, 0)),
                  pl.BlockSpec((None, two_n1, nb * cw), blk)],
        out_specs=pl.BlockSpec((None, k1, nb * cw), blk),
        out_shape=jax.ShapeDtypeStruct((bsz, k1, w), BF16),
        compiler_params=_cparams(("parallel", "arbitrary"), 40),
        name="fft_stage_c",
    )(hmat, c2d)


def _ctx_conv_kernel(fc_ref, fci_ref, h_ref, ss_ref, z_ref, o_ref):
    n = fc_ref.shape[0] // 2
    fc = fc_ref[...]
    kf = _dot(fc, h_ref[0])
    kb = _dot(fc, h_ref[1])
    rs = lax.rsqrt(ss_ref[...] + EPS)
    kr = (kf[:n] + kb[:n]) * rs
    ki = (kf[n:] - kb[n:]) * rs
    x = _dot(fc, z_ref[...])
    xr, xi = x[:n], x[n:]
    y = jnp.concatenate([xr * kr - xi * ki, xr * ki + xi * kr], axis=0).astype(BF16)
    o_ref[...] = _dot(fci_ref[...], y).astype(o_ref.dtype)


def _ctx_conv(fc, fci, hfb, ssq, z):
    bsz, lc, c = z.shape
    blk = pl.BlockSpec((None, lc, c), lambda b: (b, 0, 0))
    return pl.pallas_call(
        _ctx_conv_kernel,
        grid=(bsz,),
        in_specs=[_const_spec(fc.shape), _const_spec(fci.shape), _const_spec(hfb.shape),
                  _const_spec(ssq.shape), blk],
        out_specs=blk,
        out_shape=jax.ShapeDtypeStruct((bsz, lc, c), BF16),
        compiler_params=_cparams(("arbitrary",), 32),
        name="hyena_ctx_conv",
    )(fc, fci, hfb, ssq, z)


@functools.lru_cache(maxsize=None)
def _fft_tables(length):
    n = 2 * length
    n2 = FFT_N2
    n1 = n // n2
    k1 = length // n2
    kk = np.arange(n1)[:, None]
    g = np.empty((n2, 2 * n1, k1), np.float64)
    h = np.empty((n2, k1, 2 * n1), np.float64)
    nn = np.arange(k1)[None, :]
    for j in range(n2):
        ang = 2.0 * np.pi * (((n2 * nn * kk) % n) + (j * kk) % n) / n
        g[j, :n1] = np.cos(ang)
        g[j, n1:] = -np.sin(ang)
        h[j, :, :n1] = np.cos(ang).T
        h[j, :, n1:] = -np.sin(ang).T
    a = np.arange(n2)
    ph = 2.0 * np.pi * ((a[:, None] * a[None, :]) % n2) / n2
    c, s = np.cos(ph), np.sin(ph)
    fb = np.block([[c, s], [-s, c]])
    fbi = np.block([[c, -s], [s, c]])
    return (jnp.asarray(g, BF16), jnp.asarray(h, BF16), jnp.asarray(fb, BF16), jnp.asarray(fbi, BF16), n1, k1)


@functools.lru_cache(maxsize=None)
def _dft_tables(length):
    n = 2 * length
    k = np.arange(n)[:, None]
    t = np.arange(length)[None, :]
    ang = 2.0 * np.pi * ((k * t) % n) / n
    fc = np.concatenate([np.cos(ang), -np.sin(ang)], axis=0)
    fci = np.concatenate([np.cos(ang).T, -np.sin(ang).T], axis=1) / n
    return jnp.asarray(fc, BF16), jnp.asarray(fci, BF16)


@functools.lru_cache(maxsize=None)
def _filter_features(length):
    t = np.linspace(0.0, 1.0, length, dtype=np.float32)[:, None]
    w = (2.0 * math.pi * np.arange(length, dtype=np.float32)[:, None] / length).astype(np.float32)
    f = np.linspace(1e-4, HY_BANDS - 1, HY_BANDS, dtype=np.float32)[None, :]
    z = np.concatenate([t, np.cos(f * w), -np.sin(f * w)], axis=-1).astype(np.float32)
    zp = np.zeros((length, V7X_LANES), np.float32)
    zp[:, :HY_EMB_DIM] = z
    return jnp.asarray(zp)


def _hyena_deltas():
    max_decay = math.log(HY_DECAY_TARGET) / HY_FAST_DECAY
    min_decay = math.log(HY_DECAY_TARGET) / HY_SLOW_DECAY
    return jnp.abs(jnp.linspace(min_decay, max_decay, HY_WIDTH, dtype=F32))[None, :]


def _hyena(u, short_w, fw, lat, lc):
    bsz = u.shape[0]
    c = HY_WIDTH
    deltas = _hyena_deltas()
    x0, z = _short_conv(u, short_w, 0, lat)
    gmat, hmat, fb, fbi, n1, k1 = _fft_tables(lat)
    n2 = FFT_N2
    hfb, ssq = _filter(_filter_features(lat), fw, deltas)
    fa = _fft_a(hfb.reshape(2, k1, n2 * c), gmat)
    kf = _fft_filter_b(fa.reshape(2, 2, n1, n2, c), fb, ssq, 1.0 / (2 * lat))
    za = _fft_a(z.reshape(bsz, k1, n2 * c), gmat)
    zc = _fft_b(za.reshape(bsz, 2, n1, n2, c), kf, fb, fbi)
    conv = _fft_c(zc.reshape(bsz, 2 * n1, n2 * c), hmat).reshape(bsz, lat, c)
    x0_c, z_c = _short_conv(u, short_w, lat, lc)
    hfb_c, ssq_c = _filter(_filter_features(lc), fw, deltas)
    fc, fci = _dft_tables(lc)
    conv_c = _ctx_conv(fc, fci, hfb_c, ssq_c, z_c)
    return (conv, x0, z), (conv_c, x0_c, z_c)


def _gqa_kernel(q_ref, k_ref, v_ref, sink_ref, o_ref, s_a, s_b, s_c, *, lat, lc):
    i = pl.program_id(1)
    tq = q_ref.shape[1]
    s_len = k_ref.shape[0]
    hd = GQA_HEAD_DIM
    w = WINDOW
    nwin = tq + 2 * w
    start = pl.multiple_of(jnp.clip(i * tq - w, 0, s_len - nwin), w)
    k_all = jnp.concatenate([k_ref[pl.ds(start, nwin), :], k_ref[lat:lat + lc, :]], axis=0)
    v_all = jnp.concatenate([v_ref[:, pl.ds(start, nwin)], v_ref[:, lat:lat + lc]], axis=1)
    q_pos = i * tq + lax.broadcasted_iota(jnp.int32, (1, tq), 1)
    k_pos = start + lax.broadcasted_iota(jnp.int32, (nwin, 1), 0)
    bias = (jnp.where(jnp.abs(k_pos - q_pos) <= w, 0.0, NEG_BIG)
            + jnp.where(k_pos < lat, 0.0, NEG_BIG)
            + jnp.where(q_pos < lat, 0.0, NEG_BIG))
    bias = jnp.concatenate([bias, jnp.zeros((lc, tq), F32)], axis=0)
    bias2 = jnp.concatenate([bias, bias], axis=1)
    zq = jnp.zeros((V7X_LANES - hd, 2 * tq), BF16)

    def score(pair, dst):
        g = pair // (GQA_GROUP // 2)
        r = 2 * pair * hd
        q2 = jnp.concatenate([q_ref[r:r + hd, :], q_ref[r + hd:r + 2 * hd, :]], axis=1)
        s = _dot(k_all[:, g * V7X_LANES:(g + 1) * V7X_LANES], jnp.concatenate([q2, zq], axis=0)) + bias2
        dst[...] = s
        return jnp.max(s, axis=0, keepdims=True)

    def update(pair, s_sc, s_max):
        g = pair // (GQA_GROUP // 2)
        r = 2 * pair * hd
        sink = jnp.concatenate([sink_ref[2 * pair:2 * pair + 1, :], sink_ref[2 * pair + 1:2 * pair + 2, :]],
                               axis=1) * LOG2E
        m = jnp.maximum(s_max, sink)
        p = jnp.exp2(s_sc[...] - m).astype(BF16)
        pv = _dot(v_all[g * V_ROWS:(g + 1) * V_ROWS, :], p)
        o = pv[:hd] / (pv[hd:hd + 1] + jnp.exp2(sink - m))
        o_ref[r:r + hd, :] = o[:, :tq].astype(o_ref.dtype)
        o_ref[r + hd:r + 2 * hd, :] = o[:, tq:].astype(o_ref.dtype)

    bufs = (s_a, s_b, s_c)
    npair = GQA_HEADS // 2
    mx = {0: score(0, bufs[0]), 1: score(1, bufs[1])}
    for j in range(npair):
        if j + 2 < npair:
            mx[j + 2] = score(j + 2, bufs[(j + 2) % 3])
        update(j, bufs[j % 3], mx[j])


def _gqa(qg, kg, vg, sink_rows, lat, lc):
    bsz, nq, s = qg.shape
    tq = GQA_TQ
    return pl.pallas_call(
        functools.partial(_gqa_kernel, lat=lat, lc=lc),
        grid=(bsz, s // tq),
        in_specs=[
            pl.BlockSpec((None, nq, tq), lambda b, i: (b, 0, i)),
            pl.BlockSpec((None, s, kg.shape[2]), lambda b, i: (b, 0, 0)),
            pl.BlockSpec((None, vg.shape[1], s), lambda b, i: (b, 0, 0)),
            _const_spec(sink_rows.shape),
        ],
        out_specs=pl.BlockSpec((None, nq, tq), lambda b, i: (b, 0, i)),
        out_shape=jax.ShapeDtypeStruct((bsz, nq, s), BF16),
        scratch_shapes=[pltpu.VMEM((tq + 2 * WINDOW + lc, 2 * tq), F32)] * 3,
        compiler_params=_cparams(("parallel", "arbitrary"), 40),
        name="gqa_window_attn",
    )(qg, kg, vg, sink_rows)


def _mla_update(s_ref, s_max, vt, m, acc):
    m_new = jnp.maximum(m, s_max)
    alpha = jnp.exp2(m - m_new)
    p = jnp.exp2(s_ref[...] - m_new).astype(BF16)
    acc = alpha * acc + _dot(vt, p)
    return m_new, acc


def _mla_kernel(q_ref, k_ref, v_ref, o_ref, s_a, s_b, s_c, *, nc, ctx_only, tq):
    for t in range(q_ref.shape[1] // tq):
        lanes = slice(t * tq, (t + 1) * tq)
        _mla_tile(q_ref[:, lanes], k_ref, v_ref, o_ref.at[:, lanes], s_a, s_b, s_c, nc, ctx_only)


def _mla_tile(q, k_ref, v_ref, o_ref, s_a, s_b, s_c, nc, ctx_only):
    tq = q.shape[1]
    ck = KV_CHUNK
    m = jnp.full((1, tq), NEG_BIG, F32)
    acc = jnp.zeros((V_ROWS, tq), F32)

    def score(k, dst):
        s = _dot(k, q)
        dst[...] = s
        return jnp.max(s, axis=0, keepdims=True)

    def score_chunk(j, dst):
        return score(k_ref[pl.ds(pl.multiple_of(j * ck, ck), ck), :], dst)

    if ctx_only:
        s_x = s_a.at[0:KV_SUB, :]
        mx = score(k_ref[k_ref.shape[0] - KV_SUB:, :], s_x)
        m, acc = _mla_update(s_x, mx, v_ref[v_ref.shape[0] - 1][:, ck - KV_SUB:], m, acc)
    elif nc == 1:
        m, acc = _mla_update(s_a, score_chunk(0, s_a), v_ref[0], m, acc)
    else:
        mx_a = score_chunk(0, s_a)
        mx_b = score_chunk(1, s_b)

        def triple(j, c):
            m, acc, mx_a, mx_b = c
            mx_c = score_chunk(j + 2, s_c)
            m, acc = _mla_update(s_a, mx_a, v_ref[j], m, acc)
            mx_a = score_chunk(j + 3, s_a)
            m, acc = _mla_update(s_b, mx_b, v_ref[j + 1], m, acc)
            mx_b = score_chunk(j + 4, s_b)
            m, acc = _mla_update(s_c, mx_c, v_ref[j + 2], m, acc)
            return m, acc, mx_a, mx_b

        def triples(i, c):
            for u in range(MLA_UNROLL):
                c = triple(3 * (MLA_UNROLL * i + u), c)
            return c

        nt = (nc - 2) // 3
        c = (m, acc, mx_a, mx_b)
        if nt >= MLA_UNROLL:
            c = lax.fori_loop(0, nt // MLA_UNROLL, triples, c)
        for t in range(nt - nt % MLA_UNROLL, nt):
            c = triple(3 * t, c)
        m, acc, mx_a, mx_b = c
        j = 3 * nt
        left = nc - j
        if left >= 3:
            mx_c = score_chunk(j + 2, s_c)
        m, acc = _mla_update(s_a, mx_a, v_ref[j], m, acc)
        if left == 4:
            mx_a = score_chunk(j + 3, s_a)
        m, acc = _mla_update(s_b, mx_b, v_ref[j + 1], m, acc)
        if left >= 3:
            m, acc = _mla_update(s_c, mx_c, v_ref[j + 2], m, acc)
        if left == 4:
            m, acc = _mla_update(s_a, mx_a, v_ref[j + 3], m, acc)
    o_ref[...] = (acc[:MLA_V_DIM] / acc[MLA_V_DIM:MLA_V_DIM + 1]).astype(o_ref.dtype)


def _mla_call(qm, km, vm, tq, sub, q_tile0, n_q, ctx_only):
    bsz, _, s = qm.shape
    nc = vm.shape[1]
    wq = tq * sub
    if ctx_only:
        k_spec = pl.BlockSpec((None, KV_SUB, MLA_HEAD_PAD), lambda b, h, i: (b, s // KV_SUB - 1, h))
        v_spec = pl.BlockSpec((None, 1, V_ROWS, KV_CHUNK), lambda b, h, i: (b, nc - 1, h, 0))
    else:
        k_spec = pl.BlockSpec((None, s, MLA_HEAD_PAD), lambda b, h, i: (b, 0, h))
        v_spec = pl.BlockSpec((None, nc, V_ROWS, KV_CHUNK), lambda b, h, i: (b, 0, h, 0))
    return pl.pallas_call(
        functools.partial(_mla_kernel, nc=nc, ctx_only=ctx_only, tq=tq),
        grid=(bsz, MLA_HEADS, n_q),
        in_specs=[pl.BlockSpec((None, MLA_HEAD_PAD, wq), lambda b, h, i: (b, h, q_tile0 + i)), k_spec, v_spec],
        out_specs=pl.BlockSpec((None, MLA_V_DIM, wq), lambda b, h, i: (b, h, i)),
        out_shape=jax.ShapeDtypeStruct((bsz, MLA_HEADS * MLA_V_DIM, n_q * wq), BF16),
        scratch_shapes=[pltpu.VMEM((KV_CHUNK, tq), F32)] * 3,
        compiler_params=_cparams(("parallel", "arbitrary", "arbitrary"), 40),
        name="mla_attn_ctx" if ctx_only else "mla_attn",
    )(qm, km, vm)


def _mla(qm, km, vm, lat, lc):
    tq = min(MLA_TQ, lat)
    sub = MLA_SUBTILES if lat % (tq * MLA_SUBTILES) == 0 else 1
    assert lc == KV_SUB and lat % lc == 0
    return (_mla_call(qm, km, vm, tq, sub, 0, lat // (tq * sub), False),
            _mla_call(qm, km, vm, lc, 1, lat // lc, 1, True))


def _merge_mlp_kernel(x_ref, mod_ref, hl_refs, hc_refs, sk_ref, yg_ref, yml_ref, ymc_ref, gt_ref, wb_ref, wo_ref,
                      g_ref, w1_ref, w2_ref, o_ref, *, lat_tiles):
    d = D_MODEL
    is_lat = pl.program_id(1) < lat_tiles
    conv, x0, z = [jnp.where(is_lat, a[...], b[...]).astype(F32) for a, b in zip(hl_refs, hc_refs)]
    yh = (x0 * (conv + z * sk_ref[...])).astype(BF16)
    yg = yg_ref[...].astype(F32).T.astype(BF16)
    ym = jnp.where(is_lat, yml_ref[...], ymc_ref[...]).astype(F32).T.astype(BF16)
    merged = (gt_ref[:, 0:d].astype(F32) * _dot(yh, wb_ref[0])
              + gt_ref[:, d:2 * d].astype(F32) * _dot(yg, wb_ref[1])
              + gt_ref[:, 2 * d:].astype(F32) * _dot(ym, wb_ref[2]))
    res = _dot(merged.astype(BF16), wo_ref[...])
    x_mix = x_ref[...] + mod_ref[:, 2 * d:3 * d] * res
    o_ref[...] = _mlp_rows(x_mix, mod_ref, g_ref, w1_ref, w2_ref)


def _merge_mlp(xs, modsel, hy_lat, hy_ctx, skip, yg, ym_lat, ym_ctx, gates, wb, wo, g, w1, w2, lat_tiles, n_tiles):
    bsz, _, d = xs.shape
    tm = ROW_TILE
    c = HY_WIDTH
    row = lambda b, i: (b, i, 0)
    col = lambda b, i: (b, 0, i)
    lat_row = pl.BlockSpec((None, tm, c), lambda b, i: (b, jnp.minimum(i, lat_tiles - 1), 0))
    ctx_row = pl.BlockSpec((None, tm, c), lambda b, i: (b, jnp.maximum(i - lat_tiles, 0), 0))
    return pl.pallas_call(
        functools.partial(_merge_mlp_kernel, lat_tiles=lat_tiles),
        grid=(bsz, n_tiles),
        in_specs=[
            pl.BlockSpec((None, tm, d), row),
            pl.BlockSpec((None, None, 1, 6 * d), lambda b, i: (b, jnp.where(i >= lat_tiles, 1, 0), 0, 0)),
            [lat_row] * 3, [ctx_row] * 3, _const_spec(skip.shape),
            pl.BlockSpec((None, c, tm), col),
            pl.BlockSpec((None, c, tm), lambda b, i: (b, 0, jnp.minimum(i, lat_tiles - 1))),
            pl.BlockSpec((None, c, tm), lambda b, i: (b, 0, jnp.maximum(i - lat_tiles, 0))),
            pl.BlockSpec((None, tm, 3 * d), row),
            _const_spec(wb.shape), _const_spec(wo.shape),
            _const_spec(g.shape), _const_spec(w1.shape), _const_spec(w2.shape),
        ],
        out_specs=pl.BlockSpec((None, tm, d), row),
        out_shape=jax.ShapeDtypeStruct((bsz, n_tiles * tm, d), F32),
        compiler_params=_cparams(("parallel", "arbitrary"), 56),
        name="merge_mlp",
    )(xs, modsel, list(hy_lat), list(hy_ctx), skip, yg, ym_lat, ym_ctx, gates, wb, wo, g, w1, w2)


def _mlp_rows(x, mod_ref, g_ref, w1_ref, w2_ref):
    d = D_MODEL
    xn = x * lax.rsqrt(jnp.mean(x * x, axis=-1, keepdims=True) + EPS) * g_ref[...]
    h = (xn * (1.0 + mod_ref[:, 4 * d:5 * d]) + mod_ref[:, 3 * d:4 * d]).astype(BF16)
    acc = jnp.zeros(x.shape, F32)
    for j in range(D_FF // FF_CHUNK):
        sl = slice(j * FF_CHUNK, (j + 1) * FF_CHUNK)
        a = jnp.maximum(_dot(h, w1_ref[:, sl]), 0.0)
        acc = acc + _dot((a * a).astype(BF16), w2_ref[sl, :])
    return x + mod_ref[:, 5 * d:] * acc


def _rope_tables_t(rows, dim, lc):
    n_freq = dim // 4
    inv = ROPE_BASE ** (-jnp.arange(n_freq, dtype=F32) / n_freq)
    r = jnp.repeat(jnp.arange(rows, dtype=F32), GRID_W)
    col = jnp.tile(jnp.arange(GRID_W, dtype=F32), rows)
    ang = jnp.concatenate([r[:, None] * inv, col[:, None] * inv], axis=-1)
    cos_t = jnp.concatenate([jnp.cos(ang).T, jnp.ones((dim // 2, lc), F32)], axis=1)
    sin_t = jnp.concatenate([jnp.sin(ang).T, jnp.zeros((dim // 2, lc), F32)], axis=1)
    return cos_t, sin_t


def _lane_bcast(v):
    return jnp.broadcast_to(v.astype(F32)[:, None], (v.shape[0], ROW_TILE))


def _layer_weights(l, w_in, gqa_q_norm, gqa_k_norm, mla_q_a_norm, mla_kv_a_norm, w_q_b, w_kv_b, mla_q_norm, mla_k_norm):
    w = w_in[l]
    o = np.cumsum([0, 3 * HY_WIDTH, GQA_HEADS * GQA_HEAD_DIM, GQA_KV_HEADS * GQA_HEAD_DIM,
                   GQA_KV_HEADS * GQA_HEAD_DIM, MLA_Q_RANK, MLA_KV_RANK, MLA_ROPE_DIM, 3 * D_MODEL])
    wu = w[:, o[0]:o[1]].astype(BF16)
    wt = w[:, o[1]:o[7]].T.astype(BF16)
    wg = w[:, o[7]:o[8]].astype(BF16)
    wqb = w_q_b[l].T.astype(BF16)
    wkvb = w_kv_b[l].T.astype(BF16)
    return (wu, wg, wt, wqb, wkvb,
            _lane_bcast(gqa_q_norm[l]), _lane_bcast(gqa_k_norm[l]),
            _lane_bcast(mla_q_a_norm[l]), _lane_bcast(mla_kv_a_norm[l]),
            _lane_bcast(mla_q_norm[l]), _lane_bcast(mla_k_norm[l]))


def kernel(x, c, ctx, c_ctx, w_mod, b_mod, norm_mix_g, norm_mlp_g, w_in, hy_short_w, hy_f1_w, hy_f1_b, hy_f2_w, hy_f2_b, hy_sin_freq, hy_f3_w, hy_skip, gqa_q_norm, gqa_k_norm, gqa_sink, mla_q_a_norm, mla_kv_a_norm, w_q_b, w_kv_b, mla_q_norm, mla_k_norm, w_branch, w_out, w_mlp1, w_mlp2):
    bsz, lat, d = x.shape
    lc = ctx.shape[1]
    depth = w_mod.shape[0]
    s = lat + lc
    assert d == D_MODEL and lat % GRID_W == 0 and lat % ROW_TILE == 0 and lc == ROW_TILE and s % KV_CHUNK == 0
    lat_tiles = lat // ROW_TILE

    pad = (-(bsz + 1)) % 8
    cond = jnp.concatenate([c, c_ctx[None, :], jnp.zeros((pad, d), F32)], axis=0)
    mods = _mod_all(cond, w_mod, b_mod)

    tabs = _rope_tables_t(lat // GRID_W, GQA_HEAD_DIM, lc) + _rope_tables_t(lat // GRID_W, MLA_ROPE_DIM, lc)
    xs = jnp.concatenate([x, ctx], axis=1)

    for l in range(depth):
        ml = mods[l]
        modsel = jnp.stack([ml[:bsz], jnp.broadcast_to(ml[bsz][None], (bsz, 6 * d))], axis=1)[:, :, None, :]
        wts = _layer_weights(l, w_in, gqa_q_norm, gqa_k_norm, mla_q_a_norm, mla_kv_a_norm, w_q_b, w_kv_b,
                             mla_q_norm, mla_k_norm)
        u, gates, qg, kg, vg, qm, km, vm = _proj(xs, modsel, norm_mix_g[l][None, :], wts, tabs, lat_tiles)

        f1w = jnp.zeros((V7X_LANES, HY_FILTER_WIDTH), F32).at[:HY_EMB_DIM].set(hy_f1_w[l])
        fw = (f1w, hy_f1_b[l][None, :], hy_f2_w[l], hy_f2_b[l][None, :], hy_sin_freq[l], hy_f3_w[l])
        hy_lat, hy_ctx = _hyena(u, hy_short_w[l], fw, lat, lc)

        sink_rows = jnp.broadcast_to(gqa_sink[l].astype(F32)[:, None], (GQA_HEADS, GQA_TQ))
        yg = _gqa(qg, kg, vg, sink_rows, lat, lc)
        ym_lat, ym_ctx = _mla(qm, km, vm, lat, lc)

        n_tiles = lat_tiles if l == depth - 1 else s // ROW_TILE
        xs = _merge_mlp(xs, modsel, hy_lat, hy_ctx, hy_skip[l][None, :], yg, ym_lat, ym_ctx, gates,
                        w_branch[l].astype(BF16), w_out[l].astype(BF16), norm_mlp_g[l][None, :],
                        w_mlp1[l].astype(BF16), w_mlp2[l].astype(BF16), lat_tiles, n_tiles)
    return xs
```

```python
import functools
import math

import numpy as np
import jax
import jax.numpy as jnp
from jax import lax
from jax.experimental import pallas as pl
from jax.experimental.pallas import tpu as pltpu

D_MODEL = 1024
GRID_W = 64
HY_WIDTH = 512
HY_EMB_DIM = 33
HY_BANDS = (HY_EMB_DIM - 1) // 2
HY_FILTER_WIDTH = 64
HY_DECAY_TARGET = 1e-2
HY_FAST_DECAY = 0.3
HY_SLOW_DECAY = 1.5
GQA_HEADS = 8
GQA_KV_HEADS = 2
GQA_GROUP = GQA_HEADS // GQA_KV_HEADS
GQA_HEAD_DIM = 64
GQA_SCALE = GQA_HEAD_DIM ** -0.5
WINDOW = 128
MLA_HEADS = 8
MLA_Q_RANK = 384
MLA_KV_RANK = 256
MLA_NOPE_DIM = 64
MLA_ROPE_DIM = 32
MLA_V_DIM = 64
MLA_QK_DIM = MLA_NOPE_DIM + MLA_ROPE_DIM
MLA_SCALE = MLA_QK_DIM ** -0.5
D_FF = 4 * D_MODEL
ROPE_BASE = 10000.0
EPS = 1e-6
LOG2E = 1.4426950408889634
NEG_BIG = -1e30

V7X_LANES = 128
V7X_VMEM_BYTES = 64 * 1024 * 1024

ROW_TILE = 256
PROJ_BATCH = 2
CONV_TILE = 512
MLA_HEAD_PAD = 128
V_ROWS = 80
KV_CHUNK = 1 * ROW_TILE
KV_SUB = 256
MLA_UNROLL = 5
MLA_TQ = 1024
MLA_SUBTILES = 2
GQA_TQ = 256
FF_CHUNK = 1024
FFT_N2 = 128
FFT_NB = 16
FFT_KB = 8

F32 = jnp.float32
BF16 = jnp.bfloat16


def _cparams(sem, vmem_mb, flags=None):
    return pltpu.CompilerParams(dimension_semantics=sem, vmem_limit_bytes=vmem_mb * 1024 * 1024, flags=flags)


def _dot(a, b):
    return jnp.dot(a, b, preferred_element_type=F32)


def _dot_hi(a, b):
    return jnp.dot(a, b, preferred_element_type=F32, precision=lax.Precision.HIGHEST)


def _ones_row_block(width):
    r = lax.broadcasted_iota(jnp.int32, (V_ROWS - MLA_V_DIM, width), 0)
    return jnp.where(r == 0, 1.0, 0.0).astype(BF16)


def _const_spec(shape):
    nd = len(shape)
    return pl.BlockSpec(shape, lambda *_: (0,) * nd, pipeline_mode=pl.Buffered(1))


def _mod_kernel(c_ref, w_ref, b_ref, o_ref):
    c = c_ref[...]
    s = c * jax.nn.sigmoid(c)
    o_ref[...] = _dot_hi(s, w_ref[...]) + b_ref[...]


def _mod_all(cond, w_mod, b_mod):
    depth, d, n = w_mod.shape
    rows = cond.shape[0]
    tn = 1536
    return pl.pallas_call(
        _mod_kernel,
        grid=(depth, n // tn),
        in_specs=[
            pl.BlockSpec((rows, d), lambda l, j: (0, 0)),
            pl.BlockSpec((None, d, tn), lambda l, j: (l, 0, j)),
            pl.BlockSpec((None, 1, tn), lambda l, j: (l, 0, j)),
        ],
        out_specs=pl.BlockSpec((None, rows, tn), lambda l, j: (l, 0, j)),
        out_shape=jax.ShapeDtypeStruct((depth, rows, n), F32),
        compiler_params=_cparams(("arbitrary", "arbitrary"), 40),
        name="adaln_mod",
    )(cond, w_mod, b_mod.reshape(depth, 1, n))


def _rms_rows(x, n):
    return lax.rsqrt(jnp.sum(x * x, axis=0, keepdims=True) * (1.0 / n) + EPS)


def _rope_rows(x1, x2, cs, sn):
    return x1 * cs - x2 * sn, x1 * sn + x2 * cs


def _proj_kernel(x_ref, mod_ref, *refs):
    consts, outs = refs[:16], refs[16:]
    for n in range(x_ref.shape[0]):
        _proj_one(x_ref.at[n], mod_ref.at[n], *consts, *[o.at[n] for o in outs])


def _proj_one(x_ref, mod_ref, g_ref, wu_ref, wg_ref, wt_ref, wqb_ref, wkvb_ref,
              gq_ref, gk_ref, gqa_ref, gkva_ref, gmq_ref, gmk_ref,
              cg_ref, sg_ref, cm_ref, sm_ref,
              u_ref, gate_ref, qg_ref, kg_ref, vg_ref, qm_ref, km_ref, vm_ref):
    d = D_MODEL
    x = x_ref[...]
    tm = x.shape[0]
    shift = mod_ref[:, 0:d]
    scale = mod_ref[:, d:2 * d]
    xn = x * lax.rsqrt(jnp.mean(x * x, axis=-1, keepdims=True) + EPS) * g_ref[...]
    h = xn * (1.0 + scale) + shift
    hb = h.astype(BF16)

    ht = h.T.astype(BF16)
    t = _dot(wt_ref[...], ht)
    o_q, o_k, o_v = 0, 512, 640
    o_cq, o_ckv, o_kr = 768, 768 + MLA_Q_RANK, 768 + MLA_Q_RANK + MLA_KV_RANK

    u_ref[...] = _dot(hb, wu_ref[...]).astype(BF16)
    gate_ref[...] = jax.nn.sigmoid(_dot(hb, wg_ref[...])).astype(BF16)

    cg, sg = cg_ref[...], sg_ref[...]
    cm, sm = cm_ref[...], sm_ref[...]
    hd, hh = GQA_HEAD_DIM, GQA_HEAD_DIM // 2

    cq = t[o_cq:o_cq + MLA_Q_RANK]
    cqn = (cq * _rms_rows(cq, MLA_Q_RANK) * gqa_ref[...]).astype(BF16)
    qm = _dot(wqb_ref[...], cqn)
    gmq = gmq_ref[...]
    nd, rh = MLA_NOPE_DIM, MLA_ROPE_DIM // 2
    qpad = jnp.zeros((MLA_HEAD_PAD - MLA_QK_DIM, tm), BF16)
    for n in range(MLA_HEADS):
        xh = qm[n * MLA_QK_DIM:(n + 1) * MLA_QK_DIM]
        xh = xh * _rms_rows(xh, MLA_QK_DIM) * gmq * (MLA_SCALE * LOG2E)
        a, b = _rope_rows(xh[nd:nd + rh], xh[nd + rh:], cm, sm)
        base = n * MLA_HEAD_PAD
        qm_ref[base:base + nd, :] = xh[:nd].astype(BF16)
        qm_ref[base + nd:base + nd + rh, :] = a.astype(BF16)
        qm_ref[base + nd + rh:base + MLA_QK_DIM, :] = b.astype(BF16)
        qm_ref[base + MLA_QK_DIM:base + MLA_HEAD_PAD, :] = qpad

    ckv = t[o_ckv:o_ckv + MLA_KV_RANK]
    ckvn = (ckv * _rms_rows(ckv, MLA_KV_RANK) * gkva_ref[...]).astype(BF16)
    kv = _dot(wkvb_ref[...], ckvn)
    kr = t[o_kr:o_kr + MLA_ROPE_DIM]
    kr_ss = jnp.sum(kr * kr, axis=0, keepdims=True)
    gmk = gmk_ref[...]
    kzero = jnp.zeros((MLA_HEAD_PAD - MLA_QK_DIM, tm), F32)
    ones_rows = _ones_row_block(tm)
    parts = []
    for n in range(MLA_HEADS):
        kn = kv[n * 128:n * 128 + nd]
        vm_ref[n * V_ROWS:n * V_ROWS + MLA_V_DIM, :] = kv[n * 128 + nd:(n + 1) * 128].astype(BF16)
        vm_ref[n * V_ROWS + MLA_V_DIM:(n + 1) * V_ROWS, :] = ones_rows
        rs = lax.rsqrt((jnp.sum(kn * kn, axis=0, keepdims=True) + kr_ss) * (1.0 / MLA_QK_DIM) + EPS)
        krn = kr * rs * gmk[nd:]
        a, b = _rope_rows(krn[:rh], krn[rh:], cm, sm)
        parts += [kn * rs * gmk[:nd], a, b, kzero]
    km_ref[...] = jnp.concatenate(parts, axis=0).T.astype(BF16)

    gq = gq_ref[...]
    for n in range(GQA_HEADS):
        xh = t[o_q + n * hd:o_q + (n + 1) * hd]
        xh = xh * _rms_rows(xh, hd) * gq
        a, b = _rope_rows(xh[:hh], xh[hh:], cg, sg)
        qg_ref[n * hd:n * hd + hh, :] = (a * (GQA_SCALE * LOG2E)).astype(BF16)
        qg_ref[n * hd + hh:(n + 1) * hd, :] = (b * (GQA_SCALE * LOG2E)).astype(BF16)

    gk = gk_ref[...]
    zpad = jnp.zeros((V7X_LANES - hd, tm), F32)
    parts = []
    for n in range(GQA_KV_HEADS):
        xh = t[o_k + n * hd:o_k + (n + 1) * hd]
        xh = xh * _rms_rows(xh, hd) * gk
        a, b = _rope_rows(xh[:hh], xh[hh:], cg, sg)
        parts += [a, b, zpad]
    kg_ref[...] = jnp.concatenate(parts, axis=0).T.astype(BF16)
    for n in range(GQA_KV_HEADS):
        vg_ref[n * V_ROWS:n * V_ROWS + hd, :] = t[o_v + n * hd:o_v + (n + 1) * hd].astype(BF16)
        vg_ref[n * V_ROWS + hd:(n + 1) * V_ROWS, :] = ones_rows


def _proj(xs, modsel, g, wts, tabs, lat_tiles):
    bsz, s, d = xs.shape
    tm = ROW_TILE
    nt = s // tm
    ncs = KV_CHUNK // tm
    nb = PROJ_BATCH if bsz % PROJ_BATCH == 0 else 1
    (wu, wg, wt, wqb, wkvb, gq, gk, gqa, gkva, gmq, gmk) = wts
    cg, sg, cm, sm = tabs
    row = lambda b, i: (b, i, 0)
    col = lambda b, i: (b, 0, i)
    tab = lambda b, i: (0, i)
    in_specs = [
        pl.BlockSpec((nb, tm, d), row),
        pl.BlockSpec((nb, None, 1, 6 * d), lambda b, i: (b, jnp.where(i >= lat_tiles, 1, 0), 0, 0)),
        _const_spec(g.shape), _const_spec(wu.shape), _const_spec(wg.shape), _const_spec(wt.shape),
        _const_spec(wqb.shape), _const_spec(wkvb.shape),
        _const_spec(gq.shape), _const_spec(gk.shape), _const_spec(gqa.shape), _const_spec(gkva.shape),
        _const_spec(gmq.shape), _const_spec(gmk.shape),
        pl.BlockSpec((cg.shape[0], tm), tab), pl.BlockSpec((sg.shape[0], tm), tab),
        pl.BlockSpec((cm.shape[0], tm), tab), pl.BlockSpec((sm.shape[0], tm), tab),
    ]
    out_shape = [
        jax.ShapeDtypeStruct((bsz, s, 3 * HY_WIDTH), BF16),
        jax.ShapeDtypeStruct((bsz, s, 3 * d), BF16),
        jax.ShapeDtypeStruct((bsz, GQA_HEADS * GQA_HEAD_DIM, s), BF16),
        jax.ShapeDtypeStruct((bsz, s, GQA_KV_HEADS * V7X_LANES), BF16),
        jax.ShapeDtypeStruct((bsz, GQA_KV_HEADS * V_ROWS, s), BF16),
        jax.ShapeDtypeStruct((bsz, MLA_HEADS * MLA_HEAD_PAD, s), BF16),
        jax.ShapeDtypeStruct((bsz, s, MLA_HEADS * MLA_HEAD_PAD), BF16),
        jax.ShapeDtypeStruct((bsz, s // KV_CHUNK, MLA_HEADS * V_ROWS, KV_CHUNK), BF16),
    ]
    out_specs = [
        pl.BlockSpec((nb, tm, 3 * HY_WIDTH), row),
        pl.BlockSpec((nb, tm, 3 * d), row),
        pl.BlockSpec((nb, GQA_HEADS * GQA_HEAD_DIM, tm), col),
        pl.BlockSpec((nb, tm, GQA_KV_HEADS * V7X_LANES), row),
        pl.BlockSpec((nb, GQA_KV_HEADS * V_ROWS, tm), col),
        pl.BlockSpec((nb, MLA_HEADS * MLA_HEAD_PAD, tm), col),
        pl.BlockSpec((nb, tm, MLA_HEADS * MLA_HEAD_PAD), row),
        pl.BlockSpec((nb, None, MLA_HEADS * V_ROWS, tm), lambda b, i: (b, i // ncs, 0, i % ncs)),
    ]
    return pl.pallas_call(
        _proj_kernel,
        grid=(bsz // nb, nt),
        in_specs=in_specs,
        out_specs=out_specs,
        out_shape=out_shape,
        compiler_params=_cparams(("parallel", "arbitrary"), 56),
        name="in_proj",
    )(xs, modsel, g, wu, wg, wt, wqb, wkvb, gq, gk, gqa, gkva, gmq, gmk, cg, sg, cm, sm)


def _short_conv_kernel(u_ref, up_ref, un_ref, w_ref, x0_ref, z_ref, *, n_tiles):
    i = pl.program_id(1)
    u = u_ref[...].astype(F32)
    tm = u.shape[0]
    prev = jnp.where(i > 0, up_ref[7:8, :].astype(F32), 0.0)
    nxt = jnp.where(i < n_tiles - 1, un_ref[0:1, :].astype(F32), 0.0)
    ridx = lax.broadcasted_iota(jnp.int32, u.shape, 0)
    up = jnp.where(ridx == 0, prev, pltpu.roll(u, 1, axis=0))
    dn = jnp.where(ridx == tm - 1, nxt, pltpu.roll(u, tm - 1, axis=0))
    uc = up * w_ref[0:1, :] + u * w_ref[1:2, :] + dn * w_ref[2:3, :]
    c = HY_WIDTH
    x0_ref[...] = uc[:, :c].astype(BF16)
    z_ref[...] = (uc[:, c:2 * c] * uc[:, 2 * c:]).astype(BF16)


def _short_conv(u, short_w, row0, rows):
    bsz, s, c3 = u.shape
    tm = math.gcd(math.gcd(rows, row0), CONV_TILE) if row0 else math.gcd(rows, CONV_TILE)
    nt = rows // tm
    t0 = row0 // tm
    r8 = tm // 8
    last8 = s // 8 - 1
    return pl.pallas_call(
        functools.partial(_short_conv_kernel, n_tiles=nt),
        grid=(bsz, nt),
        in_specs=[
            pl.BlockSpec((None, tm, c3), lambda b, i: (b, t0 + i, 0)),
            pl.BlockSpec((None, 8, c3), lambda b, i: (b, jnp.maximum((t0 + i) * r8 - 1, 0), 0)),
            pl.BlockSpec((None, 8, c3), lambda b, i: (b, jnp.minimum((t0 + i + 1) * r8, last8), 0)),
            _const_spec(short_w.shape),
        ],
        out_specs=[pl.BlockSpec((None, tm, HY_WIDTH), lambda b, i: (b, i, 0))] * 2,
        out_shape=[jax.ShapeDtypeStruct((bsz, rows, HY_WIDTH), BF16)] * 2,
        compiler_params=_cparams(("parallel", "arbitrary"), 32),
        name="hyena_short_conv",
    )(u, u, u, short_w)


def _filter_kernel(zf_ref, w1_ref, b1_ref, w2_ref, b2_ref, fr_ref, w3_ref, dl_ref, h_ref, ss_ref):
    i = pl.program_id(0)
    zf = zf_ref[...]
    tl = zf.shape[0]
    h = jnp.sin(fr_ref[0:1, :] * (_dot_hi(zf, w1_ref[...]) + b1_ref[...]))
    h = jnp.sin(fr_ref[1:2, :] * (_dot_hi(h, w2_ref[...]) + b2_ref[...]))
    h = _dot_hi(h, w3_ref[...])
    decay = jnp.exp(-zf[:, 0:1] * dl_ref[...])
    c = HY_WIDTH
    hf = h[:, :c] * decay
    ridx = lax.broadcasted_iota(jnp.int32, (tl, c), 0) + i * tl
    hb = jnp.where(ridx == 0, 0.0, h[:, c:] * decay)
    h_ref[0] = hf.astype(BF16)
    h_ref[1] = hb.astype(BF16)
    ss = jnp.sum(hf * hf + hb * hb, axis=0, keepdims=True)

    @pl.when(i == 0)
    def _():
        ss_ref[...] = ss

    @pl.when(i > 0)
    def _():
        ss_ref[...] += ss


def _filter(zfeat, fw, deltas):
    length = zfeat.shape[0]
    tl = min(length, 1024)
    w1, b1, w2, b2, fr, w3 = fw
    return pl.pallas_call(
        _filter_kernel,
        grid=(length // tl,),
        in_specs=[pl.BlockSpec((tl, zfeat.shape[1]), lambda i: (i, 0))]
        + [_const_spec(a.shape) for a in (w1, b1, w2, b2, fr, w3, deltas)],
        out_specs=[pl.BlockSpec((2, tl, HY_WIDTH), lambda i: (0, i, 0)),
                   pl.BlockSpec((1, HY_WIDTH), lambda i: (0, 0))],
        out_shape=[jax.ShapeDtypeStruct((2, length, HY_WIDTH), BF16),
                   jax.ShapeDtypeStruct((1, HY_WIDTH), F32)],
        compiler_params=_cparams(("arbitrary",), 40),
        name="hyena_filter",
    )(zfeat, w1, b1, w2, b2, fr, w3, deltas)


def _fft_a_kernel(g_ref, x_ref, o_ref, *, nb, cw):
    for j in range(nb):
        sl = slice(j * cw, (j + 1) * cw)
        o_ref[:, sl] = _dot(g_ref[j], x_ref[:, sl]).astype(o_ref.dtype)


def _fft_a(x2d, gmat):
    bx, k1, w = x2d.shape
    n2, two_n1, _ = gmat.shape
    cw = w // n2
    nb = FFT_NB
    return pl.pallas_call(
        functools.partial(_fft_a_kernel, nb=nb, cw=cw),
        grid=(bx, n2 // nb),
        in_specs=[pl.BlockSpec((nb, two_n1, k1), lambda b, j: (j, 0, 0)),
                  pl.BlockSpec((None, k1, nb * cw), lambda b, j: (b, 0, j))],
        out_specs=pl.BlockSpec((None, two_n1, nb * cw), lambda b, j: (b, 0, j)),
        out_shape=jax.ShapeDtypeStruct((bx, two_n1, w), BF16),
        compiler_params=_cparams(("parallel", "arbitrary"), 40),
        name="fft_stage_a",
    )(gmat, x2d)


def _fft_filter_b_kernel(fb_ref, a_ref, ss_ref, kf_ref, *, inv_n):
    n = a_ref.shape[3]
    rs = lax.rsqrt(ss_ref[...] + EPS) * inv_n
    for r in range(a_ref.shape[2]):
        xf = _dot(fb_ref[...], jnp.concatenate([a_ref[0, 0, r], a_ref[0, 1, r]], axis=0))
        xb = _dot(fb_ref[...], jnp.concatenate([a_ref[1, 0, r], a_ref[1, 1, r]], axis=0))
        kf_ref[0, r] = (xf[:n] + xb[:n]) * rs
        kf_ref[1, r] = (xf[n:] - xb[n:]) * rs


def _fft_filter_b(a5, fb, ssq, inv_n):
    _, _, n1, n2, c = a5.shape
    kb = min(FFT_KB, n1)
    return pl.pallas_call(
        functools.partial(_fft_filter_b_kernel, inv_n=inv_n),
        grid=(n1 // kb,),
        in_specs=[_const_spec(fb.shape),
                  pl.BlockSpec((2, 2, kb, n2, c), lambda k: (0, 0, k, 0, 0)),
                  _const_spec(ssq.shape)],
        out_specs=pl.BlockSpec((2, kb, n2, c), lambda k: (0, k, 0, 0)),
        out_shape=jax.ShapeDtypeStruct((2, n1, n2, c), F32),
        compiler_params=_cparams(("arbitrary",), 32),
        name="fft_filter_stage_b",
    )(fb, a5, ssq)


def _fft_b_kernel(fb_ref, fbi_ref, a_ref, kf_ref, o_ref):
    kb, n, cw = a_ref.shape[1:]
    a_all = jnp.concatenate([jnp.concatenate([a_ref[0, r], a_ref[1, r]], axis=0) for r in range(kb)], axis=1)
    x = _dot(fb_ref[...], a_all)
    ys = []
    for r in range(kb):
        xr, xi = x[:n, r * cw:(r + 1) * cw], x[n:, r * cw:(r + 1) * cw]
        kr, ki = kf_ref[0, r], kf_ref[1, r]
        ys.append(jnp.concatenate([xr * kr - xi * ki, xr * ki + xi * kr], axis=0).astype(BF16))
    c = _dot(fbi_ref[...], jnp.concatenate(ys, axis=1))
    for r in range(kb):
        o_ref[0, r] = c[:n, r * cw:(r + 1) * cw].astype(o_ref.dtype)
        o_ref[1, r] = c[n:, r * cw:(r + 1) * cw].astype(o_ref.dtype)


def _fft_b(a5, kf, fb, fbi):
    bsz, _, n1, n2, c = a5.shape
    kb = min(FFT_KB, n1)
    return pl.pallas_call(
        _fft_b_kernel,
        grid=(n1 // kb, bsz),
        in_specs=[_const_spec(fb.shape), _const_spec(fbi.shape),
                  pl.BlockSpec((None, 2, kb, n2, c), lambda k, b: (b, 0, k, 0, 0)),
                  pl.BlockSpec((2, kb, n2, c), lambda k, b: (0, k, 0, 0))],
        out_specs=pl.BlockSpec((None, 2, kb, n2, c), lambda k, b: (b, 0, k, 0, 0)),
        out_shape=jax.ShapeDtypeStruct(a5.shape, BF16),
        compiler_params=_cparams(("arbitrary", "arbitrary"), 32),
        name="fft_stage_b",
    )(fb, fbi, a5, kf)


def _fft_c_kernel(h_ref, c_ref, o_ref, *, nb, cw):
    for j in range(nb):
        sl = slice(j * cw, (j + 1) * cw)
        o_ref[:, sl] = _dot(h_ref[j], c_ref[:, sl]).astype(o_ref.dtype)


def _fft_c(c2d, hmat):
    bsz, two_n1, w = c2d.shape
    n2, k1, _ = hmat.shape
    cw = w // n2
    nb = FFT_NB
    blk = lambda b, j: (b, 0, j)
    return pl.pallas_call(
        functools.partial(_fft_c_kernel, nb=nb, cw=cw),
        grid=(bsz, n2 // nb),
        in_specs=[pl.BlockSpec((nb, k1, two_n1), lambda b, j: (j, 0, 0)),
                  pl.BlockSpec((None, two_n1, nb * cw), blk)],
        out_specs=pl.BlockSpec((None, k1, nb * cw), blk),
        out_shape=jax.ShapeDtypeStruct((bsz, k1, w), BF16),
        compiler_params=_cparams(("parallel", "arbitrary"), 40),
        name="fft_stage_c",
    )(hmat, c2d)


def _ctx_conv_kernel(fc_ref, fci_ref, h_ref, ss_ref, z_ref, o_ref):
    n = fc_ref.shape[0] // 2
    fc = fc_ref[...]
    kf = _dot(fc, h_ref[0])
    kb = _dot(fc, h_ref[1])
    rs = lax.rsqrt(ss_ref[...] + EPS)
    kr = (kf[:n] + kb[:n]) * rs
    ki = (kf[n:] - kb[n:]) * rs
    x = _dot(fc, z_ref[...])
    xr, xi = x[:n], x[n:]
    y = jnp.concatenate([xr * kr - xi * ki, xr * ki + xi * kr], axis=0).astype(BF16)
    o_ref[...] = _dot(fci_ref[...], y).astype(o_ref.dtype)


def _ctx_conv(fc, fci, hfb, ssq, z):
    bsz, lc, c = z.shape
    blk = pl.BlockSpec((None, lc, c), lambda b: (b, 0, 0))
    return pl.pallas_call(
        _ctx_conv_kernel,
        grid=(bsz,),
        in_specs=[_const_spec(fc.shape), _const_spec(fci.shape), _const_spec(hfb.shape),
                  _const_spec(ssq.shape), blk],
        out_specs=blk,
        out_shape=jax.ShapeDtypeStruct((bsz, lc, c), BF16),
        compiler_params=_cparams(("arbitrary",), 32),
        name="hyena_ctx_conv",
    )(fc, fci, hfb, ssq, z)


@functools.lru_cache(maxsize=None)
def _fft_tables(length):
    n = 2 * length
    n2 = FFT_N2
    n1 = n // n2
    k1 = length // n2
    kk = np.arange(n1)[:, None]
    g = np.empty((n2, 2 * n1, k1), np.float64)
    h = np.empty((n2, k1, 2 * n1), np.float64)
    nn = np.arange(k1)[None, :]
    for j in range(n2):
        ang = 2.0 * np.pi * (((n2 * nn * kk) % n) + (j * kk) % n) / n
        g[j, :n1] = np.cos(ang)
        g[j, n1:] = -np.sin(ang)
        h[j, :, :n1] = np.cos(ang).T
        h[j, :, n1:] = -np.sin(ang).T
    a = np.arange(n2)
    ph = 2.0 * np.pi * ((a[:, None] * a[None, :]) % n2) / n2
    c, s = np.cos(ph), np.sin(ph)
    fb = np.block([[c, s], [-s, c]])
    fbi = np.block([[c, -s], [s, c]])
    return (jnp.asarray(g, BF16), jnp.asarray(h, BF16), jnp.asarray(fb, BF16), jnp.asarray(fbi, BF16), n1, k1)


@functools.lru_cache(maxsize=None)
def _dft_tables(length):
    n = 2 * length
    k = np.arange(n)[:, None]
    t = np.arange(length)[None, :]
    ang = 2.0 * np.pi * ((k * t) % n) / n
    fc = np.concatenate([np.cos(ang), -np.sin(ang)], axis=0)
    fci = np.concatenate([np.cos(ang).T, -np.sin(ang).T], axis=1) / n
    return jnp.asarray(fc, BF16), jnp.asarray(fci, BF16)


@functools.lru_cache(maxsize=None)
def _filter_features(length):
    t = np.linspace(0.0, 1.0, length, dtype=np.float32)[:, None]
    w = (2.0 * math.pi * np.arange(length, dtype=np.float32)[:, None] / length).astype(np.float32)
    f = np.linspace(1e-4, HY_BANDS - 1, HY_BANDS, dtype=np.float32)[None, :]
    z = np.concatenate([t, np.cos(f * w), -np.sin(f * w)], axis=-1).astype(np.float32)
    zp = np.zeros((length, V7X_LANES), np.float32)
    zp[:, :HY_EMB_DIM] = z
    return jnp.asarray(zp)


def _hyena_deltas():
    max_decay = math.log(HY_DECAY_TARGET) / HY_FAST_DECAY
    min_decay = math.log(HY_DECAY_TARGET) / HY_SLOW_DECAY
    return jnp.abs(jnp.linspace(min_decay, max_decay, HY_WIDTH, dtype=F32))[None, :]


def _hyena(u, short_w, fw, lat, lc):
    bsz = u.shape[0]
    c = HY_WIDTH
    deltas = _hyena_deltas()
    x0, z = _short_conv(u, short_w, 0, lat)
    gmat, hmat, fb, fbi, n1, k1 = _fft_tables(lat)
    n2 = FFT_N2
    hfb, ssq = _filter(_filter_features(lat), fw, deltas)
    fa = _fft_a(hfb.reshape(2, k1, n2 * c), gmat)
    kf = _fft_filter_b(fa.reshape(2, 2, n1, n2, c), fb, ssq, 1.0 / (2 * lat))
    za = _fft_a(z.reshape(bsz, k1, n2 * c), gmat)
    zc = _fft_b(za.reshape(bsz, 2, n1, n2, c), kf, fb, fbi)
    conv = _fft_c(zc.reshape(bsz, 2 * n1, n2 * c), hmat).reshape(bsz, lat, c)
    x0_c, z_c = _short_conv(u, short_w, lat, lc)
    hfb_c, ssq_c = _filter(_filter_features(lc), fw, deltas)
    fc, fci = _dft_tables(lc)
    conv_c = _ctx_conv(fc, fci, hfb_c, ssq_c, z_c)
    return (conv, x0, z), (conv_c, x0_c, z_c)


def _gqa_kernel(q_ref, k_ref, v_ref, sink_ref, o_ref, s_a, s_b, s_c, s_d, *, lat, lc):
    i = pl.program_id(1)
    tq = q_ref.shape[1]
    s_len = k_ref.shape[0]
    hd = GQA_HEAD_DIM
    w = WINDOW
    nwin = tq + 2 * w
    start = pl.multiple_of(jnp.clip(i * tq - w, 0, s_len - nwin), w)
    k_all = jnp.concatenate([k_ref[pl.ds(start, nwin), :], k_ref[lat:lat + lc, :]], axis=0)
    v_all = jnp.concatenate([v_ref[:, pl.ds(start, nwin)], v_ref[:, lat:lat + lc]], axis=1)
    q_pos = i * tq + lax.broadcasted_iota(jnp.int32, (1, tq), 1)
    k_pos = start + lax.broadcasted_iota(jnp.int32, (nwin, 1), 0)
    bias = (jnp.where(jnp.abs(k_pos - q_pos) <= w, 0.0, NEG_BIG)
            + jnp.where(k_pos < lat, 0.0, NEG_BIG)
            + jnp.where(q_pos < lat, 0.0, NEG_BIG))
    bias = jnp.concatenate([bias, jnp.zeros((lc, tq), F32)], axis=0)
    bias2 = jnp.concatenate([bias, bias], axis=1)
    zq = jnp.zeros((V7X_LANES - hd, 2 * tq), BF16)

    def score(pair, dst):
        g = pair // (GQA_GROUP // 2)
        r = 2 * pair * hd
        q2 = jnp.concatenate([q_ref[r:r + hd, :], q_ref[r + hd:r + 2 * hd, :]], axis=1)
        s = _dot(k_all[:, g * V7X_LANES:(g + 1) * V7X_LANES], jnp.concatenate([q2, zq], axis=0)) + bias2
        dst[...] = s
        return jnp.max(s, axis=0, keepdims=True)

    def update(pair, s_sc, s_max):
        g = pair // (GQA_GROUP // 2)
        r = 2 * pair * hd
        sink = jnp.concatenate([sink_ref[2 * pair:2 * pair + 1, :], sink_ref[2 * pair + 1:2 * pair + 2, :]],
                               axis=1) * LOG2E
        m = jnp.maximum(s_max, sink)
        p = jnp.exp2(s_sc[...] - m).astype(BF16)
        pv = _dot(v_all[g * V_ROWS:(g + 1) * V_ROWS, :], p)
        o = pv[:hd] / (pv[hd:hd + 1] + jnp.exp2(sink - m))
        o_ref[r:r + hd, :] = o[:, :tq].astype(o_ref.dtype)
        o_ref[r + hd:r + 2 * hd, :] = o[:, tq:].astype(o_ref.dtype)

    bufs = (s_a, s_b, s_c, s_d)
    npair = GQA_HEADS // 2
    mx = [score(j, bufs[j]) for j in range(npair)]
    for j in range(npair):
        update(j, bufs[j], mx[j])


def _gqa(qg, kg, vg, sink_rows, lat, lc):
    bsz, nq, s = qg.shape
    tq = GQA_TQ
    return pl.pallas_call(
        functools.partial(_gqa_kernel, lat=lat, lc=lc),
        grid=(bsz, s // tq),
        in_specs=[
            pl.BlockSpec((None, nq, tq), lambda b, i: (b, 0, i)),
            pl.BlockSpec((None, s, kg.shape[2]), lambda b, i: (b, 0, 0)),
            pl.BlockSpec((None, vg.shape[1], s), lambda b, i: (b, 0, 0)),
            _const_spec(sink_rows.shape),
        ],
        out_specs=pl.BlockSpec((None, nq, tq), lambda b, i: (b, 0, i)),
        out_shape=jax.ShapeDtypeStruct((bsz, nq, s), BF16),
        scratch_shapes=[pltpu.VMEM((tq + 2 * WINDOW + lc, 2 * tq), F32)] * 4,
        compiler_params=_cparams(("parallel", "arbitrary"), 40),
        name="gqa_window_attn",
    )(qg, kg, vg, sink_rows)


def _mla_update(s_ref, s_max, vt, m, acc):
    m_new = jnp.maximum(m, s_max)
    alpha = jnp.exp2(m - m_new)
    p = jnp.exp2(s_ref[...] - m_new).astype(BF16)
    acc = alpha * acc + _dot(vt, p)
    return m_new, acc


def _mla_kernel(q_ref, k_ref, v_ref, o_ref, s_a, s_b, s_c, *, nc, ctx_only, tq):
    for t in range(q_ref.shape[1] // tq):
        lanes = slice(t * tq, (t + 1) * tq)
        _mla_tile(q_ref[:, lanes], k_ref, v_ref, o_ref.at[:, lanes], s_a, s_b, s_c, nc, ctx_only)


def _mla_tile(q, k_ref, v_ref, o_ref, s_a, s_b, s_c, nc, ctx_only):
    tq = q.shape[1]
    ck = KV_CHUNK
    m = jnp.full((1, tq), NEG_BIG, F32)
    acc = jnp.zeros((V_ROWS, tq), F32)

    def score(k, dst):
        s = _dot(k, q)
        dst[...] = s
        return jnp.max(s, axis=0, keepdims=True)

    def score_chunk(j, dst):
        return score(k_ref[pl.ds(pl.multiple_of(j * ck, ck), ck), :], dst)

    if ctx_only:
        s_x = s_a.at[0:KV_SUB, :]
        mx = score(k_ref[k_ref.shape[0] - KV_SUB:, :], s_x)
        m, acc = _mla_update(s_x, mx, v_ref[v_ref.shape[0] - 1][:, ck - KV_SUB:], m, acc)
    elif nc == 1:
        m, acc = _mla_update(s_a, score_chunk(0, s_a), v_ref[0], m, acc)
    else:
        mx_a = score_chunk(0, s_a)
        mx_b = score_chunk(1, s_b)

        def triple(j, c):
            m, acc, mx_a, mx_b = c
            mx_c = score_chunk(j + 2, s_c)
            m, acc = _mla_update(s_a, mx_a, v_ref[j], m, acc)
            mx_a = score_chunk(j + 3, s_a)
            m, acc = _mla_update(s_b, mx_b, v_ref[j + 1], m, acc)
            mx_b = score_chunk(j + 4, s_b)
            m, acc = _mla_update(s_c, mx_c, v_ref[j + 2], m, acc)
            return m, acc, mx_a, mx_b

        def triples(i, c):
            for u in range(MLA_UNROLL):
                c = triple(3 * (MLA_UNROLL * i + u), c)
            return c

        nt = (nc - 2) // 3
        c = (m, acc, mx_a, mx_b)
        if nt >= MLA_UNROLL:
            c = lax.fori_loop(0, nt // MLA_UNROLL, triples, c)
        for t in range(nt - nt % MLA_UNROLL, nt):
            c = triple(3 * t, c)
        m, acc, mx_a, mx_b = c
        j = 3 * nt
        left = nc - j
        if left >= 3:
            mx_c = score_chunk(j + 2, s_c)
        m, acc = _mla_update(s_a, mx_a, v_ref[j], m, acc)
        if left == 4:
            mx_a = score_chunk(j + 3, s_a)
        m, acc = _mla_update(s_b, mx_b, v_ref[j + 1], m, acc)
        if left >= 3:
            m, acc = _mla_update(s_c, mx_c, v_ref[j + 2], m, acc)
        if left == 4:
            m, acc = _mla_update(s_a, mx_a, v_ref[j + 3], m, acc)
    o_ref[...] = (acc[:MLA_V_DIM] / acc[MLA_V_DIM:MLA_V_DIM + 1]).astype(o_ref.dtype)


def _mla_call(qm, km, vm, tq, sub, q_tile0, n_q, ctx_only):
    bsz, _, s = qm.shape
    nc = vm.shape[1]
    wq = tq * sub
    if ctx_only:
        k_spec = pl.BlockSpec((None, KV_SUB, MLA_HEAD_PAD), lambda b, h, i: (b, s // KV_SUB - 1, h))
        v_spec = pl.BlockSpec((None, 1, V_ROWS, KV_CHUNK), lambda b, h, i: (b, nc - 1, h, 0))
    else:
        k_spec = pl.BlockSpec((None, s, MLA_HEAD_PAD), lambda b, h, i: (b, 0, h))
        v_spec = pl.BlockSpec((None, nc, V_ROWS, KV_CHUNK), lambda b, h, i: (b, 0, h, 0))
    return pl.pallas_call(
        functools.partial(_mla_kernel, nc=nc, ctx_only=ctx_only, tq=tq),
        grid=(bsz, MLA_HEADS, n_q),
        in_specs=[pl.BlockSpec((None, MLA_HEAD_PAD, wq), lambda b, h, i: (b, h, q_tile0 + i)), k_spec, v_spec],
        out_specs=pl.BlockSpec((None, MLA_V_DIM, wq), lambda b, h, i: (b, h, i)),
        out_shape=jax.ShapeDtypeStruct((bsz, MLA_HEADS * MLA_V_DIM, n_q * wq), BF16),
        scratch_shapes=[pltpu.VMEM((KV_CHUNK, tq), F32)] * 3,
        compiler_params=_cparams(("parallel", "arbitrary", "arbitrary"), 40),
        name="mla_attn_ctx" if ctx_only else "mla_attn",
    )(qm, km, vm)


def _mla(qm, km, vm, lat, lc):
    tq = min(MLA_TQ, lat)
    sub = MLA_SUBTILES if lat % (tq * MLA_SUBTILES) == 0 else 1
    assert lc == KV_SUB and lat % lc == 0
    return (_mla_call(qm, km, vm, tq, sub, 0, lat // (tq * sub), False),
            _mla_call(qm, km, vm, lc, 1, lat // lc, 1, True))


def _merge_mlp_kernel(x_ref, mod_ref, hl_refs, hc_refs, sk_ref, yg_ref, yml_ref, ymc_ref, gt_ref, wb_ref, wo_ref,
                      g_ref, w1_ref, w2_ref, o_ref, *, lat_tiles):
    d = D_MODEL
    is_lat = pl.program_id(1) < lat_tiles
    conv, x0, z = [jnp.where(is_lat, a[...], b[...]).astype(F32) for a, b in zip(hl_refs, hc_refs)]
    yh = (x0 * (conv + z * sk_ref[...])).astype(BF16)
    yg = yg_ref[...].astype(F32).T.astype(BF16)
    ym = jnp.where(is_lat, yml_ref[...], ymc_ref[...]).astype(F32).T.astype(BF16)
    merged = (gt_ref[:, 0:d].astype(F32) * _dot(yh, wb_ref[0])
              + gt_ref[:, d:2 * d].astype(F32) * _dot(yg, wb_ref[1])
              + gt_ref[:, 2 * d:].astype(F32) * _dot(ym, wb_ref[2]))
    res = _dot(merged.astype(BF16), wo_ref[...])
    x_mix = x_ref[...] + mod_ref[:, 2 * d:3 * d] * res
    o_ref[...] = _mlp_rows(x_mix, mod_ref, g_ref, w1_ref, w2_ref)


def _merge_mlp(xs, modsel, hy_lat, hy_ctx, skip, yg, ym_lat, ym_ctx, gates, wb, wo, g, w1, w2, lat_tiles, n_tiles):
    bsz, _, d = xs.shape
    tm = ROW_TILE
    c = HY_WIDTH
    row = lambda b, i: (b, i, 0)
    col = lambda b, i: (b, 0, i)
    lat_row = pl.BlockSpec((None, tm, c), lambda b, i: (b, jnp.minimum(i, lat_tiles - 1), 0))
    ctx_row = pl.BlockSpec((None, tm, c), lambda b, i: (b, jnp.maximum(i - lat_tiles, 0), 0))
    return pl.pallas_call(
        functools.partial(_merge_mlp_kernel, lat_tiles=lat_tiles),
        grid=(bsz, n_tiles),
        in_specs=[
            pl.BlockSpec((None, tm, d), row),
            pl.BlockSpec((None, None, 1, 6 * d), lambda b, i: (b, jnp.where(i >= lat_tiles, 1, 0), 0, 0)),
            [lat_row] * 3, [ctx_row] * 3, _const_spec(skip.shape),
            pl.BlockSpec((None, c, tm), col),
            pl.BlockSpec((None, c, tm), lambda b, i: (b, 0, jnp.minimum(i, lat_tiles - 1))),
            pl.BlockSpec((None, c, tm), lambda b, i: (b, 0, jnp.maximum(i - lat_tiles, 0))),
            pl.BlockSpec((None, tm, 3 * d), row),
            _const_spec(wb.shape), _const_spec(wo.shape),
            _const_spec(g.shape), _const_spec(w1.shape), _const_spec(w2.shape),
        ],
        out_specs=pl.BlockSpec((None, tm, d), row),
        out_shape=jax.ShapeDtypeStruct((bsz, n_tiles * tm, d), F32),
        compiler_params=_cparams(("parallel", "arbitrary"), 56),
        name="merge_mlp",
    )(xs, modsel, list(hy_lat), list(hy_ctx), skip, yg, ym_lat, ym_ctx, gates, wb, wo, g, w1, w2)


def _mlp_rows(x, mod_ref, g_ref, w1_ref, w2_ref):
    d = D_MODEL
    xn = x * lax.rsqrt(jnp.mean(x * x, axis=-1, keepdims=True) + EPS) * g_ref[...]
    h = (xn * (1.0 + mod_ref[:, 4 * d:5 * d]) + mod_ref[:, 3 * d:4 * d]).astype(BF16)
    acc = jnp.zeros(x.shape, F32)
    for j in range(D_FF // FF_CHUNK):
        sl = slice(j * FF_CHUNK, (j + 1) * FF_CHUNK)
        a = jnp.maximum(_dot(h, w1_ref[:, sl]), 0.0)
        acc = acc + _dot((a * a).astype(BF16), w2_ref[sl, :])
    return x + mod_ref[:, 5 * d:] * acc


def _rope_tables_t(rows, dim, lc):
    n_freq = dim // 4
    inv = ROPE_BASE ** (-jnp.arange(n_freq, dtype=F32) / n_freq)
    r = jnp.repeat(jnp.arange(rows, dtype=F32), GRID_W)
    col = jnp.tile(jnp.arange(GRID_W, dtype=F32), rows)
    ang = jnp.concatenate([r[:, None] * inv, col[:, None] * inv], axis=-1)
    cos_t = jnp.concatenate([jnp.cos(ang).T, jnp.ones((dim // 2, lc), F32)], axis=1)
    sin_t = jnp.concatenate([jnp.sin(ang).T, jnp.zeros((dim // 2, lc), F32)], axis=1)
    return cos_t, sin_t


def _lane_bcast(v):
    return jnp.broadcast_to(v.astype(F32)[:, None], (v.shape[0], ROW_TILE))


def _layer_weights(l, w_in, gqa_q_norm, gqa_k_norm, mla_q_a_norm, mla_kv_a_norm, w_q_b, w_kv_b, mla_q_norm, mla_k_norm):
    w = w_in[l]
    o = np.cumsum([0, 3 * HY_WIDTH, GQA_HEADS * GQA_HEAD_DIM, GQA_KV_HEADS * GQA_HEAD_DIM,
                   GQA_KV_HEADS * GQA_HEAD_DIM, MLA_Q_RANK, MLA_KV_RANK, MLA_ROPE_DIM, 3 * D_MODEL])
    wu = w[:, o[0]:o[1]].astype(BF16)
    wt = w[:, o[1]:o[7]].T.astype(BF16)
    wg = w[:, o[7]:o[8]].astype(BF16)
    wqb = w_q_b[l].T.astype(BF16)
    wkvb = w_kv_b[l].T.astype(BF16)
    return (wu, wg, wt, wqb, wkvb,
            _lane_bcast(gqa_q_norm[l]), _lane_bcast(gqa_k_norm[l]),
            _lane_bcast(mla_q_a_norm[l]), _lane_bcast(mla_kv_a_norm[l]),
            _lane_bcast(mla_q_norm[l]), _lane_bcast(mla_k_norm[l]))


def kernel(x, c, ctx, c_ctx, w_mod, b_mod, norm_mix_g, norm_mlp_g, w_in, hy_short_w, hy_f1_w, hy_f1_b, hy_f2_w, hy_f2_b, hy_sin_freq, hy_f3_w, hy_skip, gqa_q_norm, gqa_k_norm, gqa_sink, mla_q_a_norm, mla_kv_a_norm, w_q_b, w_kv_b, mla_q_norm, mla_k_norm, w_branch, w_out, w_mlp1, w_mlp2):
    bsz, lat, d = x.shape
    lc = ctx.shape[1]
    depth = w_mod.shape[0]
    s = lat + lc
    assert d == D_MODEL and lat % GRID_W == 0 and lat % ROW_TILE == 0 and lc == ROW_TILE and s % KV_CHUNK == 0
    lat_tiles = lat // ROW_TILE

    pad = (-(bsz + 1)) % 8
    cond = jnp.concatenate([c, c_ctx[None, :], jnp.zeros((pad, d), F32)], axis=0)
    mods = _mod_all(cond, w_mod, b_mod)

    tabs = _rope_tables_t(lat // GRID_W, GQA_HEAD_DIM, lc) + _rope_tables_t(lat // GRID_W, MLA_ROPE_DIM, lc)
    xs = jnp.concatenate([x, ctx], axis=1)

    for l in range(depth):
        ml = mods[l]
        modsel = jnp.stack([ml[:bsz], jnp.broadcast_to(ml[bsz][None], (bsz, 6 * d))], axis=1)[:, :, None, :]
        wts = _layer_weights(l, w_in, gqa_q_norm, gqa_k_norm, mla_q_a_norm, mla_kv_a_norm, w_q_b, w_kv_b,
                             mla_q_norm, mla_k_norm)
        u, gates, qg, kg, vg, qm, km, vm = _proj(xs, modsel, norm_mix_g[l][None, :], wts, tabs, lat_tiles)

        f1w = jnp.zeros((V7X_LANES, HY_FILTER_WIDTH), F32).at[:HY_EMB_DIM].set(hy_f1_w[l])
        fw = (f1w, hy_f1_b[l][None, :], hy_f2_w[l], hy_f2_b[l][None, :], hy_sin_freq[l], hy_f3_w[l])
        hy_lat, hy_ctx = _hyena(u, hy_short_w[l], fw, lat, lc)

        sink_rows = jnp.broadcast_to(gqa_sink[l].astype(F32)[:, None], (GQA_HEADS, GQA_TQ))
        yg = _gqa(qg, kg, vg, sink_rows, lat, lc)
        ym_lat, ym_ctx = _mla(qm, km, vm, lat, lc)

        n_tiles = lat_tiles if l == depth - 1 else s // ROW_TILE
        xs = _merge_mlp(xs, modsel, hy_lat, hy_ctx, hy_skip[l][None, :], yg, ym_lat, ym_ctx, gates,
                        w_branch[l].astype(BF16), w_out[l].astype(BF16), norm_mlp_g[l][None, :],
                        w_mlp1[l].astype(BF16), w_mlp2[l].astype(BF16), lat_tiles, n_tiles)
    return xs
```

```python
import functools
import math

import numpy as np
import jax
import jax.numpy as jnp
from jax import lax
from jax.experimental import pallas as pl
from jax.experimental.pallas import tpu as pltpu

D_MODEL = 1024
GRID_W = 64
HY_WIDTH = 512
HY_EMB_DIM = 33
HY_BANDS = (HY_EMB_DIM - 1) // 2
HY_FILTER_WIDTH = 64
HY_DECAY_TARGET = 1e-2
HY_FAST_DECAY = 0.3
HY_SLOW_DECAY = 1.5
GQA_HEADS = 8
GQA_KV_HEADS = 2
GQA_GROUP = GQA_HEADS // GQA_KV_HEADS
GQA_HEAD_DIM = 64
GQA_SCALE = GQA_HEAD_DIM ** -0.5
WINDOW = 128
MLA_HEADS = 8
MLA_Q_RANK = 384
MLA_KV_RANK = 256
MLA_NOPE_DIM = 64
MLA_ROPE_DIM = 32
MLA_V_DIM = 64
MLA_QK_DIM = MLA_NOPE_DIM + MLA_ROPE_DIM
MLA_SCALE = MLA_QK_DIM ** -0.5
D_FF = 4 * D_MODEL
ROPE_BASE = 10000.0
EPS = 1e-6
LOG2E = 1.4426950408889634
NEG_BIG = -1e30

V7X_LANES = 128
V7X_VMEM_BYTES = 64 * 1024 * 1024

ROW_TILE = 256
PROJ_BATCH = 2
CONV_TILE = 512
MLA_HEAD_PAD = 128
V_ROWS = 80
KV_CHUNK = 1 * ROW_TILE
KV_SUB = 256
MLA_UNROLL = 5
MLA_TQ = 1024
MLA_SUBTILES = 2
GQA_TQ = 256
FF_CHUNK = 1024
FFT_N2 = 128
FFT_NB = 16
FFT_KB = 8

F32 = jnp.float32
BF16 = jnp.bfloat16


def _cparams(sem, vmem_mb, flags=None):
    return pltpu.CompilerParams(dimension_semantics=sem, vmem_limit_bytes=vmem_mb * 1024 * 1024, flags=flags)


def _dot(a, b):
    return jnp.dot(a, b, preferred_element_type=F32)


def _dot_hi(a, b):
    return jnp.dot(a, b, preferred_element_type=F32, precision=lax.Precision.HIGHEST)


def _ones_row_block(width):
    r = lax.broadcasted_iota(jnp.int32, (V_ROWS - MLA_V_DIM, width), 0)
    return jnp.where(r == 0, 1.0, 0.0).astype(BF16)


def _const_spec(shape):
    nd = len(shape)
    return pl.BlockSpec(shape, lambda *_: (0,) * nd, pipeline_mode=pl.Buffered(1))


def _mod_kernel(c_ref, w_ref, b_ref, o_ref):
    c = c_ref[...]
    s = c * jax.nn.sigmoid(c)
    o_ref[...] = _dot_hi(s, w_ref[...]) + b_ref[...]


def _mod_all(cond, w_mod, b_mod):
    depth, d, n = w_mod.shape
    rows = cond.shape[0]
    tn = 1536
    return pl.pallas_call(
        _mod_kernel,
        grid=(depth, n // tn),
        in_specs=[
            pl.BlockSpec((rows, d), lambda l, j: (0, 0)),
            pl.BlockSpec((None, d, tn), lambda l, j: (l, 0, j)),
            pl.BlockSpec((None, 1, tn), lambda l, j: (l, 0, j)),
        ],
        out_specs=pl.BlockSpec((None, rows, tn), lambda l, j: (l, 0, j)),
        out_shape=jax.ShapeDtypeStruct((depth, rows, n), F32),
        compiler_params=_cparams(("arbitrary", "arbitrary"), 40),
        name="adaln_mod",
    )(cond, w_mod, b_mod.reshape(depth, 1, n))


def _rms_rows(x, n):
    return lax.rsqrt(jnp.sum(x * x, axis=0, keepdims=True) * (1.0 / n) + EPS)


def _rope_rows(x1, x2, cs, sn):
    return x1 * cs - x2 * sn, x1 * sn + x2 * cs


def _proj_kernel(x_ref, mod_ref, *refs):
    consts, outs = refs[:16], refs[16:]
    for n in range(x_ref.shape[0]):
        _proj_one(x_ref.at[n], mod_ref.at[n], *consts, *[o.at[n] for o in outs])


def _proj_one(x_ref, mod_ref, g_ref, wu_ref, wg_ref, wt_ref, wqb_ref, wkvb_ref,
              gq_ref, gk_ref, gqa_ref, gkva_ref, gmq_ref, gmk_ref,
              cg_ref, sg_ref, cm_ref, sm_ref,
              u_ref, gate_ref, qg_ref, kg_ref, vg_ref, qm_ref, km_ref, vm_ref):
    d = D_MODEL
    x = x_ref[...]
    tm = x.shape[0]
    shift = mod_ref[:, 0:d]
    scale = mod_ref[:, d:2 * d]
    xn = x * lax.rsqrt(jnp.mean(x * x, axis=-1, keepdims=True) + EPS) * g_ref[...]
    h = xn * (1.0 + scale) + shift
    hb = h.astype(BF16)

    ht = h.T.astype(BF16)
    t = _dot(wt_ref[...], ht)
    o_q, o_k, o_v = 0, 512, 640
    o_cq, o_ckv, o_kr = 768, 768 + MLA_Q_RANK, 768 + MLA_Q_RANK + MLA_KV_RANK

    u_ref[...] = _dot(hb, wu_ref[...]).astype(BF16)
    gate_ref[...] = jax.nn.sigmoid(_dot(hb, wg_ref[...])).astype(BF16)

    cg, sg = cg_ref[...], sg_ref[...]
    cm, sm = cm_ref[...], sm_ref[...]
    hd, hh = GQA_HEAD_DIM, GQA_HEAD_DIM // 2

    cq = t[o_cq:o_cq + MLA_Q_RANK]
    cqn = (cq * _rms_rows(cq, MLA_Q_RANK) * gqa_ref[...]).astype(BF16)
    qm = _dot(wqb_ref[...], cqn)
    gmq = gmq_ref[...]
    nd, rh = MLA_NOPE_DIM, MLA_ROPE_DIM // 2
    qpad = jnp.zeros((MLA_HEAD_PAD - MLA_QK_DIM, tm), BF16)
    for n in range(MLA_HEADS):
        xh = qm[n * MLA_QK_DIM:(n + 1) * MLA_QK_DIM]
        xh = xh * _rms_rows(xh, MLA_QK_DIM) * gmq * (MLA_SCALE * LOG2E)
        a, b = _rope_rows(xh[nd:nd + rh], xh[nd + rh:], cm, sm)
        base = n * MLA_HEAD_PAD
        qm_ref[base:base + nd, :] = xh[:nd].astype(BF16)
        qm_ref[base + nd:base + nd + rh, :] = a.astype(BF16)
        qm_ref[base + nd + rh:base + MLA_QK_DIM, :] = b.astype(BF16)
        qm_ref[base + MLA_QK_DIM:base + MLA_HEAD_PAD, :] = qpad

    ckv = t[o_ckv:o_ckv + MLA_KV_RANK]
    ckvn = (ckv * _rms_rows(ckv, MLA_KV_RANK) * gkva_ref[...]).astype(BF16)
    kv = _dot(wkvb_ref[...], ckvn)
    kr = t[o_kr:o_kr + MLA_ROPE_DIM]
    kr_ss = jnp.sum(kr * kr, axis=0, keepdims=True)
    gmk = gmk_ref[...]
    kzero = jnp.zeros((MLA_HEAD_PAD - MLA_QK_DIM, tm), F32)
    ones_rows = _ones_row_block(tm)
    parts = []
    for n in range(MLA_HEADS):
        kn = kv[n * 128:n * 128 + nd]
        vm_ref[n * V_ROWS:n * V_ROWS + MLA_V_DIM, :] = kv[n * 128 + nd:(n + 1) * 128].astype(BF16)
        vm_ref[n * V_ROWS + MLA_V_DIM:(n + 1) * V_ROWS, :] = ones_rows
        rs = lax.rsqrt((jnp.sum(kn * kn, axis=0, keepdims=True) + kr_ss) * (1.0 / MLA_QK_DIM) + EPS)
        krn = kr * rs * gmk[nd:]
        a, b = _rope_rows(krn[:rh], krn[rh:], cm, sm)
        parts += [kn * rs * gmk[:nd], a, b, kzero]
    km_ref[...] = jnp.concatenate(parts, axis=0).T.astype(BF16)

    gq = gq_ref[...]
    for n in range(GQA_HEADS):
        xh = t[o_q + n * hd:o_q + (n + 1) * hd]
        xh = xh * _rms_rows(xh, hd) * gq
        a, b = _rope_rows(xh[:hh], xh[hh:], cg, sg)
        qg_ref[n * hd:n * hd + hh, :] = (a * (GQA_SCALE * LOG2E)).astype(BF16)
        qg_ref[n * hd + hh:(n + 1) * hd, :] = (b * (GQA_SCALE * LOG2E)).astype(BF16)

    gk = gk_ref[...]
    zpad = jnp.zeros((V7X_LANES - hd, tm), F32)
    parts = []
    for n in range(GQA_KV_HEADS):
        xh = t[o_k + n * hd:o_k + (n + 1) * hd]
        xh = xh * _rms_rows(xh, hd) * gk
        a, b = _rope_rows(xh[:hh], xh[hh:], cg, sg)
        parts += [a, b, zpad]
    kg_ref[...] = jnp.concatenate(parts, axis=0).T.astype(BF16)
    for n in range(GQA_KV_HEADS):
        vg_ref[n * V_ROWS:n * V_ROWS + hd, :] = t[o_v + n * hd:o_v + (n + 1) * hd].astype(BF16)
        vg_ref[n * V_ROWS + hd:(n + 1) * V_ROWS, :] = ones_rows


def _proj(xs, modsel, g, wts, tabs, lat_tiles):
    bsz, s, d = xs.shape
    tm = ROW_TILE
    nt = s // tm
    ncs = KV_CHUNK // tm
    nb = PROJ_BATCH if bsz % PROJ_BATCH == 0 else 1
    (wu, wg, wt, wqb, wkvb, gq, gk, gqa, gkva, gmq, gmk) = wts
    cg, sg, cm, sm = tabs
    row = lambda b, i: (b, i, 0)
    col = lambda b, i: (b, 0, i)
    tab = lambda b, i: (0, i)
    in_specs = [
        pl.BlockSpec((nb, tm, d), row),
        pl.BlockSpec((nb, None, 1, 6 * d), lambda b, i: (b, jnp.where(i >= lat_tiles, 1, 0), 0, 0)),
        _const_spec(g.shape), _const_spec(wu.shape), _const_spec(wg.shape), _const_spec(wt.shape),
        _const_spec(wqb.shape), _const_spec(wkvb.shape),
        _const_spec(gq.shape), _const_spec(gk.shape), _const_spec(gqa.shape), _const_spec(gkva.shape),
        _const_spec(gmq.shape), _const_spec(gmk.shape),
        pl.BlockSpec((cg.shape[0], tm), tab), pl.BlockSpec((sg.shape[0], tm), tab),
        pl.BlockSpec((cm.shape[0], tm), tab), pl.BlockSpec((sm.shape[0], tm), tab),
    ]
    out_shape = [
        jax.ShapeDtypeStruct((bsz, s, 3 * HY_WIDTH), BF16),
        jax.ShapeDtypeStruct((bsz, s, 3 * d), BF16),
        jax.ShapeDtypeStruct((bsz, GQA_HEADS * GQA_HEAD_DIM, s), BF16),
        jax.ShapeDtypeStruct((bsz, s, GQA_KV_HEADS * V7X_LANES), BF16),
        jax.ShapeDtypeStruct((bsz, GQA_KV_HEADS * V_ROWS, s), BF16),
        jax.ShapeDtypeStruct((bsz, MLA_HEADS * MLA_HEAD_PAD, s), BF16),
        jax.ShapeDtypeStruct((bsz, s, MLA_HEADS * MLA_HEAD_PAD), BF16),
        jax.ShapeDtypeStruct((bsz, s // KV_CHUNK, MLA_HEADS * V_ROWS, KV_CHUNK), BF16),
    ]
    out_specs = [
        pl.BlockSpec((nb, tm, 3 * HY_WIDTH), row),
        pl.BlockSpec((nb, tm, 3 * d), row),
        pl.BlockSpec((nb, GQA_HEADS * GQA_HEAD_DIM, tm), col),
        pl.BlockSpec((nb, tm, GQA_KV_HEADS * V7X_LANES), row),
        pl.BlockSpec((nb, GQA_KV_HEADS * V_ROWS, tm), col),
        pl.BlockSpec((nb, MLA_HEADS * MLA_HEAD_PAD, tm), col),
        pl.BlockSpec((nb, tm, MLA_HEADS * MLA_HEAD_PAD), row),
        pl.BlockSpec((nb, None, MLA_HEADS * V_ROWS, tm), lambda b, i: (b, i // ncs, 0, i % ncs)),
    ]
    return pl.pallas_call(
        _proj_kernel,
        grid=(bsz // nb, nt),
        in_specs=in_specs,
        out_specs=out_specs,
        out_shape=out_shape,
        compiler_params=_cparams(("parallel", "arbitrary"), 56),
        name="in_proj",
    )(xs, modsel, g, wu, wg, wt, wqb, wkvb, gq, gk, gqa, gkva, gmq, gmk, cg, sg, cm, sm)


def _short_conv_kernel(u_ref, up_ref, un_ref, w_ref, x0_ref, z_ref, *, n_tiles):
    i = pl.program_id(1)
    u = u_ref[...].astype(F32)
    tm = u.shape[0]
    prev = jnp.where(i > 0, up_ref[7:8, :].astype(F32), 0.0)
    nxt = jnp.where(i < n_tiles - 1, un_ref[0:1, :].astype(F32), 0.0)
    ridx = lax.broadcasted_iota(jnp.int32, u.shape, 0)
    up = jnp.where(ridx == 0, prev, pltpu.roll(u, 1, axis=0))
    dn = jnp.where(ridx == tm - 1, nxt, pltpu.roll(u, tm - 1, axis=0))
    uc = up * w_ref[0:1, :] + u * w_ref[1:2, :] + dn * w_ref[2:3, :]
    c = HY_WIDTH
    x0_ref[...] = uc[:, :c].astype(BF16)
    z_ref[...] = (uc[:, c:2 * c] * uc[:, 2 * c:]).astype(BF16)


def _short_conv(u, short_w, row0, rows):
    bsz, s, c3 = u.shape
    tm = math.gcd(math.gcd(rows, row0), CONV_TILE) if row0 else math.gcd(rows, CONV_TILE)
    nt = rows // tm
    t0 = row0 // tm
    r8 = tm // 8
    last8 = s // 8 - 1
    return pl.pallas_call(
        functools.partial(_short_conv_kernel, n_tiles=nt),
        grid=(bsz, nt),
        in_specs=[
            pl.BlockSpec((None, tm, c3), lambda b, i: (b, t0 + i, 0)),
            pl.BlockSpec((None, 8, c3), lambda b, i: (b, jnp.maximum((t0 + i) * r8 - 1, 0), 0)),
            pl.BlockSpec((None, 8, c3), lambda b, i: (b, jnp.minimum((t0 + i + 1) * r8, last8), 0)),
            _const_spec(short_w.shape),
        ],
        out_specs=[pl.BlockSpec((None, tm, HY_WIDTH), lambda b, i: (b, i, 0))] * 2,
        out_shape=[jax.ShapeDtypeStruct((bsz, rows, HY_WIDTH), BF16)] * 2,
        compiler_params=_cparams(("parallel", "arbitrary"), 32),
        name="hyena_short_conv",
    )(u, u, u, short_w)


def _filter_kernel(zf_ref, w1_ref, b1_ref, w2_ref, b2_ref, fr_ref, w3_ref, dl_ref, h_ref, ss_ref):
    i = pl.program_id(0)
    zf = zf_ref[...]
    tl = zf.shape[0]
    h = jnp.sin(fr_ref[0:1, :] * (_dot_hi(zf, w1_ref[...]) + b1_ref[...]))
    h = jnp.sin(fr_ref[1:2, :] * (_dot_hi(h, w2_ref[...]) + b2_ref[...]))
    h = _dot_hi(h, w3_ref[...])
    decay = jnp.exp(-zf[:, 0:1] * dl_ref[...])
    c = HY_WIDTH
    hf = h[:, :c] * decay
    ridx = lax.broadcasted_iota(jnp.int32, (tl, c), 0) + i * tl
    hb = jnp.where(ridx == 0, 0.0, h[:, c:] * decay)
    h_ref[0] = hf.astype(BF16)
    h_ref[1] = hb.astype(BF16)
    ss = jnp.sum(hf * hf + hb * hb, axis=0, keepdims=True)

    @pl.when(i == 0)
    def _():
        ss_ref[...] = ss

    @pl.when(i > 0)
    def _():
        ss_ref[...] += ss


def _filter(zfeat, fw, deltas):
    length = zfeat.shape[0]
    tl = min(length, 1024)
    w1, b1, w2, b2, fr, w3 = fw
    return pl.pallas_call(
        _filter_kernel,
        grid=(length // tl,),
        in_specs=[pl.BlockSpec((tl, zfeat.shape[1]), lambda i: (i, 0))]
        + [_const_spec(a.shape) for a in (w1, b1, w2, b2, fr, w3, deltas)],
        out_specs=[pl.BlockSpec((2, tl, HY_WIDTH), lambda i: (0, i, 0)),
                   pl.BlockSpec((1, HY_WIDTH), lambda i: (0, 0))],
        out_shape=[jax.ShapeDtypeStruct((2, length, HY_WIDTH), BF16),
                   jax.ShapeDtypeStruct((1, HY_WIDTH), F32)],
        compiler_params=_cparams(("arbitrary",), 40),
        name="hyena_filter",
    )(zfeat, w1, b1, w2, b2, fr, w3, deltas)


def _fft_a_kernel(g_ref, x_ref, o_ref, *, nb, cw):
    for j in range(nb):
        sl = slice(j * cw, (j + 1) * cw)
        x = jnp.concatenate([x_ref[p, :, sl] for p in range(x_ref.shape[0])], axis=0)
        o_ref[:, sl] = _dot(g_ref[j], x).astype(o_ref.dtype)


def _fft_a(x3d, gmat):
    bx, parts, k1, w = x3d.shape
    n2, two_n1, _ = gmat.shape
    cw = w // n2
    nb = FFT_NB
    return pl.pallas_call(
        functools.partial(_fft_a_kernel, nb=nb, cw=cw),
        grid=(bx, n2 // nb),
        in_specs=[pl.BlockSpec((nb, two_n1, parts * k1), lambda b, j: (j, 0, 0)),
                  pl.BlockSpec((None, parts, k1, nb * cw), lambda b, j: (b, 0, 0, j))],
        out_specs=pl.BlockSpec((None, two_n1, nb * cw), lambda b, j: (b, 0, j)),
        out_shape=jax.ShapeDtypeStruct((bx, two_n1, w), BF16),
        compiler_params=_cparams(("parallel", "arbitrary"), 40),
        name="fft_stage_a",
    )(gmat, x3d)


def _fft_filter_b_kernel(fb_ref, a_ref, ss_ref, kf_ref, *, inv_n):
    n = a_ref.shape[3]
    rs = lax.rsqrt(ss_ref[...] + EPS) * inv_n
    for r in range(a_ref.shape[2]):
        xf = _dot(fb_ref[...], jnp.concatenate([a_ref[0, 0, r], a_ref[0, 1, r]], axis=0))
        xb = _dot(fb_ref[...], jnp.concatenate([a_ref[1, 0, r], a_ref[1, 1, r]], axis=0))
        kf_ref[0, r] = (xf[:n] + xb[:n]) * rs
        kf_ref[1, r] = (xf[n:] - xb[n:]) * rs


def _fft_filter_b(a5, fb, ssq, inv_n):
    _, _, n1, n2, c = a5.shape
    kb = min(FFT_KB, n1)
    return pl.pallas_call(
        functools.partial(_fft_filter_b_kernel, inv_n=inv_n),
        grid=(n1 // kb,),
        in_specs=[_const_spec(fb.shape),
                  pl.BlockSpec((2, 2, kb, n2, c), lambda k: (0, 0, k, 0, 0)),
                  _const_spec(ssq.shape)],
        out_specs=pl.BlockSpec((2, kb, n2, c), lambda k: (0, k, 0, 0)),
        out_shape=jax.ShapeDtypeStruct((2, n1, n2, c), F32),
        compiler_params=_cparams(("arbitrary",), 32),
        name="fft_filter_stage_b",
    )(fb, a5, ssq)


def _fft_b_kernel(fb_ref, fbi_ref, a_ref, kf_ref, o_ref):
    kb, n, cw = a_ref.shape[1:]
    a_all = jnp.concatenate([jnp.concatenate([a_ref[0, r], a_ref[1, r]], axis=0) for r in range(kb)], axis=1)
    x = _dot(fb_ref[...], a_all)
    ys = []
    for r in range(kb):
        xr, xi = x[:n, r * cw:(r + 1) * cw], x[n:, r * cw:(r + 1) * cw]
        kr, ki = kf_ref[0, r], kf_ref[1, r]
        ys.append(jnp.concatenate([xr * kr - xi * ki, xr * ki + xi * kr], axis=0).astype(BF16))
    c = _dot(fbi_ref[...], jnp.concatenate(ys, axis=1))
    for r in range(kb):
        o_ref[0, r] = c[:n, r * cw:(r + 1) * cw].astype(o_ref.dtype)
        o_ref[1, r] = c[n:, r * cw:(r + 1) * cw].astype(o_ref.dtype)


def _fft_b(a5, kf, fb, fbi):
    bsz, _, n1, n2, c = a5.shape
    kb = min(FFT_KB, n1)
    return pl.pallas_call(
        _fft_b_kernel,
        grid=(n1 // kb, bsz),
        in_specs=[_const_spec(fb.shape), _const_spec(fbi.shape),
                  pl.BlockSpec((None, 2, kb, n2, c), lambda k, b: (b, 0, k, 0, 0)),
                  pl.BlockSpec((2, kb, n2, c), lambda k, b: (0, k, 0, 0))],
        out_specs=pl.BlockSpec((None, 2, kb, n2, c), lambda k, b: (b, 0, k, 0, 0)),
        out_shape=jax.ShapeDtypeStruct(a5.shape, BF16),
        compiler_params=_cparams(("arbitrary", "arbitrary"), 32),
        name="fft_stage_b",
    )(fb, fbi, a5, kf)


def _fft_c_kernel(h_ref, c_ref, o_ref, *, nb, cw):
    parts, k1 = o_ref.shape[0], o_ref.shape[1]
    for j in range(nb):
        sl = slice(j * cw, (j + 1) * cw)
        y = _dot(h_ref[j], c_ref[:, sl])
        for p in range(parts):
            o_ref[p, :, sl] = y[p * k1:(p + 1) * k1].astype(o_ref.dtype)


def _fft_c(c2d, hmat, parts):
    bx, two_n1, w = c2d.shape
    n2, rows, _ = hmat.shape
    k1 = rows // parts
    cw = w // n2
    nb = FFT_NB
    return pl.pallas_call(
        functools.partial(_fft_c_kernel, nb=nb, cw=cw),
        grid=(bx, n2 // nb),
        in_specs=[pl.BlockSpec((nb, rows, two_n1), lambda b, j: (j, 0, 0)),
                  pl.BlockSpec((None, two_n1, nb * cw), lambda b, j: (b, 0, j))],
        out_specs=pl.BlockSpec((None, parts, k1, nb * cw), lambda b, j: (b, 0, 0, j)),
        out_shape=jax.ShapeDtypeStruct((bx, parts, k1, w), BF16),
        compiler_params=_cparams(("parallel", "arbitrary"), 40),
        name="fft_stage_c",
    )(hmat, c2d)


def _ctx_conv_kernel(fc_ref, fci_ref, h_ref, ss_ref, z_ref, o_ref):
    n = fc_ref.shape[0] // 2
    fc = fc_ref[...]
    kf = _dot(fc, h_ref[0])
    kb = _dot(fc, h_ref[1])
    rs = lax.rsqrt(ss_ref[...] + EPS)
    kr = (kf[:n] + kb[:n]) * rs
    ki = (kf[n:] - kb[n:]) * rs
    x = _dot(fc, z_ref[...])
    xr, xi = x[:n], x[n:]
    y = jnp.concatenate([xr * kr - xi * ki, xr * ki + xi * kr], axis=0).astype(BF16)
    o_ref[...] = _dot(fci_ref[...], y).astype(o_ref.dtype)


def _ctx_conv(fc, fci, hfb, ssq, z):
    bsz, lc, c = z.shape
    blk = pl.BlockSpec((None, lc, c), lambda b: (b, 0, 0))
    return pl.pallas_call(
        _ctx_conv_kernel,
        grid=(bsz,),
        in_specs=[_const_spec(fc.shape), _const_spec(fci.shape), _const_spec(hfb.shape),
                  _const_spec(ssq.shape), blk],
        out_specs=blk,
        out_shape=jax.ShapeDtypeStruct((bsz, lc, c), BF16),
        compiler_params=_cparams(("arbitrary",), 32),
        name="hyena_ctx_conv",
    )(fc, fci, hfb, ssq, z)


@functools.lru_cache(maxsize=None)
def _fft_tables(length):
    n = 2 * length
    n2 = FFT_N2
    n1 = n // n2
    k1 = length // n2
    kk = np.arange(n1)[:, None]
    g = np.empty((n2, 2 * n1, k1), np.float64)
    g2 = np.empty((n2, 2 * n1, 2 * k1), np.float64)
    h2 = np.empty((n2, 2 * k1, 2 * n1), np.float64)
    nn = np.arange(k1)[None, :]
    for j in range(n2):
        ang = 2.0 * np.pi * (((n2 * nn * kk) % n) + (j * kk) % n) / n
        c_, s_ = np.cos(ang), np.sin(ang)
        g[j] = np.concatenate([c_, -s_], axis=0)
        g2[j] = np.block([[c_, s_], [-s_, c_]])
        h2[j] = np.block([[c_.T, -s_.T], [s_.T, c_.T]])
    a = np.arange(n2)
    ph = 2.0 * np.pi * ((a[:, None] * a[None, :]) % n2) / n2
    c, s = np.cos(ph), np.sin(ph)
    fb = np.block([[c, s], [-s, c]])
    fbi = np.block([[c, -s], [s, c]])
    return (jnp.asarray(g, BF16), jnp.asarray(g2, BF16), jnp.asarray(h2, BF16), jnp.asarray(fb, BF16),
            jnp.asarray(fbi, BF16), n1, k1)


@functools.lru_cache(maxsize=None)
def _dft_tables(length):
    n = 2 * length
    k = np.arange(n)[:, None]
    t = np.arange(length)[None, :]
    ang = 2.0 * np.pi * ((k * t) % n) / n
    fc = np.concatenate([np.cos(ang), -np.sin(ang)], axis=0)
    fci = np.concatenate([np.cos(ang).T, -np.sin(ang).T], axis=1) / n
    return jnp.asarray(fc, BF16), jnp.asarray(fci, BF16)


@functools.lru_cache(maxsize=None)
def _filter_features(length):
    t = np.linspace(0.0, 1.0, length, dtype=np.float32)[:, None]
    w = (2.0 * math.pi * np.arange(length, dtype=np.float32)[:, None] / length).astype(np.float32)
    f = np.linspace(1e-4, HY_BANDS - 1, HY_BANDS, dtype=np.float32)[None, :]
    z = np.concatenate([t, np.cos(f * w), -np.sin(f * w)], axis=-1).astype(np.float32)
    zp = np.zeros((length, V7X_LANES), np.float32)
    zp[:, :HY_EMB_DIM] = z
    return jnp.asarray(zp)


def _hyena_deltas():
    max_decay = math.log(HY_DECAY_TARGET) / HY_FAST_DECAY
    min_decay = math.log(HY_DECAY_TARGET) / HY_SLOW_DECAY
    return jnp.abs(jnp.linspace(min_decay, max_decay, HY_WIDTH, dtype=F32))[None, :]


def _hyena(u, short_w, fw, lat, lc):
    bsz = u.shape[0]
    c = HY_WIDTH
    deltas = _hyena_deltas()
    x0, z = _short_conv(u, short_w, 0, lat)
    gmat, gmat2, hmat2, fb, fbi, n1, k1 = _fft_tables(lat)
    n2 = FFT_N2
    hfb, ssq = _filter(_filter_features(lat), fw, deltas)
    fa = _fft_a(hfb.reshape(2, 1, k1, n2 * c), gmat)
    kf = _fft_filter_b(fa.reshape(2, 2, n1, n2, c), fb, ssq, 1.0 / (2 * lat))
    npair = (bsz + 1) // 2
    zp = z if bsz % 2 == 0 else jnp.concatenate([z, jnp.zeros_like(z[:1])], axis=0)
    za = _fft_a(zp.reshape(npair, 2, k1, n2 * c), gmat2)
    zc = _fft_b(za.reshape(npair, 2, n1, n2, c), kf, fb, fbi)
    conv = _fft_c(zc.reshape(npair, 2 * n1, n2 * c), hmat2, 2).reshape(2 * npair, lat, c)[:bsz]
    x0_c, z_c = _short_conv(u, short_w, lat, lc)
    hfb_c, ssq_c = _filter(_filter_features(lc), fw, deltas)
    fc, fci = _dft_tables(lc)
    conv_c = _ctx_conv(fc, fci, hfb_c, ssq_c, z_c)
    return (conv, x0, z), (conv_c, x0_c, z_c)


def _gqa_kernel(q_ref, k_ref, v_ref, sink_ref, o_ref, s_a, s_b, s_c, s_d, *, lat, lc):
    i = pl.program_id(1)
    tq = q_ref.shape[1]
    s_len = k_ref.shape[0]
    hd = GQA_HEAD_DIM
    w = WINDOW
    nwin = tq + 2 * w
    start = pl.multiple_of(jnp.clip(i * tq - w, 0, s_len - nwin), w)
    k_all = jnp.concatenate([k_ref[pl.ds(start, nwin), :], k_ref[lat:lat + lc, :]], axis=0)
    v_all = jnp.concatenate([v_ref[:, pl.ds(start, nwin)], v_ref[:, lat:lat + lc]], axis=1)
    q_pos = i * tq + lax.broadcasted_iota(jnp.int32, (1, tq), 1)
    k_pos = start + lax.broadcasted_iota(jnp.int32, (nwin, 1), 0)
    bias = (jnp.where(jnp.abs(k_pos - q_pos) <= w, 0.0, NEG_BIG)
            + jnp.where(k_pos < lat, 0.0, NEG_BIG)
            + jnp.where(q_pos < lat, 0.0, NEG_BIG))
    bias = jnp.concatenate([bias, jnp.zeros((lc, tq), F32)], axis=0)
    bias2 = jnp.concatenate([bias, bias], axis=1)
    zq = jnp.zeros((V7X_LANES - hd, 2 * tq), BF16)

    def score(pair, dst):
        g = pair // (GQA_GROUP // 2)
        r = 2 * pair * hd
        q2 = jnp.concatenate([q_ref[r:r + hd, :], q_ref[r + hd:r + 2 * hd, :]], axis=1)
        s = _dot(k_all[:, g * V7X_LANES:(g + 1) * V7X_LANES], jnp.concatenate([q2, zq], axis=0)) + bias2
        dst[...] = s
        return jnp.max(s, axis=0, keepdims=True)

    def update(pair, s_sc, s_max):
        g = pair // (GQA_GROUP // 2)
        r = 2 * pair * hd
        sink = jnp.concatenate([sink_ref[2 * pair:2 * pair + 1, :], sink_ref[2 * pair + 1:2 * pair + 2, :]],
                               axis=1) * LOG2E
        m = jnp.maximum(s_max, sink)
        p = jnp.exp2(s_sc[...] - m).astype(BF16)
        pv = _dot(v_all[g * V_ROWS:(g + 1) * V_ROWS, :], p)
        o = pv[:hd] / (pv[hd:hd + 1] + jnp.exp2(sink - m))
        o_ref[r:r + hd, :] = o[:, :tq].astype(o_ref.dtype)
        o_ref[r + hd:r + 2 * hd, :] = o[:, tq:].astype(o_ref.dtype)

    bufs = (s_a, s_b, s_c, s_d)
    npair = GQA_HEADS // 2
    mx = [score(j, bufs[j]) for j in range(npair)]
    for j in range(npair):
        update(j, bufs[j], mx[j])


def _gqa(qg, kg, vg, sink_rows, lat, lc):
    bsz, nq, s = qg.shape
    tq = GQA_TQ
    return pl.pallas_call(
        functools.partial(_gqa_kernel, lat=lat, lc=lc),
        grid=(bsz, s // tq),
        in_specs=[
            pl.BlockSpec((None, nq, tq), lambda b, i: (b, 0, i)),
            pl.BlockSpec((None, s, kg.shape[2]), lambda b, i: (b, 0, 0)),
            pl.BlockSpec((None, vg.shape[1], s), lambda b, i: (b, 0, 0)),
            _const_spec(sink_rows.shape),
        ],
        out_specs=pl.BlockSpec((None, nq, tq), lambda b, i: (b, 0, i)),
        out_shape=jax.ShapeDtypeStruct((bsz, nq, s), BF16),
        scratch_shapes=[pltpu.VMEM((tq + 2 * WINDOW + lc, 2 * tq), F32)] * 4,
        compiler_params=_cparams(("parallel", "arbitrary"), 40),
        name="gqa_window_attn",
    )(qg, kg, vg, sink_rows)


def _mla_update(s_ref, s_max, vt, m, acc):
    m_new = jnp.maximum(m, s_max)
    alpha = jnp.exp2(m - m_new)
    p = jnp.exp2(s_ref[...] - m_new).astype(BF16)
    acc = alpha * acc + _dot(vt, p)
    return m_new, acc


def _mla_kernel(q_ref, k_ref, v_ref, o_ref, s_a, s_b, s_c, *, nc, ctx_only, tq):
    for t in range(q_ref.shape[1] // tq):
        lanes = slice(t * tq, (t + 1) * tq)
        _mla_tile(q_ref[:, lanes], k_ref, v_ref, o_ref.at[:, lanes], s_a, s_b, s_c, nc, ctx_only)


def _mla_tile(q, k_ref, v_ref, o_ref, s_a, s_b, s_c, nc, ctx_only):
    tq = q.shape[1]
    ck = KV_CHUNK
    m = jnp.full((1, tq), NEG_BIG, F32)
    acc = jnp.zeros((V_ROWS, tq), F32)

    def score(k, dst):
        s = _dot(k, q)
        dst[...] = s
        return jnp.max(s, axis=0, keepdims=True)

    def score_chunk(j, dst):
        return score(k_ref[pl.ds(pl.multiple_of(j * ck, ck), ck), :], dst)

    if ctx_only:
        s_x = s_a.at[0:KV_SUB, :]
        mx = score(k_ref[k_ref.shape[0] - KV_SUB:, :], s_x)
        m, acc = _mla_update(s_x, mx, v_ref[v_ref.shape[0] - 1][:, ck - KV_SUB:], m, acc)
    elif nc == 1:
        m, acc = _mla_update(s_a, score_chunk(0, s_a), v_ref[0], m, acc)
    else:
        mx_a = score_chunk(0, s_a)
        mx_b = score_chunk(1, s_b)

        def triple(j, c):
            m, acc, mx_a, mx_b = c
            mx_c = score_chunk(j + 2, s_c)
            m, acc = _mla_update(s_a, mx_a, v_ref[j], m, acc)
            mx_a = score_chunk(j + 3, s_a)
            m, acc = _mla_update(s_b, mx_b, v_ref[j + 1], m, acc)
            mx_b = score_chunk(j + 4, s_b)
            m, acc = _mla_update(s_c, mx_c, v_ref[j + 2], m, acc)
            return m, acc, mx_a, mx_b

        def triples(i, c):
            for u in range(MLA_UNROLL):
                c = triple(3 * (MLA_UNROLL * i + u), c)
            return c

        nt = (nc - 2) // 3
        c = (m, acc, mx_a, mx_b)
        if nt >= MLA_UNROLL:
            c = lax.fori_loop(0, nt // MLA_UNROLL, triples, c)
        for t in range(nt - nt % MLA_UNROLL, nt):
            c = triple(3 * t, c)
        m, acc, mx_a, mx_b = c
        j = 3 * nt
        left = nc - j
        if left >= 3:
            mx_c = score_chunk(j + 2, s_c)
        m, acc = _mla_update(s_a, mx_a, v_ref[j], m, acc)
        if left == 4:
            mx_a = score_chunk(j + 3, s_a)
        m, acc = _mla_update(s_b, mx_b, v_ref[j + 1], m, acc)
        if left >= 3:
            m, acc = _mla_update(s_c, mx_c, v_ref[j + 2], m, acc)
        if left == 4:
            m, acc = _mla_update(s_a, mx_a, v_ref[j + 3], m, acc)
    o_ref[...] = (acc[:MLA_V_DIM] / acc[MLA_V_DIM:MLA_V_DIM + 1]).astype(o_ref.dtype)


def _mla_call(qm, km, vm, tq, sub, q_tile0, n_q, ctx_only):
    bsz, _, s = qm.shape
    nc = vm.shape[1]
    wq = tq * sub
    if ctx_only:
        k_spec = pl.BlockSpec((None, KV_SUB, MLA_HEAD_PAD), lambda b, h, i: (b, s // KV_SUB - 1, h))
        v_spec = pl.BlockSpec((None, 1, V_ROWS, KV_CHUNK), lambda b, h, i: (b, nc - 1, h, 0))
    else:
        k_spec = pl.BlockSpec((None, s, MLA_HEAD_PAD), lambda b, h, i: (b, 0, h))
        v_spec = pl.BlockSpec((None, nc, V_ROWS, KV_CHUNK), lambda b, h, i: (b, 0, h, 0))
    return pl.pallas_call(
        functools.partial(_mla_kernel, nc=nc, ctx_only=ctx_only, tq=tq),
        grid=(bsz, MLA_HEADS, n_q),
        in_specs=[pl.BlockSpec((None, MLA_HEAD_PAD, wq), lambda b, h, i: (b, h, q_tile0 + i)), k_spec, v_spec],
        out_specs=pl.BlockSpec((None, MLA_V_DIM, wq), lambda b, h, i: (b, h, i)),
        out_shape=jax.ShapeDtypeStruct((bsz, MLA_HEADS * MLA_V_DIM, n_q * wq), BF16),
        scratch_shapes=[pltpu.VMEM((KV_CHUNK, tq), F32)] * 3,
        compiler_params=_cparams(("parallel", "arbitrary", "arbitrary"), 40),
        name="mla_attn_ctx" if ctx_only else "mla_attn",
    )(qm, km, vm)


def _mla(qm, km, vm, lat, lc):
    tq = min(MLA_TQ, lat)
    sub = MLA_SUBTILES if lat % (tq * MLA_SUBTILES) == 0 else 1
    assert lc == KV_SUB and lat % lc == 0
    return (_mla_call(qm, km, vm, tq, sub, 0, lat // (tq * sub), False),
            _mla_call(qm, km, vm, lc, 1, lat // lc, 1, True))


def _merge_mlp_kernel(x_ref, mod_ref, hl_refs, hc_refs, sk_ref, yg_ref, yml_ref, ymc_ref, gt_ref, wb_ref, wo_ref,
                      g_ref, w1_ref, w2_ref, o_ref, *, lat_tiles):
    d = D_MODEL
    is_lat = pl.program_id(1) < lat_tiles
    conv, x0, z = [jnp.where(is_lat, a[...], b[...]).astype(F32) for a, b in zip(hl_refs, hc_refs)]
    yh = (x0 * (conv + z * sk_ref[...])).astype(BF16)
    yg = yg_ref[...].astype(F32).T.astype(BF16)
    ym = jnp.where(is_lat, yml_ref[...], ymc_ref[...]).astype(F32).T.astype(BF16)
    merged = (gt_ref[:, 0:d].astype(F32) * _dot(yh, wb_ref[0])
              + gt_ref[:, d:2 * d].astype(F32) * _dot(yg, wb_ref[1])
              + gt_ref[:, 2 * d:].astype(F32) * _dot(ym, wb_ref[2]))
    res = _dot(merged.astype(BF16), wo_ref[...])
    x_mix = x_ref[...] + mod_ref[:, 2 * d:3 * d] * res
    o_ref[...] = _mlp_rows(x_mix, mod_ref, g_ref, w1_ref, w2_ref)


def _merge_mlp(xs, modsel, hy_lat, hy_ctx, skip, yg, ym_lat, ym_ctx, gates, wb, wo, g, w1, w2, lat_tiles, n_tiles):
    bsz, _, d = xs.shape
    tm = ROW_TILE
    c = HY_WIDTH
    row = lambda b, i: (b, i, 0)
    col = lambda b, i: (b, 0, i)
    lat_row = pl.BlockSpec((None, tm, c), lambda b, i: (b, jnp.minimum(i, lat_tiles - 1), 0))
    ctx_row = pl.BlockSpec((None, tm, c), lambda b, i: (b, jnp.maximum(i - lat_tiles, 0), 0))
    return pl.pallas_call(
        functools.partial(_merge_mlp_kernel, lat_tiles=lat_tiles),
        grid=(bsz, n_tiles),
        in_specs=[
            pl.BlockSpec((None, tm, d), row),
            pl.BlockSpec((None, None, 1, 6 * d), lambda b, i: (b, jnp.where(i >= lat_tiles, 1, 0), 0, 0)),
            [lat_row] * 3, [ctx_row] * 3, _const_spec(skip.shape),
            pl.BlockSpec((None, c, tm), col),
            pl.BlockSpec((None, c, tm), lambda b, i: (b, 0, jnp.minimum(i, lat_tiles - 1))),
            pl.BlockSpec((None, c, tm), lambda b, i: (b, 0, jnp.maximum(i - lat_tiles, 0))),
            pl.BlockSpec((None, tm, 3 * d), row),
            _const_spec(wb.shape), _const_spec(wo.shape),
            _const_spec(g.shape), _const_spec(w1.shape), _const_spec(w2.shape),
        ],
        out_specs=pl.BlockSpec((None, tm, d), row),
        out_shape=jax.ShapeDtypeStruct((bsz, n_tiles * tm, d), F32),
        compiler_params=_cparams(("parallel", "arbitrary"), 56),
        name="merge_mlp",
    )(xs, modsel, list(hy_lat), list(hy_ctx), skip, yg, ym_lat, ym_ctx, gates, wb, wo, g, w1, w2)


def _mlp_rows(x, mod_ref, g_ref, w1_ref, w2_ref):
    d = D_MODEL
    xn = x * lax.rsqrt(jnp.mean(x * x, axis=-1, keepdims=True) + EPS) * g_ref[...]
    h = (xn * (1.0 + mod_ref[:, 4 * d:5 * d]) + mod_ref[:, 3 * d:4 * d]).astype(BF16)
    acc = jnp.zeros(x.shape, F32)
    for j in range(D_FF // FF_CHUNK):
        sl = slice(j * FF_CHUNK, (j + 1) * FF_CHUNK)
        a = jnp.maximum(_dot(h, w1_ref[:, sl]), 0.0)
        acc = acc + _dot((a * a).astype(BF16), w2_ref[sl, :])
    return x + mod_ref[:, 5 * d:] * acc


def _rope_tables_t(rows, dim, lc):
    n_freq = dim // 4
    inv = ROPE_BASE ** (-jnp.arange(n_freq, dtype=F32) / n_freq)
    r = jnp.repeat(jnp.arange(rows, dtype=F32), GRID_W)
    col = jnp.tile(jnp.arange(GRID_W, dtype=F32), rows)
    ang = jnp.concatenate([r[:, None] * inv, col[:, None] * inv], axis=-1)
    cos_t = jnp.concatenate([jnp.cos(ang).T, jnp.ones((dim // 2, lc), F32)], axis=1)
    sin_t = jnp.concatenate([jnp.sin(ang).T, jnp.zeros((dim // 2, lc), F32)], axis=1)
    return cos_t, sin_t


def _lane_bcast(v):
    return jnp.broadcast_to(v.astype(F32)[:, None], (v.shape[0], ROW_TILE))


def _layer_weights(l, w_in, gqa_q_norm, gqa_k_norm, mla_q_a_norm, mla_kv_a_norm, w_q_b, w_kv_b, mla_q_norm, mla_k_norm):
    w = w_in[l]
    o = np.cumsum([0, 3 * HY_WIDTH, GQA_HEADS * GQA_HEAD_DIM, GQA_KV_HEADS * GQA_HEAD_DIM,
                   GQA_KV_HEADS * GQA_HEAD_DIM, MLA_Q_RANK, MLA_KV_RANK, MLA_ROPE_DIM, 3 * D_MODEL])
    wu = w[:, o[0]:o[1]].astype(BF16)
    wt = w[:, o[1]:o[7]].T.astype(BF16)
    wg = w[:, o[7]:o[8]].astype(BF16)
    wqb = w_q_b[l].T.astype(BF16)
    wkvb = w_kv_b[l].T.astype(BF16)
    return (wu, wg, wt, wqb, wkvb,
            _lane_bcast(gqa_q_norm[l]), _lane_bcast(gqa_k_norm[l]),
            _lane_bcast(mla_q_a_norm[l]), _lane_bcast(mla_kv_a_norm[l]),
            _lane_bcast(mla_q_norm[l]), _lane_bcast(mla_k_norm[l]))


def kernel(x, c, ctx, c_ctx, w_mod, b_mod, norm_mix_g, norm_mlp_g, w_in, hy_short_w, hy_f1_w, hy_f1_b, hy_f2_w, hy_f2_b, hy_sin_freq, hy_f3_w, hy_skip, gqa_q_norm, gqa_k_norm, gqa_sink, mla_q_a_norm, mla_kv_a_norm, w_q_b, w_kv_b, mla_q_norm, mla_k_norm, w_branch, w_out, w_mlp1, w_mlp2):
    bsz, lat, d = x.shape
    lc = ctx.shape[1]
    depth = w_mod.shape[0]
    s = lat + lc
    assert d == D_MODEL and lat % GRID_W == 0 and lat % ROW_TILE == 0 and lc == ROW_TILE and s % KV_CHUNK == 0
    lat_tiles = lat // ROW_TILE

    pad = (-(bsz + 1)) % 8
    cond = jnp.concatenate([c, c_ctx[None, :], jnp.zeros((pad, d), F32)], axis=0)
    mods = _mod_all(cond, w_mod, b_mod)

    tabs = _rope_tables_t(lat // GRID_W, GQA_HEAD_DIM, lc) + _rope_tables_t(lat // GRID_W, MLA_ROPE_DIM, lc)
    xs = jnp.concatenate([x, ctx], axis=1)

    for l in range(depth):
        ml = mods[l]
        modsel = jnp.stack([ml[:bsz], jnp.broadcast_to(ml[bsz][None], (bsz, 6 * d))], axis=1)[:, :, None, :]
        wts = _layer_weights(l, w_in, gqa_q_norm, gqa_k_norm, mla_q_a_norm, mla_kv_a_norm, w_q_b, w_kv_b,
                             mla_q_norm, mla_k_norm)
        u, gates, qg, kg, vg, qm, km, vm = _proj(xs, modsel, norm_mix_g[l][None, :], wts, tabs, lat_tiles)

        f1w = jnp.zeros((V7X_LANES, HY_FILTER_WIDTH), F32).at[:HY_EMB_DIM].set(hy_f1_w[l])
        fw = (f1w, hy_f1_b[l][None, :], hy_f2_w[l], hy_f2_b[l][None, :], hy_sin_freq[l], hy_f3_w[l])
        hy_lat, hy_ctx = _hyena(u, hy_short_w[l], fw, lat, lc)

        sink_rows = jnp.broadcast_to(gqa_sink[l].astype(F32)[:, None], (GQA_HEADS, GQA_TQ))
        yg = _gqa(qg, kg, vg, sink_rows, lat, lc)
        ym_lat, ym_ctx = _mla(qm, km, vm, lat, lc)

        n_tiles = lat_tiles if l == depth - 1 else s // ROW_TILE
        xs = _merge_mlp(xs, modsel, hy_lat, hy_ctx, hy_skip[l][None, :], yg, ym_lat, ym_ctx, gates,
                        w_branch[l].astype(BF16), w_out[l].astype(BF16), norm_mlp_g[l][None, :],
                        w_mlp1[l].astype(BF16), w_mlp2[l].astype(BF16), lat_tiles, n_tiles)
    return xs
```

```python
import functools
import math

import numpy as np
import jax
import jax.numpy as jnp
from jax import lax
from jax.experimental import pallas as pl
from jax.experimental.pallas import tpu as pltpu

D_MODEL = 1024
GRID_W = 64
HY_WIDTH = 512
HY_EMB_DIM = 33
HY_BANDS = (HY_EMB_DIM - 1) // 2
HY_FILTER_WIDTH = 64
HY_DECAY_TARGET = 1e-2
HY_FAST_DECAY = 0.3
HY_SLOW_DECAY = 1.5
GQA_HEADS = 8
GQA_KV_HEADS = 2
GQA_GROUP = GQA_HEADS // GQA_KV_HEADS
GQA_HEAD_DIM = 64
GQA_SCALE = GQA_HEAD_DIM ** -0.5
WINDOW = 128
MLA_HEADS = 8
MLA_Q_RANK = 384
MLA_KV_RANK = 256
MLA_NOPE_DIM = 64
MLA_ROPE_DIM = 32
MLA_V_DIM = 64
MLA_QK_DIM = MLA_NOPE_DIM + MLA_ROPE_DIM
MLA_SCALE = MLA_QK_DIM ** -0.5
D_FF = 4 * D_MODEL
ROPE_BASE = 10000.0
EPS = 1e-6
LOG2E = 1.4426950408889634
NEG_BIG = -1e30

V7X_LANES = 128
V7X_VMEM_BYTES = 64 * 1024 * 1024

ROW_TILE = 256
PROJ_BATCH = 2
CONV_TILE = 512
MLA_HEAD_PAD = 128
V_ROWS = 80
KV_CHUNK = ROW_TILE
KV_SUB = 256
MLA_NBUF = 3
MLA_UNROLL = 5
MLA_TQ = 1024
MLA_SUBTILES = 2
GQA_TQ = 256
GQA_SUBTILES = 3
FF_CHUNK = 1024
FFT_N2 = 128
FFT_NB = 16
FFT_KB = 8

F32 = jnp.float32
BF16 = jnp.bfloat16


def _cparams(sem, vmem_mb, flags=None):
    return pltpu.CompilerParams(dimension_semantics=sem, vmem_limit_bytes=vmem_mb * 1024 * 1024, flags=flags)


def _dot(a, b):
    return jnp.dot(a, b, preferred_element_type=F32)


def _dot_hi(a, b):
    return jnp.dot(a, b, preferred_element_type=F32, precision=lax.Precision.HIGHEST)


def _ones_row_block(width):
    r = lax.broadcasted_iota(jnp.int32, (V_ROWS - MLA_V_DIM, width), 0)
    return jnp.where(r == 0, 1.0, 0.0).astype(BF16)


def _const_spec(shape):
    nd = len(shape)
    return pl.BlockSpec(shape, lambda *_: (0,) * nd, pipeline_mode=pl.Buffered(1))


def _mod_kernel(c_ref, w_ref, b_ref, o_ref):
    c = c_ref[...]
    s = c * jax.nn.sigmoid(c)
    o_ref[...] = _dot_hi(s, w_ref[...]) + b_ref[...]


def _mod_all(cond, w_mod, b_mod):
    depth, d, n = w_mod.shape
    rows = cond.shape[0]
    tn = 1536
    return pl.pallas_call(
        _mod_kernel,
        grid=(depth, n // tn),
        in_specs=[
            pl.BlockSpec((rows, d), lambda l, j: (0, 0)),
            pl.BlockSpec((None, d, tn), lambda l, j: (l, 0, j)),
            pl.BlockSpec((None, 1, tn), lambda l, j: (l, 0, j)),
        ],
        out_specs=pl.BlockSpec((None, rows, tn), lambda l, j: (l, 0, j)),
        out_shape=jax.ShapeDtypeStruct((depth, rows, n), F32),
        compiler_params=_cparams(("arbitrary", "arbitrary"), 40),
        name="adaln_mod",
    )(cond, w_mod, b_mod.reshape(depth, 1, n))


def _rms_rows(x, n):
    return lax.rsqrt(jnp.sum(x * x, axis=0, keepdims=True) * (1.0 / n) + EPS)


def _rope_rows(x1, x2, cs, sn):
    return x1 * cs - x2 * sn, x1 * sn + x2 * cs


def _proj_kernel(x_ref, mod_ref, *refs):
    consts, outs = refs[:16], refs[16:]
    for n in range(x_ref.shape[0]):
        _proj_one(x_ref.at[n], mod_ref.at[n], *consts, *[o.at[n] for o in outs])


def _proj_one(x_ref, mod_ref, g_ref, wu_ref, wg_ref, wt_ref, wqb_ref, wkvb_ref,
              gq_ref, gk_ref, gqa_ref, gkva_ref, gmq_ref, gmk_ref,
              cg_ref, sg_ref, cm_ref, sm_ref,
              u_ref, gate_ref, qg_ref, kg_ref, vg_ref, qm_ref, km_ref, vm_ref):
    d = D_MODEL
    x = x_ref[...]
    tm = x.shape[0]
    shift = mod_ref[:, 0:d]
    scale = mod_ref[:, d:2 * d]
    xn = x * lax.rsqrt(jnp.mean(x * x, axis=-1, keepdims=True) + EPS) * g_ref[...]
    h = xn * (1.0 + scale) + shift
    hb = h.astype(BF16)

    ht = h.T.astype(BF16)
    t = _dot(wt_ref[...], ht)
    o_q, o_k, o_v = 0, 512, 640
    o_cq, o_ckv, o_kr = 768, 768 + MLA_Q_RANK, 768 + MLA_Q_RANK + MLA_KV_RANK

    u_ref[...] = _dot(hb, wu_ref[...]).astype(BF16)
    gate_ref[...] = jax.nn.sigmoid(_dot(hb, wg_ref[...])).astype(BF16)

    cg, sg = cg_ref[...], sg_ref[...]
    cm, sm = cm_ref[...], sm_ref[...]
    hd, hh = GQA_HEAD_DIM, GQA_HEAD_DIM // 2

    cq = t[o_cq:o_cq + MLA_Q_RANK]
    cqn = (cq * _rms_rows(cq, MLA_Q_RANK) * gqa_ref[...]).astype(BF16)
    qm = _dot(wqb_ref[...], cqn)
    gmq = gmq_ref[...]
    nd, rh = MLA_NOPE_DIM, MLA_ROPE_DIM // 2
    qpad = jnp.zeros((MLA_HEAD_PAD - MLA_QK_DIM, tm), BF16)
    for n in range(MLA_HEADS):
        xh = qm[n * MLA_QK_DIM:(n + 1) * MLA_QK_DIM]
        xh = xh * _rms_rows(xh, MLA_QK_DIM) * gmq * (MLA_SCALE * LOG2E)
        a, b = _rope_rows(xh[nd:nd + rh], xh[nd + rh:], cm, sm)
        base = n * MLA_HEAD_PAD
        qm_ref[base:base + nd, :] = xh[:nd].astype(BF16)
        qm_ref[base + nd:base + nd + rh, :] = a.astype(BF16)
        qm_ref[base + nd + rh:base + MLA_QK_DIM, :] = b.astype(BF16)
        qm_ref[base + MLA_QK_DIM:base + MLA_HEAD_PAD, :] = qpad

    ckv = t[o_ckv:o_ckv + MLA_KV_RANK]
    ckvn = (ckv * _rms_rows(ckv, MLA_KV_RANK) * gkva_ref[...]).astype(BF16)
    kv = _dot(wkvb_ref[...], ckvn)
    kr = t[o_kr:o_kr + MLA_ROPE_DIM]
    kr_ss = jnp.sum(kr * kr, axis=0, keepdims=True)
    gmk = gmk_ref[...]
    kzero = jnp.zeros((MLA_HEAD_PAD - MLA_QK_DIM, tm), F32)
    ones_rows = _ones_row_block(tm)
    parts = []
    for n in range(MLA_HEADS):
        kn = kv[n * 128:n * 128 + nd]
        vm_ref[n * V_ROWS:n * V_ROWS + MLA_V_DIM, :] = kv[n * 128 + nd:(n + 1) * 128].astype(BF16)
        vm_ref[n * V_ROWS + MLA_V_DIM:(n + 1) * V_ROWS, :] = ones_rows
        rs = lax.rsqrt((jnp.sum(kn * kn, axis=0, keepdims=True) + kr_ss) * (1.0 / MLA_QK_DIM) + EPS)
        krn = kr * rs * gmk[nd:]
        a, b = _rope_rows(krn[:rh], krn[rh:], cm, sm)
        parts += [kn * rs * gmk[:nd], a, b, kzero]
    km_ref[...] = jnp.concatenate(parts, axis=0).T.astype(BF16)

    gq = gq_ref[...]
    for n in range(GQA_HEADS):
        xh = t[o_q + n * hd:o_q + (n + 1) * hd]
        xh = xh * _rms_rows(xh, hd) * gq
        a, b = _rope_rows(xh[:hh], xh[hh:], cg, sg)
        qg_ref[n * hd:n * hd + hh, :] = (a * (GQA_SCALE * LOG2E)).astype(BF16)
        qg_ref[n * hd + hh:(n + 1) * hd, :] = (b * (GQA_SCALE * LOG2E)).astype(BF16)

    gk = gk_ref[...]
    zpad = jnp.zeros((V7X_LANES - hd, tm), F32)
    parts = []
    for n in range(GQA_KV_HEADS):
        xh = t[o_k + n * hd:o_k + (n + 1) * hd]
        xh = xh * _rms_rows(xh, hd) * gk
        a, b = _rope_rows(xh[:hh], xh[hh:], cg, sg)
        parts += [a, b, zpad]
    kg_ref[...] = jnp.concatenate(parts, axis=0).T.astype(BF16)
    for n in range(GQA_KV_HEADS):
        vg_ref[n * V_ROWS:n * V_ROWS + hd, :] = t[o_v + n * hd:o_v + (n + 1) * hd].astype(BF16)
        vg_ref[n * V_ROWS + hd:(n + 1) * V_ROWS, :] = ones_rows


def _proj(xs, modsel, g, wts, tabs, lat_tiles):
    bsz, s, d = xs.shape
    tm = ROW_TILE
    nt = s // tm
    ncs = KV_CHUNK // tm
    nb = PROJ_BATCH if bsz % PROJ_BATCH == 0 else 1
    (wu, wg, wt, wqb, wkvb, gq, gk, gqa, gkva, gmq, gmk) = wts
    cg, sg, cm, sm = tabs
    row = lambda b, i: (b, i, 0)
    col = lambda b, i: (b, 0, i)
    tab = lambda b, i: (0, i)
    in_specs = [
        pl.BlockSpec((nb, tm, d), row),
        pl.BlockSpec((nb, None, 1, 6 * d), lambda b, i: (b, jnp.where(i >= lat_tiles, 1, 0), 0, 0)),
        _const_spec(g.shape), _const_spec(wu.shape), _const_spec(wg.shape), _const_spec(wt.shape),
        _const_spec(wqb.shape), _const_spec(wkvb.shape),
        _const_spec(gq.shape), _const_spec(gk.shape), _const_spec(gqa.shape), _const_spec(gkva.shape),
        _const_spec(gmq.shape), _const_spec(gmk.shape),
        pl.BlockSpec((cg.shape[0], tm), tab), pl.BlockSpec((sg.shape[0], tm), tab),
        pl.BlockSpec((cm.shape[0], tm), tab), pl.BlockSpec((sm.shape[0], tm), tab),
    ]
    out_shape = [
        jax.ShapeDtypeStruct((bsz, s, 3 * HY_WIDTH), BF16),
        jax.ShapeDtypeStruct((bsz, s, 3 * d), BF16),
        jax.ShapeDtypeStruct((bsz, GQA_HEADS * GQA_HEAD_DIM, s), BF16),
        jax.ShapeDtypeStruct((bsz, s, GQA_KV_HEADS * V7X_LANES), BF16),
        jax.ShapeDtypeStruct((bsz, GQA_KV_HEADS * V_ROWS, s), BF16),
        jax.ShapeDtypeStruct((bsz, MLA_HEADS * MLA_HEAD_PAD, s), BF16),
        jax.ShapeDtypeStruct((bsz, s, MLA_HEADS * MLA_HEAD_PAD), BF16),
        jax.ShapeDtypeStruct((bsz, s // KV_CHUNK, MLA_HEADS * V_ROWS, KV_CHUNK), BF16),
    ]
    out_specs = [
        pl.BlockSpec((nb, tm, 3 * HY_WIDTH), row),
        pl.BlockSpec((nb, tm, 3 * d), row),
        pl.BlockSpec((nb, GQA_HEADS * GQA_HEAD_DIM, tm), col),
        pl.BlockSpec((nb, tm, GQA_KV_HEADS * V7X_LANES), row),
        pl.BlockSpec((nb, GQA_KV_HEADS * V_ROWS, tm), col),
        pl.BlockSpec((nb, MLA_HEADS * MLA_HEAD_PAD, tm), col),
        pl.BlockSpec((nb, tm, MLA_HEADS * MLA_HEAD_PAD), row),
        pl.BlockSpec((nb, None, MLA_HEADS * V_ROWS, tm), lambda b, i: (b, i // ncs, 0, i % ncs)),
    ]
    return pl.pallas_call(
        _proj_kernel,
        grid=(bsz // nb, nt),
        in_specs=in_specs,
        out_specs=out_specs,
        out_shape=out_shape,
        compiler_params=_cparams(("parallel", "arbitrary"), 56),
        name="in_proj",
    )(xs, modsel, g, wu, wg, wt, wqb, wkvb, gq, gk, gqa, gkva, gmq, gmk, cg, sg, cm, sm)


def _short_conv_kernel(u_ref, up_ref, un_ref, w_ref, x0_ref, z_ref, *, n_tiles):
    i = pl.program_id(1)
    u = u_ref[...].astype(F32)
    tm = u.shape[0]
    prev = jnp.where(i > 0, up_ref[7:8, :].astype(F32), 0.0)
    nxt = jnp.where(i < n_tiles - 1, un_ref[0:1, :].astype(F32), 0.0)
    ridx = lax.broadcasted_iota(jnp.int32, u.shape, 0)
    up = jnp.where(ridx == 0, prev, pltpu.roll(u, 1, axis=0))
    dn = jnp.where(ridx == tm - 1, nxt, pltpu.roll(u, tm - 1, axis=0))
    uc = up * w_ref[0:1, :] + u * w_ref[1:2, :] + dn * w_ref[2:3, :]
    c = HY_WIDTH
    x0_ref[...] = uc[:, :c].astype(BF16)
    z_ref[...] = (uc[:, c:2 * c] * uc[:, 2 * c:]).astype(BF16)


def _short_conv(u, short_w, row0, rows):
    bsz, s, c3 = u.shape
    tm = math.gcd(math.gcd(rows, row0), CONV_TILE) if row0 else math.gcd(rows, CONV_TILE)
    nt = rows // tm
    t0 = row0 // tm
    r8 = tm // 8
    last8 = s // 8 - 1
    return pl.pallas_call(
        functools.partial(_short_conv_kernel, n_tiles=nt),
        grid=(bsz, nt),
        in_specs=[
            pl.BlockSpec((None, tm, c3), lambda b, i: (b, t0 + i, 0)),
            pl.BlockSpec((None, 8, c3), lambda b, i: (b, jnp.maximum((t0 + i) * r8 - 1, 0), 0)),
            pl.BlockSpec((None, 8, c3), lambda b, i: (b, jnp.minimum((t0 + i + 1) * r8, last8), 0)),
            _const_spec(short_w.shape),
        ],
        out_specs=[pl.BlockSpec((None, tm, HY_WIDTH), lambda b, i: (b, i, 0))] * 2,
        out_shape=[jax.ShapeDtypeStruct((bsz, rows, HY_WIDTH), BF16)] * 2,
        compiler_params=_cparams(("parallel", "arbitrary"), 32),
        name="hyena_short_conv",
    )(u, u, u, short_w)


def _filter_kernel(zf_ref, w1_ref, b1_ref, w2_ref, b2_ref, fr_ref, w3_ref, dl_ref, h_ref, ss_ref):
    i = pl.program_id(0)
    zf = zf_ref[...]
    tl = zf.shape[0]
    h = jnp.sin(fr_ref[0:1, :] * (_dot_hi(zf, w1_ref[...]) + b1_ref[...]))
    h = jnp.sin(fr_ref[1:2, :] * (_dot_hi(h, w2_ref[...]) + b2_ref[...]))
    h = _dot_hi(h, w3_ref[...])
    decay = jnp.exp(-zf[:, 0:1] * dl_ref[...])
    c = HY_WIDTH
    hf = h[:, :c] * decay
    ridx = lax.broadcasted_iota(jnp.int32, (tl, c), 0) + i * tl
    hb = jnp.where(ridx == 0, 0.0, h[:, c:] * decay)
    h_ref[0] = hf.astype(BF16)
    h_ref[1] = hb.astype(BF16)
    ss = jnp.sum(hf * hf + hb * hb, axis=0, keepdims=True)

    @pl.when(i == 0)
    def _():
        ss_ref[...] = ss

    @pl.when(i > 0)
    def _():
        ss_ref[...] += ss


def _filter(zfeat, fw, deltas):
    length = zfeat.shape[0]
    tl = min(length, 1024)
    w1, b1, w2, b2, fr, w3 = fw
    return pl.pallas_call(
        _filter_kernel,
        grid=(length // tl,),
        in_specs=[pl.BlockSpec((tl, zfeat.shape[1]), lambda i: (i, 0))]
        + [_const_spec(a.shape) for a in (w1, b1, w2, b2, fr, w3, deltas)],
        out_specs=[pl.BlockSpec((2, tl, HY_WIDTH), lambda i: (0, i, 0)),
                   pl.BlockSpec((1, HY_WIDTH), lambda i: (0, 0))],
        out_shape=[jax.ShapeDtypeStruct((2, length, HY_WIDTH), BF16),
                   jax.ShapeDtypeStruct((1, HY_WIDTH), F32)],
        compiler_params=_cparams(("arbitrary",), 40),
        name="hyena_filter",
    )(zfeat, w1, b1, w2, b2, fr, w3, deltas)


def _fft_a_kernel(g_ref, x_ref, o_ref, *, nb, cw):
    for j in range(nb):
        sl = slice(j * cw, (j + 1) * cw)
        x = jnp.concatenate([x_ref[p, :, sl] for p in range(x_ref.shape[0])], axis=0)
        o_ref[:, sl] = _dot(g_ref[j], x).astype(o_ref.dtype)


def _fft_a(x3d, gmat):
    bx, parts, k1, w = x3d.shape
    n2, two_n1, _ = gmat.shape
    cw = w // n2
    nb = FFT_NB
    return pl.pallas_call(
        functools.partial(_fft_a_kernel, nb=nb, cw=cw),
        grid=(bx, n2 // nb),
        in_specs=[pl.BlockSpec((nb, two_n1, parts * k1), lambda b, j: (j, 0, 0)),
                  pl.BlockSpec((None, parts, k1, nb * cw), lambda b, j: (b, 0, 0, j))],
        out_specs=pl.BlockSpec((None, two_n1, nb * cw), lambda b, j: (b, 0, j)),
        out_shape=jax.ShapeDtypeStruct((bx, two_n1, w), BF16),
        compiler_params=_cparams(("parallel", "arbitrary"), 40),
        name="fft_stage_a",
    )(gmat, x3d)


def _fft_filter_b_kernel(fb_ref, a_ref, ss_ref, kf_ref, *, inv_n):
    n = a_ref.shape[3]
    rs = lax.rsqrt(ss_ref[...] + EPS) * inv_n
    for r in range(a_ref.shape[2]):
        xf = _dot(fb_ref[...], jnp.concatenate([a_ref[0, 0, r], a_ref[0, 1, r]], axis=0))
        xb = _dot(fb_ref[...], jnp.concatenate([a_ref[1, 0, r], a_ref[1, 1, r]], axis=0))
        kf_ref[0, r] = (xf[:n] + xb[:n]) * rs
        kf_ref[1, r] = (xf[n:] - xb[n:]) * rs


def _fft_filter_b(a5, fb, ssq, inv_n):
    _, _, n1, n2, c = a5.shape
    kb = min(FFT_KB, n1)
    return pl.pallas_call(
        functools.partial(_fft_filter_b_kernel, inv_n=inv_n),
        grid=(n1 // kb,),
        in_specs=[_const_spec(fb.shape),
                  pl.BlockSpec((2, 2, kb, n2, c), lambda k: (0, 0, k, 0, 0)),
                  _const_spec(ssq.shape)],
        out_specs=pl.BlockSpec((2, kb, n2, c), lambda k: (0, k, 0, 0)),
        out_shape=jax.ShapeDtypeStruct((2, n1, n2, c), F32),
        compiler_params=_cparams(("arbitrary",), 32),
        name="fft_filter_stage_b",
    )(fb, a5, ssq)


def _fft_b_kernel(fb_ref, fbi_ref, a_ref, kf_ref, o_ref):
    kb, n, cw = a_ref.shape[1:]
    a_all = jnp.concatenate([jnp.concatenate([a_ref[0, r], a_ref[1, r]], axis=0) for r in range(kb)], axis=1)
    x = _dot(fb_ref[...], a_all)
    ys = []
    for r in range(kb):
        xr, xi = x[:n, r * cw:(r + 1) * cw], x[n:, r * cw:(r + 1) * cw]
        kr, ki = kf_ref[0, r], kf_ref[1, r]
        ys.append(jnp.concatenate([xr * kr - xi * ki, xr * ki + xi * kr], axis=0).astype(BF16))
    c = _dot(fbi_ref[...], jnp.concatenate(ys, axis=1))
    for r in range(kb):
        o_ref[0, r] = c[:n, r * cw:(r + 1) * cw].astype(o_ref.dtype)
        o_ref[1, r] = c[n:, r * cw:(r + 1) * cw].astype(o_ref.dtype)


def _fft_b(a5, kf, fb, fbi):
    bsz, _, n1, n2, c = a5.shape
    kb = min(FFT_KB, n1)
    return pl.pallas_call(
        _fft_b_kernel,
        grid=(n1 // kb, bsz),
        in_specs=[_const_spec(fb.shape), _const_spec(fbi.shape),
                  pl.BlockSpec((None, 2, kb, n2, c), lambda k, b: (b, 0, k, 0, 0)),
                  pl.BlockSpec((2, kb, n2, c), lambda k, b: (0, k, 0, 0))],
        out_specs=pl.BlockSpec((None, 2, kb, n2, c), lambda k, b: (b, 0, k, 0, 0)),
        out_shape=jax.ShapeDtypeStruct(a5.shape, BF16),
        compiler_params=_cparams(("arbitrary", "arbitrary"), 32),
        name="fft_stage_b",
    )(fb, fbi, a5, kf)


def _fft_c_kernel(h_ref, c_ref, o_ref, *, nb, cw):
    parts, k1 = o_ref.shape[0], o_ref.shape[1]
    for j in range(nb):
        sl = slice(j * cw, (j + 1) * cw)
        y = _dot(h_ref[j], c_ref[:, sl])
        for p in range(parts):
            o_ref[p, :, sl] = y[p * k1:(p + 1) * k1].astype(o_ref.dtype)


def _fft_c(c2d, hmat, parts):
    bx, two_n1, w = c2d.shape
    n2, rows, _ = hmat.shape
    k1 = rows // parts
    cw = w // n2
    nb = FFT_NB
    return pl.pallas_call(
        functools.partial(_fft_c_kernel, nb=nb, cw=cw),
        grid=(bx, n2 // nb),
        in_specs=[pl.BlockSpec((nb, rows, two_n1), lambda b, j: (j, 0, 0)),
                  pl.BlockSpec((None, two_n1, nb * cw), lambda b, j: (b, 0, j))],
        out_specs=pl.BlockSpec((None, parts, k1, nb * cw), lambda b, j: (b, 0, 0, j)),
        out_shape=jax.ShapeDtypeStruct((bx, parts, k1, w), BF16),
        compiler_params=_cparams(("parallel", "arbitrary"), 40),
        name="fft_stage_c",
    )(hmat, c2d)


def _ctx_conv_kernel(fc_ref, fci_ref, h_ref, ss_ref, z_ref, o_ref):
    n = fc_ref.shape[0] // 2
    fc = fc_ref[...]
    kf = _dot(fc, h_ref[0])
    kb = _dot(fc, h_ref[1])
    rs = lax.rsqrt(ss_ref[...] + EPS)
    kr = (kf[:n] + kb[:n]) * rs
    ki = (kf[n:] - kb[n:]) * rs
    x = _dot(fc, z_ref[...])
    xr, xi = x[:n], x[n:]
    y = jnp.concatenate([xr * kr - xi * ki, xr * ki + xi * kr], axis=0).astype(BF16)
    o_ref[...] = _dot(fci_ref[...], y).astype(o_ref.dtype)


def _ctx_conv(fc, fci, hfb, ssq, z):
    bsz, lc, c = z.shape
    blk = pl.BlockSpec((None, lc, c), lambda b: (b, 0, 0))
    return pl.pallas_call(
        _ctx_conv_kernel,
        grid=(bsz,),
        in_specs=[_const_spec(fc.shape), _const_spec(fci.shape), _const_spec(hfb.shape),
                  _const_spec(ssq.shape), blk],
        out_specs=blk,
        out_shape=jax.ShapeDtypeStruct((bsz, lc, c), BF16),
        compiler_params=_cparams(("arbitrary",), 32),
        name="hyena_ctx_conv",
    )(fc, fci, hfb, ssq, z)


@functools.lru_cache(maxsize=None)
def _fft_tables(length):
    n = 2 * length
    n2 = FFT_N2
    n1 = n // n2
    k1 = length // n2
    kk = np.arange(n1)[:, None]
    g = np.empty((n2, 2 * n1, k1), np.float64)
    g2 = np.empty((n2, 2 * n1, 2 * k1), np.float64)
    h2 = np.empty((n2, 2 * k1, 2 * n1), np.float64)
    nn = np.arange(k1)[None, :]
    for j in range(n2):
        ang = 2.0 * np.pi * (((n2 * nn * kk) % n) + (j * kk) % n) / n
        c_, s_ = np.cos(ang), np.sin(ang)
        g[j] = np.concatenate([c_, -s_], axis=0)
        g2[j] = np.block([[c_, s_], [-s_, c_]])
        h2[j] = np.block([[c_.T, -s_.T], [s_.T, c_.T]])
    a = np.arange(n2)
    ph = 2.0 * np.pi * ((a[:, None] * a[None, :]) % n2) / n2
    c, s = np.cos(ph), np.sin(ph)
    fb = np.block([[c, s], [-s, c]])
    fbi = np.block([[c, -s], [s, c]])
    return (jnp.asarray(g, BF16), jnp.asarray(g2, BF16), jnp.asarray(h2, BF16), jnp.asarray(fb, BF16),
            jnp.asarray(fbi, BF16), n1, k1)


@functools.lru_cache(maxsize=None)
def _dft_tables(length):
    n = 2 * length
    k = np.arange(n)[:, None]
    t = np.arange(length)[None, :]
    ang = 2.0 * np.pi * ((k * t) % n) / n
    fc = np.concatenate([np.cos(ang), -np.sin(ang)], axis=0)
    fci = np.concatenate([np.cos(ang).T, -np.sin(ang).T], axis=1) / n
    return jnp.asarray(fc, BF16), jnp.asarray(fci, BF16)


@functools.lru_cache(maxsize=None)
def _filter_features(length):
    t = np.linspace(0.0, 1.0, length, dtype=np.float32)[:, None]
    w = (2.0 * math.pi * np.arange(length, dtype=np.float32)[:, None] / length).astype(np.float32)
    f = np.linspace(1e-4, HY_BANDS - 1, HY_BANDS, dtype=np.float32)[None, :]
    z = np.concatenate([t, np.cos(f * w), -np.sin(f * w)], axis=-1).astype(np.float32)
    zp = np.zeros((length, V7X_LANES), np.float32)
    zp[:, :HY_EMB_DIM] = z
    return jnp.asarray(zp)


def _hyena_deltas():
    max_decay = math.log(HY_DECAY_TARGET) / HY_FAST_DECAY
    min_decay = math.log(HY_DECAY_TARGET) / HY_SLOW_DECAY
    return jnp.abs(jnp.linspace(min_decay, max_decay, HY_WIDTH, dtype=F32))[None, :]


def _hyena(u, short_w, fw, lat, lc):
    bsz = u.shape[0]
    c = HY_WIDTH
    deltas = _hyena_deltas()
    x0, z = _short_conv(u, short_w, 0, lat)
    gmat, gmat2, hmat2, fb, fbi, n1, k1 = _fft_tables(lat)
    n2 = FFT_N2
    hfb, ssq = _filter(_filter_features(lat), fw, deltas)
    fa = _fft_a(hfb.reshape(2, 1, k1, n2 * c), gmat)
    kf = _fft_filter_b(fa.reshape(2, 2, n1, n2, c), fb, ssq, 1.0 / (2 * lat))
    npair = (bsz + 1) // 2
    zp = z if bsz % 2 == 0 else jnp.concatenate([z, jnp.zeros_like(z[:1])], axis=0)
    za = _fft_a(zp.reshape(npair, 2, k1, n2 * c), gmat2)
    zc = _fft_b(za.reshape(npair, 2, n1, n2, c), kf, fb, fbi)
    conv = _fft_c(zc.reshape(npair, 2 * n1, n2 * c), hmat2, 2).reshape(2 * npair, lat, c)[:bsz]
    x0_c, z_c = _short_conv(u, short_w, lat, lc)
    hfb_c, ssq_c = _filter(_filter_features(lc), fw, deltas)
    fc, fci = _dft_tables(lc)
    conv_c = _ctx_conv(fc, fci, hfb_c, ssq_c, z_c)
    return (conv, x0, z), (conv_c, x0_c, z_c)


def _gqa_kernel(q_ref, k_ref, v_ref, sink_ref, o_ref, *bufs, lat, lc, tq):
    sub = q_ref.shape[1] // tq
    for t in range(sub):
        lanes = slice(t * tq, (t + 1) * tq)
        _gqa_tile(pl.program_id(1) * sub + t, q_ref.at[:, lanes], k_ref, v_ref, sink_ref, o_ref.at[:, lanes], bufs,
                  lat, lc)


def _gqa_tile(i, q_ref, k_ref, v_ref, sink_ref, o_ref, bufs, lat, lc):
    tq = q_ref.shape[1]
    s_len = k_ref.shape[0]
    hd = GQA_HEAD_DIM
    w = WINDOW
    nwin = tq + 2 * w
    start = pl.multiple_of(jnp.clip(i * tq - w, 0, s_len - nwin), w)
    k_all = jnp.concatenate([k_ref[pl.ds(start, nwin), :], k_ref[lat:lat + lc, :]], axis=0)
    v_all = jnp.concatenate([v_ref[:, pl.ds(start, nwin)], v_ref[:, lat:lat + lc]], axis=1)
    q_pos = i * tq + lax.broadcasted_iota(jnp.int32, (1, tq), 1)
    k_pos = start + lax.broadcasted_iota(jnp.int32, (nwin, 1), 0)
    bias = (jnp.where(jnp.abs(k_pos - q_pos) <= w, 0.0, NEG_BIG)
            + jnp.where(k_pos < lat, 0.0, NEG_BIG)
            + jnp.where(q_pos < lat, 0.0, NEG_BIG))
    bias = jnp.concatenate([bias, jnp.zeros((lc, tq), F32)], axis=0)
    bias2 = jnp.concatenate([bias, bias], axis=1)
    zq = jnp.zeros((V7X_LANES - hd, 2 * tq), BF16)

    def score(pair, dst):
        g = pair // (GQA_GROUP // 2)
        r = 2 * pair * hd
        q2 = jnp.concatenate([q_ref[r:r + hd, :], q_ref[r + hd:r + 2 * hd, :]], axis=1)
        s = _dot(k_all[:, g * V7X_LANES:(g + 1) * V7X_LANES], jnp.concatenate([q2, zq], axis=0)) + bias2
        dst[...] = s
        return jnp.max(s, axis=0, keepdims=True)

    def update(pair, s_sc, s_max):
        g = pair // (GQA_GROUP // 2)
        r = 2 * pair * hd
        sink = jnp.concatenate([sink_ref[2 * pair:2 * pair + 1, :], sink_ref[2 * pair + 1:2 * pair + 2, :]],
                               axis=1) * LOG2E
        m = jnp.maximum(s_max, sink)
        p = jnp.exp2(s_sc[...] - m).astype(BF16)
        pv = _dot(v_all[g * V_ROWS:(g + 1) * V_ROWS, :], p)
        o = pv[:hd] / (pv[hd:hd + 1] + jnp.exp2(sink - m))
        o_ref[r:r + hd, :] = o[:, :tq].astype(o_ref.dtype)
        o_ref[r + hd:r + 2 * hd, :] = o[:, tq:].astype(o_ref.dtype)

    npair = GQA_HEADS // 2
    mx = [score(j, bufs[j]) for j in range(npair)]
    for j in range(npair):
        update(j, bufs[j], mx[j])


def _gqa(qg, kg, vg, sink_rows, lat, lc):
    bsz, nq, s = qg.shape
    tq = GQA_TQ
    sub = GQA_SUBTILES if (s // tq) % GQA_SUBTILES == 0 else 1
    wq = tq * sub
    return pl.pallas_call(
        functools.partial(_gqa_kernel, lat=lat, lc=lc, tq=tq),
        grid=(bsz, s // wq),
        in_specs=[
            pl.BlockSpec((None, nq, wq), lambda b, i: (b, 0, i)),
            pl.BlockSpec((None, s, kg.shape[2]), lambda b, i: (b, 0, 0)),
            pl.BlockSpec((None, vg.shape[1], s), lambda b, i: (b, 0, 0)),
            _const_spec(sink_rows.shape),
        ],
        out_specs=pl.BlockSpec((None, nq, wq), lambda b, i: (b, 0, i)),
        out_shape=jax.ShapeDtypeStruct((bsz, nq, s), BF16),
        scratch_shapes=[pltpu.VMEM((tq + 2 * WINDOW + lc, 2 * tq), F32)] * (GQA_HEADS // 2),
        compiler_params=_cparams(("parallel", "arbitrary"), 40),
        name="gqa_window_attn",
    )(qg, kg, vg, sink_rows)


def _mla_update(s_ref, s_max, vt, m, acc):
    m_new = jnp.maximum(m, s_max)
    alpha = jnp.exp2(m - m_new)
    p = jnp.exp2(s_ref[...] - m_new).astype(BF16)
    acc = alpha * acc + _dot(vt, p)
    return m_new, acc


def _mla_kernel(q_ref, k_ref, v_ref, o_ref, *bufs, nc, ctx_only, tq):
    for t in range(q_ref.shape[1] // tq):
        lanes = slice(t * tq, (t + 1) * tq)
        _mla_tile(q_ref[:, lanes], k_ref, v_ref, o_ref.at[:, lanes], bufs, nc, ctx_only)


def _mla_tile(q, k_ref, v_ref, o_ref, bufs, nc, ctx_only):
    tq = q.shape[1]
    ck = KV_CHUNK
    nbuf = len(bufs)
    m = jnp.full((1, tq), NEG_BIG, F32)
    acc = jnp.zeros((V_ROWS, tq), F32)

    def score(k, dst):
        s = _dot(k, q)
        dst[...] = s
        return jnp.max(s, axis=0, keepdims=True)

    def score_chunk(j, dst):
        return score(k_ref[pl.ds(pl.multiple_of(j * ck, ck), ck), :], dst)

    if ctx_only:
        s_x = bufs[0].at[0:KV_SUB, :]
        mx = score(k_ref[k_ref.shape[0] - KV_SUB:, :], s_x)
        m, acc = _mla_update(s_x, mx, v_ref[v_ref.shape[0] - 1][:, ck - KV_SUB:], m, acc)
    else:
        ahead = nbuf - 1
        mx = [None] * nbuf
        for j in range(min(ahead, nc)):
            mx[j] = score_chunk(j, bufs[j])

        def step(j, slot, m, acc, mx, last):
            if not last:
                nxt = (slot + ahead) % nbuf
                mx[nxt] = score_chunk(j + ahead, bufs[nxt])
            return _mla_update(bufs[slot], mx[slot], v_ref[j], m, acc)

        per_body = MLA_UNROLL * nbuf

        def body(g, c):
            m, acc, mx = c[0], c[1], list(c[2:])
            for u in range(per_body):
                m, acc = step(g * per_body + u, u % nbuf, m, acc, mx, False)
            return (m, acc, *mx)

        n_body = max(nc - ahead, 0) // per_body
        if n_body > 0:
            out = lax.fori_loop(0, n_body, body, (m, acc, *[m if x is None else x for x in mx]))
            m, acc, mx = out[0], out[1], list(out[2:])
        for j in range(n_body * per_body, nc):
            m, acc = step(j, j % nbuf, m, acc, mx, j + ahead >= nc)
    o_ref[...] = (acc[:MLA_V_DIM] / acc[MLA_V_DIM:MLA_V_DIM + 1]).astype(o_ref.dtype)


def _mla_call(qm, km, vm, tq, sub, q_tile0, n_q, ctx_only):
    bsz, _, s = qm.shape
    nc = vm.shape[1]
    wq = tq * sub
    if ctx_only:
        k_spec = pl.BlockSpec((None, KV_SUB, MLA_HEAD_PAD), lambda b, h, i: (b, s // KV_SUB - 1, h))
        v_spec = pl.BlockSpec((None, 1, V_ROWS, KV_CHUNK), lambda b, h, i: (b, nc - 1, h, 0))
    else:
        k_spec = pl.BlockSpec((None, s, MLA_HEAD_PAD), lambda b, h, i: (b, 0, h))
        v_spec = pl.BlockSpec((None, nc, V_ROWS, KV_CHUNK), lambda b, h, i: (b, 0, h, 0))
    return pl.pallas_call(
        functools.partial(_mla_kernel, nc=nc, ctx_only=ctx_only, tq=tq),
        grid=(bsz, MLA_HEADS, n_q),
        in_specs=[pl.BlockSpec((None, MLA_HEAD_PAD, wq), lambda b, h, i: (b, h, q_tile0 + i)), k_spec, v_spec],
        out_specs=pl.BlockSpec((None, MLA_V_DIM, wq), lambda b, h, i: (b, h, i)),
        out_shape=jax.ShapeDtypeStruct((bsz, MLA_HEADS * MLA_V_DIM, n_q * wq), BF16),
        scratch_shapes=[pltpu.VMEM((KV_CHUNK, tq), F32)] * MLA_NBUF,
        compiler_params=_cparams(("parallel", "arbitrary", "arbitrary"), 40),
        name="mla_attn_ctx" if ctx_only else "mla_attn",
    )(qm, km, vm)


def _mla(qm, km, vm, lat, lc):
    tq = min(MLA_TQ, lat)
    sub = MLA_SUBTILES if lat % (tq * MLA_SUBTILES) == 0 else 1
    assert lc == KV_SUB and lat % lc == 0
    return (_mla_call(qm, km, vm, tq, sub, 0, lat // (tq * sub), False),
            _mla_call(qm, km, vm, lc, 1, lat // lc, 1, True))


def _merge_mlp_kernel(x_ref, mod_ref, hl_refs, hc_refs, sk_ref, yg_ref, yml_ref, ymc_ref, gt_ref, wb_ref, wo_ref,
                      g_ref, w1_ref, w2_ref, o_ref, *, lat_tiles):
    d = D_MODEL
    is_lat = pl.program_id(1) < lat_tiles
    conv, x0, z = [jnp.where(is_lat, a[...], b[...]).astype(F32) for a, b in zip(hl_refs, hc_refs)]
    yh = (x0 * (conv + z * sk_ref[...])).astype(BF16)
    yg = yg_ref[...].astype(F32).T.astype(BF16)
    ym = jnp.where(is_lat, yml_ref[...], ymc_ref[...]).astype(F32).T.astype(BF16)
    merged = (gt_ref[:, 0:d].astype(F32) * _dot(yh, wb_ref[0])
              + gt_ref[:, d:2 * d].astype(F32) * _dot(yg, wb_ref[1])
              + gt_ref[:, 2 * d:].astype(F32) * _dot(ym, wb_ref[2]))
    res = _dot(merged.astype(BF16), wo_ref[...])
    x_mix = x_ref[...] + mod_ref[:, 2 * d:3 * d] * res
    o_ref[...] = _mlp_rows(x_mix, mod_ref, g_ref, w1_ref, w2_ref)


def _merge_mlp(xs, modsel, hy_lat, hy_ctx, skip, yg, ym_lat, ym_ctx, gates, wb, wo, g, w1, w2, lat_tiles, n_tiles):
    bsz, _, d = xs.shape
    tm = ROW_TILE
    c = HY_WIDTH
    row = lambda b, i: (b, i, 0)
    col = lambda b, i: (b, 0, i)
    lat_row = pl.BlockSpec((None, tm, c), lambda b, i: (b, jnp.minimum(i, lat_tiles - 1), 0))
    ctx_row = pl.BlockSpec((None, tm, c), lambda b, i: (b, jnp.maximum(i - lat_tiles, 0), 0))
    return pl.pallas_call(
        functools.partial(_merge_mlp_kernel, lat_tiles=lat_tiles),
        grid=(bsz, n_tiles),
        in_specs=[
            pl.BlockSpec((None, tm, d), row),
            pl.BlockSpec((None, None, 1, 6 * d), lambda b, i: (b, jnp.where(i >= lat_tiles, 1, 0), 0, 0)),
            [lat_row] * 3, [ctx_row] * 3, _const_spec(skip.shape),
            pl.BlockSpec((None, c, tm), col),
            pl.BlockSpec((None, c, tm), lambda b, i: (b, 0, jnp.minimum(i, lat_tiles - 1))),
            pl.BlockSpec((None, c, tm), lambda b, i: (b, 0, jnp.maximum(i - lat_tiles, 0))),
            pl.BlockSpec((None, tm, 3 * d), row),
            _const_spec(wb.shape), _const_spec(wo.shape),
            _const_spec(g.shape), _const_spec(w1.shape), _const_spec(w2.shape),
        ],
        out_specs=pl.BlockSpec((None, tm, d), row),
        out_shape=jax.ShapeDtypeStruct((bsz, n_tiles * tm, d), F32),
        compiler_params=_cparams(("parallel", "arbitrary"), 56),
        name="merge_mlp",
    )(xs, modsel, list(hy_lat), list(hy_ctx), skip, yg, ym_lat, ym_ctx, gates, wb, wo, g, w1, w2)


def _mlp_rows(x, mod_ref, g_ref, w1_ref, w2_ref):
    d = D_MODEL
    xn = x * lax.rsqrt(jnp.mean(x * x, axis=-1, keepdims=True) + EPS) * g_ref[...]
    h = (xn * (1.0 + mod_ref[:, 4 * d:5 * d]) + mod_ref[:, 3 * d:4 * d]).astype(BF16)
    acc = jnp.zeros(x.shape, F32)
    for j in range(D_FF // FF_CHUNK):
        sl = slice(j * FF_CHUNK, (j + 1) * FF_CHUNK)
        a = jnp.maximum(_dot(h, w1_ref[:, sl]), 0.0)
        acc = acc + _dot((a * a).astype(BF16), w2_ref[sl, :])
    return x + mod_ref[:, 5 * d:] * acc


def _rope_tables_t(rows, dim, lc):
    n_freq = dim // 4
    inv = ROPE_BASE ** (-jnp.arange(n_freq, dtype=F32) / n_freq)
    r = jnp.repeat(jnp.arange(rows, dtype=F32), GRID_W)
    col = jnp.tile(jnp.arange(GRID_W, dtype=F32), rows)
    ang = jnp.concatenate([r[:, None] * inv, col[:, None] * inv], axis=-1)
    cos_t = jnp.concatenate([jnp.cos(ang).T, jnp.ones((dim // 2, lc), F32)], axis=1)
    sin_t = jnp.concatenate([jnp.sin(ang).T, jnp.zeros((dim // 2, lc), F32)], axis=1)
    return cos_t, sin_t


def _lane_bcast(v):
    return jnp.broadcast_to(v.astype(F32)[:, None], (v.shape[0], ROW_TILE))


def _layer_weights(l, w_in, gqa_q_norm, gqa_k_norm, mla_q_a_norm, mla_kv_a_norm, w_q_b, w_kv_b, mla_q_norm, mla_k_norm):
    w = w_in[l]
    o = np.cumsum([0, 3 * HY_WIDTH, GQA_HEADS * GQA_HEAD_DIM, GQA_KV_HEADS * GQA_HEAD_DIM,
                   GQA_KV_HEADS * GQA_HEAD_DIM, MLA_Q_RANK, MLA_KV_RANK, MLA_ROPE_DIM, 3 * D_MODEL])
    wu = w[:, o[0]:o[1]].astype(BF16)
    wt = w[:, o[1]:o[7]].T.astype(BF16)
    wg = w[:, o[7]:o[8]].astype(BF16)
    wqb = w_q_b[l].T.astype(BF16)
    wkvb = w_kv_b[l].T.astype(BF16)
    return (wu, wg, wt, wqb, wkvb,
            _lane_bcast(gqa_q_norm[l]), _lane_bcast(gqa_k_norm[l]),
            _lane_bcast(mla_q_a_norm[l]), _lane_bcast(mla_kv_a_norm[l]),
            _lane_bcast(mla_q_norm[l]), _lane_bcast(mla_k_norm[l]))


def kernel(x, c, ctx, c_ctx, w_mod, b_mod, norm_mix_g, norm_mlp_g, w_in, hy_short_w, hy_f1_w, hy_f1_b, hy_f2_w, hy_f2_b, hy_sin_freq, hy_f3_w, hy_skip, gqa_q_norm, gqa_k_norm, gqa_sink, mla_q_a_norm, mla_kv_a_norm, w_q_b, w_kv_b, mla_q_norm, mla_k_norm, w_branch, w_out, w_mlp1, w_mlp2):
    bsz, lat, d = x.shape
    lc = ctx.shape[1]
    depth = w_mod.shape[0]
    s = lat + lc
    assert d == D_MODEL and lat % GRID_W == 0 and lat % ROW_TILE == 0 and lc == ROW_TILE and s % KV_CHUNK == 0
    lat_tiles = lat // ROW_TILE

    pad = (-(bsz + 1)) % 8
    cond = jnp.concatenate([c, c_ctx[None, :], jnp.zeros((pad, d), F32)], axis=0)
    mods = _mod_all(cond, w_mod, b_mod)

    tabs = _rope_tables_t(lat // GRID_W, GQA_HEAD_DIM, lc) + _rope_tables_t(lat // GRID_W, MLA_ROPE_DIM, lc)
    xs = jnp.concatenate([x, ctx], axis=1)

    for l in range(depth):
        ml = mods[l]
        modsel = jnp.stack([ml[:bsz], jnp.broadcast_to(ml[bsz][None], (bsz, 6 * d))], axis=1)[:, :, None, :]
        wts = _layer_weights(l, w_in, gqa_q_norm, gqa_k_norm, mla_q_a_norm, mla_kv_a_norm, w_q_b, w_kv_b,
                             mla_q_norm, mla_k_norm)
        u, gates, qg, kg, vg, qm, km, vm = _proj(xs, modsel, norm_mix_g[l][None, :], wts, tabs, lat_tiles)

        f1w = jnp.zeros((V7X_LANES, HY_FILTER_WIDTH), F32).at[:HY_EMB_DIM].set(hy_f1_w[l])
        fw = (f1w, hy_f1_b[l][None, :], hy_f2_w[l], hy_f2_b[l][None, :], hy_sin_freq[l], hy_f3_w[l])
        hy_lat, hy_ctx = _hyena(u, hy_short_w[l], fw, lat, lc)

        sink_rows = jnp.broadcast_to(gqa_sink[l].astype(F32)[:, None], (GQA_HEADS, GQA_TQ))
        yg = _gqa(qg, kg, vg, sink_rows, lat, lc)
        ym_lat, ym_ctx = _mla(qm, km, vm, lat, lc)

        n_tiles = lat_tiles if l == depth - 1 else s // ROW_TILE
        xs = _merge_mlp(xs, modsel, hy_lat, hy_ctx, hy_skip[l][None, :], yg, ym_lat, ym_ctx, gates,
                        w_branch[l].astype(BF16), w_out[l].astype(BF16), norm_mlp_g[l][None, :],
                        w_mlp1[l].astype(BF16), w_mlp2[l].astype(BF16), lat_tiles, n_tiles)
    return xs
```

```python
import functools
import math

import numpy as np
import jax
import jax.numpy as jnp
from jax import lax
from jax.experimental import pallas as pl
from jax.experimental.pallas import tpu as pltpu

D_MODEL = 1024
GRID_W = 64
HY_WIDTH = 512
HY_EMB_DIM = 33
HY_BANDS = (HY_EMB_DIM - 1) // 2
HY_FILTER_WIDTH = 64
HY_DECAY_TARGET = 1e-2
HY_FAST_DECAY = 0.3
HY_SLOW_DECAY = 1.5
GQA_HEADS = 8
GQA_KV_HEADS = 2
GQA_GROUP = GQA_HEADS // GQA_KV_HEADS
GQA_HEAD_DIM = 64
GQA_SCALE = GQA_HEAD_DIM ** -0.5
WINDOW = 128
MLA_HEADS = 8
MLA_Q_RANK = 384
MLA_KV_RANK = 256
MLA_NOPE_DIM = 64
MLA_ROPE_DIM = 32
MLA_V_DIM = 64
MLA_QK_DIM = MLA_NOPE_DIM + MLA_ROPE_DIM
MLA_SCALE = MLA_QK_DIM ** -0.5
D_FF = 4 * D_MODEL
ROPE_BASE = 10000.0
EPS = 1e-6
LOG2E = 1.4426950408889634
NEG_BIG = -1e30

V7X_LANES = 128
V7X_VMEM_BYTES = 64 * 1024 * 1024

ROW_TILE = 256
PROJ_BATCH = 2
CONV_TILE = 512
MLA_HEAD_PAD = 128
V_ROWS = 80
KV_CHUNK = ROW_TILE
KV_SUB = 256
MLA_NBUF = 3
MLA_UNROLL = 5
MLA_TQ = 1024
MLA_SUBTILES = 4
GQA_TQ = 256
GQA_SUBTILES = 3
FF_CHUNK = 1024
FFT_N2 = 128
FFT_NB = 16
FFT_KB = 8

F32 = jnp.float32
BF16 = jnp.bfloat16


def _cparams(sem, vmem_mb, flags=None):
    return pltpu.CompilerParams(dimension_semantics=sem, vmem_limit_bytes=vmem_mb * 1024 * 1024, flags=flags)


def _dot(a, b):
    return jnp.dot(a, b, preferred_element_type=F32)


def _dot_hi(a, b):
    return jnp.dot(a, b, preferred_element_type=F32, precision=lax.Precision.HIGHEST)


def _ones_row_block(width):
    r = lax.broadcasted_iota(jnp.int32, (V_ROWS - MLA_V_DIM, width), 0)
    return jnp.where(r == 0, 1.0, 0.0).astype(BF16)


def _const_spec(shape):
    nd = len(shape)
    return pl.BlockSpec(shape, lambda *_: (0,) * nd, pipeline_mode=pl.Buffered(1))


def _mod_kernel(c_ref, w_ref, b_ref, o_ref):
    c = c_ref[...]
    s = c * jax.nn.sigmoid(c)
    o_ref[...] = _dot_hi(s, w_ref[...]) + b_ref[...]


def _mod_all(cond, w_mod, b_mod):
    depth, d, n = w_mod.shape
    rows = cond.shape[0]
    tn = 1536
    return pl.pallas_call(
        _mod_kernel,
        grid=(depth, n // tn),
        in_specs=[
            pl.BlockSpec((rows, d), lambda l, j: (0, 0)),
            pl.BlockSpec((None, d, tn), lambda l, j: (l, 0, j)),
            pl.BlockSpec((None, 1, tn), lambda l, j: (l, 0, j)),
        ],
        out_specs=pl.BlockSpec((None, rows, tn), lambda l, j: (l, 0, j)),
        out_shape=jax.ShapeDtypeStruct((depth, rows, n), F32),
        compiler_params=_cparams(("arbitrary", "arbitrary"), 40),
        name="adaln_mod",
    )(cond, w_mod, b_mod.reshape(depth, 1, n))


def _rms_rows(x, n):
    return lax.rsqrt(jnp.sum(x * x, axis=0, keepdims=True) * (1.0 / n) + EPS)


def _rope_rows(x1, x2, cs, sn):
    return x1 * cs - x2 * sn, x1 * sn + x2 * cs


def _proj_kernel(x_ref, mod_ref, *refs):
    consts, outs = refs[:16], refs[16:]
    for n in range(x_ref.shape[0]):
        _proj_one(x_ref.at[n], mod_ref.at[n], *consts, *[o.at[n] for o in outs])


def _proj_one(x_ref, mod_ref, g_ref, wu_ref, wg_ref, wt_ref, wqb_ref, wkvb_ref,
              gq_ref, gk_ref, gqa_ref, gkva_ref, gmq_ref, gmk_ref,
              cg_ref, sg_ref, cm_ref, sm_ref,
              u_ref, gate_ref, qg_ref, kg_ref, vg_ref, qm_ref, km_ref, vm_ref):
    d = D_MODEL
    x = x_ref[...]
    tm = x.shape[0]
    shift = mod_ref[:, 0:d]
    scale = mod_ref[:, d:2 * d]
    xn = x * lax.rsqrt(jnp.mean(x * x, axis=-1, keepdims=True) + EPS) * g_ref[...]
    h = xn * (1.0 + scale) + shift
    hb = h.astype(BF16)

    ht = h.T.astype(BF16)
    t = _dot(wt_ref[...], ht)
    o_q, o_k, o_v = 0, 512, 640
    o_cq, o_ckv, o_kr = 768, 768 + MLA_Q_RANK, 768 + MLA_Q_RANK + MLA_KV_RANK

    u_ref[...] = _dot(hb, wu_ref[...]).astype(BF16)

    def gate_cols(n):
        cols = slice(n * d, (n + 1) * d)
        gate_ref[:, cols] = jax.nn.sigmoid(_dot(hb, wg_ref[:, cols])).astype(BF16)

    gate_cols(0)
    cg, sg = cg_ref[...], sg_ref[...]
    cm, sm = cm_ref[...], sm_ref[...]
    hd, hh = GQA_HEAD_DIM, GQA_HEAD_DIM // 2

    cq = t[o_cq:o_cq + MLA_Q_RANK]
    cqn = (cq * _rms_rows(cq, MLA_Q_RANK) * gqa_ref[...]).astype(BF16)
    qm = _dot(wqb_ref[...], cqn)
    gmq = gmq_ref[...]
    nd, rh = MLA_NOPE_DIM, MLA_ROPE_DIM // 2
    qpad = jnp.zeros((MLA_HEAD_PAD - MLA_QK_DIM, tm), BF16)
    for n in range(MLA_HEADS):
        xh = qm[n * MLA_QK_DIM:(n + 1) * MLA_QK_DIM]
        xh = xh * _rms_rows(xh, MLA_QK_DIM) * gmq * (MLA_SCALE * LOG2E)
        a, b = _rope_rows(xh[nd:nd + rh], xh[nd + rh:], cm, sm)
        base = n * MLA_HEAD_PAD
        qm_ref[base:base + nd, :] = xh[:nd].astype(BF16)
        qm_ref[base + nd:base + nd + rh, :] = a.astype(BF16)
        qm_ref[base + nd + rh:base + MLA_QK_DIM, :] = b.astype(BF16)
        qm_ref[base + MLA_QK_DIM:base + MLA_HEAD_PAD, :] = qpad

    gate_cols(1)
    ckv = t[o_ckv:o_ckv + MLA_KV_RANK]
    ckvn = (ckv * _rms_rows(ckv, MLA_KV_RANK) * gkva_ref[...]).astype(BF16)
    kv = _dot(wkvb_ref[...], ckvn)
    kr = t[o_kr:o_kr + MLA_ROPE_DIM]
    kr_ss = jnp.sum(kr * kr, axis=0, keepdims=True)
    gmk = gmk_ref[...]
    kzero = jnp.zeros((MLA_HEAD_PAD - MLA_QK_DIM, tm), F32)
    ones_rows = _ones_row_block(tm)
    parts = []
    for n in range(MLA_HEADS):
        kn = kv[n * 128:n * 128 + nd]
        vm_ref[n * V_ROWS:n * V_ROWS + MLA_V_DIM, :] = kv[n * 128 + nd:(n + 1) * 128].astype(BF16)
        vm_ref[n * V_ROWS + MLA_V_DIM:(n + 1) * V_ROWS, :] = ones_rows
        rs = lax.rsqrt((jnp.sum(kn * kn, axis=0, keepdims=True) + kr_ss) * (1.0 / MLA_QK_DIM) + EPS)
        krn = kr * rs * gmk[nd:]
        a, b = _rope_rows(krn[:rh], krn[rh:], cm, sm)
        parts += [kn * rs * gmk[:nd], a, b, kzero]
    km_ref[...] = jnp.concatenate(parts, axis=0).T.astype(BF16)

    gq = gq_ref[...]
    for n in range(GQA_HEADS):
        xh = t[o_q + n * hd:o_q + (n + 1) * hd]
        xh = xh * _rms_rows(xh, hd) * gq
        a, b = _rope_rows(xh[:hh], xh[hh:], cg, sg)
        qg_ref[n * hd:n * hd + hh, :] = (a * (GQA_SCALE * LOG2E)).astype(BF16)
        qg_ref[n * hd + hh:(n + 1) * hd, :] = (b * (GQA_SCALE * LOG2E)).astype(BF16)

    gate_cols(2)
    gk = gk_ref[...]
    zpad = jnp.zeros((V7X_LANES - hd, tm), F32)
    parts = []
    for n in range(GQA_KV_HEADS):
        xh = t[o_k + n * hd:o_k + (n + 1) * hd]
        xh = xh * _rms_rows(xh, hd) * gk
        a, b = _rope_rows(xh[:hh], xh[hh:], cg, sg)
        parts += [a, b, zpad]
    kg_ref[...] = jnp.concatenate(parts, axis=0).T.astype(BF16)
    for n in range(GQA_KV_HEADS):
        vg_ref[n * V_ROWS:n * V_ROWS + hd, :] = t[o_v + n * hd:o_v + (n + 1) * hd].astype(BF16)
        vg_ref[n * V_ROWS + hd:(n + 1) * V_ROWS, :] = ones_rows


def _proj(xs, modsel, g, wts, tabs, lat_tiles):
    bsz, s, d = xs.shape
    tm = ROW_TILE
    nt = s // tm
    ncs = KV_CHUNK // tm
    nb = PROJ_BATCH if bsz % PROJ_BATCH == 0 else 1
    (wu, wg, wt, wqb, wkvb, gq, gk, gqa, gkva, gmq, gmk) = wts
    cg, sg, cm, sm = tabs
    row = lambda b, i: (b, i, 0)
    col = lambda b, i: (b, 0, i)
    tab = lambda b, i: (0, i)
    in_specs = [
        pl.BlockSpec((nb, tm, d), row),
        pl.BlockSpec((nb, None, 1, 6 * d), lambda b, i: (b, jnp.where(i >= lat_tiles, 1, 0), 0, 0)),
        _const_spec(g.shape), _const_spec(wu.shape), _const_spec(wg.shape), _const_spec(wt.shape),
        _const_spec(wqb.shape), _const_spec(wkvb.shape),
        _const_spec(gq.shape), _const_spec(gk.shape), _const_spec(gqa.shape), _const_spec(gkva.shape),
        _const_spec(gmq.shape), _const_spec(gmk.shape),
        pl.BlockSpec((cg.shape[0], tm), tab), pl.BlockSpec((sg.shape[0], tm), tab),
        pl.BlockSpec((cm.shape[0], tm), tab), pl.BlockSpec((sm.shape[0], tm), tab),
    ]
    out_shape = [
        jax.ShapeDtypeStruct((bsz, s, 3 * HY_WIDTH), BF16),
        jax.ShapeDtypeStruct((bsz, s, 3 * d), BF16),
        jax.ShapeDtypeStruct((bsz, GQA_HEADS * GQA_HEAD_DIM, s), BF16),
        jax.ShapeDtypeStruct((bsz, s, GQA_KV_HEADS * V7X_LANES), BF16),
        jax.ShapeDtypeStruct((bsz, GQA_KV_HEADS * V_ROWS, s), BF16),
        jax.ShapeDtypeStruct((bsz, MLA_HEADS * MLA_HEAD_PAD, s), BF16),
        jax.ShapeDtypeStruct((bsz, s, MLA_HEADS * MLA_HEAD_PAD), BF16),
        jax.ShapeDtypeStruct((bsz, s // KV_CHUNK, MLA_HEADS * V_ROWS, KV_CHUNK), BF16),
    ]
    out_specs = [
        pl.BlockSpec((nb, tm, 3 * HY_WIDTH), row),
        pl.BlockSpec((nb, tm, 3 * d), row),
        pl.BlockSpec((nb, GQA_HEADS * GQA_HEAD_DIM, tm), col),
        pl.BlockSpec((nb, tm, GQA_KV_HEADS * V7X_LANES), row),
        pl.BlockSpec((nb, GQA_KV_HEADS * V_ROWS, tm), col),
        pl.BlockSpec((nb, MLA_HEADS * MLA_HEAD_PAD, tm), col),
        pl.BlockSpec((nb, tm, MLA_HEADS * MLA_HEAD_PAD), row),
        pl.BlockSpec((nb, None, MLA_HEADS * V_ROWS, tm), lambda b, i: (b, i // ncs, 0, i % ncs)),
    ]
    return pl.pallas_call(
        _proj_kernel,
        grid=(bsz // nb, nt),
        in_specs=in_specs,
        out_specs=out_specs,
        out_shape=out_shape,
        compiler_params=_cparams(("parallel", "arbitrary"), 56),
        name="in_proj",
    )(xs, modsel, g, wu, wg, wt, wqb, wkvb, gq, gk, gqa, gkva, gmq, gmk, cg, sg, cm, sm)


def _short_conv_kernel(u_ref, up_ref, un_ref, w_ref, x0_ref, z_ref, *, n_tiles):
    i = pl.program_id(1)
    u = u_ref[...].astype(F32)
    tm = u.shape[0]
    prev = jnp.where(i > 0, up_ref[7:8, :].astype(F32), 0.0)
    nxt = jnp.where(i < n_tiles - 1, un_ref[0:1, :].astype(F32), 0.0)
    ridx = lax.broadcasted_iota(jnp.int32, u.shape, 0)
    up = jnp.where(ridx == 0, prev, pltpu.roll(u, 1, axis=0))
    dn = jnp.where(ridx == tm - 1, nxt, pltpu.roll(u, tm - 1, axis=0))
    uc = up * w_ref[0:1, :] + u * w_ref[1:2, :] + dn * w_ref[2:3, :]
    c = HY_WIDTH
    x0_ref[...] = uc[:, :c].astype(BF16)
    z_ref[...] = (uc[:, c:2 * c] * uc[:, 2 * c:]).astype(BF16)


def _short_conv(u, short_w, row0, rows):
    bsz, s, c3 = u.shape
    tm = math.gcd(math.gcd(rows, row0), CONV_TILE) if row0 else math.gcd(rows, CONV_TILE)
    nt = rows // tm
    t0 = row0 // tm
    r8 = tm // 8
    last8 = s // 8 - 1
    return pl.pallas_call(
        functools.partial(_short_conv_kernel, n_tiles=nt),
        grid=(bsz, nt),
        in_specs=[
            pl.BlockSpec((None, tm, c3), lambda b, i: (b, t0 + i, 0)),
            pl.BlockSpec((None, 8, c3), lambda b, i: (b, jnp.maximum((t0 + i) * r8 - 1, 0), 0)),
            pl.BlockSpec((None, 8, c3), lambda b, i: (b, jnp.minimum((t0 + i + 1) * r8, last8), 0)),
            _const_spec(short_w.shape),
        ],
        out_specs=[pl.BlockSpec((None, tm, HY_WIDTH), lambda b, i: (b, i, 0))] * 2,
        out_shape=[jax.ShapeDtypeStruct((bsz, rows, HY_WIDTH), BF16)] * 2,
        compiler_params=_cparams(("parallel", "arbitrary"), 32),
        name="hyena_short_conv",
    )(u, u, u, short_w)


def _filter_kernel(zf_ref, w1_ref, b1_ref, w2_ref, b2_ref, fr_ref, w3_ref, dl_ref, h_ref, ss_ref):
    i = pl.program_id(0)
    zf = zf_ref[...]
    tl = zf.shape[0]
    h = jnp.sin(fr_ref[0:1, :] * (_dot_hi(zf, w1_ref[...]) + b1_ref[...]))
    h = jnp.sin(fr_ref[1:2, :] * (_dot_hi(h, w2_ref[...]) + b2_ref[...]))
    h = _dot_hi(h, w3_ref[...])
    decay = jnp.exp(-zf[:, 0:1] * dl_ref[...])
    c = HY_WIDTH
    hf = h[:, :c] * decay
    ridx = lax.broadcasted_iota(jnp.int32, (tl, c), 0) + i * tl
    hb = jnp.where(ridx == 0, 0.0, h[:, c:] * decay)
    h_ref[0] = hf.astype(BF16)
    h_ref[1] = hb.astype(BF16)
    ss = jnp.sum(hf * hf + hb * hb, axis=0, keepdims=True)

    @pl.when(i == 0)
    def _():
        ss_ref[...] = ss

    @pl.when(i > 0)
    def _():
        ss_ref[...] += ss


def _filter(zfeat, fw, deltas):
    length = zfeat.shape[0]
    tl = min(length, 1024)
    w1, b1, w2, b2, fr, w3 = fw
    return pl.pallas_call(
        _filter_kernel,
        grid=(length // tl,),
        in_specs=[pl.BlockSpec((tl, zfeat.shape[1]), lambda i: (i, 0))]
        + [_const_spec(a.shape) for a in (w1, b1, w2, b2, fr, w3, deltas)],
        out_specs=[pl.BlockSpec((2, tl, HY_WIDTH), lambda i: (0, i, 0)),
                   pl.BlockSpec((1, HY_WIDTH), lambda i: (0, 0))],
        out_shape=[jax.ShapeDtypeStruct((2, length, HY_WIDTH), BF16),
                   jax.ShapeDtypeStruct((1, HY_WIDTH), F32)],
        compiler_params=_cparams(("arbitrary",), 40),
        name="hyena_filter",
    )(zfeat, w1, b1, w2, b2, fr, w3, deltas)


def _fft_a_kernel(g_ref, x_ref, o_ref, *, nb, cw):
    for j in range(nb):
        sl = slice(j * cw, (j + 1) * cw)
        x = jnp.concatenate([x_ref[p, :, sl] for p in range(x_ref.shape[0])], axis=0)
        o_ref[:, sl] = _dot(g_ref[j], x).astype(o_ref.dtype)


def _fft_a(x3d, gmat):
    bx, parts, k1, w = x3d.shape
    n2, two_n1, _ = gmat.shape
    cw = w // n2
    nb = FFT_NB
    return pl.pallas_call(
        functools.partial(_fft_a_kernel, nb=nb, cw=cw),
        grid=(bx, n2 // nb),
        in_specs=[pl.BlockSpec((nb, two_n1, parts * k1), lambda b, j: (j, 0, 0)),
                  pl.BlockSpec((None, parts, k1, nb * cw), lambda b, j: (b, 0, 0, j))],
        out_specs=pl.BlockSpec((None, two_n1, nb * cw), lambda b, j: (b, 0, j)),
        out_shape=jax.ShapeDtypeStruct((bx, two_n1, w), BF16),
        compiler_params=_cparams(("parallel", "arbitrary"), 40),
        name="fft_stage_a",
    )(gmat, x3d)


def _fft_filter_b_kernel(fb_ref, a_ref, ss_ref, kf_ref, *, inv_n):
    n = a_ref.shape[3]
    rs = lax.rsqrt(ss_ref[...] + EPS) * inv_n
    for r in range(a_ref.shape[2]):
        xf = _dot(fb_ref[...], jnp.concatenate([a_ref[0, 0, r], a_ref[0, 1, r]], axis=0))
        xb = _dot(fb_ref[...], jnp.concatenate([a_ref[1, 0, r], a_ref[1, 1, r]], axis=0))
        kf_ref[0, r] = (xf[:n] + xb[:n]) * rs
        kf_ref[1, r] = (xf[n:] - xb[n:]) * rs


def _fft_filter_b(a5, fb, ssq, inv_n):
    _, _, n1, n2, c = a5.shape
    kb = min(FFT_KB, n1)
    return pl.pallas_call(
        functools.partial(_fft_filter_b_kernel, inv_n=inv_n),
        grid=(n1 // kb,),
        in_specs=[_const_spec(fb.shape),
                  pl.BlockSpec((2, 2, kb, n2, c), lambda k: (0, 0, k, 0, 0)),
                  _const_spec(ssq.shape)],
        out_specs=pl.BlockSpec((2, kb, n2, c), lambda k: (0, k, 0, 0)),
        out_shape=jax.ShapeDtypeStruct((2, n1, n2, c), F32),
        compiler_params=_cparams(("arbitrary",), 32),
        name="fft_filter_stage_b",
    )(fb, a5, ssq)


def _fft_b_kernel(fb_ref, fbi_ref, a_ref, kf_ref, o_ref):
    kb, n, cw = a_ref.shape[1:]
    a_all = jnp.concatenate([jnp.concatenate([a_ref[0, r], a_ref[1, r]], axis=0) for r in range(kb)], axis=1)
    x = _dot(fb_ref[...], a_all)
    ys = []
    for r in range(kb):
        xr, xi = x[:n, r * cw:(r + 1) * cw], x[n:, r * cw:(r + 1) * cw]
        kr, ki = kf_ref[0, r], kf_ref[1, r]
        ys.append(jnp.concatenate([xr * kr - xi * ki, xr * ki + xi * kr], axis=0).astype(BF16))
    c = _dot(fbi_ref[...], jnp.concatenate(ys, axis=1))
    for r in range(kb):
        o_ref[0, r] = c[:n, r * cw:(r + 1) * cw].astype(o_ref.dtype)
        o_ref[1, r] = c[n:, r * cw:(r + 1) * cw].astype(o_ref.dtype)


def _fft_b(a5, kf, fb, fbi):
    bsz, _, n1, n2, c = a5.shape
    kb = min(FFT_KB, n1)
    return pl.pallas_call(
        _fft_b_kernel,
        grid=(n1 // kb, bsz),
        in_specs=[_const_spec(fb.shape), _const_spec(fbi.shape),
                  pl.BlockSpec((None, 2, kb, n2, c), lambda k, b: (b, 0, k, 0, 0)),
                  pl.BlockSpec((2, kb, n2, c), lambda k, b: (0, k, 0, 0))],
        out_specs=pl.BlockSpec((None, 2, kb, n2, c), lambda k, b: (b, 0, k, 0, 0)),
        out_shape=jax.ShapeDtypeStruct(a5.shape, BF16),
        compiler_params=_cparams(("arbitrary", "arbitrary"), 32),
        name="fft_stage_b",
    )(fb, fbi, a5, kf)


def _fft_c_kernel(h_ref, c_ref, o_ref, *, nb, cw):
    parts, k1 = o_ref.shape[0], o_ref.shape[1]
    for j in range(nb):
        sl = slice(j * cw, (j + 1) * cw)
        y = _dot(h_ref[j], c_ref[:, sl])
        for p in range(parts):
            o_ref[p, :, sl] = y[p * k1:(p + 1) * k1].astype(o_ref.dtype)


def _fft_c(c2d, hmat, parts):
    bx, two_n1, w = c2d.shape
    n2, rows, _ = hmat.shape
    k1 = rows // parts
    cw = w // n2
    nb = FFT_NB
    return pl.pallas_call(
        functools.partial(_fft_c_kernel, nb=nb, cw=cw),
        grid=(bx, n2 // nb),
        in_specs=[pl.BlockSpec((nb, rows, two_n1), lambda b, j: (j, 0, 0)),
                  pl.BlockSpec((None, two_n1, nb * cw), lambda b, j: (b, 0, j))],
        out_specs=pl.BlockSpec((None, parts, k1, nb * cw), lambda b, j: (b, 0, 0, j)),
        out_shape=jax.ShapeDtypeStruct((bx, parts, k1, w), BF16),
        compiler_params=_cparams(("parallel", "arbitrary"), 40),
        name="fft_stage_c",
    )(hmat, c2d)


def _ctx_conv_kernel(fc_ref, fci_ref, h_ref, ss_ref, z_ref, o_ref):
    n = fc_ref.shape[0] // 2
    fc = fc_ref[...]
    kf = _dot(fc, h_ref[0])
    kb = _dot(fc, h_ref[1])
    rs = lax.rsqrt(ss_ref[...] + EPS)
    kr = (kf[:n] + kb[:n]) * rs
    ki = (kf[n:] - kb[n:]) * rs
    x = _dot(fc, z_ref[...])
    xr, xi = x[:n], x[n:]
    y = jnp.concatenate([xr * kr - xi * ki, xr * ki + xi * kr], axis=0).astype(BF16)
    o_ref[...] = _dot(fci_ref[...], y).astype(o_ref.dtype)


def _ctx_conv(fc, fci, hfb, ssq, z):
    bsz, lc, c = z.shape
    blk = pl.BlockSpec((None, lc, c), lambda b: (b, 0, 0))
    return pl.pallas_call(
        _ctx_conv_kernel,
        grid=(bsz,),
        in_specs=[_const_spec(fc.shape), _const_spec(fci.shape), _const_spec(hfb.shape),
                  _const_spec(ssq.shape), blk],
        out_specs=blk,
        out_shape=jax.ShapeDtypeStruct((bsz, lc, c), BF16),
        compiler_params=_cparams(("arbitrary",), 32),
        name="hyena_ctx_conv",
    )(fc, fci, hfb, ssq, z)


@functools.lru_cache(maxsize=None)
def _fft_tables(length):
    n = 2 * length
    n2 = FFT_N2
    n1 = n // n2
    k1 = length // n2
    kk = np.arange(n1)[:, None]
    g = np.empty((n2, 2 * n1, k1), np.float64)
    g2 = np.empty((n2, 2 * n1, 2 * k1), np.float64)
    h2 = np.empty((n2, 2 * k1, 2 * n1), np.float64)
    nn = np.arange(k1)[None, :]
    for j in range(n2):
        ang = 2.0 * np.pi * (((n2 * nn * kk) % n) + (j * kk) % n) / n
        c_, s_ = np.cos(ang), np.sin(ang)
        g[j] = np.concatenate([c_, -s_], axis=0)
        g2[j] = np.block([[c_, s_], [-s_, c_]])
        h2[j] = np.block([[c_.T, -s_.T], [s_.T, c_.T]])
    a = np.arange(n2)
    ph = 2.0 * np.pi * ((a[:, None] * a[None, :]) % n2) / n2
    c, s = np.cos(ph), np.sin(ph)
    fb = np.block([[c, s], [-s, c]])
    fbi = np.block([[c, -s], [s, c]])
    return (jnp.asarray(g, BF16), jnp.asarray(g2, BF16), jnp.asarray(h2, BF16), jnp.asarray(fb, BF16),
            jnp.asarray(fbi, BF16), n1, k1)


@functools.lru_cache(maxsize=None)
def _dft_tables(length):
    n = 2 * length
    k = np.arange(n)[:, None]
    t = np.arange(length)[None, :]
    ang = 2.0 * np.pi * ((k * t) % n) / n
    fc = np.concatenate([np.cos(ang), -np.sin(ang)], axis=0)
    fci = np.concatenate([np.cos(ang).T, -np.sin(ang).T], axis=1) / n
    return jnp.asarray(fc, BF16), jnp.asarray(fci, BF16)


@functools.lru_cache(maxsize=None)
def _filter_features(length):
    t = np.linspace(0.0, 1.0, length, dtype=np.float32)[:, None]
    w = (2.0 * math.pi * np.arange(length, dtype=np.float32)[:, None] / length).astype(np.float32)
    f = np.linspace(1e-4, HY_BANDS - 1, HY_BANDS, dtype=np.float32)[None, :]
    z = np.concatenate([t, np.cos(f * w), -np.sin(f * w)], axis=-1).astype(np.float32)
    zp = np.zeros((length, V7X_LANES), np.float32)
    zp[:, :HY_EMB_DIM] = z
    return jnp.asarray(zp)


def _hyena_deltas():
    max_decay = math.log(HY_DECAY_TARGET) / HY_FAST_DECAY
    min_decay = math.log(HY_DECAY_TARGET) / HY_SLOW_DECAY
    return jnp.abs(jnp.linspace(min_decay, max_decay, HY_WIDTH, dtype=F32))[None, :]


def _hyena(u, short_w, fw, lat, lc):
    bsz = u.shape[0]
    c = HY_WIDTH
    deltas = _hyena_deltas()
    x0, z = _short_conv(u, short_w, 0, lat)
    gmat, gmat2, hmat2, fb, fbi, n1, k1 = _fft_tables(lat)
    n2 = FFT_N2
    hfb, ssq = _filter(_filter_features(lat), fw, deltas)
    fa = _fft_a(hfb.reshape(2, 1, k1, n2 * c), gmat)
    kf = _fft_filter_b(fa.reshape(2, 2, n1, n2, c), fb, ssq, 1.0 / (2 * lat))
    npair = (bsz + 1) // 2
    zp = z if bsz % 2 == 0 else jnp.concatenate([z, jnp.zeros_like(z[:1])], axis=0)
    za = _fft_a(zp.reshape(npair, 2, k1, n2 * c), gmat2)
    zc = _fft_b(za.reshape(npair, 2, n1, n2, c), kf, fb, fbi)
    conv = _fft_c(zc.reshape(npair, 2 * n1, n2 * c), hmat2, 2).reshape(2 * npair, lat, c)[:bsz]
    x0_c, z_c = _short_conv(u, short_w, lat, lc)
    hfb_c, ssq_c = _filter(_filter_features(lc), fw, deltas)
    fc, fci = _dft_tables(lc)
    conv_c = _ctx_conv(fc, fci, hfb_c, ssq_c, z_c)
    return (conv, x0, z), (conv_c, x0_c, z_c)


def _gqa_kernel(q_ref, k_ref, v_ref, sink_ref, o_ref, *bufs, lat, lc, tq):
    sub = q_ref.shape[1] // tq
    for t in range(sub):
        lanes = slice(t * tq, (t + 1) * tq)
        _gqa_tile(pl.program_id(1) * sub + t, q_ref.at[:, lanes], k_ref, v_ref, sink_ref, o_ref.at[:, lanes], bufs,
                  lat, lc)


def _gqa_tile(i, q_ref, k_ref, v_ref, sink_ref, o_ref, bufs, lat, lc):
    tq = q_ref.shape[1]
    s_len = k_ref.shape[0]
    hd = GQA_HEAD_DIM
    w = WINDOW
    nwin = tq + 2 * w
    start = pl.multiple_of(jnp.clip(i * tq - w, 0, s_len - nwin), w)
    k_all = jnp.concatenate([k_ref[pl.ds(start, nwin), :], k_ref[lat:lat + lc, :]], axis=0)
    v_all = jnp.concatenate([v_ref[:, pl.ds(start, nwin)], v_ref[:, lat:lat + lc]], axis=1)
    q_pos = i * tq + lax.broadcasted_iota(jnp.int32, (1, tq), 1)
    k_pos = start + lax.broadcasted_iota(jnp.int32, (nwin, 1), 0)
    bias = (jnp.where(jnp.abs(k_pos - q_pos) <= w, 0.0, NEG_BIG)
            + jnp.where(k_pos < lat, 0.0, NEG_BIG)
            + jnp.where(q_pos < lat, 0.0, NEG_BIG))
    bias = jnp.concatenate([bias, jnp.zeros((lc, tq), F32)], axis=0)
    bias2 = jnp.concatenate([bias, bias], axis=1)
    zq = jnp.zeros((V7X_LANES - hd, 2 * tq), BF16)

    def score(pair, dst):
        g = pair // (GQA_GROUP // 2)
        r = 2 * pair * hd
        q2 = jnp.concatenate([q_ref[r:r + hd, :], q_ref[r + hd:r + 2 * hd, :]], axis=1)
        s = _dot(k_all[:, g * V7X_LANES:(g + 1) * V7X_LANES], jnp.concatenate([q2, zq], axis=0)) + bias2
        dst[...] = s
        return jnp.max(s, axis=0, keepdims=True)

    def update(pair, s_sc, s_max):
        g = pair // (GQA_GROUP // 2)
        r = 2 * pair * hd
        sink = jnp.concatenate([sink_ref[2 * pair:2 * pair + 1, :], sink_ref[2 * pair + 1:2 * pair + 2, :]],
                               axis=1) * LOG2E
        m = jnp.maximum(s_max, sink)
        p = jnp.exp2(s_sc[...] - m).astype(BF16)
        pv = _dot(v_all[g * V_ROWS:(g + 1) * V_ROWS, :], p)
        o = pv[:hd] / (pv[hd:hd + 1] + jnp.exp2(sink - m))
        o_ref[r:r + hd, :] = o[:, :tq].astype(o_ref.dtype)
        o_ref[r + hd:r + 2 * hd, :] = o[:, tq:].astype(o_ref.dtype)

    npair = GQA_HEADS // 2
    mx = [score(j, bufs[j]) for j in range(npair)]
    for j in range(npair):
        update(j, bufs[j], mx[j])


def _gqa(qg, kg, vg, sink_rows, lat, lc):
    bsz, nq, s = qg.shape
    tq = GQA_TQ
    sub = GQA_SUBTILES if (s // tq) % GQA_SUBTILES == 0 else 1
    wq = tq * sub
    return pl.pallas_call(
        functools.partial(_gqa_kernel, lat=lat, lc=lc, tq=tq),
        grid=(bsz, s // wq),
        in_specs=[
            pl.BlockSpec((None, nq, wq), lambda b, i: (b, 0, i)),
            pl.BlockSpec((None, s, kg.shape[2]), lambda b, i: (b, 0, 0)),
            pl.BlockSpec((None, vg.shape[1], s), lambda b, i: (b, 0, 0)),
            _const_spec(sink_rows.shape),
        ],
        out_specs=pl.BlockSpec((None, nq, wq), lambda b, i: (b, 0, i)),
        out_shape=jax.ShapeDtypeStruct((bsz, nq, s), BF16),
        scratch_shapes=[pltpu.VMEM((tq + 2 * WINDOW + lc, 2 * tq), F32)] * (GQA_HEADS // 2),
        compiler_params=_cparams(("parallel", "arbitrary"), 40),
        name="gqa_window_attn",
    )(qg, kg, vg, sink_rows)


def _mla_update(s_ref, s_max, vt, m, acc):
    m_new = jnp.maximum(m, s_max)
    alpha = jnp.exp2(m - m_new)
    p = jnp.exp2(s_ref[...] - m_new).astype(BF16)
    acc = alpha * acc + _dot(vt, p)
    return m_new, acc


def _mla_kernel(q_ref, k_ref, v_ref, o_ref, *bufs, nc, ctx_only, tq):
    for t in range(q_ref.shape[1] // tq):
        lanes = slice(t * tq, (t + 1) * tq)
        _mla_tile(q_ref[:, lanes], k_ref, v_ref, o_ref.at[:, lanes], bufs, nc, ctx_only)


def _mla_tile(q, k_ref, v_ref, o_ref, bufs, nc, ctx_only):
    tq = q.shape[1]
    ck = KV_CHUNK
    nbuf = len(bufs)
    m = jnp.full((1, tq), NEG_BIG, F32)
    acc = jnp.zeros((V_ROWS, tq), F32)

    def score(k, dst):
        s = _dot(k, q)
        dst[...] = s
        return jnp.max(s, axis=0, keepdims=True)

    def score_chunk(j, dst):
        return score(k_ref[pl.ds(pl.multiple_of(j * ck, ck), ck), :], dst)

    if ctx_only:
        s_x = bufs[0].at[0:KV_SUB, :]
        mx = score(k_ref[k_ref.shape[0] - KV_SUB:, :], s_x)
        m, acc = _mla_update(s_x, mx, v_ref[v_ref.shape[0] - 1][:, ck - KV_SUB:], m, acc)
    else:
        ahead = nbuf - 1
        mx = [None] * nbuf
        for j in range(min(ahead, nc)):
            mx[j] = score_chunk(j, bufs[j])

        def step(j, slot, m, acc, mx, last):
            if not last:
                nxt = (slot + ahead) % nbuf
                mx[nxt] = score_chunk(j + ahead, bufs[nxt])
            return _mla_update(bufs[slot], mx[slot], v_ref[j], m, acc)

        per_body = MLA_UNROLL * nbuf

        def body(g, c):
            m, acc, mx = c[0], c[1], list(c[2:])
            for u in range(per_body):
                m, acc = step(g * per_body + u, u % nbuf, m, acc, mx, False)
            return (m, acc, *mx)

        n_body = max(nc - ahead, 0) // per_body
        if n_body > 0:
            out = lax.fori_loop(0, n_body, body, (m, acc, *[m if x is None else x for x in mx]))
            m, acc, mx = out[0], out[1], list(out[2:])
        for j in range(n_body * per_body, nc):
            m, acc = step(j, j % nbuf, m, acc, mx, j + ahead >= nc)
    o_ref[...] = (acc[:MLA_V_DIM] / acc[MLA_V_DIM:MLA_V_DIM + 1]).astype(o_ref.dtype)


def _mla_call(qm, km, vm, tq, sub, q_tile0, n_q, ctx_only):
    bsz, _, s = qm.shape
    nc = vm.shape[1]
    wq = tq * sub
    if ctx_only:
        k_spec = pl.BlockSpec((None, KV_SUB, MLA_HEAD_PAD), lambda b, h, i: (b, s // KV_SUB - 1, h))
        v_spec = pl.BlockSpec((None, 1, V_ROWS, KV_CHUNK), lambda b, h, i: (b, nc - 1, h, 0))
    else:
        k_spec = pl.BlockSpec((None, s, MLA_HEAD_PAD), lambda b, h, i: (b, 0, h))
        v_spec = pl.BlockSpec((None, nc, V_ROWS, KV_CHUNK), lambda b, h, i: (b, 0, h, 0))
    return pl.pallas_call(
        functools.partial(_mla_kernel, nc=nc, ctx_only=ctx_only, tq=tq),
        grid=(bsz, MLA_HEADS, n_q),
        in_specs=[pl.BlockSpec((None, MLA_HEAD_PAD, wq), lambda b, h, i: (b, h, q_tile0 + i)), k_spec, v_spec],
        out_specs=pl.BlockSpec((None, MLA_V_DIM, wq), lambda b, h, i: (b, h, i)),
        out_shape=jax.ShapeDtypeStruct((bsz, MLA_HEADS * MLA_V_DIM, n_q * wq), BF16),
        scratch_shapes=[pltpu.VMEM((KV_CHUNK, tq), F32)] * MLA_NBUF,
        compiler_params=_cparams(("parallel", "arbitrary", "arbitrary"), 40),
        name="mla_attn_ctx" if ctx_only else "mla_attn",
    )(qm, km, vm)


def _mla(qm, km, vm, lat, lc):
    tq = min(MLA_TQ, lat)
    sub = MLA_SUBTILES if lat % (tq * MLA_SUBTILES) == 0 else 1
    assert lc == KV_SUB and lat % lc == 0
    return (_mla_call(qm, km, vm, tq, sub, 0, lat // (tq * sub), False),
            _mla_call(qm, km, vm, lc, 1, lat // lc, 1, True))


def _merge_mlp_kernel(x_ref, mod_ref, hl_refs, hc_refs, sk_ref, yg_ref, yml_ref, ymc_ref, gt_ref, wb_ref, wo_ref,
                      g_ref, w1_ref, w2_ref, o_ref, *, lat_tiles):
    d = D_MODEL
    is_lat = pl.program_id(1) < lat_tiles
    conv, x0, z = [jnp.where(is_lat, a[...], b[...]).astype(F32) for a, b in zip(hl_refs, hc_refs)]
    yh = (x0 * (conv + z * sk_ref[...])).astype(BF16)
    yg = yg_ref[...].astype(F32).T.astype(BF16)
    ym = jnp.where(is_lat, yml_ref[...], ymc_ref[...]).astype(F32).T.astype(BF16)
    merged = (gt_ref[:, 0:d].astype(F32) * _dot(yh, wb_ref[0])
              + gt_ref[:, d:2 * d].astype(F32) * _dot(yg, wb_ref[1])
              + gt_ref[:, 2 * d:].astype(F32) * _dot(ym, wb_ref[2]))
    res = _dot(merged.astype(BF16), wo_ref[...])
    x_mix = x_ref[...] + mod_ref[:, 2 * d:3 * d] * res
    o_ref[...] = _mlp_rows(x_mix, mod_ref, g_ref, w1_ref, w2_ref)


def _merge_mlp(xs, modsel, hy_lat, hy_ctx, skip, yg, ym_lat, ym_ctx, gates, wb, wo, g, w1, w2, lat_tiles, n_tiles):
    bsz, _, d = xs.shape
    tm = ROW_TILE
    c = HY_WIDTH
    row = lambda b, i: (b, i, 0)
    col = lambda b, i: (b, 0, i)
    lat_row = pl.BlockSpec((None, tm, c), lambda b, i: (b, jnp.minimum(i, lat_tiles - 1), 0))
    ctx_row = pl.BlockSpec((None, tm, c), lambda b, i: (b, jnp.maximum(i - lat_tiles, 0), 0))
    return pl.pallas_call(
        functools.partial(_merge_mlp_kernel, lat_tiles=lat_tiles),
        grid=(bsz, n_tiles),
        in_specs=[
            pl.BlockSpec((None, tm, d), row),
            pl.BlockSpec((None, None, 1, 6 * d), lambda b, i: (b, jnp.where(i >= lat_tiles, 1, 0), 0, 0)),
            [lat_row] * 3, [ctx_row] * 3, _const_spec(skip.shape),
            pl.BlockSpec((None, c, tm), col),
            pl.BlockSpec((None, c, tm), lambda b, i: (b, 0, jnp.minimum(i, lat_tiles - 1))),
            pl.BlockSpec((None, c, tm), lambda b, i: (b, 0, jnp.maximum(i - lat_tiles, 0))),
            pl.BlockSpec((None, tm, 3 * d), row),
            _const_spec(wb.shape), _const_spec(wo.shape),
            _const_spec(g.shape), _const_spec(w1.shape), _const_spec(w2.shape),
        ],
        out_specs=pl.BlockSpec((None, tm, d), row),
        out_shape=jax.ShapeDtypeStruct((bsz, n_tiles * tm, d), F32),
        compiler_params=_cparams(("parallel", "arbitrary"), 56),
        name="merge_mlp",
    )(xs, modsel, list(hy_lat), list(hy_ctx), skip, yg, ym_lat, ym_ctx, gates, wb, wo, g, w1, w2)


def _mlp_rows(x, mod_ref, g_ref, w1_ref, w2_ref):
    d = D_MODEL
    xn = x * lax.rsqrt(jnp.mean(x * x, axis=-1, keepdims=True) + EPS) * g_ref[...]
    h = (xn * (1.0 + mod_ref[:, 4 * d:5 * d]) + mod_ref[:, 3 * d:4 * d]).astype(BF16)
    acc = jnp.zeros(x.shape, F32)
    for j in range(D_FF // FF_CHUNK):
        sl = slice(j * FF_CHUNK, (j + 1) * FF_CHUNK)
        a = jnp.maximum(_dot(h, w1_ref[:, sl]), 0.0)
        acc = acc + _dot((a * a).astype(BF16), w2_ref[sl, :])
    return x + mod_ref[:, 5 * d:] * acc


def _rope_tables_t(rows, dim, lc):
    n_freq = dim // 4
    inv = ROPE_BASE ** (-jnp.arange(n_freq, dtype=F32) / n_freq)
    r = jnp.repeat(jnp.arange(rows, dtype=F32), GRID_W)
    col = jnp.tile(jnp.arange(GRID_W, dtype=F32), rows)
    ang = jnp.concatenate([r[:, None] * inv, col[:, None] * inv], axis=-1)
    cos_t = jnp.concatenate([jnp.cos(ang).T, jnp.ones((dim // 2, lc), F32)], axis=1)
    sin_t = jnp.concatenate([jnp.sin(ang).T, jnp.zeros((dim // 2, lc), F32)], axis=1)
    return cos_t, sin_t


def _lane_bcast(v):
    return jnp.broadcast_to(v.astype(F32)[:, None], (v.shape[0], ROW_TILE))


def _layer_weights(l, w_in, gqa_q_norm, gqa_k_norm, mla_q_a_norm, mla_kv_a_norm, w_q_b, w_kv_b, mla_q_norm, mla_k_norm):
    w = w_in[l]
    o = np.cumsum([0, 3 * HY_WIDTH, GQA_HEADS * GQA_HEAD_DIM, GQA_KV_HEADS * GQA_HEAD_DIM,
                   GQA_KV_HEADS * GQA_HEAD_DIM, MLA_Q_RANK, MLA_KV_RANK, MLA_ROPE_DIM, 3 * D_MODEL])
    wu = w[:, o[0]:o[1]].astype(BF16)
    wt = w[:, o[1]:o[7]].T.astype(BF16)
    wg = w[:, o[7]:o[8]].astype(BF16)
    wqb = w_q_b[l].T.astype(BF16)
    wkvb = w_kv_b[l].T.astype(BF16)
    return (wu, wg, wt, wqb, wkvb,
            _lane_bcast(gqa_q_norm[l]), _lane_bcast(gqa_k_norm[l]),
            _lane_bcast(mla_q_a_norm[l]), _lane_bcast(mla_kv_a_norm[l]),
            _lane_bcast(mla_q_norm[l]), _lane_bcast(mla_k_norm[l]))


def kernel(x, c, ctx, c_ctx, w_mod, b_mod, norm_mix_g, norm_mlp_g, w_in, hy_short_w, hy_f1_w, hy_f1_b, hy_f2_w, hy_f2_b, hy_sin_freq, hy_f3_w, hy_skip, gqa_q_norm, gqa_k_norm, gqa_sink, mla_q_a_norm, mla_kv_a_norm, w_q_b, w_kv_b, mla_q_norm, mla_k_norm, w_branch, w_out, w_mlp1, w_mlp2):
    bsz, lat, d = x.shape
    lc = ctx.shape[1]
    depth = w_mod.shape[0]
    s = lat + lc
    assert d == D_MODEL and lat % GRID_W == 0 and lat % ROW_TILE == 0 and lc == ROW_TILE and s % KV_CHUNK == 0
    lat_tiles = lat // ROW_TILE

    pad = (-(bsz + 1)) % 8
    cond = jnp.concatenate([c, c_ctx[None, :], jnp.zeros((pad, d), F32)], axis=0)
    mods = _mod_all(cond, w_mod, b_mod)

    tabs = _rope_tables_t(lat // GRID_W, GQA_HEAD_DIM, lc) + _rope_tables_t(lat // GRID_W, MLA_ROPE_DIM, lc)
    xs = jnp.concatenate([x, ctx], axis=1)

    for l in range(depth):
        ml = mods[l]
        modsel = jnp.stack([ml[:bsz], jnp.broadcast_to(ml[bsz][None], (bsz, 6 * d))], axis=1)[:, :, None, :]
        wts = _layer_weights(l, w_in, gqa_q_norm, gqa_k_norm, mla_q_a_norm, mla_kv_a_norm, w_q_b, w_kv_b,
                             mla_q_norm, mla_k_norm)
        u, gates, qg, kg, vg, qm, km, vm = _proj(xs, modsel, norm_mix_g[l][None, :], wts, tabs, lat_tiles)

        f1w = jnp.zeros((V7X_LANES, HY_FILTER_WIDTH), F32).at[:HY_EMB_DIM].set(hy_f1_w[l])
        fw = (f1w, hy_f1_b[l][None, :], hy_f2_w[l], hy_f2_b[l][None, :], hy_sin_freq[l], hy_f3_w[l])
        hy_lat, hy_ctx = _hyena(u, hy_short_w[l], fw, lat, lc)

        sink_rows = jnp.broadcast_to(gqa_sink[l].astype(F32)[:, None], (GQA_HEADS, GQA_TQ))
        yg = _gqa(qg, kg, vg, sink_rows, lat, lc)
        ym_lat, ym_ctx = _mla(qm, km, vm, lat, lc)

        n_tiles = lat_tiles if l == depth - 1 else s // ROW_TILE
        xs = _merge_mlp(xs, modsel, hy_lat, hy_ctx, hy_skip[l][None, :], yg, ym_lat, ym_ctx, gates,
                        w_branch[l].astype(BF16), w_out[l].astype(BF16), norm_mlp_g[l][None, :],
                        w_mlp1[l].astype(BF16), w_mlp2[l].astype(BF16), lat_tiles, n_tiles)
    return xs
```

```python
import functools
import math

import numpy as np
import jax
import jax.numpy as jnp
from jax import lax
from jax.experimental import pallas as pl
from jax.experimental.pallas import tpu as pltpu

D_MODEL = 1024
GRID_W = 64
HY_WIDTH = 512
HY_EMB_DIM = 33
HY_BANDS = (HY_EMB_DIM - 1) // 2
HY_FILTER_WIDTH = 64
HY_DECAY_TARGET = 1e-2
HY_FAST_DECAY = 0.3
HY_SLOW_DECAY = 1.5
GQA_HEADS = 8
GQA_KV_HEADS = 2
GQA_GROUP = GQA_HEADS // GQA_KV_HEADS
GQA_HEAD_DIM = 64
GQA_SCALE = GQA_HEAD_DIM ** -0.5
WINDOW = 128
MLA_HEADS = 8
MLA_Q_RANK = 384
MLA_KV_RANK = 256
MLA_NOPE_DIM = 64
MLA_ROPE_DIM = 32
MLA_V_DIM = 64
MLA_QK_DIM = MLA_NOPE_DIM + MLA_ROPE_DIM
MLA_SCALE = MLA_QK_DIM ** -0.5
D_FF = 4 * D_MODEL
ROPE_BASE = 10000.0
EPS = 1e-6
LOG2E = 1.4426950408889634
NEG_BIG = -1e30

V7X_LANES = 128
V7X_VMEM_BYTES = 64 * 1024 * 1024

ROW_TILE = 256
PROJ_BATCH = 2
CONV_TILE = 512
MLA_HEAD_PAD = 128
V_ROWS = 80
KV_CHUNK = ROW_TILE
KV_SUB = 256
MLA_NBUF = 3
MLA_UNROLL = 5
MLA_TQ = 1024
MLA_SUBTILES = 4
GQA_TQ = 256
GQA_SUBTILES = 3
FF_CHUNK = 1024
FFT_N2 = 128
FFT_NB = 16
FFT_KB = 8

F32 = jnp.float32
BF16 = jnp.bfloat16


def _cparams(sem, vmem_mb):
    limit = vmem_mb * 1024 * 1024
    assert limit < V7X_VMEM_BYTES
    return pltpu.CompilerParams(dimension_semantics=sem, vmem_limit_bytes=limit)


def _dot(a, b):
    return jnp.dot(a, b, preferred_element_type=F32)


def _dot_hi(a, b):
    return jnp.dot(a, b, preferred_element_type=F32, precision=lax.Precision.HIGHEST)


def _ones_row_block(width):
    r = lax.broadcasted_iota(jnp.int32, (V_ROWS - MLA_V_DIM, width), 0)
    return jnp.where(r == 0, 1.0, 0.0).astype(BF16)


def _const_spec(shape):
    nd = len(shape)
    return pl.BlockSpec(shape, lambda *_: (0,) * nd, pipeline_mode=pl.Buffered(1))


def _mod_kernel(c_ref, w_ref, b_ref, o_ref):
    c = c_ref[...]
    s = c * jax.nn.sigmoid(c)
    o_ref[...] = _dot_hi(s, w_ref[...]) + b_ref[...]


def _mod_all(cond, w_mod, b_mod):
    depth, d, n = w_mod.shape
    rows = cond.shape[0]
    tn = 1536
    return pl.pallas_call(
        _mod_kernel,
        grid=(depth, n // tn),
        in_specs=[
            pl.BlockSpec((rows, d), lambda l, j: (0, 0)),
            pl.BlockSpec((None, d, tn), lambda l, j: (l, 0, j)),
            pl.BlockSpec((None, 1, tn), lambda l, j: (l, 0, j)),
        ],
        out_specs=pl.BlockSpec((None, rows, tn), lambda l, j: (l, 0, j)),
        out_shape=jax.ShapeDtypeStruct((depth, rows, n), F32),
        compiler_params=_cparams(("arbitrary", "arbitrary"), 40),
        name="adaln_mod",
    )(cond, w_mod, b_mod.reshape(depth, 1, n))


def _rms_rows(x, n):
    return lax.rsqrt(jnp.sum(x * x, axis=0, keepdims=True) * (1.0 / n) + EPS)


def _rope_rows(x1, x2, cs, sn):
    return x1 * cs - x2 * sn, x1 * sn + x2 * cs


def _proj_kernel(x_ref, mod_ref, *refs):
    consts, outs = refs[:16], refs[16:]
    for n in range(x_ref.shape[0]):
        _proj_one(x_ref.at[n], mod_ref.at[n], *consts, *[o.at[n] for o in outs])


def _proj_one(x_ref, mod_ref, g_ref, wu_ref, wg_ref, wt_ref, wqb_ref, wkvb_ref,
              gq_ref, gk_ref, gqa_ref, gkva_ref, gmq_ref, gmk_ref,
              cg_ref, sg_ref, cm_ref, sm_ref,
              u_ref, gate_ref, qg_ref, kg_ref, vg_ref, qm_ref, km_ref, vm_ref):
    d = D_MODEL
    x = x_ref[...]
    tm = x.shape[0]
    shift = mod_ref[:, 0:d]
    scale = mod_ref[:, d:2 * d]
    xn = x * lax.rsqrt(jnp.mean(x * x, axis=-1, keepdims=True) + EPS) * g_ref[...]
    h = xn * (1.0 + scale) + shift
    hb = h.astype(BF16)

    ht = h.T.astype(BF16)
    t = _dot(wt_ref[...], ht)
    o_q, o_k, o_v = 0, 512, 640
    o_cq, o_ckv, o_kr = 768, 768 + MLA_Q_RANK, 768 + MLA_Q_RANK + MLA_KV_RANK

    u_ref[...] = _dot(hb, wu_ref[...]).astype(BF16)

    def gate_cols(n):
        cols = slice(n * d, (n + 1) * d)
        gate_ref[:, cols] = jax.nn.sigmoid(_dot(hb, wg_ref[:, cols])).astype(BF16)

    gate_cols(0)
    cg, sg = cg_ref[...], sg_ref[...]
    cm, sm = cm_ref[...], sm_ref[...]
    hd, hh = GQA_HEAD_DIM, GQA_HEAD_DIM // 2

    cq = t[o_cq:o_cq + MLA_Q_RANK]
    cqn = (cq * _rms_rows(cq, MLA_Q_RANK) * gqa_ref[...]).astype(BF16)
    qm = _dot(wqb_ref[...], cqn)
    gmq = gmq_ref[...]
    nd, rh = MLA_NOPE_DIM, MLA_ROPE_DIM // 2
    qpad = jnp.zeros((MLA_HEAD_PAD - MLA_QK_DIM, tm), BF16)
    for n in range(MLA_HEADS):
        xh = qm[n * MLA_QK_DIM:(n + 1) * MLA_QK_DIM]
        xh = xh * _rms_rows(xh, MLA_QK_DIM) * gmq * (MLA_SCALE * LOG2E)
        a, b = _rope_rows(xh[nd:nd + rh], xh[nd + rh:], cm, sm)
        base = n * MLA_HEAD_PAD
        qm_ref[base:base + nd, :] = xh[:nd].astype(BF16)
        qm_ref[base + nd:base + nd + rh, :] = a.astype(BF16)
        qm_ref[base + nd + rh:base + MLA_QK_DIM, :] = b.astype(BF16)
        qm_ref[base + MLA_QK_DIM:base + MLA_HEAD_PAD, :] = qpad

    gate_cols(1)
    ckv = t[o_ckv:o_ckv + MLA_KV_RANK]
    ckvn = (ckv * _rms_rows(ckv, MLA_KV_RANK) * gkva_ref[...]).astype(BF16)
    kv = _dot(wkvb_ref[...], ckvn)
    kr = t[o_kr:o_kr + MLA_ROPE_DIM]
    kr_ss = jnp.sum(kr * kr, axis=0, keepdims=True)
    gmk = gmk_ref[...]
    kzero = jnp.zeros((MLA_HEAD_PAD - MLA_QK_DIM, tm), F32)
    ones_rows = _ones_row_block(tm)
    parts = []
    for n in range(MLA_HEADS):
        kn = kv[n * 128:n * 128 + nd]
        vm_ref[n * V_ROWS:n * V_ROWS + MLA_V_DIM, :] = kv[n * 128 + nd:(n + 1) * 128].astype(BF16)
        vm_ref[n * V_ROWS + MLA_V_DIM:(n + 1) * V_ROWS, :] = ones_rows
        rs = lax.rsqrt((jnp.sum(kn * kn, axis=0, keepdims=True) + kr_ss) * (1.0 / MLA_QK_DIM) + EPS)
        krn = kr * rs * gmk[nd:]
        a, b = _rope_rows(krn[:rh], krn[rh:], cm, sm)
        parts += [kn * rs * gmk[:nd], a, b, kzero]
    km_ref[...] = jnp.concatenate(parts, axis=0).T.astype(BF16)

    gq = gq_ref[...]
    for n in range(GQA_HEADS):
        xh = t[o_q + n * hd:o_q + (n + 1) * hd]
        xh = xh * _rms_rows(xh, hd) * gq
        a, b = _rope_rows(xh[:hh], xh[hh:], cg, sg)
        qg_ref[n * hd:n * hd + hh, :] = (a * (GQA_SCALE * LOG2E)).astype(BF16)
        qg_ref[n * hd + hh:(n + 1) * hd, :] = (b * (GQA_SCALE * LOG2E)).astype(BF16)

    gate_cols(2)
    gk = gk_ref[...]
    zpad = jnp.zeros((V7X_LANES - hd, tm), F32)
    parts = []
    for n in range(GQA_KV_HEADS):
        xh = t[o_k + n * hd:o_k + (n + 1) * hd]
        xh = xh * _rms_rows(xh, hd) * gk
        a, b = _rope_rows(xh[:hh], xh[hh:], cg, sg)
        parts += [a, b, zpad]
    kg_ref[...] = jnp.concatenate(parts, axis=0).T.astype(BF16)
    for n in range(GQA_KV_HEADS):
        vg_ref[n * V_ROWS:n * V_ROWS + hd, :] = t[o_v + n * hd:o_v + (n + 1) * hd].astype(BF16)
        vg_ref[n * V_ROWS + hd:(n + 1) * V_ROWS, :] = ones_rows


def _proj(xs, modsel, g, wts, tabs, lat_tiles):
    bsz, s, d = xs.shape
    tm = ROW_TILE
    nt = s // tm
    ncs = KV_CHUNK // tm
    nb = PROJ_BATCH if bsz % PROJ_BATCH == 0 else 1
    (wu, wg, wt, wqb, wkvb, gq, gk, gqa, gkva, gmq, gmk) = wts
    cg, sg, cm, sm = tabs
    row = lambda b, i: (b, i, 0)
    col = lambda b, i: (b, 0, i)
    tab = lambda b, i: (0, i)
    in_specs = [
        pl.BlockSpec((nb, tm, d), row),
        pl.BlockSpec((nb, None, 1, 6 * d), lambda b, i: (b, jnp.where(i >= lat_tiles, 1, 0), 0, 0)),
        _const_spec(g.shape), _const_spec(wu.shape), _const_spec(wg.shape), _const_spec(wt.shape),
        _const_spec(wqb.shape), _const_spec(wkvb.shape),
        _const_spec(gq.shape), _const_spec(gk.shape), _const_spec(gqa.shape), _const_spec(gkva.shape),
        _const_spec(gmq.shape), _const_spec(gmk.shape),
        pl.BlockSpec((cg.shape[0], tm), tab), pl.BlockSpec((sg.shape[0], tm), tab),
        pl.BlockSpec((cm.shape[0], tm), tab), pl.BlockSpec((sm.shape[0], tm), tab),
    ]
    out_shape = [
        jax.ShapeDtypeStruct((bsz, s, 3 * HY_WIDTH), BF16),
        jax.ShapeDtypeStruct((bsz, s, 3 * d), BF16),
        jax.ShapeDtypeStruct((bsz, GQA_HEADS * GQA_HEAD_DIM, s), BF16),
        jax.ShapeDtypeStruct((bsz, s, GQA_KV_HEADS * V7X_LANES), BF16),
        jax.ShapeDtypeStruct((bsz, GQA_KV_HEADS * V_ROWS, s), BF16),
        jax.ShapeDtypeStruct((bsz, MLA_HEADS * MLA_HEAD_PAD, s), BF16),
        jax.ShapeDtypeStruct((bsz, s, MLA_HEADS * MLA_HEAD_PAD), BF16),
        jax.ShapeDtypeStruct((bsz, s // KV_CHUNK, MLA_HEADS * V_ROWS, KV_CHUNK), BF16),
    ]
    out_specs = [
        pl.BlockSpec((nb, tm, 3 * HY_WIDTH), row),
        pl.BlockSpec((nb, tm, 3 * d), row),
        pl.BlockSpec((nb, GQA_HEADS * GQA_HEAD_DIM, tm), col),
        pl.BlockSpec((nb, tm, GQA_KV_HEADS * V7X_LANES), row),
        pl.BlockSpec((nb, GQA_KV_HEADS * V_ROWS, tm), col),
        pl.BlockSpec((nb, MLA_HEADS * MLA_HEAD_PAD, tm), col),
        pl.BlockSpec((nb, tm, MLA_HEADS * MLA_HEAD_PAD), row),
        pl.BlockSpec((nb, None, MLA_HEADS * V_ROWS, tm), lambda b, i: (b, i // ncs, 0, i % ncs)),
    ]
    return pl.pallas_call(
        _proj_kernel,
        grid=(bsz // nb, nt),
        in_specs=in_specs,
        out_specs=out_specs,
        out_shape=out_shape,
        compiler_params=_cparams(("parallel", "arbitrary"), 56),
        name="in_proj",
    )(xs, modsel, g, wu, wg, wt, wqb, wkvb, gq, gk, gqa, gkva, gmq, gmk, cg, sg, cm, sm)


def _short_conv_kernel(u_ref, up_ref, un_ref, w_ref, x0_ref, z_ref, *, n_tiles):
    i = pl.program_id(1)
    u = u_ref[...].astype(F32)
    tm = u.shape[0]
    prev = jnp.where(i > 0, up_ref[7:8, :].astype(F32), 0.0)
    nxt = jnp.where(i < n_tiles - 1, un_ref[0:1, :].astype(F32), 0.0)
    ridx = lax.broadcasted_iota(jnp.int32, u.shape, 0)
    up = jnp.where(ridx == 0, prev, pltpu.roll(u, 1, axis=0))
    dn = jnp.where(ridx == tm - 1, nxt, pltpu.roll(u, tm - 1, axis=0))
    uc = up * w_ref[0:1, :] + u * w_ref[1:2, :] + dn * w_ref[2:3, :]
    c = HY_WIDTH
    x0_ref[...] = uc[:, :c].astype(BF16)
    z_ref[...] = (uc[:, c:2 * c] * uc[:, 2 * c:]).astype(BF16)


def _short_conv(u, short_w, row0, rows):
    bsz, s, c3 = u.shape
    tm = math.gcd(math.gcd(rows, row0), CONV_TILE) if row0 else math.gcd(rows, CONV_TILE)
    nt = rows // tm
    t0 = row0 // tm
    r8 = tm // 8
    last8 = s // 8 - 1
    return pl.pallas_call(
        functools.partial(_short_conv_kernel, n_tiles=nt),
        grid=(bsz, nt),
        in_specs=[
            pl.BlockSpec((None, tm, c3), lambda b, i: (b, t0 + i, 0)),
            pl.BlockSpec((None, 8, c3), lambda b, i: (b, jnp.maximum((t0 + i) * r8 - 1, 0), 0)),
            pl.BlockSpec((None, 8, c3), lambda b, i: (b, jnp.minimum((t0 + i + 1) * r8, last8), 0)),
            _const_spec(short_w.shape),
        ],
        out_specs=[pl.BlockSpec((None, tm, HY_WIDTH), lambda b, i: (b, i, 0))] * 2,
        out_shape=[jax.ShapeDtypeStruct((bsz, rows, HY_WIDTH), BF16)] * 2,
        compiler_params=_cparams(("parallel", "arbitrary"), 32),
        name="hyena_short_conv",
    )(u, u, u, short_w)


def _filter_kernel(zf_ref, w1_ref, b1_ref, w2_ref, b2_ref, fr_ref, w3_ref, dl_ref, h_ref, ss_ref):
    i = pl.program_id(0)
    zf = zf_ref[...]
    tl = zf.shape[0]
    h = jnp.sin(fr_ref[0:1, :] * (_dot_hi(zf, w1_ref[...]) + b1_ref[...]))
    h = jnp.sin(fr_ref[1:2, :] * (_dot_hi(h, w2_ref[...]) + b2_ref[...]))
    h = _dot_hi(h, w3_ref[...])
    decay = jnp.exp(-zf[:, 0:1] * dl_ref[...])
    c = HY_WIDTH
    hf = h[:, :c] * decay
    ridx = lax.broadcasted_iota(jnp.int32, (tl, c), 0) + i * tl
    hb = jnp.where(ridx == 0, 0.0, h[:, c:] * decay)
    h_ref[0] = hf.astype(BF16)
    h_ref[1] = hb.astype(BF16)
    ss = jnp.sum(hf * hf + hb * hb, axis=0, keepdims=True)

    @pl.when(i == 0)
    def _():
        ss_ref[...] = ss

    @pl.when(i > 0)
    def _():
        ss_ref[...] += ss


def _filter(zfeat, fw, deltas):
    length = zfeat.shape[0]
    tl = min(length, 1024)
    w1, b1, w2, b2, fr, w3 = fw
    return pl.pallas_call(
        _filter_kernel,
        grid=(length // tl,),
        in_specs=[pl.BlockSpec((tl, zfeat.shape[1]), lambda i: (i, 0))]
        + [_const_spec(a.shape) for a in (w1, b1, w2, b2, fr, w3, deltas)],
        out_specs=[pl.BlockSpec((2, tl, HY_WIDTH), lambda i: (0, i, 0)),
                   pl.BlockSpec((1, HY_WIDTH), lambda i: (0, 0))],
        out_shape=[jax.ShapeDtypeStruct((2, length, HY_WIDTH), BF16),
                   jax.ShapeDtypeStruct((1, HY_WIDTH), F32)],
        compiler_params=_cparams(("arbitrary",), 40),
        name="hyena_filter",
    )(zfeat, w1, b1, w2, b2, fr, w3, deltas)


def _fft_a_kernel(g_ref, x_ref, o_ref, *, nb, cw):
    for j in range(nb):
        sl = slice(j * cw, (j + 1) * cw)
        x = jnp.concatenate([x_ref[p, :, sl] for p in range(x_ref.shape[0])], axis=0)
        o_ref[:, sl] = _dot(g_ref[j], x).astype(o_ref.dtype)


def _fft_a(x3d, gmat):
    bx, parts, k1, w = x3d.shape
    n2, two_n1, _ = gmat.shape
    cw = w // n2
    nb = FFT_NB
    return pl.pallas_call(
        functools.partial(_fft_a_kernel, nb=nb, cw=cw),
        grid=(bx, n2 // nb),
        in_specs=[pl.BlockSpec((nb, two_n1, parts * k1), lambda b, j: (j, 0, 0)),
                  pl.BlockSpec((None, parts, k1, nb * cw), lambda b, j: (b, 0, 0, j))],
        out_specs=pl.BlockSpec((None, two_n1, nb * cw), lambda b, j: (b, 0, j)),
        out_shape=jax.ShapeDtypeStruct((bx, two_n1, w), BF16),
        compiler_params=_cparams(("parallel", "arbitrary"), 40),
        name="fft_stage_a",
    )(gmat, x3d)


def _fft_filter_b_kernel(fb_ref, a_ref, ss_ref, kf_ref, *, inv_n):
    n = a_ref.shape[3]
    rs = lax.rsqrt(ss_ref[...] + EPS) * inv_n
    for r in range(a_ref.shape[2]):
        xf = _dot(fb_ref[...], jnp.concatenate([a_ref[0, 0, r], a_ref[0, 1, r]], axis=0))
        xb = _dot(fb_ref[...], jnp.concatenate([a_ref[1, 0, r], a_ref[1, 1, r]], axis=0))
        kf_ref[0, r] = (xf[:n] + xb[:n]) * rs
        kf_ref[1, r] = (xf[n:] - xb[n:]) * rs


def _fft_filter_b(a5, fb, ssq, inv_n):
    _, _, n1, n2, c = a5.shape
    kb = min(FFT_KB, n1)
    return pl.pallas_call(
        functools.partial(_fft_filter_b_kernel, inv_n=inv_n),
        grid=(n1 // kb,),
        in_specs=[_const_spec(fb.shape),
                  pl.BlockSpec((2, 2, kb, n2, c), lambda k: (0, 0, k, 0, 0)),
                  _const_spec(ssq.shape)],
        out_specs=pl.BlockSpec((2, kb, n2, c), lambda k: (0, k, 0, 0)),
        out_shape=jax.ShapeDtypeStruct((2, n1, n2, c), F32),
        compiler_params=_cparams(("arbitrary",), 32),
        name="fft_filter_stage_b",
    )(fb, a5, ssq)


def _fft_b_kernel(fb_ref, fbi_ref, a_ref, kf_ref, o_ref):
    kb, n, cw = a_ref.shape[1:]
    a_all = jnp.concatenate([jnp.concatenate([a_ref[0, r], a_ref[1, r]], axis=0) for r in range(kb)], axis=1)
    x = _dot(fb_ref[...], a_all)
    ys = []
    for r in range(kb):
        xr, xi = x[:n, r * cw:(r + 1) * cw], x[n:, r * cw:(r + 1) * cw]
        kr, ki = kf_ref[0, r], kf_ref[1, r]
        ys.append(jnp.concatenate([xr * kr - xi * ki, xr * ki + xi * kr], axis=0).astype(BF16))
    c = _dot(fbi_ref[...], jnp.concatenate(ys, axis=1))
    for r in range(kb):
        o_ref[0, r] = c[:n, r * cw:(r + 1) * cw].astype(o_ref.dtype)
        o_ref[1, r] = c[n:, r * cw:(r + 1) * cw].astype(o_ref.dtype)


def _fft_b(a5, kf, fb, fbi):
    bsz, _, n1, n2, c = a5.shape
    kb = min(FFT_KB, n1)
    return pl.pallas_call(
        _fft_b_kernel,
        grid=(n1 // kb, bsz),
        in_specs=[_const_spec(fb.shape), _const_spec(fbi.shape),
                  pl.BlockSpec((None, 2, kb, n2, c), lambda k, b: (b, 0, k, 0, 0)),
                  pl.BlockSpec((2, kb, n2, c), lambda k, b: (0, k, 0, 0))],
        out_specs=pl.BlockSpec((None, 2, kb, n2, c), lambda k, b: (b, 0, k, 0, 0)),
        out_shape=jax.ShapeDtypeStruct(a5.shape, BF16),
        compiler_params=_cparams(("arbitrary", "arbitrary"), 32),
        name="fft_stage_b",
    )(fb, fbi, a5, kf)


def _fft_c_kernel(h_ref, c_ref, o_ref, *, nb, cw):
    parts, k1 = o_ref.shape[0], o_ref.shape[1]
    for j in range(nb):
        sl = slice(j * cw, (j + 1) * cw)
        y = _dot(h_ref[j], c_ref[:, sl])
        for p in range(parts):
            o_ref[p, :, sl] = y[p * k1:(p + 1) * k1].astype(o_ref.dtype)


def _fft_c(c2d, hmat, parts):
    bx, two_n1, w = c2d.shape
    n2, rows, _ = hmat.shape
    k1 = rows // parts
    cw = w // n2
    nb = FFT_NB
    return pl.pallas_call(
        functools.partial(_fft_c_kernel, nb=nb, cw=cw),
        grid=(bx, n2 // nb),
        in_specs=[pl.BlockSpec((nb, rows, two_n1), lambda b, j: (j, 0, 0)),
                  pl.BlockSpec((None, two_n1, nb * cw), lambda b, j: (b, 0, j))],
        out_specs=pl.BlockSpec((None, parts, k1, nb * cw), lambda b, j: (b, 0, 0, j)),
        out_shape=jax.ShapeDtypeStruct((bx, parts, k1, w), BF16),
        compiler_params=_cparams(("parallel", "arbitrary"), 40),
        name="fft_stage_c",
    )(hmat, c2d)


def _ctx_conv_kernel(fc_ref, fci_ref, h_ref, ss_ref, z_ref, o_ref):
    n = fc_ref.shape[0] // 2
    fc = fc_ref[...]
    kf = _dot(fc, h_ref[0])
    kb = _dot(fc, h_ref[1])
    rs = lax.rsqrt(ss_ref[...] + EPS)
    kr = (kf[:n] + kb[:n]) * rs
    ki = (kf[n:] - kb[n:]) * rs
    x = _dot(fc, z_ref[...])
    xr, xi = x[:n], x[n:]
    y = jnp.concatenate([xr * kr - xi * ki, xr * ki + xi * kr], axis=0).astype(BF16)
    o_ref[...] = _dot(fci_ref[...], y).astype(o_ref.dtype)


def _ctx_conv(fc, fci, hfb, ssq, z):
    bsz, lc, c = z.shape
    blk = pl.BlockSpec((None, lc, c), lambda b: (b, 0, 0))
    return pl.pallas_call(
        _ctx_conv_kernel,
        grid=(bsz,),
        in_specs=[_const_spec(fc.shape), _const_spec(fci.shape), _const_spec(hfb.shape),
                  _const_spec(ssq.shape), blk],
        out_specs=blk,
        out_shape=jax.ShapeDtypeStruct((bsz, lc, c), BF16),
        compiler_params=_cparams(("arbitrary",), 32),
        name="hyena_ctx_conv",
    )(fc, fci, hfb, ssq, z)


@functools.lru_cache(maxsize=None)
def _fft_tables(length):
    n = 2 * length
    n2 = FFT_N2
    n1 = n // n2
    k1 = length // n2
    kk = np.arange(n1)[:, None]
    g = np.empty((n2, 2 * n1, k1), np.float64)
    g2 = np.empty((n2, 2 * n1, 2 * k1), np.float64)
    h2 = np.empty((n2, 2 * k1, 2 * n1), np.float64)
    nn = np.arange(k1)[None, :]
    for j in range(n2):
        ang = 2.0 * np.pi * (((n2 * nn * kk) % n) + (j * kk) % n) / n
        c_, s_ = np.cos(ang), np.sin(ang)
        g[j] = np.concatenate([c_, -s_], axis=0)
        g2[j] = np.block([[c_, s_], [-s_, c_]])
        h2[j] = np.block([[c_.T, -s_.T], [s_.T, c_.T]])
    a = np.arange(n2)
    ph = 2.0 * np.pi * ((a[:, None] * a[None, :]) % n2) / n2
    c, s = np.cos(ph), np.sin(ph)
    fb = np.block([[c, s], [-s, c]])
    fbi = np.block([[c, -s], [s, c]])
    return (jnp.asarray(g, BF16), jnp.asarray(g2, BF16), jnp.asarray(h2, BF16), jnp.asarray(fb, BF16),
            jnp.asarray(fbi, BF16), n1, k1)


@functools.lru_cache(maxsize=None)
def _dft_tables(length):
    n = 2 * length
    k = np.arange(n)[:, None]
    t = np.arange(length)[None, :]
    ang = 2.0 * np.pi * ((k * t) % n) / n
    fc = np.concatenate([np.cos(ang), -np.sin(ang)], axis=0)
    fci = np.concatenate([np.cos(ang).T, -np.sin(ang).T], axis=1) / n
    return jnp.asarray(fc, BF16), jnp.asarray(fci, BF16)


@functools.lru_cache(maxsize=None)
def _filter_features(length):
    t = np.linspace(0.0, 1.0, length, dtype=np.float32)[:, None]
    w = (2.0 * math.pi * np.arange(length, dtype=np.float32)[:, None] / length).astype(np.float32)
    f = np.linspace(1e-4, HY_BANDS - 1, HY_BANDS, dtype=np.float32)[None, :]
    z = np.concatenate([t, np.cos(f * w), -np.sin(f * w)], axis=-1).astype(np.float32)
    zp = np.zeros((length, V7X_LANES), np.float32)
    zp[:, :HY_EMB_DIM] = z
    return jnp.asarray(zp)


def _hyena_deltas():
    max_decay = math.log(HY_DECAY_TARGET) / HY_FAST_DECAY
    min_decay = math.log(HY_DECAY_TARGET) / HY_SLOW_DECAY
    return jnp.abs(jnp.linspace(min_decay, max_decay, HY_WIDTH, dtype=F32))[None, :]


def _hyena(u, short_w, fw, lat, lc):
    bsz = u.shape[0]
    c = HY_WIDTH
    deltas = _hyena_deltas()
    x0, z = _short_conv(u, short_w, 0, lat)
    gmat, gmat2, hmat2, fb, fbi, n1, k1 = _fft_tables(lat)
    n2 = FFT_N2
    hfb, ssq = _filter(_filter_features(lat), fw, deltas)
    fa = _fft_a(hfb.reshape(2, 1, k1, n2 * c), gmat)
    kf = _fft_filter_b(fa.reshape(2, 2, n1, n2, c), fb, ssq, 1.0 / (2 * lat))
    npair = (bsz + 1) // 2
    zp = z if bsz % 2 == 0 else jnp.concatenate([z, jnp.zeros_like(z[:1])], axis=0)
    za = _fft_a(zp.reshape(npair, 2, k1, n2 * c), gmat2)
    zc = _fft_b(za.reshape(npair, 2, n1, n2, c), kf, fb, fbi)
    conv = _fft_c(zc.reshape(npair, 2 * n1, n2 * c), hmat2, 2).reshape(2 * npair, lat, c)[:bsz]
    x0_c, z_c = _short_conv(u, short_w, lat, lc)
    hfb_c, ssq_c = _filter(_filter_features(lc), fw, deltas)
    fc, fci = _dft_tables(lc)
    conv_c = _ctx_conv(fc, fci, hfb_c, ssq_c, z_c)
    return (conv, x0, z), (conv_c, x0_c, z_c)


def _gqa_kernel(q_ref, k_ref, v_ref, sink_ref, o_ref, *bufs, lat, lc, tq):
    sub = q_ref.shape[1] // tq
    for t in range(sub):
        lanes = slice(t * tq, (t + 1) * tq)
        _gqa_tile(pl.program_id(1) * sub + t, q_ref.at[:, lanes], k_ref, v_ref, sink_ref, o_ref.at[:, lanes], bufs,
                  lat, lc)


def _gqa_tile(i, q_ref, k_ref, v_ref, sink_ref, o_ref, bufs, lat, lc):
    tq = q_ref.shape[1]
    s_len = k_ref.shape[0]
    hd = GQA_HEAD_DIM
    w = WINDOW
    nwin = tq + 2 * w
    start = pl.multiple_of(jnp.clip(i * tq - w, 0, s_len - nwin), w)
    k_all = jnp.concatenate([k_ref[pl.ds(start, nwin), :], k_ref[lat:lat + lc, :]], axis=0)
    v_all = jnp.concatenate([v_ref[:, pl.ds(start, nwin)], v_ref[:, lat:lat + lc]], axis=1)
    q_pos = i * tq + lax.broadcasted_iota(jnp.int32, (1, tq), 1)
    k_pos = start + lax.broadcasted_iota(jnp.int32, (nwin, 1), 0)
    bias = (jnp.where(jnp.abs(k_pos - q_pos) <= w, 0.0, NEG_BIG)
            + jnp.where(k_pos < lat, 0.0, NEG_BIG)
            + jnp.where(q_pos < lat, 0.0, NEG_BIG))
    bias = jnp.concatenate([bias, jnp.zeros((lc, tq), F32)], axis=0)
    bias2 = jnp.concatenate([bias, bias], axis=1)
    zq = jnp.zeros((V7X_LANES - hd, 2 * tq), BF16)

    def score(pair, dst):
        g = pair // (GQA_GROUP // 2)
        r = 2 * pair * hd
        q2 = jnp.concatenate([q_ref[r:r + hd, :], q_ref[r + hd:r + 2 * hd, :]], axis=1)
        s = _dot(k_all[:, g * V7X_LANES:(g + 1) * V7X_LANES], jnp.concatenate([q2, zq], axis=0)) + bias2
        dst[...] = s
        return jnp.max(s, axis=0, keepdims=True)

    def update(pair, s_sc, s_max):
        g = pair // (GQA_GROUP // 2)
        r = 2 * pair * hd
        sink = jnp.concatenate([sink_ref[2 * pair:2 * pair + 1, :], sink_ref[2 * pair + 1:2 * pair + 2, :]],
                               axis=1) * LOG2E
        m = jnp.maximum(s_max, sink)
        p = jnp.exp2(s_sc[...] - m).astype(BF16)
        pv = _dot(v_all[g * V_ROWS:(g + 1) * V_ROWS, :], p)
        o = pv[:hd] / (pv[hd:hd + 1] + jnp.exp2(sink - m))
        o_ref[r:r + hd, :] = o[:, :tq].astype(o_ref.dtype)
        o_ref[r + hd:r + 2 * hd, :] = o[:, tq:].astype(o_ref.dtype)

    npair = GQA_HEADS // 2
    mx = [score(j, bufs[j]) for j in range(npair)]
    for j in range(npair):
        update(j, bufs[j], mx[j])


def _gqa(qg, kg, vg, sink_rows, lat, lc):
    bsz, nq, s = qg.shape
    tq = GQA_TQ
    sub = GQA_SUBTILES if (s // tq) % GQA_SUBTILES == 0 else 1
    wq = tq * sub
    return pl.pallas_call(
        functools.partial(_gqa_kernel, lat=lat, lc=lc, tq=tq),
        grid=(bsz, s // wq),
        in_specs=[
            pl.BlockSpec((None, nq, wq), lambda b, i: (b, 0, i)),
            pl.BlockSpec((None, s, kg.shape[2]), lambda b, i: (b, 0, 0)),
            pl.BlockSpec((None, vg.shape[1], s), lambda b, i: (b, 0, 0)),
            _const_spec(sink_rows.shape),
        ],
        out_specs=pl.BlockSpec((None, nq, wq), lambda b, i: (b, 0, i)),
        out_shape=jax.ShapeDtypeStruct((bsz, nq, s), BF16),
        scratch_shapes=[pltpu.VMEM((tq + 2 * WINDOW + lc, 2 * tq), F32)] * (GQA_HEADS // 2),
        compiler_params=_cparams(("parallel", "arbitrary"), 40),
        name="gqa_window_attn",
    )(qg, kg, vg, sink_rows)


def _mla_update(s_ref, s_max, vt, m, acc):
    m_new = jnp.maximum(m, s_max)
    alpha = jnp.exp2(m - m_new)
    p = jnp.exp2(s_ref[...] - m_new).astype(BF16)
    acc = alpha * acc + _dot(vt, p)
    return m_new, acc


def _mla_kernel(q_ref, k_ref, v_ref, o_ref, *bufs, nc, ctx_only, tq):
    for t in range(q_ref.shape[1] // tq):
        lanes = slice(t * tq, (t + 1) * tq)
        _mla_tile(q_ref[:, lanes], k_ref, v_ref, o_ref.at[:, lanes], bufs, nc, ctx_only)


def _mla_tile(q, k_ref, v_ref, o_ref, bufs, nc, ctx_only):
    tq = q.shape[1]
    ck = KV_CHUNK
    nbuf = len(bufs)
    m = jnp.full((1, tq), NEG_BIG, F32)
    acc = jnp.zeros((V_ROWS, tq), F32)

    def score(k, dst):
        s = _dot(k, q)
        dst[...] = s
        return jnp.max(s, axis=0, keepdims=True)

    def score_chunk(j, dst):
        return score(k_ref[pl.ds(pl.multiple_of(j * ck, ck), ck), :], dst)

    if ctx_only:
        s_x = bufs[0].at[0:KV_SUB, :]
        mx = score(k_ref[k_ref.shape[0] - KV_SUB:, :], s_x)
        m, acc = _mla_update(s_x, mx, v_ref[v_ref.shape[0] - 1][:, ck - KV_SUB:], m, acc)
    else:
        ahead = nbuf - 1
        mx = [None] * nbuf
        for j in range(min(ahead, nc)):
            mx[j] = score_chunk(j, bufs[j])

        def step(j, slot, m, acc, mx, last):
            if not last:
                nxt = (slot + ahead) % nbuf
                mx[nxt] = score_chunk(j + ahead, bufs[nxt])
            return _mla_update(bufs[slot], mx[slot], v_ref[j], m, acc)

        per_body = MLA_UNROLL * nbuf

        def body(g, c):
            m, acc, mx = c[0], c[1], list(c[2:])
            for u in range(per_body):
                m, acc = step(g * per_body + u, u % nbuf, m, acc, mx, False)
            return (m, acc, *mx)

        n_body = max(nc - ahead, 0) // per_body
        if n_body > 0:
            out = lax.fori_loop(0, n_body, body, (m, acc, *[m if x is None else x for x in mx]))
            m, acc, mx = out[0], out[1], list(out[2:])
        for j in range(n_body * per_body, nc):
            m, acc = step(j, j % nbuf, m, acc, mx, j + ahead >= nc)
    o_ref[...] = (acc[:MLA_V_DIM] / acc[MLA_V_DIM:MLA_V_DIM + 1]).astype(o_ref.dtype)


def _mla_call(qm, km, vm, tq, sub, q_tile0, n_q, ctx_only):
    bsz, _, s = qm.shape
    nc = vm.shape[1]
    wq = tq * sub
    if ctx_only:
        k_spec = pl.BlockSpec((None, KV_SUB, MLA_HEAD_PAD), lambda b, h, i: (b, s // KV_SUB - 1, h))
        v_spec = pl.BlockSpec((None, 1, V_ROWS, KV_CHUNK), lambda b, h, i: (b, nc - 1, h, 0))
    else:
        k_spec = pl.BlockSpec((None, s, MLA_HEAD_PAD), lambda b, h, i: (b, 0, h))
        v_spec = pl.BlockSpec((None, nc, V_ROWS, KV_CHUNK), lambda b, h, i: (b, 0, h, 0))
    return pl.pallas_call(
        functools.partial(_mla_kernel, nc=nc, ctx_only=ctx_only, tq=tq),
        grid=(bsz, MLA_HEADS, n_q),
        in_specs=[pl.BlockSpec((None, MLA_HEAD_PAD, wq), lambda b, h, i: (b, h, q_tile0 + i)), k_spec, v_spec],
        out_specs=pl.BlockSpec((None, MLA_V_DIM, wq), lambda b, h, i: (b, h, i)),
        out_shape=jax.ShapeDtypeStruct((bsz, MLA_HEADS * MLA_V_DIM, n_q * wq), BF16),
        scratch_shapes=[pltpu.VMEM((KV_CHUNK, tq), F32)] * MLA_NBUF,
        compiler_params=_cparams(("parallel", "arbitrary", "arbitrary"), 40),
        name="mla_attn_ctx" if ctx_only else "mla_attn",
    )(qm, km, vm)


def _mla(qm, km, vm, lat, lc):
    tq = min(MLA_TQ, lat)
    sub = MLA_SUBTILES if lat % (tq * MLA_SUBTILES) == 0 else 1
    assert lc == KV_SUB and lat % lc == 0 and lat % tq == 0
    return (_mla_call(qm, km, vm, tq, sub, 0, lat // (tq * sub), False),
            _mla_call(qm, km, vm, lc, 1, lat // lc, 1, True))


def _merge_mlp_kernel(x_ref, mod_ref, hl_refs, hc_refs, sk_ref, yg_ref, yml_ref, ymc_ref, gt_ref, wb_ref, wo_ref,
                      g_ref, w1_ref, w2_ref, o_ref, *, lat_tiles):
    d = D_MODEL
    is_lat = pl.program_id(1) < lat_tiles
    conv, x0, z = [jnp.where(is_lat, a[...], b[...]).astype(F32) for a, b in zip(hl_refs, hc_refs)]
    yh = (x0 * (conv + z * sk_ref[...])).astype(BF16)
    yg = yg_ref[...].astype(F32).T.astype(BF16)
    ym = jnp.where(is_lat, yml_ref[...], ymc_ref[...]).astype(F32).T.astype(BF16)
    merged = (gt_ref[:, 0:d].astype(F32) * _dot(yh, wb_ref[0])
              + gt_ref[:, d:2 * d].astype(F32) * _dot(yg, wb_ref[1])
              + gt_ref[:, 2 * d:].astype(F32) * _dot(ym, wb_ref[2]))
    res = _dot(merged.astype(BF16), wo_ref[...])
    x_mix = x_ref[...] + mod_ref[:, 2 * d:3 * d] * res
    o_ref[...] = _mlp_rows(x_mix, mod_ref, g_ref, w1_ref, w2_ref)


def _merge_mlp(xs, modsel, hy_lat, hy_ctx, skip, yg, ym_lat, ym_ctx, gates, wb, wo, g, w1, w2, lat_tiles, n_tiles):
    bsz, _, d = xs.shape
    tm = ROW_TILE
    c = HY_WIDTH
    row = lambda b, i: (b, i, 0)
    col = lambda b, i: (b, 0, i)
    lat_row = pl.BlockSpec((None, tm, c), lambda b, i: (b, jnp.minimum(i, lat_tiles - 1), 0))
    ctx_row = pl.BlockSpec((None, tm, c), lambda b, i: (b, jnp.maximum(i - lat_tiles, 0), 0))
    return pl.pallas_call(
        functools.partial(_merge_mlp_kernel, lat_tiles=lat_tiles),
        grid=(bsz, n_tiles),
        in_specs=[
            pl.BlockSpec((None, tm, d), row),
            pl.BlockSpec((None, None, 1, 6 * d), lambda b, i: (b, jnp.where(i >= lat_tiles, 1, 0), 0, 0)),
            [lat_row] * 3, [ctx_row] * 3, _const_spec(skip.shape),
            pl.BlockSpec((None, c, tm), col),
            pl.BlockSpec((None, c, tm), lambda b, i: (b, 0, jnp.minimum(i, lat_tiles - 1))),
            pl.BlockSpec((None, c, tm), lambda b, i: (b, 0, jnp.maximum(i - lat_tiles, 0))),
            pl.BlockSpec((None, tm, 3 * d), row),
            _const_spec(wb.shape), _const_spec(wo.shape),
            _const_spec(g.shape), _const_spec(w1.shape), _const_spec(w2.shape),
        ],
        out_specs=pl.BlockSpec((None, tm, d), row),
        out_shape=jax.ShapeDtypeStruct((bsz, n_tiles * tm, d), F32),
        compiler_params=_cparams(("parallel", "arbitrary"), 56),
        name="merge_mlp",
    )(xs, modsel, list(hy_lat), list(hy_ctx), skip, yg, ym_lat, ym_ctx, gates, wb, wo, g, w1, w2)


def _mlp_rows(x, mod_ref, g_ref, w1_ref, w2_ref):
    d = D_MODEL
    xn = x * lax.rsqrt(jnp.mean(x * x, axis=-1, keepdims=True) + EPS) * g_ref[...]
    h = (xn * (1.0 + mod_ref[:, 4 * d:5 * d]) + mod_ref[:, 3 * d:4 * d]).astype(BF16)
    acc = jnp.zeros(x.shape, F32)
    for j in range(D_FF // FF_CHUNK):
        sl = slice(j * FF_CHUNK, (j + 1) * FF_CHUNK)
        a = jnp.maximum(_dot(h, w1_ref[:, sl]), 0.0)
        acc = acc + _dot((a * a).astype(BF16), w2_ref[sl, :])
    return x + mod_ref[:, 5 * d:] * acc


def _rope_tables_t(rows, dim, lc):
    n_freq = dim // 4
    inv = ROPE_BASE ** (-jnp.arange(n_freq, dtype=F32) / n_freq)
    r = jnp.repeat(jnp.arange(rows, dtype=F32), GRID_W)
    col = jnp.tile(jnp.arange(GRID_W, dtype=F32), rows)
    ang = jnp.concatenate([r[:, None] * inv, col[:, None] * inv], axis=-1)
    cos_t = jnp.concatenate([jnp.cos(ang).T, jnp.ones((dim // 2, lc), F32)], axis=1)
    sin_t = jnp.concatenate([jnp.sin(ang).T, jnp.zeros((dim // 2, lc), F32)], axis=1)
    return cos_t, sin_t


def _lane_bcast(v):
    return jnp.broadcast_to(v.astype(F32)[:, None], (v.shape[0], ROW_TILE))


def _layer_weights(l, w_in, gqa_q_norm, gqa_k_norm, mla_q_a_norm, mla_kv_a_norm, w_q_b, w_kv_b, mla_q_norm, mla_k_norm):
    w = w_in[l]
    o = np.cumsum([0, 3 * HY_WIDTH, GQA_HEADS * GQA_HEAD_DIM, GQA_KV_HEADS * GQA_HEAD_DIM,
                   GQA_KV_HEADS * GQA_HEAD_DIM, MLA_Q_RANK, MLA_KV_RANK, MLA_ROPE_DIM, 3 * D_MODEL])
    wu = w[:, o[0]:o[1]].astype(BF16)
    wt = w[:, o[1]:o[7]].T.astype(BF16)
    wg = w[:, o[7]:o[8]].astype(BF16)
    wqb = w_q_b[l].T.astype(BF16)
    wkvb = w_kv_b[l].T.astype(BF16)
    return (wu, wg, wt, wqb, wkvb,
            _lane_bcast(gqa_q_norm[l]), _lane_bcast(gqa_k_norm[l]),
            _lane_bcast(mla_q_a_norm[l]), _lane_bcast(mla_kv_a_norm[l]),
            _lane_bcast(mla_q_norm[l]), _lane_bcast(mla_k_norm[l]))


def kernel(x, c, ctx, c_ctx, w_mod, b_mod, norm_mix_g, norm_mlp_g, w_in, hy_short_w, hy_f1_w, hy_f1_b, hy_f2_w, hy_f2_b, hy_sin_freq, hy_f3_w, hy_skip, gqa_q_norm, gqa_k_norm, gqa_sink, mla_q_a_norm, mla_kv_a_norm, w_q_b, w_kv_b, mla_q_norm, mla_k_norm, w_branch, w_out, w_mlp1, w_mlp2):
    bsz, lat, d = x.shape
    lc = ctx.shape[1]
    depth = w_mod.shape[0]
    s = lat + lc
    assert d == D_MODEL and lat % GRID_W == 0 and lat % ROW_TILE == 0 and lc == ROW_TILE and s % KV_CHUNK == 0
    lat_tiles = lat // ROW_TILE

    pad = (-(bsz + 1)) % 8
    cond = jnp.concatenate([c, c_ctx[None, :], jnp.zeros((pad, d), F32)], axis=0)
    mods = _mod_all(cond, w_mod, b_mod)

    tabs = _rope_tables_t(lat // GRID_W, GQA_HEAD_DIM, lc) + _rope_tables_t(lat // GRID_W, MLA_ROPE_DIM, lc)
    xs = jnp.concatenate([x, ctx], axis=1)

    for l in range(depth):
        ml = mods[l]
        modsel = jnp.stack([ml[:bsz], jnp.broadcast_to(ml[bsz][None], (bsz, 6 * d))], axis=1)[:, :, None, :]
        wts = _layer_weights(l, w_in, gqa_q_norm, gqa_k_norm, mla_q_a_norm, mla_kv_a_norm, w_q_b, w_kv_b,
                             mla_q_norm, mla_k_norm)
        u, gates, qg, kg, vg, qm, km, vm = _proj(xs, modsel, norm_mix_g[l][None, :], wts, tabs, lat_tiles)

        f1w = jnp.zeros((V7X_LANES, HY_FILTER_WIDTH), F32).at[:HY_EMB_DIM].set(hy_f1_w[l])
        fw = (f1w, hy_f1_b[l][None, :], hy_f2_w[l], hy_f2_b[l][None, :], hy_sin_freq[l], hy_f3_w[l])
        hy_lat, hy_ctx = _hyena(u, hy_short_w[l], fw, lat, lc)

        sink_rows = jnp.broadcast_to(gqa_sink[l].astype(F32)[:, None], (GQA_HEADS, GQA_TQ))
        yg = _gqa(qg, kg, vg, sink_rows, lat, lc)
        ym_lat, ym_ctx = _mla(qm, km, vm, lat, lc)

        n_tiles = lat_tiles if l == depth - 1 else s // ROW_TILE
        xs = _merge_mlp(xs, modsel, hy_lat, hy_ctx, hy_skip[l][None, :], yg, ym_lat, ym_ctx, gates,
                        w_branch[l].astype(BF16), w_out[l].astype(BF16), norm_mlp_g[l][None, :],
                        w_mlp1[l].astype(BF16), w_mlp2[l].astype(BF16), lat_tiles, n_tiles)
    return xs
```

```python
import functools
import math

import numpy as np
import jax
import jax.numpy as jnp
from jax import lax
from jax.experimental import pallas as pl
from jax.experimental.pallas import tpu as pltpu

D_MODEL = 1024
GRID_W = 64
HY_WIDTH = 512
HY_EMB_DIM = 33
HY_BANDS = (HY_EMB_DIM - 1) // 2
HY_FILTER_WIDTH = 64
HY_DECAY_TARGET = 1e-2
HY_FAST_DECAY = 0.3
HY_SLOW_DECAY = 1.5
GQA_HEADS = 8
GQA_KV_HEADS = 2
GQA_GROUP = GQA_HEADS // GQA_KV_HEADS
GQA_HEAD_DIM = 64
GQA_SCALE = GQA_HEAD_DIM ** -0.5
WINDOW = 128
MLA_HEADS = 8
MLA_Q_RANK = 384
MLA_KV_RANK = 256
MLA_NOPE_DIM = 64
MLA_ROPE_DIM = 32
MLA_V_DIM = 64
MLA_QK_DIM = MLA_NOPE_DIM + MLA_ROPE_DIM
MLA_SCALE = MLA_QK_DIM ** -0.5
D_FF = 4 * D_MODEL
ROPE_BASE = 10000.0
EPS = 1e-6
LOG2E = 1.4426950408889634
NEG_BIG = -1e30

V7X_LANES = 128
V7X_VMEM_BYTES = 64 * 1024 * 1024

ROW_TILE = 256
PROJ_BATCH = 2
MERGE_BATCH = 2
CONV_TILE = 512
MLA_HEAD_PAD = 128
V_ROWS = 80
KV_CHUNK = ROW_TILE
KV_SUB = 256
MLA_NBUF = 3
MLA_UNROLL = 5
MLA_TQ = 1024
MLA_SUBTILES = 4
GQA_TQ = 256
GQA_SUBTILES = 3
FF_CHUNK = 1024
FFT_N2 = 128
FFT_NB = 16
FFT_KB = 8

F32 = jnp.float32
BF16 = jnp.bfloat16


def _cparams(sem, vmem_mb):
    limit = vmem_mb * 1024 * 1024
    assert limit < V7X_VMEM_BYTES
    return pltpu.CompilerParams(dimension_semantics=sem, vmem_limit_bytes=limit)


def _dot(a, b):
    return jnp.dot(a, b, preferred_element_type=F32)


def _dot_hi(a, b):
    return jnp.dot(a, b, preferred_element_type=F32, precision=lax.Precision.HIGHEST)


def _ones_row_block(width):
    r = lax.broadcasted_iota(jnp.int32, (V_ROWS - MLA_V_DIM, width), 0)
    return jnp.where(r == 0, 1.0, 0.0).astype(BF16)


def _const_spec(shape):
    nd = len(shape)
    return pl.BlockSpec(shape, lambda *_: (0,) * nd, pipeline_mode=pl.Buffered(1))


def _mod_kernel(c_ref, w_ref, b_ref, o_ref):
    c = c_ref[...]
    s = c * jax.nn.sigmoid(c)
    o_ref[...] = _dot_hi(s, w_ref[...]) + b_ref[...]


def _mod_all(cond, w_mod, b_mod):
    depth, d, n = w_mod.shape
    rows = cond.shape[0]
    tn = 1536
    return pl.pallas_call(
        _mod_kernel,
        grid=(depth, n // tn),
        in_specs=[
            pl.BlockSpec((rows, d), lambda l, j: (0, 0)),
            pl.BlockSpec((None, d, tn), lambda l, j: (l, 0, j)),
            pl.BlockSpec((None, 1, tn), lambda l, j: (l, 0, j)),
        ],
        out_specs=pl.BlockSpec((None, rows, tn), lambda l, j: (l, 0, j)),
        out_shape=jax.ShapeDtypeStruct((depth, rows, n), F32),
        compiler_params=_cparams(("arbitrary", "arbitrary"), 40),
        name="adaln_mod",
    )(cond, w_mod, b_mod.reshape(depth, 1, n))


def _rms_rows(x, n):
    return lax.rsqrt(jnp.sum(x * x, axis=0, keepdims=True) * (1.0 / n) + EPS)


def _rope_rows(x1, x2, cs, sn):
    return x1 * cs - x2 * sn, x1 * sn + x2 * cs


def _proj_kernel(x_ref, mod_ref, *refs):
    consts, outs = refs[:16], refs[16:]
    for n in range(x_ref.shape[0]):
        _proj_one(x_ref.at[n], mod_ref.at[n], *consts, *[o.at[n] for o in outs])


def _proj_one(x_ref, mod_ref, g_ref, wu_ref, wg_ref, wt_ref, wqb_ref, wkvb_ref,
              gq_ref, gk_ref, gqa_ref, gkva_ref, gmq_ref, gmk_ref,
              cg_ref, sg_ref, cm_ref, sm_ref,
              u_ref, gate_ref, qg_ref, kg_ref, vg_ref, qm_ref, km_ref, vm_ref):
    d = D_MODEL
    x = x_ref[...]
    tm = x.shape[0]
    shift = mod_ref[:, 0:d]
    scale = mod_ref[:, d:2 * d]
    xn = x * lax.rsqrt(jnp.mean(x * x, axis=-1, keepdims=True) + EPS) * g_ref[...]
    h = xn * (1.0 + scale) + shift
    hb = h.astype(BF16)

    ht = h.T.astype(BF16)
    t = _dot(wt_ref[...], ht)
    o_q, o_k, o_v = 0, 512, 640
    o_cq, o_ckv, o_kr = 768, 768 + MLA_Q_RANK, 768 + MLA_Q_RANK + MLA_KV_RANK

    u_ref[...] = _dot(hb, wu_ref[...]).astype(BF16)

    def gate_cols(n):
        cols = slice(n * d, (n + 1) * d)
        gate_ref[:, cols] = jax.nn.sigmoid(_dot(hb, wg_ref[:, cols])).astype(BF16)

    gate_cols(0)
    cg, sg = cg_ref[...], sg_ref[...]
    cm, sm = cm_ref[...], sm_ref[...]
    hd, hh = GQA_HEAD_DIM, GQA_HEAD_DIM // 2

    cq = t[o_cq:o_cq + MLA_Q_RANK]
    cqn = (cq * _rms_rows(cq, MLA_Q_RANK) * gqa_ref[...]).astype(BF16)
    qm = _dot(wqb_ref[...], cqn)
    gmq = gmq_ref[...]
    nd, rh = MLA_NOPE_DIM, MLA_ROPE_DIM // 2
    qpad = jnp.zeros((MLA_HEAD_PAD - MLA_QK_DIM, tm), BF16)
    for n in range(MLA_HEADS):
        xh = qm[n * MLA_QK_DIM:(n + 1) * MLA_QK_DIM]
        xh = xh * _rms_rows(xh, MLA_QK_DIM) * gmq * (MLA_SCALE * LOG2E)
        a, b = _rope_rows(xh[nd:nd + rh], xh[nd + rh:], cm, sm)
        base = n * MLA_HEAD_PAD
        qm_ref[base:base + nd, :] = xh[:nd].astype(BF16)
        qm_ref[base + nd:base + nd + rh, :] = a.astype(BF16)
        qm_ref[base + nd + rh:base + MLA_QK_DIM, :] = b.astype(BF16)
        qm_ref[base + MLA_QK_DIM:base + MLA_HEAD_PAD, :] = qpad

    gate_cols(1)
    ckv = t[o_ckv:o_ckv + MLA_KV_RANK]
    ckvn = (ckv * _rms_rows(ckv, MLA_KV_RANK) * gkva_ref[...]).astype(BF16)
    kv = _dot(wkvb_ref[...], ckvn)
    kr = t[o_kr:o_kr + MLA_ROPE_DIM]
    kr_ss = jnp.sum(kr * kr, axis=0, keepdims=True)
    gmk = gmk_ref[...]
    kzero = jnp.zeros((MLA_HEAD_PAD - MLA_QK_DIM, tm), F32)
    ones_rows = _ones_row_block(tm)
    parts = []
    for n in range(MLA_HEADS):
        kn = kv[n * 128:n * 128 + nd]
        vm_ref[n * V_ROWS:n * V_ROWS + MLA_V_DIM, :] = kv[n * 128 + nd:(n + 1) * 128].astype(BF16)
        vm_ref[n * V_ROWS + MLA_V_DIM:(n + 1) * V_ROWS, :] = ones_rows
        rs = lax.rsqrt((jnp.sum(kn * kn, axis=0, keepdims=True) + kr_ss) * (1.0 / MLA_QK_DIM) + EPS)
        krn = kr * rs * gmk[nd:]
        a, b = _rope_rows(krn[:rh], krn[rh:], cm, sm)
        parts += [kn * rs * gmk[:nd], a, b, kzero]
    km_ref[...] = jnp.concatenate(parts, axis=0).T.astype(BF16)

    gq = gq_ref[...]
    for n in range(GQA_HEADS):
        xh = t[o_q + n * hd:o_q + (n + 1) * hd]
        xh = xh * _rms_rows(xh, hd) * gq
        a, b = _rope_rows(xh[:hh], xh[hh:], cg, sg)
        qg_ref[n * hd:n * hd + hh, :] = (a * (GQA_SCALE * LOG2E)).astype(BF16)
        qg_ref[n * hd + hh:(n + 1) * hd, :] = (b * (GQA_SCALE * LOG2E)).astype(BF16)

    gate_cols(2)
    gk = gk_ref[...]
    zpad = jnp.zeros((V7X_LANES - hd, tm), F32)
    parts = []
    for n in range(GQA_KV_HEADS):
        xh = t[o_k + n * hd:o_k + (n + 1) * hd]
        xh = xh * _rms_rows(xh, hd) * gk
        a, b = _rope_rows(xh[:hh], xh[hh:], cg, sg)
        parts += [a, b, zpad]
    kg_ref[...] = jnp.concatenate(parts, axis=0).T.astype(BF16)
    for n in range(GQA_KV_HEADS):
        vg_ref[n * V_ROWS:n * V_ROWS + hd, :] = t[o_v + n * hd:o_v + (n + 1) * hd].astype(BF16)
        vg_ref[n * V_ROWS + hd:(n + 1) * V_ROWS, :] = ones_rows


def _proj(xs, modsel, g, wts, tabs, lat_tiles):
    bsz, s, d = xs.shape
    tm = ROW_TILE
    nt = s // tm
    ncs = KV_CHUNK // tm
    nb = PROJ_BATCH if bsz % PROJ_BATCH == 0 else 1
    (wu, wg, wt, wqb, wkvb, gq, gk, gqa, gkva, gmq, gmk) = wts
    cg, sg, cm, sm = tabs
    row = lambda b, i: (b, i, 0)
    col = lambda b, i: (b, 0, i)
    tab = lambda b, i: (0, i)
    in_specs = [
        pl.BlockSpec((nb, tm, d), row),
        pl.BlockSpec((nb, None, 1, 6 * d), lambda b, i: (b, jnp.where(i >= lat_tiles, 1, 0), 0, 0)),
        _const_spec(g.shape), _const_spec(wu.shape), _const_spec(wg.shape), _const_spec(wt.shape),
        _const_spec(wqb.shape), _const_spec(wkvb.shape),
        _const_spec(gq.shape), _const_spec(gk.shape), _const_spec(gqa.shape), _const_spec(gkva.shape),
        _const_spec(gmq.shape), _const_spec(gmk.shape),
        pl.BlockSpec((cg.shape[0], tm), tab), pl.BlockSpec((sg.shape[0], tm), tab),
        pl.BlockSpec((cm.shape[0], tm), tab), pl.BlockSpec((sm.shape[0], tm), tab),
    ]
    out_shape = [
        jax.ShapeDtypeStruct((bsz, s, 3 * HY_WIDTH), BF16),
        jax.ShapeDtypeStruct((bsz, s, 3 * d), BF16),
        jax.ShapeDtypeStruct((bsz, GQA_HEADS * GQA_HEAD_DIM, s), BF16),
        jax.ShapeDtypeStruct((bsz, s, GQA_KV_HEADS * V7X_LANES), BF16),
        jax.ShapeDtypeStruct((bsz, GQA_KV_HEADS * V_ROWS, s), BF16),
        jax.ShapeDtypeStruct((bsz, MLA_HEADS * MLA_HEAD_PAD, s), BF16),
        jax.ShapeDtypeStruct((bsz, s, MLA_HEADS * MLA_HEAD_PAD), BF16),
        jax.ShapeDtypeStruct((bsz, s // KV_CHUNK, MLA_HEADS * V_ROWS, KV_CHUNK), BF16),
    ]
    out_specs = [
        pl.BlockSpec((nb, tm, 3 * HY_WIDTH), row),
        pl.BlockSpec((nb, tm, 3 * d), row),
        pl.BlockSpec((nb, GQA_HEADS * GQA_HEAD_DIM, tm), col),
        pl.BlockSpec((nb, tm, GQA_KV_HEADS * V7X_LANES), row),
        pl.BlockSpec((nb, GQA_KV_HEADS * V_ROWS, tm), col),
        pl.BlockSpec((nb, MLA_HEADS * MLA_HEAD_PAD, tm), col),
        pl.BlockSpec((nb, tm, MLA_HEADS * MLA_HEAD_PAD), row),
        pl.BlockSpec((nb, None, MLA_HEADS * V_ROWS, tm), lambda b, i: (b, i // ncs, 0, i % ncs)),
    ]
    return pl.pallas_call(
        _proj_kernel,
        grid=(bsz // nb, nt),
        in_specs=in_specs,
        out_specs=out_specs,
        out_shape=out_shape,
        compiler_params=_cparams(("parallel", "arbitrary"), 56),
        name="in_proj",
    )(xs, modsel, g, wu, wg, wt, wqb, wkvb, gq, gk, gqa, gkva, gmq, gmk, cg, sg, cm, sm)


def _short_conv_kernel(u_ref, up_ref, un_ref, w_ref, x0_ref, z_ref, *, n_tiles):
    i = pl.program_id(1)
    u = u_ref[...].astype(F32)
    tm = u.shape[0]
    prev = jnp.where(i > 0, up_ref[7:8, :].astype(F32), 0.0)
    nxt = jnp.where(i < n_tiles - 1, un_ref[0:1, :].astype(F32), 0.0)
    ridx = lax.broadcasted_iota(jnp.int32, u.shape, 0)
    up = jnp.where(ridx == 0, prev, pltpu.roll(u, 1, axis=0))
    dn = jnp.where(ridx == tm - 1, nxt, pltpu.roll(u, tm - 1, axis=0))
    uc = up * w_ref[0:1, :] + u * w_ref[1:2, :] + dn * w_ref[2:3, :]
    c = HY_WIDTH
    x0_ref[...] = uc[:, :c].astype(BF16)
    z_ref[...] = (uc[:, c:2 * c] * uc[:, 2 * c:]).astype(BF16)


def _short_conv(u, short_w, row0, rows):
    bsz, s, c3 = u.shape
    tm = math.gcd(math.gcd(rows, row0), CONV_TILE) if row0 else math.gcd(rows, CONV_TILE)
    nt = rows // tm
    t0 = row0 // tm
    r8 = tm // 8
    last8 = s // 8 - 1
    return pl.pallas_call(
        functools.partial(_short_conv_kernel, n_tiles=nt),
        grid=(bsz, nt),
        in_specs=[
            pl.BlockSpec((None, tm, c3), lambda b, i: (b, t0 + i, 0)),
            pl.BlockSpec((None, 8, c3), lambda b, i: (b, jnp.maximum((t0 + i) * r8 - 1, 0), 0)),
            pl.BlockSpec((None, 8, c3), lambda b, i: (b, jnp.minimum((t0 + i + 1) * r8, last8), 0)),
            _const_spec(short_w.shape),
        ],
        out_specs=[pl.BlockSpec((None, tm, HY_WIDTH), lambda b, i: (b, i, 0))] * 2,
        out_shape=[jax.ShapeDtypeStruct((bsz, rows, HY_WIDTH), BF16)] * 2,
        compiler_params=_cparams(("parallel", "arbitrary"), 32),
        name="hyena_short_conv",
    )(u, u, u, short_w)


def _filter_kernel(zf_ref, w1_ref, b1_ref, w2_ref, b2_ref, fr_ref, w3_ref, dl_ref, h_ref, ss_ref):
    i = pl.program_id(0)
    zf = zf_ref[...]
    tl = zf.shape[0]
    h = jnp.sin(fr_ref[0:1, :] * (_dot_hi(zf, w1_ref[...]) + b1_ref[...]))
    h = jnp.sin(fr_ref[1:2, :] * (_dot_hi(h, w2_ref[...]) + b2_ref[...]))
    h = _dot_hi(h, w3_ref[...])
    decay = jnp.exp(-zf[:, 0:1] * dl_ref[...])
    c = HY_WIDTH
    hf = h[:, :c] * decay
    ridx = lax.broadcasted_iota(jnp.int32, (tl, c), 0) + i * tl
    hb = jnp.where(ridx == 0, 0.0, h[:, c:] * decay)
    h_ref[0] = hf.astype(BF16)
    h_ref[1] = hb.astype(BF16)
    ss = jnp.sum(hf * hf + hb * hb, axis=0, keepdims=True)

    @pl.when(i == 0)
    def _():
        ss_ref[...] = ss

    @pl.when(i > 0)
    def _():
        ss_ref[...] += ss


def _filter(zfeat, fw, deltas):
    length = zfeat.shape[0]
    tl = min(length, 1024)
    w1, b1, w2, b2, fr, w3 = fw
    return pl.pallas_call(
        _filter_kernel,
        grid=(length // tl,),
        in_specs=[pl.BlockSpec((tl, zfeat.shape[1]), lambda i: (i, 0))]
        + [_const_spec(a.shape) for a in (w1, b1, w2, b2, fr, w3, deltas)],
        out_specs=[pl.BlockSpec((2, tl, HY_WIDTH), lambda i: (0, i, 0)),
                   pl.BlockSpec((1, HY_WIDTH), lambda i: (0, 0))],
        out_shape=[jax.ShapeDtypeStruct((2, length, HY_WIDTH), BF16),
                   jax.ShapeDtypeStruct((1, HY_WIDTH), F32)],
        compiler_params=_cparams(("arbitrary",), 40),
        name="hyena_filter",
    )(zfeat, w1, b1, w2, b2, fr, w3, deltas)


def _fft_a_kernel(g_ref, x_ref, o_ref, *, nb, cw):
    for j in range(nb):
        sl = slice(j * cw, (j + 1) * cw)
        x = jnp.concatenate([x_ref[p, :, sl] for p in range(x_ref.shape[0])], axis=0)
        o_ref[:, sl] = _dot(g_ref[j], x).astype(o_ref.dtype)


def _fft_a(x3d, gmat):
    bx, parts, k1, w = x3d.shape
    n2, two_n1, _ = gmat.shape
    cw = w // n2
    nb = FFT_NB
    return pl.pallas_call(
        functools.partial(_fft_a_kernel, nb=nb, cw=cw),
        grid=(bx, n2 // nb),
        in_specs=[pl.BlockSpec((nb, two_n1, parts * k1), lambda b, j: (j, 0, 0)),
                  pl.BlockSpec((None, parts, k1, nb * cw), lambda b, j: (b, 0, 0, j))],
        out_specs=pl.BlockSpec((None, two_n1, nb * cw), lambda b, j: (b, 0, j)),
        out_shape=jax.ShapeDtypeStruct((bx, two_n1, w), BF16),
        compiler_params=_cparams(("parallel", "arbitrary"), 40),
        name="fft_stage_a",
    )(gmat, x3d)


def _fft_filter_b_kernel(fb_ref, a_ref, ss_ref, kf_ref, *, inv_n):
    n = a_ref.shape[3]
    rs = lax.rsqrt(ss_ref[...] + EPS) * inv_n
    for r in range(a_ref.shape[2]):
        xf = _dot(fb_ref[...], jnp.concatenate([a_ref[0, 0, r], a_ref[0, 1, r]], axis=0))
        xb = _dot(fb_ref[...], jnp.concatenate([a_ref[1, 0, r], a_ref[1, 1, r]], axis=0))
        kf_ref[0, r] = (xf[:n] + xb[:n]) * rs
        kf_ref[1, r] = (xf[n:] - xb[n:]) * rs


def _fft_filter_b(a5, fb, ssq, inv_n):
    _, _, n1, n2, c = a5.shape
    kb = min(FFT_KB, n1)
    return pl.pallas_call(
        functools.partial(_fft_filter_b_kernel, inv_n=inv_n),
        grid=(n1 // kb,),
        in_specs=[_const_spec(fb.shape),
                  pl.BlockSpec((2, 2, kb, n2, c), lambda k: (0, 0, k, 0, 0)),
                  _const_spec(ssq.shape)],
        out_specs=pl.BlockSpec((2, kb, n2, c), lambda k: (0, k, 0, 0)),
        out_shape=jax.ShapeDtypeStruct((2, n1, n2, c), F32),
        compiler_params=_cparams(("arbitrary",), 32),
        name="fft_filter_stage_b",
    )(fb, a5, ssq)


def _fft_b_kernel(fb_ref, fbi_ref, a_ref, kf_ref, o_ref):
    kb, n, cw = a_ref.shape[1:]
    a_all = jnp.concatenate([jnp.concatenate([a_ref[0, r], a_ref[1, r]], axis=0) for r in range(kb)], axis=1)
    x = _dot(fb_ref[...], a_all)
    ys = []
    for r in range(kb):
        xr, xi = x[:n, r * cw:(r + 1) * cw], x[n:, r * cw:(r + 1) * cw]
        kr, ki = kf_ref[0, r], kf_ref[1, r]
        ys.append(jnp.concatenate([xr * kr - xi * ki, xr * ki + xi * kr], axis=0).astype(BF16))
    c = _dot(fbi_ref[...], jnp.concatenate(ys, axis=1))
    for r in range(kb):
        o_ref[0, r] = c[:n, r * cw:(r + 1) * cw].astype(o_ref.dtype)
        o_ref[1, r] = c[n:, r * cw:(r + 1) * cw].astype(o_ref.dtype)


def _fft_b(a5, kf, fb, fbi):
    bsz, _, n1, n2, c = a5.shape
    kb = min(FFT_KB, n1)
    return pl.pallas_call(
        _fft_b_kernel,
        grid=(n1 // kb, bsz),
        in_specs=[_const_spec(fb.shape), _const_spec(fbi.shape),
                  pl.BlockSpec((None, 2, kb, n2, c), lambda k, b: (b, 0, k, 0, 0)),
                  pl.BlockSpec((2, kb, n2, c), lambda k, b: (0, k, 0, 0))],
        out_specs=pl.BlockSpec((None, 2, kb, n2, c), lambda k, b: (b, 0, k, 0, 0)),
        out_shape=jax.ShapeDtypeStruct(a5.shape, BF16),
        compiler_params=_cparams(("arbitrary", "arbitrary"), 32),
        name="fft_stage_b",
    )(fb, fbi, a5, kf)


def _fft_c_kernel(h_ref, c_ref, o_ref, *, nb, cw):
    parts, k1 = o_ref.shape[0], o_ref.shape[1]
    for j in range(nb):
        sl = slice(j * cw, (j + 1) * cw)
        y = _dot(h_ref[j], c_ref[:, sl])
        for p in range(parts):
            o_ref[p, :, sl] = y[p * k1:(p + 1) * k1].astype(o_ref.dtype)


def _fft_c(c2d, hmat, parts):
    bx, two_n1, w = c2d.shape
    n2, rows, _ = hmat.shape
    k1 = rows // parts
    cw = w // n2
    nb = FFT_NB
    return pl.pallas_call(
        functools.partial(_fft_c_kernel, nb=nb, cw=cw),
        grid=(bx, n2 // nb),
        in_specs=[pl.BlockSpec((nb, rows, two_n1), lambda b, j: (j, 0, 0)),
                  pl.BlockSpec((None, two_n1, nb * cw), lambda b, j: (b, 0, j))],
        out_specs=pl.BlockSpec((None, parts, k1, nb * cw), lambda b, j: (b, 0, 0, j)),
        out_shape=jax.ShapeDtypeStruct((bx, parts, k1, w), BF16),
        compiler_params=_cparams(("parallel", "arbitrary"), 40),
        name="fft_stage_c",
    )(hmat, c2d)


def _ctx_conv_kernel(fc_ref, fci_ref, h_ref, ss_ref, z_ref, o_ref):
    n = fc_ref.shape[0] // 2
    fc = fc_ref[...]
    kf = _dot(fc, h_ref[0])
    kb = _dot(fc, h_ref[1])
    rs = lax.rsqrt(ss_ref[...] + EPS)
    kr = (kf[:n] + kb[:n]) * rs
    ki = (kf[n:] - kb[n:]) * rs
    x = _dot(fc, z_ref[...])
    xr, xi = x[:n], x[n:]
    y = jnp.concatenate([xr * kr - xi * ki, xr * ki + xi * kr], axis=0).astype(BF16)
    o_ref[...] = _dot(fci_ref[...], y).astype(o_ref.dtype)


def _ctx_conv(fc, fci, hfb, ssq, z):
    bsz, lc, c = z.shape
    blk = pl.BlockSpec((None, lc, c), lambda b: (b, 0, 0))
    return pl.pallas_call(
        _ctx_conv_kernel,
        grid=(bsz,),
        in_specs=[_const_spec(fc.shape), _const_spec(fci.shape), _const_spec(hfb.shape),
                  _const_spec(ssq.shape), blk],
        out_specs=blk,
        out_shape=jax.ShapeDtypeStruct((bsz, lc, c), BF16),
        compiler_params=_cparams(("arbitrary",), 32),
        name="hyena_ctx_conv",
    )(fc, fci, hfb, ssq, z)


@functools.lru_cache(maxsize=None)
def _fft_tables(length):
    n = 2 * length
    n2 = FFT_N2
    n1 = n // n2
    k1 = length // n2
    kk = np.arange(n1)[:, None]
    g = np.empty((n2, 2 * n1, k1), np.float64)
    g2 = np.empty((n2, 2 * n1, 2 * k1), np.float64)
    h2 = np.empty((n2, 2 * k1, 2 * n1), np.float64)
    nn = np.arange(k1)[None, :]
    for j in range(n2):
        ang = 2.0 * np.pi * (((n2 * nn * kk) % n) + (j * kk) % n) / n
        c_, s_ = np.cos(ang), np.sin(ang)
        g[j] = np.concatenate([c_, -s_], axis=0)
        g2[j] = np.block([[c_, s_], [-s_, c_]])
        h2[j] = np.block([[c_.T, -s_.T], [s_.T, c_.T]])
    a = np.arange(n2)
    ph = 2.0 * np.pi * ((a[:, None] * a[None, :]) % n2) / n2
    c, s = np.cos(ph), np.sin(ph)
    fb = np.block([[c, s], [-s, c]])
    fbi = np.block([[c, -s], [s, c]])
    return (jnp.asarray(g, BF16), jnp.asarray(g2, BF16), jnp.asarray(h2, BF16), jnp.asarray(fb, BF16),
            jnp.asarray(fbi, BF16), n1, k1)


@functools.lru_cache(maxsize=None)
def _dft_tables(length):
    n = 2 * length
    k = np.arange(n)[:, None]
    t = np.arange(length)[None, :]
    ang = 2.0 * np.pi * ((k * t) % n) / n
    fc = np.concatenate([np.cos(ang), -np.sin(ang)], axis=0)
    fci = np.concatenate([np.cos(ang).T, -np.sin(ang).T], axis=1) / n
    return jnp.asarray(fc, BF16), jnp.asarray(fci, BF16)


@functools.lru_cache(maxsize=None)
def _filter_features(length):
    t = np.linspace(0.0, 1.0, length, dtype=np.float32)[:, None]
    w = (2.0 * math.pi * np.arange(length, dtype=np.float32)[:, None] / length).astype(np.float32)
    f = np.linspace(1e-4, HY_BANDS - 1, HY_BANDS, dtype=np.float32)[None, :]
    z = np.concatenate([t, np.cos(f * w), -np.sin(f * w)], axis=-1).astype(np.float32)
    zp = np.zeros((length, V7X_LANES), np.float32)
    zp[:, :HY_EMB_DIM] = z
    return jnp.asarray(zp)


def _hyena_deltas():
    max_decay = math.log(HY_DECAY_TARGET) / HY_FAST_DECAY
    min_decay = math.log(HY_DECAY_TARGET) / HY_SLOW_DECAY
    return jnp.abs(jnp.linspace(min_decay, max_decay, HY_WIDTH, dtype=F32))[None, :]


def _hyena(u, short_w, fw, lat, lc):
    bsz = u.shape[0]
    c = HY_WIDTH
    deltas = _hyena_deltas()
    x0, z = _short_conv(u, short_w, 0, lat)
    gmat, gmat2, hmat2, fb, fbi, n1, k1 = _fft_tables(lat)
    n2 = FFT_N2
    hfb, ssq = _filter(_filter_features(lat), fw, deltas)
    fa = _fft_a(hfb.reshape(2, 1, k1, n2 * c), gmat)
    kf = _fft_filter_b(fa.reshape(2, 2, n1, n2, c), fb, ssq, 1.0 / (2 * lat))
    npair = (bsz + 1) // 2
    zp = z if bsz % 2 == 0 else jnp.concatenate([z, jnp.zeros_like(z[:1])], axis=0)
    za = _fft_a(zp.reshape(npair, 2, k1, n2 * c), gmat2)
    zc = _fft_b(za.reshape(npair, 2, n1, n2, c), kf, fb, fbi)
    conv = _fft_c(zc.reshape(npair, 2 * n1, n2 * c), hmat2, 2).reshape(2 * npair, lat, c)[:bsz]
    x0_c, z_c = _short_conv(u, short_w, lat, lc)
    hfb_c, ssq_c = _filter(_filter_features(lc), fw, deltas)
    fc, fci = _dft_tables(lc)
    conv_c = _ctx_conv(fc, fci, hfb_c, ssq_c, z_c)
    return (conv, x0, z), (conv_c, x0_c, z_c)


def _gqa_kernel(q_ref, k_ref, v_ref, sink_ref, o_ref, *bufs, lat, lc, tq):
    sub = q_ref.shape[1] // tq
    for t in range(sub):
        lanes = slice(t * tq, (t + 1) * tq)
        _gqa_tile(pl.program_id(1) * sub + t, q_ref.at[:, lanes], k_ref, v_ref, sink_ref, o_ref.at[:, lanes], bufs,
                  lat, lc)


def _gqa_tile(i, q_ref, k_ref, v_ref, sink_ref, o_ref, bufs, lat, lc):
    tq = q_ref.shape[1]
    s_len = k_ref.shape[0]
    hd = GQA_HEAD_DIM
    w = WINDOW
    nwin = tq + 2 * w
    start = pl.multiple_of(jnp.clip(i * tq - w, 0, s_len - nwin), w)
    k_all = jnp.concatenate([k_ref[pl.ds(start, nwin), :], k_ref[lat:lat + lc, :]], axis=0)
    v_all = jnp.concatenate([v_ref[:, pl.ds(start, nwin)], v_ref[:, lat:lat + lc]], axis=1)
    q_pos = i * tq + lax.broadcasted_iota(jnp.int32, (1, tq), 1)
    k_pos = start + lax.broadcasted_iota(jnp.int32, (nwin, 1), 0)
    bias = (jnp.where(jnp.abs(k_pos - q_pos) <= w, 0.0, NEG_BIG)
            + jnp.where(k_pos < lat, 0.0, NEG_BIG)
            + jnp.where(q_pos < lat, 0.0, NEG_BIG))
    bias = jnp.concatenate([bias, jnp.zeros((lc, tq), F32)], axis=0)
    bias2 = jnp.concatenate([bias, bias], axis=1)
    zq = jnp.zeros((V7X_LANES - hd, 2 * tq), BF16)

    def score(pair, dst):
        g = pair // (GQA_GROUP // 2)
        r = 2 * pair * hd
        q2 = jnp.concatenate([q_ref[r:r + hd, :], q_ref[r + hd:r + 2 * hd, :]], axis=1)
        s = _dot(k_all[:, g * V7X_LANES:(g + 1) * V7X_LANES], jnp.concatenate([q2, zq], axis=0)) + bias2
        dst[...] = s
        return jnp.max(s, axis=0, keepdims=True)

    def update(pair, s_sc, s_max):
        g = pair // (GQA_GROUP // 2)
        r = 2 * pair * hd
        sink = jnp.concatenate([sink_ref[2 * pair:2 * pair + 1, :], sink_ref[2 * pair + 1:2 * pair + 2, :]],
                               axis=1) * LOG2E
        m = jnp.maximum(s_max, sink)
        p = jnp.exp2(s_sc[...] - m).astype(BF16)
        pv = _dot(v_all[g * V_ROWS:(g + 1) * V_ROWS, :], p)
        o = pv[:hd] / (pv[hd:hd + 1] + jnp.exp2(sink - m))
        o_ref[r:r + hd, :] = o[:, :tq].astype(o_ref.dtype)
        o_ref[r + hd:r + 2 * hd, :] = o[:, tq:].astype(o_ref.dtype)

    npair = GQA_HEADS // 2
    mx = [score(j, bufs[j]) for j in range(npair)]
    for j in range(npair):
        update(j, bufs[j], mx[j])


def _gqa(qg, kg, vg, sink_rows, lat, lc):
    bsz, nq, s = qg.shape
    tq = GQA_TQ
    sub = GQA_SUBTILES if (s // tq) % GQA_SUBTILES == 0 else 1
    wq = tq * sub
    return pl.pallas_call(
        functools.partial(_gqa_kernel, lat=lat, lc=lc, tq=tq),
        grid=(bsz, s // wq),
        in_specs=[
            pl.BlockSpec((None, nq, wq), lambda b, i: (b, 0, i)),
            pl.BlockSpec((None, s, kg.shape[2]), lambda b, i: (b, 0, 0)),
            pl.BlockSpec((None, vg.shape[1], s), lambda b, i: (b, 0, 0)),
            _const_spec(sink_rows.shape),
        ],
        out_specs=pl.BlockSpec((None, nq, wq), lambda b, i: (b, 0, i)),
        out_shape=jax.ShapeDtypeStruct((bsz, nq, s), BF16),
        scratch_shapes=[pltpu.VMEM((tq + 2 * WINDOW + lc, 2 * tq), F32)] * (GQA_HEADS // 2),
        compiler_params=_cparams(("parallel", "arbitrary"), 40),
        name="gqa_window_attn",
    )(qg, kg, vg, sink_rows)


def _mla_update(s_ref, s_max, vt, m, acc):
    m_new = jnp.maximum(m, s_max)
    alpha = jnp.exp2(m - m_new)
    p = jnp.exp2(s_ref[...] - m_new).astype(BF16)
    acc = alpha * acc + _dot(vt, p)
    return m_new, acc


def _mla_kernel(q_ref, k_ref, v_ref, o_ref, *bufs, nc, ctx_only, tq):
    for t in range(q_ref.shape[1] // tq):
        lanes = slice(t * tq, (t + 1) * tq)
        _mla_tile(q_ref[:, lanes], k_ref, v_ref, o_ref.at[:, lanes], bufs, nc, ctx_only)


def _mla_tile(q, k_ref, v_ref, o_ref, bufs, nc, ctx_only):
    tq = q.shape[1]
    ck = KV_CHUNK
    nbuf = len(bufs)
    m = jnp.full((1, tq), NEG_BIG, F32)
    acc = jnp.zeros((V_ROWS, tq), F32)

    def score(k, dst):
        s = _dot(k, q)
        dst[...] = s
        return jnp.max(s, axis=0, keepdims=True)

    def score_chunk(j, dst):
        return score(k_ref[pl.ds(pl.multiple_of(j * ck, ck), ck), :], dst)

    if ctx_only:
        s_x = bufs[0].at[0:KV_SUB, :]
        mx = score(k_ref[k_ref.shape[0] - KV_SUB:, :], s_x)
        m, acc = _mla_update(s_x, mx, v_ref[v_ref.shape[0] - 1][:, ck - KV_SUB:], m, acc)
    else:
        ahead = nbuf - 1
        mx = [None] * nbuf
        for j in range(min(ahead, nc)):
            mx[j] = score_chunk(j, bufs[j])

        def step(j, slot, m, acc, mx, last):
            if not last:
                nxt = (slot + ahead) % nbuf
                mx[nxt] = score_chunk(j + ahead, bufs[nxt])
            return _mla_update(bufs[slot], mx[slot], v_ref[j], m, acc)

        per_body = MLA_UNROLL * nbuf

        def body(g, c):
            m, acc, mx = c[0], c[1], list(c[2:])
            for u in range(per_body):
                m, acc = step(g * per_body + u, u % nbuf, m, acc, mx, False)
            return (m, acc, *mx)

        n_body = max(nc - ahead, 0) // per_body
        if n_body > 0:
            out = lax.fori_loop(0, n_body, body, (m, acc, *[m if x is None else x for x in mx]))
            m, acc, mx = out[0], out[1], list(out[2:])
        for j in range(n_body * per_body, nc):
            m, acc = step(j, j % nbuf, m, acc, mx, j + ahead >= nc)
    o_ref[...] = (acc[:MLA_V_DIM] / acc[MLA_V_DIM:MLA_V_DIM + 1]).astype(o_ref.dtype)


def _mla_call(qm, km, vm, tq, sub, q_tile0, n_q, ctx_only):
    bsz, _, s = qm.shape
    nc = vm.shape[1]
    wq = tq * sub
    if ctx_only:
        k_spec = pl.BlockSpec((None, KV_SUB, MLA_HEAD_PAD), lambda b, h, i: (b, s // KV_SUB - 1, h))
        v_spec = pl.BlockSpec((None, 1, V_ROWS, KV_CHUNK), lambda b, h, i: (b, nc - 1, h, 0))
    else:
        k_spec = pl.BlockSpec((None, s, MLA_HEAD_PAD), lambda b, h, i: (b, 0, h))
        v_spec = pl.BlockSpec((None, nc, V_ROWS, KV_CHUNK), lambda b, h, i: (b, 0, h, 0))
    return pl.pallas_call(
        functools.partial(_mla_kernel, nc=nc, ctx_only=ctx_only, tq=tq),
        grid=(bsz, MLA_HEADS, n_q),
        in_specs=[pl.BlockSpec((None, MLA_HEAD_PAD, wq), lambda b, h, i: (b, h, q_tile0 + i)), k_spec, v_spec],
        out_specs=pl.BlockSpec((None, MLA_V_DIM, wq), lambda b, h, i: (b, h, i)),
        out_shape=jax.ShapeDtypeStruct((bsz, MLA_HEADS * MLA_V_DIM, n_q * wq), BF16),
        scratch_shapes=[pltpu.VMEM((KV_CHUNK, tq), F32)] * MLA_NBUF,
        compiler_params=_cparams(("parallel", "arbitrary", "arbitrary"), 40),
        name="mla_attn_ctx" if ctx_only else "mla_attn",
    )(qm, km, vm)


def _mla(qm, km, vm, lat, lc):
    tq = min(MLA_TQ, lat)
    sub = MLA_SUBTILES if lat % (tq * MLA_SUBTILES) == 0 else 1
    assert lc == KV_SUB and lat % lc == 0 and lat % tq == 0
    return (_mla_call(qm, km, vm, tq, sub, 0, lat // (tq * sub), False),
            _mla_call(qm, km, vm, lc, 1, lat // lc, 1, True))


def _merge_mlp_kernel(x_ref, mod_ref, hl_refs, hc_refs, sk_ref, yg_ref, yml_ref, ymc_ref, gt_ref, wb_ref, wo_ref,
                      g_ref, w1_ref, w2_ref, o_ref, *, lat_tiles):
    for n in range(x_ref.shape[0]):
        _merge_mlp_one(x_ref.at[n], mod_ref.at[n], [r.at[n] for r in hl_refs], [r.at[n] for r in hc_refs], sk_ref,
                       yg_ref.at[n], yml_ref.at[n], ymc_ref.at[n], gt_ref.at[n], wb_ref, wo_ref, g_ref, w1_ref,
                       w2_ref, o_ref.at[n], lat_tiles)


def _merge_mlp_one(x_ref, mod_ref, hl_refs, hc_refs, sk_ref, yg_ref, yml_ref, ymc_ref, gt_ref, wb_ref, wo_ref,
                   g_ref, w1_ref, w2_ref, o_ref, lat_tiles):
    d = D_MODEL
    is_lat = pl.program_id(1) < lat_tiles
    conv, x0, z = [jnp.where(is_lat, a[...], b[...]).astype(F32) for a, b in zip(hl_refs, hc_refs)]
    yh = (x0 * (conv + z * sk_ref[...])).astype(BF16)
    yg = yg_ref[...].astype(F32).T.astype(BF16)
    ym = jnp.where(is_lat, yml_ref[...], ymc_ref[...]).astype(F32).T.astype(BF16)
    merged = (gt_ref[:, 0:d].astype(F32) * _dot(yh, wb_ref[0])
              + gt_ref[:, d:2 * d].astype(F32) * _dot(yg, wb_ref[1])
              + gt_ref[:, 2 * d:].astype(F32) * _dot(ym, wb_ref[2]))
    res = _dot(merged.astype(BF16), wo_ref[...])
    x_mix = x_ref[...] + mod_ref[:, 2 * d:3 * d] * res
    o_ref[...] = _mlp_rows(x_mix, mod_ref, g_ref, w1_ref, w2_ref)


def _merge_mlp(xs, modsel, hy_lat, hy_ctx, skip, yg, ym_lat, ym_ctx, gates, wb, wo, g, w1, w2, lat_tiles, n_tiles):
    bsz, _, d = xs.shape
    tm = ROW_TILE
    c = HY_WIDTH
    nb = MERGE_BATCH if bsz % MERGE_BATCH == 0 else 1
    row = lambda b, i: (b, i, 0)
    col = lambda b, i: (b, 0, i)
    lat_row = pl.BlockSpec((nb, tm, c), lambda b, i: (b, jnp.minimum(i, lat_tiles - 1), 0))
    ctx_row = pl.BlockSpec((nb, tm, c), lambda b, i: (b, jnp.maximum(i - lat_tiles, 0), 0))
    return pl.pallas_call(
        functools.partial(_merge_mlp_kernel, lat_tiles=lat_tiles),
        grid=(bsz // nb, n_tiles),
        in_specs=[
            pl.BlockSpec((nb, tm, d), row),
            pl.BlockSpec((nb, None, 1, 6 * d), lambda b, i: (b, jnp.where(i >= lat_tiles, 1, 0), 0, 0)),
            [lat_row] * 3, [ctx_row] * 3, _const_spec(skip.shape),
            pl.BlockSpec((nb, c, tm), col),
            pl.BlockSpec((nb, c, tm), lambda b, i: (b, 0, jnp.minimum(i, lat_tiles - 1))),
            pl.BlockSpec((nb, c, tm), lambda b, i: (b, 0, jnp.maximum(i - lat_tiles, 0))),
            pl.BlockSpec((nb, tm, 3 * d), row),
            _const_spec(wb.shape), _const_spec(wo.shape),
            _const_spec(g.shape), _const_spec(w1.shape), _const_spec(w2.shape),
        ],
        out_specs=pl.BlockSpec((nb, tm, d), row),
        out_shape=jax.ShapeDtypeStruct((bsz, n_tiles * tm, d), F32),
        compiler_params=_cparams(("parallel", "arbitrary"), 56),
        name="merge_mlp",
    )(xs, modsel, list(hy_lat), list(hy_ctx), skip, yg, ym_lat, ym_ctx, gates, wb, wo, g, w1, w2)


def _mlp_rows(x, mod_ref, g_ref, w1_ref, w2_ref):
    d = D_MODEL
    xn = x * lax.rsqrt(jnp.mean(x * x, axis=-1, keepdims=True) + EPS) * g_ref[...]
    h = (xn * (1.0 + mod_ref[:, 4 * d:5 * d]) + mod_ref[:, 3 * d:4 * d]).astype(BF16)
    acc = jnp.zeros(x.shape, F32)
    for j in range(D_FF // FF_CHUNK):
        sl = slice(j * FF_CHUNK, (j + 1) * FF_CHUNK)
        a = jnp.maximum(_dot(h, w1_ref[:, sl]), 0.0)
        acc = acc + _dot((a * a).astype(BF16), w2_ref[sl, :])
    return x + mod_ref[:, 5 * d:] * acc


def _rope_tables_t(rows, dim, lc):
    n_freq = dim // 4
    inv = ROPE_BASE ** (-jnp.arange(n_freq, dtype=F32) / n_freq)
    r = jnp.repeat(jnp.arange(rows, dtype=F32), GRID_W)
    col = jnp.tile(jnp.arange(GRID_W, dtype=F32), rows)
    ang = jnp.concatenate([r[:, None] * inv, col[:, None] * inv], axis=-1)
    cos_t = jnp.concatenate([jnp.cos(ang).T, jnp.ones((dim // 2, lc), F32)], axis=1)
    sin_t = jnp.concatenate([jnp.sin(ang).T, jnp.zeros((dim // 2, lc), F32)], axis=1)
    return cos_t, sin_t


def _lane_bcast(v):
    return jnp.broadcast_to(v.astype(F32)[:, None], (v.shape[0], ROW_TILE))


def _layer_weights(l, w_in, gqa_q_norm, gqa_k_norm, mla_q_a_norm, mla_kv_a_norm, w_q_b, w_kv_b, mla_q_norm, mla_k_norm):
    w = w_in[l]
    o = np.cumsum([0, 3 * HY_WIDTH, GQA_HEADS * GQA_HEAD_DIM, GQA_KV_HEADS * GQA_HEAD_DIM,
                   GQA_KV_HEADS * GQA_HEAD_DIM, MLA_Q_RANK, MLA_KV_RANK, MLA_ROPE_DIM, 3 * D_MODEL])
    wu = w[:, o[0]:o[1]].astype(BF16)
    wt = w[:, o[1]:o[7]].T.astype(BF16)
    wg = w[:, o[7]:o[8]].astype(BF16)
    wqb = w_q_b[l].T.astype(BF16)
    wkvb = w_kv_b[l].T.astype(BF16)
    return (wu, wg, wt, wqb, wkvb,
            _lane_bcast(gqa_q_norm[l]), _lane_bcast(gqa_k_norm[l]),
            _lane_bcast(mla_q_a_norm[l]), _lane_bcast(mla_kv_a_norm[l]),
            _lane_bcast(mla_q_norm[l]), _lane_bcast(mla_k_norm[l]))


def kernel(x, c, ctx, c_ctx, w_mod, b_mod, norm_mix_g, norm_mlp_g, w_in, hy_short_w, hy_f1_w, hy_f1_b, hy_f2_w, hy_f2_b, hy_sin_freq, hy_f3_w, hy_skip, gqa_q_norm, gqa_k_norm, gqa_sink, mla_q_a_norm, mla_kv_a_norm, w_q_b, w_kv_b, mla_q_norm, mla_k_norm, w_branch, w_out, w_mlp1, w_mlp2):
    bsz, lat, d = x.shape
    lc = ctx.shape[1]
    depth = w_mod.shape[0]
    s = lat + lc
    assert d == D_MODEL and lat % GRID_W == 0 and lat % ROW_TILE == 0 and lc == ROW_TILE and s % KV_CHUNK == 0
    lat_tiles = lat // ROW_TILE

    pad = (-(bsz + 1)) % 8
    cond = jnp.concatenate([c, c_ctx[None, :], jnp.zeros((pad, d), F32)], axis=0)
    mods = _mod_all(cond, w_mod, b_mod)

    tabs = _rope_tables_t(lat // GRID_W, GQA_HEAD_DIM, lc) + _rope_tables_t(lat // GRID_W, MLA_ROPE_DIM, lc)
    xs = jnp.concatenate([x, ctx], axis=1)

    for l in range(depth):
        ml = mods[l]
        modsel = jnp.stack([ml[:bsz], jnp.broadcast_to(ml[bsz][None], (bsz, 6 * d))], axis=1)[:, :, None, :]
        wts = _layer_weights(l, w_in, gqa_q_norm, gqa_k_norm, mla_q_a_norm, mla_kv_a_norm, w_q_b, w_kv_b,
                             mla_q_norm, mla_k_norm)
        u, gates, qg, kg, vg, qm, km, vm = _proj(xs, modsel, norm_mix_g[l][None, :], wts, tabs, lat_tiles)

        f1w = jnp.zeros((V7X_LANES, HY_FILTER_WIDTH), F32).at[:HY_EMB_DIM].set(hy_f1_w[l])
        fw = (f1w, hy_f1_b[l][None, :], hy_f2_w[l], hy_f2_b[l][None, :], hy_sin_freq[l], hy_f3_w[l])
        hy_lat, hy_ctx = _hyena(u, hy_short_w[l], fw, lat, lc)

        sink_rows = jnp.broadcast_to(gqa_sink[l].astype(F32)[:, None], (GQA_HEADS, GQA_TQ))
        yg = _gqa(qg, kg, vg, sink_rows, lat, lc)
        ym_lat, ym_ctx = _mla(qm, km, vm, lat, lc)

        n_tiles = lat_tiles if l == depth - 1 else s // ROW_TILE
        xs = _merge_mlp(xs, modsel, hy_lat, hy_ctx, hy_skip[l][None, :], yg, ym_lat, ym_ctx, gates,
                        w_branch[l].astype(BF16), w_out[l].astype(BF16), norm_mlp_g[l][None, :],
                        w_mlp1[l].astype(BF16), w_mlp2[l].astype(BF16), lat_tiles, n_tiles)
    return xs
```

```python
import functools
import math

import numpy as np
import jax
import jax.numpy as jnp
from jax import lax
from jax.experimental import pallas as pl
from jax.experimental.pallas import tpu as pltpu

D_MODEL = 1024
GRID_W = 64
HY_WIDTH = 512
HY_EMB_DIM = 33
HY_BANDS = (HY_EMB_DIM - 1) // 2
HY_FILTER_WIDTH = 64
HY_DECAY_TARGET = 1e-2
HY_FAST_DECAY = 0.3
HY_SLOW_DECAY = 1.5
GQA_HEADS = 8
GQA_KV_HEADS = 2
GQA_GROUP = GQA_HEADS // GQA_KV_HEADS
GQA_HEAD_DIM = 64
GQA_SCALE = GQA_HEAD_DIM ** -0.5
WINDOW = 128
MLA_HEADS = 8
MLA_Q_RANK = 384
MLA_KV_RANK = 256
MLA_NOPE_DIM = 64
MLA_ROPE_DIM = 32
MLA_V_DIM = 64
MLA_QK_DIM = MLA_NOPE_DIM + MLA_ROPE_DIM
MLA_SCALE = MLA_QK_DIM ** -0.5
D_FF = 4 * D_MODEL
ROPE_BASE = 10000.0
EPS = 1e-6
LOG2E = 1.4426950408889634
NEG_BIG = -1e30

V7X_LANES = 128
V7X_VMEM_BYTES = 64 * 1024 * 1024

ROW_TILE = 256
PROJ_BATCH = 2
MERGE_BATCH = 2
CONV_TILE = 512
MLA_HEAD_PAD = 128
V_ROWS = 80
KV_CHUNK = ROW_TILE
KV_SUB = 256
MLA_NBUF = 3
MLA_UNROLL = 5
MLA_TQ = 1024
MLA_SUBTILES = 4
GQA_TQ = 128
GQA_SUBTILES = 6
FF_CHUNK = 1024
FFT_N2 = 128
FFT_NB = 16
FFT_KB = 8

F32 = jnp.float32
BF16 = jnp.bfloat16


def _cparams(sem, vmem_mb):
    limit = vmem_mb * 1024 * 1024
    assert limit < V7X_VMEM_BYTES
    return pltpu.CompilerParams(dimension_semantics=sem, vmem_limit_bytes=limit)


def _dot(a, b):
    return jnp.dot(a, b, preferred_element_type=F32)


def _dot_hi(a, b):
    return jnp.dot(a, b, preferred_element_type=F32, precision=lax.Precision.HIGHEST)


def _ones_row_block(width):
    r = lax.broadcasted_iota(jnp.int32, (V_ROWS - MLA_V_DIM, width), 0)
    return jnp.where(r == 0, 1.0, 0.0).astype(BF16)


def _const_spec(shape):
    nd = len(shape)
    return pl.BlockSpec(shape, lambda *_: (0,) * nd, pipeline_mode=pl.Buffered(1))


def _mod_kernel(c_ref, w_ref, b_ref, o_ref):
    c = c_ref[...]
    s = c * jax.nn.sigmoid(c)
    o_ref[...] = _dot_hi(s, w_ref[...]) + b_ref[...]


def _mod_all(cond, w_mod, b_mod):
    depth, d, n = w_mod.shape
    rows = cond.shape[0]
    tn = 1536
    return pl.pallas_call(
        _mod_kernel,
        grid=(depth, n // tn),
        in_specs=[
            pl.BlockSpec((rows, d), lambda l, j: (0, 0)),
            pl.BlockSpec((None, d, tn), lambda l, j: (l, 0, j)),
            pl.BlockSpec((None, 1, tn), lambda l, j: (l, 0, j)),
        ],
        out_specs=pl.BlockSpec((None, rows, tn), lambda l, j: (l, 0, j)),
        out_shape=jax.ShapeDtypeStruct((depth, rows, n), F32),
        compiler_params=_cparams(("arbitrary", "arbitrary"), 40),
        name="adaln_mod",
    )(cond, w_mod, b_mod.reshape(depth, 1, n))


def _rms_rows(x, n):
    return lax.rsqrt(jnp.sum(x * x, axis=0, keepdims=True) * (1.0 / n) + EPS)


def _rope_rows(x1, x2, cs, sn):
    return x1 * cs - x2 * sn, x1 * sn + x2 * cs


def _stream_specs(nb, tm, d, n_main):
    return (pl.BlockSpec((nb, tm, d), lambda b, i: (b, jnp.minimum(i, n_main - 1), 0)),
            pl.BlockSpec((nb, tm, d), lambda b, i: (b, jnp.maximum(i - n_main, 0), 0)))


def _proj_kernel(xa_ref, xb_ref, mod_ref, *refs, n_main):
    consts, outs = refs[:16], refs[16:]
    from_main = pl.program_id(1) < n_main
    for n in range(xa_ref.shape[0]):
        x = jnp.where(from_main, xa_ref[n], xb_ref[n])
        _proj_one(x, mod_ref.at[n], *consts, *[o.at[n] for o in outs])


def _proj_one(x, mod_ref, g_ref, wu_ref, wg_ref, wt_ref, wqb_ref, wkvb_ref,
              gq_ref, gk_ref, gqa_ref, gkva_ref, gmq_ref, gmk_ref,
              cg_ref, sg_ref, cm_ref, sm_ref,
              u_ref, gate_ref, qg_ref, kg_ref, vg_ref, qm_ref, km_ref, vm_ref):
    d = D_MODEL
    tm = x.shape[0]
    shift = mod_ref[:, 0:d]
    scale = mod_ref[:, d:2 * d]
    xn = x * lax.rsqrt(jnp.mean(x * x, axis=-1, keepdims=True) + EPS) * g_ref[...]
    h = xn * (1.0 + scale) + shift
    hb = h.astype(BF16)

    ht = h.T.astype(BF16)
    t = _dot(wt_ref[...], ht)
    o_q, o_k, o_v = 0, 512, 640
    o_cq, o_ckv, o_kr = 768, 768 + MLA_Q_RANK, 768 + MLA_Q_RANK + MLA_KV_RANK

    u_ref[...] = _dot(hb, wu_ref[...]).astype(BF16)

    def gate_cols(n):
        cols = slice(n * d, (n + 1) * d)
        gate_ref[:, cols] = jax.nn.sigmoid(_dot(hb, wg_ref[:, cols])).astype(BF16)

    gate_cols(0)
    cg, sg = cg_ref[...], sg_ref[...]
    cm, sm = cm_ref[...], sm_ref[...]
    hd, hh = GQA_HEAD_DIM, GQA_HEAD_DIM // 2

    cq = t[o_cq:o_cq + MLA_Q_RANK]
    cqn = (cq * _rms_rows(cq, MLA_Q_RANK) * gqa_ref[...]).astype(BF16)
    qm = _dot(wqb_ref[...], cqn)
    gmq = gmq_ref[...]
    nd, rh = MLA_NOPE_DIM, MLA_ROPE_DIM // 2
    qpad = jnp.zeros((MLA_HEAD_PAD - MLA_QK_DIM, tm), BF16)
    for n in range(MLA_HEADS):
        xh = qm[n * MLA_QK_DIM:(n + 1) * MLA_QK_DIM]
        xh = xh * _rms_rows(xh, MLA_QK_DIM) * gmq * (MLA_SCALE * LOG2E)
        a, b = _rope_rows(xh[nd:nd + rh], xh[nd + rh:], cm, sm)
        base = n * MLA_HEAD_PAD
        qm_ref[base:base + nd, :] = xh[:nd].astype(BF16)
        qm_ref[base + nd:base + nd + rh, :] = a.astype(BF16)
        qm_ref[base + nd + rh:base + MLA_QK_DIM, :] = b.astype(BF16)
        qm_ref[base + MLA_QK_DIM:base + MLA_HEAD_PAD, :] = qpad

    gate_cols(1)
    ckv = t[o_ckv:o_ckv + MLA_KV_RANK]
    ckvn = (ckv * _rms_rows(ckv, MLA_KV_RANK) * gkva_ref[...]).astype(BF16)
    kv = _dot(wkvb_ref[...], ckvn)
    kr = t[o_kr:o_kr + MLA_ROPE_DIM]
    kr_ss = jnp.sum(kr * kr, axis=0, keepdims=True)
    gmk = gmk_ref[...]
    kzero = jnp.zeros((MLA_HEAD_PAD - MLA_QK_DIM, tm), F32)
    ones_rows = _ones_row_block(tm)
    parts = []
    for n in range(MLA_HEADS):
        kn = kv[n * 128:n * 128 + nd]
        vm_ref[n * V_ROWS:n * V_ROWS + MLA_V_DIM, :] = kv[n * 128 + nd:(n + 1) * 128].astype(BF16)
        vm_ref[n * V_ROWS + MLA_V_DIM:(n + 1) * V_ROWS, :] = ones_rows
        rs = lax.rsqrt((jnp.sum(kn * kn, axis=0, keepdims=True) + kr_ss) * (1.0 / MLA_QK_DIM) + EPS)
        krn = kr * rs * gmk[nd:]
        a, b = _rope_rows(krn[:rh], krn[rh:], cm, sm)
        parts += [kn * rs * gmk[:nd], a, b, kzero]
    km_ref[...] = jnp.concatenate(parts, axis=0).T.astype(BF16)

    gq = gq_ref[...]
    for n in range(GQA_HEADS):
        xh = t[o_q + n * hd:o_q + (n + 1) * hd]
        xh = xh * _rms_rows(xh, hd) * gq
        a, b = _rope_rows(xh[:hh], xh[hh:], cg, sg)
        qg_ref[n * hd:n * hd + hh, :] = (a * (GQA_SCALE * LOG2E)).astype(BF16)
        qg_ref[n * hd + hh:(n + 1) * hd, :] = (b * (GQA_SCALE * LOG2E)).astype(BF16)

    gate_cols(2)
    gk = gk_ref[...]
    zpad = jnp.zeros((V7X_LANES - hd, tm), F32)
    parts = []
    for n in range(GQA_KV_HEADS):
        xh = t[o_k + n * hd:o_k + (n + 1) * hd]
        xh = xh * _rms_rows(xh, hd) * gk
        a, b = _rope_rows(xh[:hh], xh[hh:], cg, sg)
        parts += [a, b, zpad]
    kg_ref[...] = jnp.concatenate(parts, axis=0).T.astype(BF16)
    for n in range(GQA_KV_HEADS):
        vg_ref[n * V_ROWS:n * V_ROWS + hd, :] = t[o_v + n * hd:o_v + (n + 1) * hd].astype(BF16)
        vg_ref[n * V_ROWS + hd:(n + 1) * V_ROWS, :] = ones_rows


def _proj(stream, s, modsel, g, wts, tabs, lat_tiles):
    xa, xb, n_main = stream
    bsz, _, d = xa.shape
    tm = ROW_TILE
    nt = s // tm
    ncs = KV_CHUNK // tm
    nb = PROJ_BATCH if bsz % PROJ_BATCH == 0 else 1
    (wu, wg, wt, wqb, wkvb, gq, gk, gqa, gkva, gmq, gmk) = wts
    cg, sg, cm, sm = tabs
    row = lambda b, i: (b, i, 0)
    col = lambda b, i: (b, 0, i)
    tab = lambda b, i: (0, i)
    in_specs = [
        *_stream_specs(nb, tm, d, n_main),
        pl.BlockSpec((nb, None, 1, 6 * d), lambda b, i: (b, jnp.where(i >= lat_tiles, 1, 0), 0, 0)),
        _const_spec(g.shape), _const_spec(wu.shape), _const_spec(wg.shape), _const_spec(wt.shape),
        _const_spec(wqb.shape), _const_spec(wkvb.shape),
        _const_spec(gq.shape), _const_spec(gk.shape), _const_spec(gqa.shape), _const_spec(gkva.shape),
        _const_spec(gmq.shape), _const_spec(gmk.shape),
        pl.BlockSpec((cg.shape[0], tm), tab), pl.BlockSpec((sg.shape[0], tm), tab),
        pl.BlockSpec((cm.shape[0], tm), tab), pl.BlockSpec((sm.shape[0], tm), tab),
    ]
    out_shape = [
        jax.ShapeDtypeStruct((bsz, s, 3 * HY_WIDTH), BF16),
        jax.ShapeDtypeStruct((bsz, s, 3 * d), BF16),
        jax.ShapeDtypeStruct((bsz, GQA_HEADS * GQA_HEAD_DIM, s), BF16),
        jax.ShapeDtypeStruct((bsz, s, GQA_KV_HEADS * V7X_LANES), BF16),
        jax.ShapeDtypeStruct((bsz, GQA_KV_HEADS * V_ROWS, s), BF16),
        jax.ShapeDtypeStruct((bsz, MLA_HEADS * MLA_HEAD_PAD, s), BF16),
        jax.ShapeDtypeStruct((bsz, s, MLA_HEADS * MLA_HEAD_PAD), BF16),
        jax.ShapeDtypeStruct((bsz, s // KV_CHUNK, MLA_HEADS * V_ROWS, KV_CHUNK), BF16),
    ]
    out_specs = [
        pl.BlockSpec((nb, tm, 3 * HY_WIDTH), row),
        pl.BlockSpec((nb, tm, 3 * d), row),
        pl.BlockSpec((nb, GQA_HEADS * GQA_HEAD_DIM, tm), col),
        pl.BlockSpec((nb, tm, GQA_KV_HEADS * V7X_LANES), row),
        pl.BlockSpec((nb, GQA_KV_HEADS * V_ROWS, tm), col),
        pl.BlockSpec((nb, MLA_HEADS * MLA_HEAD_PAD, tm), col),
        pl.BlockSpec((nb, tm, MLA_HEADS * MLA_HEAD_PAD), row),
        pl.BlockSpec((nb, None, MLA_HEADS * V_ROWS, tm), lambda b, i: (b, i // ncs, 0, i % ncs)),
    ]
    return pl.pallas_call(
        functools.partial(_proj_kernel, n_main=n_main),
        grid=(bsz // nb, nt),
        in_specs=in_specs,
        out_specs=out_specs,
        out_shape=out_shape,
        compiler_params=_cparams(("parallel", "arbitrary"), 56),
        name="in_proj",
    )(xa, xb, modsel, g, wu, wg, wt, wqb, wkvb, gq, gk, gqa, gkva, gmq, gmk, cg, sg, cm, sm)


def _short_conv_kernel(u_ref, up_ref, un_ref, w_ref, x0_ref, z_ref, *, n_tiles):
    i = pl.program_id(1)
    u = u_ref[...].astype(F32)
    tm = u.shape[0]
    prev = jnp.where(i > 0, up_ref[7:8, :].astype(F32), 0.0)
    nxt = jnp.where(i < n_tiles - 1, un_ref[0:1, :].astype(F32), 0.0)
    ridx = lax.broadcasted_iota(jnp.int32, u.shape, 0)
    up = jnp.where(ridx == 0, prev, pltpu.roll(u, 1, axis=0))
    dn = jnp.where(ridx == tm - 1, nxt, pltpu.roll(u, tm - 1, axis=0))
    uc = up * w_ref[0:1, :] + u * w_ref[1:2, :] + dn * w_ref[2:3, :]
    c = HY_WIDTH
    x0_ref[...] = uc[:, :c].astype(BF16)
    z_ref[...] = (uc[:, c:2 * c] * uc[:, 2 * c:]).astype(BF16)


def _short_conv(u, short_w, row0, rows):
    bsz, s, c3 = u.shape
    tm = math.gcd(math.gcd(rows, row0), CONV_TILE) if row0 else math.gcd(rows, CONV_TILE)
    nt = rows // tm
    t0 = row0 // tm
    r8 = tm // 8
    last8 = s // 8 - 1
    return pl.pallas_call(
        functools.partial(_short_conv_kernel, n_tiles=nt),
        grid=(bsz, nt),
        in_specs=[
            pl.BlockSpec((None, tm, c3), lambda b, i: (b, t0 + i, 0)),
            pl.BlockSpec((None, 8, c3), lambda b, i: (b, jnp.maximum((t0 + i) * r8 - 1, 0), 0)),
            pl.BlockSpec((None, 8, c3), lambda b, i: (b, jnp.minimum((t0 + i + 1) * r8, last8), 0)),
            _const_spec(short_w.shape),
        ],
        out_specs=[pl.BlockSpec((None, tm, HY_WIDTH), lambda b, i: (b, i, 0))] * 2,
        out_shape=[jax.ShapeDtypeStruct((bsz, rows, HY_WIDTH), BF16)] * 2,
        compiler_params=_cparams(("parallel", "arbitrary"), 32),
        name="hyena_short_conv",
    )(u, u, u, short_w)


def _filter_kernel(zf_ref, w1_ref, b1_ref, w2_ref, b2_ref, fr_ref, w3_ref, dl_ref, h_ref, ss_ref):
    i = pl.program_id(0)
    zf = zf_ref[...]
    tl = zf.shape[0]
    h = jnp.sin(fr_ref[0:1, :] * (_dot_hi(zf, w1_ref[...]) + b1_ref[...]))
    h = jnp.sin(fr_ref[1:2, :] * (_dot_hi(h, w2_ref[...]) + b2_ref[...]))
    h = _dot_hi(h, w3_ref[...])
    decay = jnp.exp(-zf[:, 0:1] * dl_ref[...])
    c = HY_WIDTH
    hf = h[:, :c] * decay
    ridx = lax.broadcasted_iota(jnp.int32, (tl, c), 0) + i * tl
    hb = jnp.where(ridx == 0, 0.0, h[:, c:] * decay)
    h_ref[0] = hf.astype(BF16)
    h_ref[1] = hb.astype(BF16)
    ss = jnp.sum(hf * hf + hb * hb, axis=0, keepdims=True)

    @pl.when(i == 0)
    def _():
        ss_ref[...] = ss

    @pl.when(i > 0)
    def _():
        ss_ref[...] += ss


def _filter(zfeat, fw, deltas):
    length = zfeat.shape[0]
    tl = min(length, 1024)
    w1, b1, w2, b2, fr, w3 = fw
    return pl.pallas_call(
        _filter_kernel,
        grid=(length // tl,),
        in_specs=[pl.BlockSpec((tl, zfeat.shape[1]), lambda i: (i, 0))]
        + [_const_spec(a.shape) for a in (w1, b1, w2, b2, fr, w3, deltas)],
        out_specs=[pl.BlockSpec((2, tl, HY_WIDTH), lambda i: (0, i, 0)),
                   pl.BlockSpec((1, HY_WIDTH), lambda i: (0, 0))],
        out_shape=[jax.ShapeDtypeStruct((2, length, HY_WIDTH), BF16),
                   jax.ShapeDtypeStruct((1, HY_WIDTH), F32)],
        compiler_params=_cparams(("arbitrary",), 40),
        name="hyena_filter",
    )(zfeat, w1, b1, w2, b2, fr, w3, deltas)


def _fft_a_kernel(g_ref, x_ref, o_ref, *, nb, cw):
    for j in range(nb):
        sl = slice(j * cw, (j + 1) * cw)
        x = jnp.concatenate([x_ref[p, :, sl] for p in range(x_ref.shape[0])], axis=0)
        o_ref[:, sl] = _dot(g_ref[j], x).astype(o_ref.dtype)


def _fft_a(x3d, gmat):
    bx, parts, k1, w = x3d.shape
    n2, two_n1, _ = gmat.shape
    cw = w // n2
    nb = FFT_NB
    return pl.pallas_call(
        functools.partial(_fft_a_kernel, nb=nb, cw=cw),
        grid=(bx, n2 // nb),
        in_specs=[pl.BlockSpec((nb, two_n1, parts * k1), lambda b, j: (j, 0, 0)),
                  pl.BlockSpec((None, parts, k1, nb * cw), lambda b, j: (b, 0, 0, j))],
        out_specs=pl.BlockSpec((None, two_n1, nb * cw), lambda b, j: (b, 0, j)),
        out_shape=jax.ShapeDtypeStruct((bx, two_n1, w), BF16),
        compiler_params=_cparams(("parallel", "arbitrary"), 40),
        name="fft_stage_a",
    )(gmat, x3d)


def _fft_filter_b_kernel(fb_ref, a_ref, ss_ref, kf_ref, *, inv_n):
    n = a_ref.shape[3]
    rs = lax.rsqrt(ss_ref[...] + EPS) * inv_n
    for r in range(a_ref.shape[2]):
        xf = _dot(fb_ref[...], jnp.concatenate([a_ref[0, 0, r], a_ref[0, 1, r]], axis=0))
        xb = _dot(fb_ref[...], jnp.concatenate([a_ref[1, 0, r], a_ref[1, 1, r]], axis=0))
        kf_ref[0, r] = (xf[:n] + xb[:n]) * rs
        kf_ref[1, r] = (xf[n:] - xb[n:]) * rs


def _fft_filter_b(a5, fb, ssq, inv_n):
    _, _, n1, n2, c = a5.shape
    kb = min(FFT_KB, n1)
    return pl.pallas_call(
        functools.partial(_fft_filter_b_kernel, inv_n=inv_n),
        grid=(n1 // kb,),
        in_specs=[_const_spec(fb.shape),
                  pl.BlockSpec((2, 2, kb, n2, c), lambda k: (0, 0, k, 0, 0)),
                  _const_spec(ssq.shape)],
        out_specs=pl.BlockSpec((2, kb, n2, c), lambda k: (0, k, 0, 0)),
        out_shape=jax.ShapeDtypeStruct((2, n1, n2, c), F32),
        compiler_params=_cparams(("arbitrary",), 32),
        name="fft_filter_stage_b",
    )(fb, a5, ssq)


def _fft_b_kernel(fb_ref, fbi_ref, a_ref, kf_ref, o_ref):
    kb, n, cw = a_ref.shape[1:]
    a_all = jnp.concatenate([jnp.concatenate([a_ref[0, r], a_ref[1, r]], axis=0) for r in range(kb)], axis=1)
    x = _dot(fb_ref[...], a_all)
    ys = []
    for r in range(kb):
        xr, xi = x[:n, r * cw:(r + 1) * cw], x[n:, r * cw:(r + 1) * cw]
        kr, ki = kf_ref[0, r], kf_ref[1, r]
        ys.append(jnp.concatenate([xr * kr - xi * ki, xr * ki + xi * kr], axis=0).astype(BF16))
    c = _dot(fbi_ref[...], jnp.concatenate(ys, axis=1))
    for r in range(kb):
        o_ref[0, r] = c[:n, r * cw:(r + 1) * cw].astype(o_ref.dtype)
        o_ref[1, r] = c[n:, r * cw:(r + 1) * cw].astype(o_ref.dtype)


def _fft_b(a5, kf, fb, fbi):
    bsz, _, n1, n2, c = a5.shape
    kb = min(FFT_KB, n1)
    return pl.pallas_call(
        _fft_b_kernel,
        grid=(n1 // kb, bsz),
        in_specs=[_const_spec(fb.shape), _const_spec(fbi.shape),
                  pl.BlockSpec((None, 2, kb, n2, c), lambda k, b: (b, 0, k, 0, 0)),
                  pl.BlockSpec((2, kb, n2, c), lambda k, b: (0, k, 0, 0))],
        out_specs=pl.BlockSpec((None, 2, kb, n2, c), lambda k, b: (b, 0, k, 0, 0)),
        out_shape=jax.ShapeDtypeStruct(a5.shape, BF16),
        compiler_params=_cparams(("arbitrary", "arbitrary"), 32),
        name="fft_stage_b",
    )(fb, fbi, a5, kf)


def _fft_c_kernel(h_ref, c_ref, o_ref, *, nb, cw):
    parts, k1 = o_ref.shape[0], o_ref.shape[1]
    for j in range(nb):
        sl = slice(j * cw, (j + 1) * cw)
        y = _dot(h_ref[j], c_ref[:, sl])
        for p in range(parts):
            o_ref[p, :, sl] = y[p * k1:(p + 1) * k1].astype(o_ref.dtype)


def _fft_c(c2d, hmat, parts):
    bx, two_n1, w = c2d.shape
    n2, rows, _ = hmat.shape
    k1 = rows // parts
    cw = w // n2
    nb = FFT_NB
    return pl.pallas_call(
        functools.partial(_fft_c_kernel, nb=nb, cw=cw),
        grid=(bx, n2 // nb),
        in_specs=[pl.BlockSpec((nb, rows, two_n1), lambda b, j: (j, 0, 0)),
                  pl.BlockSpec((None, two_n1, nb * cw), lambda b, j: (b, 0, j))],
        out_specs=pl.BlockSpec((None, parts, k1, nb * cw), lambda b, j: (b, 0, 0, j)),
        out_shape=jax.ShapeDtypeStruct((bx, parts, k1, w), BF16),
        compiler_params=_cparams(("parallel", "arbitrary"), 40),
        name="fft_stage_c",
    )(hmat, c2d)


def _ctx_conv_kernel(fc_ref, fci_ref, h_ref, ss_ref, z_ref, o_ref):
    n = fc_ref.shape[0] // 2
    fc = fc_ref[...]
    kf = _dot(fc, h_ref[0])
    kb = _dot(fc, h_ref[1])
    rs = lax.rsqrt(ss_ref[...] + EPS)
    kr = (kf[:n] + kb[:n]) * rs
    ki = (kf[n:] - kb[n:]) * rs
    x = _dot(fc, z_ref[...])
    xr, xi = x[:n], x[n:]
    y = jnp.concatenate([xr * kr - xi * ki, xr * ki + xi * kr], axis=0).astype(BF16)
    o_ref[...] = _dot(fci_ref[...], y).astype(o_ref.dtype)


def _ctx_conv(fc, fci, hfb, ssq, z):
    bsz, lc, c = z.shape
    blk = pl.BlockSpec((None, lc, c), lambda b: (b, 0, 0))
    return pl.pallas_call(
        _ctx_conv_kernel,
        grid=(bsz,),
        in_specs=[_const_spec(fc.shape), _const_spec(fci.shape), _const_spec(hfb.shape),
                  _const_spec(ssq.shape), blk],
        out_specs=blk,
        out_shape=jax.ShapeDtypeStruct((bsz, lc, c), BF16),
        compiler_params=_cparams(("arbitrary",), 32),
        name="hyena_ctx_conv",
    )(fc, fci, hfb, ssq, z)


@functools.lru_cache(maxsize=None)
def _fft_tables(length):
    n = 2 * length
    n2 = FFT_N2
    n1 = n // n2
    k1 = length // n2
    kk = np.arange(n1)[:, None]
    g = np.empty((n2, 2 * n1, k1), np.float64)
    g2 = np.empty((n2, 2 * n1, 2 * k1), np.float64)
    h2 = np.empty((n2, 2 * k1, 2 * n1), np.float64)
    nn = np.arange(k1)[None, :]
    for j in range(n2):
        ang = 2.0 * np.pi * (((n2 * nn * kk) % n) + (j * kk) % n) / n
        c_, s_ = np.cos(ang), np.sin(ang)
        g[j] = np.concatenate([c_, -s_], axis=0)
        g2[j] = np.block([[c_, s_], [-s_, c_]])
        h2[j] = np.block([[c_.T, -s_.T], [s_.T, c_.T]])
    a = np.arange(n2)
    ph = 2.0 * np.pi * ((a[:, None] * a[None, :]) % n2) / n2
    c, s = np.cos(ph), np.sin(ph)
    fb = np.block([[c, s], [-s, c]])
    fbi = np.block([[c, -s], [s, c]])
    return (jnp.asarray(g, BF16), jnp.asarray(g2, BF16), jnp.asarray(h2, BF16), jnp.asarray(fb, BF16),
            jnp.asarray(fbi, BF16), n1, k1)


@functools.lru_cache(maxsize=None)
def _dft_tables(length):
    n = 2 * length
    k = np.arange(n)[:, None]
    t = np.arange(length)[None, :]
    ang = 2.0 * np.pi * ((k * t) % n) / n
    fc = np.concatenate([np.cos(ang), -np.sin(ang)], axis=0)
    fci = np.concatenate([np.cos(ang).T, -np.sin(ang).T], axis=1) / n
    return jnp.asarray(fc, BF16), jnp.asarray(fci, BF16)


@functools.lru_cache(maxsize=None)
def _filter_features(length):
    t = np.linspace(0.0, 1.0, length, dtype=np.float32)[:, None]
    w = (2.0 * math.pi * np.arange(length, dtype=np.float32)[:, None] / length).astype(np.float32)
    f = np.linspace(1e-4, HY_BANDS - 1, HY_BANDS, dtype=np.float32)[None, :]
    z = np.concatenate([t, np.cos(f * w), -np.sin(f * w)], axis=-1).astype(np.float32)
    zp = np.zeros((length, V7X_LANES), np.float32)
    zp[:, :HY_EMB_DIM] = z
    return jnp.asarray(zp)


def _hyena_deltas():
    max_decay = math.log(HY_DECAY_TARGET) / HY_FAST_DECAY
    min_decay = math.log(HY_DECAY_TARGET) / HY_SLOW_DECAY
    return jnp.abs(jnp.linspace(min_decay, max_decay, HY_WIDTH, dtype=F32))[None, :]


def _hyena(u, short_w, fw, lat, lc):
    bsz = u.shape[0]
    c = HY_WIDTH
    deltas = _hyena_deltas()
    x0, z = _short_conv(u, short_w, 0, lat)
    gmat, gmat2, hmat2, fb, fbi, n1, k1 = _fft_tables(lat)
    n2 = FFT_N2
    hfb, ssq = _filter(_filter_features(lat), fw, deltas)
    fa = _fft_a(hfb.reshape(2, 1, k1, n2 * c), gmat)
    kf = _fft_filter_b(fa.reshape(2, 2, n1, n2, c), fb, ssq, 1.0 / (2 * lat))
    npair = (bsz + 1) // 2
    zp = z if bsz % 2 == 0 else jnp.concatenate([z, jnp.zeros_like(z[:1])], axis=0)
    za = _fft_a(zp.reshape(npair, 2, k1, n2 * c), gmat2)
    zc = _fft_b(za.reshape(npair, 2, n1, n2, c), kf, fb, fbi)
    conv = _fft_c(zc.reshape(npair, 2 * n1, n2 * c), hmat2, 2).reshape(2 * npair, lat, c)[:bsz]
    x0_c, z_c = _short_conv(u, short_w, lat, lc)
    hfb_c, ssq_c = _filter(_filter_features(lc), fw, deltas)
    fc, fci = _dft_tables(lc)
    conv_c = _ctx_conv(fc, fci, hfb_c, ssq_c, z_c)
    return (conv, x0, z), (conv_c, x0_c, z_c)


def _gqa_kernel(q_ref, k_ref, v_ref, sink_ref, o_ref, *bufs, lat, lc, tq):
    sub = q_ref.shape[1] // tq
    for t in range(sub):
        lanes = slice(t * tq, (t + 1) * tq)
        _gqa_tile(pl.program_id(1) * sub + t, q_ref.at[:, lanes], k_ref, v_ref, sink_ref, o_ref.at[:, lanes], bufs,
                  lat, lc)


def _gqa_tile(i, q_ref, k_ref, v_ref, sink_ref, o_ref, bufs, lat, lc):
    tq = q_ref.shape[1]
    s_len = k_ref.shape[0]
    hd = GQA_HEAD_DIM
    w = WINDOW
    nwin = tq + 2 * w
    start = pl.multiple_of(jnp.clip(i * tq - w, 0, s_len - nwin), w)
    k_all = jnp.concatenate([k_ref[pl.ds(start, nwin), :], k_ref[lat:lat + lc, :]], axis=0)
    v_all = jnp.concatenate([v_ref[:, pl.ds(start, nwin)], v_ref[:, lat:lat + lc]], axis=1)
    q_pos = i * tq + lax.broadcasted_iota(jnp.int32, (1, tq), 1)
    k_pos = start + lax.broadcasted_iota(jnp.int32, (nwin, 1), 0)
    bias = (jnp.where(jnp.abs(k_pos - q_pos) <= w, 0.0, NEG_BIG)
            + jnp.where(k_pos < lat, 0.0, NEG_BIG)
            + jnp.where(q_pos < lat, 0.0, NEG_BIG))
    bias = jnp.concatenate([bias, jnp.zeros((lc, tq), F32)], axis=0)
    bias2 = jnp.concatenate([bias, bias], axis=1)
    zq = jnp.zeros((V7X_LANES - hd, 2 * tq), BF16)

    def score(pair, dst):
        g = pair // (GQA_GROUP // 2)
        r = 2 * pair * hd
        q2 = jnp.concatenate([q_ref[r:r + hd, :], q_ref[r + hd:r + 2 * hd, :]], axis=1)
        s = _dot(k_all[:, g * V7X_LANES:(g + 1) * V7X_LANES], jnp.concatenate([q2, zq], axis=0)) + bias2
        dst[...] = s
        return jnp.max(s, axis=0, keepdims=True)

    def update(pair, s_sc, s_max):
        g = pair // (GQA_GROUP // 2)
        r = 2 * pair * hd
        sink = jnp.concatenate([sink_ref[2 * pair:2 * pair + 1, :], sink_ref[2 * pair + 1:2 * pair + 2, :]],
                               axis=1) * LOG2E
        m = jnp.maximum(s_max, sink)
        p = jnp.exp2(s_sc[...] - m).astype(BF16)
        pv = _dot(v_all[g * V_ROWS:(g + 1) * V_ROWS, :], p)
        o = pv[:hd] / (pv[hd:hd + 1] + jnp.exp2(sink - m))
        o_ref[r:r + hd, :] = o[:, :tq].astype(o_ref.dtype)
        o_ref[r + hd:r + 2 * hd, :] = o[:, tq:].astype(o_ref.dtype)

    npair = GQA_HEADS // 2
    mx = [score(j, bufs[j]) for j in range(npair)]
    for j in range(npair):
        update(j, bufs[j], mx[j])


def _gqa(qg, kg, vg, sink_rows, lat, lc):
    bsz, nq, s = qg.shape
    tq = GQA_TQ
    sub = GQA_SUBTILES if (s // tq) % GQA_SUBTILES == 0 else 1
    wq = tq * sub
    return pl.pallas_call(
        functools.partial(_gqa_kernel, lat=lat, lc=lc, tq=tq),
        grid=(bsz, s // wq),
        in_specs=[
            pl.BlockSpec((None, nq, wq), lambda b, i: (b, 0, i)),
            pl.BlockSpec((None, s, kg.shape[2]), lambda b, i: (b, 0, 0)),
            pl.BlockSpec((None, vg.shape[1], s), lambda b, i: (b, 0, 0)),
            _const_spec(sink_rows.shape),
        ],
        out_specs=pl.BlockSpec((None, nq, wq), lambda b, i: (b, 0, i)),
        out_shape=jax.ShapeDtypeStruct((bsz, nq, s), BF16),
        scratch_shapes=[pltpu.VMEM((tq + 2 * WINDOW + lc, 2 * tq), F32)] * (GQA_HEADS // 2),
        compiler_params=_cparams(("parallel", "arbitrary"), 40),
        name="gqa_window_attn",
    )(qg, kg, vg, sink_rows)


def _mla_update(s_ref, s_max, vt, m, acc):
    m_new = jnp.maximum(m, s_max)
    alpha = jnp.exp2(m - m_new)
    p = jnp.exp2(s_ref[...] - m_new).astype(BF16)
    acc = alpha * acc + _dot(vt, p)
    return m_new, acc


def _mla_kernel(q_ref, k_ref, v_ref, o_ref, *bufs, nc, ctx_only, tq):
    for t in range(q_ref.shape[1] // tq):
        lanes = slice(t * tq, (t + 1) * tq)
        _mla_tile(q_ref[:, lanes], k_ref, v_ref, o_ref.at[:, lanes], bufs, nc, ctx_only)


def _mla_tile(q, k_ref, v_ref, o_ref, bufs, nc, ctx_only):
    tq = q.shape[1]
    ck = KV_CHUNK
    nbuf = len(bufs)
    m = jnp.full((1, tq), NEG_BIG, F32)
    acc = jnp.zeros((V_ROWS, tq), F32)

    def score(k, dst):
        s = _dot(k, q)
        dst[...] = s
        return jnp.max(s, axis=0, keepdims=True)

    def score_chunk(j, dst):
        return score(k_ref[pl.ds(pl.multiple_of(j * ck, ck), ck), :], dst)

    if ctx_only:
        s_x = bufs[0].at[0:KV_SUB, :]
        mx = score(k_ref[k_ref.shape[0] - KV_SUB:, :], s_x)
        m, acc = _mla_update(s_x, mx, v_ref[v_ref.shape[0] - 1][:, ck - KV_SUB:], m, acc)
    else:
        ahead = nbuf - 1
        mx = [None] * nbuf
        for j in range(min(ahead, nc)):
            mx[j] = score_chunk(j, bufs[j])

        def step(j, slot, m, acc, mx, last):
            if not last:
                nxt = (slot + ahead) % nbuf
                mx[nxt] = score_chunk(j + ahead, bufs[nxt])
            return _mla_update(bufs[slot], mx[slot], v_ref[j], m, acc)

        per_body = MLA_UNROLL * nbuf

        def body(g, c):
            m, acc, mx = c[0], c[1], list(c[2:])
            for u in range(per_body):
                m, acc = step(g * per_body + u, u % nbuf, m, acc, mx, False)
            return (m, acc, *mx)

        n_body = max(nc - ahead, 0) // per_body
        if n_body > 0:
            out = lax.fori_loop(0, n_body, body, (m, acc, *[m if x is None else x for x in mx]))
            m, acc, mx = out[0], out[1], list(out[2:])
        for j in range(n_body * per_body, nc):
            m, acc = step(j, j % nbuf, m, acc, mx, j + ahead >= nc)
    o_ref[...] = (acc[:MLA_V_DIM] / acc[MLA_V_DIM:MLA_V_DIM + 1]).astype(o_ref.dtype)


def _mla_call(qm, km, vm, tq, sub, q_tile0, n_q, ctx_only):
    bsz, _, s = qm.shape
    nc = vm.shape[1]
    wq = tq * sub
    if ctx_only:
        k_spec = pl.BlockSpec((None, KV_SUB, MLA_HEAD_PAD), lambda b, h, i: (b, s // KV_SUB - 1, h))
        v_spec = pl.BlockSpec((None, 1, V_ROWS, KV_CHUNK), lambda b, h, i: (b, nc - 1, h, 0))
    else:
        k_spec = pl.BlockSpec((None, s, MLA_HEAD_PAD), lambda b, h, i: (b, 0, h))
        v_spec = pl.BlockSpec((None, nc, V_ROWS, KV_CHUNK), lambda b, h, i: (b, 0, h, 0))
    return pl.pallas_call(
        functools.partial(_mla_kernel, nc=nc, ctx_only=ctx_only, tq=tq),
        grid=(bsz, MLA_HEADS, n_q),
        in_specs=[pl.BlockSpec((None, MLA_HEAD_PAD, wq), lambda b, h, i: (b, h, q_tile0 + i)), k_spec, v_spec],
        out_specs=pl.BlockSpec((None, MLA_V_DIM, wq), lambda b, h, i: (b, h, i)),
        out_shape=jax.ShapeDtypeStruct((bsz, MLA_HEADS * MLA_V_DIM, n_q * wq), BF16),
        scratch_shapes=[pltpu.VMEM((KV_CHUNK, tq), F32)] * MLA_NBUF,
        compiler_params=_cparams(("parallel", "arbitrary", "arbitrary"), 40),
        name="mla_attn_ctx" if ctx_only else "mla_attn",
    )(qm, km, vm)


def _mla(qm, km, vm, lat, lc):
    tq = min(MLA_TQ, lat)
    sub = MLA_SUBTILES if lat % (tq * MLA_SUBTILES) == 0 else 1
    assert lc == KV_SUB and lat % lc == 0 and lat % tq == 0
    return (_mla_call(qm, km, vm, tq, sub, 0, lat // (tq * sub), False),
            _mla_call(qm, km, vm, lc, 1, lat // lc, 1, True))


def _merge_mlp_kernel(xa_ref, xb_ref, mod_ref, hl_refs, hc_refs, sk_ref, yg_ref, yml_ref, ymc_ref, gt_ref, wb_ref,
                      wo_ref, g_ref, w1_ref, w2_ref, o_ref, *, lat_tiles, n_main):
    from_main = pl.program_id(1) < n_main
    for n in range(xa_ref.shape[0]):
        x = jnp.where(from_main, xa_ref[n], xb_ref[n])
        _merge_mlp_one(x, mod_ref.at[n], [r.at[n] for r in hl_refs], [r.at[n] for r in hc_refs], sk_ref,
                       yg_ref.at[n], yml_ref.at[n], ymc_ref.at[n], gt_ref.at[n], wb_ref, wo_ref, g_ref, w1_ref,
                       w2_ref, o_ref.at[n], lat_tiles)


def _merge_mlp_one(x, mod_ref, hl_refs, hc_refs, sk_ref, yg_ref, yml_ref, ymc_ref, gt_ref, wb_ref, wo_ref,
                   g_ref, w1_ref, w2_ref, o_ref, lat_tiles):
    d = D_MODEL
    is_lat = pl.program_id(1) < lat_tiles
    conv, x0, z = [jnp.where(is_lat, a[...], b[...]).astype(F32) for a, b in zip(hl_refs, hc_refs)]
    yh = (x0 * (conv + z * sk_ref[...])).astype(BF16)
    yg = yg_ref[...].astype(F32).T.astype(BF16)
    ym = jnp.where(is_lat, yml_ref[...], ymc_ref[...]).astype(F32).T.astype(BF16)
    merged = (gt_ref[:, 0:d].astype(F32) * _dot(yh, wb_ref[0])
              + gt_ref[:, d:2 * d].astype(F32) * _dot(yg, wb_ref[1])
              + gt_ref[:, 2 * d:].astype(F32) * _dot(ym, wb_ref[2]))
    res = _dot(merged.astype(BF16), wo_ref[...])
    x_mix = x + mod_ref[:, 2 * d:3 * d] * res
    o_ref[...] = _mlp_rows(x_mix, mod_ref, g_ref, w1_ref, w2_ref)


def _merge_mlp(stream, modsel, hy_lat, hy_ctx, skip, yg, ym_lat, ym_ctx, gates, wb, wo, g, w1, w2, lat_tiles,
               n_tiles):
    xa, xb, n_main = stream
    bsz, _, d = xa.shape
    tm = ROW_TILE
    c = HY_WIDTH
    nb = MERGE_BATCH if bsz % MERGE_BATCH == 0 else 1
    row = lambda b, i: (b, i, 0)
    col = lambda b, i: (b, 0, i)
    lat_row = pl.BlockSpec((nb, tm, c), lambda b, i: (b, jnp.minimum(i, lat_tiles - 1), 0))
    ctx_row = pl.BlockSpec((nb, tm, c), lambda b, i: (b, jnp.maximum(i - lat_tiles, 0), 0))
    return pl.pallas_call(
        functools.partial(_merge_mlp_kernel, lat_tiles=lat_tiles, n_main=n_main),
        grid=(bsz // nb, n_tiles),
        in_specs=[
            *_stream_specs(nb, tm, d, n_main),
            pl.BlockSpec((nb, None, 1, 6 * d), lambda b, i: (b, jnp.where(i >= lat_tiles, 1, 0), 0, 0)),
            [lat_row] * 3, [ctx_row] * 3, _const_spec(skip.shape),
            pl.BlockSpec((nb, c, tm), col),
            pl.BlockSpec((nb, c, tm), lambda b, i: (b, 0, jnp.minimum(i, lat_tiles - 1))),
            pl.BlockSpec((nb, c, tm), lambda b, i: (b, 0, jnp.maximum(i - lat_tiles, 0))),
            pl.BlockSpec((nb, tm, 3 * d), row),
            _const_spec(wb.shape), _const_spec(wo.shape),
            _const_spec(g.shape), _const_spec(w1.shape), _const_spec(w2.shape),
        ],
        out_specs=pl.BlockSpec((nb, tm, d), row),
        out_shape=jax.ShapeDtypeStruct((bsz, n_tiles * tm, d), F32),
        compiler_params=_cparams(("parallel", "arbitrary"), 56),
        name="merge_mlp",
    )(xa, xb, modsel, list(hy_lat), list(hy_ctx), skip, yg, ym_lat, ym_ctx, gates, wb, wo, g, w1, w2)


def _mlp_rows(x, mod_ref, g_ref, w1_ref, w2_ref):
    d = D_MODEL
    xn = x * lax.rsqrt(jnp.mean(x * x, axis=-1, keepdims=True) + EPS) * g_ref[...]
    h = (xn * (1.0 + mod_ref[:, 4 * d:5 * d]) + mod_ref[:, 3 * d:4 * d]).astype(BF16)
    acc = jnp.zeros(x.shape, F32)
    for j in range(D_FF // FF_CHUNK):
        sl = slice(j * FF_CHUNK, (j + 1) * FF_CHUNK)
        a = jnp.maximum(_dot(h, w1_ref[:, sl]), 0.0)
        acc = acc + _dot((a * a).astype(BF16), w2_ref[sl, :])
    return x + mod_ref[:, 5 * d:] * acc


def _rope_tables_t(rows, dim, lc):
    n_freq = dim // 4
    inv = ROPE_BASE ** (-jnp.arange(n_freq, dtype=F32) / n_freq)
    r = jnp.repeat(jnp.arange(rows, dtype=F32), GRID_W)
    col = jnp.tile(jnp.arange(GRID_W, dtype=F32), rows)
    ang = jnp.concatenate([r[:, None] * inv, col[:, None] * inv], axis=-1)
    cos_t = jnp.concatenate([jnp.cos(ang).T, jnp.ones((dim // 2, lc), F32)], axis=1)
    sin_t = jnp.concatenate([jnp.sin(ang).T, jnp.zeros((dim // 2, lc), F32)], axis=1)
    return cos_t, sin_t


def _lane_bcast(v):
    return jnp.broadcast_to(v.astype(F32)[:, None], (v.shape[0], ROW_TILE))


def _layer_weights(l, w_in, gqa_q_norm, gqa_k_norm, mla_q_a_norm, mla_kv_a_norm, w_q_b, w_kv_b, mla_q_norm, mla_k_norm):
    w = w_in[l]
    o = np.cumsum([0, 3 * HY_WIDTH, GQA_HEADS * GQA_HEAD_DIM, GQA_KV_HEADS * GQA_HEAD_DIM,
                   GQA_KV_HEADS * GQA_HEAD_DIM, MLA_Q_RANK, MLA_KV_RANK, MLA_ROPE_DIM, 3 * D_MODEL])
    wu = w[:, o[0]:o[1]].astype(BF16)
    wt = w[:, o[1]:o[7]].T.astype(BF16)
    wg = w[:, o[7]:o[8]].astype(BF16)
    wqb = w_q_b[l].T.astype(BF16)
    wkvb = w_kv_b[l].T.astype(BF16)
    return (wu, wg, wt, wqb, wkvb,
            _lane_bcast(gqa_q_norm[l]), _lane_bcast(gqa_k_norm[l]),
            _lane_bcast(mla_q_a_norm[l]), _lane_bcast(mla_kv_a_norm[l]),
            _lane_bcast(mla_q_norm[l]), _lane_bcast(mla_k_norm[l]))


def kernel(x, c, ctx, c_ctx, w_mod, b_mod, norm_mix_g, norm_mlp_g, w_in, hy_short_w, hy_f1_w, hy_f1_b, hy_f2_w, hy_f2_b, hy_sin_freq, hy_f3_w, hy_skip, gqa_q_norm, gqa_k_norm, gqa_sink, mla_q_a_norm, mla_kv_a_norm, w_q_b, w_kv_b, mla_q_norm, mla_k_norm, w_branch, w_out, w_mlp1, w_mlp2):
    bsz, lat, d = x.shape
    lc = ctx.shape[1]
    depth = w_mod.shape[0]
    s = lat + lc
    assert d == D_MODEL and lat % GRID_W == 0 and lat % ROW_TILE == 0 and lc == ROW_TILE and s % KV_CHUNK == 0
    lat_tiles = lat // ROW_TILE

    pad = (-(bsz + 1)) % 8
    cond = jnp.concatenate([c, c_ctx[None, :], jnp.zeros((pad, d), F32)], axis=0)
    mods = _mod_all(cond, w_mod, b_mod)

    tabs = _rope_tables_t(lat // GRID_W, GQA_HEAD_DIM, lc) + _rope_tables_t(lat // GRID_W, MLA_ROPE_DIM, lc)
    stream = (x, ctx, lat_tiles)

    for l in range(depth):
        ml = mods[l]
        modsel = jnp.stack([ml[:bsz], jnp.broadcast_to(ml[bsz][None], (bsz, 6 * d))], axis=1)[:, :, None, :]
        wts = _layer_weights(l, w_in, gqa_q_norm, gqa_k_norm, mla_q_a_norm, mla_kv_a_norm, w_q_b, w_kv_b,
                             mla_q_norm, mla_k_norm)
        u, gates, qg, kg, vg, qm, km, vm = _proj(stream, s, modsel, norm_mix_g[l][None, :], wts, tabs, lat_tiles)

        f1w = jnp.zeros((V7X_LANES, HY_FILTER_WIDTH), F32).at[:HY_EMB_DIM].set(hy_f1_w[l])
        fw = (f1w, hy_f1_b[l][None, :], hy_f2_w[l], hy_f2_b[l][None, :], hy_sin_freq[l], hy_f3_w[l])
        hy_lat, hy_ctx = _hyena(u, hy_short_w[l], fw, lat, lc)

        sink_rows = jnp.broadcast_to(gqa_sink[l].astype(F32)[:, None], (GQA_HEADS, GQA_TQ))
        yg = _gqa(qg, kg, vg, sink_rows, lat, lc)
        ym_lat, ym_ctx = _mla(qm, km, vm, lat, lc)

        n_tiles = lat_tiles if l == depth - 1 else s // ROW_TILE
        xs = _merge_mlp(stream, modsel, hy_lat, hy_ctx, hy_skip[l][None, :], yg, ym_lat, ym_ctx, gates,
                        w_branch[l].astype(BF16), w_out[l].astype(BF16), norm_mlp_g[l][None, :],
                        w_mlp1[l].astype(BF16), w_mlp2[l].astype(BF16), lat_tiles, n_tiles)
        stream = (xs, ctx, s // ROW_TILE)
    return xs
```

```python
import functools
import math

import numpy as np
import jax
import jax.numpy as jnp
from jax import lax
from jax.experimental import pallas as pl
from jax.experimental.pallas import tpu as pltpu

D_MODEL = 1024
GRID_W = 64
HY_WIDTH = 512
HY_EMB_DIM = 33
HY_BANDS = (HY_EMB_DIM - 1) // 2
HY_FILTER_WIDTH = 64
HY_DECAY_TARGET = 1e-2
HY_FAST_DECAY = 0.3
HY_SLOW_DECAY = 1.5
GQA_HEADS = 8
GQA_KV_HEADS = 2
GQA_GROUP = GQA_HEADS // GQA_KV_HEADS
GQA_HEAD_DIM = 64
GQA_SCALE = GQA_HEAD_DIM ** -0.5
WINDOW = 128
MLA_HEADS = 8
MLA_Q_RANK = 384
MLA_KV_RANK = 256
MLA_NOPE_DIM = 64
MLA_ROPE_DIM = 32
MLA_V_DIM = 64
MLA_QK_DIM = MLA_NOPE_DIM + MLA_ROPE_DIM
MLA_SCALE = MLA_QK_DIM ** -0.5
D_FF = 4 * D_MODEL
ROPE_BASE = 10000.0
EPS = 1e-6
LOG2E = 1.4426950408889634
NEG_BIG = -1e30

V7X_LANES = 128
V7X_VMEM_BYTES = 64 * 1024 * 1024

ROW_TILE = 256
PROJ_BATCH = 2
MERGE_BATCH = 2
CONV_TILE = 512
MLA_HEAD_PAD = 128
V_ROWS = 80
KV_CHUNK = ROW_TILE
KV_SUB = 256
MLA_NBUF = 3
MLA_UNROLL = 5
MLA_TQ = 1024
MLA_SUBTILES = 4
GQA_TQ = 256
GQA_SUBTILES = 3
FF_CHUNK = 1024
FFT_N2 = 128
FFT_NB = 16
FFT_KB = 8

F32 = jnp.float32
BF16 = jnp.bfloat16


def _cparams(sem, vmem_mb):
    limit = vmem_mb * 1024 * 1024
    assert limit < V7X_VMEM_BYTES
    return pltpu.CompilerParams(dimension_semantics=sem, vmem_limit_bytes=limit)


def _dot(a, b):
    return jnp.dot(a, b, preferred_element_type=F32)


def _dot_hi(a, b):
    return jnp.dot(a, b, preferred_element_type=F32, precision=lax.Precision.HIGHEST)


def _ones_row_block(width):
    r = lax.broadcasted_iota(jnp.int32, (V_ROWS - MLA_V_DIM, width), 0)
    return jnp.where(r == 0, 1.0, 0.0).astype(BF16)


def _const_spec(shape):
    nd = len(shape)
    return pl.BlockSpec(shape, lambda *_: (0,) * nd, pipeline_mode=pl.Buffered(1))


def _mod_kernel(c_ref, w_ref, b_ref, o_ref):
    c = c_ref[...]
    s = c * jax.nn.sigmoid(c)
    o_ref[...] = _dot_hi(s, w_ref[...]) + b_ref[...]


def _mod_all(cond, w_mod, b_mod):
    depth, d, n = w_mod.shape
    rows = cond.shape[0]
    tn = 1536
    return pl.pallas_call(
        _mod_kernel,
        grid=(depth, n // tn),
        in_specs=[
            pl.BlockSpec((rows, d), lambda l, j: (0, 0)),
            pl.BlockSpec((None, d, tn), lambda l, j: (l, 0, j)),
            pl.BlockSpec((None, 1, tn), lambda l, j: (l, 0, j)),
        ],
        out_specs=pl.BlockSpec((None, rows, tn), lambda l, j: (l, 0, j)),
        out_shape=jax.ShapeDtypeStruct((depth, rows, n), F32),
        compiler_params=_cparams(("arbitrary", "arbitrary"), 40),
        name="adaln_mod",
    )(cond, w_mod, b_mod.reshape(depth, 1, n))


def _rms_rows(x, n):
    return lax.rsqrt(jnp.sum(x * x, axis=0, keepdims=True) * (1.0 / n) + EPS)


def _rope_rows(x1, x2, cs, sn):
    return x1 * cs - x2 * sn, x1 * sn + x2 * cs


def _proj_kernel(x_ref, mod_ref, *refs):
    consts, outs = refs[:16], refs[16:]
    for n in range(x_ref.shape[0]):
        _proj_one(x_ref.at[n], mod_ref.at[n], *consts, *[o.at[n] for o in outs])


def _proj_one(x_ref, mod_ref, g_ref, wu_ref, wg_ref, wt_ref, wqb_ref, wkvb_ref,
              gq_ref, gk_ref, gqa_ref, gkva_ref, gmq_ref, gmk_ref,
              cg_ref, sg_ref, cm_ref, sm_ref,
              u_ref, gate_ref, qg_ref, kg_ref, vg_ref, qm_ref, km_ref, vm_ref):
    d = D_MODEL
    x = x_ref[...]
    tm = x.shape[0]
    shift = mod_ref[:, 0:d]
    scale = mod_ref[:, d:2 * d]
    xn = x * lax.rsqrt(jnp.mean(x * x, axis=-1, keepdims=True) + EPS) * g_ref[...]
    h = xn * (1.0 + scale) + shift
    hb = h.astype(BF16)

    ht = h.T.astype(BF16)
    t = _dot(wt_ref[...], ht)
    o_q, o_k, o_v = 0, 512, 640
    o_cq, o_ckv, o_kr = 768, 768 + MLA_Q_RANK, 768 + MLA_Q_RANK + MLA_KV_RANK

    u_ref[...] = _dot(hb, wu_ref[...]).astype(BF16)

    def gate_cols(n):
        cols = slice(n * d, (n + 1) * d)
        gate_ref[:, cols] = jax.nn.sigmoid(_dot(hb, wg_ref[:, cols])).astype(BF16)

    gate_cols(0)
    cg, sg = cg_ref[...], sg_ref[...]
    cm, sm = cm_ref[...], sm_ref[...]
    hd, hh = GQA_HEAD_DIM, GQA_HEAD_DIM // 2

    cq = t[o_cq:o_cq + MLA_Q_RANK]
    cqn = (cq * _rms_rows(cq, MLA_Q_RANK) * gqa_ref[...]).astype(BF16)
    qm = _dot(wqb_ref[...], cqn)
    gmq = gmq_ref[...]
    nd, rh = MLA_NOPE_DIM, MLA_ROPE_DIM // 2
    qpad = jnp.zeros((MLA_HEAD_PAD - MLA_QK_DIM, tm), BF16)
    for n in range(MLA_HEADS):
        xh = qm[n * MLA_QK_DIM:(n + 1) * MLA_QK_DIM]
        xh = xh * _rms_rows(xh, MLA_QK_DIM) * gmq * (MLA_SCALE * LOG2E)
        a, b = _rope_rows(xh[nd:nd + rh], xh[nd + rh:], cm, sm)
        base = n * MLA_HEAD_PAD
        qm_ref[base:base + nd, :] = xh[:nd].astype(BF16)
        qm_ref[base + nd:base + nd + rh, :] = a.astype(BF16)
        qm_ref[base + nd + rh:base + MLA_QK_DIM, :] = b.astype(BF16)
        qm_ref[base + MLA_QK_DIM:base + MLA_HEAD_PAD, :] = qpad

    gate_cols(1)
    ckv = t[o_ckv:o_ckv + MLA_KV_RANK]
    ckvn = (ckv * _rms_rows(ckv, MLA_KV_RANK) * gkva_ref[...]).astype(BF16)
    kv = _dot(wkvb_ref[...], ckvn)
    kr = t[o_kr:o_kr + MLA_ROPE_DIM]
    kr_ss = jnp.sum(kr * kr, axis=0, keepdims=True)
    gmk = gmk_ref[...]
    kzero = jnp.zeros((MLA_HEAD_PAD - MLA_QK_DIM, tm), F32)
    ones_rows = _ones_row_block(tm)
    parts = []
    for n in range(MLA_HEADS):
        kn = kv[n * 128:n * 128 + nd]
        vm_ref[n * V_ROWS:n * V_ROWS + MLA_V_DIM, :] = kv[n * 128 + nd:(n + 1) * 128].astype(BF16)
        vm_ref[n * V_ROWS + MLA_V_DIM:(n + 1) * V_ROWS, :] = ones_rows
        rs = lax.rsqrt((jnp.sum(kn * kn, axis=0, keepdims=True) + kr_ss) * (1.0 / MLA_QK_DIM) + EPS)
        krn = kr * rs * gmk[nd:]
        a, b = _rope_rows(krn[:rh], krn[rh:], cm, sm)
        parts += [kn * rs * gmk[:nd], a, b, kzero]
    km_ref[...] = jnp.concatenate(parts, axis=0).T.astype(BF16)

    gq = gq_ref[...]
    for n in range(GQA_HEADS):
        xh = t[o_q + n * hd:o_q + (n + 1) * hd]
        xh = xh * _rms_rows(xh, hd) * gq
        a, b = _rope_rows(xh[:hh], xh[hh:], cg, sg)
        qg_ref[n * hd:n * hd + hh, :] = (a * (GQA_SCALE * LOG2E)).astype(BF16)
        qg_ref[n * hd + hh:(n + 1) * hd, :] = (b * (GQA_SCALE * LOG2E)).astype(BF16)

    gate_cols(2)
    gk = gk_ref[...]
    zpad = jnp.zeros((V7X_LANES - hd, tm), F32)
    parts = []
    for n in range(GQA_KV_HEADS):
        xh = t[o_k + n * hd:o_k + (n + 1) * hd]
        xh = xh * _rms_rows(xh, hd) * gk
        a, b = _rope_rows(xh[:hh], xh[hh:], cg, sg)
        parts += [a, b, zpad]
    kg_ref[...] = jnp.concatenate(parts, axis=0).T.astype(BF16)
    for n in range(GQA_KV_HEADS):
        vg_ref[n * V_ROWS:n * V_ROWS + hd, :] = t[o_v + n * hd:o_v + (n + 1) * hd].astype(BF16)
        vg_ref[n * V_ROWS + hd:(n + 1) * V_ROWS, :] = ones_rows


def _proj(xs, modsel, g, wts, tabs, lat_tiles):
    bsz, s, d = xs.shape
    tm = ROW_TILE
    nt = s // tm
    ncs = KV_CHUNK // tm
    nb = PROJ_BATCH if bsz % PROJ_BATCH == 0 else 1
    (wu, wg, wt, wqb, wkvb, gq, gk, gqa, gkva, gmq, gmk) = wts
    cg, sg, cm, sm = tabs
    row = lambda b, i: (b, i, 0)
    col = lambda b, i: (b, 0, i)
    tab = lambda b, i: (0, i)
    in_specs = [
        pl.BlockSpec((nb, tm, d), row),
        pl.BlockSpec((nb, None, 1, 6 * d), lambda b, i: (b, jnp.where(i >= lat_tiles, 1, 0), 0, 0)),
        _const_spec(g.shape), _const_spec(wu.shape), _const_spec(wg.shape), _const_spec(wt.shape),
        _const_spec(wqb.shape), _const_spec(wkvb.shape),
        _const_spec(gq.shape), _const_spec(gk.shape), _const_spec(gqa.shape), _const_spec(gkva.shape),
        _const_spec(gmq.shape), _const_spec(gmk.shape),
        pl.BlockSpec((cg.shape[0], tm), tab), pl.BlockSpec((sg.shape[0], tm), tab),
        pl.BlockSpec((cm.shape[0], tm), tab), pl.BlockSpec((sm.shape[0], tm), tab),
    ]
    out_shape = [
        jax.ShapeDtypeStruct((bsz, s, 3 * HY_WIDTH), BF16),
        jax.ShapeDtypeStruct((bsz, s, 3 * d), BF16),
        jax.ShapeDtypeStruct((bsz, GQA_HEADS * GQA_HEAD_DIM, s), BF16),
        jax.ShapeDtypeStruct((bsz, s, GQA_KV_HEADS * V7X_LANES), BF16),
        jax.ShapeDtypeStruct((bsz, GQA_KV_HEADS * V_ROWS, s), BF16),
        jax.ShapeDtypeStruct((bsz, MLA_HEADS * MLA_HEAD_PAD, s), BF16),
        jax.ShapeDtypeStruct((bsz, s, MLA_HEADS * MLA_HEAD_PAD), BF16),
        jax.ShapeDtypeStruct((bsz, s // KV_CHUNK, MLA_HEADS * V_ROWS, KV_CHUNK), BF16),
    ]
    out_specs = [
        pl.BlockSpec((nb, tm, 3 * HY_WIDTH), row),
        pl.BlockSpec((nb, tm, 3 * d), row),
        pl.BlockSpec((nb, GQA_HEADS * GQA_HEAD_DIM, tm), col),
        pl.BlockSpec((nb, tm, GQA_KV_HEADS * V7X_LANES), row),
        pl.BlockSpec((nb, GQA_KV_HEADS * V_ROWS, tm), col),
        pl.BlockSpec((nb, MLA_HEADS * MLA_HEAD_PAD, tm), col),
        pl.BlockSpec((nb, tm, MLA_HEADS * MLA_HEAD_PAD), row),
        pl.BlockSpec((nb, None, MLA_HEADS * V_ROWS, tm), lambda b, i: (b, i // ncs, 0, i % ncs)),
    ]
    return pl.pallas_call(
        _proj_kernel,
        grid=(bsz // nb, nt),
        in_specs=in_specs,
        out_specs=out_specs,
        out_shape=out_shape,
        compiler_params=_cparams(("parallel", "arbitrary"), 56),
        name="in_proj",
    )(xs, modsel, g, wu, wg, wt, wqb, wkvb, gq, gk, gqa, gkva, gmq, gmk, cg, sg, cm, sm)


def _short_conv_kernel(u_ref, up_ref, un_ref, w_ref, x0_ref, z_ref, *, n_tiles):
    i = pl.program_id(1)
    u = u_ref[...].astype(F32)
    tm = u.shape[0]
    prev = jnp.where(i > 0, up_ref[7:8, :].astype(F32), 0.0)
    nxt = jnp.where(i < n_tiles - 1, un_ref[0:1, :].astype(F32), 0.0)
    ridx = lax.broadcasted_iota(jnp.int32, u.shape, 0)
    up = jnp.where(ridx == 0, prev, pltpu.roll(u, 1, axis=0))
    dn = jnp.where(ridx == tm - 1, nxt, pltpu.roll(u, tm - 1, axis=0))
    uc = up * w_ref[0:1, :] + u * w_ref[1:2, :] + dn * w_ref[2:3, :]
    c = HY_WIDTH
    x0_ref[...] = uc[:, :c].astype(BF16)
    z_ref[...] = (uc[:, c:2 * c] * uc[:, 2 * c:]).astype(BF16)


def _short_conv(u, short_w, row0, rows):
    bsz, s, c3 = u.shape
    tm = math.gcd(math.gcd(rows, row0), CONV_TILE) if row0 else math.gcd(rows, CONV_TILE)
    nt = rows // tm
    t0 = row0 // tm
    r8 = tm // 8
    last8 = s // 8 - 1
    return pl.pallas_call(
        functools.partial(_short_conv_kernel, n_tiles=nt),
        grid=(bsz, nt),
        in_specs=[
            pl.BlockSpec((None, tm, c3), lambda b, i: (b, t0 + i, 0)),
            pl.BlockSpec((None, 8, c3), lambda b, i: (b, jnp.maximum((t0 + i) * r8 - 1, 0), 0)),
            pl.BlockSpec((None, 8, c3), lambda b, i: (b, jnp.minimum((t0 + i + 1) * r8, last8), 0)),
            _const_spec(short_w.shape),
        ],
        out_specs=[pl.BlockSpec((None, tm, HY_WIDTH), lambda b, i: (b, i, 0))] * 2,
        out_shape=[jax.ShapeDtypeStruct((bsz, rows, HY_WIDTH), BF16)] * 2,
        compiler_params=_cparams(("parallel", "arbitrary"), 32),
        name="hyena_short_conv",
    )(u, u, u, short_w)


def _filter_kernel(zf_ref, w1_ref, b1_ref, w2_ref, b2_ref, fr_ref, w3_ref, dl_ref, h_ref, ss_ref):
    i = pl.program_id(0)
    zf = zf_ref[...]
    tl = zf.shape[0]
    h = jnp.sin(fr_ref[0:1, :] * (_dot_hi(zf, w1_ref[...]) + b1_ref[...]))
    h = jnp.sin(fr_ref[1:2, :] * (_dot_hi(h, w2_ref[...]) + b2_ref[...]))
    h = _dot_hi(h, w3_ref[...])
    decay = jnp.exp(-zf[:, 0:1] * dl_ref[...])
    c = HY_WIDTH
    hf = h[:, :c] * decay
    ridx = lax.broadcasted_iota(jnp.int32, (tl, c), 0) + i * tl
    hb = jnp.where(ridx == 0, 0.0, h[:, c:] * decay)
    h_ref[0] = hf.astype(BF16)
    h_ref[1] = hb.astype(BF16)
    ss = jnp.sum(hf * hf + hb * hb, axis=0, keepdims=True)

    @pl.when(i == 0)
    def _():
        ss_ref[...] = ss

    @pl.when(i > 0)
    def _():
        ss_ref[...] += ss


def _filter(zfeat, fw, deltas):
    length = zfeat.shape[0]
    tl = min(length, 1024)
    w1, b1, w2, b2, fr, w3 = fw
    return pl.pallas_call(
        _filter_kernel,
        grid=(length // tl,),
        in_specs=[pl.BlockSpec((tl, zfeat.shape[1]), lambda i: (i, 0))]
        + [_const_spec(a.shape) for a in (w1, b1, w2, b2, fr, w3, deltas)],
        out_specs=[pl.BlockSpec((2, tl, HY_WIDTH), lambda i: (0, i, 0)),
                   pl.BlockSpec((1, HY_WIDTH), lambda i: (0, 0))],
        out_shape=[jax.ShapeDtypeStruct((2, length, HY_WIDTH), BF16),
                   jax.ShapeDtypeStruct((1, HY_WIDTH), F32)],
        compiler_params=_cparams(("arbitrary",), 40),
        name="hyena_filter",
    )(zfeat, w1, b1, w2, b2, fr, w3, deltas)


def _fft_a_kernel(g_ref, x_ref, o_ref, *, nb, cw):
    for j in range(nb):
        sl = slice(j * cw, (j + 1) * cw)
        x = jnp.concatenate([x_ref[p, :, sl] for p in range(x_ref.shape[0])], axis=0)
        o_ref[:, sl] = _dot(g_ref[j], x).astype(o_ref.dtype)


def _fft_a(x3d, gmat):
    bx, parts, k1, w = x3d.shape
    n2, two_n1, _ = gmat.shape
    cw = w // n2
    nb = FFT_NB
    return pl.pallas_call(
        functools.partial(_fft_a_kernel, nb=nb, cw=cw),
        grid=(bx, n2 // nb),
        in_specs=[pl.BlockSpec((nb, two_n1, parts * k1), lambda b, j: (j, 0, 0)),
                  pl.BlockSpec((None, parts, k1, nb * cw), lambda b, j: (b, 0, 0, j))],
        out_specs=pl.BlockSpec((None, two_n1, nb * cw), lambda b, j: (b, 0, j)),
        out_shape=jax.ShapeDtypeStruct((bx, two_n1, w), BF16),
        compiler_params=_cparams(("parallel", "arbitrary"), 40),
        name="fft_stage_a",
    )(gmat, x3d)


def _fft_filter_b_kernel(fb_ref, a_ref, ss_ref, kf_ref, *, inv_n):
    n = a_ref.shape[3]
    rs = lax.rsqrt(ss_ref[...] + EPS) * inv_n
    for r in range(a_ref.shape[2]):
        xf = _dot(fb_ref[...], jnp.concatenate([a_ref[0, 0, r], a_ref[0, 1, r]], axis=0))
        xb = _dot(fb_ref[...], jnp.concatenate([a_ref[1, 0, r], a_ref[1, 1, r]], axis=0))
        kf_ref[0, r] = (xf[:n] + xb[:n]) * rs
        kf_ref[1, r] = (xf[n:] - xb[n:]) * rs


def _fft_filter_b(a5, fb, ssq, inv_n):
    _, _, n1, n2, c = a5.shape
    kb = min(FFT_KB, n1)
    return pl.pallas_call(
        functools.partial(_fft_filter_b_kernel, inv_n=inv_n),
        grid=(n1 // kb,),
        in_specs=[_const_spec(fb.shape),
                  pl.BlockSpec((2, 2, kb, n2, c), lambda k: (0, 0, k, 0, 0)),
                  _const_spec(ssq.shape)],
        out_specs=pl.BlockSpec((2, kb, n2, c), lambda k: (0, k, 0, 0)),
        out_shape=jax.ShapeDtypeStruct((2, n1, n2, c), F32),
        compiler_params=_cparams(("arbitrary",), 32),
        name="fft_filter_stage_b",
    )(fb, a5, ssq)


def _fft_b_kernel(fb_ref, fbi_ref, a_ref, kf_ref, o_ref):
    kb, n, cw = a_ref.shape[1:]
    a_all = jnp.concatenate([jnp.concatenate([a_ref[0, r], a_ref[1, r]], axis=0) for r in range(kb)], axis=1)
    x = _dot(fb_ref[...], a_all)
    ys = []
    for r in range(kb):
        xr, xi = x[:n, r * cw:(r + 1) * cw], x[n:, r * cw:(r + 1) * cw]
        kr, ki = kf_ref[0, r], kf_ref[1, r]
        ys.append(jnp.concatenate([xr * kr - xi * ki, xr * ki + xi * kr], axis=0).astype(BF16))
    c = _dot(fbi_ref[...], jnp.concatenate(ys, axis=1))
    for r in range(kb):
        o_ref[0, r] = c[:n, r * cw:(r + 1) * cw].astype(o_ref.dtype)
        o_ref[1, r] = c[n:, r * cw:(r + 1) * cw].astype(o_ref.dtype)


def _fft_b(a5, kf, fb, fbi):
    bsz, _, n1, n2, c = a5.shape
    kb = min(FFT_KB, n1)
    return pl.pallas_call(
        _fft_b_kernel,
        grid=(n1 // kb, bsz),
        in_specs=[_const_spec(fb.shape), _const_spec(fbi.shape),
                  pl.BlockSpec((None, 2, kb, n2, c), lambda k, b: (b, 0, k, 0, 0)),
                  pl.BlockSpec((2, kb, n2, c), lambda k, b: (0, k, 0, 0))],
        out_specs=pl.BlockSpec((None, 2, kb, n2, c), lambda k, b: (b, 0, k, 0, 0)),
        out_shape=jax.ShapeDtypeStruct(a5.shape, BF16),
        compiler_params=_cparams(("arbitrary", "arbitrary"), 32),
        name="fft_stage_b",
    )(fb, fbi, a5, kf)


def _fft_c_kernel(h_ref, c_ref, o_ref, *, nb, cw):
    parts, k1 = o_ref.shape[0], o_ref.shape[1]
    for j in range(nb):
        sl = slice(j * cw, (j + 1) * cw)
        y = _dot(h_ref[j], c_ref[:, sl])
        for p in range(parts):
            o_ref[p, :, sl] = y[p * k1:(p + 1) * k1].astype(o_ref.dtype)


def _fft_c(c2d, hmat, parts):
    bx, two_n1, w = c2d.shape
    n2, rows, _ = hmat.shape
    k1 = rows // parts
    cw = w // n2
    nb = FFT_NB
    return pl.pallas_call(
        functools.partial(_fft_c_kernel, nb=nb, cw=cw),
        grid=(bx, n2 // nb),
        in_specs=[pl.BlockSpec((nb, rows, two_n1), lambda b, j: (j, 0, 0)),
                  pl.BlockSpec((None, two_n1, nb * cw), lambda b, j: (b, 0, j))],
        out_specs=pl.BlockSpec((None, parts, k1, nb * cw), lambda b, j: (b, 0, 0, j)),
        out_shape=jax.ShapeDtypeStruct((bx, parts, k1, w), BF16),
        compiler_params=_cparams(("parallel", "arbitrary"), 40),
        name="fft_stage_c",
    )(hmat, c2d)


def _ctx_conv_kernel(fc_ref, fci_ref, h_ref, ss_ref, z_ref, o_ref):
    n = fc_ref.shape[0] // 2
    fc = fc_ref[...]
    kf = _dot(fc, h_ref[0])
    kb = _dot(fc, h_ref[1])
    rs = lax.rsqrt(ss_ref[...] + EPS)
    kr = (kf[:n] + kb[:n]) * rs
    ki = (kf[n:] - kb[n:]) * rs
    x = _dot(fc, z_ref[...])
    xr, xi = x[:n], x[n:]
    y = jnp.concatenate([xr * kr - xi * ki, xr * ki + xi * kr], axis=0).astype(BF16)
    o_ref[...] = _dot(fci_ref[...], y).astype(o_ref.dtype)


def _ctx_conv(fc, fci, hfb, ssq, z):
    bsz, lc, c = z.shape
    blk = pl.BlockSpec((None, lc, c), lambda b: (b, 0, 0))
    return pl.pallas_call(
        _ctx_conv_kernel,
        grid=(bsz,),
        in_specs=[_const_spec(fc.shape), _const_spec(fci.shape), _const_spec(hfb.shape),
                  _const_spec(ssq.shape), blk],
        out_specs=blk,
        out_shape=jax.ShapeDtypeStruct((bsz, lc, c), BF16),
        compiler_params=_cparams(("arbitrary",), 32),
        name="hyena_ctx_conv",
    )(fc, fci, hfb, ssq, z)


@functools.lru_cache(maxsize=None)
def _fft_tables(length):
    n = 2 * length
    n2 = FFT_N2
    n1 = n // n2
    k1 = length // n2
    kk = np.arange(n1)[:, None]
    g = np.empty((n2, 2 * n1, k1), np.float64)
    g2 = np.empty((n2, 2 * n1, 2 * k1), np.float64)
    h2 = np.empty((n2, 2 * k1, 2 * n1), np.float64)
    nn = np.arange(k1)[None, :]
    for j in range(n2):
        ang = 2.0 * np.pi * (((n2 * nn * kk) % n) + (j * kk) % n) / n
        c_, s_ = np.cos(ang), np.sin(ang)
        g[j] = np.concatenate([c_, -s_], axis=0)
        g2[j] = np.block([[c_, s_], [-s_, c_]])
        h2[j] = np.block([[c_.T, -s_.T], [s_.T, c_.T]])
    a = np.arange(n2)
    ph = 2.0 * np.pi * ((a[:, None] * a[None, :]) % n2) / n2
    c, s = np.cos(ph), np.sin(ph)
    fb = np.block([[c, s], [-s, c]])
    fbi = np.block([[c, -s], [s, c]])
    return (jnp.asarray(g, BF16), jnp.asarray(g2, BF16), jnp.asarray(h2, BF16), jnp.asarray(fb, BF16),
            jnp.asarray(fbi, BF16), n1, k1)


@functools.lru_cache(maxsize=None)
def _dft_tables(length):
    n = 2 * length
    k = np.arange(n)[:, None]
    t = np.arange(length)[None, :]
    ang = 2.0 * np.pi * ((k * t) % n) / n
    fc = np.concatenate([np.cos(ang), -np.sin(ang)], axis=0)
    fci = np.concatenate([np.cos(ang).T, -np.sin(ang).T], axis=1) / n
    return jnp.asarray(fc, BF16), jnp.asarray(fci, BF16)


@functools.lru_cache(maxsize=None)
def _filter_features(length):
    t = np.linspace(0.0, 1.0, length, dtype=np.float32)[:, None]
    w = (2.0 * math.pi * np.arange(length, dtype=np.float32)[:, None] / length).astype(np.float32)
    f = np.linspace(1e-4, HY_BANDS - 1, HY_BANDS, dtype=np.float32)[None, :]
    z = np.concatenate([t, np.cos(f * w), -np.sin(f * w)], axis=-1).astype(np.float32)
    zp = np.zeros((length, V7X_LANES), np.float32)
    zp[:, :HY_EMB_DIM] = z
    return jnp.asarray(zp)


def _hyena_deltas():
    max_decay = math.log(HY_DECAY_TARGET) / HY_FAST_DECAY
    min_decay = math.log(HY_DECAY_TARGET) / HY_SLOW_DECAY
    return jnp.abs(jnp.linspace(min_decay, max_decay, HY_WIDTH, dtype=F32))[None, :]


def _hyena(u, short_w, fw, lat, lc):
    bsz = u.shape[0]
    c = HY_WIDTH
    deltas = _hyena_deltas()
    x0, z = _short_conv(u, short_w, 0, lat)
    gmat, gmat2, hmat2, fb, fbi, n1, k1 = _fft_tables(lat)
    n2 = FFT_N2
    hfb, ssq = _filter(_filter_features(lat), fw, deltas)
    fa = _fft_a(hfb.reshape(2, 1, k1, n2 * c), gmat)
    kf = _fft_filter_b(fa.reshape(2, 2, n1, n2, c), fb, ssq, 1.0 / (2 * lat))
    npair = (bsz + 1) // 2
    zp = z if bsz % 2 == 0 else jnp.concatenate([z, jnp.zeros_like(z[:1])], axis=0)
    za = _fft_a(zp.reshape(npair, 2, k1, n2 * c), gmat2)
    zc = _fft_b(za.reshape(npair, 2, n1, n2, c), kf, fb, fbi)
    conv = _fft_c(zc.reshape(npair, 2 * n1, n2 * c), hmat2, 2).reshape(2 * npair, lat, c)[:bsz]
    x0_c, z_c = _short_conv(u, short_w, lat, lc)
    hfb_c, ssq_c = _filter(_filter_features(lc), fw, deltas)
    fc, fci = _dft_tables(lc)
    conv_c = _ctx_conv(fc, fci, hfb_c, ssq_c, z_c)
    return (conv, x0, z), (conv_c, x0_c, z_c)


def _gqa_kernel(q_ref, k_ref, v_ref, sink_ref, o_ref, *bufs, lat, lc, tq):
    sub = q_ref.shape[1] // tq
    for t in range(sub):
        lanes = slice(t * tq, (t + 1) * tq)
        _gqa_tile(pl.program_id(1) * sub + t, q_ref.at[:, lanes], k_ref, v_ref, sink_ref, o_ref.at[:, lanes], bufs,
                  lat, lc)


def _gqa_tile(i, q_ref, k_ref, v_ref, sink_ref, o_ref, bufs, lat, lc):
    tq = q_ref.shape[1]
    s_len = k_ref.shape[0]
    hd = GQA_HEAD_DIM
    w = WINDOW
    nwin = tq + 2 * w
    start = pl.multiple_of(jnp.clip(i * tq - w, 0, s_len - nwin), w)
    k_all = jnp.concatenate([k_ref[pl.ds(start, nwin), :], k_ref[lat:lat + lc, :]], axis=0)
    v_all = jnp.concatenate([v_ref[:, pl.ds(start, nwin)], v_ref[:, lat:lat + lc]], axis=1)
    q_pos = i * tq + lax.broadcasted_iota(jnp.int32, (1, tq), 1)
    k_pos = start + lax.broadcasted_iota(jnp.int32, (nwin, 1), 0)
    bias = (jnp.where(jnp.abs(k_pos - q_pos) <= w, 0.0, NEG_BIG)
            + jnp.where(k_pos < lat, 0.0, NEG_BIG)
            + jnp.where(q_pos < lat, 0.0, NEG_BIG))
    bias = jnp.concatenate([bias, jnp.zeros((lc, tq), F32)], axis=0)
    bias2 = jnp.concatenate([bias, bias], axis=1)
    zq = jnp.zeros((V7X_LANES - hd, 2 * tq), BF16)

    def score(pair, dst):
        g = pair // (GQA_GROUP // 2)
        r = 2 * pair * hd
        q2 = jnp.concatenate([q_ref[r:r + hd, :], q_ref[r + hd:r + 2 * hd, :]], axis=1)
        s = _dot(k_all[:, g * V7X_LANES:(g + 1) * V7X_LANES], jnp.concatenate([q2, zq], axis=0)) + bias2
        dst[...] = s
        return jnp.max(s, axis=0, keepdims=True)

    def update(pair, s_sc, s_max):
        g = pair // (GQA_GROUP // 2)
        r = 2 * pair * hd
        sink = jnp.concatenate([sink_ref[2 * pair:2 * pair + 1, :], sink_ref[2 * pair + 1:2 * pair + 2, :]],
                               axis=1) * LOG2E
        m = jnp.maximum(s_max, sink)
        p = jnp.exp2(s_sc[...] - m).astype(BF16)
        pv = _dot(v_all[g * V_ROWS:(g + 1) * V_ROWS, :], p)
        o = pv[:hd] / (pv[hd:hd + 1] + jnp.exp2(sink - m))
        o_ref[r:r + hd, :] = o[:, :tq].astype(o_ref.dtype)
        o_ref[r + hd:r + 2 * hd, :] = o[:, tq:].astype(o_ref.dtype)

    npair = GQA_HEADS // 2
    mx = [score(j, bufs[j]) for j in range(npair)]
    for j in range(npair):
        update(j, bufs[j], mx[j])


def _gqa(qg, kg, vg, sink_rows, lat, lc):
    bsz, nq, s = qg.shape
    tq = GQA_TQ
    sub = GQA_SUBTILES if (s // tq) % GQA_SUBTILES == 0 else 1
    wq = tq * sub
    return pl.pallas_call(
        functools.partial(_gqa_kernel, lat=lat, lc=lc, tq=tq),
        grid=(bsz, s // wq),
        in_specs=[
            pl.BlockSpec((None, nq, wq), lambda b, i: (b, 0, i)),
            pl.BlockSpec((None, s, kg.shape[2]), lambda b, i: (b, 0, 0)),
            pl.BlockSpec((None, vg.shape[1], s), lambda b, i: (b, 0, 0)),
            _const_spec(sink_rows.shape),
        ],
        out_specs=pl.BlockSpec((None, nq, wq), lambda b, i: (b, 0, i)),
        out_shape=jax.ShapeDtypeStruct((bsz, nq, s), BF16),
        scratch_shapes=[pltpu.VMEM((tq + 2 * WINDOW + lc, 2 * tq), F32)] * (GQA_HEADS // 2),
        compiler_params=_cparams(("parallel", "arbitrary"), 40),
        name="gqa_window_attn",
    )(qg, kg, vg, sink_rows)


def _mla_update(s_ref, s_max, vt, m, acc):
    m_new = jnp.maximum(m, s_max)
    alpha = jnp.exp2(m - m_new)
    p = jnp.exp2(s_ref[...] - m_new).astype(BF16)
    acc = alpha * acc + _dot(vt, p)
    return m_new, acc


def _mla_kernel(q_ref, k_ref, v_ref, o_ref, *bufs, nc, ctx_only, tq):
    bufs = [b.at[:, 0:tq] for b in bufs]
    for t in range(q_ref.shape[1] // tq):
        lanes = slice(t * tq, (t + 1) * tq)
        _mla_tile(q_ref[:, lanes], k_ref, v_ref, o_ref.at[:, lanes], bufs, nc, ctx_only)


def _mla_tile(q, k_ref, v_ref, o_ref, bufs, nc, ctx_only):
    tq = q.shape[1]
    ck = KV_CHUNK
    nbuf = len(bufs)
    m = jnp.full((1, tq), NEG_BIG, F32)
    acc = jnp.zeros((V_ROWS, tq), F32)

    def score(k, dst):
        s = _dot(k, q)
        dst[...] = s
        return jnp.max(s, axis=0, keepdims=True)

    def score_chunk(j, dst):
        return score(k_ref[pl.ds(pl.multiple_of(j * ck, ck), ck), :], dst)

    if ctx_only:
        s_x = bufs[0].at[0:KV_SUB, :]
        mx = score(k_ref[k_ref.shape[0] - KV_SUB:, :], s_x)
        m, acc = _mla_update(s_x, mx, v_ref[v_ref.shape[0] - 1][:, ck - KV_SUB:], m, acc)
    else:
        ahead = nbuf - 1
        mx = [None] * nbuf
        for j in range(min(ahead, nc)):
            mx[j] = score_chunk(j, bufs[j])

        def step(j, slot, m, acc, mx, last):
            if not last:
                nxt = (slot + ahead) % nbuf
                mx[nxt] = score_chunk(j + ahead, bufs[nxt])
            return _mla_update(bufs[slot], mx[slot], v_ref[j], m, acc)

        per_body = MLA_UNROLL * nbuf

        def body(g, c):
            m, acc, mx = c[0], c[1], list(c[2:])
            for u in range(per_body):
                m, acc = step(g * per_body + u, u % nbuf, m, acc, mx, False)
            return (m, acc, *mx)

        n_body = max(nc - ahead, 0) // per_body
        if n_body > 0:
            out = lax.fori_loop(0, n_body, body, (m, acc, *[m if x is None else x for x in mx]))
            m, acc, mx = out[0], out[1], list(out[2:])
        for j in range(n_body * per_body, nc):
            m, acc = step(j, j % nbuf, m, acc, mx, j + ahead >= nc)
    o_ref[...] = (acc[:MLA_V_DIM] / acc[MLA_V_DIM:MLA_V_DIM + 1]).astype(o_ref.dtype)


def _mla_call(qm, km, vm, tq, sub, q_tile0, n_q, ctx_only):
    bsz, _, s = qm.shape
    nc = vm.shape[1]
    wq = tq * sub
    if ctx_only:
        k_spec = pl.BlockSpec((None, KV_SUB, MLA_HEAD_PAD), lambda b, h, i: (b, s // KV_SUB - 1, h))
        v_spec = pl.BlockSpec((None, 1, V_ROWS, KV_CHUNK), lambda b, h, i: (b, nc - 1, h, 0))
    else:
        k_spec = pl.BlockSpec((None, s, MLA_HEAD_PAD), lambda b, h, i: (b, 0, h))
        v_spec = pl.BlockSpec((None, nc, V_ROWS, KV_CHUNK), lambda b, h, i: (b, 0, h, 0))
    return pl.pallas_call(
        functools.partial(_mla_kernel, nc=nc, ctx_only=ctx_only, tq=tq),
        grid=(bsz, MLA_HEADS, n_q),
        in_specs=[pl.BlockSpec((None, MLA_HEAD_PAD, wq), lambda b, h, i: (b, h, q_tile0 + i)), k_spec, v_spec],
        out_specs=pl.BlockSpec((None, MLA_V_DIM, wq), lambda b, h, i: (b, h, i)),
        out_shape=jax.ShapeDtypeStruct((bsz, MLA_HEADS * MLA_V_DIM, n_q * wq), BF16),
        scratch_shapes=[pltpu.VMEM((KV_CHUNK, tq + V7X_LANES), F32)] * MLA_NBUF,
        compiler_params=_cparams(("parallel", "arbitrary", "arbitrary"), 40),
        name="mla_attn_ctx" if ctx_only else "mla_attn",
    )(qm, km, vm)


def _mla(qm, km, vm, lat, lc):
    tq = min(MLA_TQ, lat)
    sub = MLA_SUBTILES if lat % (tq * MLA_SUBTILES) == 0 else 1
    assert lc == KV_SUB and lat % lc == 0 and lat % tq == 0
    return (_mla_call(qm, km, vm, tq, sub, 0, lat // (tq * sub), False),
            _mla_call(qm, km, vm, lc, 1, lat // lc, 1, True))


def _merge_mlp_kernel(x_ref, mod_ref, hl_refs, hc_refs, sk_ref, yg_ref, yml_ref, ymc_ref, gt_ref, wb_ref, wo_ref,
                      g_ref, w1_ref, w2_ref, o_ref, *, lat_tiles):
    for n in range(x_ref.shape[0]):
        _merge_mlp_one(x_ref.at[n], mod_ref.at[n], [r.at[n] for r in hl_refs], [r.at[n] for r in hc_refs], sk_ref,
                       yg_ref.at[n], yml_ref.at[n], ymc_ref.at[n], gt_ref.at[n], wb_ref, wo_ref, g_ref, w1_ref,
                       w2_ref, o_ref.at[n], lat_tiles)


def _merge_mlp_one(x_ref, mod_ref, hl_refs, hc_refs, sk_ref, yg_ref, yml_ref, ymc_ref, gt_ref, wb_ref, wo_ref,
                   g_ref, w1_ref, w2_ref, o_ref, lat_tiles):
    d = D_MODEL
    is_lat = pl.program_id(1) < lat_tiles
    conv, x0, z = [jnp.where(is_lat, a[...], b[...]).astype(F32) for a, b in zip(hl_refs, hc_refs)]
    yh = (x0 * (conv + z * sk_ref[...])).astype(BF16)
    yg = yg_ref[...].astype(F32).T.astype(BF16)
    ym = jnp.where(is_lat, yml_ref[...], ymc_ref[...]).astype(F32).T.astype(BF16)
    merged = (gt_ref[:, 0:d].astype(F32) * _dot(yh, wb_ref[0])
              + gt_ref[:, d:2 * d].astype(F32) * _dot(yg, wb_ref[1])
              + gt_ref[:, 2 * d:].astype(F32) * _dot(ym, wb_ref[2]))
    res = _dot(merged.astype(BF16), wo_ref[...])
    x_mix = x_ref[...] + mod_ref[:, 2 * d:3 * d] * res
    o_ref[...] = _mlp_rows(x_mix, mod_ref, g_ref, w1_ref, w2_ref)


def _merge_mlp(xs, modsel, hy_lat, hy_ctx, skip, yg, ym_lat, ym_ctx, gates, wb, wo, g, w1, w2, lat_tiles, n_tiles):
    bsz, _, d = xs.shape
    tm = ROW_TILE
    c = HY_WIDTH
    nb = MERGE_BATCH if bsz % MERGE_BATCH == 0 else 1
    row = lambda b, i: (b, i, 0)
    col = lambda b, i: (b, 0, i)
    lat_row = pl.BlockSpec((nb, tm, c), lambda b, i: (b, jnp.minimum(i, lat_tiles - 1), 0))
    ctx_row = pl.BlockSpec((nb, tm, c), lambda b, i: (b, jnp.maximum(i - lat_tiles, 0), 0))
    return pl.pallas_call(
        functools.partial(_merge_mlp_kernel, lat_tiles=lat_tiles),
        grid=(bsz // nb, n_tiles),
        in_specs=[
            pl.BlockSpec((nb, tm, d), row),
            pl.BlockSpec((nb, None, 1, 6 * d), lambda b, i: (b, jnp.where(i >= lat_tiles, 1, 0), 0, 0)),
            [lat_row] * 3, [ctx_row] * 3, _const_spec(skip.shape),
            pl.BlockSpec((nb, c, tm), col),
            pl.BlockSpec((nb, c, tm), lambda b, i: (b, 0, jnp.minimum(i, lat_tiles - 1))),
            pl.BlockSpec((nb, c, tm), lambda b, i: (b, 0, jnp.maximum(i - lat_tiles, 0))),
            pl.BlockSpec((nb, tm, 3 * d), row),
            _const_spec(wb.shape), _const_spec(wo.shape),
            _const_spec(g.shape), _const_spec(w1.shape), _const_spec(w2.shape),
        ],
        out_specs=pl.BlockSpec((nb, tm, d), row),
        out_shape=jax.ShapeDtypeStruct((bsz, n_tiles * tm, d), F32),
        compiler_params=_cparams(("parallel", "arbitrary"), 56),
        name="merge_mlp",
    )(xs, modsel, list(hy_lat), list(hy_ctx), skip, yg, ym_lat, ym_ctx, gates, wb, wo, g, w1, w2)


def _mlp_rows(x, mod_ref, g_ref, w1_ref, w2_ref):
    d = D_MODEL
    xn = x * lax.rsqrt(jnp.mean(x * x, axis=-1, keepdims=True) + EPS) * g_ref[...]
    h = (xn * (1.0 + mod_ref[:, 4 * d:5 * d]) + mod_ref[:, 3 * d:4 * d]).astype(BF16)
    acc = jnp.zeros(x.shape, F32)
    for j in range(D_FF // FF_CHUNK):
        sl = slice(j * FF_CHUNK, (j + 1) * FF_CHUNK)
        a = jnp.maximum(_dot(h, w1_ref[:, sl]), 0.0)
        acc = acc + _dot((a * a).astype(BF16), w2_ref[sl, :])
    return x + mod_ref[:, 5 * d:] * acc


def _rope_tables_t(rows, dim, lc):
    n_freq = dim // 4
    inv = ROPE_BASE ** (-jnp.arange(n_freq, dtype=F32) / n_freq)
    r = jnp.repeat(jnp.arange(rows, dtype=F32), GRID_W)
    col = jnp.tile(jnp.arange(GRID_W, dtype=F32), rows)
    ang = jnp.concatenate([r[:, None] * inv, col[:, None] * inv], axis=-1)
    cos_t = jnp.concatenate([jnp.cos(ang).T, jnp.ones((dim // 2, lc), F32)], axis=1)
    sin_t = jnp.concatenate([jnp.sin(ang).T, jnp.zeros((dim // 2, lc), F32)], axis=1)
    return cos_t, sin_t


def _lane_bcast(v):
    return jnp.broadcast_to(v.astype(F32)[:, None], (v.shape[0], ROW_TILE))


def _layer_weights(l, w_in, gqa_q_norm, gqa_k_norm, mla_q_a_norm, mla_kv_a_norm, w_q_b, w_kv_b, mla_q_norm, mla_k_norm):
    w = w_in[l]
    o = np.cumsum([0, 3 * HY_WIDTH, GQA_HEADS * GQA_HEAD_DIM, GQA_KV_HEADS * GQA_HEAD_DIM,
                   GQA_KV_HEADS * GQA_HEAD_DIM, MLA_Q_RANK, MLA_KV_RANK, MLA_ROPE_DIM, 3 * D_MODEL])
    wu = w[:, o[0]:o[1]].astype(BF16)
    wt = w[:, o[1]:o[7]].T.astype(BF16)
    wg = w[:, o[7]:o[8]].astype(BF16)
    wqb = w_q_b[l].T.astype(BF16)
    wkvb = w_kv_b[l].T.astype(BF16)
    return (wu, wg, wt, wqb, wkvb,
            _lane_bcast(gqa_q_norm[l]), _lane_bcast(gqa_k_norm[l]),
            _lane_bcast(mla_q_a_norm[l]), _lane_bcast(mla_kv_a_norm[l]),
            _lane_bcast(mla_q_norm[l]), _lane_bcast(mla_k_norm[l]))


def kernel(x, c, ctx, c_ctx, w_mod, b_mod, norm_mix_g, norm_mlp_g, w_in, hy_short_w, hy_f1_w, hy_f1_b, hy_f2_w, hy_f2_b, hy_sin_freq, hy_f3_w, hy_skip, gqa_q_norm, gqa_k_norm, gqa_sink, mla_q_a_norm, mla_kv_a_norm, w_q_b, w_kv_b, mla_q_norm, mla_k_norm, w_branch, w_out, w_mlp1, w_mlp2):
    bsz, lat, d = x.shape
    lc = ctx.shape[1]
    depth = w_mod.shape[0]
    s = lat + lc
    assert d == D_MODEL and lat % GRID_W == 0 and lat % ROW_TILE == 0 and lc == ROW_TILE and s % KV_CHUNK == 0
    lat_tiles = lat // ROW_TILE

    pad = (-(bsz + 1)) % 8
    cond = jnp.concatenate([c, c_ctx[None, :], jnp.zeros((pad, d), F32)], axis=0)
    mods = _mod_all(cond, w_mod, b_mod)

    tabs = _rope_tables_t(lat // GRID_W, GQA_HEAD_DIM, lc) + _rope_tables_t(lat // GRID_W, MLA_ROPE_DIM, lc)
    xs = jnp.concatenate([x, ctx], axis=1)

    for l in range(depth):
        ml = mods[l]
        modsel = jnp.stack([ml[:bsz], jnp.broadcast_to(ml[bsz][None], (bsz, 6 * d))], axis=1)[:, :, None, :]
        wts = _layer_weights(l, w_in, gqa_q_norm, gqa_k_norm, mla_q_a_norm, mla_kv_a_norm, w_q_b, w_kv_b,
                             mla_q_norm, mla_k_norm)
        u, gates, qg, kg, vg, qm, km, vm = _proj(xs, modsel, norm_mix_g[l][None, :], wts, tabs, lat_tiles)

        f1w = jnp.zeros((V7X_LANES, HY_FILTER_WIDTH), F32).at[:HY_EMB_DIM].set(hy_f1_w[l])
        fw = (f1w, hy_f1_b[l][None, :], hy_f2_w[l], hy_f2_b[l][None, :], hy_sin_freq[l], hy_f3_w[l])
        hy_lat, hy_ctx = _hyena(u, hy_short_w[l], fw, lat, lc)

        sink_rows = jnp.broadcast_to(gqa_sink[l].astype(F32)[:, None], (GQA_HEADS, GQA_TQ))
        yg = _gqa(qg, kg, vg, sink_rows, lat, lc)
        ym_lat, ym_ctx = _mla(qm, km, vm, lat, lc)

        n_tiles = lat_tiles if l == depth - 1 else s // ROW_TILE
        xs = _merge_mlp(xs, modsel, hy_lat, hy_ctx, hy_skip[l][None, :], yg, ym_lat, ym_ctx, gates,
                        w_branch[l].astype(BF16), w_out[l].astype(BF16), norm_mlp_g[l][None, :],
                        w_mlp1[l].astype(BF16), w_mlp2[l].astype(BF16), lat_tiles, n_tiles)
    return xs
```

```python
import functools
import math

import numpy as np
import jax
import jax.numpy as jnp
from jax import lax
from jax.experimental import pallas as pl
from jax.experimental.pallas import tpu as pltpu

D_MODEL = 1024
GRID_W = 64
HY_WIDTH = 512
HY_EMB_DIM = 33
HY_BANDS = (HY_EMB_DIM - 1) // 2
HY_FILTER_WIDTH = 64
HY_DECAY_TARGET = 1e-2
HY_FAST_DECAY = 0.3
HY_SLOW_DECAY = 1.5
GQA_HEADS = 8
GQA_KV_HEADS = 2
GQA_GROUP = GQA_HEADS // GQA_KV_HEADS
GQA_HEAD_DIM = 64
GQA_SCALE = GQA_HEAD_DIM ** -0.5
WINDOW = 128
MLA_HEADS = 8
MLA_Q_RANK = 384
MLA_KV_RANK = 256
MLA_NOPE_DIM = 64
MLA_ROPE_DIM = 32
MLA_V_DIM = 64
MLA_QK_DIM = MLA_NOPE_DIM + MLA_ROPE_DIM
MLA_SCALE = MLA_QK_DIM ** -0.5
D_FF = 4 * D_MODEL
ROPE_BASE = 10000.0
EPS = 1e-6
LOG2E = 1.4426950408889634
NEG_BIG = -1e30

V7X_LANES = 128
V7X_VMEM_BYTES = 64 * 1024 * 1024

ROW_TILE = 256
PROJ_BATCH = 2
MERGE_BATCH = 2
CONV_TILE = 512
MLA_HEAD_PAD = 128
V_ROWS = 80
KV_CHUNK = ROW_TILE
KV_SUB = 256
MLA_NBUF = 2
MLA_UNROLL = 8
MLA_TQ = 1024
MLA_SUBTILES = 4
GQA_TQ = 256
GQA_SUBTILES = 3
FF_CHUNK = 1024
FFT_N2 = 128
FFT_NB = 16
FFT_KB = 8

F32 = jnp.float32
BF16 = jnp.bfloat16


def _cparams(sem, vmem_mb):
    limit = vmem_mb * 1024 * 1024
    assert limit < V7X_VMEM_BYTES
    return pltpu.CompilerParams(dimension_semantics=sem, vmem_limit_bytes=limit)


def _dot(a, b):
    return jnp.dot(a, b, preferred_element_type=F32)


def _dot_hi(a, b):
    return jnp.dot(a, b, preferred_element_type=F32, precision=lax.Precision.HIGHEST)


def _ones_row_block(width):
    r = lax.broadcasted_iota(jnp.int32, (V_ROWS - MLA_V_DIM, width), 0)
    return jnp.where(r == 0, 1.0, 0.0).astype(BF16)


def _const_spec(shape):
    nd = len(shape)
    return pl.BlockSpec(shape, lambda *_: (0,) * nd, pipeline_mode=pl.Buffered(1))


def _mod_kernel(c_ref, w_ref, b_ref, o_ref):
    c = c_ref[...]
    s = c * jax.nn.sigmoid(c)
    o_ref[...] = _dot_hi(s, w_ref[...]) + b_ref[...]


def _mod_all(cond, w_mod, b_mod):
    depth, d, n = w_mod.shape
    rows = cond.shape[0]
    tn = 1536
    return pl.pallas_call(
        _mod_kernel,
        grid=(depth, n // tn),
        in_specs=[
            pl.BlockSpec((rows, d), lambda l, j: (0, 0)),
            pl.BlockSpec((None, d, tn), lambda l, j: (l, 0, j)),
            pl.BlockSpec((None, 1, tn), lambda l, j: (l, 0, j)),
        ],
        out_specs=pl.BlockSpec((None, rows, tn), lambda l, j: (l, 0, j)),
        out_shape=jax.ShapeDtypeStruct((depth, rows, n), F32),
        compiler_params=_cparams(("arbitrary", "arbitrary"), 40),
        name="adaln_mod",
    )(cond, w_mod, b_mod.reshape(depth, 1, n))


def _rms_rows(x, n):
    return lax.rsqrt(jnp.sum(x * x, axis=0, keepdims=True) * (1.0 / n) + EPS)


def _rope_rows(x1, x2, cs, sn):
    return x1 * cs - x2 * sn, x1 * sn + x2 * cs


def _proj_kernel(x_ref, mod_ref, *refs):
    consts, outs = refs[:16], refs[16:]
    for n in range(x_ref.shape[0]):
        _proj_one(x_ref.at[n], mod_ref.at[n], *consts, *[o.at[n] for o in outs])


def _proj_one(x_ref, mod_ref, g_ref, wu_ref, wg_ref, wt_ref, wqb_ref, wkvb_ref,
              gq_ref, gk_ref, gqa_ref, gkva_ref, gmq_ref, gmk_ref,
              cg_ref, sg_ref, cm_ref, sm_ref,
              u_ref, gate_ref, qg_ref, kg_ref, vg_ref, qm_ref, km_ref, vm_ref):
    d = D_MODEL
    x = x_ref[...]
    tm = x.shape[0]
    shift = mod_ref[:, 0:d]
    scale = mod_ref[:, d:2 * d]
    xn = x * lax.rsqrt(jnp.mean(x * x, axis=-1, keepdims=True) + EPS) * g_ref[...]
    h = xn * (1.0 + scale) + shift
    hb = h.astype(BF16)

    ht = h.T.astype(BF16)
    t = _dot(wt_ref[...], ht)
    o_q, o_k, o_v = 0, 512, 640
    o_cq, o_ckv, o_kr = 768, 768 + MLA_Q_RANK, 768 + MLA_Q_RANK + MLA_KV_RANK

    u_ref[...] = _dot(hb, wu_ref[...]).astype(BF16)

    def gate_cols(n):
        cols = slice(n * d, (n + 1) * d)
        gate_ref[:, cols] = jax.nn.sigmoid(_dot(hb, wg_ref[:, cols])).astype(BF16)

    gate_cols(0)
    cg, sg = cg_ref[...], sg_ref[...]
    cm, sm = cm_ref[...], sm_ref[...]
    hd, hh = GQA_HEAD_DIM, GQA_HEAD_DIM // 2

    cq = t[o_cq:o_cq + MLA_Q_RANK]
    cqn = (cq * _rms_rows(cq, MLA_Q_RANK) * gqa_ref[...]).astype(BF16)
    qm = _dot(wqb_ref[...], cqn)
    gmq = gmq_ref[...]
    nd, rh = MLA_NOPE_DIM, MLA_ROPE_DIM // 2
    qpad = jnp.zeros((MLA_HEAD_PAD - MLA_QK_DIM, tm), BF16)
    for n in range(MLA_HEADS):
        xh = qm[n * MLA_QK_DIM:(n + 1) * MLA_QK_DIM]
        xh = xh * _rms_rows(xh, MLA_QK_DIM) * gmq * (MLA_SCALE * LOG2E)
        a, b = _rope_rows(xh[nd:nd + rh], xh[nd + rh:], cm, sm)
        base = n * MLA_HEAD_PAD
        qm_ref[base:base + nd, :] = xh[:nd].astype(BF16)
        qm_ref[base + nd:base + nd + rh, :] = a.astype(BF16)
        qm_ref[base + nd + rh:base + MLA_QK_DIM, :] = b.astype(BF16)
        qm_ref[base + MLA_QK_DIM:base + MLA_HEAD_PAD, :] = qpad

    gate_cols(1)
    ckv = t[o_ckv:o_ckv + MLA_KV_RANK]
    ckvn = (ckv * _rms_rows(ckv, MLA_KV_RANK) * gkva_ref[...]).astype(BF16)
    kv = _dot(wkvb_ref[...], ckvn)
    kr = t[o_kr:o_kr + MLA_ROPE_DIM]
    kr_ss = jnp.sum(kr * kr, axis=0, keepdims=True)
    gmk = gmk_ref[...]
    kzero = jnp.zeros((MLA_HEAD_PAD - MLA_QK_DIM, tm), F32)
    ones_rows = _ones_row_block(tm)
    parts = []
    for n in range(MLA_HEADS):
        kn = kv[n * 128:n * 128 + nd]
        vm_ref[n * V_ROWS:n * V_ROWS + MLA_V_DIM, :] = kv[n * 128 + nd:(n + 1) * 128].astype(BF16)
        vm_ref[n * V_ROWS + MLA_V_DIM:(n + 1) * V_ROWS, :] = ones_rows
        rs = lax.rsqrt((jnp.sum(kn * kn, axis=0, keepdims=True) + kr_ss) * (1.0 / MLA_QK_DIM) + EPS)
        krn = kr * rs * gmk[nd:]
        a, b = _rope_rows(krn[:rh], krn[rh:], cm, sm)
        parts += [kn * rs * gmk[:nd], a, b, kzero]
    km_ref[...] = jnp.concatenate(parts, axis=0).T.astype(BF16)

    gq = gq_ref[...]
    for n in range(GQA_HEADS):
        xh = t[o_q + n * hd:o_q + (n + 1) * hd]
        xh = xh * _rms_rows(xh, hd) * gq
        a, b = _rope_rows(xh[:hh], xh[hh:], cg, sg)
        qg_ref[n * hd:n * hd + hh, :] = (a * (GQA_SCALE * LOG2E)).astype(BF16)
        qg_ref[n * hd + hh:(n + 1) * hd, :] = (b * (GQA_SCALE * LOG2E)).astype(BF16)

    gate_cols(2)
    gk = gk_ref[...]
    zpad = jnp.zeros((V7X_LANES - hd, tm), F32)
    parts = []
    for n in range(GQA_KV_HEADS):
        xh = t[o_k + n * hd:o_k + (n + 1) * hd]
        xh = xh * _rms_rows(xh, hd) * gk
        a, b = _rope_rows(xh[:hh], xh[hh:], cg, sg)
        parts += [a, b, zpad]
    kg_ref[...] = jnp.concatenate(parts, axis=0).T.astype(BF16)
    for n in range(GQA_KV_HEADS):
        vg_ref[n * V_ROWS:n * V_ROWS + hd, :] = t[o_v + n * hd:o_v + (n + 1) * hd].astype(BF16)
        vg_ref[n * V_ROWS + hd:(n + 1) * V_ROWS, :] = ones_rows


def _proj(xs, modsel, g, wts, tabs, lat_tiles):
    bsz, s, d = xs.shape
    tm = ROW_TILE
    nt = s // tm
    ncs = KV_CHUNK // tm
    nb = PROJ_BATCH if bsz % PROJ_BATCH == 0 else 1
    (wu, wg, wt, wqb, wkvb, gq, gk, gqa, gkva, gmq, gmk) = wts
    cg, sg, cm, sm = tabs
    row = lambda b, i: (b, i, 0)
    col = lambda b, i: (b, 0, i)
    tab = lambda b, i: (0, i)
    in_specs = [
        pl.BlockSpec((nb, tm, d), row),
        pl.BlockSpec((nb, None, 1, 6 * d), lambda b, i: (b, jnp.where(i >= lat_tiles, 1, 0), 0, 0)),
        _const_spec(g.shape), _const_spec(wu.shape), _const_spec(wg.shape), _const_spec(wt.shape),
        _const_spec(wqb.shape), _const_spec(wkvb.shape),
        _const_spec(gq.shape), _const_spec(gk.shape), _const_spec(gqa.shape), _const_spec(gkva.shape),
        _const_spec(gmq.shape), _const_spec(gmk.shape),
        pl.BlockSpec((cg.shape[0], tm), tab), pl.BlockSpec((sg.shape[0], tm), tab),
        pl.BlockSpec((cm.shape[0], tm), tab), pl.BlockSpec((sm.shape[0], tm), tab),
    ]
    out_shape = [
        jax.ShapeDtypeStruct((bsz, s, 3 * HY_WIDTH), BF16),
        jax.ShapeDtypeStruct((bsz, s, 3 * d), BF16),
        jax.ShapeDtypeStruct((bsz, GQA_HEADS * GQA_HEAD_DIM, s), BF16),
        jax.ShapeDtypeStruct((bsz, s, GQA_KV_HEADS * V7X_LANES), BF16),
        jax.ShapeDtypeStruct((bsz, GQA_KV_HEADS * V_ROWS, s), BF16),
        jax.ShapeDtypeStruct((bsz, MLA_HEADS * MLA_HEAD_PAD, s), BF16),
        jax.ShapeDtypeStruct((bsz, s, MLA_HEADS * MLA_HEAD_PAD), BF16),
        jax.ShapeDtypeStruct((bsz, s // KV_CHUNK, MLA_HEADS * V_ROWS, KV_CHUNK), BF16),
    ]
    out_specs = [
        pl.BlockSpec((nb, tm, 3 * HY_WIDTH), row),
        pl.BlockSpec((nb, tm, 3 * d), row),
        pl.BlockSpec((nb, GQA_HEADS * GQA_HEAD_DIM, tm), col),
        pl.BlockSpec((nb, tm, GQA_KV_HEADS * V7X_LANES), row),
        pl.BlockSpec((nb, GQA_KV_HEADS * V_ROWS, tm), col),
        pl.BlockSpec((nb, MLA_HEADS * MLA_HEAD_PAD, tm), col),
        pl.BlockSpec((nb, tm, MLA_HEADS * MLA_HEAD_PAD), row),
        pl.BlockSpec((nb, None, MLA_HEADS * V_ROWS, tm), lambda b, i: (b, i // ncs, 0, i % ncs)),
    ]
    return pl.pallas_call(
        _proj_kernel,
        grid=(bsz // nb, nt),
        in_specs=in_specs,
        out_specs=out_specs,
        out_shape=out_shape,
        compiler_params=_cparams(("parallel", "arbitrary"), 56),
        name="in_proj",
    )(xs, modsel, g, wu, wg, wt, wqb, wkvb, gq, gk, gqa, gkva, gmq, gmk, cg, sg, cm, sm)


def _short_conv_kernel(u_ref, up_ref, un_ref, w_ref, x0_ref, z_ref, *, n_tiles):
    i = pl.program_id(1)
    u = u_ref[...].astype(F32)
    tm = u.shape[0]
    prev = jnp.where(i > 0, up_ref[7:8, :].astype(F32), 0.0)
    nxt = jnp.where(i < n_tiles - 1, un_ref[0:1, :].astype(F32), 0.0)
    ridx = lax.broadcasted_iota(jnp.int32, u.shape, 0)
    up = jnp.where(ridx == 0, prev, pltpu.roll(u, 1, axis=0))
    dn = jnp.where(ridx == tm - 1, nxt, pltpu.roll(u, tm - 1, axis=0))
    uc = up * w_ref[0:1, :] + u * w_ref[1:2, :] + dn * w_ref[2:3, :]
    c = HY_WIDTH
    x0_ref[...] = uc[:, :c].astype(BF16)
    z_ref[...] = (uc[:, c:2 * c] * uc[:, 2 * c:]).astype(BF16)


def _short_conv(u, short_w, row0, rows):
    bsz, s, c3 = u.shape
    tm = math.gcd(math.gcd(rows, row0), CONV_TILE) if row0 else math.gcd(rows, CONV_TILE)
    nt = rows // tm
    t0 = row0 // tm
    r8 = tm // 8
    last8 = s // 8 - 1
    return pl.pallas_call(
        functools.partial(_short_conv_kernel, n_tiles=nt),
        grid=(bsz, nt),
        in_specs=[
            pl.BlockSpec((None, tm, c3), lambda b, i: (b, t0 + i, 0)),
            pl.BlockSpec((None, 8, c3), lambda b, i: (b, jnp.maximum((t0 + i) * r8 - 1, 0), 0)),
            pl.BlockSpec((None, 8, c3), lambda b, i: (b, jnp.minimum((t0 + i + 1) * r8, last8), 0)),
            _const_spec(short_w.shape),
        ],
        out_specs=[pl.BlockSpec((None, tm, HY_WIDTH), lambda b, i: (b, i, 0))] * 2,
        out_shape=[jax.ShapeDtypeStruct((bsz, rows, HY_WIDTH), BF16)] * 2,
        compiler_params=_cparams(("parallel", "arbitrary"), 32),
        name="hyena_short_conv",
    )(u, u, u, short_w)


def _filter_kernel(zf_ref, w1_ref, b1_ref, w2_ref, b2_ref, fr_ref, w3_ref, dl_ref, h_ref, ss_ref):
    i = pl.program_id(0)
    zf = zf_ref[...]
    tl = zf.shape[0]
    h = jnp.sin(fr_ref[0:1, :] * (_dot_hi(zf, w1_ref[...]) + b1_ref[...]))
    h = jnp.sin(fr_ref[1:2, :] * (_dot_hi(h, w2_ref[...]) + b2_ref[...]))
    h = _dot_hi(h, w3_ref[...])
    decay = jnp.exp(-zf[:, 0:1] * dl_ref[...])
    c = HY_WIDTH
    hf = h[:, :c] * decay
    ridx = lax.broadcasted_iota(jnp.int32, (tl, c), 0) + i * tl
    hb = jnp.where(ridx == 0, 0.0, h[:, c:] * decay)
    h_ref[0] = hf.astype(BF16)
    h_ref[1] = hb.astype(BF16)
    ss = jnp.sum(hf * hf + hb * hb, axis=0, keepdims=True)

    @pl.when(i == 0)
    def _():
        ss_ref[...] = ss

    @pl.when(i > 0)
    def _():
        ss_ref[...] += ss


def _filter(zfeat, fw, deltas):
    length = zfeat.shape[0]
    tl = min(length, 1024)
    w1, b1, w2, b2, fr, w3 = fw
    return pl.pallas_call(
        _filter_kernel,
        grid=(length // tl,),
        in_specs=[pl.BlockSpec((tl, zfeat.shape[1]), lambda i: (i, 0))]
        + [_const_spec(a.shape) for a in (w1, b1, w2, b2, fr, w3, deltas)],
        out_specs=[pl.BlockSpec((2, tl, HY_WIDTH), lambda i: (0, i, 0)),
                   pl.BlockSpec((1, HY_WIDTH), lambda i: (0, 0))],
        out_shape=[jax.ShapeDtypeStruct((2, length, HY_WIDTH), BF16),
                   jax.ShapeDtypeStruct((1, HY_WIDTH), F32)],
        compiler_params=_cparams(("arbitrary",), 40),
        name="hyena_filter",
    )(zfeat, w1, b1, w2, b2, fr, w3, deltas)


def _fft_a_kernel(g_ref, x_ref, o_ref, *, nb, cw):
    for j in range(nb):
        sl = slice(j * cw, (j + 1) * cw)
        x = jnp.concatenate([x_ref[p, :, sl] for p in range(x_ref.shape[0])], axis=0)
        o_ref[:, sl] = _dot(g_ref[j], x).astype(o_ref.dtype)


def _fft_a(x3d, gmat):
    bx, parts, k1, w = x3d.shape
    n2, two_n1, _ = gmat.shape
    cw = w // n2
    nb = FFT_NB
    return pl.pallas_call(
        functools.partial(_fft_a_kernel, nb=nb, cw=cw),
        grid=(bx, n2 // nb),
        in_specs=[pl.BlockSpec((nb, two_n1, parts * k1), lambda b, j: (j, 0, 0)),
                  pl.BlockSpec((None, parts, k1, nb * cw), lambda b, j: (b, 0, 0, j))],
        out_specs=pl.BlockSpec((None, two_n1, nb * cw), lambda b, j: (b, 0, j)),
        out_shape=jax.ShapeDtypeStruct((bx, two_n1, w), BF16),
        compiler_params=_cparams(("parallel", "arbitrary"), 40),
        name="fft_stage_a",
    )(gmat, x3d)


def _fft_filter_b_kernel(fb_ref, a_ref, ss_ref, kf_ref, *, inv_n):
    n = a_ref.shape[3]
    rs = lax.rsqrt(ss_ref[...] + EPS) * inv_n
    for r in range(a_ref.shape[2]):
        xf = _dot(fb_ref[...], jnp.concatenate([a_ref[0, 0, r], a_ref[0, 1, r]], axis=0))
        xb = _dot(fb_ref[...], jnp.concatenate([a_ref[1, 0, r], a_ref[1, 1, r]], axis=0))
        kf_ref[0, r] = (xf[:n] + xb[:n]) * rs
        kf_ref[1, r] = (xf[n:] - xb[n:]) * rs


def _fft_filter_b(a5, fb, ssq, inv_n):
    _, _, n1, n2, c = a5.shape
    kb = min(FFT_KB, n1)
    return pl.pallas_call(
        functools.partial(_fft_filter_b_kernel, inv_n=inv_n),
        grid=(n1 // kb,),
        in_specs=[_const_spec(fb.shape),
                  pl.BlockSpec((2, 2, kb, n2, c), lambda k: (0, 0, k, 0, 0)),
                  _const_spec(ssq.shape)],
        out_specs=pl.BlockSpec((2, kb, n2, c), lambda k: (0, k, 0, 0)),
        out_shape=jax.ShapeDtypeStruct((2, n1, n2, c), F32),
        compiler_params=_cparams(("arbitrary",), 32),
        name="fft_filter_stage_b",
    )(fb, a5, ssq)


def _fft_b_kernel(fb_ref, fbi_ref, a_ref, kf_ref, o_ref):
    kb, n, cw = a_ref.shape[1:]
    a_all = jnp.concatenate([jnp.concatenate([a_ref[0, r], a_ref[1, r]], axis=0) for r in range(kb)], axis=1)
    x = _dot(fb_ref[...], a_all)
    ys = []
    for r in range(kb):
        xr, xi = x[:n, r * cw:(r + 1) * cw], x[n:, r * cw:(r + 1) * cw]
        kr, ki = kf_ref[0, r], kf_ref[1, r]
        ys.append(jnp.concatenate([xr * kr - xi * ki, xr * ki + xi * kr], axis=0).astype(BF16))
    c = _dot(fbi_ref[...], jnp.concatenate(ys, axis=1))
    for r in range(kb):
        o_ref[0, r] = c[:n, r * cw:(r + 1) * cw].astype(o_ref.dtype)
        o_ref[1, r] = c[n:, r * cw:(r + 1) * cw].astype(o_ref.dtype)


def _fft_b(a5, kf, fb, fbi):
    bsz, _, n1, n2, c = a5.shape
    kb = min(FFT_KB, n1)
    return pl.pallas_call(
        _fft_b_kernel,
        grid=(n1 // kb, bsz),
        in_specs=[_const_spec(fb.shape), _const_spec(fbi.shape),
                  pl.BlockSpec((None, 2, kb, n2, c), lambda k, b: (b, 0, k, 0, 0)),
                  pl.BlockSpec((2, kb, n2, c), lambda k, b: (0, k, 0, 0))],
        out_specs=pl.BlockSpec((None, 2, kb, n2, c), lambda k, b: (b, 0, k, 0, 0)),
        out_shape=jax.ShapeDtypeStruct(a5.shape, BF16),
        compiler_params=_cparams(("arbitrary", "arbitrary"), 32),
        name="fft_stage_b",
    )(fb, fbi, a5, kf)


def _fft_c_kernel(h_ref, c_ref, o_ref, *, nb, cw):
    parts, k1 = o_ref.shape[0], o_ref.shape[1]
    for j in range(nb):
        sl = slice(j * cw, (j + 1) * cw)
        y = _dot(h_ref[j], c_ref[:, sl])
        for p in range(parts):
            o_ref[p, :, sl] = y[p * k1:(p + 1) * k1].astype(o_ref.dtype)


def _fft_c(c2d, hmat, parts):
    bx, two_n1, w = c2d.shape
    n2, rows, _ = hmat.shape
    k1 = rows // parts
    cw = w // n2
    nb = FFT_NB
    return pl.pallas_call(
        functools.partial(_fft_c_kernel, nb=nb, cw=cw),
        grid=(bx, n2 // nb),
        in_specs=[pl.BlockSpec((nb, rows, two_n1), lambda b, j: (j, 0, 0)),
                  pl.BlockSpec((None, two_n1, nb * cw), lambda b, j: (b, 0, j))],
        out_specs=pl.BlockSpec((None, parts, k1, nb * cw), lambda b, j: (b, 0, 0, j)),
        out_shape=jax.ShapeDtypeStruct((bx, parts, k1, w), BF16),
        compiler_params=_cparams(("parallel", "arbitrary"), 40),
        name="fft_stage_c",
    )(hmat, c2d)


def _ctx_conv_kernel(fc_ref, fci_ref, h_ref, ss_ref, z_ref, o_ref):
    n = fc_ref.shape[0] // 2
    fc = fc_ref[...]
    kf = _dot(fc, h_ref[0])
    kb = _dot(fc, h_ref[1])
    rs = lax.rsqrt(ss_ref[...] + EPS)
    kr = (kf[:n] + kb[:n]) * rs
    ki = (kf[n:] - kb[n:]) * rs
    x = _dot(fc, z_ref[...])
    xr, xi = x[:n], x[n:]
    y = jnp.concatenate([xr * kr - xi * ki, xr * ki + xi * kr], axis=0).astype(BF16)
    o_ref[...] = _dot(fci_ref[...], y).astype(o_ref.dtype)


def _ctx_conv(fc, fci, hfb, ssq, z):
    bsz, lc, c = z.shape
    blk = pl.BlockSpec((None, lc, c), lambda b: (b, 0, 0))
    return pl.pallas_call(
        _ctx_conv_kernel,
        grid=(bsz,),
        in_specs=[_const_spec(fc.shape), _const_spec(fci.shape), _const_spec(hfb.shape),
                  _const_spec(ssq.shape), blk],
        out_specs=blk,
        out_shape=jax.ShapeDtypeStruct((bsz, lc, c), BF16),
        compiler_params=_cparams(("arbitrary",), 32),
        name="hyena_ctx_conv",
    )(fc, fci, hfb, ssq, z)


@functools.lru_cache(maxsize=None)
def _fft_tables(length):
    n = 2 * length
    n2 = FFT_N2
    n1 = n // n2
    k1 = length // n2
    kk = np.arange(n1)[:, None]
    g = np.empty((n2, 2 * n1, k1), np.float64)
    g2 = np.empty((n2, 2 * n1, 2 * k1), np.float64)
    h2 = np.empty((n2, 2 * k1, 2 * n1), np.float64)
    nn = np.arange(k1)[None, :]
    for j in range(n2):
        ang = 2.0 * np.pi * (((n2 * nn * kk) % n) + (j * kk) % n) / n
        c_, s_ = np.cos(ang), np.sin(ang)
        g[j] = np.concatenate([c_, -s_], axis=0)
        g2[j] = np.block([[c_, s_], [-s_, c_]])
        h2[j] = np.block([[c_.T, -s_.T], [s_.T, c_.T]])
    a = np.arange(n2)
    ph = 2.0 * np.pi * ((a[:, None] * a[None, :]) % n2) / n2
    c, s = np.cos(ph), np.sin(ph)
    fb = np.block([[c, s], [-s, c]])
    fbi = np.block([[c, -s], [s, c]])
    return (jnp.asarray(g, BF16), jnp.asarray(g2, BF16), jnp.asarray(h2, BF16), jnp.asarray(fb, BF16),
            jnp.asarray(fbi, BF16), n1, k1)


@functools.lru_cache(maxsize=None)
def _dft_tables(length):
    n = 2 * length
    k = np.arange(n)[:, None]
    t = np.arange(length)[None, :]
    ang = 2.0 * np.pi * ((k * t) % n) / n
    fc = np.concatenate([np.cos(ang), -np.sin(ang)], axis=0)
    fci = np.concatenate([np.cos(ang).T, -np.sin(ang).T], axis=1) / n
    return jnp.asarray(fc, BF16), jnp.asarray(fci, BF16)


@functools.lru_cache(maxsize=None)
def _filter_features(length):
    t = np.linspace(0.0, 1.0, length, dtype=np.float32)[:, None]
    w = (2.0 * math.pi * np.arange(length, dtype=np.float32)[:, None] / length).astype(np.float32)
    f = np.linspace(1e-4, HY_BANDS - 1, HY_BANDS, dtype=np.float32)[None, :]
    z = np.concatenate([t, np.cos(f * w), -np.sin(f * w)], axis=-1).astype(np.float32)
    zp = np.zeros((length, V7X_LANES), np.float32)
    zp[:, :HY_EMB_DIM] = z
    return jnp.asarray(zp)


def _hyena_deltas():
    max_decay = math.log(HY_DECAY_TARGET) / HY_FAST_DECAY
    min_decay = math.log(HY_DECAY_TARGET) / HY_SLOW_DECAY
    return jnp.abs(jnp.linspace(min_decay, max_decay, HY_WIDTH, dtype=F32))[None, :]


def _hyena(u, short_w, fw, lat, lc):
    bsz = u.shape[0]
    c = HY_WIDTH
    deltas = _hyena_deltas()
    x0, z = _short_conv(u, short_w, 0, lat)
    gmat, gmat2, hmat2, fb, fbi, n1, k1 = _fft_tables(lat)
    n2 = FFT_N2
    hfb, ssq = _filter(_filter_features(lat), fw, deltas)
    fa = _fft_a(hfb.reshape(2, 1, k1, n2 * c), gmat)
    kf = _fft_filter_b(fa.reshape(2, 2, n1, n2, c), fb, ssq, 1.0 / (2 * lat))
    npair = (bsz + 1) // 2
    zp = z if bsz % 2 == 0 else jnp.concatenate([z, jnp.zeros_like(z[:1])], axis=0)
    za = _fft_a(zp.reshape(npair, 2, k1, n2 * c), gmat2)
    zc = _fft_b(za.reshape(npair, 2, n1, n2, c), kf, fb, fbi)
    conv = _fft_c(zc.reshape(npair, 2 * n1, n2 * c), hmat2, 2).reshape(2 * npair, lat, c)[:bsz]
    x0_c, z_c = _short_conv(u, short_w, lat, lc)
    hfb_c, ssq_c = _filter(_filter_features(lc), fw, deltas)
    fc, fci = _dft_tables(lc)
    conv_c = _ctx_conv(fc, fci, hfb_c, ssq_c, z_c)
    return (conv, x0, z), (conv_c, x0_c, z_c)


def _gqa_kernel(q_ref, k_ref, v_ref, sink_ref, o_ref, *bufs, lat, lc, tq):
    sub = q_ref.shape[1] // tq
    for t in range(sub):
        lanes = slice(t * tq, (t + 1) * tq)
        _gqa_tile(pl.program_id(1) * sub + t, q_ref.at[:, lanes], k_ref, v_ref, sink_ref, o_ref.at[:, lanes], bufs,
                  lat, lc)


def _gqa_tile(i, q_ref, k_ref, v_ref, sink_ref, o_ref, bufs, lat, lc):
    tq = q_ref.shape[1]
    s_len = k_ref.shape[0]
    hd = GQA_HEAD_DIM
    w = WINDOW
    nwin = tq + 2 * w
    start = pl.multiple_of(jnp.clip(i * tq - w, 0, s_len - nwin), w)
    k_all = jnp.concatenate([k_ref[pl.ds(start, nwin), :], k_ref[lat:lat + lc, :]], axis=0)
    v_all = jnp.concatenate([v_ref[:, pl.ds(start, nwin)], v_ref[:, lat:lat + lc]], axis=1)
    q_pos = i * tq + lax.broadcasted_iota(jnp.int32, (1, tq), 1)
    k_pos = start + lax.broadcasted_iota(jnp.int32, (nwin, 1), 0)
    bias = (jnp.where(jnp.abs(k_pos - q_pos) <= w, 0.0, NEG_BIG)
            + jnp.where(k_pos < lat, 0.0, NEG_BIG)
            + jnp.where(q_pos < lat, 0.0, NEG_BIG))
    bias = jnp.concatenate([bias, jnp.zeros((lc, tq), F32)], axis=0)
    bias2 = jnp.concatenate([bias, bias], axis=1)
    zq = jnp.zeros((V7X_LANES - hd, 2 * tq), BF16)

    def score(pair, dst):
        g = pair // (GQA_GROUP // 2)
        r = 2 * pair * hd
        q2 = jnp.concatenate([q_ref[r:r + hd, :], q_ref[r + hd:r + 2 * hd, :]], axis=1)
        s = _dot(k_all[:, g * V7X_LANES:(g + 1) * V7X_LANES], jnp.concatenate([q2, zq], axis=0)) + bias2
        dst[...] = s
        return jnp.max(s, axis=0, keepdims=True)

    def update(pair, s_sc, s_max):
        g = pair // (GQA_GROUP // 2)
        r = 2 * pair * hd
        sink = jnp.concatenate([sink_ref[2 * pair:2 * pair + 1, :], sink_ref[2 * pair + 1:2 * pair + 2, :]],
                               axis=1) * LOG2E
        m = jnp.maximum(s_max, sink)
        p = jnp.exp2(s_sc[...] - m).astype(BF16)
        pv = _dot(v_all[g * V_ROWS:(g + 1) * V_ROWS, :], p)
        o = pv[:hd] / (pv[hd:hd + 1] + jnp.exp2(sink - m))
        o_ref[r:r + hd, :] = o[:, :tq].astype(o_ref.dtype)
        o_ref[r + hd:r + 2 * hd, :] = o[:, tq:].astype(o_ref.dtype)

    npair = GQA_HEADS // 2
    mx = [score(j, bufs[j]) for j in range(npair)]
    for j in range(npair):
        update(j, bufs[j], mx[j])


def _gqa(qg, kg, vg, sink_rows, lat, lc):
    bsz, nq, s = qg.shape
    tq = GQA_TQ
    sub = GQA_SUBTILES if (s // tq) % GQA_SUBTILES == 0 else 1
    wq = tq * sub
    return pl.pallas_call(
        functools.partial(_gqa_kernel, lat=lat, lc=lc, tq=tq),
        grid=(bsz, s // wq),
        in_specs=[
            pl.BlockSpec((None, nq, wq), lambda b, i: (b, 0, i)),
            pl.BlockSpec((None, s, kg.shape[2]), lambda b, i: (b, 0, 0)),
            pl.BlockSpec((None, vg.shape[1], s), lambda b, i: (b, 0, 0)),
            _const_spec(sink_rows.shape),
        ],
        out_specs=pl.BlockSpec((None, nq, wq), lambda b, i: (b, 0, i)),
        out_shape=jax.ShapeDtypeStruct((bsz, nq, s), BF16),
        scratch_shapes=[pltpu.VMEM((tq + 2 * WINDOW + lc, 2 * tq), F32)] * (GQA_HEADS // 2),
        compiler_params=_cparams(("parallel", "arbitrary"), 40),
        name="gqa_window_attn",
    )(qg, kg, vg, sink_rows)


def _mla_update(s_ref, s_max, vt, m, acc):
    m_new = jnp.maximum(m, s_max)
    alpha = jnp.exp2(m - m_new)
    p = jnp.exp2(s_ref[...] - m_new).astype(BF16)
    acc = alpha * acc + _dot(vt, p)
    return m_new, acc


def _mla_kernel(q_ref, k_ref, v_ref, o_ref, *bufs, nc, ctx_only, tq):
    for t in range(q_ref.shape[1] // tq):
        lanes = slice(t * tq, (t + 1) * tq)
        _mla_tile(q_ref[:, lanes], k_ref, v_ref, o_ref.at[:, lanes], bufs, nc, ctx_only)


def _mla_tile(q, k_ref, v_ref, o_ref, bufs, nc, ctx_only):
    tq = q.shape[1]
    ck = KV_CHUNK
    nbuf = len(bufs)
    m = jnp.full((1, tq), NEG_BIG, F32)
    acc = jnp.zeros((V_ROWS, tq), F32)

    def score(k, dst):
        s = _dot(k, q)
        dst[...] = s
        return jnp.max(s, axis=0, keepdims=True)

    def score_chunk(j, dst):
        return score(k_ref[pl.ds(pl.multiple_of(j * ck, ck), ck), :], dst)

    if ctx_only:
        s_x = bufs[0].at[0:KV_SUB, :]
        mx = score(k_ref[k_ref.shape[0] - KV_SUB:, :], s_x)
        m, acc = _mla_update(s_x, mx, v_ref[v_ref.shape[0] - 1][:, ck - KV_SUB:], m, acc)
    else:
        ahead = nbuf - 1
        mx = [None] * nbuf
        for j in range(min(ahead, nc)):
            mx[j] = score_chunk(j, bufs[j])

        def step(j, slot, m, acc, mx, last):
            if not last:
                nxt = (slot + ahead) % nbuf
                mx[nxt] = score_chunk(j + ahead, bufs[nxt])
            return _mla_update(bufs[slot], mx[slot], v_ref[j], m, acc)

        per_body = MLA_UNROLL * nbuf

        def body(g, c):
            m, acc, mx = c[0], c[1], list(c[2:])
            for u in range(per_body):
                m, acc = step(g * per_body + u, u % nbuf, m, acc, mx, False)
            return (m, acc, *mx)

        n_body = max(nc - ahead, 0) // per_body
        if n_body > 0:
            out = lax.fori_loop(0, n_body, body, (m, acc, *[m if x is None else x for x in mx]))
            m, acc, mx = out[0], out[1], list(out[2:])
        for j in range(n_body * per_body, nc):
            m, acc = step(j, j % nbuf, m, acc, mx, j + ahead >= nc)
    o_ref[...] = (acc[:MLA_V_DIM] / acc[MLA_V_DIM:MLA_V_DIM + 1]).astype(o_ref.dtype)


def _mla_call(qm, km, vm, tq, sub, q_tile0, n_q, ctx_only):
    bsz, _, s = qm.shape
    nc = vm.shape[1]
    wq = tq * sub
    if ctx_only:
        k_spec = pl.BlockSpec((None, KV_SUB, MLA_HEAD_PAD), lambda b, h, i: (b, s // KV_SUB - 1, h))
        v_spec = pl.BlockSpec((None, 1, V_ROWS, KV_CHUNK), lambda b, h, i: (b, nc - 1, h, 0))
    else:
        k_spec = pl.BlockSpec((None, s, MLA_HEAD_PAD), lambda b, h, i: (b, 0, h))
        v_spec = pl.BlockSpec((None, nc, V_ROWS, KV_CHUNK), lambda b, h, i: (b, 0, h, 0))
    return pl.pallas_call(
        functools.partial(_mla_kernel, nc=nc, ctx_only=ctx_only, tq=tq),
        grid=(bsz, MLA_HEADS, n_q),
        in_specs=[pl.BlockSpec((None, MLA_HEAD_PAD, wq), lambda b, h, i: (b, h, q_tile0 + i)), k_spec, v_spec],
        out_specs=pl.BlockSpec((None, MLA_V_DIM, wq), lambda b, h, i: (b, h, i)),
        out_shape=jax.ShapeDtypeStruct((bsz, MLA_HEADS * MLA_V_DIM, n_q * wq), BF16),
        scratch_shapes=[pltpu.VMEM((KV_CHUNK, tq), F32)] * MLA_NBUF,
        compiler_params=_cparams(("parallel", "arbitrary", "arbitrary"), 40),
        name="mla_attn_ctx" if ctx_only else "mla_attn",
    )(qm, km, vm)


def _mla(qm, km, vm, lat, lc):
    tq = min(MLA_TQ, lat)
    sub = MLA_SUBTILES if lat % (tq * MLA_SUBTILES) == 0 else 1
    assert lc == KV_SUB and lat % lc == 0 and lat % tq == 0
    return (_mla_call(qm, km, vm, tq, sub, 0, lat // (tq * sub), False),
            _mla_call(qm, km, vm, lc, 1, lat // lc, 1, True))


def _merge_mlp_kernel(x_ref, mod_ref, hl_refs, hc_refs, sk_ref, yg_ref, yml_ref, ymc_ref, gt_ref, wb_ref, wo_ref,
                      g_ref, w1_ref, w2_ref, o_ref, *, lat_tiles):
    for n in range(x_ref.shape[0]):
        _merge_mlp_one(x_ref.at[n], mod_ref.at[n], [r.at[n] for r in hl_refs], [r.at[n] for r in hc_refs], sk_ref,
                       yg_ref.at[n], yml_ref.at[n], ymc_ref.at[n], gt_ref.at[n], wb_ref, wo_ref, g_ref, w1_ref,
                       w2_ref, o_ref.at[n], lat_tiles)


def _merge_mlp_one(x_ref, mod_ref, hl_refs, hc_refs, sk_ref, yg_ref, yml_ref, ymc_ref, gt_ref, wb_ref, wo_ref,
                   g_ref, w1_ref, w2_ref, o_ref, lat_tiles):
    d = D_MODEL
    is_lat = pl.program_id(1) < lat_tiles
    conv, x0, z = [jnp.where(is_lat, a[...], b[...]).astype(F32) for a, b in zip(hl_refs, hc_refs)]
    yh = (x0 * (conv + z * sk_ref[...])).astype(BF16)
    yg = yg_ref[...].astype(F32).T.astype(BF16)
    ym = jnp.where(is_lat, yml_ref[...], ymc_ref[...]).astype(F32).T.astype(BF16)
    merged = (gt_ref[:, 0:d].astype(F32) * _dot(yh, wb_ref[0])
              + gt_ref[:, d:2 * d].astype(F32) * _dot(yg, wb_ref[1])
              + gt_ref[:, 2 * d:].astype(F32) * _dot(ym, wb_ref[2]))
    res = _dot(merged.astype(BF16), wo_ref[...])
    x_mix = x_ref[...] + mod_ref[:, 2 * d:3 * d] * res
    o_ref[...] = _mlp_rows(x_mix, mod_ref, g_ref, w1_ref, w2_ref)


def _merge_mlp(xs, modsel, hy_lat, hy_ctx, skip, yg, ym_lat, ym_ctx, gates, wb, wo, g, w1, w2, lat_tiles, n_tiles):
    bsz, _, d = xs.shape
    tm = ROW_TILE
    c = HY_WIDTH
    nb = MERGE_BATCH if bsz % MERGE_BATCH == 0 else 1
    row = lambda b, i: (b, i, 0)
    col = lambda b, i: (b, 0, i)
    lat_row = pl.BlockSpec((nb, tm, c), lambda b, i: (b, jnp.minimum(i, lat_tiles - 1), 0))
    ctx_row = pl.BlockSpec((nb, tm, c), lambda b, i: (b, jnp.maximum(i - lat_tiles, 0), 0))
    return pl.pallas_call(
        functools.partial(_merge_mlp_kernel, lat_tiles=lat_tiles),
        grid=(bsz // nb, n_tiles),
        in_specs=[
            pl.BlockSpec((nb, tm, d), row),
            pl.BlockSpec((nb, None, 1, 6 * d), lambda b, i: (b, jnp.where(i >= lat_tiles, 1, 0), 0, 0)),
            [lat_row] * 3, [ctx_row] * 3, _const_spec(skip.shape),
            pl.BlockSpec((nb, c, tm), col),
            pl.BlockSpec((nb, c, tm), lambda b, i: (b, 0, jnp.minimum(i, lat_tiles - 1))),
            pl.BlockSpec((nb, c, tm), lambda b, i: (b, 0, jnp.maximum(i - lat_tiles, 0))),
            pl.BlockSpec((nb, tm, 3 * d), row),
            _const_spec(wb.shape), _const_spec(wo.shape),
            _const_spec(g.shape), _const_spec(w1.shape), _const_spec(w2.shape),
        ],
        out_specs=pl.BlockSpec((nb, tm, d), row),
        out_shape=jax.ShapeDtypeStruct((bsz, n_tiles * tm, d), F32),
        compiler_params=_cparams(("parallel", "arbitrary"), 56),
        name="merge_mlp",
    )(xs, modsel, list(hy_lat), list(hy_ctx), skip, yg, ym_lat, ym_ctx, gates, wb, wo, g, w1, w2)


def _mlp_rows(x, mod_ref, g_ref, w1_ref, w2_ref):
    d = D_MODEL
    xn = x * lax.rsqrt(jnp.mean(x * x, axis=-1, keepdims=True) + EPS) * g_ref[...]
    h = (xn * (1.0 + mod_ref[:, 4 * d:5 * d]) + mod_ref[:, 3 * d:4 * d]).astype(BF16)
    acc = jnp.zeros(x.shape, F32)
    for j in range(D_FF // FF_CHUNK):
        sl = slice(j * FF_CHUNK, (j + 1) * FF_CHUNK)
        a = jnp.maximum(_dot(h, w1_ref[:, sl]), 0.0)
        acc = acc + _dot((a * a).astype(BF16), w2_ref[sl, :])
    return x + mod_ref[:, 5 * d:] * acc


def _rope_tables_t(rows, dim, lc):
    n_freq = dim // 4
    inv = ROPE_BASE ** (-jnp.arange(n_freq, dtype=F32) / n_freq)
    r = jnp.repeat(jnp.arange(rows, dtype=F32), GRID_W)
    col = jnp.tile(jnp.arange(GRID_W, dtype=F32), rows)
    ang = jnp.concatenate([r[:, None] * inv, col[:, None] * inv], axis=-1)
    cos_t = jnp.concatenate([jnp.cos(ang).T, jnp.ones((dim // 2, lc), F32)], axis=1)
    sin_t = jnp.concatenate([jnp.sin(ang).T, jnp.zeros((dim // 2, lc), F32)], axis=1)
    return cos_t, sin_t


def _lane_bcast(v):
    return jnp.broadcast_to(v.astype(F32)[:, None], (v.shape[0], ROW_TILE))


def _layer_weights(l, w_in, gqa_q_norm, gqa_k_norm, mla_q_a_norm, mla_kv_a_norm, w_q_b, w_kv_b, mla_q_norm, mla_k_norm):
    w = w_in[l]
    o = np.cumsum([0, 3 * HY_WIDTH, GQA_HEADS * GQA_HEAD_DIM, GQA_KV_HEADS * GQA_HEAD_DIM,
                   GQA_KV_HEADS * GQA_HEAD_DIM, MLA_Q_RANK, MLA_KV_RANK, MLA_ROPE_DIM, 3 * D_MODEL])
    wu = w[:, o[0]:o[1]].astype(BF16)
    wt = w[:, o[1]:o[7]].T.astype(BF16)
    wg = w[:, o[7]:o[8]].astype(BF16)
    wqb = w_q_b[l].T.astype(BF16)
    wkvb = w_kv_b[l].T.astype(BF16)
    return (wu, wg, wt, wqb, wkvb,
            _lane_bcast(gqa_q_norm[l]), _lane_bcast(gqa_k_norm[l]),
            _lane_bcast(mla_q_a_norm[l]), _lane_bcast(mla_kv_a_norm[l]),
            _lane_bcast(mla_q_norm[l]), _lane_bcast(mla_k_norm[l]))


def kernel(x, c, ctx, c_ctx, w_mod, b_mod, norm_mix_g, norm_mlp_g, w_in, hy_short_w, hy_f1_w, hy_f1_b, hy_f2_w, hy_f2_b, hy_sin_freq, hy_f3_w, hy_skip, gqa_q_norm, gqa_k_norm, gqa_sink, mla_q_a_norm, mla_kv_a_norm, w_q_b, w_kv_b, mla_q_norm, mla_k_norm, w_branch, w_out, w_mlp1, w_mlp2):
    bsz, lat, d = x.shape
    lc = ctx.shape[1]
    depth = w_mod.shape[0]
    s = lat + lc
    assert d == D_MODEL and lat % GRID_W == 0 and lat % ROW_TILE == 0 and lc == ROW_TILE and s % KV_CHUNK == 0
    lat_tiles = lat // ROW_TILE

    pad = (-(bsz + 1)) % 8
    cond = jnp.concatenate([c, c_ctx[None, :], jnp.zeros((pad, d), F32)], axis=0)
    mods = _mod_all(cond, w_mod, b_mod)

    tabs = _rope_tables_t(lat // GRID_W, GQA_HEAD_DIM, lc) + _rope_tables_t(lat // GRID_W, MLA_ROPE_DIM, lc)
    xs = jnp.concatenate([x, ctx], axis=1)

    for l in range(depth):
        ml = mods[l]
        modsel = jnp.stack([ml[:bsz], jnp.broadcast_to(ml[bsz][None], (bsz, 6 * d))], axis=1)[:, :, None, :]
        wts = _layer_weights(l, w_in, gqa_q_norm, gqa_k_norm, mla_q_a_norm, mla_kv_a_norm, w_q_b, w_kv_b,
                             mla_q_norm, mla_k_norm)
        u, gates, qg, kg, vg, qm, km, vm = _proj(xs, modsel, norm_mix_g[l][None, :], wts, tabs, lat_tiles)

        f1w = jnp.zeros((V7X_LANES, HY_FILTER_WIDTH), F32).at[:HY_EMB_DIM].set(hy_f1_w[l])
        fw = (f1w, hy_f1_b[l][None, :], hy_f2_w[l], hy_f2_b[l][None, :], hy_sin_freq[l], hy_f3_w[l])
        hy_lat, hy_ctx = _hyena(u, hy_short_w[l], fw, lat, lc)

        sink_rows = jnp.broadcast_to(gqa_sink[l].astype(F32)[:, None], (GQA_HEADS, GQA_TQ))
        yg = _gqa(qg, kg, vg, sink_rows, lat, lc)
        ym_lat, ym_ctx = _mla(qm, km, vm, lat, lc)

        n_tiles = lat_tiles if l == depth - 1 else s // ROW_TILE
        xs = _merge_mlp(xs, modsel, hy_lat, hy_ctx, hy_skip[l][None, :], yg, ym_lat, ym_ctx, gates,
                        w_branch[l].astype(BF16), w_out[l].astype(BF16), norm_mlp_g[l][None, :],
                        w_mlp1[l].astype(BF16), w_mlp2[l].astype(BF16), lat_tiles, n_tiles)
    return xs
```

```python
import functools
import math

import numpy as np
import jax
import jax.numpy as jnp
from jax import lax
from jax.experimental import pallas as pl
from jax.experimental.pallas import tpu as pltpu

D_MODEL = 1024
GRID_W = 64
HY_WIDTH = 512
HY_EMB_DIM = 33
HY_BANDS = (HY_EMB_DIM - 1) // 2
HY_FILTER_WIDTH = 64
HY_DECAY_TARGET = 1e-2
HY_FAST_DECAY = 0.3
HY_SLOW_DECAY = 1.5
GQA_HEADS = 8
GQA_KV_HEADS = 2
GQA_GROUP = GQA_HEADS // GQA_KV_HEADS
GQA_HEAD_DIM = 64
GQA_SCALE = GQA_HEAD_DIM ** -0.5
WINDOW = 128
MLA_HEADS = 8
MLA_Q_RANK = 384
MLA_KV_RANK = 256
MLA_NOPE_DIM = 64
MLA_ROPE_DIM = 32
MLA_V_DIM = 64
MLA_QK_DIM = MLA_NOPE_DIM + MLA_ROPE_DIM
MLA_SCALE = MLA_QK_DIM ** -0.5
D_FF = 4 * D_MODEL
ROPE_BASE = 10000.0
EPS = 1e-6
LOG2E = 1.4426950408889634
NEG_BIG = -1e30

V7X_LANES = 128
V7X_VMEM_BYTES = 64 * 1024 * 1024

ROW_TILE = 256
PROJ_BATCH = 2
MERGE_BATCH = 2
CONV_TILE = 512
MLA_HEAD_PAD = 128
V_ROWS = 80
KV_CHUNK = ROW_TILE
KV_SUB = 256
MLA_NBUF = 3
MLA_UNROLL = 5
MLA_TQ = 1024
MLA_SUBTILES = 4
GQA_TQ = 256
GQA_SUBTILES = 3
FF_CHUNK = 1024
FFT_N2 = 128
FFT_NB = 16
FFT_KB = 8

F32 = jnp.float32
BF16 = jnp.bfloat16


def _cparams(sem, vmem_mb):
    limit = vmem_mb * 1024 * 1024
    assert limit < V7X_VMEM_BYTES
    return pltpu.CompilerParams(dimension_semantics=sem, vmem_limit_bytes=limit)


def _dot(a, b):
    return jnp.dot(a, b, preferred_element_type=F32)


def _dot_hi(a, b):
    return jnp.dot(a, b, preferred_element_type=F32, precision=lax.Precision.HIGHEST)


def _ones_row_block(width):
    r = lax.broadcasted_iota(jnp.int32, (V_ROWS - MLA_V_DIM, width), 0)
    return jnp.where(r == 0, 1.0, 0.0).astype(BF16)


def _const_spec(shape):
    nd = len(shape)
    return pl.BlockSpec(shape, lambda *_: (0,) * nd, pipeline_mode=pl.Buffered(1))


def _mod_kernel(c_ref, w_ref, b_ref, o_ref):
    c = c_ref[...]
    s = c * jax.nn.sigmoid(c)
    o_ref[...] = _dot_hi(s, w_ref[...]) + b_ref[...]


def _mod_all(cond, w_mod, b_mod):
    depth, d, n = w_mod.shape
    rows = cond.shape[0]
    tn = 1536
    return pl.pallas_call(
        _mod_kernel,
        grid=(depth, n // tn),
        in_specs=[
            pl.BlockSpec((rows, d), lambda l, j: (0, 0)),
            pl.BlockSpec((None, d, tn), lambda l, j: (l, 0, j)),
            pl.BlockSpec((None, 1, tn), lambda l, j: (l, 0, j)),
        ],
        out_specs=pl.BlockSpec((None, rows, tn), lambda l, j: (l, 0, j)),
        out_shape=jax.ShapeDtypeStruct((depth, rows, n), F32),
        compiler_params=_cparams(("arbitrary", "arbitrary"), 40),
        name="adaln_mod",
    )(cond, w_mod, b_mod.reshape(depth, 1, n))


def _rms_rows(x, n):
    return lax.rsqrt(jnp.sum(x * x, axis=0, keepdims=True) * (1.0 / n) + EPS)


def _rope_rows(x1, x2, cs, sn):
    return x1 * cs - x2 * sn, x1 * sn + x2 * cs


def _proj_kernel(x_ref, mod_ref, *refs):
    consts, outs = refs[:16], refs[16:]
    for n in range(x_ref.shape[0]):
        _proj_one(x_ref.at[n], mod_ref.at[n], *consts, *[o.at[n] for o in outs])


def _proj_one(x_ref, mod_ref, g_ref, wu_ref, wg_ref, wt_ref, wqb_ref, wkvb_ref,
              gq_ref, gk_ref, gqa_ref, gkva_ref, gmq_ref, gmk_ref,
              cg_ref, sg_ref, cm_ref, sm_ref,
              u_ref, gate_ref, qg_ref, kg_ref, vg_ref, qm_ref, km_ref, vm_ref):
    d = D_MODEL
    x = x_ref[...]
    tm = x.shape[0]
    shift = mod_ref[:, 0:d]
    scale = mod_ref[:, d:2 * d]
    xn = x * lax.rsqrt(jnp.mean(x * x, axis=-1, keepdims=True) + EPS) * g_ref[...]
    h = xn * (1.0 + scale) + shift
    hb = h.astype(BF16)

    t = lax.dot_general(wt_ref[...], hb, (((1,), (1,)), ((), ())), preferred_element_type=F32)
    o_q, o_k, o_v = 0, 512, 640
    o_cq, o_ckv, o_kr = 768, 768 + MLA_Q_RANK, 768 + MLA_Q_RANK + MLA_KV_RANK

    u_ref[...] = _dot(hb, wu_ref[...]).astype(BF16)

    def gate_cols(n):
        cols = slice(n * d, (n + 1) * d)
        gate_ref[:, cols] = jax.nn.sigmoid(_dot(hb, wg_ref[:, cols])).astype(BF16)

    gate_cols(0)
    cg, sg = cg_ref[...], sg_ref[...]
    cm, sm = cm_ref[...], sm_ref[...]
    hd, hh = GQA_HEAD_DIM, GQA_HEAD_DIM // 2

    cq = t[o_cq:o_cq + MLA_Q_RANK]
    cqn = (cq * _rms_rows(cq, MLA_Q_RANK) * gqa_ref[...]).astype(BF16)
    qm = _dot(wqb_ref[...], cqn)
    gmq = gmq_ref[...]
    nd, rh = MLA_NOPE_DIM, MLA_ROPE_DIM // 2
    qpad = jnp.zeros((MLA_HEAD_PAD - MLA_QK_DIM, tm), BF16)
    for n in range(MLA_HEADS):
        xh = qm[n * MLA_QK_DIM:(n + 1) * MLA_QK_DIM]
        xh = xh * _rms_rows(xh, MLA_QK_DIM) * gmq * (MLA_SCALE * LOG2E)
        a, b = _rope_rows(xh[nd:nd + rh], xh[nd + rh:], cm, sm)
        base = n * MLA_HEAD_PAD
        qm_ref[base:base + nd, :] = xh[:nd].astype(BF16)
        qm_ref[base + nd:base + nd + rh, :] = a.astype(BF16)
        qm_ref[base + nd + rh:base + MLA_QK_DIM, :] = b.astype(BF16)
        qm_ref[base + MLA_QK_DIM:base + MLA_HEAD_PAD, :] = qpad

    gate_cols(1)
    ckv = t[o_ckv:o_ckv + MLA_KV_RANK]
    ckvn = (ckv * _rms_rows(ckv, MLA_KV_RANK) * gkva_ref[...]).astype(BF16)
    kv = _dot(wkvb_ref[...], ckvn)
    kr = t[o_kr:o_kr + MLA_ROPE_DIM]
    kr_ss = jnp.sum(kr * kr, axis=0, keepdims=True)
    gmk = gmk_ref[...]
    kzero = jnp.zeros((MLA_HEAD_PAD - MLA_QK_DIM, tm), F32)
    ones_rows = _ones_row_block(tm)
    parts = []
    for n in range(MLA_HEADS):
        kn = kv[n * 128:n * 128 + nd]
        vm_ref[n * V_ROWS:n * V_ROWS + MLA_V_DIM, :] = kv[n * 128 + nd:(n + 1) * 128].astype(BF16)
        vm_ref[n * V_ROWS + MLA_V_DIM:(n + 1) * V_ROWS, :] = ones_rows
        rs = lax.rsqrt((jnp.sum(kn * kn, axis=0, keepdims=True) + kr_ss) * (1.0 / MLA_QK_DIM) + EPS)
        krn = kr * rs * gmk[nd:]
        a, b = _rope_rows(krn[:rh], krn[rh:], cm, sm)
        parts += [kn * rs * gmk[:nd], a, b, kzero]
    km_ref[...] = jnp.concatenate(parts, axis=0).T.astype(BF16)

    gq = gq_ref[...]
    for n in range(GQA_HEADS):
        xh = t[o_q + n * hd:o_q + (n + 1) * hd]
        xh = xh * _rms_rows(xh, hd) * gq
        a, b = _rope_rows(xh[:hh], xh[hh:], cg, sg)
        qg_ref[n * hd:n * hd + hh, :] = (a * (GQA_SCALE * LOG2E)).astype(BF16)
        qg_ref[n * hd + hh:(n + 1) * hd, :] = (b * (GQA_SCALE * LOG2E)).astype(BF16)

    gate_cols(2)
    gk = gk_ref[...]
    zpad = jnp.zeros((V7X_LANES - hd, tm), F32)
    parts = []
    for n in range(GQA_KV_HEADS):
        xh = t[o_k + n * hd:o_k + (n + 1) * hd]
        xh = xh * _rms_rows(xh, hd) * gk
        a, b = _rope_rows(xh[:hh], xh[hh:], cg, sg)
        parts += [a, b, zpad]
    kg_ref[...] = jnp.concatenate(parts, axis=0).T.astype(BF16)
    for n in range(GQA_KV_HEADS):
        vg_ref[n * V_ROWS:n * V_ROWS + hd, :] = t[o_v + n * hd:o_v + (n + 1) * hd].astype(BF16)
        vg_ref[n * V_ROWS + hd:(n + 1) * V_ROWS, :] = ones_rows


def _proj(xs, modsel, g, wts, tabs, lat_tiles):
    bsz, s, d = xs.shape
    tm = ROW_TILE
    nt = s // tm
    ncs = KV_CHUNK // tm
    nb = PROJ_BATCH if bsz % PROJ_BATCH == 0 else 1
    (wu, wg, wt, wqb, wkvb, gq, gk, gqa, gkva, gmq, gmk) = wts
    cg, sg, cm, sm = tabs
    row = lambda b, i: (b, i, 0)
    col = lambda b, i: (b, 0, i)
    tab = lambda b, i: (0, i)
    in_specs = [
        pl.BlockSpec((nb, tm, d), row),
        pl.BlockSpec((nb, None, 1, 6 * d), lambda b, i: (b, jnp.where(i >= lat_tiles, 1, 0), 0, 0)),
        _const_spec(g.shape), _const_spec(wu.shape), _const_spec(wg.shape), _const_spec(wt.shape),
        _const_spec(wqb.shape), _const_spec(wkvb.shape),
        _const_spec(gq.shape), _const_spec(gk.shape), _const_spec(gqa.shape), _const_spec(gkva.shape),
        _const_spec(gmq.shape), _const_spec(gmk.shape),
        pl.BlockSpec((cg.shape[0], tm), tab), pl.BlockSpec((sg.shape[0], tm), tab),
        pl.BlockSpec((cm.shape[0], tm), tab), pl.BlockSpec((sm.shape[0], tm), tab),
    ]
    out_shape = [
        jax.ShapeDtypeStruct((bsz, s, 3 * HY_WIDTH), BF16),
        jax.ShapeDtypeStruct((bsz, s, 3 * d), BF16),
        jax.ShapeDtypeStruct((bsz, GQA_HEADS * GQA_HEAD_DIM, s), BF16),
        jax.ShapeDtypeStruct((bsz, s, GQA_KV_HEADS * V7X_LANES), BF16),
        jax.ShapeDtypeStruct((bsz, GQA_KV_HEADS * V_ROWS, s), BF16),
        jax.ShapeDtypeStruct((bsz, MLA_HEADS * MLA_HEAD_PAD, s), BF16),
        jax.ShapeDtypeStruct((bsz, s, MLA_HEADS * MLA_HEAD_PAD), BF16),
        jax.ShapeDtypeStruct((bsz, s // KV_CHUNK, MLA_HEADS * V_ROWS, KV_CHUNK), BF16),
    ]
    out_specs = [
        pl.BlockSpec((nb, tm, 3 * HY_WIDTH), row),
        pl.BlockSpec((nb, tm, 3 * d), row),
        pl.BlockSpec((nb, GQA_HEADS * GQA_HEAD_DIM, tm), col),
        pl.BlockSpec((nb, tm, GQA_KV_HEADS * V7X_LANES), row),
        pl.BlockSpec((nb, GQA_KV_HEADS * V_ROWS, tm), col),
        pl.BlockSpec((nb, MLA_HEADS * MLA_HEAD_PAD, tm), col),
        pl.BlockSpec((nb, tm, MLA_HEADS * MLA_HEAD_PAD), row),
        pl.BlockSpec((nb, None, MLA_HEADS * V_ROWS, tm), lambda b, i: (b, i // ncs, 0, i % ncs)),
    ]
    return pl.pallas_call(
        _proj_kernel,
        grid=(bsz // nb, nt),
        in_specs=in_specs,
        out_specs=out_specs,
        out_shape=out_shape,
        compiler_params=_cparams(("parallel", "arbitrary"), 56),
        name="in_proj",
    )(xs, modsel, g, wu, wg, wt, wqb, wkvb, gq, gk, gqa, gkva, gmq, gmk, cg, sg, cm, sm)


def _short_conv_kernel(u_ref, up_ref, un_ref, w_ref, x0_ref, z_ref, *, n_tiles):
    i = pl.program_id(1)
    u = u_ref[...].astype(F32)
    tm = u.shape[0]
    prev = jnp.where(i > 0, up_ref[7:8, :].astype(F32), 0.0)
    nxt = jnp.where(i < n_tiles - 1, un_ref[0:1, :].astype(F32), 0.0)
    ridx = lax.broadcasted_iota(jnp.int32, u.shape, 0)
    up = jnp.where(ridx == 0, prev, pltpu.roll(u, 1, axis=0))
    dn = jnp.where(ridx == tm - 1, nxt, pltpu.roll(u, tm - 1, axis=0))
    uc = up * w_ref[0:1, :] + u * w_ref[1:2, :] + dn * w_ref[2:3, :]
    c = HY_WIDTH
    x0_ref[...] = uc[:, :c].astype(BF16)
    z_ref[...] = (uc[:, c:2 * c] * uc[:, 2 * c:]).astype(BF16)


def _short_conv(u, short_w, row0, rows):
    bsz, s, c3 = u.shape
    tm = math.gcd(math.gcd(rows, row0), CONV_TILE) if row0 else math.gcd(rows, CONV_TILE)
    nt = rows // tm
    t0 = row0 // tm
    r8 = tm // 8
    last8 = s // 8 - 1
    return pl.pallas_call(
        functools.partial(_short_conv_kernel, n_tiles=nt),
        grid=(bsz, nt),
        in_specs=[
            pl.BlockSpec((None, tm, c3), lambda b, i: (b, t0 + i, 0)),
            pl.BlockSpec((None, 8, c3), lambda b, i: (b, jnp.maximum((t0 + i) * r8 - 1, 0), 0)),
            pl.BlockSpec((None, 8, c3), lambda b, i: (b, jnp.minimum((t0 + i + 1) * r8, last8), 0)),
            _const_spec(short_w.shape),
        ],
        out_specs=[pl.BlockSpec((None, tm, HY_WIDTH), lambda b, i: (b, i, 0))] * 2,
        out_shape=[jax.ShapeDtypeStruct((bsz, rows, HY_WIDTH), BF16)] * 2,
        compiler_params=_cparams(("parallel", "arbitrary"), 32),
        name="hyena_short_conv",
    )(u, u, u, short_w)


def _filter_kernel(zf_ref, w1_ref, b1_ref, w2_ref, b2_ref, fr_ref, w3_ref, dl_ref, h_ref, ss_ref):
    i = pl.program_id(0)
    zf = zf_ref[...]
    tl = zf.shape[0]
    h = jnp.sin(fr_ref[0:1, :] * (_dot_hi(zf, w1_ref[...]) + b1_ref[...]))
    h = jnp.sin(fr_ref[1:2, :] * (_dot_hi(h, w2_ref[...]) + b2_ref[...]))
    h = _dot_hi(h, w3_ref[...])
    decay = jnp.exp(-zf[:, 0:1] * dl_ref[...])
    c = HY_WIDTH
    hf = h[:, :c] * decay
    ridx = lax.broadcasted_iota(jnp.int32, (tl, c), 0) + i * tl
    hb = jnp.where(ridx == 0, 0.0, h[:, c:] * decay)
    h_ref[0] = hf.astype(BF16)
    h_ref[1] = hb.astype(BF16)
    ss = jnp.sum(hf * hf + hb * hb, axis=0, keepdims=True)

    @pl.when(i == 0)
    def _():
        ss_ref[...] = ss

    @pl.when(i > 0)
    def _():
        ss_ref[...] += ss


def _filter(zfeat, fw, deltas):
    length = zfeat.shape[0]
    tl = min(length, 1024)
    w1, b1, w2, b2, fr, w3 = fw
    return pl.pallas_call(
        _filter_kernel,
        grid=(length // tl,),
        in_specs=[pl.BlockSpec((tl, zfeat.shape[1]), lambda i: (i, 0))]
        + [_const_spec(a.shape) for a in (w1, b1, w2, b2, fr, w3, deltas)],
        out_specs=[pl.BlockSpec((2, tl, HY_WIDTH), lambda i: (0, i, 0)),
                   pl.BlockSpec((1, HY_WIDTH), lambda i: (0, 0))],
        out_shape=[jax.ShapeDtypeStruct((2, length, HY_WIDTH), BF16),
                   jax.ShapeDtypeStruct((1, HY_WIDTH), F32)],
        compiler_params=_cparams(("arbitrary",), 40),
        name="hyena_filter",
    )(zfeat, w1, b1, w2, b2, fr, w3, deltas)


def _fft_a_kernel(g_ref, x_ref, o_ref, *, nb, cw):
    for j in range(nb):
        sl = slice(j * cw, (j + 1) * cw)
        x = jnp.concatenate([x_ref[p, :, sl] for p in range(x_ref.shape[0])], axis=0)
        o_ref[:, sl] = _dot(g_ref[j], x).astype(o_ref.dtype)


def _fft_a(x3d, gmat):
    bx, parts, k1, w = x3d.shape
    n2, two_n1, _ = gmat.shape
    cw = w // n2
    nb = FFT_NB
    return pl.pallas_call(
        functools.partial(_fft_a_kernel, nb=nb, cw=cw),
        grid=(bx, n2 // nb),
        in_specs=[pl.BlockSpec((nb, two_n1, parts * k1), lambda b, j: (j, 0, 0)),
                  pl.BlockSpec((None, parts, k1, nb * cw), lambda b, j: (b, 0, 0, j))],
        out_specs=pl.BlockSpec((None, two_n1, nb * cw), lambda b, j: (b, 0, j)),
        out_shape=jax.ShapeDtypeStruct((bx, two_n1, w), BF16),
        compiler_params=_cparams(("parallel", "arbitrary"), 40),
        name="fft_stage_a",
    )(gmat, x3d)


def _fft_filter_b_kernel(fb_ref, a_ref, ss_ref, kf_ref, *, inv_n):
    n = a_ref.shape[3]
    rs = lax.rsqrt(ss_ref[...] + EPS) * inv_n
    for r in range(a_ref.shape[2]):
        xf = _dot(fb_ref[...], jnp.concatenate([a_ref[0, 0, r], a_ref[0, 1, r]], axis=0))
        xb = _dot(fb_ref[...], jnp.concatenate([a_ref[1, 0, r], a_ref[1, 1, r]], axis=0))
        kf_ref[0, r] = (xf[:n] + xb[:n]) * rs
        kf_ref[1, r] = (xf[n:] - xb[n:]) * rs


def _fft_filter_b(a5, fb, ssq, inv_n):
    _, _, n1, n2, c = a5.shape
    kb = min(FFT_KB, n1)
    return pl.pallas_call(
        functools.partial(_fft_filter_b_kernel, inv_n=inv_n),
        grid=(n1 // kb,),
        in_specs=[_const_spec(fb.shape),
                  pl.BlockSpec((2, 2, kb, n2, c), lambda k: (0, 0, k, 0, 0)),
                  _const_spec(ssq.shape)],
        out_specs=pl.BlockSpec((2, kb, n2, c), lambda k: (0, k, 0, 0)),
        out_shape=jax.ShapeDtypeStruct((2, n1, n2, c), F32),
        compiler_params=_cparams(("arbitrary",), 32),
        name="fft_filter_stage_b",
    )(fb, a5, ssq)


def _fft_b_kernel(fb_ref, fbi_ref, a_ref, kf_ref, o_ref):
    kb, n, cw = a_ref.shape[1:]
    a_all = jnp.concatenate([jnp.concatenate([a_ref[0, r], a_ref[1, r]], axis=0) for r in range(kb)], axis=1)
    x = _dot(fb_ref[...], a_all)
    ys = []
    for r in range(kb):
        xr, xi = x[:n, r * cw:(r + 1) * cw], x[n:, r * cw:(r + 1) * cw]
        kr, ki = kf_ref[0, r], kf_ref[1, r]
        ys.append(jnp.concatenate([xr * kr - xi * ki, xr * ki + xi * kr], axis=0).astype(BF16))
    c = _dot(fbi_ref[...], jnp.concatenate(ys, axis=1))
    for r in range(kb):
        o_ref[0, r] = c[:n, r * cw:(r + 1) * cw].astype(o_ref.dtype)
        o_ref[1, r] = c[n:, r * cw:(r + 1) * cw].astype(o_ref.dtype)


def _fft_b(a5, kf, fb, fbi):
    bsz, _, n1, n2, c = a5.shape
    kb = min(FFT_KB, n1)
    return pl.pallas_call(
        _fft_b_kernel,
        grid=(n1 // kb, bsz),
        in_specs=[_const_spec(fb.shape), _const_spec(fbi.shape),
                  pl.BlockSpec((None, 2, kb, n2, c), lambda k, b: (b, 0, k, 0, 0)),
                  pl.BlockSpec((2, kb, n2, c), lambda k, b: (0, k, 0, 0))],
        out_specs=pl.BlockSpec((None, 2, kb, n2, c), lambda k, b: (b, 0, k, 0, 0)),
        out_shape=jax.ShapeDtypeStruct(a5.shape, BF16),
        compiler_params=_cparams(("arbitrary", "arbitrary"), 32),
        name="fft_stage_b",
    )(fb, fbi, a5, kf)


def _fft_c_kernel(h_ref, c_ref, o_ref, *, nb, cw):
    parts, k1 = o_ref.shape[0], o_ref.shape[1]
    for j in range(nb):
        sl = slice(j * cw, (j + 1) * cw)
        y = _dot(h_ref[j], c_ref[:, sl])
        for p in range(parts):
            o_ref[p, :, sl] = y[p * k1:(p + 1) * k1].astype(o_ref.dtype)


def _fft_c(c2d, hmat, parts):
    bx, two_n1, w = c2d.shape
    n2, rows, _ = hmat.shape
    k1 = rows // parts
    cw = w // n2
    nb = FFT_NB
    return pl.pallas_call(
        functools.partial(_fft_c_kernel, nb=nb, cw=cw),
        grid=(bx, n2 // nb),
        in_specs=[pl.BlockSpec((nb, rows, two_n1), lambda b, j: (j, 0, 0)),
                  pl.BlockSpec((None, two_n1, nb * cw), lambda b, j: (b, 0, j))],
        out_specs=pl.BlockSpec((None, parts, k1, nb * cw), lambda b, j: (b, 0, 0, j)),
        out_shape=jax.ShapeDtypeStruct((bx, parts, k1, w), BF16),
        compiler_params=_cparams(("parallel", "arbitrary"), 40),
        name="fft_stage_c",
    )(hmat, c2d)


def _ctx_conv_kernel(fc_ref, fci_ref, h_ref, ss_ref, z_ref, o_ref):
    n = fc_ref.shape[0] // 2
    fc = fc_ref[...]
    kf = _dot(fc, h_ref[0])
    kb = _dot(fc, h_ref[1])
    rs = lax.rsqrt(ss_ref[...] + EPS)
    kr = (kf[:n] + kb[:n]) * rs
    ki = (kf[n:] - kb[n:]) * rs
    x = _dot(fc, z_ref[...])
    xr, xi = x[:n], x[n:]
    y = jnp.concatenate([xr * kr - xi * ki, xr * ki + xi * kr], axis=0).astype(BF16)
    o_ref[...] = _dot(fci_ref[...], y).astype(o_ref.dtype)


def _ctx_conv(fc, fci, hfb, ssq, z):
    bsz, lc, c = z.shape
    blk = pl.BlockSpec((None, lc, c), lambda b: (b, 0, 0))
    return pl.pallas_call(
        _ctx_conv_kernel,
        grid=(bsz,),
        in_specs=[_const_spec(fc.shape), _const_spec(fci.shape), _const_spec(hfb.shape),
                  _const_spec(ssq.shape), blk],
        out_specs=blk,
        out_shape=jax.ShapeDtypeStruct((bsz, lc, c), BF16),
        compiler_params=_cparams(("arbitrary",), 32),
        name="hyena_ctx_conv",
    )(fc, fci, hfb, ssq, z)


@functools.lru_cache(maxsize=None)
def _fft_tables(length):
    n = 2 * length
    n2 = FFT_N2
    n1 = n // n2
    k1 = length // n2
    kk = np.arange(n1)[:, None]
    g = np.empty((n2, 2 * n1, k1), np.float64)
    g2 = np.empty((n2, 2 * n1, 2 * k1), np.float64)
    h2 = np.empty((n2, 2 * k1, 2 * n1), np.float64)
    nn = np.arange(k1)[None, :]
    for j in range(n2):
        ang = 2.0 * np.pi * (((n2 * nn * kk) % n) + (j * kk) % n) / n
        c_, s_ = np.cos(ang), np.sin(ang)
        g[j] = np.concatenate([c_, -s_], axis=0)
        g2[j] = np.block([[c_, s_], [-s_, c_]])
        h2[j] = np.block([[c_.T, -s_.T], [s_.T, c_.T]])
    a = np.arange(n2)
    ph = 2.0 * np.pi * ((a[:, None] * a[None, :]) % n2) / n2
    c, s = np.cos(ph), np.sin(ph)
    fb = np.block([[c, s], [-s, c]])
    fbi = np.block([[c, -s], [s, c]])
    return (jnp.asarray(g, BF16), jnp.asarray(g2, BF16), jnp.asarray(h2, BF16), jnp.asarray(fb, BF16),
            jnp.asarray(fbi, BF16), n1, k1)


@functools.lru_cache(maxsize=None)
def _dft_tables(length):
    n = 2 * length
    k = np.arange(n)[:, None]
    t = np.arange(length)[None, :]
    ang = 2.0 * np.pi * ((k * t) % n) / n
    fc = np.concatenate([np.cos(ang), -np.sin(ang)], axis=0)
    fci = np.concatenate([np.cos(ang).T, -np.sin(ang).T], axis=1) / n
    return jnp.asarray(fc, BF16), jnp.asarray(fci, BF16)


@functools.lru_cache(maxsize=None)
def _filter_features(length):
    t = np.linspace(0.0, 1.0, length, dtype=np.float32)[:, None]
    w = (2.0 * math.pi * np.arange(length, dtype=np.float32)[:, None] / length).astype(np.float32)
    f = np.linspace(1e-4, HY_BANDS - 1, HY_BANDS, dtype=np.float32)[None, :]
    z = np.concatenate([t, np.cos(f * w), -np.sin(f * w)], axis=-1).astype(np.float32)
    zp = np.zeros((length, V7X_LANES), np.float32)
    zp[:, :HY_EMB_DIM] = z
    return jnp.asarray(zp)


def _hyena_deltas():
    max_decay = math.log(HY_DECAY_TARGET) / HY_FAST_DECAY
    min_decay = math.log(HY_DECAY_TARGET) / HY_SLOW_DECAY
    return jnp.abs(jnp.linspace(min_decay, max_decay, HY_WIDTH, dtype=F32))[None, :]


def _hyena(u, short_w, fw, lat, lc):
    bsz = u.shape[0]
    c = HY_WIDTH
    deltas = _hyena_deltas()
    x0, z = _short_conv(u, short_w, 0, lat)
    gmat, gmat2, hmat2, fb, fbi, n1, k1 = _fft_tables(lat)
    n2 = FFT_N2
    hfb, ssq = _filter(_filter_features(lat), fw, deltas)
    fa = _fft_a(hfb.reshape(2, 1, k1, n2 * c), gmat)
    kf = _fft_filter_b(fa.reshape(2, 2, n1, n2, c), fb, ssq, 1.0 / (2 * lat))
    npair = (bsz + 1) // 2
    zp = z if bsz % 2 == 0 else jnp.concatenate([z, jnp.zeros_like(z[:1])], axis=0)
    za = _fft_a(zp.reshape(npair, 2, k1, n2 * c), gmat2)
    zc = _fft_b(za.reshape(npair, 2, n1, n2, c), kf, fb, fbi)
    conv = _fft_c(zc.reshape(npair, 2 * n1, n2 * c), hmat2, 2).reshape(2 * npair, lat, c)[:bsz]
    x0_c, z_c = _short_conv(u, short_w, lat, lc)
    hfb_c, ssq_c = _filter(_filter_features(lc), fw, deltas)
    fc, fci = _dft_tables(lc)
    conv_c = _ctx_conv(fc, fci, hfb_c, ssq_c, z_c)
    return (conv, x0, z), (conv_c, x0_c, z_c)


def _gqa_kernel(q_ref, k_ref, v_ref, sink_ref, o_ref, *bufs, lat, lc, tq):
    sub = q_ref.shape[1] // tq
    for t in range(sub):
        lanes = slice(t * tq, (t + 1) * tq)
        _gqa_tile(pl.program_id(1) * sub + t, q_ref.at[:, lanes], k_ref, v_ref, sink_ref, o_ref.at[:, lanes], bufs,
                  lat, lc)


def _gqa_tile(i, q_ref, k_ref, v_ref, sink_ref, o_ref, bufs, lat, lc):
    tq = q_ref.shape[1]
    s_len = k_ref.shape[0]
    hd = GQA_HEAD_DIM
    w = WINDOW
    nwin = tq + 2 * w
    start = pl.multiple_of(jnp.clip(i * tq - w, 0, s_len - nwin), w)
    k_all = jnp.concatenate([k_ref[pl.ds(start, nwin), :], k_ref[lat:lat + lc, :]], axis=0)
    v_all = jnp.concatenate([v_ref[:, pl.ds(start, nwin)], v_ref[:, lat:lat + lc]], axis=1)
    q_pos = i * tq + lax.broadcasted_iota(jnp.int32, (1, tq), 1)
    k_pos = start + lax.broadcasted_iota(jnp.int32, (nwin, 1), 0)
    bias = (jnp.where(jnp.abs(k_pos - q_pos) <= w, 0.0, NEG_BIG)
            + jnp.where(k_pos < lat, 0.0, NEG_BIG)
            + jnp.where(q_pos < lat, 0.0, NEG_BIG))
    bias = jnp.concatenate([bias, jnp.zeros((lc, tq), F32)], axis=0)
    bias2 = jnp.concatenate([bias, bias], axis=1)
    zq = jnp.zeros((V7X_LANES - hd, 2 * tq), BF16)

    def score(pair, dst):
        g = pair // (GQA_GROUP // 2)
        r = 2 * pair * hd
        q2 = jnp.concatenate([q_ref[r:r + hd, :], q_ref[r + hd:r + 2 * hd, :]], axis=1)
        s = _dot(k_all[:, g * V7X_LANES:(g + 1) * V7X_LANES], jnp.concatenate([q2, zq], axis=0)) + bias2
        dst[...] = s
        return jnp.max(s, axis=0, keepdims=True)

    def update(pair, s_sc, s_max):
        g = pair // (GQA_GROUP // 2)
        r = 2 * pair * hd
        sink = jnp.concatenate([sink_ref[2 * pair:2 * pair + 1, :], sink_ref[2 * pair + 1:2 * pair + 2, :]],
                               axis=1) * LOG2E
        m = jnp.maximum(s_max, sink)
        p = jnp.exp2(s_sc[...] - m).astype(BF16)
        pv = _dot(v_all[g * V_ROWS:(g + 1) * V_ROWS, :], p)
        o = pv[:hd] / (pv[hd:hd + 1] + jnp.exp2(sink - m))
        o_ref[r:r + hd, :] = o[:, :tq].astype(o_ref.dtype)
        o_ref[r + hd:r + 2 * hd, :] = o[:, tq:].astype(o_ref.dtype)

    npair = GQA_HEADS // 2
    mx = [score(j, bufs[j]) for j in range(npair)]
    for j in range(npair):
        update(j, bufs[j], mx[j])


def _gqa(qg, kg, vg, sink_rows, lat, lc):
    bsz, nq, s = qg.shape
    tq = GQA_TQ
    sub = GQA_SUBTILES if (s // tq) % GQA_SUBTILES == 0 else 1
    wq = tq * sub
    return pl.pallas_call(
        functools.partial(_gqa_kernel, lat=lat, lc=lc, tq=tq),
        grid=(bsz, s // wq),
        in_specs=[
            pl.BlockSpec((None, nq, wq), lambda b, i: (b, 0, i)),
            pl.BlockSpec((None, s, kg.shape[2]), lambda b, i: (b, 0, 0)),
            pl.BlockSpec((None, vg.shape[1], s), lambda b, i: (b, 0, 0)),
            _const_spec(sink_rows.shape),
        ],
        out_specs=pl.BlockSpec((None, nq, wq), lambda b, i: (b, 0, i)),
        out_shape=jax.ShapeDtypeStruct((bsz, nq, s), BF16),
        scratch_shapes=[pltpu.VMEM((tq + 2 * WINDOW + lc, 2 * tq), F32)] * (GQA_HEADS // 2),
        compiler_params=_cparams(("parallel", "arbitrary"), 40),
        name="gqa_window_attn",
    )(qg, kg, vg, sink_rows)


def _mla_update(s_ref, s_max, vt, m, acc):
    m_new = jnp.maximum(m, s_max)
    alpha = jnp.exp2(m - m_new)
    p = jnp.exp2(s_ref[...] - m_new).astype(BF16)
    acc = alpha * acc + _dot(vt, p)
    return m_new, acc


def _mla_kernel(q_ref, k_ref, v_ref, o_ref, *bufs, nc, ctx_only, tq):
    for t in range(q_ref.shape[1] // tq):
        lanes = slice(t * tq, (t + 1) * tq)
        _mla_tile(q_ref[:, lanes], k_ref, v_ref, o_ref.at[:, lanes], bufs, nc, ctx_only)


def _mla_tile(q, k_ref, v_ref, o_ref, bufs, nc, ctx_only):
    tq = q.shape[1]
    ck = KV_CHUNK
    nbuf = len(bufs)
    m = jnp.full((1, tq), NEG_BIG, F32)
    acc = jnp.zeros((V_ROWS, tq), F32)

    def score(k, dst):
        s = _dot(k, q)
        dst[...] = s
        return jnp.max(s, axis=0, keepdims=True)

    def score_chunk(j, dst):
        return score(k_ref[pl.ds(pl.multiple_of(j * ck, ck), ck), :], dst)

    if ctx_only:
        s_x = bufs[0].at[0:KV_SUB, :]
        mx = score(k_ref[k_ref.shape[0] - KV_SUB:, :], s_x)
        m, acc = _mla_update(s_x, mx, v_ref[v_ref.shape[0] - 1][:, ck - KV_SUB:], m, acc)
    else:
        ahead = nbuf - 1
        mx = [None] * nbuf
        for j in range(min(ahead, nc)):
            mx[j] = score_chunk(j, bufs[j])

        def step(j, slot, m, acc, mx, last):
            if not last:
                nxt = (slot + ahead) % nbuf
                mx[nxt] = score_chunk(j + ahead, bufs[nxt])
            return _mla_update(bufs[slot], mx[slot], v_ref[j], m, acc)

        per_body = MLA_UNROLL * nbuf

        def body(g, c):
            m, acc, mx = c[0], c[1], list(c[2:])
            for u in range(per_body):
                m, acc = step(g * per_body + u, u % nbuf, m, acc, mx, False)
            return (m, acc, *mx)

        n_body = max(nc - ahead, 0) // per_body
        if n_body > 0:
            out = lax.fori_loop(0, n_body, body, (m, acc, *[m if x is None else x for x in mx]))
            m, acc, mx = out[0], out[1], list(out[2:])
        for j in range(n_body * per_body, nc):
            m, acc = step(j, j % nbuf, m, acc, mx, j + ahead >= nc)
    o_ref[...] = (acc[:MLA_V_DIM] / acc[MLA_V_DIM:MLA_V_DIM + 1]).astype(o_ref.dtype)


def _mla_call(qm, km, vm, tq, sub, q_tile0, n_q, ctx_only):
    bsz, _, s = qm.shape
    nc = vm.shape[1]
    wq = tq * sub
    if ctx_only:
        k_spec = pl.BlockSpec((None, KV_SUB, MLA_HEAD_PAD), lambda b, h, i: (b, s // KV_SUB - 1, h))
        v_spec = pl.BlockSpec((None, 1, V_ROWS, KV_CHUNK), lambda b, h, i: (b, nc - 1, h, 0))
    else:
        k_spec = pl.BlockSpec((None, s, MLA_HEAD_PAD), lambda b, h, i: (b, 0, h))
        v_spec = pl.BlockSpec((None, nc, V_ROWS, KV_CHUNK), lambda b, h, i: (b, 0, h, 0))
    return pl.pallas_call(
        functools.partial(_mla_kernel, nc=nc, ctx_only=ctx_only, tq=tq),
        grid=(bsz, MLA_HEADS, n_q),
        in_specs=[pl.BlockSpec((None, MLA_HEAD_PAD, wq), lambda b, h, i: (b, h, q_tile0 + i)), k_spec, v_spec],
        out_specs=pl.BlockSpec((None, MLA_V_DIM, wq), lambda b, h, i: (b, h, i)),
        out_shape=jax.ShapeDtypeStruct((bsz, MLA_HEADS * MLA_V_DIM, n_q * wq), BF16),
        scratch_shapes=[pltpu.VMEM((KV_CHUNK, tq), F32)] * MLA_NBUF,
        compiler_params=_cparams(("parallel", "arbitrary", "arbitrary"), 40),
        name="mla_attn_ctx" if ctx_only else "mla_attn",
    )(qm, km, vm)


def _mla(qm, km, vm, lat, lc):
    tq = min(MLA_TQ, lat)
    sub = MLA_SUBTILES if lat % (tq * MLA_SUBTILES) == 0 else 1
    assert lc == KV_SUB and lat % lc == 0 and lat % tq == 0
    return (_mla_call(qm, km, vm, tq, sub, 0, lat // (tq * sub), False),
            _mla_call(qm, km, vm, lc, 1, lat // lc, 1, True))


def _merge_mlp_kernel(x_ref, mod_ref, hl_refs, hc_refs, sk_ref, yg_ref, yml_ref, ymc_ref, gt_ref, wb_ref, wo_ref,
                      g_ref, w1_ref, w2_ref, o_ref, *, lat_tiles):
    for n in range(x_ref.shape[0]):
        _merge_mlp_one(x_ref.at[n], mod_ref.at[n], [r.at[n] for r in hl_refs], [r.at[n] for r in hc_refs], sk_ref,
                       yg_ref.at[n], yml_ref.at[n], ymc_ref.at[n], gt_ref.at[n], wb_ref, wo_ref, g_ref, w1_ref,
                       w2_ref, o_ref.at[n], lat_tiles)


def _merge_mlp_one(x_ref, mod_ref, hl_refs, hc_refs, sk_ref, yg_ref, yml_ref, ymc_ref, gt_ref, wb_ref, wo_ref,
                   g_ref, w1_ref, w2_ref, o_ref, lat_tiles):
    d = D_MODEL
    is_lat = pl.program_id(1) < lat_tiles
    conv, x0, z = [jnp.where(is_lat, a[...], b[...]).astype(F32) for a, b in zip(hl_refs, hc_refs)]
    yh = (x0 * (conv + z * sk_ref[...])).astype(BF16)
    yg = yg_ref[...].astype(F32).T.astype(BF16)
    ym = jnp.where(is_lat, yml_ref[...], ymc_ref[...]).astype(F32).T.astype(BF16)
    merged = (gt_ref[:, 0:d].astype(F32) * _dot(yh, wb_ref[0])
              + gt_ref[:, d:2 * d].astype(F32) * _dot(yg, wb_ref[1])
              + gt_ref[:, 2 * d:].astype(F32) * _dot(ym, wb_ref[2]))
    res = _dot(merged.astype(BF16), wo_ref[...])
    x_mix = x_ref[...] + mod_ref[:, 2 * d:3 * d] * res
    o_ref[...] = _mlp_rows(x_mix, mod_ref, g_ref, w1_ref, w2_ref)


def _merge_mlp(xs, modsel, hy_lat, hy_ctx, skip, yg, ym_lat, ym_ctx, gates, wb, wo, g, w1, w2, lat_tiles, n_tiles):
    bsz, _, d = xs.shape
    tm = ROW_TILE
    c = HY_WIDTH
    nb = MERGE_BATCH if bsz % MERGE_BATCH == 0 else 1
    row = lambda b, i: (b, i, 0)
    col = lambda b, i: (b, 0, i)
    lat_row = pl.BlockSpec((nb, tm, c), lambda b, i: (b, jnp.minimum(i, lat_tiles - 1), 0))
    ctx_row = pl.BlockSpec((nb, tm, c), lambda b, i: (b, jnp.maximum(i - lat_tiles, 0), 0))
    return pl.pallas_call(
        functools.partial(_merge_mlp_kernel, lat_tiles=lat_tiles),
        grid=(bsz // nb, n_tiles),
        in_specs=[
            pl.BlockSpec((nb, tm, d), row),
            pl.BlockSpec((nb, None, 1, 6 * d), lambda b, i: (b, jnp.where(i >= lat_tiles, 1, 0), 0, 0)),
            [lat_row] * 3, [ctx_row] * 3, _const_spec(skip.shape),
            pl.BlockSpec((nb, c, tm), col),
            pl.BlockSpec((nb, c, tm), lambda b, i: (b, 0, jnp.minimum(i, lat_tiles - 1))),
            pl.BlockSpec((nb, c, tm), lambda b, i: (b, 0, jnp.maximum(i - lat_tiles, 0))),
            pl.BlockSpec((nb, tm, 3 * d), row),
            _const_spec(wb.shape), _const_spec(wo.shape),
            _const_spec(g.shape), _const_spec(w1.shape), _const_spec(w2.shape),
        ],
        out_specs=pl.BlockSpec((nb, tm, d), row),
        out_shape=jax.ShapeDtypeStruct((bsz, n_tiles * tm, d), F32),
        compiler_params=_cparams(("parallel", "arbitrary"), 56),
        name="merge_mlp",
    )(xs, modsel, list(hy_lat), list(hy_ctx), skip, yg, ym_lat, ym_ctx, gates, wb, wo, g, w1, w2)


def _mlp_rows(x, mod_ref, g_ref, w1_ref, w2_ref):
    d = D_MODEL
    xn = x * lax.rsqrt(jnp.mean(x * x, axis=-1, keepdims=True) + EPS) * g_ref[...]
    h = (xn * (1.0 + mod_ref[:, 4 * d:5 * d]) + mod_ref[:, 3 * d:4 * d]).astype(BF16)
    acc = jnp.zeros(x.shape, F32)
    for j in range(D_FF // FF_CHUNK):
        sl = slice(j * FF_CHUNK, (j + 1) * FF_CHUNK)
        a = jnp.maximum(_dot(h, w1_ref[:, sl]), 0.0)
        acc = acc + _dot((a * a).astype(BF16), w2_ref[sl, :])
    return x + mod_ref[:, 5 * d:] * acc


def _rope_tables_t(rows, dim, lc):
    n_freq = dim // 4
    inv = ROPE_BASE ** (-jnp.arange(n_freq, dtype=F32) / n_freq)
    r = jnp.repeat(jnp.arange(rows, dtype=F32), GRID_W)
    col = jnp.tile(jnp.arange(GRID_W, dtype=F32), rows)
    ang = jnp.concatenate([r[:, None] * inv, col[:, None] * inv], axis=-1)
    cos_t = jnp.concatenate([jnp.cos(ang).T, jnp.ones((dim // 2, lc), F32)], axis=1)
    sin_t = jnp.concatenate([jnp.sin(ang).T, jnp.zeros((dim // 2, lc), F32)], axis=1)
    return cos_t, sin_t


def _lane_bcast(v):
    return jnp.broadcast_to(v.astype(F32)[:, None], (v.shape[0], ROW_TILE))


def _layer_weights(l, w_in, gqa_q_norm, gqa_k_norm, mla_q_a_norm, mla_kv_a_norm, w_q_b, w_kv_b, mla_q_norm, mla_k_norm):
    w = w_in[l]
    o = np.cumsum([0, 3 * HY_WIDTH, GQA_HEADS * GQA_HEAD_DIM, GQA_KV_HEADS * GQA_HEAD_DIM,
                   GQA_KV_HEADS * GQA_HEAD_DIM, MLA_Q_RANK, MLA_KV_RANK, MLA_ROPE_DIM, 3 * D_MODEL])
    wu = w[:, o[0]:o[1]].astype(BF16)
    wt = w[:, o[1]:o[7]].T.astype(BF16)
    wg = w[:, o[7]:o[8]].astype(BF16)
    wqb = w_q_b[l].T.astype(BF16)
    wkvb = w_kv_b[l].T.astype(BF16)
    return (wu, wg, wt, wqb, wkvb,
            _lane_bcast(gqa_q_norm[l]), _lane_bcast(gqa_k_norm[l]),
            _lane_bcast(mla_q_a_norm[l]), _lane_bcast(mla_kv_a_norm[l]),
            _lane_bcast(mla_q_norm[l]), _lane_bcast(mla_k_norm[l]))


def kernel(x, c, ctx, c_ctx, w_mod, b_mod, norm_mix_g, norm_mlp_g, w_in, hy_short_w, hy_f1_w, hy_f1_b, hy_f2_w, hy_f2_b, hy_sin_freq, hy_f3_w, hy_skip, gqa_q_norm, gqa_k_norm, gqa_sink, mla_q_a_norm, mla_kv_a_norm, w_q_b, w_kv_b, mla_q_norm, mla_k_norm, w_branch, w_out, w_mlp1, w_mlp2):
    bsz, lat, d = x.shape
    lc = ctx.shape[1]
    depth = w_mod.shape[0]
    s = lat + lc
    assert d == D_MODEL and lat % GRID_W == 0 and lat % ROW_TILE == 0 and lc == ROW_TILE and s % KV_CHUNK == 0
    lat_tiles = lat // ROW_TILE

    pad = (-(bsz + 1)) % 8
    cond = jnp.concatenate([c, c_ctx[None, :], jnp.zeros((pad, d), F32)], axis=0)
    mods = _mod_all(cond, w_mod, b_mod)

    tabs = _rope_tables_t(lat // GRID_W, GQA_HEAD_DIM, lc) + _rope_tables_t(lat // GRID_W, MLA_ROPE_DIM, lc)
    xs = jnp.concatenate([x, ctx], axis=1)

    for l in range(depth):
        ml = mods[l]
        modsel = jnp.stack([ml[:bsz], jnp.broadcast_to(ml[bsz][None], (bsz, 6 * d))], axis=1)[:, :, None, :]
        wts = _layer_weights(l, w_in, gqa_q_norm, gqa_k_norm, mla_q_a_norm, mla_kv_a_norm, w_q_b, w_kv_b,
                             mla_q_norm, mla_k_norm)
        u, gates, qg, kg, vg, qm, km, vm = _proj(xs, modsel, norm_mix_g[l][None, :], wts, tabs, lat_tiles)

        f1w = jnp.zeros((V7X_LANES, HY_FILTER_WIDTH), F32).at[:HY_EMB_DIM].set(hy_f1_w[l])
        fw = (f1w, hy_f1_b[l][None, :], hy_f2_w[l], hy_f2_b[l][None, :], hy_sin_freq[l], hy_f3_w[l])
        hy_lat, hy_ctx = _hyena(u, hy_short_w[l], fw, lat, lc)

        sink_rows = jnp.broadcast_to(gqa_sink[l].astype(F32)[:, None], (GQA_HEADS, GQA_TQ))
        yg = _gqa(qg, kg, vg, sink_rows, lat, lc)
        ym_lat, ym_ctx = _mla(qm, km, vm, lat, lc)

        n_tiles = lat_tiles if l == depth - 1 else s // ROW_TILE
        xs = _merge_mlp(xs, modsel, hy_lat, hy_ctx, hy_skip[l][None, :], yg, ym_lat, ym_ctx, gates,
                        w_branch[l].astype(BF16), w_out[l].astype(BF16), norm_mlp_g[l][None, :],
                        w_mlp1[l].astype(BF16), w_mlp2[l].astype(BF16), lat_tiles, n_tiles)
    return xs
```
